```python
import jax, jax.numpy as jnp
from jax import lax
import numpy as np

D_MODEL = 1024
BATCH = 8
SEQ = 4096
DEPTH = 4

CONV_WIDTH = D_MODEL
CONV_K = 3
N_Q_HEADS = 16
N_KV_HEADS = 4
HEAD_DIM = 64
ATTN_WIDTH = N_Q_HEADS * HEAD_DIM
KV_WIDTH = N_KV_HEADS * HEAD_DIM
WINDOW = 128
BLOCK = 128
N_BRANCHES = 2
EPS = 1e-6
NEG_INF = -1e30

IN_SIZES = (CONV_WIDTH, CONV_WIDTH, CONV_WIDTH, CONV_WIDTH,
            ATTN_WIDTH, KV_WIDTH, KV_WIDTH, ATTN_WIDTH,
            N_BRANCHES * D_MODEL)
IN_COLS = sum(IN_SIZES)
SPLIT_POINTS = tuple(int(c) for c in np.cumsum(IN_SIZES)[:-1])

kernel_name = "hybrid_shortconv_swa_sink_gated_block"


def rms_norm(x, g):
    xf = x.astype(jnp.float32)
    y = xf * lax.rsqrt(jnp.mean(xf * xf, axis=-1, keepdims=True) + EPS)
    return (y * g.astype(jnp.float32)).astype(x.dtype)


def causal_depthwise_conv(u, w):
    s = u.shape[1]
    up = jnp.pad(u, ((0, 0), (CONV_K - 1, 0), (0, 0)))
    y = up[:, 0:s] * w[0]
    for k in range(1, CONV_K):
        y = y + up[:, k:k + s] * w[k]
    return y


def sliding_window_attention(q, k, v, sinks):
    b, s = q.shape[0], q.shape[1]
    nb = s // BLOCK
    g = N_Q_HEADS // N_KV_HEADS
    qb = q.reshape(b, nb, BLOCK, N_KV_HEADS, g, HEAD_DIM)

    def band(t):
        tb = t.reshape(b, nb, BLOCK, N_KV_HEADS, HEAD_DIM)
        prev = jnp.pad(tb[:, :-1], ((0, 0), (1, 0), (0, 0), (0, 0), (0, 0)))
        return jnp.concatenate([prev, tb], axis=2)

    kb, vb = band(k), band(v)
    scale = HEAD_DIM ** -0.5
    scores = jnp.einsum('bnqhgd,bnkhd->bnhgqk', qb.astype(jnp.float32),
                        kb.astype(jnp.float32)) * scale
    blk = jnp.arange(nb)[:, None, None]
    q_pos = blk * BLOCK + jnp.arange(BLOCK)[None, :, None]
    k_pos = (blk - 1) * BLOCK + jnp.arange(2 * BLOCK)[None, None, :]
    diff = q_pos - k_pos
    valid = (diff >= 0) & (diff < WINDOW) & (k_pos >= 0)
    scores = jnp.where(valid[None, :, None, None], scores, NEG_INF)
    sink = sinks.astype(jnp.float32).reshape(N_KV_HEADS, g)[None, None, :, :, None, None]
    m = jnp.maximum(jnp.max(scores, axis=-1, keepdims=True), sink)
    p = jnp.exp(scores - m)
    p = p / (jnp.sum(p, axis=-1, keepdims=True) + jnp.exp(sink - m))
    out = jnp.einsum('bnhgqk,bnkhd->bnqhgd', p.astype(v.dtype), vb)
    return out.reshape(b, s, ATTN_WIDTH)


def hybrid_layer(x, norm_g, w_in, conv_w, q_norm_g, k_norm_g, sinks,
                 w_conv_out, w_attn_out, gate_b, w_out):
    b, s, _ = x.shape
    h = rms_norm(x, norm_g)
    u = jnp.einsum('bsd,dc->bsc', h, w_in)
    v_c, b_c, c_c, z_c, q, k, v, z_a, gate_logits = jnp.split(u, SPLIT_POINTS, axis=-1)

    y_c = b_c * causal_depthwise_conv(c_c * v_c, conv_w)
    y_c = y_c * jax.nn.silu(z_c)
    y_a = jnp.einsum('bsc,cd->bsd', y_c, w_conv_out)

    q = rms_norm(q.reshape(b, s, N_Q_HEADS, HEAD_DIM), q_norm_g)
    k = rms_norm(k.reshape(b, s, N_KV_HEADS, HEAD_DIM), k_norm_g)
    v = v.reshape(b, s, N_KV_HEADS, HEAD_DIM)
    o = sliding_window_attention(q, k, v, sinks) * jax.nn.silu(z_a)
    y_b = jnp.einsum('bsc,cd->bsd', o, w_attn_out)

    gates = jax.nn.sigmoid(gate_logits + gate_b)
    g_a, g_b = jnp.split(gates, N_BRANCHES, axis=-1)
    merged = g_a * y_a + g_b * y_b
    return x + jnp.einsum('bsd,de->bse', merged, w_out)


def _fwd_setup_inputs(seed: int = 0) -> dict:
    key = jax.random.key(seed)
    ks = jax.random.split(key, 12)
    f32 = jnp.float32
    x = jax.random.normal(ks[0], (BATCH, SEQ, D_MODEL), f32)
    norm_g = 1.0 + 0.05 * jax.random.normal(ks[1], (DEPTH, D_MODEL), f32)
    w_in = jax.random.normal(ks[2], (DEPTH, D_MODEL, IN_COLS), f32) * D_MODEL ** -0.5
    conv_w = jax.random.normal(ks[3], (DEPTH, CONV_K, CONV_WIDTH), f32) * CONV_K ** -0.5
    q_norm_g = 1.0 + 0.05 * jax.random.normal(ks[4], (DEPTH, HEAD_DIM), f32)
    k_norm_g = 1.0 + 0.05 * jax.random.normal(ks[5], (DEPTH, HEAD_DIM), f32)
    sinks = 0.5 * jax.random.normal(ks[6], (DEPTH, N_Q_HEADS), f32)
    w_conv_out = jax.random.normal(ks[7], (DEPTH, CONV_WIDTH, D_MODEL), f32) * CONV_WIDTH ** -0.5
    w_attn_out = jax.random.normal(ks[8], (DEPTH, ATTN_WIDTH, D_MODEL), f32) * ATTN_WIDTH ** -0.5
    gate_b = 0.02 * jax.random.normal(ks[9], (DEPTH, N_BRANCHES * D_MODEL), f32)
    w_out = jax.random.normal(ks[10], (DEPTH, D_MODEL, D_MODEL), f32) * D_MODEL ** -0.5
    return {"x": x, "norm_g": norm_g, "w_in": w_in, "conv_w": conv_w,
            "q_norm_g": q_norm_g, "k_norm_g": k_norm_g, "sinks": sinks,
            "w_conv_out": w_conv_out, "w_attn_out": w_attn_out,
            "gate_b": gate_b, "w_out": w_out}


def _fwd_reference(x, norm_g, w_in, conv_w, q_norm_g, k_norm_g, sinks,
              w_conv_out, w_attn_out, gate_b, w_out):
    for l in range(DEPTH):
        x = hybrid_layer(x, norm_g[l], w_in[l], conv_w[l], q_norm_g[l], k_norm_g[l],
                         sinks[l], w_conv_out[l], w_attn_out[l], gate_b[l], w_out[l])
    return x


import jax as _jax
import jax.numpy as _jnp

TWIN_FORMAT = 'train_step'
FWD_PARAMS = ['x', 'norm_g', 'w_in', 'conv_w', 'q_norm_g', 'k_norm_g', 'sinks', 'w_conv_out', 'w_attn_out', 'gate_b', 'w_out']
TWIN_WEIGHTS = ['norm_g', 'w_in', 'conv_w', 'q_norm_g', 'k_norm_g', 'sinks', 'w_conv_out', 'w_attn_out', 'gate_b', 'w_out']
TWIN_DIFF_INPUT = 'x'
TWIN_INPUTS = ['x', 'norm_g', 'w_in', 'conv_w', 'q_norm_g', 'k_norm_g', 'sinks', 'w_conv_out', 'w_attn_out', 'gate_b', 'w_out', 'loss_target', 'm_norm_g', 'm_w_in', 'm_conv_w', 'm_q_norm_g', 'm_k_norm_g', 'm_sinks', 'm_w_conv_out', 'm_w_attn_out', 'm_gate_b', 'm_w_out', 'v_norm_g', 'v_w_in', 'v_conv_w', 'v_q_norm_g', 'v_k_norm_g', 'v_sinks', 'v_w_conv_out', 'v_w_attn_out', 'v_gate_b', 'v_w_out']
TWIN_OUTPUTS = ['loss', 'grad_x', 'grad_norm_g', 'grad_w_in', 'grad_conv_w', 'grad_q_norm_g', 'grad_k_norm_g', 'grad_sinks', 'grad_w_conv_out', 'grad_w_attn_out', 'grad_gate_b', 'grad_w_out', 'delta_norm_g', 'delta_w_in', 'delta_conv_w', 'delta_q_norm_g', 'delta_k_norm_g', 'delta_sinks', 'delta_w_conv_out', 'delta_w_attn_out', 'delta_gate_b', 'delta_w_out', 'new_m_norm_g', 'new_m_w_in', 'new_m_conv_w', 'new_m_q_norm_g', 'new_m_k_norm_g', 'new_m_sinks', 'new_m_w_conv_out', 'new_m_w_attn_out', 'new_m_gate_b', 'new_m_w_out', 'new_v_norm_g', 'new_v_w_in', 'new_v_conv_w', 'new_v_q_norm_g', 'new_v_k_norm_g', 'new_v_sinks', 'new_v_w_conv_out', 'new_v_w_attn_out', 'new_v_gate_b', 'new_v_w_out']
TWIN_LEAF_KINDS = {'loss': 'loss', 'grad_x': 'grad_x', 'grad_norm_g': 'grad_w', 'grad_w_in': 'grad_w', 'grad_conv_w': 'grad_w', 'grad_q_norm_g': 'grad_w', 'grad_k_norm_g': 'grad_w', 'grad_sinks': 'grad_w', 'grad_w_conv_out': 'grad_w', 'grad_w_attn_out': 'grad_w', 'grad_gate_b': 'grad_w', 'grad_w_out': 'grad_w', 'delta_norm_g': 'delta_w', 'delta_w_in': 'delta_w', 'delta_conv_w': 'delta_w', 'delta_q_norm_g': 'delta_w', 'delta_k_norm_g': 'delta_w', 'delta_sinks': 'delta_w', 'delta_w_conv_out': 'delta_w', 'delta_w_attn_out': 'delta_w', 'delta_gate_b': 'delta_w', 'delta_w_out': 'delta_w', 'new_m_norm_g': 'new_m', 'new_m_w_in': 'new_m', 'new_m_conv_w': 'new_m', 'new_m_q_norm_g': 'new_m', 'new_m_k_norm_g': 'new_m', 'new_m_sinks': 'new_m', 'new_m_w_conv_out': 'new_m', 'new_m_w_attn_out': 'new_m', 'new_m_gate_b': 'new_m', 'new_m_w_out': 'new_m', 'new_v_norm_g': 'new_v', 'new_v_w_in': 'new_v', 'new_v_conv_w': 'new_v', 'new_v_q_norm_g': 'new_v', 'new_v_k_norm_g': 'new_v', 'new_v_sinks': 'new_v', 'new_v_w_conv_out': 'new_v', 'new_v_w_attn_out': 'new_v', 'new_v_gate_b': 'new_v', 'new_v_w_out': 'new_v'}


def _forward(args):
    return _fwd_reference(*[args[k] for k in FWD_PARAMS])


def _output_shape():
    def fwd():
        inp = _fwd_setup_inputs(0)
        return _fwd_reference(*[inp[k] for k in FWD_PARAMS])
    out = _jax.eval_shape(fwd)
    return out.shape, out.dtype

N_MICROBATCH = 1
ADAM_LR = 0.001
ADAM_B1 = 0.9
ADAM_B2 = 0.999
ADAM_EPS = 1e-08
ADAM_WD = 0.01
ADAM_STEP = 10
PER_EXAMPLE_BATCH_AXIS = {'x': 0, 'loss_target': 0}
SHARED_INPUTS = []
_WEIGHT_DTYPES = {'norm_g': _jnp.float32, 'w_in': _jnp.float32, 'conv_w': _jnp.float32, 'q_norm_g': _jnp.float32, 'k_norm_g': _jnp.float32, 'sinks': _jnp.float32, 'w_conv_out': _jnp.float32, 'w_attn_out': _jnp.float32, 'gate_b': _jnp.float32, 'w_out': _jnp.float32}
MOMENT_SCALE = {'norm_g': 1.452510e+01, 'w_in': 1.847174e-01, 'conv_w': 1.867045e+00, 'q_norm_g': 7.641870e-01, 'k_norm_g': 7.586628e-01, 'sinks': 1.166243e-01, 'w_conv_out': 2.194212e-01, 'w_attn_out': 3.647184e-02, 'gate_b': 8.381339e-01, 'w_out': 2.116657e-01}


def _to_microbatches(a, axis):
    t = _jnp.moveaxis(a, axis, 0)
    t = t.reshape((N_MICROBATCH, t.shape[0] // N_MICROBATCH) + t.shape[1:])
    return _jnp.moveaxis(t, 1, axis + 1)


def setup_inputs(seed: int = 0) -> dict:
    inp = _fwd_setup_inputs(seed)
    key = _jax.random.fold_in(_jax.random.key(seed), 7919)
    shape, _ = _output_shape()
    out = dict(inp)
    out["loss_target"] = _jax.random.normal(_jax.random.fold_in(key, 0), shape, _jnp.float32)
    for i, name in enumerate(TWIN_WEIGHTS):
        w = inp[name].astype(_jnp.float32)
        if MOMENT_SCALE is None:
            s = _jnp.sqrt(_jnp.mean(_jnp.square(w)) + 1e-30)
        else:
            s = MOMENT_SCALE[name]
        km, kv = _jax.random.split(_jax.random.fold_in(key, i + 1))
        out[name] = w
        out["m_" + name] = s * _jax.random.normal(km, w.shape, _jnp.float32)
        out["v_" + name] = (s * s) * _jax.random.uniform(kv, w.shape, _jnp.float32, 0.5, 1.5)
    if N_MICROBATCH > 1:
        for name, axis in PER_EXAMPLE_BATCH_AXIS.items():
            out[name] = _to_microbatches(out[name], axis)
    return {'x': out['x'], 'norm_g': out['norm_g'], 'w_in': out['w_in'], 'conv_w': out['conv_w'], 'q_norm_g': out['q_norm_g'], 'k_norm_g': out['k_norm_g'], 'sinks': out['sinks'], 'w_conv_out': out['w_conv_out'], 'w_attn_out': out['w_attn_out'], 'gate_b': out['gate_b'], 'w_out': out['w_out'], 'loss_target': out['loss_target'], 'm_norm_g': out['m_norm_g'], 'm_w_in': out['m_w_in'], 'm_conv_w': out['m_conv_w'], 'm_q_norm_g': out['m_q_norm_g'], 'm_k_norm_g': out['m_k_norm_g'], 'm_sinks': out['m_sinks'], 'm_w_conv_out': out['m_w_conv_out'], 'm_w_attn_out': out['m_w_attn_out'], 'm_gate_b': out['m_gate_b'], 'm_w_out': out['m_w_out'], 'v_norm_g': out['v_norm_g'], 'v_w_in': out['v_w_in'], 'v_conv_w': out['v_conv_w'], 'v_q_norm_g': out['v_q_norm_g'], 'v_k_norm_g': out['v_k_norm_g'], 'v_sinks': out['v_sinks'], 'v_w_conv_out': out['v_w_conv_out'], 'v_w_attn_out': out['v_w_attn_out'], 'v_gate_b': out['v_gate_b'], 'v_w_out': out['v_w_out']}


def _loss(weights, diff, rest, loss_target):
    with _jax.named_scope("forward"):
        args = {**rest, TWIN_DIFF_INPUT: diff, **{k: w.astype(_WEIGHT_DTYPES[k]) for k, w in weights.items()}}
        y = _forward(args)
    with _jax.named_scope("loss_head"):
        err = _jnp.square(y.astype(_jnp.float32) - loss_target)
        return 0.5 * _jnp.sum(_jnp.mean(err, axis=-1)) if err.ndim else 0.5 * err


def _adamw(w, g, m, v):
    m = ADAM_B1 * m + (1.0 - ADAM_B1) * g
    v = ADAM_B2 * v + (1.0 - ADAM_B2) * _jnp.square(g)
    m_hat = m / (1.0 - ADAM_B1 ** ADAM_STEP)
    v_hat = v / (1.0 - ADAM_B2 ** ADAM_STEP)
    delta = -ADAM_LR * (m_hat / (_jnp.sqrt(v_hat) + ADAM_EPS) + ADAM_WD * w)
    return delta, m, v


def reference(x, norm_g, w_in, conv_w, q_norm_g, k_norm_g, sinks, w_conv_out, w_attn_out, gate_b, w_out, loss_target, m_norm_g, m_w_in, m_conv_w, m_q_norm_g, m_k_norm_g, m_sinks, m_w_conv_out, m_w_attn_out, m_gate_b, m_w_out, v_norm_g, v_w_in, v_conv_w, v_q_norm_g, v_k_norm_g, v_sinks, v_w_conv_out, v_w_attn_out, v_gate_b, v_w_out):
    given = dict(x=x, norm_g=norm_g, w_in=w_in, conv_w=conv_w, q_norm_g=q_norm_g, k_norm_g=k_norm_g, sinks=sinks, w_conv_out=w_conv_out, w_attn_out=w_attn_out, gate_b=gate_b, w_out=w_out, loss_target=loss_target, m_norm_g=m_norm_g, m_w_in=m_w_in, m_conv_w=m_conv_w, m_q_norm_g=m_q_norm_g, m_k_norm_g=m_k_norm_g, m_sinks=m_sinks, m_w_conv_out=m_w_conv_out, m_w_attn_out=m_w_attn_out, m_gate_b=m_gate_b, m_w_out=m_w_out, v_norm_g=v_norm_g, v_w_in=v_w_in, v_conv_w=v_conv_w, v_q_norm_g=v_q_norm_g, v_k_norm_g=v_k_norm_g, v_sinks=v_sinks, v_w_conv_out=v_w_conv_out, v_w_attn_out=v_w_attn_out, v_gate_b=v_gate_b, v_w_out=v_w_out)
    weights = {n: given[n] for n in TWIN_WEIGHTS}
    shared = {n: given[n] for n in SHARED_INPUTS}
    per_example = {n: given[n] for n in ['x']}
    grad_fn = _jax.value_and_grad(_loss, argnums=(0, 1))

    def one_microbatch(ex, loss_target):
        ex = dict(ex)
        diff = ex.pop(TWIN_DIFF_INPUT)
        return grad_fn(weights, diff, {**shared, **ex}, loss_target)

    if N_MICROBATCH == 1:
        loss, (grad_w, grad_x) = one_microbatch(per_example, given["loss_target"])
    else:
        def body(carry, xs):
            loss_sum, grad_sum = carry
            l_k, (gw_k, gx_k) = one_microbatch(xs[0], xs[1])
            with _jax.named_scope("update"):
                return (loss_sum + l_k, _jax.tree.map(_jnp.add, grad_sum, gw_k)), gx_k

        init = (_jnp.zeros((), _jnp.float32), _jax.tree.map(_jnp.zeros_like, weights))
        (loss, grad_w), grad_x = _jax.lax.scan(body, init, (per_example, given["loss_target"]))
    with _jax.named_scope("update"):
        delta_w, new_m, new_v = {}, {}, {}
        for n in TWIN_WEIGHTS:
            delta_w[n], new_m[n], new_v[n] = _adamw(weights[n], grad_w[n], given["m_" + n], given["v_" + n])
    return (loss, grad_x, *[grad_w[n] for n in TWIN_WEIGHTS], *[delta_w[n] for n in TWIN_WEIGHTS],
            *[new_m[n] for n in TWIN_WEIGHTS], *[new_v[n] for n in TWIN_WEIGHTS])
```

```python
import functools

import jax
import jax.numpy as jnp
from jax import lax
from jax.experimental import pallas as pl
from jax.experimental.pallas import tpu as pltpu

F32 = jnp.float32
BF16 = jnp.bfloat16

N_DEV = 8
DEPTH = 4
D = 1024
N_KV = 4
GROUP = 4
HEAD = 64
BLK = 128
KVW = N_KV * HEAD
IN_COLS = 8704
SHARD_COLS = IN_COLS // N_DEV
SHARD_ROWS = D // N_DEV
C_VC, C_BC, C_CC, C_ZC, C_Q, C_K, C_V, C_ZA, C_GA, C_GB = 0, 1024, 2048, 3072, 4096, 5120, 5376, 5632, 6656, 7680
EPS = 1e-6
NEG_INF = -1e30
SCALE = HEAD ** -0.5

ADAM_LR = 0.001
ADAM_B1 = 0.9
ADAM_B2 = 0.999
ADAM_EPS = 1e-08
ADAM_WD = 0.01
ADAM_STEP = 10

VMEM_LIMIT = 60 * 1024 * 1024
SM_ROWS = 16
SM_GATE, SM_CONV, SM_QG, SM_KG, SM_SINK, SM_NORM, SM_LOSS = 0, 1, 4, 5, 6, 7, 8


def _cparams(sem):
    return pltpu.CompilerParams(dimension_semantics=sem, vmem_limit_bytes=VMEM_LIMIT)


def _dot(a, b):
    return jnp.dot(a, b, preferred_element_type=F32)


def _dot_nt(a, b):
    return lax.dot_general(a, b, (((1,), (1,)), ((), ())), preferred_element_type=F32)


def _dot_tn(a, b):
    return lax.dot_general(a, b, (((0,), (0,)), ((), ())), preferred_element_type=F32)


def _dot2(x, sel):
    hi = x.astype(BF16)
    lo = (x - hi.astype(F32)).astype(BF16)
    return _dot(hi, sel) + _dot(lo, sel)


def _sigmoid(z):
    return 1.0 / (1.0 + jnp.exp(-z))


def _head_mean(t, sel, exp):
    return _dot2(_dot2(t, sel) * (1.0 / HEAD), exp)


def _bf(ref, c0, width):
    return ref[:, c0:c0 + width].astype(F32)


def _selectors():
    c = jnp.arange(D)
    sel_q = (c[:, None] // HEAD == jnp.arange(128)[None, :]).astype(BF16)
    ck = jnp.arange(KVW)
    sel_k = (ck[:, None] // HEAD == jnp.arange(128)[None, :]).astype(BF16)
    src = jnp.arange(KVW)[:, None]
    dst = jnp.arange(KVW)[None, :]
    rep = jnp.stack([((src // HEAD == h) & (src % HEAD == dst % HEAD)).astype(BF16) for h in range(N_KV)])
    fold = (c[:, None] % HEAD == jnp.arange(128)[None, :]).astype(BF16)
    return dict(sel_q=sel_q, exp_q=sel_q.T, sel_k=sel_k, exp_k=sel_k.T, rep=rep, rep_t=jnp.swapaxes(rep, 1, 2), fold=fold)


def inproj_fwd(x, ng, w, name):
    t = x.shape[0]
    tm = min(1024, t)
    cb = 2176
    def body(x_ref, ng_ref, w_ref, u_ref, h_ref, h_scr):
        @pl.when(pl.program_id(1) == 0)
        def _():
            xf = x_ref[...]
            r = lax.rsqrt(jnp.mean(xf * xf, axis=-1, keepdims=True) + EPS)
            hb = (xf * r * ng_ref[...]).astype(BF16)
            h_scr[...] = hb
            h_ref[...] = hb
        u_ref[...] = _dot(h_scr[...], w_ref[...]).astype(BF16)

    return pl.pallas_call(
        body, name=name, grid=(t // tm, IN_COLS // cb),
        in_specs=[pl.BlockSpec((tm, D), lambda i, j: (i, 0)), pl.BlockSpec((1, D), lambda i, j: (0, 0)),
                  pl.BlockSpec((D, cb), lambda i, j: (0, j))],
        out_specs=[pl.BlockSpec((tm, cb), lambda i, j: (i, j)), pl.BlockSpec((tm, D), lambda i, j: (i, 0))],
        out_shape=[jax.ShapeDtypeStruct((t, IN_COLS), BF16), jax.ShapeDtypeStruct((t, D), BF16)],
        scratch_shapes=[pltpu.VMEM((tm, D), BF16)],
        compiler_params=_cparams(("arbitrary", "arbitrary")),
    )(x, ng, w)


def _conv_fwd(u_ref, uvc_prev, ucc_prev, cw_ref, is_first, tm):
    p = _bf(u_ref, C_CC, D) * _bf(u_ref, C_VC, D)
    pprev = ucc_prev[...].astype(F32) * uvc_prev[...].astype(F32)
    pprev = jnp.where(is_first, 0.0, pprev)
    row = lax.broadcasted_iota(jnp.int32, (tm, 1), 0)
    p1 = jnp.where(row == 0, pprev[15:16, :], pltpu.roll(p, 1, 0))
    p2 = jnp.where(row == 0, pprev[14:15, :], jnp.where(row == 1, pprev[15:16, :], pltpu.roll(p, 2, 0)))
    cw = cw_ref[...]
    conv = cw[0:1, :] * p2 + cw[1:2, :] * p1 + cw[2:3, :] * p
    return p, p1, p2, conv


def _attn_inputs(u_ref, ukv_prev, qg_ref, kg_ref, c, tm):
    q = _bf(u_ref, C_Q, D)
    rq = lax.rsqrt(_head_mean(q * q, c["sel_q"][...], c["exp_q"][...]) + EPS)
    qhat = q * rq
    qn = (qhat * qg_ref[...]).astype(BF16)
    kband = jnp.concatenate([ukv_prev[:, 0:KVW].astype(F32), _bf(u_ref, C_K, KVW)], axis=0)
    rk = lax.rsqrt(_head_mean(kband * kband, c["sel_k"][...], c["exp_k"][...]) + EPS)
    khat = kband * rk
    knb = (khat * kg_ref[...]).astype(BF16)
    vband = jnp.concatenate([ukv_prev[:, KVW:2 * KVW], u_ref[:, C_V:C_V + KVW]], axis=0)
    kt = [_dot(knb, c["rep"][h]).astype(BF16) for h in range(N_KV)]
    vt = [_dot(vband, c["rep"][h]).astype(BF16) for h in range(N_KV)]
    return qhat, rq, qn, khat, rk, kt, vt


def _attn_masks(is_first):
    rows = GROUP * BLK
    r = lax.broadcasted_iota(jnp.int32, (rows, 2 * BLK), 0)
    kk = lax.broadcasted_iota(jnp.int32, (rows, 2 * BLK), 1)
    qq = r % BLK
    valid = (kk > qq) & (kk <= qq + BLK)
    valid_first = valid & ((kk >= BLK) | jnp.logical_not(is_first))
    lane_grp = lax.broadcasted_iota(jnp.int32, (BLK, KVW), 1) // HEAD
    row_grp = lax.broadcasted_iota(jnp.int32, (rows, 1), 0) // BLK
    return valid, valid_first, lane_grp, row_grp


def _sink_col(sinks_ref, h, row_grp):
    col = jnp.full(row_grp.shape, sinks_ref[0, GROUP * h], F32)
    for gi in range(1, GROUP):
        col = jnp.where(row_grp == gi, sinks_ref[0, GROUP * h + gi], col)
    return col


def _stack_groups(a256, lane_grp):
    zero = jnp.zeros_like(a256)
    return jnp.concatenate([jnp.where(lane_grp == gi, a256, zero) for gi in range(GROUP)], axis=0)


def _unstack_groups(a4, lane_grp):
    out = jnp.where(lane_grp == 0, a4[0:BLK], 0.0)
    for gi in range(1, GROUP):
        out = out + jnp.where(lane_grp == gi, a4[gi * BLK:(gi + 1) * BLK], 0.0)
    return out


def _softmax_block(qs, kt_b, valid, sink):
    s = _dot_nt(qs, kt_b) * SCALE
    s = jnp.where(valid, s, NEG_INF)
    m = jnp.maximum(jnp.max(s, axis=-1, keepdims=True), sink)
    e = jnp.exp(s - m)
    es = jnp.exp(sink - m)
    inv = 1.0 / (jnp.sum(e, axis=-1, keepdims=True) + es)
    return e * inv, es * inv


def _mixer_specs(t, tm, n_tiles, tile_of):
    nb = tm // BLK
    u_spec = pl.BlockSpec((tm, IN_COLS), lambda g: (tile_of(g), 0))
    ukv_prev = pl.BlockSpec((BLK, 2 * KVW), lambda g: (jnp.maximum(tile_of(g) * nb - 1, 0), C_K // (2 * KVW)))
    uvc_prev = pl.BlockSpec((16, D), lambda g: (jnp.maximum(tile_of(g) * (tm // 16) - 1, 0), C_VC // D))
    ucc_prev = pl.BlockSpec((16, D), lambda g: (jnp.maximum(tile_of(g) * (tm // 16) - 1, 0), C_CC // D))
    return u_spec, ukv_prev, uvc_prev, ucc_prev


def _full(shape):
    n = len(shape)
    return pl.BlockSpec(shape, lambda g: (0,) * n)


def mixer_fwd(x, u, cw, qg, kg, sinks, gb, wco, wao, wout, c, name):
    t = x.shape[0]
    tm = min(256, t)
    n_tiles = t // tm
    nb = tm // BLK
    cn = sorted(c)

    def body(x_ref, u_ref, ukv_prev, uvc_prev, ucc_prev, cw_ref, qg_ref, kg_ref, sinks_ref, gb_ref, wco_ref, wao_ref,
             wout_ref, *rest):
        cref = dict(zip(cn, rest[:len(cn)]))
        xo_ref, ya_ref, yb_ref, o_scr = rest[len(cn):]
        is_first = pl.program_id(0) == 0
        _, _, _, conv = _conv_fwd(u_ref, uvc_prev, ucc_prev, cw_ref, is_first, tm)
        zc = _bf(u_ref, C_ZC, D)
        yc = _bf(u_ref, C_BC, D) * conv * (zc * _sigmoid(zc))
        ya = _dot(yc.astype(BF16), wco_ref[...])

        _, _, qn, _, _, kt, vt = _attn_inputs(u_ref, ukv_prev, qg_ref, kg_ref, cref, tm)
        valid, valid_first, lane_grp, row_grp = _attn_masks(is_first)
        for h in range(N_KV):
            sink = _sink_col(sinks_ref, h, row_grp)
            for b in range(nb):
                qs = _stack_groups(qn[b * BLK:(b + 1) * BLK, h * KVW:(h + 1) * KVW], lane_grp)
                pn, _ = _softmax_block(qs, kt[h][b * BLK:(b + 2) * BLK], valid_first if b == 0 else valid, sink)
                o4 = _dot(pn.astype(BF16), vt[h][b * BLK:(b + 2) * BLK])
                o_scr[b * BLK:(b + 1) * BLK, h * KVW:(h + 1) * KVW] = _unstack_groups(o4, lane_grp)
        za = _bf(u_ref, C_ZA, D)
        ob = o_scr[...] * (za * _sigmoid(za))
        yb = _dot(ob.astype(BF16), wao_ref[...])

        g_a = _sigmoid(_bf(u_ref, C_GA, D) + gb_ref[:, 0:D])
        g_b = _sigmoid(_bf(u_ref, C_GB, D) + gb_ref[:, D:2 * D])
        merged = g_a * ya + g_b * yb
        xo_ref[...] = x_ref[...] + _dot(merged.astype(BF16), wout_ref[...])
        ya_ref[...] = ya.astype(BF16)
        yb_ref[...] = yb.astype(BF16)

    u_spec, ukv_prev, uvc_prev, ucc_prev = _mixer_specs(t, tm, n_tiles, lambda g: g)
    tok = pl.BlockSpec((tm, D), lambda g: (g, 0))
    consts = [c[k] for k in cn]
    return pl.pallas_call(
        body, name=name, grid=(n_tiles,),
        in_specs=[tok, u_spec, ukv_prev, uvc_prev, ucc_prev, _full((8, D)), _full((1, D)), _full((1, KVW)),
                  pl.BlockSpec(memory_space=pltpu.SMEM), _full((1, 2 * D)), _full((D, D)), _full((D, D)), _full((D, D))]
                 + [_full(a.shape) for a in consts],
        out_specs=[tok, tok, tok],
        out_shape=[jax.ShapeDtypeStruct((t, D), F32), jax.ShapeDtypeStruct((t, D), BF16), jax.ShapeDtypeStruct((t, D), BF16)],
        scratch_shapes=[pltpu.VMEM((tm, D), F32)],
        compiler_params=_cparams(("arbitrary",)),
    )(x, u, u, u, u, cw, qg, kg, sinks, gb, wco, wao, wout, *consts)


def mixer_bwd(dout, u, ya, yb, cw, qg, kg, sinks, gb, wco, wao, wout, c, name):
    t = dout.shape[0]
    tm = min(256, t)
    n_tiles = t // tm
    nb = tm // BLK
    kb = tm + BLK
    cn = sorted(c)

    def body(dout_ref, u_ref, ukv_prev, uvc_prev, ucc_prev, ya_ref, yb_ref, cw_ref, qg_ref, kg_ref, sinks_ref, gb_ref,
             wco_ref, wao_ref, wout_ref, *rest):
        cref = dict(zip(cn, rest[:len(cn)]))
        (du_ref, small_ref, merged_ref, yc_ref, ob_ref, dya_ref, dyb_ref,
         o_scr, dq_scr, dk4_scr, dv4_scr, carry_kv, carry_conv) = rest[len(cn):]
        g = pl.program_id(0)
        is_first = g == n_tiles - 1

        @pl.when(g == 0)
        def _():
            carry_kv[...] = jnp.zeros_like(carry_kv)
            carry_conv[...] = jnp.zeros_like(carry_conv)
            small_ref[...] = jnp.zeros_like(small_ref)

        dout = dout_ref[...]
        dout_b = dout.astype(BF16)
        ya_v = ya_ref[...].astype(F32)
        yb_v = yb_ref[...].astype(F32)
        g_a = _sigmoid(_bf(u_ref, C_GA, D) + gb_ref[:, 0:D])
        g_b = _sigmoid(_bf(u_ref, C_GB, D) + gb_ref[:, D:2 * D])
        merged = g_a * ya_v + g_b * yb_v
        dmerged = _dot_nt(dout_b, wout_ref[...])
        merged_ref[...] = merged.astype(BF16)
        dya = dmerged * g_a
        dyb = dmerged * g_b
        dgl_a = dmerged * ya_v * g_a * (1.0 - g_a)
        dgl_b = dmerged * yb_v * g_b * (1.0 - g_b)
        du_ref[:, C_GA:C_GA + D] = dgl_a.astype(BF16)
        du_ref[:, C_GB:C_GB + D] = dgl_b.astype(BF16)
        small_ref[SM_GATE:SM_GATE + 1, 0:D] += jnp.sum(dgl_a, axis=0, keepdims=True)
        small_ref[SM_GATE:SM_GATE + 1, D:2 * D] += jnp.sum(dgl_b, axis=0, keepdims=True)

        p, p1, p2, conv = _conv_fwd(u_ref, uvc_prev, ucc_prev, cw_ref, is_first, tm)
        zc = _bf(u_ref, C_ZC, D)
        bc = _bf(u_ref, C_BC, D)
        sg = _sigmoid(zc)
        sc = zc * sg
        yc = bc * conv * sc
        dya_b = dya.astype(BF16)
        yc_ref[...] = yc.astype(BF16)
        dya_ref[...] = dya_b
        dyc = _dot_nt(dya_b, wco_ref[...])
        du_ref[:, C_BC:C_BC + D] = (dyc * conv * sc).astype(BF16)
        du_ref[:, C_ZC:C_ZC + D] = (dyc * bc * conv * (sg * (1.0 + zc * (1.0 - sg)))).astype(BF16)
        dconv = dyc * bc * sc
        small_ref[SM_CONV + 2:SM_CONV + 3, 0:D] += jnp.sum(dconv * p, axis=0, keepdims=True)
        small_ref[SM_CONV + 1:SM_CONV + 2, 0:D] += jnp.sum(dconv * p1, axis=0, keepdims=True)
        small_ref[SM_CONV:SM_CONV + 1, 0:D] += jnp.sum(dconv * p2, axis=0, keepdims=True)
        row = lax.broadcasted_iota(jnp.int32, (tm, 1), 0)
        nxt = carry_conv[...]
        d1 = jnp.where(row == tm - 1, nxt[0:1, :], pltpu.roll(dconv, tm - 1, 0))
        d2 = jnp.where(row == tm - 1, nxt[1:2, :], jnp.where(row == tm - 2, nxt[0:1, :], pltpu.roll(dconv, tm - 2, 0)))
        carry_conv[...] = dconv[0:8, :]
        cw = cw_ref[...]
        dp = cw[2:3, :] * dconv + cw[1:2, :] * d1 + cw[0:1, :] * d2
        du_ref[:, C_CC:C_CC + D] = (dp * _bf(u_ref, C_VC, D)).astype(BF16)
        du_ref[:, C_VC:C_VC + D] = (dp * _bf(u_ref, C_CC, D)).astype(BF16)

        dyb_b = dyb.astype(BF16)
        dob = _dot_nt(dyb_b, wao_ref[...])
        za = _bf(u_ref, C_ZA, D)
        sga = _sigmoid(za)
        sa = za * sga
        do = dob * sa
        qhat, rq, qn, khat, rk, kt, vt = _attn_inputs(u_ref, ukv_prev, qg_ref, kg_ref, cref, tm)
        valid, valid_first, lane_grp, row_grp = _attn_masks(is_first)
        dk4_scr[...] = jnp.zeros_like(dk4_scr)
        dv4_scr[...] = jnp.zeros_like(dv4_scr)
        lane16 = lax.broadcasted_iota(jnp.int32, (1, 2 * D), 1)
        dsink_row = jnp.zeros((1, 2 * D), F32)
        for h in range(N_KV):
            sink = _sink_col(sinks_ref, h, row_grp)
            dsink_col = jnp.zeros((GROUP * BLK, 1), F32)
            for b in range(nb):
                rows = slice(b * BLK, (b + 1) * BLK)
                band = slice(b * BLK, (b + 2) * BLK)
                cols = slice(h * KVW, (h + 1) * KVW)
                qs = _stack_groups(qn[rows, cols], lane_grp)
                pn, ps = _softmax_block(qs, kt[h][band], valid_first if b == 0 else valid, sink)
                pn_b = pn.astype(BF16)
                o4 = _dot(pn_b, vt[h][band])
                o_scr[rows, cols] = _unstack_groups(o4, lane_grp)
                dos = _stack_groups(do[rows, cols], lane_grp).astype(BF16)
                dpn = _dot_nt(dos, vt[h][band])
                delta = jnp.sum(pn * dpn, axis=-1, keepdims=True)
                ds = (pn * (dpn - delta) * SCALE).astype(BF16)
                dsink_col = dsink_col - ps * delta
                dq_scr[rows, cols] = _unstack_groups(_dot(ds, kt[h][band]), lane_grp)
                dk4_scr[h, band, :] += _dot_tn(ds, qs)
                dv4_scr[h, band, :] += _dot_tn(pn_b, dos)
            for gi in range(GROUP):
                tot = jnp.sum(dsink_col[gi * BLK:(gi + 1) * BLK, :], axis=0, keepdims=True)
                dsink_row = dsink_row + jnp.where(lane16 == GROUP * h + gi, tot, 0.0)
        small_ref[SM_SINK:SM_SINK + 1, :] += dsink_row

        o = o_scr[...]
        ob_ref[...] = (o * sa).astype(BF16)
        dyb_ref[...] = dyb_b
        du_ref[:, C_ZA:C_ZA + D] = (dob * o * (sga * (1.0 + za * (1.0 - sga)))).astype(BF16)

        dqn = dq_scr[...]
        small_ref[SM_QG:SM_QG + 1, 0:D] += jnp.sum(dqn * qhat, axis=0, keepdims=True)
        dqh = dqn * qg_ref[...]
        dq = rq * (dqh - qhat * _head_mean(dqh * qhat, cref["sel_q"][...], cref["exp_q"][...]))
        du_ref[:, C_Q:C_Q + D] = dq.astype(BF16)

        dkn_band = jnp.zeros((kb, KVW), F32)
        dv_band = jnp.zeros((kb, KVW), F32)
        for h in range(N_KV):
            dkn_band = dkn_band + _dot2(dk4_scr[h], cref["rep_t"][h])
            dv_band = dv_band + _dot2(dv4_scr[h], cref["rep_t"][h])
        carried = carry_kv[...]
        pad = jnp.zeros((tm - BLK, KVW), F32)
        if nb > 1:
            dkn = dkn_band[BLK:, :] + jnp.concatenate([pad, carried[:, 0:KVW]], axis=0)
            dv = dv_band[BLK:, :] + jnp.concatenate([pad, carried[:, KVW:2 * KVW]], axis=0)
        else:
            dkn = dkn_band[BLK:, :] + carried[:, 0:KVW]
            dv = dv_band[BLK:, :] + carried[:, KVW:2 * KVW]
        carry_kv[:, 0:KVW] = dkn_band[0:BLK, :]
        carry_kv[:, KVW:2 * KVW] = dv_band[0:BLK, :]
        khat_t = khat[BLK:, :]
        small_ref[SM_KG:SM_KG + 1, 0:KVW] += jnp.sum(dkn * khat_t, axis=0, keepdims=True)
        dkh = dkn * kg_ref[...]
        dk = rk[BLK:, :] * (dkh - khat_t * _head_mean(dkh * khat_t, cref["sel_k"][...], cref["exp_k"][...]))
        du_ref[:, C_K:C_K + KVW] = dk.astype(BF16)
        du_ref[:, C_V:C_V + KVW] = dv.astype(BF16)

    rev = lambda g: n_tiles - 1 - g
    u_spec, ukv_prev, uvc_prev, ucc_prev = _mixer_specs(t, tm, n_tiles, rev)
    tok = pl.BlockSpec((tm, D), lambda g: (rev(g), 0))
    consts = [c[k] for k in cn]
    wspec = _full((D, D))
    return pl.pallas_call(
        body, name=name, grid=(n_tiles,),
        in_specs=[tok, u_spec, ukv_prev, uvc_prev, ucc_prev, tok, tok, _full((8, D)), _full((1, D)), _full((1, KVW)),
                  pl.BlockSpec(memory_space=pltpu.SMEM), _full((1, 2 * D)), wspec, wspec, wspec]
                 + [_full(a.shape) for a in consts],
        out_specs=[pl.BlockSpec((tm, IN_COLS), lambda g: (rev(g), 0)), _full((SM_ROWS, 2 * D))] + [tok] * 5,
        out_shape=[jax.ShapeDtypeStruct((t, IN_COLS), BF16), jax.ShapeDtypeStruct((SM_ROWS, 2 * D), F32)]
                  + [jax.ShapeDtypeStruct((t, D), BF16)] * 5,
        scratch_shapes=[pltpu.VMEM((tm, D), F32), pltpu.VMEM((tm, D), F32),
                        pltpu.VMEM((N_KV, kb, KVW), F32), pltpu.VMEM((N_KV, kb, KVW), F32),
                        pltpu.VMEM((BLK, 2 * KVW), F32), pltpu.VMEM((8, D), F32)],
        compiler_params=_cparams(("arbitrary",)),
    )(dout, u, u, u, u, ya, yb, cw, qg, kg, sinks, gb, wco, wao, wout, *consts)


def matmul_tn(a, b, name):
    t, n = b.shape
    tk = min(1024, t)
    cb = 2176 if n == IN_COLS else n
    nk = t // tk

    def body(a_ref, b_ref, o_ref, acc):
        k = pl.program_id(1)
        prod = _dot_tn(a_ref[...].astype(BF16), b_ref[...].astype(BF16))

        @pl.when(k == 0)
        def _():
            acc[...] = prod

        @pl.when(k > 0)
        def _():
            acc[...] += prod

        @pl.when(k == nk - 1)
        def _():
            o_ref[...] = acc[...].astype(BF16)

    return pl.pallas_call(
        body, name=name, grid=(n // cb, nk),
        in_specs=[pl.BlockSpec((tk, D), lambda j, k: (k, 0)), pl.BlockSpec((tk, cb), lambda j, k: (k, j))],
        out_specs=pl.BlockSpec((D, cb), lambda j, k: (0, j)),
        out_shape=jax.ShapeDtypeStruct((D, n), BF16),
        scratch_shapes=[pltpu.VMEM((D, cb), F32)],
        compiler_params=_cparams(("arbitrary", "arbitrary")),
    )(a, b)


def inproj_bwd_x(du, w, x, ng, dout, name):
    t = x.shape[0]
    tm = min(512, t)
    kc = 2176
    nk = IN_COLS // kc

    def body(du_ref, w_ref, x_ref, ng_ref, dout_ref, dx_ref, dng_ref, acc):
        i = pl.program_id(0)
        k = pl.program_id(1)
        prod = _dot_nt(du_ref[...], w_ref[...])

        @pl.when(k == 0)
        def _():
            acc[...] = prod

        @pl.when(k > 0)
        def _():
            acc[...] += prod

        @pl.when((i == 0) & (k == 0))
        def _():
            dng_ref[...] = jnp.zeros_like(dng_ref)

        @pl.when(k == nk - 1)
        def _():
            dh = acc[...]
            xf = x_ref[...]
            r = lax.rsqrt(jnp.mean(xf * xf, axis=-1, keepdims=True) + EPS)
            xhat = xf * r
            dng_ref[0:1, :] += jnp.sum(dh * xhat, axis=0, keepdims=True)
            dxh = dh * ng_ref[...]
            dx_ref[...] = dout_ref[...] + r * (dxh - xhat * jnp.mean(dxh * xhat, axis=-1, keepdims=True))

    tok = pl.BlockSpec((tm, D), lambda i, k: (i, 0))
    return pl.pallas_call(
        body, name=name, grid=(t // tm, nk),
        in_specs=[pl.BlockSpec((tm, kc), lambda i, k: (i, k)), pl.BlockSpec((D, kc), lambda i, k: (0, k)), tok,
                  pl.BlockSpec((1, D), lambda i, k: (0, 0)), tok],
        out_specs=[tok, pl.BlockSpec((8, D), lambda i, k: (0, 0))],
        out_shape=[jax.ShapeDtypeStruct((t, D), F32), jax.ShapeDtypeStruct((8, D), F32)],
        scratch_shapes=[pltpu.VMEM((tm, D), F32)],
        compiler_params=_cparams(("arbitrary", "arbitrary")),
    )(du, w, x, ng, dout)


def loss_head(y, target, name):
    t = y.shape[0]
    tm = min(1024, t)

    def body(y_ref, t_ref, dy_ref, loss_ref):
        @pl.when(pl.program_id(0) == 0)
        def _():
            loss_ref[...] = jnp.zeros_like(loss_ref)
        err = y_ref[...] - t_ref[...]
        dy_ref[...] = err * (1.0 / D)
        part = jnp.sum(jnp.sum(err * err, axis=-1, keepdims=True) * (1.0 / D), axis=0, keepdims=True)
        loss_ref[...] += 0.5 * part

    tok = pl.BlockSpec((tm, D), lambda i: (i, 0))
    return pl.pallas_call(
        body, name=name, grid=(t // tm,), in_specs=[tok, tok],
        out_specs=[tok, pl.BlockSpec((8, 128), lambda i: (0, 0))],
        out_shape=[jax.ShapeDtypeStruct((t, D), F32), jax.ShapeDtypeStruct((8, 128), F32)],
        compiler_params=_cparams(("arbitrary",)),
    )(y, target)


def layer_operands(l, norm_g, conv_w_full, q_norm_g, k_norm_g, sinks, gate_b, w_in_b, wco_b, wao_b, wout_b):
    return dict(
        ng=norm_g[l][None, :], cw=jnp.pad(conv_w_full[l], ((0, 5), (0, 0))),
        qg=jnp.tile(q_norm_g[l], D // HEAD)[None, :], kg=jnp.tile(k_norm_g[l], N_KV)[None, :],
        sinks=sinks[l][None, :], gb=gate_b[l][None, :],
        w_in=w_in_b[l], wco=wco_b[l], wao=wao_b[l], wout=wout_b[l])


def layer_fwd(x, lw, c, l):
    u, h = inproj_fwd(x, lw["ng"], lw["w_in"], f"inproj_fwd_{l}")
    xo, ya, yb = mixer_fwd(x, u, lw["cw"], lw["qg"], lw["kg"], lw["sinks"], lw["gb"], lw["wco"], lw["wao"], lw["wout"], c,
                           f"mixer_fwd_{l}")
    return xo, (x, u, h, ya, yb)


def layer_bwd(dout, saved, lw, c, l):
    x, u, h, ya, yb = saved
    du, small, merged, yc, ob, dya, dyb = mixer_bwd(dout, u, ya, yb, lw["cw"], lw["qg"], lw["kg"], lw["sinks"], lw["gb"],
                                                    lw["wco"], lw["wao"], lw["wout"], c, f"mixer_bwd_{l}")
    grads = dict(
        wout=matmul_tn(merged, dout, f"dw_out_{l}"), wco=matmul_tn(yc, dya, f"dw_conv_out_{l}"),
        wao=matmul_tn(ob, dyb, f"dw_attn_out_{l}"), w_in=matmul_tn(h, du, f"dw_in_{l}"), small=small)
    dx, grads["dng"] = inproj_bwd_x(du, lw["w_in"], x, lw["ng"], dout, f"inproj_bwd_{l}")
    return dx, grads


MESH = pl.DeviceIdType.MESH
ANY = pl.BlockSpec(memory_space=pl.ANY)


def _place():
    return lax.axis_index("x"), lax.axis_index("y"), lax.axis_index("c")


def all_gather(arrs, name):
    n = len(arrs)

    def body(*refs):
        ins, outs = refs[:n], refs[n:2 * n]
        send_sems, recv_sems, local_sems = refs[2 * n:]
        x, y, c = _place()
        me, sibling = (x, y, c), (x, y, 1 - c)
        chips = [(1 - x, y), (x, 1 - y), (1 - x, 1 - y)]

        def slot(a, block):
            px, py, pc = block
            return outs[a].at[4 * px + 2 * py + pc]

        def copy(a, k, block, to, src=None):
            return pltpu.make_async_remote_copy(
                src_ref=slot(a, block) if src is None else src, dst_ref=slot(a, block),
                send_sem=send_sems.at[a, k], recv_sem=recv_sems.at[a, k], device_id=to, device_id_type=MESH)

        mine = [pltpu.make_async_copy(ins[a], slot(a, me), local_sems.at[a]) for a in range(n)]
        for cp in mine:
            cp.start()
        first = []
        for a in range(n):
            first.append(copy(a, 0, me, sibling, src=ins[a]))
            first += [copy(a, 1 + j, me, (*chip, c), src=ins[a]) for j, chip in enumerate(chips)]
        for cp in first:
            cp.start()
        passed = []
        for j, chip in enumerate(chips):
            for a in range(n):
                copy(a, 1 + j, (*chip, c), me).wait_recv()
                passed.append(copy(a, 4 + j, (*chip, c), sibling))
                passed[-1].start()
        for a in range(n):
            copy(a, 0, sibling, me).wait_recv()
            for j, chip in enumerate(chips):
                copy(a, 4 + j, (*chip, 1 - c), me).wait_recv()
        for cp in first + passed:
            cp.wait_send()
        for cp in mine:
            cp.wait()

    return pl.pallas_call(
        body, name=name, in_specs=[ANY] * n, out_specs=[ANY] * n,
        out_shape=[jax.ShapeDtypeStruct((N_DEV,) + a.shape, a.dtype) for a in arrs],
        scratch_shapes=[pltpu.SemaphoreType.DMA((n, 7)), pltpu.SemaphoreType.DMA((n, 7)), pltpu.SemaphoreType.DMA((n,))],
    )(*arrs)


def scatter(arrs, name):
    n = len(arrs)

    def body(*refs):
        ins, outs = refs[:n], refs[n:2 * n]
        send_sems, recv_sems, local_sems = refs[2 * n:]
        x, y, c = _place()
        me = 4 * x + 2 * y + c
        peers = []
        for k in range(1, N_DEV):
            px = 1 - x if k & 4 else x
            py = 1 - y if k & 2 else y
            pc = 1 - c if k & 1 else c
            peers.append((px, py, pc))

        def copy(a, k):
            px, py, pc = peers[k - 1]
            theirs = 4 * px + 2 * py + pc
            return pltpu.make_async_remote_copy(
                src_ref=ins[a].at[theirs], dst_ref=outs[a].at[me], send_sem=send_sems.at[a, k - 1],
                recv_sem=recv_sems.at[a, k - 1], device_id=(px, py, pc), device_id_type=MESH)

        def arrival(a, k):
            px, py, pc = peers[k - 1]
            theirs = 4 * px + 2 * py + pc
            return pltpu.make_async_remote_copy(
                src_ref=ins[a].at[theirs], dst_ref=outs[a].at[theirs], send_sem=send_sems.at[a, k - 1],
                recv_sem=recv_sems.at[a, k - 1], device_id=(px, py, pc), device_id_type=MESH)

        mine = [pltpu.make_async_copy(ins[a].at[me], outs[a].at[me], local_sems.at[a]) for a in range(n)]
        for cp in mine:
            cp.start()
        sends = [copy(a, k) for a in range(n) for k in range(1, N_DEV)]
        for cp in sends:
            cp.start()
        for a in range(n):
            for k in range(1, N_DEV):
                arrival(a, k).wait_recv()
        for cp in sends:
            cp.wait_send()
        for cp in mine:
            cp.wait()

    return pl.pallas_call(
        body, name=name, in_specs=[ANY] * n, out_specs=[ANY] * n,
        out_shape=[jax.ShapeDtypeStruct(a.shape, a.dtype) for a in arrs],
        scratch_shapes=[pltpu.SemaphoreType.DMA((n, 7)), pltpu.SemaphoreType.DMA((n, 7)), pltpu.SemaphoreType.DMA((n,))],
    )(*arrs)


def adamw(w, m, v, parts, name):
    r, cdim = w.shape
    n_parts = parts.shape[0]
    rb = 256 if r % 256 == 0 else r

    def body(w_ref, m_ref, v_ref, p_ref, g_ref, d_ref, mo_ref, vo_ref):
        g = p_ref[0].astype(F32)
        for i in range(1, n_parts):
            g = g + p_ref[i].astype(F32)
        m_new = ADAM_B1 * m_ref[...] + (1.0 - ADAM_B1) * g
        v_new = ADAM_B2 * v_ref[...] + (1.0 - ADAM_B2) * (g * g)
        m_hat = m_new / (1.0 - ADAM_B1 ** ADAM_STEP)
        v_hat = v_new / (1.0 - ADAM_B2 ** ADAM_STEP)
        g_ref[...] = g
        d_ref[...] = -ADAM_LR * (m_hat / (jnp.sqrt(v_hat) + ADAM_EPS) + ADAM_WD * w_ref[...])
        mo_ref[...] = m_new
        vo_ref[...] = v_new

    blk = pl.BlockSpec((rb, cdim), lambda i: (i, 0))
    return pl.pallas_call(
        body, name=name, grid=(r // rb,),
        in_specs=[blk, blk, blk, pl.BlockSpec((n_parts, rb, cdim), lambda i: (0, i, 0))],
        out_specs=[blk] * 4, out_shape=[jax.ShapeDtypeStruct((r, cdim), F32)] * 4,
        compiler_params=_cparams(("arbitrary",)),
    )(w, m, v, parts)


def small_sum(parts, fold, name):
    rows = parts.shape[1]

    def dot3(xv, sel):
        out = jnp.zeros((xv.shape[0], sel.shape[1]), F32)
        for _ in range(3):
            hi = xv.astype(BF16)
            out = out + _dot(hi, sel)
            xv = xv - hi.astype(F32)
        return out

    def body(p_ref, fold_ref, o_ref):
        tot = p_ref[0]
        for i in range(1, N_DEV):
            tot = tot + p_ref[i]
        o_ref[...] = tot
        for l in range(rows // SM_ROWS):
            blk = tot[l * SM_ROWS:l * SM_ROWS + 8, 0:D]
            folded = dot3(blk, fold_ref[...])
            o_ref[l * SM_ROWS + 9:l * SM_ROWS + 10, 0:128] = folded[SM_QG:SM_QG + 1, :]
            o_ref[l * SM_ROWS + 10:l * SM_ROWS + 11, 0:128] = folded[SM_KG:SM_KG + 1, :]

    return pl.pallas_call(
        body, name=name, out_shape=jax.ShapeDtypeStruct((rows, 2 * D), F32),
        compiler_params=_cparams(None),
    )(parts, fold)


def kernel(x, norm_g, w_in, conv_w, q_norm_g, k_norm_g, sinks, w_conv_out, w_attn_out, gate_b, w_out, loss_target, m_norm_g, m_w_in, m_conv_w, m_q_norm_g, m_k_norm_g, m_sinks, m_w_conv_out, m_w_attn_out, m_gate_b, m_w_out, v_norm_g, v_w_in, v_conv_w, v_q_norm_g, v_k_norm_g, v_sinks, v_w_conv_out, v_w_attn_out, v_gate_b, v_w_out):
    c = _selectors()
    me = 4 * lax.axis_index("x") + 2 * lax.axis_index("y") + lax.axis_index("c")

    g_in, g_co, g_ao, g_out, g_cw = all_gather(
        [w_in.astype(BF16), w_conv_out.astype(BF16), w_attn_out.astype(BF16), w_out.astype(BF16), conv_w], "gather_weights")
    w_in_b = jnp.transpose(g_in, (1, 2, 0, 3)).reshape(DEPTH, D, IN_COLS)
    rows_full = lambda g: jnp.transpose(g, (1, 0, 2, 3)).reshape(DEPTH, D, D)
    wco_b, wao_b, wout_b = rows_full(g_co), rows_full(g_ao), rows_full(g_out)
    conv_full = jnp.transpose(g_cw, (1, 2, 0, 3)).reshape(DEPTH, 3, D)

    lws = [layer_operands(l, norm_g, conv_full, q_norm_g, k_norm_g, sinks, gate_b, w_in_b, wco_b, wao_b, wout_b)
           for l in range(DEPTH)]
    h = x[0]
    saved = []
    for l in range(DEPTH):
        h, s = layer_fwd(h, lws[l], c, l)
        saved.append(s)
    dh, loss_part = loss_head(h, loss_target[0], "loss_head")
    grads = [None] * DEPTH
    for l in reversed(range(DEPTH)):
        dh, grads[l] = layer_bwd(dh, saved[l], lws[l], c, l)

    d_in = jnp.stack([grads[l]["w_in"] for l in range(DEPTH)]).reshape(DEPTH, D, N_DEV, SHARD_COLS)
    d_in = jnp.transpose(d_in, (2, 0, 1, 3))
    rows_split = lambda k: jnp.transpose(
        jnp.stack([grads[l][k] for l in range(DEPTH)]).reshape(DEPTH, N_DEV, SHARD_ROWS, D), (1, 0, 2, 3))
    p_in, p_co, p_ao, p_out = scatter([d_in, rows_split("wco"), rows_split("wao"), rows_split("wout")], "scatter_grads")

    def update(w, m, v, parts, name):
        shp = w.shape
        r = shp[0] * shp[1]
        outs = adamw(w.reshape(r, shp[2]), m.reshape(r, shp[2]), v.reshape(r, shp[2]), parts.reshape(N_DEV, r, shp[2]), name)
        return [o.reshape(shp) for o in outs]

    u_in = update(w_in, m_w_in, v_w_in, p_in, "adamw_w_in")
    u_co = update(w_conv_out, m_w_conv_out, v_w_conv_out, p_co, "adamw_w_conv_out")
    u_ao = update(w_attn_out, m_w_attn_out, v_w_attn_out, p_ao, "adamw_w_attn_out")
    u_out = update(w_out, m_w_out, v_w_out, p_out, "adamw_w_out")

    blocks = []
    for l in range(DEPTH):
        blk = grads[l]["small"]
        blk = blk.at[SM_NORM, 0:D].set(grads[l]["dng"][0])
        if l == 0:
            blk = blk.at[SM_LOSS, 0:128].set(loss_part[0])
        blocks.append(blk)
    tot = small_sum(all_gather([jnp.concatenate(blocks, axis=0)], "gather_small")[0], c["fold"], "small_sum")
    tot = tot.reshape(DEPTH, SM_ROWS, 2 * D)
    loss = tot[0, SM_LOSS, 0]

    def update_small(w, m, v, g, name):
        return adamw(w, m, v, g[None], name)

    u_ng = update_small(norm_g, m_norm_g, v_norm_g, tot[:, SM_NORM, 0:D], "adamw_norm_g")
    u_qg = update_small(q_norm_g, m_q_norm_g, v_q_norm_g, tot[:, 9, 0:HEAD], "adamw_q_norm_g")
    u_kg = update_small(k_norm_g, m_k_norm_g, v_k_norm_g, tot[:, 10, 0:HEAD], "adamw_k_norm_g")
    u_sk = update_small(sinks, m_sinks, v_sinks, tot[:, SM_SINK, 0:16], "adamw_sinks")
    u_gb = update_small(gate_b, m_gate_b, v_gate_b, tot[:, SM_GATE, :], "adamw_gate_b")
    g_conv = lax.dynamic_slice_in_dim(tot[:, SM_CONV:SM_CONV + 3, 0:D], me * SHARD_ROWS, SHARD_ROWS, axis=2)
    u_cw = [o.reshape(DEPTH, 3, SHARD_ROWS) for o in update_small(
        conv_w.reshape(DEPTH * 3, SHARD_ROWS), m_conv_w.reshape(DEPTH * 3, SHARD_ROWS),
        v_conv_w.reshape(DEPTH * 3, SHARD_ROWS), g_conv.reshape(DEPTH * 3, SHARD_ROWS), "adamw_conv_w")]

    order = [u_ng, u_in, u_cw, u_qg, u_kg, u_sk, u_co, u_ao, u_gb, u_out]
    return (loss, dh[None], *[u[0] for u in order], *[u[1] for u in order], *[u[2] for u in order], *[u[3] for u in order])
```

```python
import functools

import jax
import jax.numpy as jnp
from jax import lax
from jax.experimental import pallas as pl
from jax.experimental.pallas import tpu as pltpu

F32 = jnp.float32
BF16 = jnp.bfloat16

N_DEV = 8
DEPTH = 4
D = 1024
N_KV = 4
GROUP = 4
HEAD = 64
BLK = 128
KVW = N_KV * HEAD
IN_COLS = 8704
SHARD_COLS = IN_COLS // N_DEV
SHARD_ROWS = D // N_DEV
C_VC, C_BC, C_CC, C_ZC, C_Q, C_K, C_V, C_ZA, C_GA, C_GB = 0, 1024, 2048, 3072, 4096, 5120, 5376, 5632, 6656, 7680
EPS = 1e-6
NEG_INF = -1e30
SCALE = HEAD ** -0.5

ADAM_LR = 0.001
ADAM_B1 = 0.9
ADAM_B2 = 0.999
ADAM_EPS = 1e-08
ADAM_WD = 0.01
ADAM_STEP = 10

VMEM_LIMIT = 60 * 1024 * 1024
SM_ROWS = 16
SM_GATE, SM_CONV, SM_QG, SM_KG, SM_SINK, SM_NORM, SM_LOSS = 0, 1, 4, 5, 6, 7, 8


def _cparams(sem):
    return pltpu.CompilerParams(dimension_semantics=sem, vmem_limit_bytes=VMEM_LIMIT)


def _dot(a, b):
    return jnp.dot(a, b, preferred_element_type=F32)


def _dot_nt(a, b):
    return lax.dot_general(a, b, (((1,), (1,)), ((), ())), preferred_element_type=F32)


def _dot_tn(a, b):
    return lax.dot_general(a, b, (((0,), (0,)), ((), ())), preferred_element_type=F32)


def _dot2(x, sel):
    hi = x.astype(BF16)
    lo = (x - hi.astype(F32)).astype(BF16)
    return _dot(hi, sel) + _dot(lo, sel)


def _sigmoid(z):
    return 1.0 / (1.0 + jnp.exp(-z))


def _head_mean(t, sel, exp):
    return _dot2(_dot2(t, sel) * (1.0 / HEAD), exp)


def _bf(ref, c0, width):
    return ref[:, c0:c0 + width].astype(F32)


def _selectors():
    c = jnp.arange(D)
    sel_q = (c[:, None] // HEAD == jnp.arange(128)[None, :]).astype(BF16)
    ck = jnp.arange(KVW)
    sel_k = (ck[:, None] // HEAD == jnp.arange(128)[None, :]).astype(BF16)
    src = jnp.arange(KVW)[:, None]
    dst = jnp.arange(KVW)[None, :]
    rep = jnp.stack([((src // HEAD == h) & (src % HEAD == dst % HEAD)).astype(BF16) for h in range(N_KV)])
    fold = (c[:, None] % HEAD == jnp.arange(128)[None, :]).astype(BF16)
    return dict(sel_q=sel_q, exp_q=sel_q.T, sel_k=sel_k, exp_k=sel_k.T, rep=rep, rep_t=jnp.swapaxes(rep, 1, 2), fold=fold)


def inproj_fwd(x, ng, w, name):
    t = x.shape[0]
    tm = min(1024, t)
    cb = 2176
    def body(x_ref, ng_ref, w_ref, u_ref, h_ref, h_scr):
        @pl.when(pl.program_id(1) == 0)
        def _():
            xf = x_ref[...]
            r = lax.rsqrt(jnp.mean(xf * xf, axis=-1, keepdims=True) + EPS)
            hb = (xf * r * ng_ref[...]).astype(BF16)
            h_scr[...] = hb
            h_ref[...] = hb
        u_ref[...] = _dot(h_scr[...], w_ref[...]).astype(BF16)

    return pl.pallas_call(
        body, name=name, grid=(t // tm, IN_COLS // cb),
        in_specs=[pl.BlockSpec((tm, D), lambda i, j: (i, 0)), pl.BlockSpec((1, D), lambda i, j: (0, 0)),
                  pl.BlockSpec((D, cb), lambda i, j: (0, j))],
        out_specs=[pl.BlockSpec((tm, cb), lambda i, j: (i, j)), pl.BlockSpec((tm, D), lambda i, j: (i, 0))],
        out_shape=[jax.ShapeDtypeStruct((t, IN_COLS), BF16), jax.ShapeDtypeStruct((t, D), BF16)],
        scratch_shapes=[pltpu.VMEM((tm, D), BF16)],
        compiler_params=_cparams(("arbitrary", "arbitrary")),
    )(x, ng, w)


def _conv_fwd(u_ref, uvc_prev, ucc_prev, cw_ref, is_first, tm):
    p = _bf(u_ref, C_CC, D) * _bf(u_ref, C_VC, D)
    pprev = ucc_prev[...].astype(F32) * uvc_prev[...].astype(F32)
    pprev = jnp.where(is_first, 0.0, pprev)
    row = lax.broadcasted_iota(jnp.int32, (tm, 1), 0)
    p1 = jnp.where(row == 0, pprev[15:16, :], pltpu.roll(p, 1, 0))
    p2 = jnp.where(row == 0, pprev[14:15, :], jnp.where(row == 1, pprev[15:16, :], pltpu.roll(p, 2, 0)))
    cw = cw_ref[...]
    conv = cw[0:1, :] * p2 + cw[1:2, :] * p1 + cw[2:3, :] * p
    return p, p1, p2, conv


def _attn_inputs(u_ref, ukv_prev, qg_ref, kg_ref, c, tm):
    q = _bf(u_ref, C_Q, D)
    rq = lax.rsqrt(_head_mean(q * q, c["sel_q"][...], c["exp_q"][...]) + EPS)
    qhat = q * rq
    qn = (qhat * qg_ref[...]).astype(BF16)
    kband = jnp.concatenate([ukv_prev[:, 0:KVW].astype(F32), _bf(u_ref, C_K, KVW)], axis=0)
    rk = lax.rsqrt(_head_mean(kband * kband, c["sel_k"][...], c["exp_k"][...]) + EPS)
    khat = kband * rk
    knb = (khat * kg_ref[...]).astype(BF16)
    vband = jnp.concatenate([ukv_prev[:, KVW:2 * KVW], u_ref[:, C_V:C_V + KVW]], axis=0)
    kt = [_dot(knb, c["rep"][h]).astype(BF16) for h in range(N_KV)]
    vt = [_dot(vband, c["rep"][h]).astype(BF16) for h in range(N_KV)]
    return qhat, rq, qn, khat, rk, kt, vt


def _attn_masks(is_first):
    rows = GROUP * BLK
    r = lax.broadcasted_iota(jnp.int32, (rows, 2 * BLK), 0)
    kk = lax.broadcasted_iota(jnp.int32, (rows, 2 * BLK), 1)
    qq = r % BLK
    valid = (kk > qq) & (kk <= qq + BLK)
    valid_first = valid & ((kk >= BLK) | jnp.logical_not(is_first))
    lane_grp = lax.broadcasted_iota(jnp.int32, (BLK, KVW), 1) // HEAD
    row_grp = lax.broadcasted_iota(jnp.int32, (rows, 1), 0) // BLK
    return valid, valid_first, lane_grp, row_grp


def _sink_col(sinks_ref, h, row_grp):
    col = jnp.full(row_grp.shape, sinks_ref[0, GROUP * h], F32)
    for gi in range(1, GROUP):
        col = jnp.where(row_grp == gi, sinks_ref[0, GROUP * h + gi], col)
    return col


def _stack_groups(a256, lane_grp):
    zero = jnp.zeros_like(a256)
    return jnp.concatenate([jnp.where(lane_grp == gi, a256, zero) for gi in range(GROUP)], axis=0)


def _unstack_groups(a4, lane_grp):
    out = jnp.where(lane_grp == 0, a4[0:BLK], 0.0)
    for gi in range(1, GROUP):
        out = out + jnp.where(lane_grp == gi, a4[gi * BLK:(gi + 1) * BLK], 0.0)
    return out


def _softmax_block(qs, kt_b, valid, sink):
    s = _dot_nt(qs, kt_b) * SCALE
    s = jnp.where(valid, s, NEG_INF)
    m = jnp.maximum(jnp.max(s, axis=-1, keepdims=True), sink)
    e = jnp.exp(s - m)
    es = jnp.exp(sink - m)
    inv = 1.0 / (jnp.sum(e, axis=-1, keepdims=True) + es)
    return e * inv, es * inv


def _mixer_specs(t, tm, n_tiles, tile_of):
    nb = tm // BLK
    u_spec = pl.BlockSpec((tm, IN_COLS), lambda g: (tile_of(g), 0))
    ukv_prev = pl.BlockSpec((BLK, 2 * KVW), lambda g: (jnp.maximum(tile_of(g) * nb - 1, 0), C_K // (2 * KVW)))
    uvc_prev = pl.BlockSpec((16, D), lambda g: (jnp.maximum(tile_of(g) * (tm // 16) - 1, 0), C_VC // D))
    ucc_prev = pl.BlockSpec((16, D), lambda g: (jnp.maximum(tile_of(g) * (tm // 16) - 1, 0), C_CC // D))
    return u_spec, ukv_prev, uvc_prev, ucc_prev


def _full(shape):
    n = len(shape)
    return pl.BlockSpec(shape, lambda g: (0,) * n)


def mixer_fwd(x, u, cw, qg, kg, sinks, gb, wco, wao, wout, c, name):
    t = x.shape[0]
    tm = min(256, t)
    n_tiles = t // tm
    nb = tm // BLK
    cn = sorted(c)

    def body(x_ref, u_ref, ukv_prev, uvc_prev, ucc_prev, cw_ref, qg_ref, kg_ref, sinks_ref, gb_ref, wco_ref, wao_ref,
             wout_ref, *rest):
        cref = dict(zip(cn, rest[:len(cn)]))
        xo_ref, ya_ref, yb_ref, o_scr = rest[len(cn):]
        is_first = pl.program_id(0) == 0
        _, _, _, conv = _conv_fwd(u_ref, uvc_prev, ucc_prev, cw_ref, is_first, tm)
        zc = _bf(u_ref, C_ZC, D)
        yc = _bf(u_ref, C_BC, D) * conv * (zc * _sigmoid(zc))
        ya = _dot(yc.astype(BF16), wco_ref[...])

        _, _, qn, _, _, kt, vt = _attn_inputs(u_ref, ukv_prev, qg_ref, kg_ref, cref, tm)
        valid, valid_first, lane_grp, row_grp = _attn_masks(is_first)
        for h in range(N_KV):
            sink = _sink_col(sinks_ref, h, row_grp)
            for b in range(nb):
                qs = _stack_groups(qn[b * BLK:(b + 1) * BLK, h * KVW:(h + 1) * KVW], lane_grp)
                pn, _ = _softmax_block(qs, kt[h][b * BLK:(b + 2) * BLK], valid_first if b == 0 else valid, sink)
                o4 = _dot(pn.astype(BF16), vt[h][b * BLK:(b + 2) * BLK])
                o_scr[b * BLK:(b + 1) * BLK, h * KVW:(h + 1) * KVW] = _unstack_groups(o4, lane_grp)
        za = _bf(u_ref, C_ZA, D)
        ob = o_scr[...] * (za * _sigmoid(za))
        yb = _dot(ob.astype(BF16), wao_ref[...])

        g_a = _sigmoid(_bf(u_ref, C_GA, D) + gb_ref[:, 0:D])
        g_b = _sigmoid(_bf(u_ref, C_GB, D) + gb_ref[:, D:2 * D])
        merged = g_a * ya + g_b * yb
        xo_ref[...] = x_ref[...] + _dot(merged.astype(BF16), wout_ref[...])
        ya_ref[...] = ya.astype(BF16)
        yb_ref[...] = yb.astype(BF16)

    u_spec, ukv_prev, uvc_prev, ucc_prev = _mixer_specs(t, tm, n_tiles, lambda g: g)
    tok = pl.BlockSpec((tm, D), lambda g: (g, 0))
    consts = [c[k] for k in cn]
    return pl.pallas_call(
        body, name=name, grid=(n_tiles,),
        in_specs=[tok, u_spec, ukv_prev, uvc_prev, ucc_prev, _full((8, D)), _full((1, D)), _full((1, KVW)),
                  pl.BlockSpec(memory_space=pltpu.SMEM), _full((1, 2 * D)), _full((D, D)), _full((D, D)), _full((D, D))]
                 + [_full(a.shape) for a in consts],
        out_specs=[tok, tok, tok],
        out_shape=[jax.ShapeDtypeStruct((t, D), F32), jax.ShapeDtypeStruct((t, D), BF16), jax.ShapeDtypeStruct((t, D), BF16)],
        scratch_shapes=[pltpu.VMEM((tm, D), F32)],
        compiler_params=_cparams(("arbitrary",)),
    )(x, u, u, u, u, cw, qg, kg, sinks, gb, wco, wao, wout, *consts)


def mixer_bwd(dout, u, ya, yb, cw, qg, kg, sinks, gb, wco, wao, wout, c, name):
    t = dout.shape[0]
    tm = min(256, t)
    n_tiles = t // tm
    nb = tm // BLK
    kb = tm + BLK
    cn = sorted(c)

    def body(dout_ref, u_ref, ukv_prev, uvc_prev, ucc_prev, ya_ref, yb_ref, cw_ref, qg_ref, kg_ref, sinks_ref, gb_ref,
             wco_ref, wao_ref, wout_ref, *rest):
        cref = dict(zip(cn, rest[:len(cn)]))
        (du_ref, small_ref, merged_ref, yc_ref, ob_ref, dya_ref, dyb_ref,
         o_scr, dq_scr, dk4_scr, dv4_scr, carry_kv, carry_conv) = rest[len(cn):]
        g = pl.program_id(0)
        is_first = g == n_tiles - 1

        @pl.when(g == 0)
        def _():
            carry_kv[...] = jnp.zeros_like(carry_kv)
            carry_conv[...] = jnp.zeros_like(carry_conv)
            small_ref[...] = jnp.zeros_like(small_ref)

        dout = dout_ref[...]
        dout_b = dout.astype(BF16)
        ya_v = ya_ref[...].astype(F32)
        yb_v = yb_ref[...].astype(F32)
        g_a = _sigmoid(_bf(u_ref, C_GA, D) + gb_ref[:, 0:D])
        g_b = _sigmoid(_bf(u_ref, C_GB, D) + gb_ref[:, D:2 * D])
        merged = g_a * ya_v + g_b * yb_v
        dmerged = _dot_nt(dout_b, wout_ref[...])
        merged_ref[...] = merged.astype(BF16)
        dya = dmerged * g_a
        dyb = dmerged * g_b
        dgl_a = dmerged * ya_v * g_a * (1.0 - g_a)
        dgl_b = dmerged * yb_v * g_b * (1.0 - g_b)
        du_ref[:, C_GA:C_GA + D] = dgl_a.astype(BF16)
        du_ref[:, C_GB:C_GB + D] = dgl_b.astype(BF16)
        small_ref[SM_GATE:SM_GATE + 1, 0:D] += jnp.sum(dgl_a, axis=0, keepdims=True)
        small_ref[SM_GATE:SM_GATE + 1, D:2 * D] += jnp.sum(dgl_b, axis=0, keepdims=True)

        p, p1, p2, conv = _conv_fwd(u_ref, uvc_prev, ucc_prev, cw_ref, is_first, tm)
        zc = _bf(u_ref, C_ZC, D)
        bc = _bf(u_ref, C_BC, D)
        sg = _sigmoid(zc)
        sc = zc * sg
        yc = bc * conv * sc
        dya_b = dya.astype(BF16)
        yc_ref[...] = yc.astype(BF16)
        dya_ref[...] = dya_b
        dyc = _dot_nt(dya_b, wco_ref[...])
        du_ref[:, C_BC:C_BC + D] = (dyc * conv * sc).astype(BF16)
        du_ref[:, C_ZC:C_ZC + D] = (dyc * bc * conv * (sg * (1.0 + zc * (1.0 - sg)))).astype(BF16)
        dconv = dyc * bc * sc
        small_ref[SM_CONV + 2:SM_CONV + 3, 0:D] += jnp.sum(dconv * p, axis=0, keepdims=True)
        small_ref[SM_CONV + 1:SM_CONV + 2, 0:D] += jnp.sum(dconv * p1, axis=0, keepdims=True)
        small_ref[SM_CONV:SM_CONV + 1, 0:D] += jnp.sum(dconv * p2, axis=0, keepdims=True)
        row = lax.broadcasted_iota(jnp.int32, (tm, 1), 0)
        nxt = carry_conv[...]
        d1 = jnp.where(row == tm - 1, nxt[0:1, :], pltpu.roll(dconv, tm - 1, 0))
        d2 = jnp.where(row == tm - 1, nxt[1:2, :], jnp.where(row == tm - 2, nxt[0:1, :], pltpu.roll(dconv, tm - 2, 0)))
        carry_conv[...] = dconv[0:8, :]
        cw = cw_ref[...]
        dp = cw[2:3, :] * dconv + cw[1:2, :] * d1 + cw[0:1, :] * d2
        du_ref[:, C_CC:C_CC + D] = (dp * _bf(u_ref, C_VC, D)).astype(BF16)
        du_ref[:, C_VC:C_VC + D] = (dp * _bf(u_ref, C_CC, D)).astype(BF16)

        dyb_b = dyb.astype(BF16)
        dob = _dot_nt(dyb_b, wao_ref[...])
        za = _bf(u_ref, C_ZA, D)
        sga = _sigmoid(za)
        sa = za * sga
        do = dob * sa
        qhat, rq, qn, khat, rk, kt, vt = _attn_inputs(u_ref, ukv_prev, qg_ref, kg_ref, cref, tm)
        valid, valid_first, lane_grp, row_grp = _attn_masks(is_first)
        dk4_scr[...] = jnp.zeros_like(dk4_scr)
        dv4_scr[...] = jnp.zeros_like(dv4_scr)
        lane16 = lax.broadcasted_iota(jnp.int32, (1, 2 * D), 1)
        dsink_row = jnp.zeros((1, 2 * D), F32)
        for h in range(N_KV):
            sink = _sink_col(sinks_ref, h, row_grp)
            dsink_col = jnp.zeros((GROUP * BLK, 1), F32)
            for b in range(nb):
                rows = slice(b * BLK, (b + 1) * BLK)
                band = slice(b * BLK, (b + 2) * BLK)
                cols = slice(h * KVW, (h + 1) * KVW)
                qs = _stack_groups(qn[rows, cols], lane_grp)
                pn, ps = _softmax_block(qs, kt[h][band], valid_first if b == 0 else valid, sink)
                pn_b = pn.astype(BF16)
                o4 = _dot(pn_b, vt[h][band])
                o_scr[rows, cols] = _unstack_groups(o4, lane_grp)
                dos = _stack_groups(do[rows, cols], lane_grp).astype(BF16)
                dpn = _dot_nt(dos, vt[h][band])
                delta = jnp.sum(pn * dpn, axis=-1, keepdims=True)
                ds = (pn * (dpn - delta) * SCALE).astype(BF16)
                dsink_col = dsink_col - ps * delta
                dq_scr[rows, cols] = _unstack_groups(_dot(ds, kt[h][band]), lane_grp)
                dk4_scr[h, band, :] += _dot_tn(ds, qs)
                dv4_scr[h, band, :] += _dot_tn(pn_b, dos)
            for gi in range(GROUP):
                tot = jnp.sum(dsink_col[gi * BLK:(gi + 1) * BLK, :], axis=0, keepdims=True)
                dsink_row = dsink_row + jnp.where(lane16 == GROUP * h + gi, tot, 0.0)
        small_ref[SM_SINK:SM_SINK + 1, :] += dsink_row

        o = o_scr[...]
        ob_ref[...] = (o * sa).astype(BF16)
        dyb_ref[...] = dyb_b
        du_ref[:, C_ZA:C_ZA + D] = (dob * o * (sga * (1.0 + za * (1.0 - sga)))).astype(BF16)

        dqn = dq_scr[...]
        small_ref[SM_QG:SM_QG + 1, 0:D] += jnp.sum(dqn * qhat, axis=0, keepdims=True)
        dqh = dqn * qg_ref[...]
        dq = rq * (dqh - qhat * _head_mean(dqh * qhat, cref["sel_q"][...], cref["exp_q"][...]))
        du_ref[:, C_Q:C_Q + D] = dq.astype(BF16)

        dkn_band = jnp.zeros((kb, KVW), F32)
        dv_band = jnp.zeros((kb, KVW), F32)
        for h in range(N_KV):
            dkn_band = dkn_band + _dot2(dk4_scr[h], cref["rep_t"][h])
            dv_band = dv_band + _dot2(dv4_scr[h], cref["rep_t"][h])
        carried = carry_kv[...]
        pad = jnp.zeros((tm - BLK, KVW), F32)
        if nb > 1:
            dkn = dkn_band[BLK:, :] + jnp.concatenate([pad, carried[:, 0:KVW]], axis=0)
            dv = dv_band[BLK:, :] + jnp.concatenate([pad, carried[:, KVW:2 * KVW]], axis=0)
        else:
            dkn = dkn_band[BLK:, :] + carried[:, 0:KVW]
            dv = dv_band[BLK:, :] + carried[:, KVW:2 * KVW]
        carry_kv[:, 0:KVW] = dkn_band[0:BLK, :]
        carry_kv[:, KVW:2 * KVW] = dv_band[0:BLK, :]
        khat_t = khat[BLK:, :]
        small_ref[SM_KG:SM_KG + 1, 0:KVW] += jnp.sum(dkn * khat_t, axis=0, keepdims=True)
        dkh = dkn * kg_ref[...]
        dk = rk[BLK:, :] * (dkh - khat_t * _head_mean(dkh * khat_t, cref["sel_k"][...], cref["exp_k"][...]))
        du_ref[:, C_K:C_K + KVW] = dk.astype(BF16)
        du_ref[:, C_V:C_V + KVW] = dv.astype(BF16)

    rev = lambda g: n_tiles - 1 - g
    u_spec, ukv_prev, uvc_prev, ucc_prev = _mixer_specs(t, tm, n_tiles, rev)
    tok = pl.BlockSpec((tm, D), lambda g: (rev(g), 0))
    consts = [c[k] for k in cn]
    wspec = _full((D, D))
    return pl.pallas_call(
        body, name=name, grid=(n_tiles,),
        in_specs=[tok, u_spec, ukv_prev, uvc_prev, ucc_prev, tok, tok, _full((8, D)), _full((1, D)), _full((1, KVW)),
                  pl.BlockSpec(memory_space=pltpu.SMEM), _full((1, 2 * D)), wspec, wspec, wspec]
                 + [_full(a.shape) for a in consts],
        out_specs=[pl.BlockSpec((tm, IN_COLS), lambda g: (rev(g), 0)), _full((SM_ROWS, 2 * D))] + [tok] * 5,
        out_shape=[jax.ShapeDtypeStruct((t, IN_COLS), BF16), jax.ShapeDtypeStruct((SM_ROWS, 2 * D), F32)]
                  + [jax.ShapeDtypeStruct((t, D), BF16)] * 5,
        scratch_shapes=[pltpu.VMEM((tm, D), F32), pltpu.VMEM((tm, D), F32),
                        pltpu.VMEM((N_KV, kb, KVW), F32), pltpu.VMEM((N_KV, kb, KVW), F32),
                        pltpu.VMEM((BLK, 2 * KVW), F32), pltpu.VMEM((8, D), F32)],
        compiler_params=_cparams(("arbitrary",)),
    )(dout, u, u, u, u, ya, yb, cw, qg, kg, sinks, gb, wco, wao, wout, *consts)


def matmul_tn(a, b, name):
    t, n = b.shape
    tk = min(1024, t)
    cb = 2176 if n == IN_COLS else n
    nk = t // tk

    def body(a_ref, b_ref, o_ref, acc):
        k = pl.program_id(1)
        prod = _dot_tn(a_ref[...].astype(BF16), b_ref[...].astype(BF16))

        @pl.when(k == 0)
        def _():
            acc[...] = prod

        @pl.when(k > 0)
        def _():
            acc[...] += prod

        @pl.when(k == nk - 1)
        def _():
            o_ref[...] = acc[...].astype(BF16)

    return pl.pallas_call(
        body, name=name, grid=(n // cb, nk),
        in_specs=[pl.BlockSpec((tk, D), lambda j, k: (k, 0)), pl.BlockSpec((tk, cb), lambda j, k: (k, j))],
        out_specs=pl.BlockSpec((D, cb), lambda j, k: (0, j)),
        out_shape=jax.ShapeDtypeStruct((D, n), BF16),
        scratch_shapes=[pltpu.VMEM((D, cb), F32)],
        compiler_params=_cparams(("arbitrary", "arbitrary")),
    )(a, b)


def inproj_bwd_x(du, w, x, ng, dout, name):
    t = x.shape[0]
    tm = min(512, t)
    kc = 2176
    nk = IN_COLS // kc

    def body(du_ref, w_ref, x_ref, ng_ref, dout_ref, dx_ref, dng_ref, acc):
        i = pl.program_id(0)
        k = pl.program_id(1)
        prod = _dot_nt(du_ref[...], w_ref[...])

        @pl.when(k == 0)
        def _():
            acc[...] = prod

        @pl.when(k > 0)
        def _():
            acc[...] += prod

        @pl.when((i == 0) & (k == 0))
        def _():
            dng_ref[...] = jnp.zeros_like(dng_ref)

        @pl.when(k == nk - 1)
        def _():
            dh = acc[...]
            xf = x_ref[...]
            r = lax.rsqrt(jnp.mean(xf * xf, axis=-1, keepdims=True) + EPS)
            xhat = xf * r
            dng_ref[0:1, :] += jnp.sum(dh * xhat, axis=0, keepdims=True)
            dxh = dh * ng_ref[...]
            dx_ref[...] = dout_ref[...] + r * (dxh - xhat * jnp.mean(dxh * xhat, axis=-1, keepdims=True))

    tok = pl.BlockSpec((tm, D), lambda i, k: (i, 0))
    return pl.pallas_call(
        body, name=name, grid=(t // tm, nk),
        in_specs=[pl.BlockSpec((tm, kc), lambda i, k: (i, k)), pl.BlockSpec((D, kc), lambda i, k: (0, k)), tok,
                  pl.BlockSpec((1, D), lambda i, k: (0, 0)), tok],
        out_specs=[tok, pl.BlockSpec((8, D), lambda i, k: (0, 0))],
        out_shape=[jax.ShapeDtypeStruct((t, D), F32), jax.ShapeDtypeStruct((8, D), F32)],
        scratch_shapes=[pltpu.VMEM((tm, D), F32)],
        compiler_params=_cparams(("arbitrary", "arbitrary")),
    )(du, w, x, ng, dout)


def loss_head(y, target, name):
    t = y.shape[0]
    tm = min(1024, t)

    def body(y_ref, t_ref, dy_ref, loss_ref):
        @pl.when(pl.program_id(0) == 0)
        def _():
            loss_ref[...] = jnp.zeros_like(loss_ref)
        err = y_ref[...] - t_ref[...]
        dy_ref[...] = err * (1.0 / D)
        part = jnp.sum(jnp.sum(err * err, axis=-1, keepdims=True) * (1.0 / D), axis=0, keepdims=True)
        loss_ref[...] += 0.5 * part

    tok = pl.BlockSpec((tm, D), lambda i: (i, 0))
    return pl.pallas_call(
        body, name=name, grid=(t // tm,), in_specs=[tok, tok],
        out_specs=[tok, pl.BlockSpec((8, 128), lambda i: (0, 0))],
        out_shape=[jax.ShapeDtypeStruct((t, D), F32), jax.ShapeDtypeStruct((8, 128), F32)],
        compiler_params=_cparams(("arbitrary",)),
    )(y, target)


def layer_operands(l, norm_g, conv_w_full, q_norm_g, k_norm_g, sinks, gate_b, w_in_b, wco_b, wao_b, wout_b):
    return dict(
        ng=norm_g[l][None, :], cw=jnp.pad(conv_w_full[l], ((0, 5), (0, 0))),
        qg=jnp.tile(q_norm_g[l], D // HEAD)[None, :], kg=jnp.tile(k_norm_g[l], N_KV)[None, :],
        sinks=sinks[l][None, :], gb=gate_b[l][None, :],
        w_in=w_in_b[l], wco=wco_b[l], wao=wao_b[l], wout=wout_b[l])


def layer_fwd(x, lw, c, l):
    u, h = inproj_fwd(x, lw["ng"], lw["w_in"], f"inproj_fwd_{l}")
    xo, ya, yb = mixer_fwd(x, u, lw["cw"], lw["qg"], lw["kg"], lw["sinks"], lw["gb"], lw["wco"], lw["wao"], lw["wout"], c,
                           f"mixer_fwd_{l}")
    return xo, (x, u, h, ya, yb)


def layer_bwd(dout, saved, lw, c, l, send_off):
    x, u, h, ya, yb = saved
    du, small, merged, yc, ob, dya, dyb = mixer_bwd(dout, u, ya, yb, lw["cw"], lw["qg"], lw["kg"], lw["sinks"], lw["gb"],
                                                    lw["wco"], lw["wao"], lw["wout"], c, f"mixer_bwd_{l}")
    grads = dict(
        wout=matmul_tn(merged, dout, f"dw_out_{l}"), wco=matmul_tn(yc, dya, f"dw_conv_out_{l}"),
        wao=matmul_tn(ob, dyb, f"dw_attn_out_{l}"), w_in=matmul_tn(h, du, f"dw_in_{l}"), small=small)
    token = send_off(grads)
    dx, grads["dng"] = inproj_bwd_x(du, lw["w_in"], x, lw["ng"] + token[0:1, 0:1], dout, f"inproj_bwd_{l}")
    return dx, grads


MESH = pl.DeviceIdType.MESH
ANY = pl.BlockSpec(memory_space=pl.ANY)


def _place():
    return lax.axis_index("x"), lax.axis_index("y"), lax.axis_index("c")


def all_gather(arrs, name):
    n = len(arrs)

    def body(*refs):
        ins, outs = refs[:n], refs[n:2 * n]
        send_sems, recv_sems, local_sems = refs[2 * n:]
        x, y, c = _place()
        me, sibling = (x, y, c), (x, y, 1 - c)
        chips = [(1 - x, y), (x, 1 - y), (1 - x, 1 - y)]

        def slot(a, block):
            px, py, pc = block
            return outs[a].at[4 * px + 2 * py + pc]

        def copy(a, k, block, to, src=None):
            return pltpu.make_async_remote_copy(
                src_ref=slot(a, block) if src is None else src, dst_ref=slot(a, block),
                send_sem=send_sems.at[a, k], recv_sem=recv_sems.at[a, k], device_id=to, device_id_type=MESH)

        mine = [pltpu.make_async_copy(ins[a], slot(a, me), local_sems.at[a]) for a in range(n)]
        for cp in mine:
            cp.start()
        first = []
        for a in range(n):
            first.append(copy(a, 0, me, sibling, src=ins[a]))
            first += [copy(a, 1 + j, me, (*chip, c), src=ins[a]) for j, chip in enumerate(chips)]
        for cp in first:
            cp.start()
        passed = []
        for j, chip in enumerate(chips):
            for a in range(n):
                copy(a, 1 + j, (*chip, c), me).wait_recv()
                passed.append(copy(a, 4 + j, (*chip, c), sibling))
                passed[-1].start()
        for a in range(n):
            copy(a, 0, sibling, me).wait_recv()
            for j, chip in enumerate(chips):
                copy(a, 4 + j, (*chip, 1 - c), me).wait_recv()
        for cp in first + passed:
            cp.wait_send()
        for cp in mine:
            cp.wait()

    return pl.pallas_call(
        body, name=name, in_specs=[ANY] * n, out_specs=[ANY] * n,
        out_shape=[jax.ShapeDtypeStruct((N_DEV,) + a.shape, a.dtype) for a in arrs],
        scratch_shapes=[pltpu.SemaphoreType.DMA((n, 7)), pltpu.SemaphoreType.DMA((n, 7)), pltpu.SemaphoreType.DMA((n,))],
    )(*arrs)


HBM_SPEC = pl.BlockSpec(memory_space=pltpu.HBM)
SEM_SPEC = pl.BlockSpec(memory_space=pltpu.SEMAPHORE)
EFFECT = pltpu.SideEffectType.DATAFLOW_SIDE_EFFECTING


def _exchange_copies(ins, lands, send_sems, recv_sems, scatter, arriving):
    x, y, c = _place()
    me = 4 * x + 2 * y + c
    out = []
    for a in range(len(ins)):
        for k in range(1, N_DEV):
            px = 1 - x if k & 4 else x
            py = 1 - y if k & 2 else y
            pc = 1 - c if k & 1 else c
            theirs = 4 * px + 2 * py + pc
            out.append(pltpu.make_async_remote_copy(
                src_ref=ins[a].at[theirs] if scatter else ins[a], dst_ref=lands[a].at[theirs if arriving else me],
                send_sem=send_sems.at[a * (N_DEV - 1) + k - 1], recv_sem=recv_sems.at[a * (N_DEV - 1) + k - 1],
                device_id=(px, py, pc), device_id_type=MESH))
    return out


def exchange_start(arrs, scatter, name):
    n = len(arrs)
    lands = [lax.empty(a.shape if scatter else (N_DEV,) + a.shape, a.dtype) for a in arrs]

    def body(*refs):
        ins, lz = refs[:n], refs[n:2 * n]
        send_sems, recv_sems = refs[2 * n], refs[2 * n + 1]
        token = refs[-1]
        for cp in _exchange_copies(ins, lz, send_sems, recv_sems, scatter, False):
            cp.start()
        token[...] = jnp.zeros_like(token)

    sems = pltpu.SemaphoreType.DMA((n * (N_DEV - 1),))
    res = pl.pallas_call(
        body, name=name,
        out_shape=(sems, sems, *[pltpu.HBM(a.shape, a.dtype) for a in arrs], *[pltpu.HBM(a.shape, a.dtype) for a in lands],
                   jax.ShapeDtypeStruct((8, 128), F32)),
        in_specs=[HBM_SPEC] * (2 * n),
        out_specs=(SEM_SPEC, SEM_SPEC, *[HBM_SPEC] * (2 * n), pl.BlockSpec(memory_space=pltpu.VMEM)),
        input_output_aliases={i: 2 + i for i in range(2 * n)},
        compiler_params=pltpu.CompilerParams(has_side_effects=EFFECT),
    )(*[pltpu.with_memory_space_constraint(a, pltpu.HBM) for a in arrs],
      *[pltpu.with_memory_space_constraint(a, pltpu.HBM) for a in lands])
    return dict(send=res[0], recv=res[1], srcs=res[2:2 + n], lands=res[2 + n:2 + 2 * n], token=res[-1], scatter=scatter)


def exchange_wait(state, after, name):
    n = len(state["srcs"])
    scatter = state["scatter"]

    def body(*refs):
        ins, lz = refs[:n], refs[n:2 * n]
        send_sems, recv_sems = refs[2 * n], refs[2 * n + 1]
        for cp in _exchange_copies(ins, lz, send_sems, recv_sems, scatter, True):
            cp.wait_send()
            cp.wait_recv()

    both = list(state["srcs"]) + list(state["lands"])
    res = pl.pallas_call(
        body, name=name, out_shape=tuple(pltpu.HBM(a.shape, a.dtype) for a in both),
        in_specs=[HBM_SPEC] * (2 * n) + [SEM_SPEC, SEM_SPEC, ANY], out_specs=tuple([HBM_SPEC] * (2 * n)),
        input_output_aliases={i: i for i in range(2 * n)},
        compiler_params=pltpu.CompilerParams(has_side_effects=EFFECT),
    )(*both, state["send"], state["recv"], after)
    return res[:n], res[n:]


def adamw(w, m, v, parts, name):
    r, cdim = w.shape
    n_parts = parts.shape[0]
    rb = 256 if r % 256 == 0 else r

    def body(w_ref, m_ref, v_ref, p_ref, g_ref, d_ref, mo_ref, vo_ref):
        g = p_ref[0].astype(F32)
        for i in range(1, n_parts):
            g = g + p_ref[i].astype(F32)
        m_new = ADAM_B1 * m_ref[...] + (1.0 - ADAM_B1) * g
        v_new = ADAM_B2 * v_ref[...] + (1.0 - ADAM_B2) * (g * g)
        m_hat = m_new / (1.0 - ADAM_B1 ** ADAM_STEP)
        v_hat = v_new / (1.0 - ADAM_B2 ** ADAM_STEP)
        g_ref[...] = g
        d_ref[...] = -ADAM_LR * (m_hat / (jnp.sqrt(v_hat) + ADAM_EPS) + ADAM_WD * w_ref[...])
        mo_ref[...] = m_new
        vo_ref[...] = v_new

    blk = pl.BlockSpec((rb, cdim), lambda i: (i, 0))
    return pl.pallas_call(
        body, name=name, grid=(r // rb,),
        in_specs=[blk, blk, blk, pl.BlockSpec((n_parts, rb, cdim), lambda i: (0, i, 0))],
        out_specs=[blk] * 4, out_shape=[jax.ShapeDtypeStruct((r, cdim), F32)] * 4,
        compiler_params=_cparams(("arbitrary",)),
    )(w, m, v, parts)


def adamw_layers(w, m, v, parts, name):
    nl, r, cdim = w.shape
    n_parts = parts[0].shape[0]
    rb = 256 if r % 256 == 0 else r
    nblk = r // rb

    def body(w_ref, m_ref, v_ref, *rest):
        p_refs = rest[:nl]
        g_ref, d_ref, mo_ref, vo_ref = rest[nl:]
        for k in range(nl):
            @pl.when(pl.program_id(0) == k)
            def _(k=k):
                g = p_refs[k][0].astype(F32)
                for i in range(1, n_parts):
                    g = g + p_refs[k][i].astype(F32)
                m_new = ADAM_B1 * m_ref[...] + (1.0 - ADAM_B1) * g
                v_new = ADAM_B2 * v_ref[...] + (1.0 - ADAM_B2) * (g * g)
                m_hat = m_new / (1.0 - ADAM_B1 ** ADAM_STEP)
                v_hat = v_new / (1.0 - ADAM_B2 ** ADAM_STEP)
                g_ref[...] = g
                d_ref[...] = -ADAM_LR * (m_hat / (jnp.sqrt(v_hat) + ADAM_EPS) + ADAM_WD * w_ref[...])
                mo_ref[...] = m_new
                vo_ref[...] = v_new

    blk = pl.BlockSpec((None, rb, cdim), lambda l, i: (l, i, 0))

    def part_spec(k):
        return pl.BlockSpec((n_parts, rb, cdim),
                            lambda l, i: (0, jnp.where(l < k, 0, jnp.where(l == k, i, nblk - 1)), 0))

    return pl.pallas_call(
        body, name=name, grid=(nl, nblk),
        in_specs=[blk, blk, blk] + [part_spec(k) for k in range(nl)],
        out_specs=[blk] * 4, out_shape=[jax.ShapeDtypeStruct((nl, r, cdim), F32)] * 4,
        compiler_params=_cparams(("arbitrary", "arbitrary")),
    )(w, m, v, *parts)


def small_sum(parts, fold, name):
    rows = parts.shape[1]

    def dot3(xv, sel):
        out = jnp.zeros((xv.shape[0], sel.shape[1]), F32)
        for _ in range(3):
            hi = xv.astype(BF16)
            out = out + _dot(hi, sel)
            xv = xv - hi.astype(F32)
        return out

    def body(p_ref, fold_ref, o_ref):
        tot = p_ref[0]
        for i in range(1, N_DEV):
            tot = tot + p_ref[i]
        o_ref[...] = tot
        for l in range(rows // SM_ROWS):
            blk = tot[l * SM_ROWS:l * SM_ROWS + 8, 0:D]
            folded = dot3(blk, fold_ref[...])
            o_ref[l * SM_ROWS + 9:l * SM_ROWS + 10, 0:128] = folded[SM_QG:SM_QG + 1, :]
            o_ref[l * SM_ROWS + 10:l * SM_ROWS + 11, 0:128] = folded[SM_KG:SM_KG + 1, :]

    return pl.pallas_call(
        body, name=name, out_shape=jax.ShapeDtypeStruct((rows, 2 * D), F32),
        compiler_params=_cparams(None),
    )(parts, fold)


def kernel(x, norm_g, w_in, conv_w, q_norm_g, k_norm_g, sinks, w_conv_out, w_attn_out, gate_b, w_out, loss_target, m_norm_g, m_w_in, m_conv_w, m_q_norm_g, m_k_norm_g, m_sinks, m_w_conv_out, m_w_attn_out, m_gate_b, m_w_out, v_norm_g, v_w_in, v_conv_w, v_q_norm_g, v_k_norm_g, v_sinks, v_w_conv_out, v_w_attn_out, v_gate_b, v_w_out):
    c = _selectors()
    me = 4 * lax.axis_index("x") + 2 * lax.axis_index("y") + lax.axis_index("c")

    own = lambda land, mine: lax.dynamic_update_index_in_dim(land, mine, me, 0)

    shards, gathers = [], []
    for l in range(DEPTH):
        arrs = [w_in[l].astype(BF16), w_conv_out[l].astype(BF16), w_attn_out[l].astype(BF16), w_out[l].astype(BF16)]
        if l == 0:
            arrs.append(conv_w)
        shards.append(arrs)
        gathers.append(exchange_start(arrs, False, f"gather_start_{l}"))
    started = gathers[0]["token"]
    for l in range(1, DEPTH):
        started = started + gathers[l]["token"]

    h = x[0]
    saved, lws = [], []
    conv_full = None
    for l in range(DEPTH):
        mine, lands = exchange_wait(gathers[l], started if l == 0 else h, f"gather_wait_{l}")
        lands = [own(land, src) for land, src in zip(lands, mine)]
        if l == 0:
            conv_full = jnp.transpose(lands[4], (1, 2, 0, 3)).reshape(DEPTH, 3, D)
        w_in_b = jnp.transpose(lands[0], (1, 0, 2)).reshape(D, IN_COLS)
        lws.append(layer_operands(l, norm_g, conv_full, q_norm_g, k_norm_g, sinks, gate_b,
                                  {l: w_in_b}, {l: lands[1].reshape(D, D)}, {l: lands[2].reshape(D, D)},
                                  {l: lands[3].reshape(D, D)}))
        h, s = layer_fwd(h, lws[l], c, l)
        saved.append(s)
    dh, loss_part = loss_head(h, loss_target[0], "loss_head")

    grads, scatters, sources = [None] * DEPTH, [None] * DEPTH, [None] * DEPTH
    for l in reversed(range(DEPTH)):
        def send_off(g, l=l):
            d_in = jnp.transpose(g["w_in"].reshape(D, N_DEV, SHARD_COLS), (1, 0, 2))
            sources[l] = [d_in] + [g[k].reshape(N_DEV, SHARD_ROWS, D) for k in ("wco", "wao", "wout")]
            scatters[l] = exchange_start(sources[l], True, f"scatter_start_{l}")
            return scatters[l]["token"]
        dh, grads[l] = layer_bwd(dh, saved[l], lws[l], c, l, send_off)
    parts = [None] * DEPTH
    for l in reversed(range(DEPTH)):
        mine, lands = exchange_wait(scatters[l], dh, f"scatter_wait_{l}")
        parts[l] = [own(land, lax.dynamic_index_in_dim(src, me, 0, keepdims=False)) for land, src in zip(lands, mine)]

    u_in = adamw_layers(w_in, m_w_in, v_w_in, [parts[l][0] for l in range(DEPTH)], "adamw_w_in")
    u_co = adamw_layers(w_conv_out, m_w_conv_out, v_w_conv_out, [parts[l][1] for l in range(DEPTH)], "adamw_w_conv_out")
    u_ao = adamw_layers(w_attn_out, m_w_attn_out, v_w_attn_out, [parts[l][2] for l in range(DEPTH)], "adamw_w_attn_out")
    u_out = adamw_layers(w_out, m_w_out, v_w_out, [parts[l][3] for l in range(DEPTH)], "adamw_w_out")

    blocks = []
    for l in range(DEPTH):
        blk = grads[l]["small"]
        blk = blk.at[SM_NORM, 0:D].set(grads[l]["dng"][0])
        if l == 0:
            blk = blk.at[SM_LOSS, 0:128].set(loss_part[0])
        blocks.append(blk)
    tot = small_sum(all_gather([jnp.concatenate(blocks, axis=0)], "gather_small")[0], c["fold"], "small_sum")
    tot = tot.reshape(DEPTH, SM_ROWS, 2 * D)
    loss = tot[0, SM_LOSS, 0]

    def update_small(w, m, v, g, name):
        return adamw(w, m, v, g[None], name)

    u_ng = update_small(norm_g, m_norm_g, v_norm_g, tot[:, SM_NORM, 0:D], "adamw_norm_g")
    u_qg = update_small(q_norm_g, m_q_norm_g, v_q_norm_g, tot[:, 9, 0:HEAD], "adamw_q_norm_g")
    u_kg = update_small(k_norm_g, m_k_norm_g, v_k_norm_g, tot[:, 10, 0:HEAD], "adamw_k_norm_g")
    u_sk = update_small(sinks, m_sinks, v_sinks, tot[:, SM_SINK, 0:16], "adamw_sinks")
    u_gb = update_small(gate_b, m_gate_b, v_gate_b, tot[:, SM_GATE, :], "adamw_gate_b")
    g_conv = lax.dynamic_slice_in_dim(tot[:, SM_CONV:SM_CONV + 3, 0:D], me * SHARD_ROWS, SHARD_ROWS, axis=2)
    u_cw = [o.reshape(DEPTH, 3, SHARD_ROWS) for o in update_small(
        conv_w.reshape(DEPTH * 3, SHARD_ROWS), m_conv_w.reshape(DEPTH * 3, SHARD_ROWS),
        v_conv_w.reshape(DEPTH * 3, SHARD_ROWS), g_conv.reshape(DEPTH * 3, SHARD_ROWS), "adamw_conv_w")]

    order = [u_ng, u_in, u_cw, u_qg, u_kg, u_sk, u_co, u_ao, u_gb, u_out]
    return (loss, dh[None], *[u[0] for u in order], *[u[1] for u in order], *[u[2] for u in order], *[u[3] for u in order])
```

```python
import functools

import jax
import jax.numpy as jnp
from jax import lax
from jax.experimental import pallas as pl
from jax.experimental.pallas import tpu as pltpu

F32 = jnp.float32
BF16 = jnp.bfloat16

N_DEV = 8
DEPTH = 4
D = 1024
N_KV = 4
GROUP = 4
HEAD = 64
BLK = 128
KVW = N_KV * HEAD
IN_COLS = 8704
SHARD_COLS = IN_COLS // N_DEV
SHARD_ROWS = D // N_DEV
C_VC, C_BC, C_CC, C_ZC, C_Q, C_K, C_V, C_ZA, C_GA, C_GB = 0, 1024, 2048, 3072, 4096, 5120, 5376, 5632, 6656, 7680
EPS = 1e-6
NEG_INF = -1e30
SCALE = HEAD ** -0.5

ADAM_LR = 0.001
ADAM_B1 = 0.9
ADAM_B2 = 0.999
ADAM_EPS = 1e-08
ADAM_WD = 0.01
ADAM_STEP = 10

VMEM_LIMIT = 60 * 1024 * 1024
SM_ROWS = 16
SM_GATE, SM_CONV, SM_QG, SM_KG, SM_SINK, SM_NORM, SM_LOSS = 0, 1, 4, 5, 6, 7, 8


def _cparams(sem):
    return pltpu.CompilerParams(dimension_semantics=sem, vmem_limit_bytes=VMEM_LIMIT)


def _dot(a, b):
    return jnp.dot(a, b, preferred_element_type=F32)


def _dot_nt(a, b):
    return lax.dot_general(a, b, (((1,), (1,)), ((), ())), preferred_element_type=F32)


def _dot_tn(a, b):
    return lax.dot_general(a, b, (((0,), (0,)), ((), ())), preferred_element_type=F32)


def _dot2(x, sel):
    hi = x.astype(BF16)
    lo = (x - hi.astype(F32)).astype(BF16)
    return _dot(hi, sel) + _dot(lo, sel)


def _sigmoid(z):
    return 1.0 / (1.0 + jnp.exp(-z))


def _head_mean(t, sel, exp):
    return _dot2(_dot2(t, sel) * (1.0 / HEAD), exp)


def _bf(ref, c0, width):
    return ref[:, c0:c0 + width].astype(F32)


def _selectors():
    c = jnp.arange(D)
    sel_q = (c[:, None] // HEAD == jnp.arange(128)[None, :]).astype(BF16)
    ck = jnp.arange(KVW)
    sel_k = (ck[:, None] // HEAD == jnp.arange(128)[None, :]).astype(BF16)
    src = jnp.arange(KVW)[:, None]
    dst = jnp.arange(KVW)[None, :]
    rep = jnp.stack([((src // HEAD == h) & (src % HEAD == dst % HEAD)).astype(BF16) for h in range(N_KV)])
    fold = (c[:, None] % HEAD == jnp.arange(128)[None, :]).astype(BF16)
    return dict(sel_q=sel_q, exp_q=sel_q.T, sel_k=sel_k, exp_k=sel_k.T, rep=rep, rep_t=jnp.swapaxes(rep, 1, 2), fold=fold)


def inproj_fwd(x, ng, w, name):
    t = x.shape[0]
    tm = min(1024, t)
    cb = 2176
    def body(x_ref, ng_ref, w_ref, u_ref, h_ref, h_scr):
        @pl.when(pl.program_id(1) == 0)
        def _():
            xf = x_ref[...]
            r = lax.rsqrt(jnp.mean(xf * xf, axis=-1, keepdims=True) + EPS)
            hb = (xf * r * ng_ref[...]).astype(BF16)
            h_scr[...] = hb
            h_ref[...] = hb
        u_ref[...] = _dot_nt(h_scr[...], w_ref[...]).astype(BF16)

    return pl.pallas_call(
        body, name=name, grid=(t // tm, IN_COLS // cb),
        in_specs=[pl.BlockSpec((tm, D), lambda i, j: (i, 0)), pl.BlockSpec((1, D), lambda i, j: (0, 0)),
                  pl.BlockSpec((cb, D), lambda i, j: (j, 0))],
        out_specs=[pl.BlockSpec((tm, cb), lambda i, j: (i, j)), pl.BlockSpec((tm, D), lambda i, j: (i, 0))],
        out_shape=[jax.ShapeDtypeStruct((t, IN_COLS), BF16), jax.ShapeDtypeStruct((t, D), BF16)],
        scratch_shapes=[pltpu.VMEM((tm, D), BF16)],
        compiler_params=_cparams(("arbitrary", "arbitrary")),
    )(x, ng, w)


def _conv_fwd(u_ref, uvc_prev, ucc_prev, cw_ref, is_first, tm):
    p = _bf(u_ref, C_CC, D) * _bf(u_ref, C_VC, D)
    pprev = ucc_prev[...].astype(F32) * uvc_prev[...].astype(F32)
    pprev = jnp.where(is_first, 0.0, pprev)
    row = lax.broadcasted_iota(jnp.int32, (tm, 1), 0)
    p1 = jnp.where(row == 0, pprev[15:16, :], pltpu.roll(p, 1, 0))
    p2 = jnp.where(row == 0, pprev[14:15, :], jnp.where(row == 1, pprev[15:16, :], pltpu.roll(p, 2, 0)))
    cw = cw_ref[...]
    conv = cw[0:1, :] * p2 + cw[1:2, :] * p1 + cw[2:3, :] * p
    return p, p1, p2, conv


def _attn_inputs(u_ref, ukv_prev, qg_ref, kg_ref, c, tm):
    q = _bf(u_ref, C_Q, D)
    rq = lax.rsqrt(_head_mean(q * q, c["sel_q"][...], c["exp_q"][...]) + EPS)
    qhat = q * rq
    qn = (qhat * qg_ref[...]).astype(BF16)
    kband = jnp.concatenate([ukv_prev[:, 0:KVW].astype(F32), _bf(u_ref, C_K, KVW)], axis=0)
    rk = lax.rsqrt(_head_mean(kband * kband, c["sel_k"][...], c["exp_k"][...]) + EPS)
    khat = kband * rk
    knb = (khat * kg_ref[...]).astype(BF16)
    vband = jnp.concatenate([ukv_prev[:, KVW:2 * KVW], u_ref[:, C_V:C_V + KVW]], axis=0)
    kt = [_dot(knb, c["rep"][h]).astype(BF16) for h in range(N_KV)]
    vt = [_dot(vband, c["rep"][h]).astype(BF16) for h in range(N_KV)]
    return qhat, rq, qn, khat, rk, kt, vt


def _attn_masks(is_first):
    rows = GROUP * BLK
    r = lax.broadcasted_iota(jnp.int32, (rows, 2 * BLK), 0)
    kk = lax.broadcasted_iota(jnp.int32, (rows, 2 * BLK), 1)
    qq = r % BLK
    valid = (kk > qq) & (kk <= qq + BLK)
    valid_first = valid & ((kk >= BLK) | jnp.logical_not(is_first))
    lane_grp = lax.broadcasted_iota(jnp.int32, (BLK, KVW), 1) // HEAD
    row_grp = lax.broadcasted_iota(jnp.int32, (rows, 1), 0) // BLK
    return valid, valid_first, lane_grp, row_grp


def _sink_col(sinks_ref, h, row_grp):
    col = jnp.full(row_grp.shape, sinks_ref[0, GROUP * h], F32)
    for gi in range(1, GROUP):
        col = jnp.where(row_grp == gi, sinks_ref[0, GROUP * h + gi], col)
    return col


def _stack_groups(a256, lane_grp):
    zero = jnp.zeros_like(a256)
    return jnp.concatenate([jnp.where(lane_grp == gi, a256, zero) for gi in range(GROUP)], axis=0)


def _unstack_groups(a4, lane_grp):
    out = jnp.where(lane_grp == 0, a4[0:BLK], 0.0)
    for gi in range(1, GROUP):
        out = out + jnp.where(lane_grp == gi, a4[gi * BLK:(gi + 1) * BLK], 0.0)
    return out


def _softmax_block(qs, kt_b, valid, sink):
    s = _dot_nt(qs, kt_b) * SCALE
    s = jnp.where(valid, s, NEG_INF)
    m = jnp.maximum(jnp.max(s, axis=-1, keepdims=True), sink)
    e = jnp.exp(s - m)
    es = jnp.exp(sink - m)
    inv = 1.0 / (jnp.sum(e, axis=-1, keepdims=True) + es)
    return e * inv, es * inv


def _mixer_specs(t, tm, n_tiles, tile_of):
    nb = tm // BLK
    u_spec = pl.BlockSpec((tm, IN_COLS), lambda g: (tile_of(g), 0))
    ukv_prev = pl.BlockSpec((BLK, 2 * KVW), lambda g: (jnp.maximum(tile_of(g) * nb - 1, 0), C_K // (2 * KVW)))
    uvc_prev = pl.BlockSpec((16, D), lambda g: (jnp.maximum(tile_of(g) * (tm // 16) - 1, 0), C_VC // D))
    ucc_prev = pl.BlockSpec((16, D), lambda g: (jnp.maximum(tile_of(g) * (tm // 16) - 1, 0), C_CC // D))
    return u_spec, ukv_prev, uvc_prev, ucc_prev


def _full(shape):
    n = len(shape)
    return pl.BlockSpec(shape, lambda g: (0,) * n)


def mixer_fwd(x, u, cw, qg, kg, sinks, gb, wco, wao, wout, c, name):
    t = x.shape[0]
    tm = min(256, t)
    n_tiles = t // tm
    nb = tm // BLK
    cn = sorted(c)

    def body(x_ref, u_ref, ukv_prev, uvc_prev, ucc_prev, cw_ref, qg_ref, kg_ref, sinks_ref, gb_ref, wco_ref, wao_ref,
             wout_ref, *rest):
        cref = dict(zip(cn, rest[:len(cn)]))
        xo_ref, ya_ref, yb_ref, o_scr = rest[len(cn):]
        is_first = pl.program_id(0) == 0
        _, _, _, conv = _conv_fwd(u_ref, uvc_prev, ucc_prev, cw_ref, is_first, tm)
        zc = _bf(u_ref, C_ZC, D)
        yc = _bf(u_ref, C_BC, D) * conv * (zc * _sigmoid(zc))
        ya = _dot(yc.astype(BF16), wco_ref[...])

        _, _, qn, _, _, kt, vt = _attn_inputs(u_ref, ukv_prev, qg_ref, kg_ref, cref, tm)
        valid, valid_first, lane_grp, row_grp = _attn_masks(is_first)
        for h in range(N_KV):
            sink = _sink_col(sinks_ref, h, row_grp)
            for b in range(nb):
                qs = _stack_groups(qn[b * BLK:(b + 1) * BLK, h * KVW:(h + 1) * KVW], lane_grp)
                pn, _ = _softmax_block(qs, kt[h][b * BLK:(b + 2) * BLK], valid_first if b == 0 else valid, sink)
                o4 = _dot(pn.astype(BF16), vt[h][b * BLK:(b + 2) * BLK])
                o_scr[b * BLK:(b + 1) * BLK, h * KVW:(h + 1) * KVW] = _unstack_groups(o4, lane_grp)
        za = _bf(u_ref, C_ZA, D)
        ob = o_scr[...] * (za * _sigmoid(za))
        yb = _dot(ob.astype(BF16), wao_ref[...])

        g_a = _sigmoid(_bf(u_ref, C_GA, D) + gb_ref[:, 0:D])
        g_b = _sigmoid(_bf(u_ref, C_GB, D) + gb_ref[:, D:2 * D])
        merged = g_a * ya + g_b * yb
        xo_ref[...] = x_ref[...] + _dot(merged.astype(BF16), wout_ref[...])
        ya_ref[...] = ya.astype(BF16)
        yb_ref[...] = yb.astype(BF16)

    u_spec, ukv_prev, uvc_prev, ucc_prev = _mixer_specs(t, tm, n_tiles, lambda g: g)
    tok = pl.BlockSpec((tm, D), lambda g: (g, 0))
    consts = [c[k] for k in cn]
    return pl.pallas_call(
        body, name=name, grid=(n_tiles,),
        in_specs=[tok, u_spec, ukv_prev, uvc_prev, ucc_prev, _full((8, D)), _full((1, D)), _full((1, KVW)),
                  pl.BlockSpec(memory_space=pltpu.SMEM), _full((1, 2 * D)), _full((D, D)), _full((D, D)), _full((D, D))]
                 + [_full(a.shape) for a in consts],
        out_specs=[tok, tok, tok],
        out_shape=[jax.ShapeDtypeStruct((t, D), F32), jax.ShapeDtypeStruct((t, D), BF16), jax.ShapeDtypeStruct((t, D), BF16)],
        scratch_shapes=[pltpu.VMEM((tm, D), F32)],
        compiler_params=_cparams(("arbitrary",)),
    )(x, u, u, u, u, cw, qg, kg, sinks, gb, wco, wao, wout, *consts)


def mixer_bwd(dout, u, ya, yb, cw, qg, kg, sinks, gb, wco, wao, wout, c, name):
    t = dout.shape[0]
    tm = min(256, t)
    n_tiles = t // tm
    nb = tm // BLK
    kb = tm + BLK
    cn = sorted(c)

    def body(dout_ref, u_ref, ukv_prev, uvc_prev, ucc_prev, ya_ref, yb_ref, cw_ref, qg_ref, kg_ref, sinks_ref, gb_ref,
             wco_ref, wao_ref, wout_ref, *rest):
        cref = dict(zip(cn, rest[:len(cn)]))
        (du_ref, small_ref, merged_ref, yc_ref, ob_ref, dya_ref, dyb_ref,
         o_scr, dq_scr, dk4_scr, dv4_scr, carry_kv, carry_conv) = rest[len(cn):]
        g = pl.program_id(0)
        is_first = g == n_tiles - 1

        @pl.when(g == 0)
        def _():
            carry_kv[...] = jnp.zeros_like(carry_kv)
            carry_conv[...] = jnp.zeros_like(carry_conv)
            small_ref[...] = jnp.zeros_like(small_ref)

        dout = dout_ref[...]
        dout_b = dout.astype(BF16)
        ya_v = ya_ref[...].astype(F32)
        yb_v = yb_ref[...].astype(F32)
        g_a = _sigmoid(_bf(u_ref, C_GA, D) + gb_ref[:, 0:D])
        g_b = _sigmoid(_bf(u_ref, C_GB, D) + gb_ref[:, D:2 * D])
        merged = g_a * ya_v + g_b * yb_v
        dmerged = _dot_nt(dout_b, wout_ref[...])
        merged_ref[...] = merged.astype(BF16)
        dya = dmerged * g_a
        dyb = dmerged * g_b
        dgl_a = dmerged * ya_v * g_a * (1.0 - g_a)
        dgl_b = dmerged * yb_v * g_b * (1.0 - g_b)
        du_ref[:, C_GA:C_GA + D] = dgl_a.astype(BF16)
        du_ref[:, C_GB:C_GB + D] = dgl_b.astype(BF16)
        small_ref[SM_GATE:SM_GATE + 1, 0:D] += jnp.sum(dgl_a, axis=0, keepdims=True)
        small_ref[SM_GATE:SM_GATE + 1, D:2 * D] += jnp.sum(dgl_b, axis=0, keepdims=True)

        p, p1, p2, conv = _conv_fwd(u_ref, uvc_prev, ucc_prev, cw_ref, is_first, tm)
        zc = _bf(u_ref, C_ZC, D)
        bc = _bf(u_ref, C_BC, D)
        sg = _sigmoid(zc)
        sc = zc * sg
        yc = bc * conv * sc
        dya_b = dya.astype(BF16)
        yc_ref[...] = yc.astype(BF16)
        dya_ref[...] = dya_b
        dyc = _dot_nt(dya_b, wco_ref[...])
        du_ref[:, C_BC:C_BC + D] = (dyc * conv * sc).astype(BF16)
        du_ref[:, C_ZC:C_ZC + D] = (dyc * bc * conv * (sg * (1.0 + zc * (1.0 - sg)))).astype(BF16)
        dconv = dyc * bc * sc
        small_ref[SM_CONV + 2:SM_CONV + 3, 0:D] += jnp.sum(dconv * p, axis=0, keepdims=True)
        small_ref[SM_CONV + 1:SM_CONV + 2, 0:D] += jnp.sum(dconv * p1, axis=0, keepdims=True)
        small_ref[SM_CONV:SM_CONV + 1, 0:D] += jnp.sum(dconv * p2, axis=0, keepdims=True)
        row = lax.broadcasted_iota(jnp.int32, (tm, 1), 0)
        nxt = carry_conv[...]
        d1 = jnp.where(row == tm - 1, nxt[0:1, :], pltpu.roll(dconv, tm - 1, 0))
        d2 = jnp.where(row == tm - 1, nxt[1:2, :], jnp.where(row == tm - 2, nxt[0:1, :], pltpu.roll(dconv, tm - 2, 0)))
        carry_conv[...] = dconv[0:8, :]
        cw = cw_ref[...]
        dp = cw[2:3, :] * dconv + cw[1:2, :] * d1 + cw[0:1, :] * d2
        du_ref[:, C_CC:C_CC + D] = (dp * _bf(u_ref, C_VC, D)).astype(BF16)
        du_ref[:, C_VC:C_VC + D] = (dp * _bf(u_ref, C_CC, D)).astype(BF16)

        dyb_b = dyb.astype(BF16)
        dob = _dot_nt(dyb_b, wao_ref[...])
        za = _bf(u_ref, C_ZA, D)
        sga = _sigmoid(za)
        sa = za * sga
        do = dob * sa
        qhat, rq, qn, khat, rk, kt, vt = _attn_inputs(u_ref, ukv_prev, qg_ref, kg_ref, cref, tm)
        valid, valid_first, lane_grp, row_grp = _attn_masks(is_first)
        dk4_scr[...] = jnp.zeros_like(dk4_scr)
        dv4_scr[...] = jnp.zeros_like(dv4_scr)
        lane16 = lax.broadcasted_iota(jnp.int32, (1, 2 * D), 1)
        dsink_row = jnp.zeros((1, 2 * D), F32)
        for h in range(N_KV):
            sink = _sink_col(sinks_ref, h, row_grp)
            dsink_col = jnp.zeros((GROUP * BLK, 1), F32)
            for b in range(nb):
                rows = slice(b * BLK, (b + 1) * BLK)
                band = slice(b * BLK, (b + 2) * BLK)
                cols = slice(h * KVW, (h + 1) * KVW)
                qs = _stack_groups(qn[rows, cols], lane_grp)
                pn, ps = _softmax_block(qs, kt[h][band], valid_first if b == 0 else valid, sink)
                pn_b = pn.astype(BF16)
                o4 = _dot(pn_b, vt[h][band])
                o_scr[rows, cols] = _unstack_groups(o4, lane_grp)
                dos = _stack_groups(do[rows, cols], lane_grp).astype(BF16)
                dpn = _dot_nt(dos, vt[h][band])
                delta = jnp.sum(pn * dpn, axis=-1, keepdims=True)
                ds = (pn * (dpn - delta) * SCALE).astype(BF16)
                dsink_col = dsink_col - ps * delta
                dq_scr[rows, cols] = _unstack_groups(_dot(ds, kt[h][band]), lane_grp)
                dk4_scr[h, band, :] += _dot_tn(ds, qs)
                dv4_scr[h, band, :] += _dot_tn(pn_b, dos)
            for gi in range(GROUP):
                tot = jnp.sum(dsink_col[gi * BLK:(gi + 1) * BLK, :], axis=0, keepdims=True)
                dsink_row = dsink_row + jnp.where(lane16 == GROUP * h + gi, tot, 0.0)
        small_ref[SM_SINK:SM_SINK + 1, :] += dsink_row

        o = o_scr[...]
        ob_ref[...] = (o * sa).astype(BF16)
        dyb_ref[...] = dyb_b
        du_ref[:, C_ZA:C_ZA + D] = (dob * o * (sga * (1.0 + za * (1.0 - sga)))).astype(BF16)

        dqn = dq_scr[...]
        small_ref[SM_QG:SM_QG + 1, 0:D] += jnp.sum(dqn * qhat, axis=0, keepdims=True)
        dqh = dqn * qg_ref[...]
        dq = rq * (dqh - qhat * _head_mean(dqh * qhat, cref["sel_q"][...], cref["exp_q"][...]))
        du_ref[:, C_Q:C_Q + D] = dq.astype(BF16)

        dkn_band = jnp.zeros((kb, KVW), F32)
        dv_band = jnp.zeros((kb, KVW), F32)
        for h in range(N_KV):
            dkn_band = dkn_band + _dot2(dk4_scr[h], cref["rep_t"][h])
            dv_band = dv_band + _dot2(dv4_scr[h], cref["rep_t"][h])
        carried = carry_kv[...]
        pad = jnp.zeros((tm - BLK, KVW), F32)
        if nb > 1:
            dkn = dkn_band[BLK:, :] + jnp.concatenate([pad, carried[:, 0:KVW]], axis=0)
            dv = dv_band[BLK:, :] + jnp.concatenate([pad, carried[:, KVW:2 * KVW]], axis=0)
        else:
            dkn = dkn_band[BLK:, :] + carried[:, 0:KVW]
            dv = dv_band[BLK:, :] + carried[:, KVW:2 * KVW]
        carry_kv[:, 0:KVW] = dkn_band[0:BLK, :]
        carry_kv[:, KVW:2 * KVW] = dv_band[0:BLK, :]
        khat_t = khat[BLK:, :]
        small_ref[SM_KG:SM_KG + 1, 0:KVW] += jnp.sum(dkn * khat_t, axis=0, keepdims=True)
        dkh = dkn * kg_ref[...]
        dk = rk[BLK:, :] * (dkh - khat_t * _head_mean(dkh * khat_t, cref["sel_k"][...], cref["exp_k"][...]))
        du_ref[:, C_K:C_K + KVW] = dk.astype(BF16)
        du_ref[:, C_V:C_V + KVW] = dv.astype(BF16)

    rev = lambda g: n_tiles - 1 - g
    u_spec, ukv_prev, uvc_prev, ucc_prev = _mixer_specs(t, tm, n_tiles, rev)
    tok = pl.BlockSpec((tm, D), lambda g: (rev(g), 0))
    consts = [c[k] for k in cn]
    wspec = _full((D, D))
    return pl.pallas_call(
        body, name=name, grid=(n_tiles,),
        in_specs=[tok, u_spec, ukv_prev, uvc_prev, ucc_prev, tok, tok, _full((8, D)), _full((1, D)), _full((1, KVW)),
                  pl.BlockSpec(memory_space=pltpu.SMEM), _full((1, 2 * D)), wspec, wspec, wspec]
                 + [_full(a.shape) for a in consts],
        out_specs=[pl.BlockSpec((tm, IN_COLS), lambda g: (rev(g), 0)), _full((SM_ROWS, 2 * D))] + [tok] * 5,
        out_shape=[jax.ShapeDtypeStruct((t, IN_COLS), BF16), jax.ShapeDtypeStruct((SM_ROWS, 2 * D), F32)]
                  + [jax.ShapeDtypeStruct((t, D), BF16)] * 5,
        scratch_shapes=[pltpu.VMEM((tm, D), F32), pltpu.VMEM((tm, D), F32),
                        pltpu.VMEM((N_KV, kb, KVW), F32), pltpu.VMEM((N_KV, kb, KVW), F32),
                        pltpu.VMEM((BLK, 2 * KVW), F32), pltpu.VMEM((8, D), F32)],
        compiler_params=_cparams(("arbitrary",)),
    )(dout, u, u, u, u, ya, yb, cw, qg, kg, sinks, gb, wco, wao, wout, *consts)


def matmul_tn(a, b, name):
    t, m = a.shape
    tk = min(1024, t)
    mb = 2176 if m == IN_COLS else m
    nk = t // tk

    def body(a_ref, b_ref, o_ref, acc):
        k = pl.program_id(1)
        prod = _dot_tn(a_ref[...].astype(BF16), b_ref[...].astype(BF16))

        @pl.when(k == 0)
        def _():
            acc[...] = prod

        @pl.when(k > 0)
        def _():
            acc[...] += prod

        @pl.when(k == nk - 1)
        def _():
            o_ref[...] = acc[...].astype(BF16)

    return pl.pallas_call(
        body, name=name, grid=(m // mb, nk),
        in_specs=[pl.BlockSpec((tk, mb), lambda j, k: (k, j)), pl.BlockSpec((tk, D), lambda j, k: (k, 0))],
        out_specs=pl.BlockSpec((mb, D), lambda j, k: (j, 0)),
        out_shape=jax.ShapeDtypeStruct((m, D), BF16),
        scratch_shapes=[pltpu.VMEM((mb, D), F32)],
        compiler_params=_cparams(("arbitrary", "arbitrary")),
    )(a, b)


def inproj_bwd_x(du, w, x, ng, dout, name):
    t = x.shape[0]
    tm = min(512, t)
    kc = 2176
    nk = IN_COLS // kc

    def body(du_ref, w_ref, x_ref, ng_ref, dout_ref, dx_ref, dng_ref, acc):
        i = pl.program_id(0)
        k = pl.program_id(1)
        prod = _dot(du_ref[...], w_ref[...])

        @pl.when(k == 0)
        def _():
            acc[...] = prod

        @pl.when(k > 0)
        def _():
            acc[...] += prod

        @pl.when((i == 0) & (k == 0))
        def _():
            dng_ref[...] = jnp.zeros_like(dng_ref)

        @pl.when(k == nk - 1)
        def _():
            dh = acc[...]
            xf = x_ref[...]
            r = lax.rsqrt(jnp.mean(xf * xf, axis=-1, keepdims=True) + EPS)
            xhat = xf * r
            dng_ref[0:1, :] += jnp.sum(dh * xhat, axis=0, keepdims=True)
            dxh = dh * ng_ref[...]
            dx_ref[...] = dout_ref[...] + r * (dxh - xhat * jnp.mean(dxh * xhat, axis=-1, keepdims=True))

    tok = pl.BlockSpec((tm, D), lambda i, k: (i, 0))
    return pl.pallas_call(
        body, name=name, grid=(t // tm, nk),
        in_specs=[pl.BlockSpec((tm, kc), lambda i, k: (i, k)), pl.BlockSpec((kc, D), lambda i, k: (k, 0)), tok,
                  pl.BlockSpec((1, D), lambda i, k: (0, 0)), tok],
        out_specs=[tok, pl.BlockSpec((8, D), lambda i, k: (0, 0))],
        out_shape=[jax.ShapeDtypeStruct((t, D), F32), jax.ShapeDtypeStruct((8, D), F32)],
        scratch_shapes=[pltpu.VMEM((tm, D), F32)],
        compiler_params=_cparams(("arbitrary", "arbitrary")),
    )(du, w, x, ng, dout)


def loss_head(y, target, name):
    t = y.shape[0]
    tm = min(1024, t)

    def body(y_ref, t_ref, dy_ref, loss_ref):
        @pl.when(pl.program_id(0) == 0)
        def _():
            loss_ref[...] = jnp.zeros_like(loss_ref)
        err = y_ref[...] - t_ref[...]
        dy_ref[...] = err * (1.0 / D)
        part = jnp.sum(jnp.sum(err * err, axis=-1, keepdims=True) * (1.0 / D), axis=0, keepdims=True)
        loss_ref[...] += 0.5 * part

    tok = pl.BlockSpec((tm, D), lambda i: (i, 0))
    return pl.pallas_call(
        body, name=name, grid=(t // tm,), in_specs=[tok, tok],
        out_specs=[tok, pl.BlockSpec((8, 128), lambda i: (0, 0))],
        out_shape=[jax.ShapeDtypeStruct((t, D), F32), jax.ShapeDtypeStruct((8, 128), F32)],
        compiler_params=_cparams(("arbitrary",)),
    )(y, target)


def layer_operands(l, norm_g, conv_w_full, q_norm_g, k_norm_g, sinks, gate_b, w_in_b, wco_b, wao_b, wout_b):
    return dict(
        ng=norm_g[l][None, :], cw=jnp.pad(conv_w_full[l], ((0, 5), (0, 0))),
        qg=jnp.tile(q_norm_g[l], D // HEAD)[None, :], kg=jnp.tile(k_norm_g[l], N_KV)[None, :],
        sinks=sinks[l][None, :], gb=gate_b[l][None, :],
        w_in=w_in_b[l], wco=wco_b[l], wao=wao_b[l], wout=wout_b[l])


def layer_fwd(x, lw, c, l, token):
    u, h = inproj_fwd(x, lw["ng"] + token[0:1, 0:1], lw["w_in"], f"inproj_fwd_{l}")
    xo, ya, yb = mixer_fwd(x, u, lw["cw"], lw["qg"], lw["kg"], lw["sinks"], lw["gb"], lw["wco"], lw["wao"], lw["wout"], c,
                           f"mixer_fwd_{l}")
    return xo, (x, u, h, ya, yb)


def layer_bwd(dout, saved, lw, c, l, send_off):
    x, u, h, ya, yb = saved
    du, small, merged, yc, ob, dya, dyb = mixer_bwd(dout, u, ya, yb, lw["cw"], lw["qg"], lw["kg"], lw["sinks"], lw["gb"],
                                                    lw["wco"], lw["wao"], lw["wout"], c, f"mixer_bwd_{l}")
    grads = dict(
        wout=matmul_tn(merged, dout, f"dw_out_{l}"), wco=matmul_tn(yc, dya, f"dw_conv_out_{l}"),
        wao=matmul_tn(ob, dyb, f"dw_attn_out_{l}"), w_in=matmul_tn(du, h, f"dw_in_{l}"), small=small)
    token = send_off(grads)
    dx, grads["dng"] = inproj_bwd_x(du, lw["w_in"], x, lw["ng"] + token[0:1, 0:1], dout, f"inproj_bwd_{l}")
    return dx, grads


MESH = pl.DeviceIdType.MESH
ANY = pl.BlockSpec(memory_space=pl.ANY)


def _place():
    return lax.axis_index("x"), lax.axis_index("y"), lax.axis_index("c")


def all_gather(arrs, name):
    n = len(arrs)

    def body(*refs):
        ins, outs = refs[:n], refs[n:2 * n]
        send_sems, recv_sems, local_sems = refs[2 * n:]
        x, y, c = _place()
        me, sibling = (x, y, c), (x, y, 1 - c)
        chips = [(1 - x, y), (x, 1 - y), (1 - x, 1 - y)]

        def slot(a, block):
            px, py, pc = block
            return outs[a].at[4 * px + 2 * py + pc]

        def copy(a, k, block, to, src=None):
            return pltpu.make_async_remote_copy(
                src_ref=slot(a, block) if src is None else src, dst_ref=slot(a, block),
                send_sem=send_sems.at[a, k], recv_sem=recv_sems.at[a, k], device_id=to, device_id_type=MESH)

        mine = [pltpu.make_async_copy(ins[a], slot(a, me), local_sems.at[a]) for a in range(n)]
        for cp in mine:
            cp.start()
        first = []
        for a in range(n):
            first.append(copy(a, 0, me, sibling, src=ins[a]))
            first += [copy(a, 1 + j, me, (*chip, c), src=ins[a]) for j, chip in enumerate(chips)]
        for cp in first:
            cp.start()
        passed = []
        for j, chip in enumerate(chips):
            for a in range(n):
                copy(a, 1 + j, (*chip, c), me).wait_recv()
                passed.append(copy(a, 4 + j, (*chip, c), sibling))
                passed[-1].start()
        for a in range(n):
            copy(a, 0, sibling, me).wait_recv()
            for j, chip in enumerate(chips):
                copy(a, 4 + j, (*chip, 1 - c), me).wait_recv()
        for cp in first + passed:
            cp.wait_send()
        for cp in mine:
            cp.wait()

    return pl.pallas_call(
        body, name=name, in_specs=[ANY] * n, out_specs=[ANY] * n,
        out_shape=[jax.ShapeDtypeStruct((N_DEV,) + a.shape, a.dtype) for a in arrs],
        scratch_shapes=[pltpu.SemaphoreType.DMA((n, 7)), pltpu.SemaphoreType.DMA((n, 7)), pltpu.SemaphoreType.DMA((n,))],
    )(*arrs)


HBM_SPEC = pl.BlockSpec(memory_space=pltpu.HBM)
SEM_SPEC = pl.BlockSpec(memory_space=pltpu.SEMAPHORE)
EFFECT = pltpu.SideEffectType.DATAFLOW_SIDE_EFFECTING


def _exchange_copies(ins, lands, send_sems, recv_sems, scatter, arriving):
    x, y, c = _place()
    me = 4 * x + 2 * y + c
    out = []
    for a in range(len(ins)):
        for k in range(1, N_DEV):
            px = 1 - x if k & 4 else x
            py = 1 - y if k & 2 else y
            pc = 1 - c if k & 1 else c
            theirs = 4 * px + 2 * py + pc
            out.append(pltpu.make_async_remote_copy(
                src_ref=ins[a].at[theirs] if scatter else ins[a], dst_ref=lands[a].at[theirs if arriving else me],
                send_sem=send_sems.at[a * (N_DEV - 1) + k - 1], recv_sem=recv_sems.at[a * (N_DEV - 1) + k - 1],
                device_id=(px, py, pc), device_id_type=MESH))
    return out


def exchange_start(arrs, scatter, after, name):
    n = len(arrs)
    lands = [lax.empty(a.shape if scatter else (N_DEV,) + a.shape, a.dtype) for a in arrs]

    def body(*refs):
        ins, lz = refs[:n], refs[n:2 * n]
        send_sems, recv_sems = refs[2 * n + 1], refs[2 * n + 2]
        token = refs[-1]
        for cp in _exchange_copies(ins, lz, send_sems, recv_sems, scatter, False):
            cp.start()
        token[...] = jnp.zeros_like(token)

    sems = pltpu.SemaphoreType.DMA((n * (N_DEV - 1),))
    res = pl.pallas_call(
        body, name=name,
        out_shape=(sems, sems, *[pltpu.HBM(a.shape, a.dtype) for a in arrs], *[pltpu.HBM(a.shape, a.dtype) for a in lands],
                   jax.ShapeDtypeStruct((8, 128), F32)),
        in_specs=[HBM_SPEC] * (2 * n) + [ANY],
        out_specs=(SEM_SPEC, SEM_SPEC, *[HBM_SPEC] * (2 * n), pl.BlockSpec(memory_space=pltpu.VMEM)),
        input_output_aliases={i: 2 + i for i in range(2 * n)},
        compiler_params=pltpu.CompilerParams(has_side_effects=EFFECT),
    )(*[pltpu.with_memory_space_constraint(a, pltpu.HBM) for a in arrs],
      *[pltpu.with_memory_space_constraint(a, pltpu.HBM) for a in lands], after)
    return dict(send=res[0], recv=res[1], srcs=res[2:2 + n], lands=res[2 + n:2 + 2 * n], token=res[-1], scatter=scatter)


def exchange_wait(state, after, name):
    n = len(state["srcs"])
    scatter = state["scatter"]

    def body(*refs):
        ins, lz = refs[:n], refs[n:2 * n]
        send_sems, recv_sems = refs[2 * n], refs[2 * n + 1]
        for cp in _exchange_copies(ins, lz, send_sems, recv_sems, scatter, True):
            cp.wait_send()
            cp.wait_recv()

    both = list(state["srcs"]) + list(state["lands"])
    res = pl.pallas_call(
        body, name=name, out_shape=tuple(pltpu.HBM(a.shape, a.dtype) for a in both),
        in_specs=[HBM_SPEC] * (2 * n) + [SEM_SPEC, SEM_SPEC, ANY], out_specs=tuple([HBM_SPEC] * (2 * n)),
        input_output_aliases={i: i for i in range(2 * n)},
        compiler_params=pltpu.CompilerParams(has_side_effects=EFFECT),
    )(*both, state["send"], state["recv"], after)
    return res[:n], res[n:]


def adamw(w, m, v, parts, name):
    r, cdim = w.shape
    n_parts = parts.shape[0]
    rb = 256 if r % 256 == 0 else (SHARD_COLS // 4 if r == SHARD_COLS else r)

    def body(w_ref, m_ref, v_ref, p_ref, g_ref, d_ref, mo_ref, vo_ref):
        g = p_ref[0].astype(F32)
        for i in range(1, n_parts):
            g = g + p_ref[i].astype(F32)
        m_new = ADAM_B1 * m_ref[...] + (1.0 - ADAM_B1) * g
        v_new = ADAM_B2 * v_ref[...] + (1.0 - ADAM_B2) * (g * g)
        m_hat = m_new / (1.0 - ADAM_B1 ** ADAM_STEP)
        v_hat = v_new / (1.0 - ADAM_B2 ** ADAM_STEP)
        g_ref[...] = g
        d_ref[...] = -ADAM_LR * (m_hat / (jnp.sqrt(v_hat) + ADAM_EPS) + ADAM_WD * w_ref[...])
        mo_ref[...] = m_new
        vo_ref[...] = v_new

    blk = pl.BlockSpec((rb, cdim), lambda i: (i, 0))
    return pl.pallas_call(
        body, name=name, grid=(r // rb,),
        in_specs=[blk, blk, blk, pl.BlockSpec((n_parts, rb, cdim), lambda i: (0, i, 0))],
        out_specs=[blk] * 4, out_shape=[jax.ShapeDtypeStruct((r, cdim), F32)] * 4,
        compiler_params=_cparams(("arbitrary",)),
    )(w, m, v, parts)


def adamw_layers(w, m, v, parts, name):
    nl, r, cdim = w.shape
    n_parts = parts[0].shape[0]
    rb = 256 if r % 256 == 0 else (SHARD_COLS // 4 if r == SHARD_COLS else r)
    nblk = r // rb

    def body(w_ref, m_ref, v_ref, *rest):
        p_refs = rest[:nl]
        g_ref, d_ref, mo_ref, vo_ref = rest[nl:]
        for k in range(nl):
            @pl.when(pl.program_id(0) == k)
            def _(k=k):
                g = p_refs[k][0].astype(F32)
                for i in range(1, n_parts):
                    g = g + p_refs[k][i].astype(F32)
                m_new = ADAM_B1 * m_ref[...] + (1.0 - ADAM_B1) * g
                v_new = ADAM_B2 * v_ref[...] + (1.0 - ADAM_B2) * (g * g)
                m_hat = m_new / (1.0 - ADAM_B1 ** ADAM_STEP)
                v_hat = v_new / (1.0 - ADAM_B2 ** ADAM_STEP)
                g_ref[...] = g
                d_ref[...] = -ADAM_LR * (m_hat / (jnp.sqrt(v_hat) + ADAM_EPS) + ADAM_WD * w_ref[...])
                mo_ref[...] = m_new
                vo_ref[...] = v_new

    blk = pl.BlockSpec((None, rb, cdim), lambda l, i: (l, i, 0))

    def part_spec(k):
        return pl.BlockSpec((n_parts, rb, cdim),
                            lambda l, i: (0, jnp.where(l < k, 0, jnp.where(l == k, i, nblk - 1)), 0))

    return pl.pallas_call(
        body, name=name, grid=(nl, nblk),
        in_specs=[blk, blk, blk] + [part_spec(k) for k in range(nl)],
        out_specs=[blk] * 4, out_shape=[jax.ShapeDtypeStruct((nl, r, cdim), F32)] * 4,
        compiler_params=_cparams(("arbitrary", "arbitrary")),
    )(w, m, v, *parts)


def small_sum(parts, fold, name):
    rows = parts.shape[1]

    def dot3(xv, sel):
        out = jnp.zeros((xv.shape[0], sel.shape[1]), F32)
        for _ in range(3):
            hi = xv.astype(BF16)
            out = out + _dot(hi, sel)
            xv = xv - hi.astype(F32)
        return out

    def body(p_ref, fold_ref, o_ref):
        tot = p_ref[0]
        for i in range(1, N_DEV):
            tot = tot + p_ref[i]
        o_ref[...] = tot
        for l in range(rows // SM_ROWS):
            blk = tot[l * SM_ROWS:l * SM_ROWS + 8, 0:D]
            folded = dot3(blk, fold_ref[...])
            o_ref[l * SM_ROWS + 9:l * SM_ROWS + 10, 0:128] = folded[SM_QG:SM_QG + 1, :]
            o_ref[l * SM_ROWS + 10:l * SM_ROWS + 11, 0:128] = folded[SM_KG:SM_KG + 1, :]

    return pl.pallas_call(
        body, name=name, out_shape=jax.ShapeDtypeStruct((rows, 2 * D), F32),
        compiler_params=_cparams(None),
    )(parts, fold)


def kernel(x, norm_g, w_in, conv_w, q_norm_g, k_norm_g, sinks, w_conv_out, w_attn_out, gate_b, w_out, loss_target, m_norm_g, m_w_in, m_conv_w, m_q_norm_g, m_k_norm_g, m_sinks, m_w_conv_out, m_w_attn_out, m_gate_b, m_w_out, v_norm_g, v_w_in, v_conv_w, v_q_norm_g, v_k_norm_g, v_sinks, v_w_conv_out, v_w_attn_out, v_gate_b, v_w_out):
    c = _selectors()
    me = 4 * lax.axis_index("x") + 2 * lax.axis_index("y") + lax.axis_index("c")

    own = lambda land, mine: lax.dynamic_update_index_in_dim(land, mine, me, 0)
    w_in_t, m_w_in_t, v_w_in_t = (jnp.swapaxes(a, 1, 2) for a in (w_in, m_w_in, v_w_in))

    def gather_start(l, after):
        arrs = [w_in_t[l].astype(BF16), w_conv_out[l].astype(BF16), w_attn_out[l].astype(BF16), w_out[l].astype(BF16)]
        if l == 0:
            arrs.append(conv_w)
        return exchange_start(arrs, False, after, f"gather_start_{l}")

    h = x[0]
    saved, lws = [], []
    conv_full = None
    gather = gather_start(0, h)
    for l in range(DEPTH):
        mine, lands = exchange_wait(gather, gather["token"] if l == 0 else h, f"gather_wait_{l}")
        if l + 1 < DEPTH:
            gather = gather_start(l + 1, mine[0])
        lands = [own(land, src) for land, src in zip(lands, mine)]
        if l == 0:
            conv_full = jnp.transpose(lands[4], (1, 2, 0, 3)).reshape(DEPTH, 3, D)
        lws.append(layer_operands(l, norm_g, conv_full, q_norm_g, k_norm_g, sinks, gate_b,
                                  {l: lands[0].reshape(IN_COLS, D)}, {l: lands[1].reshape(D, D)},
                                  {l: lands[2].reshape(D, D)}, {l: lands[3].reshape(D, D)}))
        h, s = layer_fwd(h, lws[l], c, l, gather["token"])
        saved.append(s)
    dh, loss_part = loss_head(h, loss_target[0], "loss_head")

    grads, scatters = [None] * DEPTH, [None] * DEPTH
    for l in reversed(range(DEPTH)):
        def send_off(g, l=l):
            srcs = [g["w_in"].reshape(N_DEV, SHARD_COLS, D)] + [g[k].reshape(N_DEV, SHARD_ROWS, D) for k in ("wco", "wao", "wout")]
            scatters[l] = exchange_start(srcs, True, g["small"], f"scatter_start_{l}")
            return scatters[l]["token"]
        dh, grads[l] = layer_bwd(dh, saved[l], lws[l], c, l, send_off)
    parts = [None] * DEPTH
    for l in reversed(range(DEPTH)):
        mine, lands = exchange_wait(scatters[l], dh, f"scatter_wait_{l}")
        parts[l] = [own(land, lax.dynamic_index_in_dim(src, me, 0, keepdims=False)) for land, src in zip(lands, mine)]

    u_in = [jnp.swapaxes(o, 1, 2) for o in
            adamw_layers(w_in_t, m_w_in_t, v_w_in_t, [parts[l][0] for l in range(DEPTH)], "adamw_w_in")]
    u_co = adamw_layers(w_conv_out, m_w_conv_out, v_w_conv_out, [parts[l][1] for l in range(DEPTH)], "adamw_w_conv_out")
    u_ao = adamw_layers(w_attn_out, m_w_attn_out, v_w_attn_out, [parts[l][2] for l in range(DEPTH)], "adamw_w_attn_out")
    u_out = adamw_layers(w_out, m_w_out, v_w_out, [parts[l][3] for l in range(DEPTH)], "adamw_w_out")

    blocks = []
    for l in range(DEPTH):
        blk = grads[l]["small"]
        blk = blk.at[SM_NORM, 0:D].set(grads[l]["dng"][0])
        if l == 0:
            blk = blk.at[SM_LOSS, 0:128].set(loss_part[0])
        blocks.append(blk)
    tot = small_sum(all_gather([jnp.concatenate(blocks, axis=0)], "gather_small")[0], c["fold"], "small_sum")
    tot = tot.reshape(DEPTH, SM_ROWS, 2 * D)
    loss = tot[0, SM_LOSS, 0]

    def update_small(w, m, v, g, name):
        return adamw(w, m, v, g[None], name)

    u_ng = update_small(norm_g, m_norm_g, v_norm_g, tot[:, SM_NORM, 0:D], "adamw_norm_g")
    u_qg = update_small(q_norm_g, m_q_norm_g, v_q_norm_g, tot[:, 9, 0:HEAD], "adamw_q_norm_g")
    u_kg = update_small(k_norm_g, m_k_norm_g, v_k_norm_g, tot[:, 10, 0:HEAD], "adamw_k_norm_g")
    u_sk = update_small(sinks, m_sinks, v_sinks, tot[:, SM_SINK, 0:16], "adamw_sinks")
    u_gb = update_small(gate_b, m_gate_b, v_gate_b, tot[:, SM_GATE, :], "adamw_gate_b")
    g_conv = lax.dynamic_slice_in_dim(tot[:, SM_CONV:SM_CONV + 3, 0:D], me * SHARD_ROWS, SHARD_ROWS, axis=2)
    u_cw = [o.reshape(DEPTH, 3, SHARD_ROWS) for o in update_small(
        conv_w.reshape(DEPTH * 3, SHARD_ROWS), m_conv_w.reshape(DEPTH * 3, SHARD_ROWS),
        v_conv_w.reshape(DEPTH * 3, SHARD_ROWS), g_conv.reshape(DEPTH * 3, SHARD_ROWS), "adamw_conv_w")]

    order = [u_ng, u_in, u_cw, u_qg, u_kg, u_sk, u_co, u_ao, u_gb, u_out]
    return (loss, dh[None], *[u[0] for u in order], *[u[1] for u in order], *[u[2] for u in order], *[u[3] for u in order])
```

```python
import functools

import jax
import jax.numpy as jnp
from jax import lax
from jax.experimental import pallas as pl
from jax.experimental.pallas import tpu as pltpu

F32 = jnp.float32
BF16 = jnp.bfloat16

N_DEV = 8
DEPTH = 4
D = 1024
N_KV = 4
GROUP = 4
HEAD = 64
BLK = 128
KVW = N_KV * HEAD
IN_COLS = 8704
SHARD_COLS = IN_COLS // N_DEV
SHARD_ROWS = D // N_DEV
C_VC, C_BC, C_CC, C_ZC, C_Q, C_K, C_V, C_ZA, C_GA, C_GB = 0, 1024, 2048, 3072, 4096, 5120, 5376, 5632, 6656, 7680
EPS = 1e-6
NEG_INF = -1e30
SCALE = HEAD ** -0.5

ADAM_LR = 0.001
ADAM_B1 = 0.9
ADAM_B2 = 0.999
ADAM_EPS = 1e-08
ADAM_WD = 0.01
ADAM_STEP = 10

VMEM_LIMIT = 60 * 1024 * 1024
SM_ROWS = 16
SM_GATE, SM_CONV, SM_QG, SM_KG, SM_SINK, SM_NORM, SM_LOSS = 0, 1, 4, 5, 6, 7, 8


def _cparams(sem):
    return pltpu.CompilerParams(dimension_semantics=sem, vmem_limit_bytes=VMEM_LIMIT)


def _dot(a, b):
    return jnp.dot(a, b, preferred_element_type=F32)


def _dot_nt(a, b):
    return lax.dot_general(a, b, (((1,), (1,)), ((), ())), preferred_element_type=F32)


def _dot_tn(a, b):
    return lax.dot_general(a, b, (((0,), (0,)), ((), ())), preferred_element_type=F32)


def _dot2(x, sel):
    hi = x.astype(BF16)
    lo = (x - hi.astype(F32)).astype(BF16)
    return _dot(hi, sel) + _dot(lo, sel)


def _sigmoid(z):
    return 1.0 / (1.0 + jnp.exp(-z))


def _head_mean(t, sel, exp):
    return _dot2(_dot2(t, sel) * (1.0 / HEAD), exp)


def _bf(ref, c0, width):
    return ref[:, c0:c0 + width].astype(F32)


def _selectors():
    c = jnp.arange(D)
    sel_q = (c[:, None] // HEAD == jnp.arange(128)[None, :]).astype(BF16)
    ck = jnp.arange(KVW)
    sel_k = (ck[:, None] // HEAD == jnp.arange(128)[None, :]).astype(BF16)
    src = jnp.arange(KVW)[:, None]
    dst = jnp.arange(KVW)[None, :]
    rep = jnp.stack([((src // HEAD == h) & (src % HEAD == dst % HEAD)).astype(BF16) for h in range(N_KV)])
    fold = (c[:, None] % HEAD == jnp.arange(128)[None, :]).astype(BF16)
    return dict(sel_q=sel_q, exp_q=sel_q.T, sel_k=sel_k, exp_k=sel_k.T, rep=rep, rep_t=jnp.swapaxes(rep, 1, 2), fold=fold)


def inproj_fwd(x, ng, w, name):
    t = x.shape[0]
    tm = min(1024, t)
    cb = 2176
    def body(x_ref, ng_ref, w_ref, u_ref, h_ref, h_scr):
        @pl.when(pl.program_id(1) == 0)
        def _():
            xf = x_ref[...]
            r = lax.rsqrt(jnp.mean(xf * xf, axis=-1, keepdims=True) + EPS)
            hb = (xf * r * ng_ref[...]).astype(BF16)
            h_scr[...] = hb
            h_ref[...] = hb
        u_ref[...] = _dot_nt(h_scr[...], w_ref[...]).astype(BF16)

    return pl.pallas_call(
        body, name=name, grid=(t // tm, IN_COLS // cb),
        in_specs=[pl.BlockSpec((tm, D), lambda i, j: (i, 0)), pl.BlockSpec((1, D), lambda i, j: (0, 0)),
                  pl.BlockSpec((cb, D), lambda i, j: (j, 0))],
        out_specs=[pl.BlockSpec((tm, cb), lambda i, j: (i, j)), pl.BlockSpec((tm, D), lambda i, j: (i, 0))],
        out_shape=[jax.ShapeDtypeStruct((t, IN_COLS), BF16), jax.ShapeDtypeStruct((t, D), BF16)],
        scratch_shapes=[pltpu.VMEM((tm, D), BF16)],
        compiler_params=_cparams(("arbitrary", "arbitrary")),
    )(x, ng, w)


def _conv_fwd(u_ref, uvc_prev, ucc_prev, cw_ref, is_first, tm):
    p = _bf(u_ref, C_CC, D) * _bf(u_ref, C_VC, D)
    pprev = ucc_prev[...].astype(F32) * uvc_prev[...].astype(F32)
    pprev = jnp.where(is_first, 0.0, pprev)
    row = lax.broadcasted_iota(jnp.int32, (tm, 1), 0)
    p1 = jnp.where(row == 0, pprev[15:16, :], pltpu.roll(p, 1, 0))
    p2 = jnp.where(row == 0, pprev[14:15, :], jnp.where(row == 1, pprev[15:16, :], pltpu.roll(p, 2, 0)))
    cw = cw_ref[...]
    conv = cw[0:1, :] * p2 + cw[1:2, :] * p1 + cw[2:3, :] * p
    return p, p1, p2, conv


def _attn_inputs(u_ref, ukv_prev, qg_ref, kg_ref, c, tm):
    q = _bf(u_ref, C_Q, D)
    rq = lax.rsqrt(_head_mean(q * q, c["sel_q"][...], c["exp_q"][...]) + EPS)
    qhat = q * rq
    qn = (qhat * qg_ref[...]).astype(BF16)
    kband = jnp.concatenate([ukv_prev[:, 0:KVW].astype(F32), _bf(u_ref, C_K, KVW)], axis=0)
    rk = lax.rsqrt(_head_mean(kband * kband, c["sel_k"][...], c["exp_k"][...]) + EPS)
    khat = kband * rk
    knb = (khat * kg_ref[...]).astype(BF16)
    vband = jnp.concatenate([ukv_prev[:, KVW:2 * KVW], u_ref[:, C_V:C_V + KVW]], axis=0)
    kt = [_dot(knb, c["rep"][h]).astype(BF16) for h in range(N_KV)]
    vt = [_dot(vband, c["rep"][h]).astype(BF16) for h in range(N_KV)]
    return qhat, rq, qn, khat, rk, kt, vt


def _attn_masks(is_first):
    rows = GROUP * BLK
    r = lax.broadcasted_iota(jnp.int32, (rows, 2 * BLK), 0)
    kk = lax.broadcasted_iota(jnp.int32, (rows, 2 * BLK), 1)
    qq = r % BLK
    valid = (kk > qq) & (kk <= qq + BLK)
    valid_first = valid & ((kk >= BLK) | jnp.logical_not(is_first))
    lane_grp = lax.broadcasted_iota(jnp.int32, (BLK, KVW), 1) // HEAD
    row_grp = lax.broadcasted_iota(jnp.int32, (rows, 1), 0) // BLK
    return valid, valid_first, lane_grp, row_grp


def _sink_col(sinks_ref, h, row_grp):
    col = jnp.full(row_grp.shape, sinks_ref[0, GROUP * h], F32)
    for gi in range(1, GROUP):
        col = jnp.where(row_grp == gi, sinks_ref[0, GROUP * h + gi], col)
    return col


def _stack_groups(a256, lane_grp):
    zero = jnp.zeros_like(a256)
    return jnp.concatenate([jnp.where(lane_grp == gi, a256, zero) for gi in range(GROUP)], axis=0)


def _unstack_groups(a4, lane_grp):
    out = jnp.where(lane_grp == 0, a4[0:BLK], 0.0)
    for gi in range(1, GROUP):
        out = out + jnp.where(lane_grp == gi, a4[gi * BLK:(gi + 1) * BLK], 0.0)
    return out


def _softmax_block(qs, kt_b, valid, sink):
    s = _dot_nt(qs, kt_b)
    s = jnp.where(valid, s, NEG_INF)
    m = jnp.maximum(jnp.max(s, axis=-1, keepdims=True), sink)
    e = jnp.exp(s - m)
    es = jnp.exp(sink - m)
    inv = 1.0 / (jnp.sum(e, axis=-1, keepdims=True) + es)
    return e * inv, es * inv


def _mixer_specs(t, tm, n_tiles, tile_of):
    nb = tm // BLK
    u_spec = pl.BlockSpec((tm, IN_COLS), lambda g: (tile_of(g), 0))
    ukv_prev = pl.BlockSpec((BLK, 2 * KVW), lambda g: (jnp.maximum(tile_of(g) * nb - 1, 0), C_K // (2 * KVW)))
    uvc_prev = pl.BlockSpec((16, D), lambda g: (jnp.maximum(tile_of(g) * (tm // 16) - 1, 0), C_VC // D))
    ucc_prev = pl.BlockSpec((16, D), lambda g: (jnp.maximum(tile_of(g) * (tm // 16) - 1, 0), C_CC // D))
    return u_spec, ukv_prev, uvc_prev, ucc_prev


def _full(shape):
    n = len(shape)
    return pl.BlockSpec(shape, lambda g: (0,) * n)


def mixer_fwd(x, u, cw, qg, kg, sinks, gb, wco, wao, wout, c, name):
    t = x.shape[0]
    tm = min(256, t)
    n_tiles = t // tm
    nb = tm // BLK
    cn = sorted(c)

    def body(x_ref, u_ref, ukv_prev, uvc_prev, ucc_prev, cw_ref, qg_ref, kg_ref, sinks_ref, gb_ref, wco_ref, wao_ref,
             wout_ref, *rest):
        cref = dict(zip(cn, rest[:len(cn)]))
        xo_ref, ya_ref, yb_ref, o_scr = rest[len(cn):]
        is_first = pl.program_id(0) == 0
        _, _, _, conv = _conv_fwd(u_ref, uvc_prev, ucc_prev, cw_ref, is_first, tm)
        zc = _bf(u_ref, C_ZC, D)
        yc = _bf(u_ref, C_BC, D) * conv * (zc * _sigmoid(zc))
        ya = _dot(yc.astype(BF16), wco_ref[...])

        _, _, qn, _, _, kt, vt = _attn_inputs(u_ref, ukv_prev, qg_ref, kg_ref, cref, tm)
        valid, valid_first, lane_grp, row_grp = _attn_masks(is_first)
        for h in range(N_KV):
            sink = _sink_col(sinks_ref, h, row_grp)
            for b in range(nb):
                qs = _stack_groups(qn[b * BLK:(b + 1) * BLK, h * KVW:(h + 1) * KVW], lane_grp)
                pn, _ = _softmax_block(qs, kt[h][b * BLK:(b + 2) * BLK], valid_first if b == 0 else valid, sink)
                o4 = _dot(pn.astype(BF16), vt[h][b * BLK:(b + 2) * BLK])
                o_scr[b * BLK:(b + 1) * BLK, h * KVW:(h + 1) * KVW] = _unstack_groups(o4, lane_grp)
        za = _bf(u_ref, C_ZA, D)
        ob = o_scr[...] * (za * _sigmoid(za))
        yb = _dot(ob.astype(BF16), wao_ref[...])

        g_a = _sigmoid(_bf(u_ref, C_GA, D) + gb_ref[:, 0:D])
        g_b = _sigmoid(_bf(u_ref, C_GB, D) + gb_ref[:, D:2 * D])
        merged = g_a * ya + g_b * yb
        xo_ref[...] = x_ref[...] + _dot(merged.astype(BF16), wout_ref[...])
        ya_ref[...] = ya.astype(BF16)
        yb_ref[...] = yb.astype(BF16)

    u_spec, ukv_prev, uvc_prev, ucc_prev = _mixer_specs(t, tm, n_tiles, lambda g: g)
    tok = pl.BlockSpec((tm, D), lambda g: (g, 0))
    consts = [c[k] for k in cn]
    return pl.pallas_call(
        body, name=name, grid=(n_tiles,),
        in_specs=[tok, u_spec, ukv_prev, uvc_prev, ucc_prev, _full((8, D)), _full((1, D)), _full((1, KVW)),
                  pl.BlockSpec(memory_space=pltpu.SMEM), _full((1, 2 * D)), _full((D, D)), _full((D, D)), _full((D, D))]
                 + [_full(a.shape) for a in consts],
        out_specs=[tok, tok, tok],
        out_shape=[jax.ShapeDtypeStruct((t, D), F32), jax.ShapeDtypeStruct((t, D), BF16), jax.ShapeDtypeStruct((t, D), BF16)],
        scratch_shapes=[pltpu.VMEM((tm, D), F32)],
        compiler_params=_cparams(("arbitrary",)),
    )(x, u, u, u, u, cw, qg, kg, sinks, gb, wco, wao, wout, *consts)


def mixer_bwd(dout, u, ya, yb, cw, qg, kg, sinks, gb, wco, wao, wout, c, name):
    t = dout.shape[0]
    tm = min(256, t)
    n_tiles = t // tm
    nb = tm // BLK
    kb = tm + BLK
    cn = sorted(c)

    def body(dout_ref, u_ref, ukv_prev, uvc_prev, ucc_prev, ya_ref, yb_ref, cw_ref, qg_ref, kg_ref, sinks_ref, gb_ref,
             wco_ref, wao_ref, wout_ref, *rest):
        cref = dict(zip(cn, rest[:len(cn)]))
        (du_ref, small_ref, merged_ref, yc_ref, ob_ref, dya_ref, dyb_ref,
         o_scr, dq_scr, dk4_scr, dv4_scr, carry_kv, carry_conv) = rest[len(cn):]
        g = pl.program_id(0)
        is_first = g == n_tiles - 1

        @pl.when(g == 0)
        def _():
            carry_kv[...] = jnp.zeros_like(carry_kv)
            carry_conv[...] = jnp.zeros_like(carry_conv)
            small_ref[...] = jnp.zeros_like(small_ref)

        dout = dout_ref[...]
        dout_b = dout.astype(BF16)
        ya_v = ya_ref[...].astype(F32)
        yb_v = yb_ref[...].astype(F32)
        g_a = _sigmoid(_bf(u_ref, C_GA, D) + gb_ref[:, 0:D])
        g_b = _sigmoid(_bf(u_ref, C_GB, D) + gb_ref[:, D:2 * D])
        merged = g_a * ya_v + g_b * yb_v
        dmerged = _dot_nt(dout_b, wout_ref[...])
        merged_ref[...] = merged.astype(BF16)
        dya = dmerged * g_a
        dyb = dmerged * g_b
        dgl_a = dmerged * ya_v * g_a * (1.0 - g_a)
        dgl_b = dmerged * yb_v * g_b * (1.0 - g_b)
        du_ref[:, C_GA:C_GA + D] = dgl_a.astype(BF16)
        du_ref[:, C_GB:C_GB + D] = dgl_b.astype(BF16)
        small_ref[SM_GATE:SM_GATE + 1, 0:D] += jnp.sum(dgl_a, axis=0, keepdims=True)
        small_ref[SM_GATE:SM_GATE + 1, D:2 * D] += jnp.sum(dgl_b, axis=0, keepdims=True)

        p, p1, p2, conv = _conv_fwd(u_ref, uvc_prev, ucc_prev, cw_ref, is_first, tm)
        zc = _bf(u_ref, C_ZC, D)
        bc = _bf(u_ref, C_BC, D)
        sg = _sigmoid(zc)
        sc = zc * sg
        yc = bc * conv * sc
        dya_b = dya.astype(BF16)
        yc_ref[...] = yc.astype(BF16)
        dya_ref[...] = dya_b
        dyc = _dot_nt(dya_b, wco_ref[...])
        du_ref[:, C_BC:C_BC + D] = (dyc * conv * sc).astype(BF16)
        du_ref[:, C_ZC:C_ZC + D] = (dyc * bc * conv * (sg * (1.0 + zc * (1.0 - sg)))).astype(BF16)
        dconv = dyc * bc * sc
        small_ref[SM_CONV + 2:SM_CONV + 3, 0:D] += jnp.sum(dconv * p, axis=0, keepdims=True)
        small_ref[SM_CONV + 1:SM_CONV + 2, 0:D] += jnp.sum(dconv * p1, axis=0, keepdims=True)
        small_ref[SM_CONV:SM_CONV + 1, 0:D] += jnp.sum(dconv * p2, axis=0, keepdims=True)
        row = lax.broadcasted_iota(jnp.int32, (tm, 1), 0)
        nxt = carry_conv[...]
        d1 = jnp.where(row == tm - 1, nxt[0:1, :], pltpu.roll(dconv, tm - 1, 0))
        d2 = jnp.where(row == tm - 1, nxt[1:2, :], jnp.where(row == tm - 2, nxt[0:1, :], pltpu.roll(dconv, tm - 2, 0)))
        carry_conv[...] = dconv[0:8, :]
        cw = cw_ref[...]
        dp = cw[2:3, :] * dconv + cw[1:2, :] * d1 + cw[0:1, :] * d2
        du_ref[:, C_CC:C_CC + D] = (dp * _bf(u_ref, C_VC, D)).astype(BF16)
        du_ref[:, C_VC:C_VC + D] = (dp * _bf(u_ref, C_CC, D)).astype(BF16)

        dyb_b = dyb.astype(BF16)
        dob = _dot_nt(dyb_b, wao_ref[...])
        za = _bf(u_ref, C_ZA, D)
        sga = _sigmoid(za)
        sa = za * sga
        do = dob * sa
        qhat, rq, qn, khat, rk, kt, vt = _attn_inputs(u_ref, ukv_prev, qg_ref, kg_ref, cref, tm)
        valid, valid_first, lane_grp, row_grp = _attn_masks(is_first)
        dk4_scr[...] = jnp.zeros_like(dk4_scr)
        dv4_scr[...] = jnp.zeros_like(dv4_scr)
        lane16 = lax.broadcasted_iota(jnp.int32, (1, 2 * D), 1)
        dsink_row = jnp.zeros((1, 2 * D), F32)
        for h in range(N_KV):
            sink = _sink_col(sinks_ref, h, row_grp)
            dsink_col = jnp.zeros((GROUP * BLK, 1), F32)
            for b in range(nb):
                rows = slice(b * BLK, (b + 1) * BLK)
                band = slice(b * BLK, (b + 2) * BLK)
                cols = slice(h * KVW, (h + 1) * KVW)
                qs = _stack_groups(qn[rows, cols], lane_grp)
                pn, ps = _softmax_block(qs, kt[h][band], valid_first if b == 0 else valid, sink)
                pn_b = pn.astype(BF16)
                o4 = _dot(pn_b, vt[h][band])
                o_scr[rows, cols] = _unstack_groups(o4, lane_grp)
                dos = _stack_groups(do[rows, cols], lane_grp).astype(BF16)
                dpn = _dot_nt(dos, vt[h][band])
                delta = jnp.sum(pn * dpn, axis=-1, keepdims=True)
                ds = (pn * (dpn - delta)).astype(BF16)
                dsink_col = dsink_col - ps * delta
                dq_scr[rows, cols] = _unstack_groups(_dot(ds, kt[h][band]), lane_grp)
                dk4_scr[h, band, :] += _dot_tn(ds, qs)
                dv4_scr[h, band, :] += _dot_tn(pn_b, dos)
            for gi in range(GROUP):
                tot = jnp.sum(dsink_col[gi * BLK:(gi + 1) * BLK, :], axis=0, keepdims=True)
                dsink_row = dsink_row + jnp.where(lane16 == GROUP * h + gi, tot, 0.0)
        small_ref[SM_SINK:SM_SINK + 1, :] += dsink_row

        o = o_scr[...]
        ob_ref[...] = (o * sa).astype(BF16)
        dyb_ref[...] = dyb_b
        du_ref[:, C_ZA:C_ZA + D] = (dob * o * (sga * (1.0 + za * (1.0 - sga)))).astype(BF16)

        dqn = dq_scr[...]
        small_ref[SM_QG:SM_QG + 1, 0:D] += SCALE * jnp.sum(dqn * qhat, axis=0, keepdims=True)
        dqh = dqn * qg_ref[...]
        dq = rq * (dqh - qhat * _head_mean(dqh * qhat, cref["sel_q"][...], cref["exp_q"][...]))
        du_ref[:, C_Q:C_Q + D] = dq.astype(BF16)

        dkn_band = jnp.zeros((kb, KVW), F32)
        dv_band = jnp.zeros((kb, KVW), F32)
        for h in range(N_KV):
            dkn_band = dkn_band + _dot2(dk4_scr[h], cref["rep_t"][h])
            dv_band = dv_band + _dot2(dv4_scr[h], cref["rep_t"][h])
        carried = carry_kv[...]
        pad = jnp.zeros((tm - BLK, KVW), F32)
        if nb > 1:
            dkn = dkn_band[BLK:, :] + jnp.concatenate([pad, carried[:, 0:KVW]], axis=0)
            dv = dv_band[BLK:, :] + jnp.concatenate([pad, carried[:, KVW:2 * KVW]], axis=0)
        else:
            dkn = dkn_band[BLK:, :] + carried[:, 0:KVW]
            dv = dv_band[BLK:, :] + carried[:, KVW:2 * KVW]
        carry_kv[:, 0:KVW] = dkn_band[0:BLK, :]
        carry_kv[:, KVW:2 * KVW] = dv_band[0:BLK, :]
        khat_t = khat[BLK:, :]
        small_ref[SM_KG:SM_KG + 1, 0:KVW] += jnp.sum(dkn * khat_t, axis=0, keepdims=True)
        dkh = dkn * kg_ref[...]
        dk = rk[BLK:, :] * (dkh - khat_t * _head_mean(dkh * khat_t, cref["sel_k"][...], cref["exp_k"][...]))
        du_ref[:, C_K:C_K + KVW] = dk.astype(BF16)
        du_ref[:, C_V:C_V + KVW] = dv.astype(BF16)

    rev = lambda g: n_tiles - 1 - g
    u_spec, ukv_prev, uvc_prev, ucc_prev = _mixer_specs(t, tm, n_tiles, rev)
    tok = pl.BlockSpec((tm, D), lambda g: (rev(g), 0))
    consts = [c[k] for k in cn]
    wspec = _full((D, D))
    return pl.pallas_call(
        body, name=name, grid=(n_tiles,),
        in_specs=[tok, u_spec, ukv_prev, uvc_prev, ucc_prev, tok, tok, _full((8, D)), _full((1, D)), _full((1, KVW)),
                  pl.BlockSpec(memory_space=pltpu.SMEM), _full((1, 2 * D)), wspec, wspec, wspec]
                 + [_full(a.shape) for a in consts],
        out_specs=[pl.BlockSpec((tm, IN_COLS), lambda g: (rev(g), 0)), _full((SM_ROWS, 2 * D))] + [tok] * 5,
        out_shape=[jax.ShapeDtypeStruct((t, IN_COLS), BF16), jax.ShapeDtypeStruct((SM_ROWS, 2 * D), F32)]
                  + [jax.ShapeDtypeStruct((t, D), BF16)] * 5,
        scratch_shapes=[pltpu.VMEM((tm, D), F32), pltpu.VMEM((tm, D), F32),
                        pltpu.VMEM((N_KV, kb, KVW), F32), pltpu.VMEM((N_KV, kb, KVW), F32),
                        pltpu.VMEM((BLK, 2 * KVW), F32), pltpu.VMEM((8, D), F32)],
        compiler_params=_cparams(("arbitrary",)),
    )(dout, u, u, u, u, ya, yb, cw, qg, kg, sinks, gb, wco, wao, wout, *consts)


def matmul_tn(a, b, name):
    t, m = a.shape
    tk = min(1024, t)
    mb = 2176 if m == IN_COLS else m
    nk = t // tk

    def body(a_ref, b_ref, o_ref, acc):
        k = pl.program_id(1)
        prod = _dot_tn(a_ref[...].astype(BF16), b_ref[...].astype(BF16))

        @pl.when(k == 0)
        def _():
            acc[...] = prod

        @pl.when(k > 0)
        def _():
            acc[...] += prod

        @pl.when(k == nk - 1)
        def _():
            o_ref[...] = acc[...].astype(BF16)

    return pl.pallas_call(
        body, name=name, grid=(m // mb, nk),
        in_specs=[pl.BlockSpec((tk, mb), lambda j, k: (k, j)), pl.BlockSpec((tk, D), lambda j, k: (k, 0))],
        out_specs=pl.BlockSpec((mb, D), lambda j, k: (j, 0)),
        out_shape=jax.ShapeDtypeStruct((m, D), BF16),
        scratch_shapes=[pltpu.VMEM((mb, D), F32)],
        compiler_params=_cparams(("arbitrary", "arbitrary")),
    )(a, b)


def inproj_bwd_x(du, w, x, ng, dout, name):
    t = x.shape[0]
    tm = min(1024, t)
    kc = 2176
    nk = IN_COLS // kc

    def body(du_ref, w_ref, x_ref, ng_ref, dout_ref, dx_ref, dng_ref, acc):
        i = pl.program_id(0)
        k = pl.program_id(1)
        prod = _dot(du_ref[...], w_ref[...])

        @pl.when(k == 0)
        def _():
            acc[...] = prod

        @pl.when(k > 0)
        def _():
            acc[...] += prod

        @pl.when((i == 0) & (k == 0))
        def _():
            dng_ref[...] = jnp.zeros_like(dng_ref)

        @pl.when(k == nk - 1)
        def _():
            dh = acc[...]
            xf = x_ref[...]
            r = lax.rsqrt(jnp.mean(xf * xf, axis=-1, keepdims=True) + EPS)
            xhat = xf * r
            dng_ref[0:1, :] += jnp.sum(dh * xhat, axis=0, keepdims=True)
            dxh = dh * ng_ref[...]
            dx_ref[...] = dout_ref[...] + r * (dxh - xhat * jnp.mean(dxh * xhat, axis=-1, keepdims=True))

    tok = pl.BlockSpec((tm, D), lambda i, k: (i, 0))
    return pl.pallas_call(
        body, name=name, grid=(t // tm, nk),
        in_specs=[pl.BlockSpec((tm, kc), lambda i, k: (i, k)), pl.BlockSpec((kc, D), lambda i, k: (k, 0)), tok,
                  pl.BlockSpec((1, D), lambda i, k: (0, 0)), tok],
        out_specs=[tok, pl.BlockSpec((8, D), lambda i, k: (0, 0))],
        out_shape=[jax.ShapeDtypeStruct((t, D), F32), jax.ShapeDtypeStruct((8, D), F32)],
        scratch_shapes=[pltpu.VMEM((tm, D), F32)],
        compiler_params=_cparams(("arbitrary", "arbitrary")),
    )(du, w, x, ng, dout)


def loss_head(y, target, name):
    t = y.shape[0]
    tm = min(1024, t)

    def body(y_ref, t_ref, dy_ref, loss_ref):
        @pl.when(pl.program_id(0) == 0)
        def _():
            loss_ref[...] = jnp.zeros_like(loss_ref)
        err = y_ref[...] - t_ref[...]
        dy_ref[...] = err * (1.0 / D)
        part = jnp.sum(jnp.sum(err * err, axis=-1, keepdims=True) * (1.0 / D), axis=0, keepdims=True)
        loss_ref[...] += 0.5 * part

    tok = pl.BlockSpec((tm, D), lambda i: (i, 0))
    return pl.pallas_call(
        body, name=name, grid=(t // tm,), in_specs=[tok, tok],
        out_specs=[tok, pl.BlockSpec((8, 128), lambda i: (0, 0))],
        out_shape=[jax.ShapeDtypeStruct((t, D), F32), jax.ShapeDtypeStruct((8, 128), F32)],
        compiler_params=_cparams(("arbitrary",)),
    )(y, target)


def layer_operands(l, norm_g, conv_w_full, q_norm_g, k_norm_g, sinks, gate_b, w_in_b, wco_b, wao_b, wout_b):
    return dict(
        ng=norm_g[l][None, :], cw=jnp.pad(conv_w_full[l], ((0, 5), (0, 0))),
        qg=jnp.tile(q_norm_g[l] * SCALE, D // HEAD)[None, :], kg=jnp.tile(k_norm_g[l], N_KV)[None, :],
        sinks=sinks[l][None, :], gb=gate_b[l][None, :],
        w_in=w_in_b[l], wco=wco_b[l], wao=wao_b[l], wout=wout_b[l])


def layer_bwd(dout, saved, lw, c, l, send_off):
    x, u, h, ya, yb = saved
    du, small, merged, yc, ob, dya, dyb = mixer_bwd(dout, u, ya, yb, lw["cw"], lw["qg"], lw["kg"], lw["sinks"], lw["gb"],
                                                    lw["wco"], lw["wao"], lw["wout"], c, f"mixer_bwd_{l}")
    grads = dict(
        wout=matmul_tn(merged, dout, f"dw_out_{l}"), wco=matmul_tn(yc, dya, f"dw_conv_out_{l}"),
        wao=matmul_tn(ob, dyb, f"dw_attn_out_{l}"), w_in=matmul_tn(du, h, f"dw_in_{l}"), small=small)
    token = send_off(grads)
    dx, grads["dng"] = inproj_bwd_x(du, lw["w_in"], x, lw["ng"] + token[0:1, 0:1], dout, f"inproj_bwd_{l}")
    return dx, grads


MESH = pl.DeviceIdType.MESH
ANY = pl.BlockSpec(memory_space=pl.ANY)


def _place():
    return lax.axis_index("x"), lax.axis_index("y"), lax.axis_index("c")


def all_gather(arrs, name):
    n = len(arrs)

    def body(*refs):
        ins, outs = refs[:n], refs[n:2 * n]
        send_sems, recv_sems, local_sems = refs[2 * n:]
        x, y, c = _place()
        me, sibling = (x, y, c), (x, y, 1 - c)
        chips = [(1 - x, y), (x, 1 - y), (1 - x, 1 - y)]

        def slot(a, block):
            px, py, pc = block
            return outs[a].at[4 * px + 2 * py + pc]

        def copy(a, k, block, to, src=None):
            return pltpu.make_async_remote_copy(
                src_ref=slot(a, block) if src is None else src, dst_ref=slot(a, block),
                send_sem=send_sems.at[a, k], recv_sem=recv_sems.at[a, k], device_id=to, device_id_type=MESH)

        mine = [pltpu.make_async_copy(ins[a], slot(a, me), local_sems.at[a]) for a in range(n)]
        for cp in mine:
            cp.start()
        first = []
        for a in range(n):
            first.append(copy(a, 0, me, sibling, src=ins[a]))
            first += [copy(a, 1 + j, me, (*chip, c), src=ins[a]) for j, chip in enumerate(chips)]
        for cp in first:
            cp.start()
        passed = []
        for j, chip in enumerate(chips):
            for a in range(n):
                copy(a, 1 + j, (*chip, c), me).wait_recv()
                passed.append(copy(a, 4 + j, (*chip, c), sibling))
                passed[-1].start()
        for a in range(n):
            copy(a, 0, sibling, me).wait_recv()
            for j, chip in enumerate(chips):
                copy(a, 4 + j, (*chip, 1 - c), me).wait_recv()
        for cp in first + passed:
            cp.wait_send()
        for cp in mine:
            cp.wait()

    return pl.pallas_call(
        body, name=name, in_specs=[ANY] * n, out_specs=[ANY] * n,
        out_shape=[jax.ShapeDtypeStruct((N_DEV,) + a.shape, a.dtype) for a in arrs],
        scratch_shapes=[pltpu.SemaphoreType.DMA((n, 7)), pltpu.SemaphoreType.DMA((n, 7)), pltpu.SemaphoreType.DMA((n,))],
    )(*arrs)


HBM_SPEC = pl.BlockSpec(memory_space=pltpu.HBM)
SEM_SPEC = pl.BlockSpec(memory_space=pltpu.SEMAPHORE)
EFFECT = pltpu.SideEffectType.DATAFLOW_SIDE_EFFECTING


ALL_PEERS = (1, 2, 3, 4, 5, 6, 7)
SAME_CORE_AND_SIBLING = (1, 2, 4, 6)


def _flip(place, k):
    x, y, c = place
    return (1 - x if k & 4 else x, 1 - y if k & 2 else y, 1 - c if k & 1 else c)


def _slot(place):
    return 4 * place[0] + 2 * place[1] + place[2]


def _exchange_copies(ins, lands, send_sems, recv_sems, scatter, flips, arriving):
    me = _place()
    out = []
    for a in range(len(ins)):
        for i, k in enumerate(flips):
            peer = _flip(me, k)
            out.append(pltpu.make_async_remote_copy(
                src_ref=ins[a].at[_slot(peer)] if scatter else ins[a],
                dst_ref=lands[a].at[_slot(peer) if arriving else _slot(me)],
                send_sem=send_sems.at[a * len(flips) + i], recv_sem=recv_sems.at[a * len(flips) + i],
                device_id=peer, device_id_type=MESH))
    return out


def exchange_start(arrs, scatter, flips, after, name):
    n = len(arrs)
    lands = [lax.empty(a.shape if scatter else (N_DEV,) + a.shape, a.dtype) for a in arrs]

    def body(*refs):
        ins, lz = refs[:n], refs[n:2 * n]
        send_sems, recv_sems = refs[2 * n + 1], refs[2 * n + 2]
        token = refs[-1]
        for cp in _exchange_copies(ins, lz, send_sems, recv_sems, scatter, flips, False):
            cp.start()
        token[...] = jnp.zeros_like(token)

    sems = pltpu.SemaphoreType.DMA((n * len(flips),))
    res = pl.pallas_call(
        body, name=name,
        out_shape=(sems, sems, *[pltpu.HBM(a.shape, a.dtype) for a in arrs], *[pltpu.HBM(a.shape, a.dtype) for a in lands],
                   jax.ShapeDtypeStruct((8, 128), F32)),
        in_specs=[HBM_SPEC] * (2 * n) + [ANY],
        out_specs=(SEM_SPEC, SEM_SPEC, *[HBM_SPEC] * (2 * n), pl.BlockSpec(memory_space=pltpu.VMEM)),
        input_output_aliases={i: 2 + i for i in range(2 * n)},
        compiler_params=pltpu.CompilerParams(has_side_effects=EFFECT),
    )(*[pltpu.with_memory_space_constraint(a, pltpu.HBM) for a in arrs],
      *[pltpu.with_memory_space_constraint(a, pltpu.HBM) for a in lands], after)
    return dict(send=res[0], recv=res[1], srcs=res[2:2 + n], lands=res[2 + n:2 + 2 * n], token=res[-1], scatter=scatter,
                flips=flips)


def exchange_wait(state, after, name):
    n = len(state["srcs"])

    def body(*refs):
        ins, lz = refs[:n], refs[n:2 * n]
        send_sems, recv_sems = refs[2 * n], refs[2 * n + 1]
        for cp in _exchange_copies(ins, lz, send_sems, recv_sems, state["scatter"], state["flips"], True):
            cp.wait_send()
            cp.wait_recv()

    both = list(state["srcs"]) + list(state["lands"])
    res = pl.pallas_call(
        body, name=name, out_shape=tuple(pltpu.HBM(a.shape, a.dtype) for a in both),
        in_specs=[HBM_SPEC] * (2 * n) + [SEM_SPEC, SEM_SPEC, ANY], out_specs=tuple([HBM_SPEC] * (2 * n)),
        input_output_aliases={i: i for i in range(2 * n)},
        compiler_params=pltpu.CompilerParams(has_side_effects=EFFECT),
    )(*both, state["send"], state["recv"], after)
    return res[:n], res[n:]


def gather_finish(mine, lands, name):
    n = len(lands)
    others = (2, 4, 6)

    def body(*refs):
        srcs, lz = refs[:n], refs[n:2 * n]
        send_sems, recv_sems, local_sems = refs[3 * n:]
        me = _place()
        sibling = _flip(me, 1)
        own = [pltpu.make_async_copy(srcs[a], lz[a].at[_slot(me)], local_sems.at[a]) for a in range(n)]
        for cp in own:
            cp.start()

        def copies(core_of_origin):
            return [pltpu.make_async_remote_copy(
                src_ref=lz[a].at[_slot(_flip(core_of_origin, k))], dst_ref=lz[a].at[_slot(_flip(core_of_origin, k))],
                send_sem=send_sems.at[a * len(others) + i], recv_sem=recv_sems.at[a * len(others) + i],
                device_id=sibling, device_id_type=MESH) for a in range(n) for i, k in enumerate(others)]

        sends = copies(me)
        for cp in sends:
            cp.start()
        for cp in copies(sibling):
            cp.wait_recv()
        for cp in sends:
            cp.wait_send()
        for cp in own:
            cp.wait()

    return pl.pallas_call(
        body, name=name, in_specs=[ANY] * (2 * n), out_specs=[ANY] * n,
        out_shape=[jax.ShapeDtypeStruct(a.shape, a.dtype) for a in lands],
        input_output_aliases={n + i: i for i in range(n)},
        scratch_shapes=[pltpu.SemaphoreType.DMA((n * len(others),)), pltpu.SemaphoreType.DMA((n * len(others),)),
                        pltpu.SemaphoreType.DMA((n,))],
    )(*mine, *lands)


def adamw(w, m, v, parts, name):
    r, cdim = w.shape
    n_parts = parts.shape[0]
    rb = 256 if r % 256 == 0 else (SHARD_COLS // 4 if r == SHARD_COLS else r)

    def body(w_ref, m_ref, v_ref, p_ref, g_ref, d_ref, mo_ref, vo_ref):
        g = p_ref[0].astype(F32)
        for i in range(1, n_parts):
            g = g + p_ref[i].astype(F32)
        m_new = ADAM_B1 * m_ref[...] + (1.0 - ADAM_B1) * g
        v_new = ADAM_B2 * v_ref[...] + (1.0 - ADAM_B2) * (g * g)
        m_hat = m_new / (1.0 - ADAM_B1 ** ADAM_STEP)
        v_hat = v_new / (1.0 - ADAM_B2 ** ADAM_STEP)
        g_ref[...] = g
        d_ref[...] = -ADAM_LR * (m_hat / (jnp.sqrt(v_hat) + ADAM_EPS) + ADAM_WD * w_ref[...])
        mo_ref[...] = m_new
        vo_ref[...] = v_new

    blk = pl.BlockSpec((rb, cdim), lambda i: (i, 0))
    return pl.pallas_call(
        body, name=name, grid=(r // rb,),
        in_specs=[blk, blk, blk, pl.BlockSpec((n_parts, rb, cdim), lambda i: (0, i, 0))],
        out_specs=[blk] * 4, out_shape=[jax.ShapeDtypeStruct((r, cdim), F32)] * 4,
        compiler_params=_cparams(("arbitrary",)),
    )(w, m, v, parts)


def adamw_layers(w, m, v, parts, name):
    nl, r, cdim = w.shape
    n_parts = parts[0].shape[0]
    rb = 256 if r % 256 == 0 else (SHARD_COLS // 4 if r == SHARD_COLS else r)
    nblk = r // rb

    def body(w_ref, m_ref, v_ref, *rest):
        p_refs = rest[:nl]
        g_ref, d_ref, mo_ref, vo_ref = rest[nl:]
        for k in range(nl):
            @pl.when(pl.program_id(0) == k)
            def _(k=k):
                g = p_refs[k][0].astype(F32)
                for i in range(1, n_parts):
                    g = g + p_refs[k][i].astype(F32)
                m_new = ADAM_B1 * m_ref[...] + (1.0 - ADAM_B1) * g
                v_new = ADAM_B2 * v_ref[...] + (1.0 - ADAM_B2) * (g * g)
                m_hat = m_new / (1.0 - ADAM_B1 ** ADAM_STEP)
                v_hat = v_new / (1.0 - ADAM_B2 ** ADAM_STEP)
                g_ref[...] = g
                d_ref[...] = -ADAM_LR * (m_hat / (jnp.sqrt(v_hat) + ADAM_EPS) + ADAM_WD * w_ref[...])
                mo_ref[...] = m_new
                vo_ref[...] = v_new

    blk = pl.BlockSpec((None, rb, cdim), lambda l, i: (l, i, 0))

    def part_spec(k):
        return pl.BlockSpec((n_parts, rb, cdim),
                            lambda l, i: (0, jnp.where(l < k, 0, jnp.where(l == k, i, nblk - 1)), 0))

    return pl.pallas_call(
        body, name=name, grid=(nl, nblk),
        in_specs=[blk, blk, blk] + [part_spec(k) for k in range(nl)],
        out_specs=[blk] * 4, out_shape=[jax.ShapeDtypeStruct((nl, r, cdim), F32)] * 4,
        compiler_params=_cparams(("arbitrary", "arbitrary")),
    )(w, m, v, *parts)


def small_sum(parts, fold, name):
    rows = parts.shape[1]

    def dot3(xv, sel):
        out = jnp.zeros((xv.shape[0], sel.shape[1]), F32)
        for _ in range(3):
            hi = xv.astype(BF16)
            out = out + _dot(hi, sel)
            xv = xv - hi.astype(F32)
        return out

    def body(p_ref, fold_ref, o_ref):
        tot = p_ref[0]
        for i in range(1, N_DEV):
            tot = tot + p_ref[i]
        o_ref[...] = tot
        for l in range(rows // SM_ROWS):
            blk = tot[l * SM_ROWS:l * SM_ROWS + 8, 0:D]
            folded = dot3(blk, fold_ref[...])
            o_ref[l * SM_ROWS + 9:l * SM_ROWS + 10, 0:128] = folded[SM_QG:SM_QG + 1, :]
            o_ref[l * SM_ROWS + 10:l * SM_ROWS + 11, 0:128] = folded[SM_KG:SM_KG + 1, :]

    return pl.pallas_call(
        body, name=name, out_shape=jax.ShapeDtypeStruct((rows, 2 * D), F32),
        compiler_params=_cparams(None),
    )(parts, fold)


def kernel(x, norm_g, w_in, conv_w, q_norm_g, k_norm_g, sinks, w_conv_out, w_attn_out, gate_b, w_out, loss_target, m_norm_g, m_w_in, m_conv_w, m_q_norm_g, m_k_norm_g, m_sinks, m_w_conv_out, m_w_attn_out, m_gate_b, m_w_out, v_norm_g, v_w_in, v_conv_w, v_q_norm_g, v_k_norm_g, v_sinks, v_w_conv_out, v_w_attn_out, v_gate_b, v_w_out):
    c = _selectors()
    me = 4 * lax.axis_index("x") + 2 * lax.axis_index("y") + lax.axis_index("c")

    own = lambda land, mine: lax.dynamic_update_index_in_dim(land, mine, me, 0)
    w_in_t, m_w_in_t, v_w_in_t = (jnp.swapaxes(a, 1, 2) for a in (w_in, m_w_in, v_w_in))

    def shards(l):
        return [w_in_t[l].astype(BF16), w_conv_out[l].astype(BF16), w_attn_out[l].astype(BF16), w_out[l].astype(BF16)]

    def gather_start(arrs, after, tag):
        return exchange_start(arrs, False, SAME_CORE_AND_SIBLING, after, f"gather_start_{tag}")

    def gather_end(state, after, tag):
        mine, lands = exchange_wait(state, after, f"gather_wait_{tag}")
        return mine, gather_finish(mine, lands, f"gather_finish_{tag}")

    h = x[0]
    saved, lws = [], []
    first = shards(0)
    g_in = gather_start(first[:1], h, "0_in")
    g_rest = gather_start(first[1:] + [conv_w], g_in["token"], "0_rest")
    mine, (w_in_full,) = gather_end(g_in, g_rest["token"], "0_in")
    gather = gather_start(shards(1), mine[0], 1)
    for l in range(DEPTH):
        if l == 0:
            ng = norm_g[0][None, :] + gather["token"][0:1, 0:1]
            u, hb = inproj_fwd(h, ng, w_in_full.reshape(IN_COLS, D), "inproj_fwd_0")
            _, rest = gather_end(g_rest, u, "0_rest")
            conv_full = jnp.transpose(rest[3], (1, 2, 0, 3)).reshape(DEPTH, 3, D)
            lands = [w_in_full] + list(rest[:3])
        else:
            mine, lands = gather_end(gather, h, l)
            if l + 1 < DEPTH:
                gather = gather_start(shards(l + 1), mine[0], l + 1)
        lws.append(layer_operands(l, norm_g, conv_full, q_norm_g, k_norm_g, sinks, gate_b,
                                  {l: lands[0].reshape(IN_COLS, D)}, {l: lands[1].reshape(D, D)},
                                  {l: lands[2].reshape(D, D)}, {l: lands[3].reshape(D, D)}))
        if l > 0:
            u, hb = inproj_fwd(h, lws[l]["ng"] + gather["token"][0:1, 0:1], lws[l]["w_in"], f"inproj_fwd_{l}")
        x_in = h
        h, ya, yb = mixer_fwd(x_in, u, lws[l]["cw"], lws[l]["qg"], lws[l]["kg"], lws[l]["sinks"], lws[l]["gb"],
                              lws[l]["wco"], lws[l]["wao"], lws[l]["wout"], c, f"mixer_fwd_{l}")
        saved.append((x_in, u, hb, ya, yb))
    dh, loss_part = loss_head(h, loss_target[0], "loss_head")

    grads, scatters = [None] * DEPTH, [None] * DEPTH
    for l in reversed(range(DEPTH)):
        def send_off(g, l=l):
            srcs = [g["w_in"].reshape(N_DEV, SHARD_COLS, D)] + [g[k].reshape(N_DEV, SHARD_ROWS, D) for k in ("wco", "wao", "wout")]
            scatters[l] = exchange_start(srcs, True, ALL_PEERS, g["small"], f"scatter_start_{l}")
            return scatters[l]["token"]
        dh, grads[l] = layer_bwd(dh, saved[l], lws[l], c, l, send_off)
    parts = [None] * DEPTH
    for l in reversed(range(DEPTH)):
        mine, lands = exchange_wait(scatters[l], dh, f"scatter_wait_{l}")
        parts[l] = [own(land, lax.dynamic_index_in_dim(src, me, 0, keepdims=False)) for land, src in zip(lands, mine)]

    u_in = [jnp.swapaxes(o, 1, 2) for o in
            adamw_layers(w_in_t, m_w_in_t, v_w_in_t, [parts[l][0] for l in range(DEPTH)], "adamw_w_in")]
    u_co = adamw_layers(w_conv_out, m_w_conv_out, v_w_conv_out, [parts[l][1] for l in range(DEPTH)], "adamw_w_conv_out")
    u_ao = adamw_layers(w_attn_out, m_w_attn_out, v_w_attn_out, [parts[l][2] for l in range(DEPTH)], "adamw_w_attn_out")
    u_out = adamw_layers(w_out, m_w_out, v_w_out, [parts[l][3] for l in range(DEPTH)], "adamw_w_out")

    blocks = []
    for l in range(DEPTH):
        blk = grads[l]["small"]
        blk = blk.at[SM_NORM, 0:D].set(grads[l]["dng"][0])
        if l == 0:
            blk = blk.at[SM_LOSS, 0:128].set(loss_part[0])
        blocks.append(blk)
    tot = small_sum(all_gather([jnp.concatenate(blocks, axis=0)], "gather_small")[0], c["fold"], "small_sum")
    tot = tot.reshape(DEPTH, SM_ROWS, 2 * D)
    loss = tot[0, SM_LOSS, 0]

    def update_small(w, m, v, g, name):
        return adamw(w, m, v, g[None], name)

    u_ng = update_small(norm_g, m_norm_g, v_norm_g, tot[:, SM_NORM, 0:D], "adamw_norm_g")
    u_qg = update_small(q_norm_g, m_q_norm_g, v_q_norm_g, tot[:, 9, 0:HEAD], "adamw_q_norm_g")
    u_kg = update_small(k_norm_g, m_k_norm_g, v_k_norm_g, tot[:, 10, 0:HEAD], "adamw_k_norm_g")
    u_sk = update_small(sinks, m_sinks, v_sinks, tot[:, SM_SINK, 0:16], "adamw_sinks")
    u_gb = update_small(gate_b, m_gate_b, v_gate_b, tot[:, SM_GATE, :], "adamw_gate_b")
    g_conv = lax.dynamic_slice_in_dim(tot[:, SM_CONV:SM_CONV + 3, 0:D], me * SHARD_ROWS, SHARD_ROWS, axis=2)
    u_cw = [o.reshape(DEPTH, 3, SHARD_ROWS) for o in update_small(
        conv_w.reshape(DEPTH * 3, SHARD_ROWS), m_conv_w.reshape(DEPTH * 3, SHARD_ROWS),
        v_conv_w.reshape(DEPTH * 3, SHARD_ROWS), g_conv.reshape(DEPTH * 3, SHARD_ROWS), "adamw_conv_w")]

    order = [u_ng, u_in, u_cw, u_qg, u_kg, u_sk, u_co, u_ao, u_gb, u_out]
    return (loss, dh[None], *[u[0] for u in order], *[u[1] for u in order], *[u[2] for u in order], *[u[3] for u in order])
```

```python
import functools

import jax
import jax.numpy as jnp
from jax import lax
from jax.experimental import pallas as pl
from jax.experimental.pallas import tpu as pltpu

F32 = jnp.float32
BF16 = jnp.bfloat16

N_DEV = 8
DEPTH = 4
D = 1024
N_KV = 4
GROUP = 4
HEAD = 64
BLK = 128
KVW = N_KV * HEAD
IN_COLS = 8704
SHARD_COLS = IN_COLS // N_DEV
SHARD_ROWS = D // N_DEV
C_VC, C_BC, C_CC, C_ZC, C_Q, C_K, C_V, C_ZA, C_GA, C_GB = 0, 1024, 2048, 3072, 4096, 5120, 5376, 5632, 6656, 7680
EPS = 1e-6
NEG_INF = -1e30
SCALE = HEAD ** -0.5

ADAM_LR = 0.001
ADAM_B1 = 0.9
ADAM_B2 = 0.999
ADAM_EPS = 1e-08
ADAM_WD = 0.01
ADAM_STEP = 10

VMEM_LIMIT = 60 * 1024 * 1024
SM_ROWS = 16
SM_GATE, SM_CONV, SM_QG, SM_KG, SM_SINK, SM_NORM, SM_LOSS = 0, 1, 4, 5, 6, 7, 8


def _cparams(sem):
    return pltpu.CompilerParams(dimension_semantics=sem, vmem_limit_bytes=VMEM_LIMIT)


def _dot(a, b):
    return jnp.dot(a, b, preferred_element_type=F32)


def _dot_nt(a, b):
    return lax.dot_general(a, b, (((1,), (1,)), ((), ())), preferred_element_type=F32)


def _dot_tn(a, b):
    return lax.dot_general(a, b, (((0,), (0,)), ((), ())), preferred_element_type=F32)


def _dot2(x, sel):
    hi = x.astype(BF16)
    lo = (x - hi.astype(F32)).astype(BF16)
    return _dot(hi, sel) + _dot(lo, sel)


def _sigmoid(z):
    return 1.0 / (1.0 + jnp.exp(-z))


def _head_mean(t, sel, exp):
    return _dot2(_dot2(t, sel) * (1.0 / HEAD), exp)


def _bf(ref, c0, width):
    return ref[:, c0:c0 + width].astype(F32)


def _selectors():
    c = jnp.arange(D)
    sel_q = (c[:, None] // HEAD == jnp.arange(128)[None, :]).astype(BF16)
    ck = jnp.arange(KVW)
    sel_k = (ck[:, None] // HEAD == jnp.arange(128)[None, :]).astype(BF16)
    src = jnp.arange(KVW)[:, None]
    dst = jnp.arange(KVW)[None, :]
    rep = jnp.stack([((src // HEAD == h) & (src % HEAD == dst % HEAD)).astype(BF16) for h in range(N_KV)])
    fold = (c[:, None] % HEAD == jnp.arange(128)[None, :]).astype(BF16)
    return dict(sel_q=sel_q, exp_q=sel_q.T, sel_k=sel_k, exp_k=sel_k.T, rep=rep, rep_t=jnp.swapaxes(rep, 1, 2), fold=fold)


def inproj_fwd(x, ng, w, name):
    t = x.shape[0]
    tm = min(1024, t)
    cb = 2176
    def body(x_ref, ng_ref, w_ref, u_ref, h_ref, h_scr):
        @pl.when(pl.program_id(1) == 0)
        def _():
            xf = x_ref[...]
            r = lax.rsqrt(jnp.mean(xf * xf, axis=-1, keepdims=True) + EPS)
            hb = (xf * r * ng_ref[...]).astype(BF16)
            h_scr[...] = hb
            h_ref[...] = hb
        u_ref[...] = _dot_nt(h_scr[...], w_ref[...]).astype(BF16)

    return pl.pallas_call(
        body, name=name, grid=(t // tm, IN_COLS // cb),
        in_specs=[pl.BlockSpec((tm, D), lambda i, j: (i, 0)), pl.BlockSpec((1, D), lambda i, j: (0, 0)),
                  pl.BlockSpec((cb, D), lambda i, j: (j, 0))],
        out_specs=[pl.BlockSpec((tm, cb), lambda i, j: (i, j)), pl.BlockSpec((tm, D), lambda i, j: (i, 0))],
        out_shape=[jax.ShapeDtypeStruct((t, IN_COLS), BF16), jax.ShapeDtypeStruct((t, D), BF16)],
        scratch_shapes=[pltpu.VMEM((tm, D), BF16)],
        compiler_params=_cparams(("arbitrary", "arbitrary")),
    )(x, ng, w)


def _conv_fwd(u_ref, uvc_prev, ucc_prev, cw_ref, is_first, tm):
    p = _bf(u_ref, C_CC, D) * _bf(u_ref, C_VC, D)
    pprev = ucc_prev[...].astype(F32) * uvc_prev[...].astype(F32)
    pprev = jnp.where(is_first, 0.0, pprev)
    row = lax.broadcasted_iota(jnp.int32, (tm, 1), 0)
    p1 = jnp.where(row == 0, pprev[15:16, :], pltpu.roll(p, 1, 0))
    p2 = jnp.where(row == 0, pprev[14:15, :], jnp.where(row == 1, pprev[15:16, :], pltpu.roll(p, 2, 0)))
    cw = cw_ref[...]
    conv = cw[0:1, :] * p2 + cw[1:2, :] * p1 + cw[2:3, :] * p
    return p, p1, p2, conv


def _attn_inputs(u_ref, ukv_prev, qg_ref, kg_ref, c, tm):
    q = _bf(u_ref, C_Q, D)
    rq = lax.rsqrt(_head_mean(q * q, c["sel_q"][...], c["exp_q"][...]) + EPS)
    qhat = q * rq
    qn = (qhat * qg_ref[...]).astype(BF16)
    kband = jnp.concatenate([ukv_prev[:, 0:KVW].astype(F32), _bf(u_ref, C_K, KVW)], axis=0)
    rk = lax.rsqrt(_head_mean(kband * kband, c["sel_k"][...], c["exp_k"][...]) + EPS)
    khat = kband * rk
    knb = (khat * kg_ref[...]).astype(BF16)
    vband = jnp.concatenate([ukv_prev[:, KVW:2 * KVW], u_ref[:, C_V:C_V + KVW]], axis=0)
    kt = [_dot(knb, c["rep"][h]).astype(BF16) for h in range(N_KV)]
    vt = [_dot(vband, c["rep"][h]).astype(BF16) for h in range(N_KV)]
    return qhat, rq, qn, khat, rk, kt, vt


def _attn_masks(is_first):
    rows = GROUP * BLK
    r = lax.broadcasted_iota(jnp.int32, (rows, 2 * BLK), 0)
    kk = lax.broadcasted_iota(jnp.int32, (rows, 2 * BLK), 1)
    qq = r % BLK
    valid = (kk > qq) & (kk <= qq + BLK)
    valid_first = valid & ((kk >= BLK) | jnp.logical_not(is_first))
    lane_grp = lax.broadcasted_iota(jnp.int32, (BLK, KVW), 1) // HEAD
    row_grp = lax.broadcasted_iota(jnp.int32, (rows, 1), 0) // BLK
    return valid, valid_first, lane_grp, row_grp


def _sink_col(sinks_ref, h, row_grp):
    col = jnp.full(row_grp.shape, sinks_ref[0, GROUP * h], F32)
    for gi in range(1, GROUP):
        col = jnp.where(row_grp == gi, sinks_ref[0, GROUP * h + gi], col)
    return col


def _stack_groups(a256, lane_grp):
    zero = jnp.zeros_like(a256)
    return jnp.concatenate([jnp.where(lane_grp == gi, a256, zero) for gi in range(GROUP)], axis=0)


def _unstack_groups(a4, lane_grp):
    out = jnp.where(lane_grp == 0, a4[0:BLK], 0.0)
    for gi in range(1, GROUP):
        out = out + jnp.where(lane_grp == gi, a4[gi * BLK:(gi + 1) * BLK], 0.0)
    return out


def _softmax_block(qs, kt_b, valid, sink):
    s = _dot_nt(qs, kt_b)
    s = jnp.where(valid, s, NEG_INF)
    m = jnp.maximum(jnp.max(s, axis=-1, keepdims=True), sink)
    e = jnp.exp(s - m)
    es = jnp.exp(sink - m)
    inv = 1.0 / (jnp.sum(e, axis=-1, keepdims=True) + es)
    return e * inv, es * inv


def _mixer_specs(t, tm, n_tiles, tile_of):
    nb = tm // BLK
    u_spec = pl.BlockSpec((tm, IN_COLS), lambda g: (tile_of(g), 0))
    ukv_prev = pl.BlockSpec((BLK, 2 * KVW), lambda g: (jnp.maximum(tile_of(g) * nb - 1, 0), C_K // (2 * KVW)))
    uvc_prev = pl.BlockSpec((16, D), lambda g: (jnp.maximum(tile_of(g) * (tm // 16) - 1, 0), C_VC // D))
    ucc_prev = pl.BlockSpec((16, D), lambda g: (jnp.maximum(tile_of(g) * (tm // 16) - 1, 0), C_CC // D))
    return u_spec, ukv_prev, uvc_prev, ucc_prev


def _full(shape):
    n = len(shape)
    return pl.BlockSpec(shape, lambda g: (0,) * n)


def mixer_fwd(x, u, cw, qg, kg, sinks, gb, wco, wao, wout, c, name):
    t = x.shape[0]
    tm = min(256, t)
    n_tiles = t // tm
    nb = tm // BLK
    cn = sorted(c)

    def body(x_ref, u_ref, ukv_prev, uvc_prev, ucc_prev, cw_ref, qg_ref, kg_ref, sinks_ref, gb_ref, wco_ref, wao_ref,
             wout_ref, *rest):
        cref = dict(zip(cn, rest[:len(cn)]))
        xo_ref, ya_ref, yb_ref, o_scr = rest[len(cn):]
        is_first = pl.program_id(0) == 0
        _, _, _, conv = _conv_fwd(u_ref, uvc_prev, ucc_prev, cw_ref, is_first, tm)
        zc = _bf(u_ref, C_ZC, D)
        yc = _bf(u_ref, C_BC, D) * conv * (zc * _sigmoid(zc))
        ya = _dot(yc.astype(BF16), wco_ref[...])

        _, _, qn, _, _, kt, vt = _attn_inputs(u_ref, ukv_prev, qg_ref, kg_ref, cref, tm)
        valid, valid_first, lane_grp, row_grp = _attn_masks(is_first)
        for h in range(N_KV):
            sink = _sink_col(sinks_ref, h, row_grp)
            for b in range(nb):
                qs = _stack_groups(qn[b * BLK:(b + 1) * BLK, h * KVW:(h + 1) * KVW], lane_grp)
                pn, _ = _softmax_block(qs, kt[h][b * BLK:(b + 2) * BLK], valid_first if b == 0 else valid, sink)
                o4 = _dot(pn.astype(BF16), vt[h][b * BLK:(b + 2) * BLK])
                o_scr[b * BLK:(b + 1) * BLK, h * KVW:(h + 1) * KVW] = _unstack_groups(o4, lane_grp)
        za = _bf(u_ref, C_ZA, D)
        ob = o_scr[...] * (za * _sigmoid(za))
        yb = _dot(ob.astype(BF16), wao_ref[...])

        g_a = _sigmoid(_bf(u_ref, C_GA, D) + gb_ref[:, 0:D])
        g_b = _sigmoid(_bf(u_ref, C_GB, D) + gb_ref[:, D:2 * D])
        merged = g_a * ya + g_b * yb
        xo_ref[...] = x_ref[...] + _dot(merged.astype(BF16), wout_ref[...])
        ya_ref[...] = ya.astype(BF16)
        yb_ref[...] = yb.astype(BF16)

    u_spec, ukv_prev, uvc_prev, ucc_prev = _mixer_specs(t, tm, n_tiles, lambda g: g)
    tok = pl.BlockSpec((tm, D), lambda g: (g, 0))
    consts = [c[k] for k in cn]
    return pl.pallas_call(
        body, name=name, grid=(n_tiles,),
        in_specs=[tok, u_spec, ukv_prev, uvc_prev, ucc_prev, _full((8, D)), _full((1, D)), _full((1, KVW)),
                  pl.BlockSpec(memory_space=pltpu.SMEM), _full((1, 2 * D)), _full((D, D)), _full((D, D)), _full((D, D))]
                 + [_full(a.shape) for a in consts],
        out_specs=[tok, tok, tok],
        out_shape=[jax.ShapeDtypeStruct((t, D), F32), jax.ShapeDtypeStruct((t, D), BF16), jax.ShapeDtypeStruct((t, D), BF16)],
        scratch_shapes=[pltpu.VMEM((tm, D), F32)],
        compiler_params=_cparams(("arbitrary",)),
    )(x, u, u, u, u, cw, qg, kg, sinks, gb, wco, wao, wout, *consts)


def mixer_bwd(dout, u, ya, yb, cw, qg, kg, sinks, gb, wco, wao, wout, c, name):
    t = dout.shape[0]
    tm = min(256, t)
    n_tiles = t // tm
    nb = tm // BLK
    kb = tm + BLK
    cn = sorted(c)

    def body(dout_ref, u_ref, ukv_prev, uvc_prev, ucc_prev, ya_ref, yb_ref, cw_ref, qg_ref, kg_ref, sinks_ref, gb_ref,
             wco_ref, wao_ref, wout_ref, *rest):
        cref = dict(zip(cn, rest[:len(cn)]))
        (du_ref, small_ref, merged_ref, yc_ref, ob_ref, dya_ref, dyb_ref,
         o_scr, dq_scr, dk4_scr, dv4_scr, carry_kv, carry_conv) = rest[len(cn):]
        g = pl.program_id(0)
        is_first = g == n_tiles - 1

        @pl.when(g == 0)
        def _():
            carry_kv[...] = jnp.zeros_like(carry_kv)
            carry_conv[...] = jnp.zeros_like(carry_conv)
            small_ref[...] = jnp.zeros_like(small_ref)

        dout = dout_ref[...]
        dout_b = dout.astype(BF16)
        ya_v = ya_ref[...].astype(F32)
        yb_v = yb_ref[...].astype(F32)
        g_a = _sigmoid(_bf(u_ref, C_GA, D) + gb_ref[:, 0:D])
        g_b = _sigmoid(_bf(u_ref, C_GB, D) + gb_ref[:, D:2 * D])
        merged = g_a * ya_v + g_b * yb_v
        dmerged = _dot_nt(dout_b, wout_ref[...])
        merged_ref[...] = merged.astype(BF16)
        dya = dmerged * g_a
        dyb = dmerged * g_b
        dgl_a = dmerged * ya_v * g_a * (1.0 - g_a)
        dgl_b = dmerged * yb_v * g_b * (1.0 - g_b)
        du_ref[:, C_GA:C_GA + D] = dgl_a.astype(BF16)
        du_ref[:, C_GB:C_GB + D] = dgl_b.astype(BF16)
        small_ref[SM_GATE:SM_GATE + 1, 0:D] += jnp.sum(dgl_a, axis=0, keepdims=True)
        small_ref[SM_GATE:SM_GATE + 1, D:2 * D] += jnp.sum(dgl_b, axis=0, keepdims=True)

        p, p1, p2, conv = _conv_fwd(u_ref, uvc_prev, ucc_prev, cw_ref, is_first, tm)
        zc = _bf(u_ref, C_ZC, D)
        bc = _bf(u_ref, C_BC, D)
        sg = _sigmoid(zc)
        sc = zc * sg
        yc = bc * conv * sc
        dya_b = dya.astype(BF16)
        yc_ref[...] = yc.astype(BF16)
        dya_ref[...] = dya_b
        dyc = _dot_nt(dya_b, wco_ref[...])
        du_ref[:, C_BC:C_BC + D] = (dyc * conv * sc).astype(BF16)
        du_ref[:, C_ZC:C_ZC + D] = (dyc * bc * conv * (sg * (1.0 + zc * (1.0 - sg)))).astype(BF16)
        dconv = dyc * bc * sc
        small_ref[SM_CONV + 2:SM_CONV + 3, 0:D] += jnp.sum(dconv * p, axis=0, keepdims=True)
        small_ref[SM_CONV + 1:SM_CONV + 2, 0:D] += jnp.sum(dconv * p1, axis=0, keepdims=True)
        small_ref[SM_CONV:SM_CONV + 1, 0:D] += jnp.sum(dconv * p2, axis=0, keepdims=True)
        row = lax.broadcasted_iota(jnp.int32, (tm, 1), 0)
        nxt = carry_conv[...]
        d1 = jnp.where(row == tm - 1, nxt[0:1, :], pltpu.roll(dconv, tm - 1, 0))
        d2 = jnp.where(row == tm - 1, nxt[1:2, :], jnp.where(row == tm - 2, nxt[0:1, :], pltpu.roll(dconv, tm - 2, 0)))
        carry_conv[...] = dconv[0:8, :]
        cw = cw_ref[...]
        dp = cw[2:3, :] * dconv + cw[1:2, :] * d1 + cw[0:1, :] * d2
        du_ref[:, C_CC:C_CC + D] = (dp * _bf(u_ref, C_VC, D)).astype(BF16)
        du_ref[:, C_VC:C_VC + D] = (dp * _bf(u_ref, C_CC, D)).astype(BF16)

        dyb_b = dyb.astype(BF16)
        dob = _dot_nt(dyb_b, wao_ref[...])
        za = _bf(u_ref, C_ZA, D)
        sga = _sigmoid(za)
        sa = za * sga
        do = dob * sa
        qhat, rq, qn, khat, rk, kt, vt = _attn_inputs(u_ref, ukv_prev, qg_ref, kg_ref, cref, tm)
        valid, valid_first, lane_grp, row_grp = _attn_masks(is_first)
        dk4_scr[...] = jnp.zeros_like(dk4_scr)
        dv4_scr[...] = jnp.zeros_like(dv4_scr)
        lane16 = lax.broadcasted_iota(jnp.int32, (1, 2 * D), 1)
        dsink_row = jnp.zeros((1, 2 * D), F32)
        for h in range(N_KV):
            sink = _sink_col(sinks_ref, h, row_grp)
            dsink_col = jnp.zeros((GROUP * BLK, 1), F32)
            for b in range(nb):
                rows = slice(b * BLK, (b + 1) * BLK)
                band = slice(b * BLK, (b + 2) * BLK)
                cols = slice(h * KVW, (h + 1) * KVW)
                qs = _stack_groups(qn[rows, cols], lane_grp)
                pn, ps = _softmax_block(qs, kt[h][band], valid_first if b == 0 else valid, sink)
                pn_b = pn.astype(BF16)
                o4 = _dot(pn_b, vt[h][band])
                o_scr[rows, cols] = _unstack_groups(o4, lane_grp)
                dos = _stack_groups(do[rows, cols], lane_grp).astype(BF16)
                dpn = _dot_nt(dos, vt[h][band])
                delta = jnp.sum(pn * dpn, axis=-1, keepdims=True)
                ds = (pn * (dpn - delta)).astype(BF16)
                dsink_col = dsink_col - ps * delta
                dq_scr[rows, cols] = _unstack_groups(_dot(ds, kt[h][band]), lane_grp)
                dk4_scr[h, band, :] += _dot_tn(ds, qs)
                dv4_scr[h, band, :] += _dot_tn(pn_b, dos)
            for gi in range(GROUP):
                tot = jnp.sum(dsink_col[gi * BLK:(gi + 1) * BLK, :], axis=0, keepdims=True)
                dsink_row = dsink_row + jnp.where(lane16 == GROUP * h + gi, tot, 0.0)
        small_ref[SM_SINK:SM_SINK + 1, :] += dsink_row

        o = o_scr[...]
        ob_ref[...] = (o * sa).astype(BF16)
        dyb_ref[...] = dyb_b
        du_ref[:, C_ZA:C_ZA + D] = (dob * o * (sga * (1.0 + za * (1.0 - sga)))).astype(BF16)

        dqn = dq_scr[...]
        small_ref[SM_QG:SM_QG + 1, 0:D] += SCALE * jnp.sum(dqn * qhat, axis=0, keepdims=True)
        dqh = dqn * qg_ref[...]
        dq = rq * (dqh - qhat * _head_mean(dqh * qhat, cref["sel_q"][...], cref["exp_q"][...]))
        du_ref[:, C_Q:C_Q + D] = dq.astype(BF16)

        dkn_band = jnp.zeros((kb, KVW), F32)
        dv_band = jnp.zeros((kb, KVW), F32)
        for h in range(N_KV):
            dkn_band = dkn_band + _dot2(dk4_scr[h], cref["rep_t"][h])
            dv_band = dv_band + _dot2(dv4_scr[h], cref["rep_t"][h])
        carried = carry_kv[...]
        pad = jnp.zeros((tm - BLK, KVW), F32)
        if nb > 1:
            dkn = dkn_band[BLK:, :] + jnp.concatenate([pad, carried[:, 0:KVW]], axis=0)
            dv = dv_band[BLK:, :] + jnp.concatenate([pad, carried[:, KVW:2 * KVW]], axis=0)
        else:
            dkn = dkn_band[BLK:, :] + carried[:, 0:KVW]
            dv = dv_band[BLK:, :] + carried[:, KVW:2 * KVW]
        carry_kv[:, 0:KVW] = dkn_band[0:BLK, :]
        carry_kv[:, KVW:2 * KVW] = dv_band[0:BLK, :]
        khat_t = khat[BLK:, :]
        small_ref[SM_KG:SM_KG + 1, 0:KVW] += jnp.sum(dkn * khat_t, axis=0, keepdims=True)
        dkh = dkn * kg_ref[...]
        dk = rk[BLK:, :] * (dkh - khat_t * _head_mean(dkh * khat_t, cref["sel_k"][...], cref["exp_k"][...]))
        du_ref[:, C_K:C_K + KVW] = dk.astype(BF16)
        du_ref[:, C_V:C_V + KVW] = dv.astype(BF16)

    rev = lambda g: n_tiles - 1 - g
    u_spec, ukv_prev, uvc_prev, ucc_prev = _mixer_specs(t, tm, n_tiles, rev)
    tok = pl.BlockSpec((tm, D), lambda g: (rev(g), 0))
    consts = [c[k] for k in cn]
    wspec = _full((D, D))
    return pl.pallas_call(
        body, name=name, grid=(n_tiles,),
        in_specs=[tok, u_spec, ukv_prev, uvc_prev, ucc_prev, tok, tok, _full((8, D)), _full((1, D)), _full((1, KVW)),
                  pl.BlockSpec(memory_space=pltpu.SMEM), _full((1, 2 * D)), wspec, wspec, wspec]
                 + [_full(a.shape) for a in consts],
        out_specs=[pl.BlockSpec((tm, IN_COLS), lambda g: (rev(g), 0)), _full((SM_ROWS, 2 * D))] + [tok] * 5,
        out_shape=[jax.ShapeDtypeStruct((t, IN_COLS), BF16), jax.ShapeDtypeStruct((SM_ROWS, 2 * D), F32)]
                  + [jax.ShapeDtypeStruct((t, D), BF16)] * 5,
        scratch_shapes=[pltpu.VMEM((tm, D), F32), pltpu.VMEM((tm, D), F32),
                        pltpu.VMEM((N_KV, kb, KVW), F32), pltpu.VMEM((N_KV, kb, KVW), F32),
                        pltpu.VMEM((BLK, 2 * KVW), F32), pltpu.VMEM((8, D), F32)],
        compiler_params=_cparams(("arbitrary",)),
    )(dout, u, u, u, u, ya, yb, cw, qg, kg, sinks, gb, wco, wao, wout, *consts)


def matmul_tn(a, b, name, after=None):
    t, m = a.shape
    tk = min(1024, t)
    mb = 2176 if m == IN_COLS else m
    nk = t // tk

    def body(a_ref, b_ref, *rest):
        o_ref, acc = rest[-2:]
        k = pl.program_id(1)
        prod = _dot_tn(a_ref[...].astype(BF16), b_ref[...].astype(BF16))

        @pl.when(k == 0)
        def _():
            acc[...] = prod

        @pl.when(k > 0)
        def _():
            acc[...] += prod

        @pl.when(k == nk - 1)
        def _():
            o_ref[...] = acc[...].astype(BF16)

    return pl.pallas_call(
        body, name=name, grid=(m // mb, nk),
        in_specs=[pl.BlockSpec((tk, mb), lambda j, k: (k, j)), pl.BlockSpec((tk, D), lambda j, k: (k, 0))]
                 + ([] if after is None else [ANY]),
        out_specs=pl.BlockSpec((mb, D), lambda j, k: (j, 0)),
        out_shape=jax.ShapeDtypeStruct((m, D), BF16),
        scratch_shapes=[pltpu.VMEM((mb, D), F32)],
        compiler_params=_cparams(("arbitrary", "arbitrary")),
    )(a, b, *([] if after is None else [after]))


def inproj_bwd_x(du, w, x, ng, dout, name):
    t = x.shape[0]
    tm = min(1024, t)
    kc = 2176
    nk = IN_COLS // kc

    def body(du_ref, w_ref, x_ref, ng_ref, dout_ref, dx_ref, dng_ref, acc):
        i = pl.program_id(0)
        k = pl.program_id(1)
        prod = _dot(du_ref[...], w_ref[...])

        @pl.when(k == 0)
        def _():
            acc[...] = prod

        @pl.when(k > 0)
        def _():
            acc[...] += prod

        @pl.when((i == 0) & (k == 0))
        def _():
            dng_ref[...] = jnp.zeros_like(dng_ref)

        @pl.when(k == nk - 1)
        def _():
            dh = acc[...]
            xf = x_ref[...]
            r = lax.rsqrt(jnp.mean(xf * xf, axis=-1, keepdims=True) + EPS)
            xhat = xf * r
            dng_ref[0:1, :] += jnp.sum(dh * xhat, axis=0, keepdims=True)
            dxh = dh * ng_ref[...]
            dx_ref[...] = dout_ref[...] + r * (dxh - xhat * jnp.mean(dxh * xhat, axis=-1, keepdims=True))

    tok = pl.BlockSpec((tm, D), lambda i, k: (i, 0))
    return pl.pallas_call(
        body, name=name, grid=(t // tm, nk),
        in_specs=[pl.BlockSpec((tm, kc), lambda i, k: (i, k)), pl.BlockSpec((kc, D), lambda i, k: (k, 0)), tok,
                  pl.BlockSpec((1, D), lambda i, k: (0, 0)), tok],
        out_specs=[tok, pl.BlockSpec((8, D), lambda i, k: (0, 0))],
        out_shape=[jax.ShapeDtypeStruct((t, D), F32), jax.ShapeDtypeStruct((8, D), F32)],
        scratch_shapes=[pltpu.VMEM((tm, D), F32)],
        compiler_params=_cparams(("arbitrary", "arbitrary")),
    )(du, w, x, ng, dout)


def loss_head(y, target, name):
    t = y.shape[0]
    tm = min(1024, t)

    def body(y_ref, t_ref, dy_ref, loss_ref):
        @pl.when(pl.program_id(0) == 0)
        def _():
            loss_ref[...] = jnp.zeros_like(loss_ref)
        err = y_ref[...] - t_ref[...]
        dy_ref[...] = err * (1.0 / D)
        part = jnp.sum(jnp.sum(err * err, axis=-1, keepdims=True) * (1.0 / D), axis=0, keepdims=True)
        loss_ref[...] += 0.5 * part

    tok = pl.BlockSpec((tm, D), lambda i: (i, 0))
    return pl.pallas_call(
        body, name=name, grid=(t // tm,), in_specs=[tok, tok],
        out_specs=[tok, pl.BlockSpec((8, 128), lambda i: (0, 0))],
        out_shape=[jax.ShapeDtypeStruct((t, D), F32), jax.ShapeDtypeStruct((8, 128), F32)],
        compiler_params=_cparams(("arbitrary",)),
    )(y, target)


def layer_operands(l, norm_g, conv_w_full, q_norm_g, k_norm_g, sinks, gate_b, w_in_b, wco_b, wao_b, wout_b):
    return dict(
        ng=norm_g[l][None, :], cw=jnp.pad(conv_w_full[l], ((0, 5), (0, 0))),
        qg=jnp.tile(q_norm_g[l] * SCALE, D // HEAD)[None, :], kg=jnp.tile(k_norm_g[l], N_KV)[None, :],
        sinks=sinks[l][None, :], gb=gate_b[l][None, :],
        w_in=w_in_b[l], wco=wco_b[l], wao=wao_b[l], wout=wout_b[l])


def layer_bwd(dout, saved, lw, c, l, send_off):
    x, u, h, ya, yb = saved
    du, small, merged, yc, ob, dya, dyb = mixer_bwd(dout, u, ya, yb, lw["cw"], lw["qg"], lw["kg"], lw["sinks"], lw["gb"],
                                                    lw["wco"], lw["wao"], lw["wout"], c, f"mixer_bwd_{l}")
    grads = dict(w_in=matmul_tn(du, h, f"dw_in_{l}"), small=small)
    token = send_off(grads, False)
    grads["wout"] = matmul_tn(merged, dout, f"dw_out_{l}", after=token)
    grads["wco"] = matmul_tn(yc, dya, f"dw_conv_out_{l}")
    grads["wao"] = matmul_tn(ob, dyb, f"dw_attn_out_{l}")
    token = send_off(grads, True)
    dx, grads["dng"] = inproj_bwd_x(du, lw["w_in"], x, lw["ng"] + token[0:1, 0:1], dout, f"inproj_bwd_{l}")
    return dx, grads


MESH = pl.DeviceIdType.MESH
ANY = pl.BlockSpec(memory_space=pl.ANY)


def _place():
    return lax.axis_index("x"), lax.axis_index("y"), lax.axis_index("c")


def all_gather(arrs, after, name):
    n = len(arrs)

    def body(*refs):
        ins, outs = refs[:n], refs[n + 1:2 * n + 1]
        send_sems, recv_sems, local_sems = refs[2 * n + 1:]
        x, y, c = _place()
        me, sibling = (x, y, c), (x, y, 1 - c)
        chips = [(1 - x, y), (x, 1 - y), (1 - x, 1 - y)]

        def slot(a, block):
            px, py, pc = block
            return outs[a].at[4 * px + 2 * py + pc]

        def copy(a, k, block, to, src=None):
            return pltpu.make_async_remote_copy(
                src_ref=slot(a, block) if src is None else src, dst_ref=slot(a, block),
                send_sem=send_sems.at[a, k], recv_sem=recv_sems.at[a, k], device_id=to, device_id_type=MESH)

        mine = [pltpu.make_async_copy(ins[a], slot(a, me), local_sems.at[a]) for a in range(n)]
        for cp in mine:
            cp.start()
        first = []
        for a in range(n):
            first.append(copy(a, 0, me, sibling, src=ins[a]))
            first += [copy(a, 1 + j, me, (*chip, c), src=ins[a]) for j, chip in enumerate(chips)]
        for cp in first:
            cp.start()
        passed = []
        for j, chip in enumerate(chips):
            for a in range(n):
                copy(a, 1 + j, (*chip, c), me).wait_recv()
                passed.append(copy(a, 4 + j, (*chip, c), sibling))
                passed[-1].start()
        for a in range(n):
            copy(a, 0, sibling, me).wait_recv()
            for j, chip in enumerate(chips):
                copy(a, 4 + j, (*chip, 1 - c), me).wait_recv()
        for cp in first + passed:
            cp.wait_send()
        for cp in mine:
            cp.wait()

    return pl.pallas_call(
        body, name=name, in_specs=[ANY] * (n + 1), out_specs=[ANY] * n,
        out_shape=[jax.ShapeDtypeStruct((N_DEV,) + a.shape, a.dtype) for a in arrs],
        scratch_shapes=[pltpu.SemaphoreType.DMA((n, 7)), pltpu.SemaphoreType.DMA((n, 7)), pltpu.SemaphoreType.DMA((n,))],
    )(*arrs, after)


HBM_SPEC = pl.BlockSpec(memory_space=pltpu.HBM)
SEM_SPEC = pl.BlockSpec(memory_space=pltpu.SEMAPHORE)
EFFECT = pltpu.SideEffectType.DATAFLOW_SIDE_EFFECTING


ALL_PEERS = (1, 2, 3, 4, 5, 6, 7)
SAME_CORE_AND_SIBLING = (1, 2, 4, 6)


def _flip(place, k):
    x, y, c = place
    return (1 - x if k & 4 else x, 1 - y if k & 2 else y, 1 - c if k & 1 else c)


def _slot(place):
    return 4 * place[0] + 2 * place[1] + place[2]


def _exchange_copies(ins, lands, send_sems, recv_sems, scatter, flips, arriving):
    me = _place()
    out = []
    for a in range(len(ins)):
        for i, k in enumerate(flips):
            peer = _flip(me, k)
            out.append(pltpu.make_async_remote_copy(
                src_ref=ins[a].at[_slot(peer)] if scatter else ins[a],
                dst_ref=lands[a].at[_slot(peer) if arriving else _slot(me)],
                send_sem=send_sems.at[a * len(flips) + i], recv_sem=recv_sems.at[a * len(flips) + i],
                device_id=peer, device_id_type=MESH))
    return out


def exchange_start(arrs, scatter, flips, after, name):
    n = len(arrs)
    lands = [lax.empty(a.shape if scatter else (N_DEV,) + a.shape, a.dtype) for a in arrs]

    def body(*refs):
        ins, lz = refs[:n], refs[n:2 * n]
        send_sems, recv_sems = refs[2 * n + 1], refs[2 * n + 2]
        token = refs[-1]
        for cp in _exchange_copies(ins, lz, send_sems, recv_sems, scatter, flips, False):
            cp.start()
        token[...] = jnp.zeros_like(token)

    sems = pltpu.SemaphoreType.DMA((n * len(flips),))
    res = pl.pallas_call(
        body, name=name,
        out_shape=(sems, sems, *[pltpu.HBM(a.shape, a.dtype) for a in arrs], *[pltpu.HBM(a.shape, a.dtype) for a in lands],
                   jax.ShapeDtypeStruct((8, 128), F32)),
        in_specs=[HBM_SPEC] * (2 * n) + [ANY],
        out_specs=(SEM_SPEC, SEM_SPEC, *[HBM_SPEC] * (2 * n), pl.BlockSpec(memory_space=pltpu.VMEM)),
        input_output_aliases={i: 2 + i for i in range(2 * n)},
        compiler_params=pltpu.CompilerParams(has_side_effects=EFFECT),
    )(*[pltpu.with_memory_space_constraint(a, pltpu.HBM) for a in arrs],
      *[pltpu.with_memory_space_constraint(a, pltpu.HBM) for a in lands], after)
    return dict(send=res[0], recv=res[1], srcs=res[2:2 + n], lands=res[2 + n:2 + 2 * n], token=res[-1], scatter=scatter,
                flips=flips)


def exchange_wait(state, after, name):
    n = len(state["srcs"])

    def body(*refs):
        ins, lz = refs[:n], refs[n:2 * n]
        send_sems, recv_sems = refs[2 * n], refs[2 * n + 1]
        for cp in _exchange_copies(ins, lz, send_sems, recv_sems, state["scatter"], state["flips"], True):
            cp.wait_send()
            cp.wait_recv()

    both = list(state["srcs"]) + list(state["lands"])
    res = pl.pallas_call(
        body, name=name, out_shape=tuple(pltpu.HBM(a.shape, a.dtype) for a in both),
        in_specs=[HBM_SPEC] * (2 * n) + [SEM_SPEC, SEM_SPEC, ANY], out_specs=tuple([HBM_SPEC] * (2 * n)),
        input_output_aliases={i: i for i in range(2 * n)},
        compiler_params=pltpu.CompilerParams(has_side_effects=EFFECT),
    )(*both, state["send"], state["recv"], after)
    return res[:n], res[n:]


def gather_finish(mine, lands, name):
    n = len(lands)
    others = (2, 4, 6)

    def body(*refs):
        srcs, lz = refs[:n], refs[n:2 * n]
        send_sems, recv_sems, local_sems = refs[3 * n:]
        me = _place()
        sibling = _flip(me, 1)
        own = [pltpu.make_async_copy(srcs[a], lz[a].at[_slot(me)], local_sems.at[a]) for a in range(n)]
        for cp in own:
            cp.start()

        def copies(core_of_origin):
            return [pltpu.make_async_remote_copy(
                src_ref=lz[a].at[_slot(_flip(core_of_origin, k))], dst_ref=lz[a].at[_slot(_flip(core_of_origin, k))],
                send_sem=send_sems.at[a * len(others) + i], recv_sem=recv_sems.at[a * len(others) + i],
                device_id=sibling, device_id_type=MESH) for a in range(n) for i, k in enumerate(others)]

        sends = copies(me)
        for cp in sends:
            cp.start()
        for cp in copies(sibling):
            cp.wait_recv()
        for cp in sends:
            cp.wait_send()
        for cp in own:
            cp.wait()

    return pl.pallas_call(
        body, name=name, in_specs=[ANY] * (2 * n), out_specs=[ANY] * n,
        out_shape=[jax.ShapeDtypeStruct(a.shape, a.dtype) for a in lands],
        input_output_aliases={n + i: i for i in range(n)},
        scratch_shapes=[pltpu.SemaphoreType.DMA((n * len(others),)), pltpu.SemaphoreType.DMA((n * len(others),)),
                        pltpu.SemaphoreType.DMA((n,))],
    )(*mine, *lands)


def adamw(w, m, v, parts, name):
    r, cdim = w.shape
    n_parts = parts.shape[0]
    rb = 256 if r % 256 == 0 else (SHARD_COLS // 4 if r == SHARD_COLS else r)

    def body(w_ref, m_ref, v_ref, p_ref, g_ref, d_ref, mo_ref, vo_ref):
        g = p_ref[0].astype(F32)
        for i in range(1, n_parts):
            g = g + p_ref[i].astype(F32)
        m_new = ADAM_B1 * m_ref[...] + (1.0 - ADAM_B1) * g
        v_new = ADAM_B2 * v_ref[...] + (1.0 - ADAM_B2) * (g * g)
        m_hat = m_new / (1.0 - ADAM_B1 ** ADAM_STEP)
        v_hat = v_new / (1.0 - ADAM_B2 ** ADAM_STEP)
        g_ref[...] = g
        d_ref[...] = -ADAM_LR * (m_hat / (jnp.sqrt(v_hat) + ADAM_EPS) + ADAM_WD * w_ref[...])
        mo_ref[...] = m_new
        vo_ref[...] = v_new

    blk = pl.BlockSpec((rb, cdim), lambda i: (i, 0))
    return pl.pallas_call(
        body, name=name, grid=(r // rb,),
        in_specs=[blk, blk, blk, pl.BlockSpec((n_parts, rb, cdim), lambda i: (0, i, 0))],
        out_specs=[blk] * 4, out_shape=[jax.ShapeDtypeStruct((r, cdim), F32)] * 4,
        compiler_params=_cparams(("arbitrary",)),
    )(w, m, v, parts)


def adamw_layers(w, m, v, lands, srcs, me, lo, prev, name):
    nl_all, r, cdim = w.shape
    nl = len(lands)
    rb = 256 if r % 256 == 0 else (SHARD_COLS // 4 if r == SHARD_COLS else r)
    nblk = r // rb
    n_prev = 0 if prev is None else 4

    def body(me_ref, w_ref, m_ref, v_ref, *rest):
        land_refs, src_refs = rest[:nl], rest[nl:2 * nl]
        g_ref, d_ref, mo_ref, vo_ref = rest[2 * nl + n_prev:]
        for k in range(nl):
            @pl.when(pl.program_id(0) == k)
            def _(k=k):
                own = src_refs[k][...].astype(F32)
                g = jnp.where(me_ref[0] == 0, own, land_refs[k][0].astype(F32))
                for i in range(1, N_DEV):
                    g = g + jnp.where(me_ref[0] == i, own, land_refs[k][i].astype(F32))
                m_new = ADAM_B1 * m_ref[...] + (1.0 - ADAM_B1) * g
                v_new = ADAM_B2 * v_ref[...] + (1.0 - ADAM_B2) * (g * g)
                m_hat = m_new / (1.0 - ADAM_B1 ** ADAM_STEP)
                v_hat = v_new / (1.0 - ADAM_B2 ** ADAM_STEP)
                g_ref[...] = g
                d_ref[...] = -ADAM_LR * (m_hat / (jnp.sqrt(v_hat) + ADAM_EPS) + ADAM_WD * w_ref[...])
                mo_ref[...] = m_new
                vo_ref[...] = v_new

    blk = pl.BlockSpec((None, rb, cdim), lambda l, i, me_ref: (lo + l, i, 0))

    def rows(l, i, k):
        return jnp.where(l < k, 0, jnp.where(l == k, i, nblk - 1))

    land_specs = [pl.BlockSpec((N_DEV, rb, cdim), lambda l, i, me_ref, k=k: (0, rows(l, i, k), 0)) for k in range(nl)]
    src_specs = [pl.BlockSpec((None, rb, cdim), lambda l, i, me_ref, k=k: (me_ref[0], rows(l, i, k), 0)) for k in range(nl)]
    return pl.pallas_call(
        body, name=name,
        grid_spec=pltpu.PrefetchScalarGridSpec(
            num_scalar_prefetch=1, grid=(nl, nblk),
            in_specs=[blk, blk, blk] + land_specs + src_specs + [ANY] * n_prev, out_specs=[blk] * 4),
        out_shape=[jax.ShapeDtypeStruct((nl_all, r, cdim), F32)] * 4,
        input_output_aliases={4 + 2 * nl + j: j for j in range(n_prev)},
        compiler_params=_cparams(("arbitrary", "arbitrary")),
    )(me, w, m, v, *lands, *srcs, *([] if prev is None else prev))


def small_sum(parts, fold, name):
    rows = parts.shape[1]

    def dot3(xv, sel):
        out = jnp.zeros((xv.shape[0], sel.shape[1]), F32)
        for _ in range(3):
            hi = xv.astype(BF16)
            out = out + _dot(hi, sel)
            xv = xv - hi.astype(F32)
        return out

    def body(p_ref, fold_ref, o_ref):
        tot = p_ref[0]
        for i in range(1, N_DEV):
            tot = tot + p_ref[i]
        o_ref[...] = tot
        for l in range(rows // SM_ROWS):
            blk = tot[l * SM_ROWS:l * SM_ROWS + 8, 0:D]
            folded = dot3(blk, fold_ref[...])
            o_ref[l * SM_ROWS + 9:l * SM_ROWS + 10, 0:128] = folded[SM_QG:SM_QG + 1, :]
            o_ref[l * SM_ROWS + 10:l * SM_ROWS + 11, 0:128] = folded[SM_KG:SM_KG + 1, :]

    return pl.pallas_call(
        body, name=name, out_shape=jax.ShapeDtypeStruct((rows, 2 * D), F32),
        compiler_params=_cparams(None),
    )(parts, fold)


def kernel(x, norm_g, w_in, conv_w, q_norm_g, k_norm_g, sinks, w_conv_out, w_attn_out, gate_b, w_out, loss_target, m_norm_g, m_w_in, m_conv_w, m_q_norm_g, m_k_norm_g, m_sinks, m_w_conv_out, m_w_attn_out, m_gate_b, m_w_out, v_norm_g, v_w_in, v_conv_w, v_q_norm_g, v_k_norm_g, v_sinks, v_w_conv_out, v_w_attn_out, v_gate_b, v_w_out):
    c = _selectors()
    me = 4 * lax.axis_index("x") + 2 * lax.axis_index("y") + lax.axis_index("c")

    w_in_t, m_w_in_t, v_w_in_t = (jnp.swapaxes(a, 1, 2) for a in (w_in, m_w_in, v_w_in))

    def shards(l):
        return [w_in_t[l].astype(BF16), w_conv_out[l].astype(BF16), w_attn_out[l].astype(BF16), w_out[l].astype(BF16)]

    def gather_start(arrs, after, tag):
        return exchange_start(arrs, False, SAME_CORE_AND_SIBLING, after, f"gather_start_{tag}")

    def gather_end(state, after, tag):
        mine, lands = exchange_wait(state, after, f"gather_wait_{tag}")
        return gather_finish(mine, lands, f"gather_finish_{tag}")

    h = x[0]
    saved, lws = [], []
    gather = gather_start(shards(0) + [conv_w], h, 0)
    for l in range(DEPTH):
        lands = gather_end(gather, gather["token"] if l == 0 else h, l)
        if l == 0:
            conv_full = jnp.transpose(lands[4], (1, 2, 0, 3)).reshape(DEPTH, 3, D)
        if l + 1 < DEPTH:
            gather = gather_start(shards(l + 1), lands[1], l + 1)
        lws.append(layer_operands(l, norm_g, conv_full, q_norm_g, k_norm_g, sinks, gate_b,
                                  {l: lands[0].reshape(IN_COLS, D)}, {l: lands[1].reshape(D, D)},
                                  {l: lands[2].reshape(D, D)}, {l: lands[3].reshape(D, D)}))
        u, hb = inproj_fwd(h, lws[l]["ng"] + gather["token"][0:1, 0:1], lws[l]["w_in"], f"inproj_fwd_{l}")
        x_in = h
        h, ya, yb = mixer_fwd(x_in, u, lws[l]["cw"], lws[l]["qg"], lws[l]["kg"], lws[l]["sinks"], lws[l]["gb"],
                              lws[l]["wco"], lws[l]["wao"], lws[l]["wout"], c, f"mixer_fwd_{l}")
        saved.append((x_in, u, hb, ya, yb))
    dh, loss_part = loss_head(h, loss_target[0], "loss_head")

    grads, scatters = [None] * DEPTH, [[] for _ in range(DEPTH)]
    for l in reversed(range(DEPTH)):
        def send_off(g, done, l=l):
            rest = [g[k].reshape(N_DEV, SHARD_ROWS, D) for k in ("wco", "wao", "wout")] if done else []
            first = [g["w_in"].reshape(N_DEV, SHARD_COLS, D)] if done == (l > 0) else []
            if not first + rest:
                return None
            tag = f"{l}" if l > 0 else ("0_rest" if done else "0_in")
            scatters[l].append(exchange_start(first + rest, True, ALL_PEERS, g["small"], f"scatter_start_{tag}"))
            return scatters[l][-1]["token"]
        dh, grads[l] = layer_bwd(dh, saved[l], lws[l], c, l, send_off)

    me1 = me.astype(jnp.int32).reshape(1)
    mine, lands = {}, {}
    for l in (3, 2, 1):
        mine[l], lands[l] = exchange_wait(scatters[l][0], dh, f"scatter_wait_{l}")
    weights = [(w_in_t, m_w_in_t, v_w_in_t, "w_in"), (w_conv_out, m_w_conv_out, v_w_conv_out, "w_conv_out"),
               (w_attn_out, m_w_attn_out, v_w_attn_out, "w_attn_out"), (w_out, m_w_out, v_w_out, "w_out")]
    upd = [adamw_layers(w, m, v, [lands[l][i] for l in (1, 2, 3)], [mine[l][i] for l in (1, 2, 3)], me1, 1, None,
                        f"adamw_{n}_upper") for i, (w, m, v, n) in enumerate(weights)]
    m_in, l_in = exchange_wait(scatters[0][0], upd[0][0], "scatter_wait_0_in")
    m_rest, l_rest = exchange_wait(scatters[0][1], upd[3][0], "scatter_wait_0_rest")
    mine[0], lands[0] = list(m_in) + list(m_rest), list(l_in) + list(l_rest)
    upd = [adamw_layers(w, m, v, [lands[0][i]], [mine[0][i]], me1, 0, upd[i], f"adamw_{n}_0")
           for i, (w, m, v, n) in enumerate(weights)]
    u_in = [jnp.swapaxes(o, 1, 2) for o in upd[0]]
    u_co, u_ao, u_out = upd[1], upd[2], upd[3]

    blocks = []
    for l in range(DEPTH):
        blk = grads[l]["small"]
        blk = blk.at[SM_NORM, 0:D].set(grads[l]["dng"][0])
        if l == 0:
            blk = blk.at[SM_LOSS, 0:128].set(loss_part[0])
        blocks.append(blk)
    gathered = all_gather([jnp.concatenate(blocks, axis=0)], u_out[0], "gather_small")[0]
    tot = small_sum(gathered, c["fold"], "small_sum")
    tot = tot.reshape(DEPTH, SM_ROWS, 2 * D)
    loss = tot[0, SM_LOSS, 0]

    def update_small(w, m, v, g, name):
        return adamw(w, m, v, g[None], name)

    u_ng = update_small(norm_g, m_norm_g, v_norm_g, tot[:, SM_NORM, 0:D], "adamw_norm_g")
    u_qg = update_small(q_norm_g, m_q_norm_g, v_q_norm_g, tot[:, 9, 0:HEAD], "adamw_q_norm_g")
    u_kg = update_small(k_norm_g, m_k_norm_g, v_k_norm_g, tot[:, 10, 0:HEAD], "adamw_k_norm_g")
    u_sk = update_small(sinks, m_sinks, v_sinks, tot[:, SM_SINK, 0:16], "adamw_sinks")
    u_gb = update_small(gate_b, m_gate_b, v_gate_b, tot[:, SM_GATE, :], "adamw_gate_b")
    g_conv = lax.dynamic_slice_in_dim(tot[:, SM_CONV:SM_CONV + 3, 0:D], me * SHARD_ROWS, SHARD_ROWS, axis=2)
    u_cw = [o.reshape(DEPTH, 3, SHARD_ROWS) for o in update_small(
        conv_w.reshape(DEPTH * 3, SHARD_ROWS), m_conv_w.reshape(DEPTH * 3, SHARD_ROWS),
        v_conv_w.reshape(DEPTH * 3, SHARD_ROWS), g_conv.reshape(DEPTH * 3, SHARD_ROWS), "adamw_conv_w")]

    order = [u_ng, u_in, u_cw, u_qg, u_kg, u_sk, u_co, u_ao, u_gb, u_out]
    return (loss, dh[None], *[u[0] for u in order], *[u[1] for u in order], *[u[2] for u in order], *[u[3] for u in order])
```

```python
import functools

import jax
import jax.numpy as jnp
from jax import lax
from jax.experimental import pallas as pl
from jax.experimental.pallas import tpu as pltpu

F32 = jnp.float32
BF16 = jnp.bfloat16

N_DEV = 8
DEPTH = 4
D = 1024
N_KV = 4
GROUP = 4
HEAD = 64
BLK = 128
KVW = N_KV * HEAD
IN_COLS = 8704
SHARD_COLS = IN_COLS // N_DEV
SHARD_ROWS = D // N_DEV
C_VC, C_BC, C_CC, C_ZC, C_Q, C_K, C_V, C_ZA, C_GA, C_GB = 0, 1024, 2048, 3072, 4096, 5120, 5376, 5632, 6656, 7680
EPS = 1e-6
NEG_INF = -1e30
SCALE = HEAD ** -0.5

ADAM_LR = 0.001
ADAM_B1 = 0.9
ADAM_B2 = 0.999
ADAM_EPS = 1e-08
ADAM_WD = 0.01
ADAM_STEP = 10

VMEM_LIMIT = 60 * 1024 * 1024
SM_ROWS = 16
SM_GATE, SM_CONV, SM_QG, SM_KG, SM_SINK, SM_NORM, SM_LOSS = 0, 1, 4, 5, 6, 7, 8


def _cparams(sem):
    return pltpu.CompilerParams(dimension_semantics=sem, vmem_limit_bytes=VMEM_LIMIT)


def _dot(a, b):
    return jnp.dot(a, b, preferred_element_type=F32)


def _dot_nt(a, b):
    return lax.dot_general(a, b, (((1,), (1,)), ((), ())), preferred_element_type=F32)


def _dot_tn(a, b):
    return lax.dot_general(a, b, (((0,), (0,)), ((), ())), preferred_element_type=F32)


def _dot2(x, sel):
    hi = x.astype(BF16)
    lo = (x - hi.astype(F32)).astype(BF16)
    return _dot(hi, sel) + _dot(lo, sel)


def _sigmoid(z):
    return 1.0 / (1.0 + jnp.exp(-z))


def _head_mean(t, sel, exp):
    return _dot2(_dot2(t, sel) * (1.0 / HEAD), exp)


def _bf(ref, c0, width):
    return ref[:, c0:c0 + width].astype(F32)


def _selectors():
    c = jnp.arange(D)
    sel_q = (c[:, None] // HEAD == jnp.arange(128)[None, :]).astype(BF16)
    ck = jnp.arange(KVW)
    sel_k = (ck[:, None] // HEAD == jnp.arange(128)[None, :]).astype(BF16)
    src = jnp.arange(KVW)[:, None]
    dst = jnp.arange(KVW)[None, :]
    rep = jnp.stack([((src // HEAD == h) & (src % HEAD == dst % HEAD)).astype(BF16) for h in range(N_KV)])
    fold = (c[:, None] % HEAD == jnp.arange(128)[None, :]).astype(BF16)
    return dict(sel_q=sel_q, exp_q=sel_q.T, sel_k=sel_k, exp_k=sel_k.T, rep=rep, rep_t=jnp.swapaxes(rep, 1, 2), fold=fold)


def inproj_fwd(x, ng, w, name):
    t = x.shape[0]
    tm = min(1024, t)
    cb = 2176
    def body(x_ref, ng_ref, w_ref, u_ref, h_ref, h_scr):
        @pl.when(pl.program_id(1) == 0)
        def _():
            xf = x_ref[...]
            r = lax.rsqrt(jnp.mean(xf * xf, axis=-1, keepdims=True) + EPS)
            hb = (xf * r * ng_ref[...]).astype(BF16)
            h_scr[...] = hb
            h_ref[...] = hb
        u_ref[...] = _dot_nt(h_scr[...], w_ref[...]).astype(BF16)

    return pl.pallas_call(
        body, name=name, grid=(t // tm, IN_COLS // cb),
        in_specs=[pl.BlockSpec((tm, D), lambda i, j: (i, 0)), pl.BlockSpec((1, D), lambda i, j: (0, 0)),
                  pl.BlockSpec((cb, D), lambda i, j: (j, 0))],
        out_specs=[pl.BlockSpec((tm, cb), lambda i, j: (i, j)), pl.BlockSpec((tm, D), lambda i, j: (i, 0))],
        out_shape=[jax.ShapeDtypeStruct((t, IN_COLS), BF16), jax.ShapeDtypeStruct((t, D), BF16)],
        scratch_shapes=[pltpu.VMEM((tm, D), BF16)],
        compiler_params=_cparams(("arbitrary", "arbitrary")),
    )(x, ng, w)


def _conv_fwd(u_ref, uvc_prev, ucc_prev, cw_ref, is_first, tm):
    p = _bf(u_ref, C_CC, D) * _bf(u_ref, C_VC, D)
    pprev = ucc_prev[...].astype(F32) * uvc_prev[...].astype(F32)
    pprev = jnp.where(is_first, 0.0, pprev)
    row = lax.broadcasted_iota(jnp.int32, (tm, 1), 0)
    p1 = jnp.where(row == 0, pprev[15:16, :], pltpu.roll(p, 1, 0))
    p2 = jnp.where(row == 0, pprev[14:15, :], jnp.where(row == 1, pprev[15:16, :], pltpu.roll(p, 2, 0)))
    cw = cw_ref[...]
    conv = cw[0:1, :] * p2 + cw[1:2, :] * p1 + cw[2:3, :] * p
    return p, p1, p2, conv


def _attn_inputs(u_ref, ukv_prev, qg_ref, kg_ref, c, tm):
    q = _bf(u_ref, C_Q, D)
    rq = lax.rsqrt(_head_mean(q * q, c["sel_q"][...], c["exp_q"][...]) + EPS)
    qhat = q * rq
    qn = (qhat * qg_ref[...]).astype(BF16)
    kband = jnp.concatenate([ukv_prev[:, 0:KVW].astype(F32), _bf(u_ref, C_K, KVW)], axis=0)
    rk = lax.rsqrt(_head_mean(kband * kband, c["sel_k"][...], c["exp_k"][...]) + EPS)
    khat = kband * rk
    knb = (khat * kg_ref[...]).astype(BF16)
    vband = jnp.concatenate([ukv_prev[:, KVW:2 * KVW], u_ref[:, C_V:C_V + KVW]], axis=0)
    kt = [_dot(knb, c["rep"][h]).astype(BF16) for h in range(N_KV)]
    vt = [_dot(vband, c["rep"][h]).astype(BF16) for h in range(N_KV)]
    return qhat, rq, qn, khat, rk, kt, vt


def _attn_masks(is_first):
    rows = GROUP * BLK
    r = lax.broadcasted_iota(jnp.int32, (rows, 2 * BLK), 0)
    kk = lax.broadcasted_iota(jnp.int32, (rows, 2 * BLK), 1)
    qq = r % BLK
    valid = (kk > qq) & (kk <= qq + BLK)
    valid_first = valid & ((kk >= BLK) | jnp.logical_not(is_first))
    lane_grp = lax.broadcasted_iota(jnp.int32, (BLK, KVW), 1) // HEAD
    row_grp = lax.broadcasted_iota(jnp.int32, (rows, 1), 0) // BLK
    return valid, valid_first, lane_grp, row_grp


def _sink_col(sinks_ref, h, row_grp):
    col = jnp.full(row_grp.shape, sinks_ref[0, GROUP * h], F32)
    for gi in range(1, GROUP):
        col = jnp.where(row_grp == gi, sinks_ref[0, GROUP * h + gi], col)
    return col


def _stack_groups(a256, lane_grp):
    zero = jnp.zeros_like(a256)
    return jnp.concatenate([jnp.where(lane_grp == gi, a256, zero) for gi in range(GROUP)], axis=0)


def _unstack_groups(a4, lane_grp):
    out = jnp.where(lane_grp == 0, a4[0:BLK], 0.0)
    for gi in range(1, GROUP):
        out = out + jnp.where(lane_grp == gi, a4[gi * BLK:(gi + 1) * BLK], 0.0)
    return out


def _softmax_block(qs, kt_b, valid, sink):
    s = _dot_nt(qs, kt_b)
    s = jnp.where(valid, s, NEG_INF)
    m = jnp.maximum(jnp.max(s, axis=-1, keepdims=True), sink)
    e = jnp.exp(s - m)
    es = jnp.exp(sink - m)
    inv = 1.0 / (jnp.sum(e, axis=-1, keepdims=True) + es)
    return e * inv, es * inv


def _mixer_specs(t, tm, n_tiles, tile_of):
    nb = tm // BLK
    u_spec = pl.BlockSpec((tm, IN_COLS), lambda g: (tile_of(g), 0))
    ukv_prev = pl.BlockSpec((BLK, 2 * KVW), lambda g: (jnp.maximum(tile_of(g) * nb - 1, 0), C_K // (2 * KVW)))
    uvc_prev = pl.BlockSpec((16, D), lambda g: (jnp.maximum(tile_of(g) * (tm // 16) - 1, 0), C_VC // D))
    ucc_prev = pl.BlockSpec((16, D), lambda g: (jnp.maximum(tile_of(g) * (tm // 16) - 1, 0), C_CC // D))
    return u_spec, ukv_prev, uvc_prev, ucc_prev


def _full(shape):
    n = len(shape)
    return pl.BlockSpec(shape, lambda g: (0,) * n)


def mixer_fwd(x, u, cw, qg, kg, sinks, gb, wco, wao, wout, c, name):
    t = x.shape[0]
    tm = min(256, t)
    n_tiles = t // tm
    nb = tm // BLK
    cn = sorted(c)

    def body(x_ref, u_ref, ukv_prev, uvc_prev, ucc_prev, cw_ref, qg_ref, kg_ref, sinks_ref, gb_ref, wco_ref, wao_ref,
             wout_ref, *rest):
        cref = dict(zip(cn, rest[:len(cn)]))
        xo_ref, ya_ref, yb_ref, o_scr = rest[len(cn):]
        is_first = pl.program_id(0) == 0
        _, _, _, conv = _conv_fwd(u_ref, uvc_prev, ucc_prev, cw_ref, is_first, tm)
        zc = _bf(u_ref, C_ZC, D)
        yc = _bf(u_ref, C_BC, D) * conv * (zc * _sigmoid(zc))
        ya = _dot(yc.astype(BF16), wco_ref[...])

        _, _, qn, _, _, kt, vt = _attn_inputs(u_ref, ukv_prev, qg_ref, kg_ref, cref, tm)
        valid, valid_first, lane_grp, row_grp = _attn_masks(is_first)
        for h in range(N_KV):
            sink = _sink_col(sinks_ref, h, row_grp)
            for b in range(nb):
                qs = _stack_groups(qn[b * BLK:(b + 1) * BLK, h * KVW:(h + 1) * KVW], lane_grp)
                pn, _ = _softmax_block(qs, kt[h][b * BLK:(b + 2) * BLK], valid_first if b == 0 else valid, sink)
                o4 = _dot(pn.astype(BF16), vt[h][b * BLK:(b + 2) * BLK])
                o_scr[b * BLK:(b + 1) * BLK, h * KVW:(h + 1) * KVW] = _unstack_groups(o4, lane_grp)
        za = _bf(u_ref, C_ZA, D)
        ob = o_scr[...] * (za * _sigmoid(za))
        yb = _dot(ob.astype(BF16), wao_ref[...])

        g_a = _sigmoid(_bf(u_ref, C_GA, D) + gb_ref[:, 0:D])
        g_b = _sigmoid(_bf(u_ref, C_GB, D) + gb_ref[:, D:2 * D])
        merged = g_a * ya + g_b * yb
        xo_ref[...] = x_ref[...] + _dot(merged.astype(BF16), wout_ref[...])
        ya_ref[...] = ya.astype(BF16)
        yb_ref[...] = yb.astype(BF16)

    u_spec, ukv_prev, uvc_prev, ucc_prev = _mixer_specs(t, tm, n_tiles, lambda g: g)
    tok = pl.BlockSpec((tm, D), lambda g: (g, 0))
    consts = [c[k] for k in cn]
    return pl.pallas_call(
        body, name=name, grid=(n_tiles,),
        in_specs=[tok, u_spec, ukv_prev, uvc_prev, ucc_prev, _full((8, D)), _full((1, D)), _full((1, KVW)),
                  pl.BlockSpec(memory_space=pltpu.SMEM), _full((1, 2 * D)), _full((D, D)), _full((D, D)), _full((D, D))]
                 + [_full(a.shape) for a in consts],
        out_specs=[tok, tok, tok],
        out_shape=[jax.ShapeDtypeStruct((t, D), F32), jax.ShapeDtypeStruct((t, D), BF16), jax.ShapeDtypeStruct((t, D), BF16)],
        scratch_shapes=[pltpu.VMEM((tm, D), F32)],
        compiler_params=_cparams(("arbitrary",)),
    )(x, u, u, u, u, cw, qg, kg, sinks, gb, wco, wao, wout, *consts)


def mixer_bwd(dout, u, ya, yb, cw, qg, kg, sinks, gb, wco, wao, wout, c, name):
    t = dout.shape[0]
    tm = min(256, t)
    n_tiles = t // tm
    nb = tm // BLK
    kb = tm + BLK
    cn = sorted(c)

    def body(dout_ref, u_ref, ukv_prev, uvc_prev, ucc_prev, ya_ref, yb_ref, cw_ref, qg_ref, kg_ref, sinks_ref, gb_ref,
             wco_ref, wao_ref, wout_ref, *rest):
        cref = dict(zip(cn, rest[:len(cn)]))
        (du_ref, small_ref, merged_ref, yc_ref, ob_ref, dya_ref, dyb_ref,
         o_scr, dq_scr, dk4_scr, dv4_scr, carry_kv, carry_conv) = rest[len(cn):]
        g = pl.program_id(0)
        is_first = g == n_tiles - 1

        @pl.when(g == 0)
        def _():
            carry_kv[...] = jnp.zeros_like(carry_kv)
            carry_conv[...] = jnp.zeros_like(carry_conv)
            small_ref[...] = jnp.zeros_like(small_ref)

        dout = dout_ref[...]
        dout_b = dout.astype(BF16)
        ya_v = ya_ref[...].astype(F32)
        yb_v = yb_ref[...].astype(F32)
        g_a = _sigmoid(_bf(u_ref, C_GA, D) + gb_ref[:, 0:D])
        g_b = _sigmoid(_bf(u_ref, C_GB, D) + gb_ref[:, D:2 * D])
        merged = g_a * ya_v + g_b * yb_v
        dmerged = _dot_nt(dout_b, wout_ref[...])
        merged_ref[...] = merged.astype(BF16)
        dya = dmerged * g_a
        dyb = dmerged * g_b
        dgl_a = dmerged * ya_v * g_a * (1.0 - g_a)
        dgl_b = dmerged * yb_v * g_b * (1.0 - g_b)
        du_ref[:, C_GA:C_GA + D] = dgl_a.astype(BF16)
        du_ref[:, C_GB:C_GB + D] = dgl_b.astype(BF16)
        small_ref[SM_GATE:SM_GATE + 1, 0:D] += jnp.sum(dgl_a, axis=0, keepdims=True)
        small_ref[SM_GATE:SM_GATE + 1, D:2 * D] += jnp.sum(dgl_b, axis=0, keepdims=True)

        p, p1, p2, conv = _conv_fwd(u_ref, uvc_prev, ucc_prev, cw_ref, is_first, tm)
        zc = _bf(u_ref, C_ZC, D)
        bc = _bf(u_ref, C_BC, D)
        sg = _sigmoid(zc)
        sc = zc * sg
        yc = bc * conv * sc
        dya_b = dya.astype(BF16)
        yc_ref[...] = yc.astype(BF16)
        dya_ref[...] = dya_b
        dyc = _dot_nt(dya_b, wco_ref[...])
        du_ref[:, C_BC:C_BC + D] = (dyc * conv * sc).astype(BF16)
        du_ref[:, C_ZC:C_ZC + D] = (dyc * bc * conv * (sg * (1.0 + zc * (1.0 - sg)))).astype(BF16)
        dconv = dyc * bc * sc
        small_ref[SM_CONV + 2:SM_CONV + 3, 0:D] += jnp.sum(dconv * p, axis=0, keepdims=True)
        small_ref[SM_CONV + 1:SM_CONV + 2, 0:D] += jnp.sum(dconv * p1, axis=0, keepdims=True)
        small_ref[SM_CONV:SM_CONV + 1, 0:D] += jnp.sum(dconv * p2, axis=0, keepdims=True)
        row = lax.broadcasted_iota(jnp.int32, (tm, 1), 0)
        nxt = carry_conv[...]
        d1 = jnp.where(row == tm - 1, nxt[0:1, :], pltpu.roll(dconv, tm - 1, 0))
        d2 = jnp.where(row == tm - 1, nxt[1:2, :], jnp.where(row == tm - 2, nxt[0:1, :], pltpu.roll(dconv, tm - 2, 0)))
        carry_conv[...] = dconv[0:8, :]
        cw = cw_ref[...]
        dp = cw[2:3, :] * dconv + cw[1:2, :] * d1 + cw[0:1, :] * d2
        du_ref[:, C_CC:C_CC + D] = (dp * _bf(u_ref, C_VC, D)).astype(BF16)
        du_ref[:, C_VC:C_VC + D] = (dp * _bf(u_ref, C_CC, D)).astype(BF16)

        dyb_b = dyb.astype(BF16)
        dob = _dot_nt(dyb_b, wao_ref[...])
        za = _bf(u_ref, C_ZA, D)
        sga = _sigmoid(za)
        sa = za * sga
        do = dob * sa
        qhat, rq, qn, khat, rk, kt, vt = _attn_inputs(u_ref, ukv_prev, qg_ref, kg_ref, cref, tm)
        valid, valid_first, lane_grp, row_grp = _attn_masks(is_first)
        dk4_scr[...] = jnp.zeros_like(dk4_scr)
        dv4_scr[...] = jnp.zeros_like(dv4_scr)
        lane16 = lax.broadcasted_iota(jnp.int32, (1, 2 * D), 1)
        dsink_row = jnp.zeros((1, 2 * D), F32)
        for h in range(N_KV):
            sink = _sink_col(sinks_ref, h, row_grp)
            dsink_col = jnp.zeros((GROUP * BLK, 1), F32)
            for b in range(nb):
                rows = slice(b * BLK, (b + 1) * BLK)
                band = slice(b * BLK, (b + 2) * BLK)
                cols = slice(h * KVW, (h + 1) * KVW)
                qs = _stack_groups(qn[rows, cols], lane_grp)
                pn, ps = _softmax_block(qs, kt[h][band], valid_first if b == 0 else valid, sink)
                pn_b = pn.astype(BF16)
                o4 = _dot(pn_b, vt[h][band])
                o_scr[rows, cols] = _unstack_groups(o4, lane_grp)
                dos = _stack_groups(do[rows, cols], lane_grp).astype(BF16)
                dpn = _dot_nt(dos, vt[h][band])
                delta = jnp.sum(pn * dpn, axis=-1, keepdims=True)
                ds = (pn * (dpn - delta)).astype(BF16)
                dsink_col = dsink_col - ps * delta
                dq_scr[rows, cols] = _unstack_groups(_dot(ds, kt[h][band]), lane_grp)
                dk4_scr[h, band, :] += _dot_tn(ds, qs)
                dv4_scr[h, band, :] += _dot_tn(pn_b, dos)
            for gi in range(GROUP):
                tot = jnp.sum(dsink_col[gi * BLK:(gi + 1) * BLK, :], axis=0, keepdims=True)
                dsink_row = dsink_row + jnp.where(lane16 == GROUP * h + gi, tot, 0.0)
        small_ref[SM_SINK:SM_SINK + 1, :] += dsink_row

        o = o_scr[...]
        ob_ref[...] = (o * sa).astype(BF16)
        dyb_ref[...] = dyb_b
        du_ref[:, C_ZA:C_ZA + D] = (dob * o * (sga * (1.0 + za * (1.0 - sga)))).astype(BF16)

        dqn = dq_scr[...]
        small_ref[SM_QG:SM_QG + 1, 0:D] += SCALE * jnp.sum(dqn * qhat, axis=0, keepdims=True)
        dqh = dqn * qg_ref[...]
        dq = rq * (dqh - qhat * _head_mean(dqh * qhat, cref["sel_q"][...], cref["exp_q"][...]))
        du_ref[:, C_Q:C_Q + D] = dq.astype(BF16)

        dkn_band = jnp.zeros((kb, KVW), F32)
        dv_band = jnp.zeros((kb, KVW), F32)
        for h in range(N_KV):
            dkn_band = dkn_band + _dot2(dk4_scr[h], cref["rep_t"][h])
            dv_band = dv_band + _dot2(dv4_scr[h], cref["rep_t"][h])
        carried = carry_kv[...]
        pad = jnp.zeros((tm - BLK, KVW), F32)
        if nb > 1:
            dkn = dkn_band[BLK:, :] + jnp.concatenate([pad, carried[:, 0:KVW]], axis=0)
            dv = dv_band[BLK:, :] + jnp.concatenate([pad, carried[:, KVW:2 * KVW]], axis=0)
        else:
            dkn = dkn_band[BLK:, :] + carried[:, 0:KVW]
            dv = dv_band[BLK:, :] + carried[:, KVW:2 * KVW]
        carry_kv[:, 0:KVW] = dkn_band[0:BLK, :]
        carry_kv[:, KVW:2 * KVW] = dv_band[0:BLK, :]
        khat_t = khat[BLK:, :]
        small_ref[SM_KG:SM_KG + 1, 0:KVW] += jnp.sum(dkn * khat_t, axis=0, keepdims=True)
        dkh = dkn * kg_ref[...]
        dk = rk[BLK:, :] * (dkh - khat_t * _head_mean(dkh * khat_t, cref["sel_k"][...], cref["exp_k"][...]))
        du_ref[:, C_K:C_K + KVW] = dk.astype(BF16)
        du_ref[:, C_V:C_V + KVW] = dv.astype(BF16)

    rev = lambda g: n_tiles - 1 - g
    u_spec, ukv_prev, uvc_prev, ucc_prev = _mixer_specs(t, tm, n_tiles, rev)
    tok = pl.BlockSpec((tm, D), lambda g: (rev(g), 0))
    consts = [c[k] for k in cn]
    wspec = _full((D, D))
    return pl.pallas_call(
        body, name=name, grid=(n_tiles,),
        in_specs=[tok, u_spec, ukv_prev, uvc_prev, ucc_prev, tok, tok, _full((8, D)), _full((1, D)), _full((1, KVW)),
                  pl.BlockSpec(memory_space=pltpu.SMEM), _full((1, 2 * D)), wspec, wspec, wspec]
                 + [_full(a.shape) for a in consts],
        out_specs=[pl.BlockSpec((tm, IN_COLS), lambda g: (rev(g), 0)), _full((SM_ROWS, 2 * D))] + [tok] * 5,
        out_shape=[jax.ShapeDtypeStruct((t, IN_COLS), BF16), jax.ShapeDtypeStruct((SM_ROWS, 2 * D), F32)]
                  + [jax.ShapeDtypeStruct((t, D), BF16)] * 5,
        scratch_shapes=[pltpu.VMEM((tm, D), F32), pltpu.VMEM((tm, D), F32),
                        pltpu.VMEM((N_KV, kb, KVW), F32), pltpu.VMEM((N_KV, kb, KVW), F32),
                        pltpu.VMEM((BLK, 2 * KVW), F32), pltpu.VMEM((8, D), F32)],
        compiler_params=_cparams(("arbitrary",)),
    )(dout, u, u, u, u, ya, yb, cw, qg, kg, sinks, gb, wco, wao, wout, *consts)


def matmul_tn(a, b, name, after=None):
    t, m = a.shape
    tk = min(1024, t)
    mb = 2176 if m == IN_COLS else m
    nk = t // tk

    def body(a_ref, b_ref, *rest):
        o_ref, acc = rest[-2:]
        k = pl.program_id(1)
        prod = _dot_tn(a_ref[...].astype(BF16), b_ref[...].astype(BF16))

        @pl.when(k == 0)
        def _():
            acc[...] = prod

        @pl.when(k > 0)
        def _():
            acc[...] += prod

        @pl.when(k == nk - 1)
        def _():
            o_ref[...] = acc[...].astype(BF16)

    return pl.pallas_call(
        body, name=name, grid=(m // mb, nk),
        in_specs=[pl.BlockSpec((tk, mb), lambda j, k: (k, j)), pl.BlockSpec((tk, D), lambda j, k: (k, 0))]
                 + ([] if after is None else [ANY]),
        out_specs=pl.BlockSpec((mb, D), lambda j, k: (j, 0)),
        out_shape=jax.ShapeDtypeStruct((m, D), BF16),
        scratch_shapes=[pltpu.VMEM((mb, D), F32)],
        compiler_params=_cparams(("arbitrary", "arbitrary")),
    )(a, b, *([] if after is None else [after]))


def inproj_bwd_x(du, w, x, ng, dout, name):
    t = x.shape[0]
    tm = min(1024, t)
    kc = 2176
    nk = IN_COLS // kc

    def body(du_ref, w_ref, x_ref, ng_ref, dout_ref, dx_ref, dng_ref, acc):
        i = pl.program_id(0)
        k = pl.program_id(1)
        prod = _dot(du_ref[...], w_ref[...])

        @pl.when(k == 0)
        def _():
            acc[...] = prod

        @pl.when(k > 0)
        def _():
            acc[...] += prod

        @pl.when((i == 0) & (k == 0))
        def _():
            dng_ref[...] = jnp.zeros_like(dng_ref)

        @pl.when(k == nk - 1)
        def _():
            dh = acc[...]
            xf = x_ref[...]
            r = lax.rsqrt(jnp.mean(xf * xf, axis=-1, keepdims=True) + EPS)
            xhat = xf * r
            dng_ref[0:1, :] += jnp.sum(dh * xhat, axis=0, keepdims=True)
            dxh = dh * ng_ref[...]
            dx_ref[...] = dout_ref[...] + r * (dxh - xhat * jnp.mean(dxh * xhat, axis=-1, keepdims=True))

    tok = pl.BlockSpec((tm, D), lambda i, k: (i, 0))
    return pl.pallas_call(
        body, name=name, grid=(t // tm, nk),
        in_specs=[pl.BlockSpec((tm, kc), lambda i, k: (i, k)), pl.BlockSpec((kc, D), lambda i, k: (k, 0)), tok,
                  pl.BlockSpec((1, D), lambda i, k: (0, 0)), tok],
        out_specs=[tok, pl.BlockSpec((8, D), lambda i, k: (0, 0))],
        out_shape=[jax.ShapeDtypeStruct((t, D), F32), jax.ShapeDtypeStruct((8, D), F32)],
        scratch_shapes=[pltpu.VMEM((tm, D), F32)],
        compiler_params=_cparams(("arbitrary", "arbitrary")),
    )(du, w, x, ng, dout)


def loss_head(y, target, name):
    t = y.shape[0]
    tm = min(1024, t)

    def body(y_ref, t_ref, dy_ref, loss_ref):
        @pl.when(pl.program_id(0) == 0)
        def _():
            loss_ref[...] = jnp.zeros_like(loss_ref)
        err = y_ref[...] - t_ref[...]
        dy_ref[...] = err * (1.0 / D)
        part = jnp.sum(jnp.sum(err * err, axis=-1, keepdims=True) * (1.0 / D), axis=0, keepdims=True)
        loss_ref[...] += 0.5 * part

    tok = pl.BlockSpec((tm, D), lambda i: (i, 0))
    return pl.pallas_call(
        body, name=name, grid=(t // tm,), in_specs=[tok, tok],
        out_specs=[tok, pl.BlockSpec((8, 128), lambda i: (0, 0))],
        out_shape=[jax.ShapeDtypeStruct((t, D), F32), jax.ShapeDtypeStruct((8, 128), F32)],
        compiler_params=_cparams(("arbitrary",)),
    )(y, target)


def layer_operands(l, norm_g, conv_w_full, q_norm_g, k_norm_g, sinks, gate_b, w_in_b, wco_b, wao_b, wout_b):
    return dict(
        ng=norm_g[l][None, :], cw=jnp.pad(conv_w_full[l], ((0, 5), (0, 0))),
        qg=jnp.tile(q_norm_g[l] * SCALE, D // HEAD)[None, :], kg=jnp.tile(k_norm_g[l], N_KV)[None, :],
        sinks=sinks[l][None, :], gb=gate_b[l][None, :],
        w_in=w_in_b[l], wco=wco_b[l], wao=wao_b[l], wout=wout_b[l])


def layer_bwd(dout, saved, lw, c, l, send_off):
    x, u, h, ya, yb = saved
    du, small, merged, yc, ob, dya, dyb = mixer_bwd(dout, u, ya, yb, lw["cw"], lw["qg"], lw["kg"], lw["sinks"], lw["gb"],
                                                    lw["wco"], lw["wao"], lw["wout"], c, f"mixer_bwd_{l}")
    grads = dict(w_in=matmul_tn(du, h, f"dw_in_{l}"), small=small)
    token = send_off(grads, False)
    grads["wout"] = matmul_tn(merged, dout, f"dw_out_{l}", after=token)
    grads["wco"] = matmul_tn(yc, dya, f"dw_conv_out_{l}")
    grads["wao"] = matmul_tn(ob, dyb, f"dw_attn_out_{l}")
    token = send_off(grads, True)
    dx, grads["dng"] = inproj_bwd_x(du, lw["w_in"], x, lw["ng"] + token[0:1, 0:1], dout, f"inproj_bwd_{l}")
    return dx, grads


MESH = pl.DeviceIdType.MESH
ANY = pl.BlockSpec(memory_space=pl.ANY)


def _place():
    return lax.axis_index("x"), lax.axis_index("y"), lax.axis_index("c")


def all_gather(arrs, after, name):
    n = len(arrs)

    def body(*refs):
        ins, outs = refs[:n], refs[n + 1:2 * n + 1]
        send_sems, recv_sems, local_sems = refs[2 * n + 1:]
        x, y, c = _place()
        me, sibling = (x, y, c), (x, y, 1 - c)
        chips = [(1 - x, y), (x, 1 - y), (1 - x, 1 - y)]

        def slot(a, block):
            px, py, pc = block
            return outs[a].at[4 * px + 2 * py + pc]

        def copy(a, k, block, to, src=None):
            return pltpu.make_async_remote_copy(
                src_ref=slot(a, block) if src is None else src, dst_ref=slot(a, block),
                send_sem=send_sems.at[a, k], recv_sem=recv_sems.at[a, k], device_id=to, device_id_type=MESH)

        mine = [pltpu.make_async_copy(ins[a], slot(a, me), local_sems.at[a]) for a in range(n)]
        for cp in mine:
            cp.start()
        first = []
        for a in range(n):
            first.append(copy(a, 0, me, sibling, src=ins[a]))
            first += [copy(a, 1 + j, me, (*chip, c), src=ins[a]) for j, chip in enumerate(chips)]
        for cp in first:
            cp.start()
        passed = []
        for j, chip in enumerate(chips):
            for a in range(n):
                copy(a, 1 + j, (*chip, c), me).wait_recv()
                passed.append(copy(a, 4 + j, (*chip, c), sibling))
                passed[-1].start()
        for a in range(n):
            copy(a, 0, sibling, me).wait_recv()
            for j, chip in enumerate(chips):
                copy(a, 4 + j, (*chip, 1 - c), me).wait_recv()
        for cp in first + passed:
            cp.wait_send()
        for cp in mine:
            cp.wait()

    return pl.pallas_call(
        body, name=name, in_specs=[ANY] * (n + 1), out_specs=[ANY] * n,
        out_shape=[jax.ShapeDtypeStruct((N_DEV,) + a.shape, a.dtype) for a in arrs],
        scratch_shapes=[pltpu.SemaphoreType.DMA((n, 7)), pltpu.SemaphoreType.DMA((n, 7)), pltpu.SemaphoreType.DMA((n,))],
    )(*arrs, after)


HBM_SPEC = pl.BlockSpec(memory_space=pltpu.HBM)
SEM_SPEC = pl.BlockSpec(memory_space=pltpu.SEMAPHORE)
EFFECT = pltpu.SideEffectType.DATAFLOW_SIDE_EFFECTING


ALL_PEERS = (1, 2, 3, 4, 5, 6, 7)


def _flip(place, k):
    x, y, c = place
    return (1 - x if k & 4 else x, 1 - y if k & 2 else y, 1 - c if k & 1 else c)


def _slot(place):
    return 4 * place[0] + 2 * place[1] + place[2]


def _exchange_copies(ins, lands, send_sems, recv_sems, scatter, flips, arriving):
    me = _place()
    out = []
    for a in range(len(ins)):
        for i, k in enumerate(flips):
            peer = _flip(me, k)
            out.append(pltpu.make_async_remote_copy(
                src_ref=ins[a].at[_slot(peer)] if scatter else ins[a],
                dst_ref=lands[a].at[_slot(peer) if arriving else _slot(me)],
                send_sem=send_sems.at[a * len(flips) + i], recv_sem=recv_sems.at[a * len(flips) + i],
                device_id=peer, device_id_type=MESH))
    return out


def exchange_start(arrs, scatter, flips, after, name):
    n = len(arrs)
    lands = [lax.empty(a.shape if scatter else (N_DEV,) + a.shape, a.dtype) for a in arrs]

    def body(*refs):
        ins, lz = refs[:n], refs[n:2 * n]
        send_sems, recv_sems = refs[2 * n + 1], refs[2 * n + 2]
        token = refs[-1]
        for cp in _exchange_copies(ins, lz, send_sems, recv_sems, scatter, flips, False):
            cp.start()
        token[...] = jnp.zeros_like(token)

    sems = pltpu.SemaphoreType.DMA((n * len(flips),))
    res = pl.pallas_call(
        body, name=name,
        out_shape=(sems, sems, *[pltpu.HBM(a.shape, a.dtype) for a in arrs], *[pltpu.HBM(a.shape, a.dtype) for a in lands],
                   jax.ShapeDtypeStruct((8, 128), F32)),
        in_specs=[HBM_SPEC] * (2 * n) + [ANY],
        out_specs=(SEM_SPEC, SEM_SPEC, *[HBM_SPEC] * (2 * n), pl.BlockSpec(memory_space=pltpu.VMEM)),
        input_output_aliases={i: 2 + i for i in range(2 * n)},
        compiler_params=pltpu.CompilerParams(has_side_effects=EFFECT),
    )(*[pltpu.with_memory_space_constraint(a, pltpu.HBM) for a in arrs],
      *[pltpu.with_memory_space_constraint(a, pltpu.HBM) for a in lands], after)
    return dict(send=res[0], recv=res[1], srcs=res[2:2 + n], lands=res[2 + n:2 + 2 * n], token=res[-1], scatter=scatter,
                flips=flips)


def exchange_wait(state, after, name):
    n = len(state["srcs"])

    def body(*refs):
        ins, lz = refs[:n], refs[n:2 * n]
        send_sems, recv_sems = refs[2 * n], refs[2 * n + 1]
        for cp in _exchange_copies(ins, lz, send_sems, recv_sems, state["scatter"], state["flips"], True):
            cp.wait_send()
            cp.wait_recv()

    both = list(state["srcs"]) + list(state["lands"])
    res = pl.pallas_call(
        body, name=name, out_shape=tuple(pltpu.HBM(a.shape, a.dtype) for a in both),
        in_specs=[HBM_SPEC] * (2 * n) + [SEM_SPEC, SEM_SPEC, ANY], out_specs=tuple([HBM_SPEC] * (2 * n)),
        input_output_aliases={i: i for i in range(2 * n)},
        compiler_params=pltpu.CompilerParams(has_side_effects=EFFECT),
    )(*both, state["send"], state["recv"], after)
    return res[:n], res[n:]


def place_own(mine, lands, name):
    n = len(lands)

    def body(*refs):
        srcs, lz = refs[:n], refs[n:2 * n]
        sems = refs[3 * n]
        me = _place()
        own = [pltpu.make_async_copy(srcs[a], lz[a].at[_slot(me)], sems.at[a]) for a in range(n)]
        for cp in own:
            cp.start()
        for cp in own:
            cp.wait()

    return pl.pallas_call(
        body, name=name, in_specs=[ANY] * (2 * n), out_specs=[ANY] * n,
        out_shape=[jax.ShapeDtypeStruct(a.shape, a.dtype) for a in lands],
        input_output_aliases={n + i: i for i in range(n)},
        scratch_shapes=[pltpu.SemaphoreType.DMA((n,))],
    )(*mine, *lands)


def adamw(w, m, v, parts, name):
    r, cdim = w.shape
    n_parts = parts.shape[0]
    rb = 256 if r % 256 == 0 else (SHARD_COLS // 4 if r == SHARD_COLS else r)

    def body(w_ref, m_ref, v_ref, p_ref, g_ref, d_ref, mo_ref, vo_ref):
        g = p_ref[0].astype(F32)
        for i in range(1, n_parts):
            g = g + p_ref[i].astype(F32)
        m_new = ADAM_B1 * m_ref[...] + (1.0 - ADAM_B1) * g
        v_new = ADAM_B2 * v_ref[...] + (1.0 - ADAM_B2) * (g * g)
        m_hat = m_new / (1.0 - ADAM_B1 ** ADAM_STEP)
        v_hat = v_new / (1.0 - ADAM_B2 ** ADAM_STEP)
        g_ref[...] = g
        d_ref[...] = -ADAM_LR * (m_hat / (jnp.sqrt(v_hat) + ADAM_EPS) + ADAM_WD * w_ref[...])
        mo_ref[...] = m_new
        vo_ref[...] = v_new

    blk = pl.BlockSpec((rb, cdim), lambda i: (i, 0))
    return pl.pallas_call(
        body, name=name, grid=(r // rb,),
        in_specs=[blk, blk, blk, pl.BlockSpec((n_parts, rb, cdim), lambda i: (0, i, 0))],
        out_specs=[blk] * 4, out_shape=[jax.ShapeDtypeStruct((r, cdim), F32)] * 4,
        compiler_params=_cparams(("arbitrary",)),
    )(w, m, v, parts)


def adamw_layers(w, m, v, lands, srcs, me, lo, prev, name):
    nl_all, r, cdim = w.shape
    nl = len(lands)
    rb = 256 if r % 256 == 0 else (SHARD_COLS // 4 if r == SHARD_COLS else r)
    nblk = r // rb
    n_prev = 0 if prev is None else 4

    def body(me_ref, w_ref, m_ref, v_ref, *rest):
        land_refs, src_refs = rest[:nl], rest[nl:2 * nl]
        g_ref, d_ref, mo_ref, vo_ref = rest[2 * nl + n_prev:]
        for k in range(nl):
            @pl.when(pl.program_id(0) == k)
            def _(k=k):
                own = src_refs[k][...].astype(F32)
                g = jnp.where(me_ref[0] == 0, own, land_refs[k][0].astype(F32))
                for i in range(1, N_DEV):
                    g = g + jnp.where(me_ref[0] == i, own, land_refs[k][i].astype(F32))
                m_new = ADAM_B1 * m_ref[...] + (1.0 - ADAM_B1) * g
                v_new = ADAM_B2 * v_ref[...] + (1.0 - ADAM_B2) * (g * g)
                m_hat = m_new / (1.0 - ADAM_B1 ** ADAM_STEP)
                v_hat = v_new / (1.0 - ADAM_B2 ** ADAM_STEP)
                g_ref[...] = g
                d_ref[...] = -ADAM_LR * (m_hat / (jnp.sqrt(v_hat) + ADAM_EPS) + ADAM_WD * w_ref[...])
                mo_ref[...] = m_new
                vo_ref[...] = v_new

    blk = pl.BlockSpec((None, rb, cdim), lambda l, i, me_ref: (lo + l, i, 0))

    def rows(l, i, k):
        return jnp.where(l < k, 0, jnp.where(l == k, i, nblk - 1))

    land_specs = [pl.BlockSpec((N_DEV, rb, cdim), lambda l, i, me_ref, k=k: (0, rows(l, i, k), 0)) for k in range(nl)]
    src_specs = [pl.BlockSpec((None, rb, cdim), lambda l, i, me_ref, k=k: (me_ref[0], rows(l, i, k), 0)) for k in range(nl)]
    return pl.pallas_call(
        body, name=name,
        grid_spec=pltpu.PrefetchScalarGridSpec(
            num_scalar_prefetch=1, grid=(nl, nblk),
            in_specs=[blk, blk, blk] + land_specs + src_specs + [ANY] * n_prev, out_specs=[blk] * 4),
        out_shape=[jax.ShapeDtypeStruct((nl_all, r, cdim), F32)] * 4,
        input_output_aliases={4 + 2 * nl + j: j for j in range(n_prev)},
        compiler_params=_cparams(("arbitrary", "arbitrary")),
    )(me, w, m, v, *lands, *srcs, *([] if prev is None else prev))


def small_sum(parts, fold, name):
    rows = parts.shape[1]

    def dot3(xv, sel):
        out = jnp.zeros((xv.shape[0], sel.shape[1]), F32)
        for _ in range(3):
            hi = xv.astype(BF16)
            out = out + _dot(hi, sel)
            xv = xv - hi.astype(F32)
        return out

    def body(p_ref, fold_ref, o_ref):
        tot = p_ref[0]
        for i in range(1, N_DEV):
            tot = tot + p_ref[i]
        o_ref[...] = tot
        for l in range(rows // SM_ROWS):
            blk = tot[l * SM_ROWS:l * SM_ROWS + 8, 0:D]
            folded = dot3(blk, fold_ref[...])
            o_ref[l * SM_ROWS + 9:l * SM_ROWS + 10, 0:128] = folded[SM_QG:SM_QG + 1, :]
            o_ref[l * SM_ROWS + 10:l * SM_ROWS + 11, 0:128] = folded[SM_KG:SM_KG + 1, :]

    return pl.pallas_call(
        body, name=name, out_shape=jax.ShapeDtypeStruct((rows, 2 * D), F32),
        compiler_params=_cparams(None),
    )(parts, fold)


def kernel(x, norm_g, w_in, conv_w, q_norm_g, k_norm_g, sinks, w_conv_out, w_attn_out, gate_b, w_out, loss_target, m_norm_g, m_w_in, m_conv_w, m_q_norm_g, m_k_norm_g, m_sinks, m_w_conv_out, m_w_attn_out, m_gate_b, m_w_out, v_norm_g, v_w_in, v_conv_w, v_q_norm_g, v_k_norm_g, v_sinks, v_w_conv_out, v_w_attn_out, v_gate_b, v_w_out):
    c = _selectors()
    me = 4 * lax.axis_index("x") + 2 * lax.axis_index("y") + lax.axis_index("c")

    w_in_t, m_w_in_t, v_w_in_t = (jnp.swapaxes(a, 1, 2) for a in (w_in, m_w_in, v_w_in))

    def shards(l):
        return [w_in_t[l].astype(BF16), w_conv_out[l].astype(BF16), w_attn_out[l].astype(BF16), w_out[l].astype(BF16)]

    def gather_start(arrs, after, tag):
        return exchange_start(arrs, False, ALL_PEERS, after, f"gather_start_{tag}")

    h = x[0]
    saved, lws = [], []
    gather = gather_start(shards(0) + [conv_w], h, 0)
    for l in range(DEPTH):
        mine, lands = exchange_wait(gather, gather["token"] if l == 0 else h, f"gather_wait_{l}")
        if l + 1 < DEPTH:
            gather = gather_start(shards(l + 1), mine[0], l + 1)
        lands = place_own(mine, lands, f"gather_own_{l}")
        if l == 0:
            conv_full = jnp.transpose(lands[4], (1, 2, 0, 3)).reshape(DEPTH, 3, D)
        lws.append(layer_operands(l, norm_g, conv_full, q_norm_g, k_norm_g, sinks, gate_b,
                                  {l: lands[0].reshape(IN_COLS, D)}, {l: lands[1].reshape(D, D)},
                                  {l: lands[2].reshape(D, D)}, {l: lands[3].reshape(D, D)}))
        u, hb = inproj_fwd(h, lws[l]["ng"] + gather["token"][0:1, 0:1], lws[l]["w_in"], f"inproj_fwd_{l}")
        x_in = h
        h, ya, yb = mixer_fwd(x_in, u, lws[l]["cw"], lws[l]["qg"], lws[l]["kg"], lws[l]["sinks"], lws[l]["gb"],
                              lws[l]["wco"], lws[l]["wao"], lws[l]["wout"], c, f"mixer_fwd_{l}")
        saved.append((x_in, u, hb, ya, yb))
    dh, loss_part = loss_head(h, loss_target[0], "loss_head")

    grads, scatters = [None] * DEPTH, [[] for _ in range(DEPTH)]
    for l in reversed(range(DEPTH)):
        def send_off(g, done, l=l):
            rest = [g[k].reshape(N_DEV, SHARD_ROWS, D) for k in ("wco", "wao", "wout")] if done else []
            first = [g["w_in"].reshape(N_DEV, SHARD_COLS, D)] if done == (l > 0) else []
            if not first + rest:
                return None
            tag = f"{l}" if l > 0 else ("0_rest" if done else "0_in")
            scatters[l].append(exchange_start(first + rest, True, ALL_PEERS, g["small"], f"scatter_start_{tag}"))
            return scatters[l][-1]["token"]
        dh, grads[l] = layer_bwd(dh, saved[l], lws[l], c, l, send_off)

    me1 = me.astype(jnp.int32).reshape(1)
    mine, lands = {}, {}
    for l in (3, 2, 1):
        mine[l], lands[l] = exchange_wait(scatters[l][0], dh, f"scatter_wait_{l}")
    weights = [(w_in_t, m_w_in_t, v_w_in_t, "w_in"), (w_conv_out, m_w_conv_out, v_w_conv_out, "w_conv_out"),
               (w_attn_out, m_w_attn_out, v_w_attn_out, "w_attn_out"), (w_out, m_w_out, v_w_out, "w_out")]
    upd = [adamw_layers(w, m, v, [lands[l][i] for l in (1, 2, 3)], [mine[l][i] for l in (1, 2, 3)], me1, 1, None,
                        f"adamw_{n}_upper") for i, (w, m, v, n) in enumerate(weights)]
    m_in, l_in = exchange_wait(scatters[0][0], upd[0][0], "scatter_wait_0_in")
    m_rest, l_rest = exchange_wait(scatters[0][1], upd[3][0], "scatter_wait_0_rest")
    mine[0], lands[0] = list(m_in) + list(m_rest), list(l_in) + list(l_rest)
    upd = [adamw_layers(w, m, v, [lands[0][i]], [mine[0][i]], me1, 0, upd[i], f"adamw_{n}_0")
           for i, (w, m, v, n) in enumerate(weights)]
    u_in = [jnp.swapaxes(o, 1, 2) for o in upd[0]]
    u_co, u_ao, u_out = upd[1], upd[2], upd[3]

    blocks = []
    for l in range(DEPTH):
        blk = grads[l]["small"]
        blk = blk.at[SM_NORM, 0:D].set(grads[l]["dng"][0])
        if l == 0:
            blk = blk.at[SM_LOSS, 0:128].set(loss_part[0])
        blocks.append(blk)
    gathered = all_gather([jnp.concatenate(blocks, axis=0)], u_out[0], "gather_small")[0]
    tot = small_sum(gathered, c["fold"], "small_sum")
    tot = tot.reshape(DEPTH, SM_ROWS, 2 * D)
    loss = tot[0, SM_LOSS, 0]

    def update_small(w, m, v, g, name):
        return adamw(w, m, v, g[None], name)

    u_ng = update_small(norm_g, m_norm_g, v_norm_g, tot[:, SM_NORM, 0:D], "adamw_norm_g")
    u_qg = update_small(q_norm_g, m_q_norm_g, v_q_norm_g, tot[:, 9, 0:HEAD], "adamw_q_norm_g")
    u_kg = update_small(k_norm_g, m_k_norm_g, v_k_norm_g, tot[:, 10, 0:HEAD], "adamw_k_norm_g")
    u_sk = update_small(sinks, m_sinks, v_sinks, tot[:, SM_SINK, 0:16], "adamw_sinks")
    u_gb = update_small(gate_b, m_gate_b, v_gate_b, tot[:, SM_GATE, :], "adamw_gate_b")
    g_conv = lax.dynamic_slice_in_dim(tot[:, SM_CONV:SM_CONV + 3, 0:D], me * SHARD_ROWS, SHARD_ROWS, axis=2)
    u_cw = [o.reshape(DEPTH, 3, SHARD_ROWS) for o in update_small(
        conv_w.reshape(DEPTH * 3, SHARD_ROWS), m_conv_w.reshape(DEPTH * 3, SHARD_ROWS),
        v_conv_w.reshape(DEPTH * 3, SHARD_ROWS), g_conv.reshape(DEPTH * 3, SHARD_ROWS), "adamw_conv_w")]

    order = [u_ng, u_in, u_cw, u_qg, u_kg, u_sk, u_co, u_ao, u_gb, u_out]
    return (loss, dh[None], *[u[0] for u in order], *[u[1] for u in order], *[u[2] for u in order], *[u[3] for u in order])
```

```python
import functools

import jax
import jax.numpy as jnp
from jax import lax
from jax.experimental import pallas as pl
from jax.experimental.pallas import tpu as pltpu

F32 = jnp.float32
BF16 = jnp.bfloat16

N_DEV = 8
DEPTH = 4
D = 1024
N_KV = 4
GROUP = 4
HEAD = 64
BLK = 128
KVW = N_KV * HEAD
IN_COLS = 8704
SHARD_COLS = IN_COLS // N_DEV
SHARD_ROWS = D // N_DEV
C_VC, C_BC, C_CC, C_ZC, C_Q, C_K, C_V, C_ZA, C_GA, C_GB = 0, 1024, 2048, 3072, 4096, 5120, 5376, 5632, 6656, 7680
EPS = 1e-6
NEG_INF = -1e30
SCALE = HEAD ** -0.5

ADAM_LR = 0.001
ADAM_B1 = 0.9
ADAM_B2 = 0.999
ADAM_EPS = 1e-08
ADAM_WD = 0.01
ADAM_STEP = 10

VMEM_LIMIT = 60 * 1024 * 1024
SM_ROWS = 16
SM_GATE, SM_CONV, SM_QG, SM_KG, SM_SINK, SM_NORM, SM_LOSS = 0, 1, 4, 5, 6, 7, 8


def _cparams(sem):
    return pltpu.CompilerParams(dimension_semantics=sem, vmem_limit_bytes=VMEM_LIMIT)


def _dot(a, b):
    return jnp.dot(a, b, preferred_element_type=F32)


def _dot_nt(a, b):
    return lax.dot_general(a, b, (((1,), (1,)), ((), ())), preferred_element_type=F32)


def _dot_tn(a, b):
    return lax.dot_general(a, b, (((0,), (0,)), ((), ())), preferred_element_type=F32)


def _dot2(x, sel):
    hi = x.astype(BF16)
    lo = (x - hi.astype(F32)).astype(BF16)
    return _dot(hi, sel) + _dot(lo, sel)


def _sigmoid(z):
    return 1.0 / (1.0 + jnp.exp(-z))


def _head_mean(t, sel, exp):
    return _dot2(_dot2(t, sel) * (1.0 / HEAD), exp)


def _bf(ref, c0, width):
    return ref[:, c0:c0 + width].astype(F32)


def _selectors():
    c = jnp.arange(D)
    sel_q = (c[:, None] // HEAD == jnp.arange(128)[None, :]).astype(BF16)
    ck = jnp.arange(KVW)
    sel_k = (ck[:, None] // HEAD == jnp.arange(128)[None, :]).astype(BF16)
    src = jnp.arange(KVW)[:, None]
    dst = jnp.arange(KVW)[None, :]
    rep = jnp.stack([((src // HEAD == h) & (src % HEAD == dst % HEAD)).astype(BF16) for h in range(N_KV)])
    fold = (c[:, None] % HEAD == jnp.arange(128)[None, :]).astype(BF16)
    return dict(sel_q=sel_q, exp_q=sel_q.T, sel_k=sel_k, exp_k=sel_k.T, rep=rep, rep_t=jnp.swapaxes(rep, 1, 2), fold=fold)


def inproj_fwd(x, ng, w, name):
    t = x.shape[0]
    tm = min(1024, t)
    cb = 2176
    def body(x_ref, ng_ref, w_ref, u_ref, h_ref, h_scr):
        @pl.when(pl.program_id(1) == 0)
        def _():
            xf = x_ref[...]
            r = lax.rsqrt(jnp.mean(xf * xf, axis=-1, keepdims=True) + EPS)
            hb = (xf * r * ng_ref[...]).astype(BF16)
            h_scr[...] = hb
            h_ref[...] = hb
        u_ref[...] = _dot_nt(h_scr[...], w_ref[...]).astype(BF16)

    return pl.pallas_call(
        body, name=name, grid=(t // tm, IN_COLS // cb),
        in_specs=[pl.BlockSpec((tm, D), lambda i, j: (i, 0)), pl.BlockSpec((1, D), lambda i, j: (0, 0)),
                  pl.BlockSpec((cb, D), lambda i, j: (j, 0))],
        out_specs=[pl.BlockSpec((tm, cb), lambda i, j: (i, j)), pl.BlockSpec((tm, D), lambda i, j: (i, 0))],
        out_shape=[jax.ShapeDtypeStruct((t, IN_COLS), BF16), jax.ShapeDtypeStruct((t, D), BF16)],
        scratch_shapes=[pltpu.VMEM((tm, D), BF16)],
        compiler_params=_cparams(("arbitrary", "arbitrary")),
    )(x, ng, w)


def _conv_fwd(u_ref, uvc_prev, ucc_prev, cw_ref, is_first, tm):
    p = _bf(u_ref, C_CC, D) * _bf(u_ref, C_VC, D)
    pprev = ucc_prev[...].astype(F32) * uvc_prev[...].astype(F32)
    pprev = jnp.where(is_first, 0.0, pprev)
    row = lax.broadcasted_iota(jnp.int32, (tm, 1), 0)
    p1 = jnp.where(row == 0, pprev[15:16, :], pltpu.roll(p, 1, 0))
    p2 = jnp.where(row == 0, pprev[14:15, :], jnp.where(row == 1, pprev[15:16, :], pltpu.roll(p, 2, 0)))
    cw = cw_ref[...]
    conv = cw[0:1, :] * p2 + cw[1:2, :] * p1 + cw[2:3, :] * p
    return p, p1, p2, conv


def _attn_inputs(u_ref, ukv_prev, qg_ref, kg_ref, c, tm):
    q = _bf(u_ref, C_Q, D)
    rq = lax.rsqrt(_head_mean(q * q, c["sel_q"][...], c["exp_q"][...]) + EPS)
    qhat = q * rq
    qn = (qhat * qg_ref[...]).astype(BF16)
    kband = jnp.concatenate([ukv_prev[:, 0:KVW].astype(F32), _bf(u_ref, C_K, KVW)], axis=0)
    rk = lax.rsqrt(_head_mean(kband * kband, c["sel_k"][...], c["exp_k"][...]) + EPS)
    khat = kband * rk
    knb = (khat * kg_ref[...]).astype(BF16)
    vband = jnp.concatenate([ukv_prev[:, KVW:2 * KVW], u_ref[:, C_V:C_V + KVW]], axis=0)
    kt = [_dot(knb, c["rep"][h]).astype(BF16) for h in range(N_KV)]
    vt = [_dot(vband, c["rep"][h]).astype(BF16) for h in range(N_KV)]
    return qhat, rq, qn, khat, rk, kt, vt


def _attn_masks(is_first):
    rows = GROUP * BLK
    r = lax.broadcasted_iota(jnp.int32, (rows, 2 * BLK), 0)
    kk = lax.broadcasted_iota(jnp.int32, (rows, 2 * BLK), 1)
    qq = r % BLK
    valid = (kk > qq) & (kk <= qq + BLK)
    valid_first = valid & ((kk >= BLK) | jnp.logical_not(is_first))
    lane_grp = lax.broadcasted_iota(jnp.int32, (BLK, KVW), 1) // HEAD
    row_grp = lax.broadcasted_iota(jnp.int32, (rows, 1), 0) // BLK
    return valid, valid_first, lane_grp, row_grp


def _sink_col(sinks_ref, h, row_grp):
    col = jnp.full(row_grp.shape, sinks_ref[0, GROUP * h], F32)
    for gi in range(1, GROUP):
        col = jnp.where(row_grp == gi, sinks_ref[0, GROUP * h + gi], col)
    return col


def _stack_groups(a256, lane_grp):
    zero = jnp.zeros_like(a256)
    return jnp.concatenate([jnp.where(lane_grp == gi, a256, zero) for gi in range(GROUP)], axis=0)


def _unstack_groups(a4, lane_grp):
    out = jnp.where(lane_grp == 0, a4[0:BLK], 0.0)
    for gi in range(1, GROUP):
        out = out + jnp.where(lane_grp == gi, a4[gi * BLK:(gi + 1) * BLK], 0.0)
    return out


def _softmax_block(qs, kt_b, valid, sink):
    s = _dot_nt(qs, kt_b)
    s = jnp.where(valid, s, NEG_INF)
    m = jnp.maximum(jnp.max(s, axis=-1, keepdims=True), sink)
    e = jnp.exp(s - m)
    es = jnp.exp(sink - m)
    inv = 1.0 / (jnp.sum(e, axis=-1, keepdims=True) + es)
    return e * inv, es * inv


def _mixer_specs(t, tm, n_tiles, tile_of):
    nb = tm // BLK
    u_spec = pl.BlockSpec((tm, IN_COLS), lambda g: (tile_of(g), 0))
    ukv_prev = pl.BlockSpec((BLK, 2 * KVW), lambda g: (jnp.maximum(tile_of(g) * nb - 1, 0), C_K // (2 * KVW)))
    uvc_prev = pl.BlockSpec((16, D), lambda g: (jnp.maximum(tile_of(g) * (tm // 16) - 1, 0), C_VC // D))
    ucc_prev = pl.BlockSpec((16, D), lambda g: (jnp.maximum(tile_of(g) * (tm // 16) - 1, 0), C_CC // D))
    return u_spec, ukv_prev, uvc_prev, ucc_prev


def _full(shape):
    n = len(shape)
    return pl.BlockSpec(shape, lambda g: (0,) * n)


def mixer_fwd(x, u, cw, qg, kg, sinks, gb, wco, wao, wout, c, name):
    t = x.shape[0]
    tm = min(256, t)
    n_tiles = t // tm
    nb = tm // BLK
    cn = sorted(c)

    def body(x_ref, u_ref, ukv_prev, uvc_prev, ucc_prev, cw_ref, qg_ref, kg_ref, sinks_ref, gb_ref, wco_ref, wao_ref,
             wout_ref, *rest):
        cref = dict(zip(cn, rest[:len(cn)]))
        xo_ref, ya_ref, yb_ref, o_scr = rest[len(cn):]
        is_first = pl.program_id(0) == 0
        _, _, _, conv = _conv_fwd(u_ref, uvc_prev, ucc_prev, cw_ref, is_first, tm)
        zc = _bf(u_ref, C_ZC, D)
        yc = _bf(u_ref, C_BC, D) * conv * (zc * _sigmoid(zc))
        ya = _dot(yc.astype(BF16), wco_ref[...])

        _, _, qn, _, _, kt, vt = _attn_inputs(u_ref, ukv_prev, qg_ref, kg_ref, cref, tm)
        valid, valid_first, lane_grp, row_grp = _attn_masks(is_first)
        for h in range(N_KV):
            sink = _sink_col(sinks_ref, h, row_grp)
            for b in range(nb):
                qs = _stack_groups(qn[b * BLK:(b + 1) * BLK, h * KVW:(h + 1) * KVW], lane_grp)
                pn, _ = _softmax_block(qs, kt[h][b * BLK:(b + 2) * BLK], valid_first if b == 0 else valid, sink)
                o4 = _dot(pn.astype(BF16), vt[h][b * BLK:(b + 2) * BLK])
                o_scr[b * BLK:(b + 1) * BLK, h * KVW:(h + 1) * KVW] = _unstack_groups(o4, lane_grp)
        za = _bf(u_ref, C_ZA, D)
        ob = o_scr[...] * (za * _sigmoid(za))
        yb = _dot(ob.astype(BF16), wao_ref[...])

        g_a = _sigmoid(_bf(u_ref, C_GA, D) + gb_ref[:, 0:D])
        g_b = _sigmoid(_bf(u_ref, C_GB, D) + gb_ref[:, D:2 * D])
        merged = g_a * ya + g_b * yb
        xo_ref[...] = x_ref[...] + _dot(merged.astype(BF16), wout_ref[...])
        ya_ref[...] = ya.astype(BF16)
        yb_ref[...] = yb.astype(BF16)

    u_spec, ukv_prev, uvc_prev, ucc_prev = _mixer_specs(t, tm, n_tiles, lambda g: g)
    tok = pl.BlockSpec((tm, D), lambda g: (g, 0))
    consts = [c[k] for k in cn]
    return pl.pallas_call(
        body, name=name, grid=(n_tiles,),
        in_specs=[tok, u_spec, ukv_prev, uvc_prev, ucc_prev, _full((8, D)), _full((1, D)), _full((1, KVW)),
                  pl.BlockSpec(memory_space=pltpu.SMEM), _full((1, 2 * D)), _full((D, D)), _full((D, D)), _full((D, D))]
                 + [_full(a.shape) for a in consts],
        out_specs=[tok, tok, tok],
        out_shape=[jax.ShapeDtypeStruct((t, D), F32), jax.ShapeDtypeStruct((t, D), BF16), jax.ShapeDtypeStruct((t, D), BF16)],
        scratch_shapes=[pltpu.VMEM((tm, D), F32)],
        compiler_params=_cparams(("arbitrary",)),
    )(x, u, u, u, u, cw, qg, kg, sinks, gb, wco, wao, wout, *consts)


def mixer_bwd(dout, u, ya, yb, cw, qg, kg, sinks, gb, wco, wao, wout, c, name):
    t = dout.shape[0]
    tm = min(256, t)
    n_tiles = t // tm
    nb = tm // BLK
    kb = tm + BLK
    cn = sorted(c)

    def body(dout_ref, u_ref, ukv_prev, uvc_prev, ucc_prev, ya_ref, yb_ref, cw_ref, qg_ref, kg_ref, sinks_ref, gb_ref,
             wco_ref, wao_ref, wout_ref, *rest):
        cref = dict(zip(cn, rest[:len(cn)]))
        (du_ref, small_ref, merged_ref, yc_ref, ob_ref, dya_ref, dyb_ref,
         o_scr, dq_scr, dk4_scr, dv4_scr, carry_kv, carry_conv) = rest[len(cn):]
        g = pl.program_id(0)
        is_first = g == n_tiles - 1

        @pl.when(g == 0)
        def _():
            carry_kv[...] = jnp.zeros_like(carry_kv)
            carry_conv[...] = jnp.zeros_like(carry_conv)
            small_ref[...] = jnp.zeros_like(small_ref)

        dout = dout_ref[...]
        dout_b = dout.astype(BF16)
        ya_v = ya_ref[...].astype(F32)
        yb_v = yb_ref[...].astype(F32)
        g_a = _sigmoid(_bf(u_ref, C_GA, D) + gb_ref[:, 0:D])
        g_b = _sigmoid(_bf(u_ref, C_GB, D) + gb_ref[:, D:2 * D])
        merged = g_a * ya_v + g_b * yb_v
        dmerged = _dot_nt(dout_b, wout_ref[...])
        merged_ref[...] = merged.astype(BF16)
        dya = dmerged * g_a
        dyb = dmerged * g_b
        dgl_a = dmerged * ya_v * g_a * (1.0 - g_a)
        dgl_b = dmerged * yb_v * g_b * (1.0 - g_b)
        du_ref[:, C_GA:C_GA + D] = dgl_a.astype(BF16)
        du_ref[:, C_GB:C_GB + D] = dgl_b.astype(BF16)
        small_ref[SM_GATE:SM_GATE + 1, 0:D] += jnp.sum(dgl_a, axis=0, keepdims=True)
        small_ref[SM_GATE:SM_GATE + 1, D:2 * D] += jnp.sum(dgl_b, axis=0, keepdims=True)

        p, p1, p2, conv = _conv_fwd(u_ref, uvc_prev, ucc_prev, cw_ref, is_first, tm)
        zc = _bf(u_ref, C_ZC, D)
        bc = _bf(u_ref, C_BC, D)
        sg = _sigmoid(zc)
        sc = zc * sg
        yc = bc * conv * sc
        dya_b = dya.astype(BF16)
        yc_ref[...] = yc.astype(BF16)
        dya_ref[...] = dya_b
        dyc = _dot_nt(dya_b, wco_ref[...])
        du_ref[:, C_BC:C_BC + D] = (dyc * conv * sc).astype(BF16)
        du_ref[:, C_ZC:C_ZC + D] = (dyc * bc * conv * (sg * (1.0 + zc * (1.0 - sg)))).astype(BF16)
        dconv = dyc * bc * sc
        small_ref[SM_CONV + 2:SM_CONV + 3, 0:D] += jnp.sum(dconv * p, axis=0, keepdims=True)
        small_ref[SM_CONV + 1:SM_CONV + 2, 0:D] += jnp.sum(dconv * p1, axis=0, keepdims=True)
        small_ref[SM_CONV:SM_CONV + 1, 0:D] += jnp.sum(dconv * p2, axis=0, keepdims=True)
        row = lax.broadcasted_iota(jnp.int32, (tm, 1), 0)
        nxt = carry_conv[...]
        d1 = jnp.where(row == tm - 1, nxt[0:1, :], pltpu.roll(dconv, tm - 1, 0))
        d2 = jnp.where(row == tm - 1, nxt[1:2, :], jnp.where(row == tm - 2, nxt[0:1, :], pltpu.roll(dconv, tm - 2, 0)))
        carry_conv[...] = dconv[0:8, :]
        cw = cw_ref[...]
        dp = cw[2:3, :] * dconv + cw[1:2, :] * d1 + cw[0:1, :] * d2
        du_ref[:, C_CC:C_CC + D] = (dp * _bf(u_ref, C_VC, D)).astype(BF16)
        du_ref[:, C_VC:C_VC + D] = (dp * _bf(u_ref, C_CC, D)).astype(BF16)

        dyb_b = dyb.astype(BF16)
        dob = _dot_nt(dyb_b, wao_ref[...])
        za = _bf(u_ref, C_ZA, D)
        sga = _sigmoid(za)
        sa = za * sga
        do = dob * sa
        qhat, rq, qn, khat, rk, kt, vt = _attn_inputs(u_ref, ukv_prev, qg_ref, kg_ref, cref, tm)
        valid, valid_first, lane_grp, row_grp = _attn_masks(is_first)
        dk4_scr[...] = jnp.zeros_like(dk4_scr)
        dv4_scr[...] = jnp.zeros_like(dv4_scr)
        lane16 = lax.broadcasted_iota(jnp.int32, (1, 2 * D), 1)
        dsink_row = jnp.zeros((1, 2 * D), F32)
        for h in range(N_KV):
            sink = _sink_col(sinks_ref, h, row_grp)
            dsink_col = jnp.zeros((GROUP * BLK, 1), F32)
            for b in range(nb):
                rows = slice(b * BLK, (b + 1) * BLK)
                band = slice(b * BLK, (b + 2) * BLK)
                cols = slice(h * KVW, (h + 1) * KVW)
                qs = _stack_groups(qn[rows, cols], lane_grp)
                pn, ps = _softmax_block(qs, kt[h][band], valid_first if b == 0 else valid, sink)
                pn_b = pn.astype(BF16)
                o4 = _dot(pn_b, vt[h][band])
                o_scr[rows, cols] = _unstack_groups(o4, lane_grp)
                dos = _stack_groups(do[rows, cols], lane_grp).astype(BF16)
                dpn = _dot_nt(dos, vt[h][band])
                delta = jnp.sum(pn * dpn, axis=-1, keepdims=True)
                ds = (pn * (dpn - delta)).astype(BF16)
                dsink_col = dsink_col - ps * delta
                dq_scr[rows, cols] = _unstack_groups(_dot(ds, kt[h][band]), lane_grp)
                dk4_scr[h, band, :] += _dot_tn(ds, qs)
                dv4_scr[h, band, :] += _dot_tn(pn_b, dos)
            for gi in range(GROUP):
                tot = jnp.sum(dsink_col[gi * BLK:(gi + 1) * BLK, :], axis=0, keepdims=True)
                dsink_row = dsink_row + jnp.where(lane16 == GROUP * h + gi, tot, 0.0)
        small_ref[SM_SINK:SM_SINK + 1, :] += dsink_row

        o = o_scr[...]
        ob_ref[...] = (o * sa).astype(BF16)
        dyb_ref[...] = dyb_b
        du_ref[:, C_ZA:C_ZA + D] = (dob * o * (sga * (1.0 + za * (1.0 - sga)))).astype(BF16)

        dqn = dq_scr[...]
        small_ref[SM_QG:SM_QG + 1, 0:D] += SCALE * jnp.sum(dqn * qhat, axis=0, keepdims=True)
        dqh = dqn * qg_ref[...]
        dq = rq * (dqh - qhat * _head_mean(dqh * qhat, cref["sel_q"][...], cref["exp_q"][...]))
        du_ref[:, C_Q:C_Q + D] = dq.astype(BF16)

        dkn_band = jnp.zeros((kb, KVW), F32)
        dv_band = jnp.zeros((kb, KVW), F32)
        for h in range(N_KV):
            dkn_band = dkn_band + _dot2(dk4_scr[h], cref["rep_t"][h])
            dv_band = dv_band + _dot2(dv4_scr[h], cref["rep_t"][h])
        carried = carry_kv[...]
        pad = jnp.zeros((tm - BLK, KVW), F32)
        if nb > 1:
            dkn = dkn_band[BLK:, :] + jnp.concatenate([pad, carried[:, 0:KVW]], axis=0)
            dv = dv_band[BLK:, :] + jnp.concatenate([pad, carried[:, KVW:2 * KVW]], axis=0)
        else:
            dkn = dkn_band[BLK:, :] + carried[:, 0:KVW]
            dv = dv_band[BLK:, :] + carried[:, KVW:2 * KVW]
        carry_kv[:, 0:KVW] = dkn_band[0:BLK, :]
        carry_kv[:, KVW:2 * KVW] = dv_band[0:BLK, :]
        khat_t = khat[BLK:, :]
        small_ref[SM_KG:SM_KG + 1, 0:KVW] += jnp.sum(dkn * khat_t, axis=0, keepdims=True)
        dkh = dkn * kg_ref[...]
        dk = rk[BLK:, :] * (dkh - khat_t * _head_mean(dkh * khat_t, cref["sel_k"][...], cref["exp_k"][...]))
        du_ref[:, C_K:C_K + KVW] = dk.astype(BF16)
        du_ref[:, C_V:C_V + KVW] = dv.astype(BF16)

    rev = lambda g: n_tiles - 1 - g
    u_spec, ukv_prev, uvc_prev, ucc_prev = _mixer_specs(t, tm, n_tiles, rev)
    tok = pl.BlockSpec((tm, D), lambda g: (rev(g), 0))
    consts = [c[k] for k in cn]
    wspec = _full((D, D))
    return pl.pallas_call(
        body, name=name, grid=(n_tiles,),
        in_specs=[tok, u_spec, ukv_prev, uvc_prev, ucc_prev, tok, tok, _full((8, D)), _full((1, D)), _full((1, KVW)),
                  pl.BlockSpec(memory_space=pltpu.SMEM), _full((1, 2 * D)), wspec, wspec, wspec]
                 + [_full(a.shape) for a in consts],
        out_specs=[pl.BlockSpec((tm, IN_COLS), lambda g: (rev(g), 0)), _full((SM_ROWS, 2 * D))] + [tok] * 5,
        out_shape=[jax.ShapeDtypeStruct((t, IN_COLS), BF16), jax.ShapeDtypeStruct((SM_ROWS, 2 * D), F32)]
                  + [jax.ShapeDtypeStruct((t, D), BF16)] * 5,
        scratch_shapes=[pltpu.VMEM((tm, D), F32), pltpu.VMEM((tm, D), F32),
                        pltpu.VMEM((N_KV, kb, KVW), F32), pltpu.VMEM((N_KV, kb, KVW), F32),
                        pltpu.VMEM((BLK, 2 * KVW), F32), pltpu.VMEM((8, D), F32)],
        compiler_params=_cparams(("arbitrary",)),
    )(dout, u, u, u, u, ya, yb, cw, qg, kg, sinks, gb, wco, wao, wout, *consts)


def matmul_tn(a, b, name, after=None):
    t, m = a.shape
    tk = min(1024, t)
    mb = 2176 if m == IN_COLS else m
    nk = t // tk

    def body(a_ref, b_ref, *rest):
        o_ref, acc = rest[-2:]
        k = pl.program_id(1)
        prod = _dot_tn(a_ref[...].astype(BF16), b_ref[...].astype(BF16))

        @pl.when(k == 0)
        def _():
            acc[...] = prod

        @pl.when(k > 0)
        def _():
            acc[...] += prod

        @pl.when(k == nk - 1)
        def _():
            o_ref[...] = acc[...].astype(BF16)

    return pl.pallas_call(
        body, name=name, grid=(m // mb, nk),
        in_specs=[pl.BlockSpec((tk, mb), lambda j, k: (k, j)), pl.BlockSpec((tk, D), lambda j, k: (k, 0))]
                 + ([] if after is None else [ANY]),
        out_specs=pl.BlockSpec((mb, D), lambda j, k: (j, 0)),
        out_shape=jax.ShapeDtypeStruct((m, D), BF16),
        scratch_shapes=[pltpu.VMEM((mb, D), F32)],
        compiler_params=_cparams(("arbitrary", "arbitrary")),
    )(a, b, *([] if after is None else [after]))


def inproj_bwd_x(du, w, x, ng, dout, name):
    t = x.shape[0]
    tm = min(1024, t)
    kc = 2176
    nk = IN_COLS // kc

    def body(du_ref, w_ref, x_ref, ng_ref, dout_ref, dx_ref, dng_ref, acc):
        i = pl.program_id(0)
        k = pl.program_id(1)
        prod = _dot(du_ref[...], w_ref[...])

        @pl.when(k == 0)
        def _():
            acc[...] = prod

        @pl.when(k > 0)
        def _():
            acc[...] += prod

        @pl.when((i == 0) & (k == 0))
        def _():
            dng_ref[...] = jnp.zeros_like(dng_ref)

        @pl.when(k == nk - 1)
        def _():
            dh = acc[...]
            xf = x_ref[...]
            r = lax.rsqrt(jnp.mean(xf * xf, axis=-1, keepdims=True) + EPS)
            xhat = xf * r
            dng_ref[0:1, :] += jnp.sum(dh * xhat, axis=0, keepdims=True)
            dxh = dh * ng_ref[...]
            dx_ref[...] = dout_ref[...] + r * (dxh - xhat * jnp.mean(dxh * xhat, axis=-1, keepdims=True))

    tok = pl.BlockSpec((tm, D), lambda i, k: (i, 0))
    return pl.pallas_call(
        body, name=name, grid=(t // tm, nk),
        in_specs=[pl.BlockSpec((tm, kc), lambda i, k: (i, k)), pl.BlockSpec((kc, D), lambda i, k: (k, 0)), tok,
                  pl.BlockSpec((1, D), lambda i, k: (0, 0)), tok],
        out_specs=[tok, pl.BlockSpec((8, D), lambda i, k: (0, 0))],
        out_shape=[jax.ShapeDtypeStruct((t, D), F32), jax.ShapeDtypeStruct((8, D), F32)],
        scratch_shapes=[pltpu.VMEM((tm, D), F32)],
        compiler_params=_cparams(("arbitrary", "arbitrary")),
    )(du, w, x, ng, dout)


def loss_head(y, target, name):
    t = y.shape[0]
    tm = min(1024, t)

    def body(y_ref, t_ref, dy_ref, loss_ref):
        @pl.when(pl.program_id(0) == 0)
        def _():
            loss_ref[...] = jnp.zeros_like(loss_ref)
        err = y_ref[...] - t_ref[...]
        dy_ref[...] = err * (1.0 / D)
        part = jnp.sum(jnp.sum(err * err, axis=-1, keepdims=True) * (1.0 / D), axis=0, keepdims=True)
        loss_ref[...] += 0.5 * part

    tok = pl.BlockSpec((tm, D), lambda i: (i, 0))
    return pl.pallas_call(
        body, name=name, grid=(t // tm,), in_specs=[tok, tok],
        out_specs=[tok, pl.BlockSpec((8, 128), lambda i: (0, 0))],
        out_shape=[jax.ShapeDtypeStruct((t, D), F32), jax.ShapeDtypeStruct((8, 128), F32)],
        compiler_params=_cparams(("arbitrary",)),
    )(y, target)


def layer_operands(l, norm_g, conv_w_full, q_norm_g, k_norm_g, sinks, gate_b, w_in_b, wco_b, wao_b, wout_b):
    return dict(
        ng=norm_g[l][None, :], cw=jnp.pad(conv_w_full[l], ((0, 5), (0, 0))),
        qg=jnp.tile(q_norm_g[l] * SCALE, D // HEAD)[None, :], kg=jnp.tile(k_norm_g[l], N_KV)[None, :],
        sinks=sinks[l][None, :], gb=gate_b[l][None, :],
        w_in=w_in_b[l], wco=wco_b[l], wao=wao_b[l], wout=wout_b[l])


def layer_bwd(dout, saved, lw, c, l, send_off):
    x, u, h, ya, yb = saved
    du, small, merged, yc, ob, dya, dyb = mixer_bwd(dout, u, ya, yb, lw["cw"], lw["qg"], lw["kg"], lw["sinks"], lw["gb"],
                                                    lw["wco"], lw["wao"], lw["wout"], c, f"mixer_bwd_{l}")
    grads = dict(w_in=matmul_tn(du, h, f"dw_in_{l}"), small=small)
    token = send_off(grads, False)
    grads["wout"] = matmul_tn(merged, dout, f"dw_out_{l}", after=token)
    grads["wco"] = matmul_tn(yc, dya, f"dw_conv_out_{l}")
    grads["wao"] = matmul_tn(ob, dyb, f"dw_attn_out_{l}")
    token = send_off(grads, True)
    dx, grads["dng"] = inproj_bwd_x(du, lw["w_in"], x, lw["ng"] + token[0:1, 0:1], dout, f"inproj_bwd_{l}")
    return dx, grads


MESH = pl.DeviceIdType.MESH
ANY = pl.BlockSpec(memory_space=pl.ANY)


def _place():
    return lax.axis_index("x"), lax.axis_index("y"), lax.axis_index("c")


def all_gather(arrs, after, name):
    n = len(arrs)

    def body(*refs):
        ins, outs = refs[:n], refs[n + 1:2 * n + 1]
        send_sems, recv_sems, local_sems = refs[2 * n + 1:]
        x, y, c = _place()
        me, sibling = (x, y, c), (x, y, 1 - c)
        chips = [(1 - x, y), (x, 1 - y), (1 - x, 1 - y)]

        def slot(a, block):
            px, py, pc = block
            return outs[a].at[4 * px + 2 * py + pc]

        def copy(a, k, block, to, src=None):
            return pltpu.make_async_remote_copy(
                src_ref=slot(a, block) if src is None else src, dst_ref=slot(a, block),
                send_sem=send_sems.at[a, k], recv_sem=recv_sems.at[a, k], device_id=to, device_id_type=MESH)

        mine = [pltpu.make_async_copy(ins[a], slot(a, me), local_sems.at[a]) for a in range(n)]
        for cp in mine:
            cp.start()
        first = []
        for a in range(n):
            first.append(copy(a, 0, me, sibling, src=ins[a]))
            first += [copy(a, 1 + j, me, (*chip, c), src=ins[a]) for j, chip in enumerate(chips)]
        for cp in first:
            cp.start()
        passed = []
        for j, chip in enumerate(chips):
            for a in range(n):
                copy(a, 1 + j, (*chip, c), me).wait_recv()
                passed.append(copy(a, 4 + j, (*chip, c), sibling))
                passed[-1].start()
        for a in range(n):
            copy(a, 0, sibling, me).wait_recv()
            for j, chip in enumerate(chips):
                copy(a, 4 + j, (*chip, 1 - c), me).wait_recv()
        for cp in first + passed:
            cp.wait_send()
        for cp in mine:
            cp.wait()

    return pl.pallas_call(
        body, name=name, in_specs=[ANY] * (n + 1), out_specs=[ANY] * n,
        out_shape=[jax.ShapeDtypeStruct((N_DEV,) + a.shape, a.dtype) for a in arrs],
        scratch_shapes=[pltpu.SemaphoreType.DMA((n, 7)), pltpu.SemaphoreType.DMA((n, 7)), pltpu.SemaphoreType.DMA((n,))],
    )(*arrs, after)


HBM_SPEC = pl.BlockSpec(memory_space=pltpu.HBM)
SEM_SPEC = pl.BlockSpec(memory_space=pltpu.SEMAPHORE)
EFFECT = pltpu.SideEffectType.DATAFLOW_SIDE_EFFECTING


ALL_PEERS = (1, 2, 3, 4, 5, 6, 7)


def _flip(place, k):
    x, y, c = place
    return (1 - x if k & 4 else x, 1 - y if k & 2 else y, 1 - c if k & 1 else c)


def _slot(place):
    return 4 * place[0] + 2 * place[1] + place[2]


def _exchange_copies(ins, lands, send_sems, recv_sems, scatter, flips, arriving):
    me = _place()
    out = []
    for a in range(len(ins)):
        for i, k in enumerate(flips):
            peer = _flip(me, k)
            out.append(pltpu.make_async_remote_copy(
                src_ref=ins[a].at[_slot(peer)] if scatter else ins[a],
                dst_ref=lands[a].at[_slot(peer) if arriving else _slot(me)],
                send_sem=send_sems.at[a * len(flips) + i], recv_sem=recv_sems.at[a * len(flips) + i],
                device_id=peer, device_id_type=MESH))
    return out


def exchange_start(arrs, scatter, flips, after, name):
    n = len(arrs)
    lands = [lax.empty(a.shape if scatter else (N_DEV,) + a.shape, a.dtype) for a in arrs]

    def body(*refs):
        ins, lz = refs[:n], refs[n:2 * n]
        send_sems, recv_sems = refs[2 * n + 1], refs[2 * n + 2]
        token = refs[-1]
        for cp in _exchange_copies(ins, lz, send_sems, recv_sems, scatter, flips, False):
            cp.start()
        token[...] = jnp.zeros_like(token)

    sems = pltpu.SemaphoreType.DMA((n * len(flips),))
    res = pl.pallas_call(
        body, name=name,
        out_shape=(sems, sems, *[pltpu.HBM(a.shape, a.dtype) for a in arrs], *[pltpu.HBM(a.shape, a.dtype) for a in lands],
                   jax.ShapeDtypeStruct((8, 128), F32)),
        in_specs=[HBM_SPEC] * (2 * n) + [ANY],
        out_specs=(SEM_SPEC, SEM_SPEC, *[HBM_SPEC] * (2 * n), pl.BlockSpec(memory_space=pltpu.VMEM)),
        input_output_aliases={i: 2 + i for i in range(2 * n)},
        compiler_params=pltpu.CompilerParams(has_side_effects=EFFECT),
    )(*[pltpu.with_memory_space_constraint(a, pltpu.HBM) for a in arrs],
      *[pltpu.with_memory_space_constraint(a, pltpu.HBM) for a in lands], after)
    return dict(send=res[0], recv=res[1], srcs=res[2:2 + n], lands=res[2 + n:2 + 2 * n], token=res[-1], scatter=scatter,
                flips=flips)


def exchange_wait(state, after, name):
    n = len(state["srcs"])

    def body(*refs):
        ins, lz = refs[:n], refs[n:2 * n]
        send_sems, recv_sems = refs[2 * n], refs[2 * n + 1]
        for cp in _exchange_copies(ins, lz, send_sems, recv_sems, state["scatter"], state["flips"], True):
            cp.wait_send()
            cp.wait_recv()

    both = list(state["srcs"]) + list(state["lands"])
    res = pl.pallas_call(
        body, name=name, out_shape=tuple(pltpu.HBM(a.shape, a.dtype) for a in both),
        in_specs=[HBM_SPEC] * (2 * n) + [SEM_SPEC, SEM_SPEC, ANY], out_specs=tuple([HBM_SPEC] * (2 * n)),
        input_output_aliases={i: i for i in range(2 * n)},
        compiler_params=pltpu.CompilerParams(has_side_effects=EFFECT),
    )(*both, state["send"], state["recv"], after)
    return res[:n], res[n:]


def place_own(mine, lands, name):
    n = len(lands)

    def body(*refs):
        srcs, lz = refs[:n], refs[n:2 * n]
        sems = refs[3 * n]
        me = _place()
        own = [pltpu.make_async_copy(srcs[a], lz[a].at[_slot(me)], sems.at[a]) for a in range(n)]
        for cp in own:
            cp.start()
        for cp in own:
            cp.wait()

    return pl.pallas_call(
        body, name=name, in_specs=[ANY] * (2 * n), out_specs=[ANY] * n,
        out_shape=[jax.ShapeDtypeStruct(a.shape, a.dtype) for a in lands],
        input_output_aliases={n + i: i for i in range(n)},
        scratch_shapes=[pltpu.SemaphoreType.DMA((n,))],
    )(*mine, *lands)


def adamw(w, m, v, parts, name):
    r, cdim = w.shape
    n_parts = parts.shape[0]
    rb = 256 if r % 256 == 0 else (SHARD_COLS // 4 if r == SHARD_COLS else r)

    def body(w_ref, m_ref, v_ref, p_ref, g_ref, d_ref, mo_ref, vo_ref):
        g = p_ref[0].astype(F32)
        for i in range(1, n_parts):
            g = g + p_ref[i].astype(F32)
        m_new = ADAM_B1 * m_ref[...] + (1.0 - ADAM_B1) * g
        v_new = ADAM_B2 * v_ref[...] + (1.0 - ADAM_B2) * (g * g)
        m_hat = m_new / (1.0 - ADAM_B1 ** ADAM_STEP)
        v_hat = v_new / (1.0 - ADAM_B2 ** ADAM_STEP)
        g_ref[...] = g
        d_ref[...] = -ADAM_LR * (m_hat / (jnp.sqrt(v_hat) + ADAM_EPS) + ADAM_WD * w_ref[...])
        mo_ref[...] = m_new
        vo_ref[...] = v_new

    blk = pl.BlockSpec((rb, cdim), lambda i: (i, 0))
    return pl.pallas_call(
        body, name=name, grid=(r // rb,),
        in_specs=[blk, blk, blk, pl.BlockSpec((n_parts, rb, cdim), lambda i: (0, i, 0))],
        out_specs=[blk] * 4, out_shape=[jax.ShapeDtypeStruct((r, cdim), F32)] * 4,
        compiler_params=_cparams(("arbitrary",)),
    )(w, m, v, parts)


def adamw_layers(w, m, v, lands, srcs, me, lo, prev, name):
    nl_all, r, cdim = w.shape
    nl = len(lands)
    rb = 256 if r % 256 == 0 else (SHARD_COLS // 4 if r == SHARD_COLS else r)
    nblk = r // rb
    n_prev = 0 if prev is None else 4

    def body(me_ref, w_ref, m_ref, v_ref, *rest):
        land_refs, src_refs = rest[:nl], rest[nl:2 * nl]
        g_ref, d_ref, mo_ref, vo_ref = rest[2 * nl + n_prev:]
        for k in range(nl):
            @pl.when(pl.program_id(0) == k)
            def _(k=k):
                own = src_refs[k][...].astype(F32)
                g = jnp.where(me_ref[0] == 0, own, land_refs[k][0].astype(F32))
                for i in range(1, N_DEV):
                    g = g + jnp.where(me_ref[0] == i, own, land_refs[k][i].astype(F32))
                m_new = ADAM_B1 * m_ref[...] + (1.0 - ADAM_B1) * g
                v_new = ADAM_B2 * v_ref[...] + (1.0 - ADAM_B2) * (g * g)
                m_hat = m_new / (1.0 - ADAM_B1 ** ADAM_STEP)
                v_hat = v_new / (1.0 - ADAM_B2 ** ADAM_STEP)
                g_ref[...] = g
                d_ref[...] = -ADAM_LR * (m_hat / (jnp.sqrt(v_hat) + ADAM_EPS) + ADAM_WD * w_ref[...])
                mo_ref[...] = m_new
                vo_ref[...] = v_new

    blk = pl.BlockSpec((None, rb, cdim), lambda l, i, me_ref: (lo + l, i, 0))

    def rows(l, i, k):
        return jnp.where(l < k, 0, jnp.where(l == k, i, nblk - 1))

    land_specs = [pl.BlockSpec((N_DEV, rb, cdim), lambda l, i, me_ref, k=k: (0, rows(l, i, k), 0)) for k in range(nl)]
    src_specs = [pl.BlockSpec((None, rb, cdim), lambda l, i, me_ref, k=k: (me_ref[0], rows(l, i, k), 0)) for k in range(nl)]
    return pl.pallas_call(
        body, name=name,
        grid_spec=pltpu.PrefetchScalarGridSpec(
            num_scalar_prefetch=1, grid=(nl, nblk),
            in_specs=[blk, blk, blk] + land_specs + src_specs + [ANY] * n_prev, out_specs=[blk] * 4),
        out_shape=[jax.ShapeDtypeStruct((nl_all, r, cdim), F32)] * 4,
        input_output_aliases={4 + 2 * nl + j: j for j in range(n_prev)},
        compiler_params=_cparams(("arbitrary", "arbitrary")),
    )(me, w, m, v, *lands, *srcs, *([] if prev is None else prev))


def small_sum(parts, fold, name):
    rows = parts.shape[1]

    def dot3(xv, sel):
        out = jnp.zeros((xv.shape[0], sel.shape[1]), F32)
        for _ in range(3):
            hi = xv.astype(BF16)
            out = out + _dot(hi, sel)
            xv = xv - hi.astype(F32)
        return out

    def body(p_ref, fold_ref, o_ref):
        tot = p_ref[0]
        for i in range(1, N_DEV):
            tot = tot + p_ref[i]
        o_ref[...] = tot
        for l in range(rows // SM_ROWS):
            blk = tot[l * SM_ROWS:l * SM_ROWS + 8, 0:D]
            folded = dot3(blk, fold_ref[...])
            o_ref[l * SM_ROWS + 9:l * SM_ROWS + 10, 0:128] = folded[SM_QG:SM_QG + 1, :]
            o_ref[l * SM_ROWS + 10:l * SM_ROWS + 11, 0:128] = folded[SM_KG:SM_KG + 1, :]

    return pl.pallas_call(
        body, name=name, out_shape=jax.ShapeDtypeStruct((rows, 2 * D), F32),
        compiler_params=_cparams(None),
    )(parts, fold)


def kernel(x, norm_g, w_in, conv_w, q_norm_g, k_norm_g, sinks, w_conv_out, w_attn_out, gate_b, w_out, loss_target, m_norm_g, m_w_in, m_conv_w, m_q_norm_g, m_k_norm_g, m_sinks, m_w_conv_out, m_w_attn_out, m_gate_b, m_w_out, v_norm_g, v_w_in, v_conv_w, v_q_norm_g, v_k_norm_g, v_sinks, v_w_conv_out, v_w_attn_out, v_gate_b, v_w_out):
    c = _selectors()
    me = 4 * lax.axis_index("x") + 2 * lax.axis_index("y") + lax.axis_index("c")

    w_in_t, m_w_in_t, v_w_in_t = (jnp.swapaxes(a, 1, 2) for a in (w_in, m_w_in, v_w_in))

    def shards(l):
        return [w_in_t[l].astype(BF16), w_conv_out[l].astype(BF16), w_attn_out[l].astype(BF16), w_out[l].astype(BF16)]

    def gather_start(arrs, after, tag):
        return exchange_start(arrs, False, ALL_PEERS, after, f"gather_start_{tag}")

    h = x[0]
    saved, lws = [], []
    gather = gather_start(shards(0) + [conv_w], h, 0)
    for l in range(DEPTH):
        mine, lands = exchange_wait(gather, gather["token"] if l == 0 else h, f"gather_wait_{l}")
        lands = place_own(mine, lands, f"gather_own_{l}")
        if l + 1 < DEPTH:
            gather = gather_start(shards(l + 1), lands[1], l + 1)
        if l == 0:
            conv_full = jnp.transpose(lands[4], (1, 2, 0, 3)).reshape(DEPTH, 3, D)
        lws.append(layer_operands(l, norm_g, conv_full, q_norm_g, k_norm_g, sinks, gate_b,
                                  {l: lands[0].reshape(IN_COLS, D)}, {l: lands[1].reshape(D, D)},
                                  {l: lands[2].reshape(D, D)}, {l: lands[3].reshape(D, D)}))
        u, hb = inproj_fwd(h, lws[l]["ng"] + gather["token"][0:1, 0:1], lws[l]["w_in"], f"inproj_fwd_{l}")
        x_in = h
        h, ya, yb = mixer_fwd(x_in, u, lws[l]["cw"], lws[l]["qg"], lws[l]["kg"], lws[l]["sinks"], lws[l]["gb"],
                              lws[l]["wco"], lws[l]["wao"], lws[l]["wout"], c, f"mixer_fwd_{l}")
        saved.append((x_in, u, hb, ya, yb))
    dh, loss_part = loss_head(h, loss_target[0], "loss_head")

    grads, scatters = [None] * DEPTH, [[] for _ in range(DEPTH)]
    for l in reversed(range(DEPTH)):
        def send_off(g, done, l=l):
            rest = [g[k].reshape(N_DEV, SHARD_ROWS, D) for k in ("wco", "wao", "wout")] if done else []
            first = [g["w_in"].reshape(N_DEV, SHARD_COLS, D)] if done == (l > 0) else []
            if not first + rest:
                return None
            tag = f"{l}" if l > 0 else ("0_rest" if done else "0_in")
            scatters[l].append(exchange_start(first + rest, True, ALL_PEERS, g["small"], f"scatter_start_{tag}"))
            return scatters[l][-1]["token"]
        dh, grads[l] = layer_bwd(dh, saved[l], lws[l], c, l, send_off)

    me1 = me.astype(jnp.int32).reshape(1)
    mine, lands = {}, {}
    for l in (3, 2, 1):
        mine[l], lands[l] = exchange_wait(scatters[l][0], dh, f"scatter_wait_{l}")
    weights = [(w_in_t, m_w_in_t, v_w_in_t, "w_in"), (w_conv_out, m_w_conv_out, v_w_conv_out, "w_conv_out"),
               (w_attn_out, m_w_attn_out, v_w_attn_out, "w_attn_out"), (w_out, m_w_out, v_w_out, "w_out")]
    upd = [adamw_layers(w, m, v, [lands[l][i] for l in (1, 2, 3)], [mine[l][i] for l in (1, 2, 3)], me1, 1, None,
                        f"adamw_{n}_upper") for i, (w, m, v, n) in enumerate(weights)]
    m_in, l_in = exchange_wait(scatters[0][0], upd[0][0], "scatter_wait_0_in")
    m_rest, l_rest = exchange_wait(scatters[0][1], upd[3][0], "scatter_wait_0_rest")
    mine[0], lands[0] = list(m_in) + list(m_rest), list(l_in) + list(l_rest)
    upd = [adamw_layers(w, m, v, [lands[0][i]], [mine[0][i]], me1, 0, upd[i], f"adamw_{n}_0")
           for i, (w, m, v, n) in enumerate(weights)]
    u_in = [jnp.swapaxes(o, 1, 2) for o in upd[0]]
    u_co, u_ao, u_out = upd[1], upd[2], upd[3]

    blocks = []
    for l in range(DEPTH):
        blk = grads[l]["small"]
        blk = blk.at[SM_NORM, 0:D].set(grads[l]["dng"][0])
        if l == 0:
            blk = blk.at[SM_LOSS, 0:128].set(loss_part[0])
        blocks.append(blk)
    gathered = all_gather([jnp.concatenate(blocks, axis=0)], u_out[0], "gather_small")[0]
    tot = small_sum(gathered, c["fold"], "small_sum")
    tot = tot.reshape(DEPTH, SM_ROWS, 2 * D)
    loss = tot[0, SM_LOSS, 0]

    def update_small(w, m, v, g, name):
        return adamw(w, m, v, g[None], name)

    u_ng = update_small(norm_g, m_norm_g, v_norm_g, tot[:, SM_NORM, 0:D], "adamw_norm_g")
    u_qg = update_small(q_norm_g, m_q_norm_g, v_q_norm_g, tot[:, 9, 0:HEAD], "adamw_q_norm_g")
    u_kg = update_small(k_norm_g, m_k_norm_g, v_k_norm_g, tot[:, 10, 0:HEAD], "adamw_k_norm_g")
    u_sk = update_small(sinks, m_sinks, v_sinks, tot[:, SM_SINK, 0:16], "adamw_sinks")
    u_gb = update_small(gate_b, m_gate_b, v_gate_b, tot[:, SM_GATE, :], "adamw_gate_b")
    g_conv = lax.dynamic_slice_in_dim(tot[:, SM_CONV:SM_CONV + 3, 0:D], me * SHARD_ROWS, SHARD_ROWS, axis=2)
    u_cw = [o.reshape(DEPTH, 3, SHARD_ROWS) for o in update_small(
        conv_w.reshape(DEPTH * 3, SHARD_ROWS), m_conv_w.reshape(DEPTH * 3, SHARD_ROWS),
        v_conv_w.reshape(DEPTH * 3, SHARD_ROWS), g_conv.reshape(DEPTH * 3, SHARD_ROWS), "adamw_conv_w")]

    order = [u_ng, u_in, u_cw, u_qg, u_kg, u_sk, u_co, u_ao, u_gb, u_out]
    return (loss, dh[None], *[u[0] for u in order], *[u[1] for u in order], *[u[2] for u in order], *[u[3] for u in order])
```

```python
import functools

import jax
import jax.numpy as jnp
from jax import lax
from jax.experimental import pallas as pl
from jax.experimental.pallas import tpu as pltpu

F32 = jnp.float32
BF16 = jnp.bfloat16

N_DEV = 8
DEPTH = 4
D = 1024
N_KV = 4
GROUP = 4
HEAD = 64
BLK = 128
KVW = N_KV * HEAD
IN_COLS = 8704
SHARD_COLS = IN_COLS // N_DEV
SHARD_ROWS = D // N_DEV
C_VC, C_BC, C_CC, C_ZC, C_Q, C_K, C_V, C_ZA, C_GA, C_GB = 0, 1024, 2048, 3072, 4096, 5120, 5376, 5632, 6656, 7680
EPS = 1e-6
NEG_INF = -1e30
SCALE = HEAD ** -0.5

ADAM_LR = 0.001
ADAM_B1 = 0.9
ADAM_B2 = 0.999
ADAM_EPS = 1e-08
ADAM_WD = 0.01
ADAM_STEP = 10

VMEM_LIMIT = 60 * 1024 * 1024
SM_ROWS = 16
SM_GATE, SM_CONV, SM_QG, SM_KG, SM_SINK, SM_NORM, SM_LOSS = 0, 1, 4, 5, 6, 7, 8


def _cparams(sem):
    return pltpu.CompilerParams(dimension_semantics=sem, vmem_limit_bytes=VMEM_LIMIT)


def _dot(a, b):
    return jnp.dot(a, b, preferred_element_type=F32)


def _dot_nt(a, b):
    return lax.dot_general(a, b, (((1,), (1,)), ((), ())), preferred_element_type=F32)


def _dot_tn(a, b):
    return lax.dot_general(a, b, (((0,), (0,)), ((), ())), preferred_element_type=F32)


def _dot2(x, sel):
    hi = x.astype(BF16)
    lo = (x - hi.astype(F32)).astype(BF16)
    return _dot(hi, sel) + _dot(lo, sel)


def _sigmoid(z):
    return 1.0 / (1.0 + jnp.exp(-z))


def _head_mean(t, sel, exp):
    return _dot2(_dot2(t, sel) * (1.0 / HEAD), exp)


def _bf(ref, c0, width):
    return ref[:, c0:c0 + width].astype(F32)


def _selectors():
    c = jnp.arange(D)
    sel_q = (c[:, None] // HEAD == jnp.arange(128)[None, :]).astype(BF16)
    ck = jnp.arange(KVW)
    sel_k = (ck[:, None] // HEAD == jnp.arange(128)[None, :]).astype(BF16)
    src = jnp.arange(KVW)[:, None]
    dst = jnp.arange(KVW)[None, :]
    rep = jnp.stack([((src // HEAD == h) & (src % HEAD == dst % HEAD)).astype(BF16) for h in range(N_KV)])
    fold = (c[:, None] % HEAD == jnp.arange(128)[None, :]).astype(BF16)
    return dict(sel_q=sel_q, exp_q=sel_q.T, sel_k=sel_k, exp_k=sel_k.T, rep=rep, rep_t=jnp.swapaxes(rep, 1, 2), fold=fold)


def inproj_fwd(x, ng, w, name):
    t = x.shape[0]
    tm = min(1024, t)
    cb = 2176
    def body(x_ref, ng_ref, w_ref, u_ref, h_ref, h_scr):
        @pl.when(pl.program_id(1) == 0)
        def _():
            xf = x_ref[...]
            r = lax.rsqrt(jnp.mean(xf * xf, axis=-1, keepdims=True) + EPS)
            hb = (xf * r * ng_ref[...]).astype(BF16)
            h_scr[...] = hb
            h_ref[...] = hb
        u_ref[...] = _dot_nt(h_scr[...], w_ref[...]).astype(BF16)

    return pl.pallas_call(
        body, name=name, grid=(t // tm, IN_COLS // cb),
        in_specs=[pl.BlockSpec((tm, D), lambda i, j: (i, 0)), pl.BlockSpec((1, D), lambda i, j: (0, 0)),
                  pl.BlockSpec((cb, D), lambda i, j: (j, 0))],
        out_specs=[pl.BlockSpec((tm, cb), lambda i, j: (i, j)), pl.BlockSpec((tm, D), lambda i, j: (i, 0))],
        out_shape=[jax.ShapeDtypeStruct((t, IN_COLS), BF16), jax.ShapeDtypeStruct((t, D), BF16)],
        scratch_shapes=[pltpu.VMEM((tm, D), BF16)],
        compiler_params=_cparams(("arbitrary", "arbitrary")),
    )(x, ng, w)


def _conv_fwd(u_ref, uvc_prev, ucc_prev, cw_ref, is_first, tm):
    p = _bf(u_ref, C_CC, D) * _bf(u_ref, C_VC, D)
    pprev = ucc_prev[...].astype(F32) * uvc_prev[...].astype(F32)
    pprev = jnp.where(is_first, 0.0, pprev)
    row = lax.broadcasted_iota(jnp.int32, (tm, 1), 0)
    p1 = jnp.where(row == 0, pprev[15:16, :], pltpu.roll(p, 1, 0))
    p2 = jnp.where(row == 0, pprev[14:15, :], jnp.where(row == 1, pprev[15:16, :], pltpu.roll(p, 2, 0)))
    cw = cw_ref[...]
    conv = cw[0:1, :] * p2 + cw[1:2, :] * p1 + cw[2:3, :] * p
    return p, p1, p2, conv


def _attn_inputs(u_ref, ukv_prev, qg_ref, kg_ref, c, tm):
    q = _bf(u_ref, C_Q, D)
    rq = lax.rsqrt(_head_mean(q * q, c["sel_q"][...], c["exp_q"][...]) + EPS)
    qhat = q * rq
    qn = (qhat * qg_ref[...]).astype(BF16)
    kband = jnp.concatenate([ukv_prev[:, 0:KVW].astype(F32), _bf(u_ref, C_K, KVW)], axis=0)
    rk = lax.rsqrt(_head_mean(kband * kband, c["sel_k"][...], c["exp_k"][...]) + EPS)
    khat = kband * rk
    knb = (khat * kg_ref[...]).astype(BF16)
    vband = jnp.concatenate([ukv_prev[:, KVW:2 * KVW], u_ref[:, C_V:C_V + KVW]], axis=0)
    kt = [_dot(knb, c["rep"][h]).astype(BF16) for h in range(N_KV)]
    vt = [_dot(vband, c["rep"][h]).astype(BF16) for h in range(N_KV)]
    return qhat, rq, qn, khat, rk, kt, vt


def _attn_masks(is_first):
    rows = GROUP * BLK
    r = lax.broadcasted_iota(jnp.int32, (rows, 2 * BLK), 0)
    kk = lax.broadcasted_iota(jnp.int32, (rows, 2 * BLK), 1)
    qq = r % BLK
    valid = (kk > qq) & (kk <= qq + BLK)
    valid_first = valid & ((kk >= BLK) | jnp.logical_not(is_first))
    lane_grp = lax.broadcasted_iota(jnp.int32, (BLK, KVW), 1) // HEAD
    row_grp = lax.broadcasted_iota(jnp.int32, (rows, 1), 0) // BLK
    return valid, valid_first, lane_grp, row_grp


def _sink_col(sinks_ref, h, row_grp):
    col = jnp.full(row_grp.shape, sinks_ref[0, GROUP * h], F32)
    for gi in range(1, GROUP):
        col = jnp.where(row_grp == gi, sinks_ref[0, GROUP * h + gi], col)
    return col


def _stack_groups(a256, lane_grp):
    zero = jnp.zeros_like(a256)
    return jnp.concatenate([jnp.where(lane_grp == gi, a256, zero) for gi in range(GROUP)], axis=0)


def _unstack_groups(a4, lane_grp):
    out = jnp.where(lane_grp == 0, a4[0:BLK], 0.0)
    for gi in range(1, GROUP):
        out = out + jnp.where(lane_grp == gi, a4[gi * BLK:(gi + 1) * BLK], 0.0)
    return out


def _softmax_block(qs, kt_b, valid, sink):
    s = _dot_nt(qs, kt_b)
    s = jnp.where(valid, s, NEG_INF)
    m = jnp.maximum(jnp.max(s, axis=-1, keepdims=True), sink)
    e = jnp.exp(s - m)
    es = jnp.exp(sink - m)
    inv = 1.0 / (jnp.sum(e, axis=-1, keepdims=True) + es)
    return e * inv, es * inv


def _mixer_specs(t, tm, n_tiles, tile_of):
    nb = tm // BLK
    u_spec = pl.BlockSpec((tm, IN_COLS), lambda g: (tile_of(g), 0))
    ukv_prev = pl.BlockSpec((BLK, 2 * KVW), lambda g: (jnp.maximum(tile_of(g) * nb - 1, 0), C_K // (2 * KVW)))
    uvc_prev = pl.BlockSpec((16, D), lambda g: (jnp.maximum(tile_of(g) * (tm // 16) - 1, 0), C_VC // D))
    ucc_prev = pl.BlockSpec((16, D), lambda g: (jnp.maximum(tile_of(g) * (tm // 16) - 1, 0), C_CC // D))
    return u_spec, ukv_prev, uvc_prev, ucc_prev


def _full(shape):
    n = len(shape)
    return pl.BlockSpec(shape, lambda g: (0,) * n)


def mixer_fwd(x, u, cw, qg, kg, sinks, gb, wco, wao, wout, c, name):
    t = x.shape[0]
    tm = min(256, t)
    n_tiles = t // tm
    nb = tm // BLK
    cn = sorted(c)

    def body(x_ref, u_ref, ukv_prev, uvc_prev, ucc_prev, cw_ref, qg_ref, kg_ref, sinks_ref, gb_ref, wco_ref, wao_ref,
             wout_ref, *rest):
        cref = dict(zip(cn, rest[:len(cn)]))
        xo_ref, ya_ref, yb_ref, o_scr = rest[len(cn):]
        is_first = pl.program_id(0) == 0
        _, _, _, conv = _conv_fwd(u_ref, uvc_prev, ucc_prev, cw_ref, is_first, tm)
        zc = _bf(u_ref, C_ZC, D)
        yc = _bf(u_ref, C_BC, D) * conv * (zc * _sigmoid(zc))
        ya = _dot(yc.astype(BF16), wco_ref[...])

        _, _, qn, _, _, kt, vt = _attn_inputs(u_ref, ukv_prev, qg_ref, kg_ref, cref, tm)
        valid, valid_first, lane_grp, row_grp = _attn_masks(is_first)
        for h in range(N_KV):
            sink = _sink_col(sinks_ref, h, row_grp)
            for b in range(nb):
                qs = _stack_groups(qn[b * BLK:(b + 1) * BLK, h * KVW:(h + 1) * KVW], lane_grp)
                pn, _ = _softmax_block(qs, kt[h][b * BLK:(b + 2) * BLK], valid_first if b == 0 else valid, sink)
                o4 = _dot(pn.astype(BF16), vt[h][b * BLK:(b + 2) * BLK])
                o_scr[b * BLK:(b + 1) * BLK, h * KVW:(h + 1) * KVW] = _unstack_groups(o4, lane_grp)
        za = _bf(u_ref, C_ZA, D)
        ob = o_scr[...] * (za * _sigmoid(za))
        yb = _dot(ob.astype(BF16), wao_ref[...])

        g_a = _sigmoid(_bf(u_ref, C_GA, D) + gb_ref[:, 0:D])
        g_b = _sigmoid(_bf(u_ref, C_GB, D) + gb_ref[:, D:2 * D])
        merged = g_a * ya + g_b * yb
        xo_ref[...] = x_ref[...] + _dot(merged.astype(BF16), wout_ref[...])
        ya_ref[...] = ya.astype(BF16)
        yb_ref[...] = yb.astype(BF16)

    u_spec, ukv_prev, uvc_prev, ucc_prev = _mixer_specs(t, tm, n_tiles, lambda g: g)
    tok = pl.BlockSpec((tm, D), lambda g: (g, 0))
    consts = [c[k] for k in cn]
    return pl.pallas_call(
        body, name=name, grid=(n_tiles,),
        in_specs=[tok, u_spec, ukv_prev, uvc_prev, ucc_prev, _full((8, D)), _full((1, D)), _full((1, KVW)),
                  pl.BlockSpec(memory_space=pltpu.SMEM), _full((1, 2 * D)), _full((D, D)), _full((D, D)), _full((D, D))]
                 + [_full(a.shape) for a in consts],
        out_specs=[tok, tok, tok],
        out_shape=[jax.ShapeDtypeStruct((t, D), F32), jax.ShapeDtypeStruct((t, D), BF16), jax.ShapeDtypeStruct((t, D), BF16)],
        scratch_shapes=[pltpu.VMEM((tm, D), F32)],
        compiler_params=_cparams(("arbitrary",)),
    )(x, u, u, u, u, cw, qg, kg, sinks, gb, wco, wao, wout, *consts)


def mixer_bwd(dout, u, ya, yb, cw, qg, kg, sinks, gb, wco, wao, wout, c, name):
    t = dout.shape[0]
    tm = min(256, t)
    n_tiles = t // tm
    nb = tm // BLK
    kb = tm + BLK
    cn = sorted(c)

    def body(dout_ref, u_ref, ukv_prev, uvc_prev, ucc_prev, ya_ref, yb_ref, cw_ref, qg_ref, kg_ref, sinks_ref, gb_ref,
             wco_ref, wao_ref, wout_ref, *rest):
        cref = dict(zip(cn, rest[:len(cn)]))
        (du_ref, small_ref, merged_ref, yc_ref, ob_ref, dya_ref, dyb_ref,
         o_scr, dq_scr, dk4_scr, dv4_scr, carry_kv, carry_conv) = rest[len(cn):]
        g = pl.program_id(0)
        is_first = g == n_tiles - 1

        @pl.when(g == 0)
        def _():
            carry_kv[...] = jnp.zeros_like(carry_kv)
            carry_conv[...] = jnp.zeros_like(carry_conv)
            small_ref[...] = jnp.zeros_like(small_ref)

        dout = dout_ref[...]
        dout_b = dout.astype(BF16)
        ya_v = ya_ref[...].astype(F32)
        yb_v = yb_ref[...].astype(F32)
        g_a = _sigmoid(_bf(u_ref, C_GA, D) + gb_ref[:, 0:D])
        g_b = _sigmoid(_bf(u_ref, C_GB, D) + gb_ref[:, D:2 * D])
        merged = g_a * ya_v + g_b * yb_v
        dmerged = _dot_nt(dout_b, wout_ref[...])
        merged_ref[...] = merged.astype(BF16)
        dya = dmerged * g_a
        dyb = dmerged * g_b
        dgl_a = dmerged * ya_v * g_a * (1.0 - g_a)
        dgl_b = dmerged * yb_v * g_b * (1.0 - g_b)
        du_ref[:, C_GA:C_GA + D] = dgl_a.astype(BF16)
        du_ref[:, C_GB:C_GB + D] = dgl_b.astype(BF16)
        small_ref[SM_GATE:SM_GATE + 1, 0:D] += jnp.sum(dgl_a, axis=0, keepdims=True)
        small_ref[SM_GATE:SM_GATE + 1, D:2 * D] += jnp.sum(dgl_b, axis=0, keepdims=True)

        p, p1, p2, conv = _conv_fwd(u_ref, uvc_prev, ucc_prev, cw_ref, is_first, tm)
        zc = _bf(u_ref, C_ZC, D)
        bc = _bf(u_ref, C_BC, D)
        sg = _sigmoid(zc)
        sc = zc * sg
        yc = bc * conv * sc
        dya_b = dya.astype(BF16)
        yc_ref[...] = yc.astype(BF16)
        dya_ref[...] = dya_b
        dyc = _dot_nt(dya_b, wco_ref[...])
        du_ref[:, C_BC:C_BC + D] = (dyc * conv * sc).astype(BF16)
        du_ref[:, C_ZC:C_ZC + D] = (dyc * bc * conv * (sg * (1.0 + zc * (1.0 - sg)))).astype(BF16)
        dconv = dyc * bc * sc
        small_ref[SM_CONV + 2:SM_CONV + 3, 0:D] += jnp.sum(dconv * p, axis=0, keepdims=True)
        small_ref[SM_CONV + 1:SM_CONV + 2, 0:D] += jnp.sum(dconv * p1, axis=0, keepdims=True)
        small_ref[SM_CONV:SM_CONV + 1, 0:D] += jnp.sum(dconv * p2, axis=0, keepdims=True)
        row = lax.broadcasted_iota(jnp.int32, (tm, 1), 0)
        nxt = carry_conv[...]
        d1 = jnp.where(row == tm - 1, nxt[0:1, :], pltpu.roll(dconv, tm - 1, 0))
        d2 = jnp.where(row == tm - 1, nxt[1:2, :], jnp.where(row == tm - 2, nxt[0:1, :], pltpu.roll(dconv, tm - 2, 0)))
        carry_conv[...] = dconv[0:8, :]
        cw = cw_ref[...]
        dp = cw[2:3, :] * dconv + cw[1:2, :] * d1 + cw[0:1, :] * d2
        du_ref[:, C_CC:C_CC + D] = (dp * _bf(u_ref, C_VC, D)).astype(BF16)
        du_ref[:, C_VC:C_VC + D] = (dp * _bf(u_ref, C_CC, D)).astype(BF16)

        dyb_b = dyb.astype(BF16)
        dob = _dot_nt(dyb_b, wao_ref[...])
        za = _bf(u_ref, C_ZA, D)
        sga = _sigmoid(za)
        sa = za * sga
        do = dob * sa
        qhat, rq, qn, khat, rk, kt, vt = _attn_inputs(u_ref, ukv_prev, qg_ref, kg_ref, cref, tm)
        valid, valid_first, lane_grp, row_grp = _attn_masks(is_first)
        dk4_scr[...] = jnp.zeros_like(dk4_scr)
        dv4_scr[...] = jnp.zeros_like(dv4_scr)
        lane16 = lax.broadcasted_iota(jnp.int32, (1, 2 * D), 1)
        dsink_row = jnp.zeros((1, 2 * D), F32)
        for h in range(N_KV):
            sink = _sink_col(sinks_ref, h, row_grp)
            dsink_col = jnp.zeros((GROUP * BLK, 1), F32)
            for b in range(nb):
                rows = slice(b * BLK, (b + 1) * BLK)
                band = slice(b * BLK, (b + 2) * BLK)
                cols = slice(h * KVW, (h + 1) * KVW)
                qs = _stack_groups(qn[rows, cols], lane_grp)
                pn, ps = _softmax_block(qs, kt[h][band], valid_first if b == 0 else valid, sink)
                pn_b = pn.astype(BF16)
                o4 = _dot(pn_b, vt[h][band])
                o_scr[rows, cols] = _unstack_groups(o4, lane_grp)
                dos = _stack_groups(do[rows, cols], lane_grp).astype(BF16)
                dpn = _dot_nt(dos, vt[h][band])
                delta = jnp.sum(pn * dpn, axis=-1, keepdims=True)
                ds = (pn * (dpn - delta)).astype(BF16)
                dsink_col = dsink_col - ps * delta
                dq_scr[rows, cols] = _unstack_groups(_dot(ds, kt[h][band]), lane_grp)
                dk4_scr[h, band, :] += _dot_tn(ds, qs)
                dv4_scr[h, band, :] += _dot_tn(pn_b, dos)
            for gi in range(GROUP):
                tot = jnp.sum(dsink_col[gi * BLK:(gi + 1) * BLK, :], axis=0, keepdims=True)
                dsink_row = dsink_row + jnp.where(lane16 == GROUP * h + gi, tot, 0.0)
        small_ref[SM_SINK:SM_SINK + 1, :] += dsink_row

        o = o_scr[...]
        ob_ref[...] = (o * sa).astype(BF16)
        dyb_ref[...] = dyb_b
        du_ref[:, C_ZA:C_ZA + D] = (dob * o * (sga * (1.0 + za * (1.0 - sga)))).astype(BF16)

        dqn = dq_scr[...]
        small_ref[SM_QG:SM_QG + 1, 0:D] += SCALE * jnp.sum(dqn * qhat, axis=0, keepdims=True)
        dqh = dqn * qg_ref[...]
        dq = rq * (dqh - qhat * _head_mean(dqh * qhat, cref["sel_q"][...], cref["exp_q"][...]))
        du_ref[:, C_Q:C_Q + D] = dq.astype(BF16)

        dkn_band = jnp.zeros((kb, KVW), F32)
        dv_band = jnp.zeros((kb, KVW), F32)
        for h in range(N_KV):
            dkn_band = dkn_band + _dot2(dk4_scr[h], cref["rep_t"][h])
            dv_band = dv_band + _dot2(dv4_scr[h], cref["rep_t"][h])
        carried = carry_kv[...]
        pad = jnp.zeros((tm - BLK, KVW), F32)
        if nb > 1:
            dkn = dkn_band[BLK:, :] + jnp.concatenate([pad, carried[:, 0:KVW]], axis=0)
            dv = dv_band[BLK:, :] + jnp.concatenate([pad, carried[:, KVW:2 * KVW]], axis=0)
        else:
            dkn = dkn_band[BLK:, :] + carried[:, 0:KVW]
            dv = dv_band[BLK:, :] + carried[:, KVW:2 * KVW]
        carry_kv[:, 0:KVW] = dkn_band[0:BLK, :]
        carry_kv[:, KVW:2 * KVW] = dv_band[0:BLK, :]
        khat_t = khat[BLK:, :]
        small_ref[SM_KG:SM_KG + 1, 0:KVW] += jnp.sum(dkn * khat_t, axis=0, keepdims=True)
        dkh = dkn * kg_ref[...]
        dk = rk[BLK:, :] * (dkh - khat_t * _head_mean(dkh * khat_t, cref["sel_k"][...], cref["exp_k"][...]))
        du_ref[:, C_K:C_K + KVW] = dk.astype(BF16)
        du_ref[:, C_V:C_V + KVW] = dv.astype(BF16)

    rev = lambda g: n_tiles - 1 - g
    u_spec, ukv_prev, uvc_prev, ucc_prev = _mixer_specs(t, tm, n_tiles, rev)
    tok = pl.BlockSpec((tm, D), lambda g: (rev(g), 0))
    consts = [c[k] for k in cn]
    wspec = _full((D, D))
    return pl.pallas_call(
        body, name=name, grid=(n_tiles,),
        in_specs=[tok, u_spec, ukv_prev, uvc_prev, ucc_prev, tok, tok, _full((8, D)), _full((1, D)), _full((1, KVW)),
                  pl.BlockSpec(memory_space=pltpu.SMEM), _full((1, 2 * D)), wspec, wspec, wspec]
                 + [_full(a.shape) for a in consts],
        out_specs=[pl.BlockSpec((tm, IN_COLS), lambda g: (rev(g), 0)), _full((SM_ROWS, 2 * D))] + [tok] * 5,
        out_shape=[jax.ShapeDtypeStruct((t, IN_COLS), BF16), jax.ShapeDtypeStruct((SM_ROWS, 2 * D), F32)]
                  + [jax.ShapeDtypeStruct((t, D), BF16)] * 5,
        scratch_shapes=[pltpu.VMEM((tm, D), F32), pltpu.VMEM((tm, D), F32),
                        pltpu.VMEM((N_KV, kb, KVW), F32), pltpu.VMEM((N_KV, kb, KVW), F32),
                        pltpu.VMEM((BLK, 2 * KVW), F32), pltpu.VMEM((8, D), F32)],
        compiler_params=_cparams(("arbitrary",)),
    )(dout, u, u, u, u, ya, yb, cw, qg, kg, sinks, gb, wco, wao, wout, *consts)


def matmul_tn(a, b, name, after=None):
    t, m = a.shape
    tk = min(1024, t)
    mb = 2176 if m == IN_COLS else m
    nk = t // tk

    def body(a_ref, b_ref, *rest):
        o_ref, acc = rest[-2:]
        k = pl.program_id(1)
        prod = _dot_tn(a_ref[...].astype(BF16), b_ref[...].astype(BF16))

        @pl.when(k == 0)
        def _():
            acc[...] = prod

        @pl.when(k > 0)
        def _():
            acc[...] += prod

        @pl.when(k == nk - 1)
        def _():
            o_ref[...] = acc[...].astype(BF16)

    return pl.pallas_call(
        body, name=name, grid=(m // mb, nk),
        in_specs=[pl.BlockSpec((tk, mb), lambda j, k: (k, j)), pl.BlockSpec((tk, D), lambda j, k: (k, 0))]
                 + ([] if after is None else [ANY]),
        out_specs=pl.BlockSpec((mb, D), lambda j, k: (j, 0)),
        out_shape=jax.ShapeDtypeStruct((m, D), BF16),
        scratch_shapes=[pltpu.VMEM((mb, D), F32)],
        compiler_params=_cparams(("arbitrary", "arbitrary")),
    )(a, b, *([] if after is None else [after]))


def inproj_bwd_x(du, w, x, ng, dout, name):
    t = x.shape[0]
    tm = min(1024, t)
    kc = 2176
    nk = IN_COLS // kc

    def body(du_ref, w_ref, x_ref, ng_ref, dout_ref, dx_ref, dng_ref, acc):
        i = pl.program_id(0)
        k = pl.program_id(1)
        prod = _dot(du_ref[...], w_ref[...])

        @pl.when(k == 0)
        def _():
            acc[...] = prod

        @pl.when(k > 0)
        def _():
            acc[...] += prod

        @pl.when((i == 0) & (k == 0))
        def _():
            dng_ref[...] = jnp.zeros_like(dng_ref)

        @pl.when(k == nk - 1)
        def _():
            dh = acc[...]
            xf = x_ref[...]
            r = lax.rsqrt(jnp.mean(xf * xf, axis=-1, keepdims=True) + EPS)
            xhat = xf * r
            dng_ref[0:1, :] += jnp.sum(dh * xhat, axis=0, keepdims=True)
            dxh = dh * ng_ref[...]
            dx_ref[...] = dout_ref[...] + r * (dxh - xhat * jnp.mean(dxh * xhat, axis=-1, keepdims=True))

    tok = pl.BlockSpec((tm, D), lambda i, k: (i, 0))
    return pl.pallas_call(
        body, name=name, grid=(t // tm, nk),
        in_specs=[pl.BlockSpec((tm, kc), lambda i, k: (i, k)), pl.BlockSpec((kc, D), lambda i, k: (k, 0)), tok,
                  pl.BlockSpec((1, D), lambda i, k: (0, 0)), tok],
        out_specs=[tok, pl.BlockSpec((8, D), lambda i, k: (0, 0))],
        out_shape=[jax.ShapeDtypeStruct((t, D), F32), jax.ShapeDtypeStruct((8, D), F32)],
        scratch_shapes=[pltpu.VMEM((tm, D), F32)],
        compiler_params=_cparams(("arbitrary", "arbitrary")),
    )(du, w, x, ng, dout)


def loss_head(y, target, name):
    t = y.shape[0]
    tm = min(1024, t)

    def body(y_ref, t_ref, dy_ref, loss_ref):
        @pl.when(pl.program_id(0) == 0)
        def _():
            loss_ref[...] = jnp.zeros_like(loss_ref)
        err = y_ref[...] - t_ref[...]
        dy_ref[...] = err * (1.0 / D)
        part = jnp.sum(jnp.sum(err * err, axis=-1, keepdims=True) * (1.0 / D), axis=0, keepdims=True)
        loss_ref[...] += 0.5 * part

    tok = pl.BlockSpec((tm, D), lambda i: (i, 0))
    return pl.pallas_call(
        body, name=name, grid=(t // tm,), in_specs=[tok, tok],
        out_specs=[tok, pl.BlockSpec((8, 128), lambda i: (0, 0))],
        out_shape=[jax.ShapeDtypeStruct((t, D), F32), jax.ShapeDtypeStruct((8, 128), F32)],
        compiler_params=_cparams(("arbitrary",)),
    )(y, target)


def layer_operands(l, norm_g, conv_w_full, q_norm_g, k_norm_g, sinks, gate_b, w_in_b, wco_b, wao_b, wout_b):
    return dict(
        ng=norm_g[l][None, :], cw=jnp.pad(conv_w_full[l], ((0, 5), (0, 0))),
        qg=jnp.tile(q_norm_g[l] * SCALE, D // HEAD)[None, :], kg=jnp.tile(k_norm_g[l], N_KV)[None, :],
        sinks=sinks[l][None, :], gb=gate_b[l][None, :],
        w_in=w_in_b[l], wco=wco_b[l], wao=wao_b[l], wout=wout_b[l])


def layer_bwd(dout, saved, lw, c, l, send_off):
    x, u, h, ya, yb = saved
    du, small, merged, yc, ob, dya, dyb = mixer_bwd(dout, u, ya, yb, lw["cw"], lw["qg"], lw["kg"], lw["sinks"], lw["gb"],
                                                    lw["wco"], lw["wao"], lw["wout"], c, f"mixer_bwd_{l}")
    grads = dict(w_in=matmul_tn(du, h, f"dw_in_{l}"), small=small)
    token = send_off(grads, False)
    grads["wout"] = matmul_tn(merged, dout, f"dw_out_{l}", after=token)
    grads["wco"] = matmul_tn(yc, dya, f"dw_conv_out_{l}")
    grads["wao"] = matmul_tn(ob, dyb, f"dw_attn_out_{l}")
    token = send_off(grads, True)
    dx, grads["dng"] = inproj_bwd_x(du, lw["w_in"], x, lw["ng"] + token[0:1, 0:1], dout, f"inproj_bwd_{l}")
    return dx, grads


MESH = pl.DeviceIdType.MESH
ANY = pl.BlockSpec(memory_space=pl.ANY)


def _place():
    return lax.axis_index("x"), lax.axis_index("y"), lax.axis_index("c")


def all_gather(arrs, after, name):
    n = len(arrs)

    def body(*refs):
        ins, outs = refs[:n], refs[n + 1:2 * n + 1]
        send_sems, recv_sems, local_sems = refs[2 * n + 1:]
        x, y, c = _place()
        me, sibling = (x, y, c), (x, y, 1 - c)
        chips = [(1 - x, y), (x, 1 - y), (1 - x, 1 - y)]

        def slot(a, block):
            px, py, pc = block
            return outs[a].at[4 * px + 2 * py + pc]

        def copy(a, k, block, to, src=None):
            return pltpu.make_async_remote_copy(
                src_ref=slot(a, block) if src is None else src, dst_ref=slot(a, block),
                send_sem=send_sems.at[a, k], recv_sem=recv_sems.at[a, k], device_id=to, device_id_type=MESH)

        mine = [pltpu.make_async_copy(ins[a], slot(a, me), local_sems.at[a]) for a in range(n)]
        for cp in mine:
            cp.start()
        first = []
        for a in range(n):
            first.append(copy(a, 0, me, sibling, src=ins[a]))
            first += [copy(a, 1 + j, me, (*chip, c), src=ins[a]) for j, chip in enumerate(chips)]
        for cp in first:
            cp.start()
        passed = []
        for j, chip in enumerate(chips):
            for a in range(n):
                copy(a, 1 + j, (*chip, c), me).wait_recv()
                passed.append(copy(a, 4 + j, (*chip, c), sibling))
                passed[-1].start()
        for a in range(n):
            copy(a, 0, sibling, me).wait_recv()
            for j, chip in enumerate(chips):
                copy(a, 4 + j, (*chip, 1 - c), me).wait_recv()
        for cp in first + passed:
            cp.wait_send()
        for cp in mine:
            cp.wait()

    return pl.pallas_call(
        body, name=name, in_specs=[ANY] * (n + 1), out_specs=[ANY] * n,
        out_shape=[jax.ShapeDtypeStruct((N_DEV,) + a.shape, a.dtype) for a in arrs],
        scratch_shapes=[pltpu.SemaphoreType.DMA((n, 7)), pltpu.SemaphoreType.DMA((n, 7)), pltpu.SemaphoreType.DMA((n,))],
    )(*arrs, after)


HBM_SPEC = pl.BlockSpec(memory_space=pltpu.HBM)
SEM_SPEC = pl.BlockSpec(memory_space=pltpu.SEMAPHORE)
EFFECT = pltpu.SideEffectType.DATAFLOW_SIDE_EFFECTING


ALL_PEERS = (1, 2, 3, 4, 5, 6, 7)


def _flip(place, k):
    x, y, c = place
    return (1 - x if k & 4 else x, 1 - y if k & 2 else y, 1 - c if k & 1 else c)


def _slot(place):
    return 4 * place[0] + 2 * place[1] + place[2]


def _exchange_copies(ins, lands, send_sems, recv_sems, scatter, flips, arriving):
    me = _place()
    out = []
    for a in range(len(ins)):
        for i, k in enumerate(flips):
            peer = _flip(me, k)
            out.append(pltpu.make_async_remote_copy(
                src_ref=ins[a].at[_slot(peer)] if scatter else ins[a],
                dst_ref=lands[a].at[_slot(peer) if arriving else _slot(me)],
                send_sem=send_sems.at[a * len(flips) + i], recv_sem=recv_sems.at[a * len(flips) + i],
                device_id=peer, device_id_type=MESH))
    return out


def exchange_start(arrs, scatter, flips, after, name):
    n = len(arrs)
    lands = [lax.empty(a.shape if scatter else (N_DEV,) + a.shape, a.dtype) for a in arrs]

    def body(*refs):
        ins, lz = refs[:n], refs[n:2 * n]
        send_sems, recv_sems = refs[2 * n + 1], refs[2 * n + 2]
        token = refs[-1]
        for cp in _exchange_copies(ins, lz, send_sems, recv_sems, scatter, flips, False):
            cp.start()
        token[...] = jnp.zeros_like(token)

    sems = pltpu.SemaphoreType.DMA((n * len(flips),))
    res = pl.pallas_call(
        body, name=name,
        out_shape=(sems, sems, *[pltpu.HBM(a.shape, a.dtype) for a in arrs], *[pltpu.HBM(a.shape, a.dtype) for a in lands],
                   jax.ShapeDtypeStruct((8, 128), F32)),
        in_specs=[HBM_SPEC] * (2 * n) + [ANY],
        out_specs=(SEM_SPEC, SEM_SPEC, *[HBM_SPEC] * (2 * n), pl.BlockSpec(memory_space=pltpu.VMEM)),
        input_output_aliases={i: 2 + i for i in range(2 * n)},
        compiler_params=pltpu.CompilerParams(has_side_effects=EFFECT),
    )(*[pltpu.with_memory_space_constraint(a, pltpu.HBM) for a in arrs],
      *[pltpu.with_memory_space_constraint(a, pltpu.HBM) for a in lands], after)
    return dict(send=res[0], recv=res[1], srcs=res[2:2 + n], lands=res[2 + n:2 + 2 * n], token=res[-1], scatter=scatter,
                flips=flips)


def exchange_wait(state, after, name):
    n = len(state["srcs"])

    def body(*refs):
        ins, lz = refs[:n], refs[n:2 * n]
        send_sems, recv_sems = refs[2 * n], refs[2 * n + 1]
        for cp in _exchange_copies(ins, lz, send_sems, recv_sems, state["scatter"], state["flips"], True):
            cp.wait_send()
            cp.wait_recv()

    both = list(state["srcs"]) + list(state["lands"])
    res = pl.pallas_call(
        body, name=name, out_shape=tuple(pltpu.HBM(a.shape, a.dtype) for a in both),
        in_specs=[HBM_SPEC] * (2 * n) + [SEM_SPEC, SEM_SPEC, ANY], out_specs=tuple([HBM_SPEC] * (2 * n)),
        input_output_aliases={i: i for i in range(2 * n)},
        compiler_params=pltpu.CompilerParams(has_side_effects=EFFECT),
    )(*both, state["send"], state["recv"], after)
    return res[:n], res[n:]


def adamw(w, m, v, parts, name):
    r, cdim = w.shape
    n_parts = parts.shape[0]
    rb = 256 if r % 256 == 0 else (SHARD_COLS // 4 if r == SHARD_COLS else r)

    def body(w_ref, m_ref, v_ref, p_ref, g_ref, d_ref, mo_ref, vo_ref):
        g = p_ref[0].astype(F32)
        for i in range(1, n_parts):
            g = g + p_ref[i].astype(F32)
        m_new = ADAM_B1 * m_ref[...] + (1.0 - ADAM_B1) * g
        v_new = ADAM_B2 * v_ref[...] + (1.0 - ADAM_B2) * (g * g)
        m_hat = m_new / (1.0 - ADAM_B1 ** ADAM_STEP)
        v_hat = v_new / (1.0 - ADAM_B2 ** ADAM_STEP)
        g_ref[...] = g
        d_ref[...] = -ADAM_LR * (m_hat / (jnp.sqrt(v_hat) + ADAM_EPS) + ADAM_WD * w_ref[...])
        mo_ref[...] = m_new
        vo_ref[...] = v_new

    blk = pl.BlockSpec((rb, cdim), lambda i: (i, 0))
    return pl.pallas_call(
        body, name=name, grid=(r // rb,),
        in_specs=[blk, blk, blk, pl.BlockSpec((n_parts, rb, cdim), lambda i: (0, i, 0))],
        out_specs=[blk] * 4, out_shape=[jax.ShapeDtypeStruct((r, cdim), F32)] * 4,
        compiler_params=_cparams(("arbitrary",)),
    )(w, m, v, parts)


def adamw_layers(w, m, v, lands, srcs, me, lo, prev, name):
    nl_all, r, cdim = w.shape
    nl = len(lands)
    rb = 256 if r % 256 == 0 else (SHARD_COLS // 4 if r == SHARD_COLS else r)
    nblk = r // rb
    n_prev = 0 if prev is None else 4

    def body(me_ref, w_ref, m_ref, v_ref, *rest):
        land_refs, src_refs = rest[:nl], rest[nl:2 * nl]
        g_ref, d_ref, mo_ref, vo_ref = rest[2 * nl + n_prev:]
        for k in range(nl):
            @pl.when(pl.program_id(0) == k)
            def _(k=k):
                own = src_refs[k][...].astype(F32)
                g = jnp.where(me_ref[0] == 0, own, land_refs[k][0].astype(F32))
                for i in range(1, N_DEV):
                    g = g + jnp.where(me_ref[0] == i, own, land_refs[k][i].astype(F32))
                m_new = ADAM_B1 * m_ref[...] + (1.0 - ADAM_B1) * g
                v_new = ADAM_B2 * v_ref[...] + (1.0 - ADAM_B2) * (g * g)
                m_hat = m_new / (1.0 - ADAM_B1 ** ADAM_STEP)
                v_hat = v_new / (1.0 - ADAM_B2 ** ADAM_STEP)
                g_ref[...] = g
                d_ref[...] = -ADAM_LR * (m_hat / (jnp.sqrt(v_hat) + ADAM_EPS) + ADAM_WD * w_ref[...])
                mo_ref[...] = m_new
                vo_ref[...] = v_new

    blk = pl.BlockSpec((None, rb, cdim), lambda l, i, me_ref: (lo + l, i, 0))

    def rows(l, i, k):
        return jnp.where(l < k, 0, jnp.where(l == k, i, nblk - 1))

    land_specs = [pl.BlockSpec((N_DEV, rb, cdim), lambda l, i, me_ref, k=k: (0, rows(l, i, k), 0)) for k in range(nl)]
    src_specs = [pl.BlockSpec((None, rb, cdim), lambda l, i, me_ref, k=k: (me_ref[0], rows(l, i, k), 0)) for k in range(nl)]
    return pl.pallas_call(
        body, name=name,
        grid_spec=pltpu.PrefetchScalarGridSpec(
            num_scalar_prefetch=1, grid=(nl, nblk),
            in_specs=[blk, blk, blk] + land_specs + src_specs + [ANY] * n_prev, out_specs=[blk] * 4),
        out_shape=[jax.ShapeDtypeStruct((nl_all, r, cdim), F32)] * 4,
        input_output_aliases={4 + 2 * nl + j: j for j in range(n_prev)},
        compiler_params=_cparams(("arbitrary", "arbitrary")),
    )(me, w, m, v, *lands, *srcs, *([] if prev is None else prev))


def small_sum(parts, fold, name):
    rows = parts.shape[1]

    def dot3(xv, sel):
        out = jnp.zeros((xv.shape[0], sel.shape[1]), F32)
        for _ in range(3):
            hi = xv.astype(BF16)
            out = out + _dot(hi, sel)
            xv = xv - hi.astype(F32)
        return out

    def body(p_ref, fold_ref, o_ref):
        tot = p_ref[0]
        for i in range(1, N_DEV):
            tot = tot + p_ref[i]
        o_ref[...] = tot
        for l in range(rows // SM_ROWS):
            blk = tot[l * SM_ROWS:l * SM_ROWS + 8, 0:D]
            folded = dot3(blk, fold_ref[...])
            o_ref[l * SM_ROWS + 9:l * SM_ROWS + 10, 0:128] = folded[SM_QG:SM_QG + 1, :]
            o_ref[l * SM_ROWS + 10:l * SM_ROWS + 11, 0:128] = folded[SM_KG:SM_KG + 1, :]

    return pl.pallas_call(
        body, name=name, out_shape=jax.ShapeDtypeStruct((rows, 2 * D), F32),
        compiler_params=_cparams(None),
    )(parts, fold)


def kernel(x, norm_g, w_in, conv_w, q_norm_g, k_norm_g, sinks, w_conv_out, w_attn_out, gate_b, w_out, loss_target, m_norm_g, m_w_in, m_conv_w, m_q_norm_g, m_k_norm_g, m_sinks, m_w_conv_out, m_w_attn_out, m_gate_b, m_w_out, v_norm_g, v_w_in, v_conv_w, v_q_norm_g, v_k_norm_g, v_sinks, v_w_conv_out, v_w_attn_out, v_gate_b, v_w_out):
    c = _selectors()
    me = 4 * lax.axis_index("x") + 2 * lax.axis_index("y") + lax.axis_index("c")

    w_in_t, m_w_in_t, v_w_in_t = (jnp.swapaxes(a, 1, 2) for a in (w_in, m_w_in, v_w_in))

    def shards(l):
        return [w_in_t[l].astype(BF16), w_conv_out[l].astype(BF16), w_attn_out[l].astype(BF16), w_out[l].astype(BF16)]

    def gather_start(arrs, after, tag):
        return exchange_start(arrs, False, ALL_PEERS, after, f"gather_start_{tag}")

    h = x[0]
    saved, lws = [], []
    gather = None
    for l in range(DEPTH):
        if l == 0:
            lands = all_gather(shards(0) + [conv_w], h, "gather_0")
            conv_full = jnp.transpose(lands[4], (1, 2, 0, 3)).reshape(DEPTH, 3, D)
            after = lands[1]
        else:
            mine, lands = exchange_wait(gather, h, f"gather_wait_{l}")
            lands = [lax.dynamic_update_index_in_dim(land, src, me, 0) for land, src in zip(lands, mine)]
            after = mine[0]
        if l + 1 < DEPTH:
            gather = gather_start(shards(l + 1), after, l + 1)
        lws.append(layer_operands(l, norm_g, conv_full, q_norm_g, k_norm_g, sinks, gate_b,
                                  {l: lands[0].reshape(IN_COLS, D)}, {l: lands[1].reshape(D, D)},
                                  {l: lands[2].reshape(D, D)}, {l: lands[3].reshape(D, D)}))
        u, hb = inproj_fwd(h, lws[l]["ng"] + gather["token"][0:1, 0:1], lws[l]["w_in"], f"inproj_fwd_{l}")
        x_in = h
        h, ya, yb = mixer_fwd(x_in, u, lws[l]["cw"], lws[l]["qg"], lws[l]["kg"], lws[l]["sinks"], lws[l]["gb"],
                              lws[l]["wco"], lws[l]["wao"], lws[l]["wout"], c, f"mixer_fwd_{l}")
        saved.append((x_in, u, hb, ya, yb))
    dh, loss_part = loss_head(h, loss_target[0], "loss_head")

    grads, scatters = [None] * DEPTH, [[] for _ in range(DEPTH)]
    for l in reversed(range(DEPTH)):
        def send_off(g, done, l=l):
            rest = [g[k].reshape(N_DEV, SHARD_ROWS, D) for k in ("wco", "wao", "wout")] if done else []
            first = [g["w_in"].reshape(N_DEV, SHARD_COLS, D)] if done == (l > 0) else []
            if not first + rest:
                return None
            tag = f"{l}" if l > 0 else ("0_rest" if done else "0_in")
            scatters[l].append(exchange_start(first + rest, True, ALL_PEERS, g["small"], f"scatter_start_{tag}"))
            return scatters[l][-1]["token"]
        dh, grads[l] = layer_bwd(dh, saved[l], lws[l], c, l, send_off)

    me1 = me.astype(jnp.int32).reshape(1)
    mine, lands = {}, {}
    for l in (3, 2, 1):
        mine[l], lands[l] = exchange_wait(scatters[l][0], dh, f"scatter_wait_{l}")
    weights = [(w_in_t, m_w_in_t, v_w_in_t, "w_in"), (w_conv_out, m_w_conv_out, v_w_conv_out, "w_conv_out"),
               (w_attn_out, m_w_attn_out, v_w_attn_out, "w_attn_out"), (w_out, m_w_out, v_w_out, "w_out")]
    upd = [adamw_layers(w, m, v, [lands[l][i] for l in (1, 2, 3)], [mine[l][i] for l in (1, 2, 3)], me1, 1, None,
                        f"adamw_{n}_upper") for i, (w, m, v, n) in enumerate(weights)]
    m_in, l_in = exchange_wait(scatters[0][0], upd[0][0], "scatter_wait_0_in")
    m_rest, l_rest = exchange_wait(scatters[0][1], upd[3][0], "scatter_wait_0_rest")
    mine[0], lands[0] = list(m_in) + list(m_rest), list(l_in) + list(l_rest)
    upd = [adamw_layers(w, m, v, [lands[0][i]], [mine[0][i]], me1, 0, upd[i], f"adamw_{n}_0")
           for i, (w, m, v, n) in enumerate(weights)]
    u_in = [jnp.swapaxes(o, 1, 2) for o in upd[0]]
    u_co, u_ao, u_out = upd[1], upd[2], upd[3]

    blocks = []
    for l in range(DEPTH):
        blk = grads[l]["small"]
        blk = blk.at[SM_NORM, 0:D].set(grads[l]["dng"][0])
        if l == 0:
            blk = blk.at[SM_LOSS, 0:128].set(loss_part[0])
        blocks.append(blk)
    gathered = all_gather([jnp.concatenate(blocks, axis=0)], u_out[0], "gather_small")[0]
    tot = small_sum(gathered, c["fold"], "small_sum")
    tot = tot.reshape(DEPTH, SM_ROWS, 2 * D)
    loss = tot[0, SM_LOSS, 0]

    def update_small(w, m, v, g, name):
        return adamw(w, m, v, g[None], name)

    u_ng = update_small(norm_g, m_norm_g, v_norm_g, tot[:, SM_NORM, 0:D], "adamw_norm_g")
    u_qg = update_small(q_norm_g, m_q_norm_g, v_q_norm_g, tot[:, 9, 0:HEAD], "adamw_q_norm_g")
    u_kg = update_small(k_norm_g, m_k_norm_g, v_k_norm_g, tot[:, 10, 0:HEAD], "adamw_k_norm_g")
    u_sk = update_small(sinks, m_sinks, v_sinks, tot[:, SM_SINK, 0:16], "adamw_sinks")
    u_gb = update_small(gate_b, m_gate_b, v_gate_b, tot[:, SM_GATE, :], "adamw_gate_b")
    g_conv = lax.dynamic_slice_in_dim(tot[:, SM_CONV:SM_CONV + 3, 0:D], me * SHARD_ROWS, SHARD_ROWS, axis=2)
    u_cw = [o.reshape(DEPTH, 3, SHARD_ROWS) for o in update_small(
        conv_w.reshape(DEPTH * 3, SHARD_ROWS), m_conv_w.reshape(DEPTH * 3, SHARD_ROWS),
        v_conv_w.reshape(DEPTH * 3, SHARD_ROWS), g_conv.reshape(DEPTH * 3, SHARD_ROWS), "adamw_conv_w")]

    order = [u_ng, u_in, u_cw, u_qg, u_kg, u_sk, u_co, u_ao, u_gb, u_out]
    return (loss, dh[None], *[u[0] for u in order], *[u[1] for u in order], *[u[2] for u in order], *[u[3] for u in order])
```

```python
import functools

import jax
import jax.numpy as jnp
from jax import lax
from jax.experimental import pallas as pl
from jax.experimental.pallas import tpu as pltpu

F32 = jnp.float32
BF16 = jnp.bfloat16

N_DEV = 8
DEPTH = 4
D = 1024
N_KV = 4
GROUP = 4
HEAD = 64
BLK = 128
KVW = N_KV * HEAD
IN_COLS = 8704
SHARD_COLS = IN_COLS // N_DEV
SHARD_ROWS = D // N_DEV
C_VC, C_BC, C_CC, C_ZC, C_Q, C_K, C_V, C_ZA, C_GA, C_GB = 0, 1024, 2048, 3072, 4096, 5120, 5376, 5632, 6656, 7680
EPS = 1e-6
NEG_INF = -1e30
SCALE = HEAD ** -0.5

ADAM_LR = 0.001
ADAM_B1 = 0.9
ADAM_B2 = 0.999
ADAM_EPS = 1e-08
ADAM_WD = 0.01
ADAM_STEP = 10

VMEM_LIMIT = 60 * 1024 * 1024
SM_ROWS = 16
SM_GATE, SM_CONV, SM_QG, SM_KG, SM_SINK, SM_NORM, SM_LOSS = 0, 1, 4, 5, 6, 7, 8


def _cparams(sem):
    return pltpu.CompilerParams(dimension_semantics=sem, vmem_limit_bytes=VMEM_LIMIT)


def _dot(a, b):
    return jnp.dot(a, b, preferred_element_type=F32)


def _dot_nt(a, b):
    return lax.dot_general(a, b, (((1,), (1,)), ((), ())), preferred_element_type=F32)


def _dot_tn(a, b):
    return lax.dot_general(a, b, (((0,), (0,)), ((), ())), preferred_element_type=F32)


def _dot2(x, sel):
    hi = x.astype(BF16)
    lo = (x - hi.astype(F32)).astype(BF16)
    return _dot(hi, sel) + _dot(lo, sel)


def _sigmoid(z):
    return 0.5 * jnp.tanh(0.5 * z) + 0.5


def _head_mean(t, sel, exp):
    return _dot2(_dot(t.astype(BF16), sel) * (1.0 / HEAD), exp)


def _shift_down(a, k, before):
    r = pltpu.roll(a, k, 0)
    row = lax.broadcasted_iota(jnp.int32, (8, 1), 0)
    head = jnp.where(row < k, pltpu.roll(before, k, 0), r[0:8, :])
    return jnp.concatenate([head, r[8:, :]], axis=0)


def _shift_up(a, k, after):
    n = a.shape[0]
    r = pltpu.roll(a, n - k, 0)
    row = lax.broadcasted_iota(jnp.int32, (8, 1), 0)
    tail = jnp.where(row >= 8 - k, pltpu.roll(after, 8 - k, 0), r[n - 8:, :])
    return jnp.concatenate([r[:n - 8, :], tail], axis=0)


def _bf(ref, c0, width):
    return ref[:, c0:c0 + width].astype(F32)


def _selectors():
    c = jnp.arange(D)
    sel_q = (c[:, None] // HEAD == jnp.arange(128)[None, :]).astype(BF16)
    ck = jnp.arange(KVW)
    sel_k = (ck[:, None] // HEAD == jnp.arange(128)[None, :]).astype(BF16)
    src = jnp.arange(KVW)[:, None]
    dst = jnp.arange(KVW)[None, :]
    rep = jnp.stack([((src // HEAD == h) & (src % HEAD == dst % HEAD)).astype(BF16) for h in range(N_KV)])
    fold = (c[:, None] % HEAD == jnp.arange(128)[None, :]).astype(BF16)
    return dict(sel_q=sel_q, exp_q=sel_q.T, sel_k=sel_k, exp_k=sel_k.T, rep=rep, rep_t=jnp.swapaxes(rep, 1, 2), fold=fold)


def inproj_fwd(x, ng, w, name):
    t = x.shape[0]
    tm = min(1024, t)
    cb = 2176
    def body(x_ref, ng_ref, w_ref, u_ref, h_ref, h_scr):
        @pl.when(pl.program_id(1) == 0)
        def _():
            xf = x_ref[...]
            r = lax.rsqrt(jnp.mean(xf * xf, axis=-1, keepdims=True) + EPS)
            hb = (xf * r * ng_ref[...]).astype(BF16)
            h_scr[...] = hb
            h_ref[...] = hb
        u_ref[...] = _dot_nt(h_scr[...], w_ref[...]).astype(BF16)

    return pl.pallas_call(
        body, name=name, grid=(t // tm, IN_COLS // cb),
        in_specs=[pl.BlockSpec((tm, D), lambda i, j: (i, 0)), pl.BlockSpec((1, D), lambda i, j: (0, 0)),
                  pl.BlockSpec((cb, D), lambda i, j: (j, 0))],
        out_specs=[pl.BlockSpec((tm, cb), lambda i, j: (i, j)), pl.BlockSpec((tm, D), lambda i, j: (i, 0))],
        out_shape=[jax.ShapeDtypeStruct((t, IN_COLS), BF16), jax.ShapeDtypeStruct((t, D), BF16)],
        scratch_shapes=[pltpu.VMEM((tm, D), BF16)],
        compiler_params=_cparams(("arbitrary", "arbitrary")),
    )(x, ng, w)


def _conv_fwd(u_ref, uvc_prev, ucc_prev, cw_ref, is_first, tm):
    p = _bf(u_ref, C_CC, D) * _bf(u_ref, C_VC, D)
    pprev = ucc_prev[...].astype(F32) * uvc_prev[...].astype(F32)
    pprev = jnp.where(is_first, 0.0, pprev)
    p1 = _shift_down(p, 1, pprev[8:16, :])
    p2 = _shift_down(p, 2, pprev[8:16, :])
    cw = cw_ref[...]
    conv = cw[0:1, :] * p2 + cw[1:2, :] * p1 + cw[2:3, :] * p
    return p, p1, p2, conv


def _attn_inputs(u_ref, ukv_prev, qg_ref, kg_ref, c, tm):
    q = _bf(u_ref, C_Q, D)
    rq = lax.rsqrt(_head_mean(q * q, c["sel_q"][...], c["exp_q"][...]) + EPS)
    qhat = q * rq
    qn = (qhat * qg_ref[...]).astype(BF16)
    kband = jnp.concatenate([ukv_prev[:, 0:KVW].astype(F32), _bf(u_ref, C_K, KVW)], axis=0)
    rk = lax.rsqrt(_head_mean(kband * kband, c["sel_k"][...], c["exp_k"][...]) + EPS)
    khat = kband * rk
    knb = (khat * kg_ref[...]).astype(BF16)
    vband = jnp.concatenate([ukv_prev[:, KVW:2 * KVW], u_ref[:, C_V:C_V + KVW]], axis=0)
    kt = [_dot(knb, c["rep"][h]).astype(BF16) for h in range(N_KV)]
    vt = [_dot(vband, c["rep"][h]).astype(BF16) for h in range(N_KV)]
    return qhat, rq, qn, khat, rk, kt, vt


def _attn_masks(is_first):
    rows = GROUP * BLK
    r = lax.broadcasted_iota(jnp.int32, (rows, 2 * BLK), 0)
    kk = lax.broadcasted_iota(jnp.int32, (rows, 2 * BLK), 1)
    qq = r % BLK
    valid = (kk > qq) & (kk <= qq + BLK)
    valid_first = valid & ((kk >= BLK) | jnp.logical_not(is_first))
    lane_grp = lax.broadcasted_iota(jnp.int32, (BLK, KVW), 1) // HEAD
    row_grp = lax.broadcasted_iota(jnp.int32, (rows, 1), 0) // BLK
    return valid, valid_first, lane_grp, row_grp


def _sink_col(sinks_ref, h, row_grp):
    col = jnp.full(row_grp.shape, sinks_ref[0, GROUP * h], F32)
    for gi in range(1, GROUP):
        col = jnp.where(row_grp == gi, sinks_ref[0, GROUP * h + gi], col)
    return col


def _stack_groups(a256, lane_grp):
    zero = jnp.zeros_like(a256)
    return jnp.concatenate([jnp.where(lane_grp == gi, a256, zero) for gi in range(GROUP)], axis=0)


def _unstack_groups(a4, lane_grp):
    out = jnp.where(lane_grp == 0, a4[0:BLK], 0.0)
    for gi in range(1, GROUP):
        out = out + jnp.where(lane_grp == gi, a4[gi * BLK:(gi + 1) * BLK], 0.0)
    return out


def _softmax_block(qs, kt_b, valid, sink):
    s = _dot_nt(qs, kt_b)
    s = jnp.where(valid, s, NEG_INF)
    m = jnp.maximum(jnp.max(s, axis=-1, keepdims=True), sink)
    e = jnp.exp(s - m)
    es = jnp.exp(sink - m)
    inv = 1.0 / (jnp.sum(e, axis=-1, keepdims=True) + es)
    return e * inv, es * inv


def _mixer_specs(t, tm, n_tiles, tile_of):
    nb = tm // BLK
    u_spec = pl.BlockSpec((tm, IN_COLS), lambda g: (tile_of(g), 0))
    ukv_prev = pl.BlockSpec((BLK, 2 * KVW), lambda g: (jnp.maximum(tile_of(g) * nb - 1, 0), C_K // (2 * KVW)))
    uvc_prev = pl.BlockSpec((16, D), lambda g: (jnp.maximum(tile_of(g) * (tm // 16) - 1, 0), C_VC // D))
    ucc_prev = pl.BlockSpec((16, D), lambda g: (jnp.maximum(tile_of(g) * (tm // 16) - 1, 0), C_CC // D))
    return u_spec, ukv_prev, uvc_prev, ucc_prev


def _full(shape):
    n = len(shape)
    return pl.BlockSpec(shape, lambda g: (0,) * n)


def mixer_fwd(x, u, cw, qg, kg, sinks, gb, wco, wao, wout, c, name):
    t = x.shape[0]
    tm = min(256, t)
    n_tiles = t // tm
    nb = tm // BLK
    cn = sorted(c)

    def body(x_ref, u_ref, ukv_prev, uvc_prev, ucc_prev, cw_ref, qg_ref, kg_ref, sinks_ref, gb_ref, wco_ref, wao_ref,
             wout_ref, *rest):
        cref = dict(zip(cn, rest[:len(cn)]))
        xo_ref, ya_ref, yb_ref, o_scr = rest[len(cn):]
        is_first = pl.program_id(0) == 0
        _, _, _, conv = _conv_fwd(u_ref, uvc_prev, ucc_prev, cw_ref, is_first, tm)
        zc = _bf(u_ref, C_ZC, D)
        yc = _bf(u_ref, C_BC, D) * conv * (zc * _sigmoid(zc))
        ya = _dot(yc.astype(BF16), wco_ref[...])

        _, _, qn, _, _, kt, vt = _attn_inputs(u_ref, ukv_prev, qg_ref, kg_ref, cref, tm)
        valid, valid_first, lane_grp, row_grp = _attn_masks(is_first)
        for h in range(N_KV):
            sink = _sink_col(sinks_ref, h, row_grp)
            for b in range(nb):
                qs = _stack_groups(qn[b * BLK:(b + 1) * BLK, h * KVW:(h + 1) * KVW], lane_grp)
                pn, _ = _softmax_block(qs, kt[h][b * BLK:(b + 2) * BLK], valid_first if b == 0 else valid, sink)
                o4 = _dot(pn.astype(BF16), vt[h][b * BLK:(b + 2) * BLK])
                o_scr[b * BLK:(b + 1) * BLK, h * KVW:(h + 1) * KVW] = _unstack_groups(o4, lane_grp)
        za = _bf(u_ref, C_ZA, D)
        ob = o_scr[...] * (za * _sigmoid(za))
        yb = _dot(ob.astype(BF16), wao_ref[...])

        g_a = _sigmoid(_bf(u_ref, C_GA, D) + gb_ref[:, 0:D])
        g_b = _sigmoid(_bf(u_ref, C_GB, D) + gb_ref[:, D:2 * D])
        merged = g_a * ya + g_b * yb
        xo_ref[...] = x_ref[...] + _dot(merged.astype(BF16), wout_ref[...])
        ya_ref[...] = ya.astype(BF16)
        yb_ref[...] = yb.astype(BF16)

    u_spec, ukv_prev, uvc_prev, ucc_prev = _mixer_specs(t, tm, n_tiles, lambda g: g)
    tok = pl.BlockSpec((tm, D), lambda g: (g, 0))
    consts = [c[k] for k in cn]
    return pl.pallas_call(
        body, name=name, grid=(n_tiles,),
        in_specs=[tok, u_spec, ukv_prev, uvc_prev, ucc_prev, _full((8, D)), _full((1, D)), _full((1, KVW)),
                  pl.BlockSpec(memory_space=pltpu.SMEM), _full((1, 2 * D)), _full((D, D)), _full((D, D)), _full((D, D))]
                 + [_full(a.shape) for a in consts],
        out_specs=[tok, tok, tok],
        out_shape=[jax.ShapeDtypeStruct((t, D), F32)] + [jax.ShapeDtypeStruct((t, D), BF16)] * 2,
        scratch_shapes=[pltpu.VMEM((tm, D), F32)],
        compiler_params=_cparams(("arbitrary",)),
    )(x, u, u, u, u, cw, qg, kg, sinks, gb, wco, wao, wout, *consts)


def mixer_bwd(dout, u, ya, yb, cw, qg, kg, sinks, gb, wco, wao, wout, c, name):
    t = dout.shape[0]
    tm = min(256, t)
    n_tiles = t // tm
    nb = tm // BLK
    kb = tm + BLK
    cn = sorted(c)

    def body(dout_ref, u_ref, ukv_prev, uvc_prev, ucc_prev, ya_ref, yb_ref, cw_ref, qg_ref, kg_ref, sinks_ref, gb_ref,
             wco_ref, wao_ref, wout_ref, *rest):
        cref = dict(zip(cn, rest[:len(cn)]))
        (du_ref, small_ref, merged_ref, yc_ref, ob_ref, dya_ref, dyb_ref,
         o_scr, dq_scr, dk4_scr, dv4_scr, carry_kv, carry_conv) = rest[len(cn):]
        g = pl.program_id(0)
        is_first = g == n_tiles - 1

        @pl.when(g == 0)
        def _():
            carry_kv[...] = jnp.zeros_like(carry_kv)
            carry_conv[...] = jnp.zeros_like(carry_conv)
            small_ref[...] = jnp.zeros_like(small_ref)

        dout = dout_ref[...]
        dout_b = dout.astype(BF16)
        ya_v = ya_ref[...].astype(F32)
        yb_v = yb_ref[...].astype(F32)
        g_a = _sigmoid(_bf(u_ref, C_GA, D) + gb_ref[:, 0:D])
        g_b = _sigmoid(_bf(u_ref, C_GB, D) + gb_ref[:, D:2 * D])
        merged = g_a * ya_v + g_b * yb_v
        dmerged = _dot_nt(dout_b, wout_ref[...])
        merged_ref[...] = merged.astype(BF16)
        dya = dmerged * g_a
        dyb = dmerged * g_b
        dgl_a = dmerged * ya_v * g_a * (1.0 - g_a)
        dgl_b = dmerged * yb_v * g_b * (1.0 - g_b)
        du_ref[:, C_GA:C_GA + D] = dgl_a.astype(BF16)
        du_ref[:, C_GB:C_GB + D] = dgl_b.astype(BF16)
        small_ref[SM_GATE:SM_GATE + 1, 0:D] += jnp.sum(dgl_a, axis=0, keepdims=True)
        small_ref[SM_GATE:SM_GATE + 1, D:2 * D] += jnp.sum(dgl_b, axis=0, keepdims=True)

        p, p1, p2, conv = _conv_fwd(u_ref, uvc_prev, ucc_prev, cw_ref, is_first, tm)
        zc = _bf(u_ref, C_ZC, D)
        bc = _bf(u_ref, C_BC, D)
        sg = _sigmoid(zc)
        sc = zc * sg
        yc = bc * conv * sc
        dya_b = dya.astype(BF16)
        yc_ref[...] = yc.astype(BF16)
        dya_ref[...] = dya_b
        dyc = _dot_nt(dya_b, wco_ref[...])
        du_ref[:, C_BC:C_BC + D] = (dyc * conv * sc).astype(BF16)
        du_ref[:, C_ZC:C_ZC + D] = (dyc * bc * conv * (sg * (1.0 + zc * (1.0 - sg)))).astype(BF16)
        dconv = dyc * bc * sc
        small_ref[SM_CONV + 2:SM_CONV + 3, 0:D] += jnp.sum(dconv * p, axis=0, keepdims=True)
        small_ref[SM_CONV + 1:SM_CONV + 2, 0:D] += jnp.sum(dconv * p1, axis=0, keepdims=True)
        small_ref[SM_CONV:SM_CONV + 1, 0:D] += jnp.sum(dconv * p2, axis=0, keepdims=True)
        nxt = carry_conv[...]
        d1 = _shift_up(dconv, 1, nxt)
        d2 = _shift_up(dconv, 2, nxt)
        carry_conv[...] = dconv[0:8, :]
        cw = cw_ref[...]
        dp = cw[2:3, :] * dconv + cw[1:2, :] * d1 + cw[0:1, :] * d2
        du_ref[:, C_CC:C_CC + D] = (dp * _bf(u_ref, C_VC, D)).astype(BF16)
        du_ref[:, C_VC:C_VC + D] = (dp * _bf(u_ref, C_CC, D)).astype(BF16)

        dyb_b = dyb.astype(BF16)
        dob = _dot_nt(dyb_b, wao_ref[...])
        za = _bf(u_ref, C_ZA, D)
        sga = _sigmoid(za)
        sa = za * sga
        do = dob * sa
        qhat, rq, qn, khat, rk, kt, vt = _attn_inputs(u_ref, ukv_prev, qg_ref, kg_ref, cref, tm)
        valid, valid_first, lane_grp, row_grp = _attn_masks(is_first)
        dk4_scr[...] = jnp.zeros_like(dk4_scr)
        dv4_scr[...] = jnp.zeros_like(dv4_scr)
        lane16 = lax.broadcasted_iota(jnp.int32, (1, 2 * D), 1)
        dsink_row = jnp.zeros((1, 2 * D), F32)
        for h in range(N_KV):
            sink = _sink_col(sinks_ref, h, row_grp)
            dsink_col = jnp.zeros((GROUP * BLK, 1), F32)
            for b in range(nb):
                rows = slice(b * BLK, (b + 1) * BLK)
                band = slice(b * BLK, (b + 2) * BLK)
                cols = slice(h * KVW, (h + 1) * KVW)
                qs = _stack_groups(qn[rows, cols], lane_grp)
                pn, ps = _softmax_block(qs, kt[h][band], valid_first if b == 0 else valid, sink)
                pn_b = pn.astype(BF16)
                o4 = _dot(pn_b, vt[h][band])
                o_scr[rows, cols] = _unstack_groups(o4, lane_grp)
                dos = _stack_groups(do[rows, cols], lane_grp).astype(BF16)
                dpn = _dot_nt(dos, vt[h][band])
                delta = jnp.sum(pn * dpn, axis=-1, keepdims=True)
                ds = (pn * (dpn - delta)).astype(BF16)
                dsink_col = dsink_col - ps * delta
                dq_scr[rows, cols] = _unstack_groups(_dot(ds, kt[h][band]), lane_grp)
                dk4_scr[h, band, :] += _dot_tn(ds, qs)
                dv4_scr[h, band, :] += _dot_tn(pn_b, dos)
            for gi in range(GROUP):
                tot = jnp.sum(dsink_col[gi * BLK:(gi + 1) * BLK, :], axis=0, keepdims=True)
                dsink_row = dsink_row + jnp.where(lane16 == GROUP * h + gi, tot, 0.0)
        small_ref[SM_SINK:SM_SINK + 1, :] += dsink_row

        o = o_scr[...]
        ob_ref[...] = (o * sa).astype(BF16)
        dyb_ref[...] = dyb_b
        du_ref[:, C_ZA:C_ZA + D] = (dob * o * (sga * (1.0 + za * (1.0 - sga)))).astype(BF16)

        dqn = dq_scr[...]
        small_ref[SM_QG:SM_QG + 1, 0:D] += SCALE * jnp.sum(dqn * qhat, axis=0, keepdims=True)
        dqh = dqn * qg_ref[...]
        dq = rq * (dqh - qhat * _head_mean(dqh * qhat, cref["sel_q"][...], cref["exp_q"][...]))
        du_ref[:, C_Q:C_Q + D] = dq.astype(BF16)

        dkn_band = jnp.zeros((kb, KVW), F32)
        dv_band = jnp.zeros((kb, KVW), F32)
        for h in range(N_KV):
            dkn_band = dkn_band + _dot2(dk4_scr[h], cref["rep_t"][h])
            dv_band = dv_band + _dot2(dv4_scr[h], cref["rep_t"][h])
        carried = carry_kv[...]
        pad = jnp.zeros((tm - BLK, KVW), F32)
        if nb > 1:
            dkn = dkn_band[BLK:, :] + jnp.concatenate([pad, carried[:, 0:KVW]], axis=0)
            dv = dv_band[BLK:, :] + jnp.concatenate([pad, carried[:, KVW:2 * KVW]], axis=0)
        else:
            dkn = dkn_band[BLK:, :] + carried[:, 0:KVW]
            dv = dv_band[BLK:, :] + carried[:, KVW:2 * KVW]
        carry_kv[:, 0:KVW] = dkn_band[0:BLK, :]
        carry_kv[:, KVW:2 * KVW] = dv_band[0:BLK, :]
        khat_t = khat[BLK:, :]
        small_ref[SM_KG:SM_KG + 1, 0:KVW] += jnp.sum(dkn * khat_t, axis=0, keepdims=True)
        dkh = dkn * kg_ref[...]
        dk = rk[BLK:, :] * (dkh - khat_t * _head_mean(dkh * khat_t, cref["sel_k"][...], cref["exp_k"][...]))
        du_ref[:, C_K:C_K + KVW] = dk.astype(BF16)
        du_ref[:, C_V:C_V + KVW] = dv.astype(BF16)

    rev = lambda g: n_tiles - 1 - g
    u_spec, ukv_prev, uvc_prev, ucc_prev = _mixer_specs(t, tm, n_tiles, rev)
    tok = pl.BlockSpec((tm, D), lambda g: (rev(g), 0))
    consts = [c[k] for k in cn]
    wspec = _full((D, D))
    return pl.pallas_call(
        body, name=name, grid=(n_tiles,),
        in_specs=[tok, u_spec, ukv_prev, uvc_prev, ucc_prev, tok, tok, _full((8, D)), _full((1, D)), _full((1, KVW)),
                  pl.BlockSpec(memory_space=pltpu.SMEM), _full((1, 2 * D)), wspec, wspec, wspec]
                 + [_full(a.shape) for a in consts],
        out_specs=[pl.BlockSpec((tm, IN_COLS), lambda g: (rev(g), 0)), _full((SM_ROWS, 2 * D))] + [tok] * 5,
        out_shape=[jax.ShapeDtypeStruct((t, IN_COLS), BF16), jax.ShapeDtypeStruct((SM_ROWS, 2 * D), F32)]
                  + [jax.ShapeDtypeStruct((t, D), BF16)] * 5,
        scratch_shapes=[pltpu.VMEM((tm, D), F32), pltpu.VMEM((tm, D), F32),
                        pltpu.VMEM((N_KV, kb, KVW), F32), pltpu.VMEM((N_KV, kb, KVW), F32),
                        pltpu.VMEM((BLK, 2 * KVW), F32), pltpu.VMEM((8, D), F32)],
        compiler_params=_cparams(("arbitrary",)),
    )(dout, u, u, u, u, ya, yb, cw, qg, kg, sinks, gb, wco, wao, wout, *consts)


def matmul_tn(a, b, name, after=None):
    t, m = a.shape
    tk = min(1024, t)
    mb = 2176 if m == IN_COLS else m
    nk = t // tk

    def body(a_ref, b_ref, *rest):
        o_ref, acc = rest[-2:]
        k = pl.program_id(1)
        prod = _dot_tn(a_ref[...].astype(BF16), b_ref[...].astype(BF16))

        @pl.when(k == 0)
        def _():
            acc[...] = prod

        @pl.when(k > 0)
        def _():
            acc[...] += prod

        @pl.when(k == nk - 1)
        def _():
            o_ref[...] = acc[...].astype(BF16)

    return pl.pallas_call(
        body, name=name, grid=(m // mb, nk),
        in_specs=[pl.BlockSpec((tk, mb), lambda j, k: (k, j)), pl.BlockSpec((tk, D), lambda j, k: (k, 0))]
                 + ([] if after is None else [ANY]),
        out_specs=pl.BlockSpec((mb, D), lambda j, k: (j, 0)),
        out_shape=jax.ShapeDtypeStruct((m, D), BF16),
        scratch_shapes=[pltpu.VMEM((mb, D), F32)],
        compiler_params=_cparams(("arbitrary", "arbitrary")),
    )(a, b, *([] if after is None else [after]))


def inproj_bwd_x(du, w, x, ng, dout, name):
    t = x.shape[0]
    tm = min(1024, t)
    kc = 2176
    nk = IN_COLS // kc

    def body(du_ref, w_ref, x_ref, ng_ref, dout_ref, dx_ref, dng_ref, acc):
        i = pl.program_id(0)
        k = pl.program_id(1)
        prod = _dot(du_ref[...], w_ref[...])

        @pl.when(k == 0)
        def _():
            acc[...] = prod

        @pl.when(k > 0)
        def _():
            acc[...] += prod

        @pl.when((i == 0) & (k == 0))
        def _():
            dng_ref[...] = jnp.zeros_like(dng_ref)

        @pl.when(k == nk - 1)
        def _():
            dh = acc[...]
            xf = x_ref[...]
            r = lax.rsqrt(jnp.mean(xf * xf, axis=-1, keepdims=True) + EPS)
            xhat = xf * r
            dng_ref[0:1, :] += jnp.sum(dh * xhat, axis=0, keepdims=True)
            dxh = dh * ng_ref[...]
            dx_ref[...] = dout_ref[...] + r * (dxh - xhat * jnp.mean(dxh * xhat, axis=-1, keepdims=True))

    tok = pl.BlockSpec((tm, D), lambda i, k: (i, 0))
    return pl.pallas_call(
        body, name=name, grid=(t // tm, nk),
        in_specs=[pl.BlockSpec((tm, kc), lambda i, k: (i, k)), pl.BlockSpec((kc, D), lambda i, k: (k, 0)), tok,
                  pl.BlockSpec((1, D), lambda i, k: (0, 0)), tok],
        out_specs=[tok, pl.BlockSpec((8, D), lambda i, k: (0, 0))],
        out_shape=[jax.ShapeDtypeStruct((t, D), F32), jax.ShapeDtypeStruct((8, D), F32)],
        scratch_shapes=[pltpu.VMEM((tm, D), F32)],
        compiler_params=_cparams(("arbitrary", "arbitrary")),
    )(du, w, x, ng, dout)


def loss_head(y, target, name):
    t = y.shape[0]
    tm = min(1024, t)

    def body(y_ref, t_ref, dy_ref, loss_ref):
        @pl.when(pl.program_id(0) == 0)
        def _():
            loss_ref[...] = jnp.zeros_like(loss_ref)
        err = y_ref[...] - t_ref[...]
        dy_ref[...] = err * (1.0 / D)
        part = jnp.sum(jnp.sum(err * err, axis=-1, keepdims=True) * (1.0 / D), axis=0, keepdims=True)
        loss_ref[...] += 0.5 * part

    tok = pl.BlockSpec((tm, D), lambda i: (i, 0))
    return pl.pallas_call(
        body, name=name, grid=(t // tm,), in_specs=[tok, tok],
        out_specs=[tok, pl.BlockSpec((8, 128), lambda i: (0, 0))],
        out_shape=[jax.ShapeDtypeStruct((t, D), F32), jax.ShapeDtypeStruct((8, 128), F32)],
        compiler_params=_cparams(("arbitrary",)),
    )(y, target)


def layer_operands(l, norm_g, conv_w_full, q_norm_g, k_norm_g, sinks, gate_b, w_in_b, wco_b, wao_b, wout_b):
    return dict(
        ng=norm_g[l][None, :], cw=jnp.pad(conv_w_full[l], ((0, 5), (0, 0))),
        qg=jnp.tile(q_norm_g[l] * SCALE, D // HEAD)[None, :], kg=jnp.tile(k_norm_g[l], N_KV)[None, :],
        sinks=sinks[l][None, :], gb=gate_b[l][None, :],
        w_in=w_in_b[l], wco=wco_b[l], wao=wao_b[l], wout=wout_b[l])


def layer_bwd(dout, saved, lw, c, l, send_off):
    x, u, h, ya, yb = saved
    du, small, merged, yc, ob, dya, dyb = mixer_bwd(dout, u, ya, yb, lw["cw"], lw["qg"], lw["kg"], lw["sinks"], lw["gb"],
                                                    lw["wco"], lw["wao"], lw["wout"], c, f"mixer_bwd_{l}")
    grads = dict(w_in=matmul_tn(du, h, f"dw_in_{l}"), small=small)
    token = send_off(grads, False)
    grads["wout"] = matmul_tn(merged, dout, f"dw_out_{l}", after=token)
    grads["wco"] = matmul_tn(yc, dya, f"dw_conv_out_{l}")
    grads["wao"] = matmul_tn(ob, dyb, f"dw_attn_out_{l}")
    token = send_off(grads, True)
    dx, grads["dng"] = inproj_bwd_x(du, lw["w_in"], x, lw["ng"] + token[0:1, 0:1], dout, f"inproj_bwd_{l}")
    return dx, grads


MESH = pl.DeviceIdType.MESH
ANY = pl.BlockSpec(memory_space=pl.ANY)


def _place():
    return lax.axis_index("x"), lax.axis_index("y"), lax.axis_index("c")


def all_gather(arrs, after, name):
    n = len(arrs)

    def body(*refs):
        ins, outs = refs[:n], refs[n + 1:2 * n + 1]
        send_sems, recv_sems, local_sems = refs[2 * n + 1:]
        x, y, c = _place()
        me, sibling = (x, y, c), (x, y, 1 - c)
        chips = [(1 - x, y), (x, 1 - y), (1 - x, 1 - y)]

        def slot(a, block):
            px, py, pc = block
            return outs[a].at[4 * px + 2 * py + pc]

        def copy(a, k, block, to, src=None):
            return pltpu.make_async_remote_copy(
                src_ref=slot(a, block) if src is None else src, dst_ref=slot(a, block),
                send_sem=send_sems.at[a, k], recv_sem=recv_sems.at[a, k], device_id=to, device_id_type=MESH)

        mine = [pltpu.make_async_copy(ins[a], slot(a, me), local_sems.at[a]) for a in range(n)]
        for cp in mine:
            cp.start()
        first = []
        for a in range(n):
            first.append(copy(a, 0, me, sibling, src=ins[a]))
            first += [copy(a, 1 + j, me, (*chip, c), src=ins[a]) for j, chip in enumerate(chips)]
        for cp in first:
            cp.start()
        passed = []
        for j, chip in enumerate(chips):
            for a in range(n):
                copy(a, 1 + j, (*chip, c), me).wait_recv()
                passed.append(copy(a, 4 + j, (*chip, c), sibling))
                passed[-1].start()
        for a in range(n):
            copy(a, 0, sibling, me).wait_recv()
            for j, chip in enumerate(chips):
                copy(a, 4 + j, (*chip, 1 - c), me).wait_recv()
        for cp in first + passed:
            cp.wait_send()
        for cp in mine:
            cp.wait()

    return pl.pallas_call(
        body, name=name, in_specs=[ANY] * (n + 1), out_specs=[ANY] * n,
        out_shape=[jax.ShapeDtypeStruct((N_DEV,) + a.shape, a.dtype) for a in arrs],
        scratch_shapes=[pltpu.SemaphoreType.DMA((n, 7)), pltpu.SemaphoreType.DMA((n, 7)), pltpu.SemaphoreType.DMA((n,))],
    )(*arrs, after)


HBM_SPEC = pl.BlockSpec(memory_space=pltpu.HBM)
SEM_SPEC = pl.BlockSpec(memory_space=pltpu.SEMAPHORE)
EFFECT = pltpu.SideEffectType.DATAFLOW_SIDE_EFFECTING


ALL_PEERS = (1, 2, 3, 4, 5, 6, 7)


def _flip(place, k):
    x, y, c = place
    return (1 - x if k & 4 else x, 1 - y if k & 2 else y, 1 - c if k & 1 else c)


def _slot(place):
    return 4 * place[0] + 2 * place[1] + place[2]


def _exchange_copies(ins, lands, send_sems, recv_sems, scatter, flips, arriving):
    me = _place()
    out = []
    for a in range(len(ins)):
        for i, k in enumerate(flips):
            peer = _flip(me, k)
            out.append(pltpu.make_async_remote_copy(
                src_ref=ins[a].at[_slot(peer)] if scatter else ins[a],
                dst_ref=lands[a].at[_slot(peer) if arriving else _slot(me)],
                send_sem=send_sems.at[a * len(flips) + i], recv_sem=recv_sems.at[a * len(flips) + i],
                device_id=peer, device_id_type=MESH))
    return out


def exchange_start(arrs, scatter, flips, after, name):
    n = len(arrs)
    lands = [lax.empty(a.shape if scatter else (N_DEV,) + a.shape, a.dtype) for a in arrs]

    def body(*refs):
        ins, lz = refs[:n], refs[n:2 * n]
        send_sems, recv_sems = refs[2 * n + 1], refs[2 * n + 2]
        token = refs[-1]
        for cp in _exchange_copies(ins, lz, send_sems, recv_sems, scatter, flips, False):
            cp.start()
        token[...] = jnp.zeros_like(token)

    sems = pltpu.SemaphoreType.DMA((n * len(flips),))
    res = pl.pallas_call(
        body, name=name,
        out_shape=(sems, sems, *[pltpu.HBM(a.shape, a.dtype) for a in arrs], *[pltpu.HBM(a.shape, a.dtype) for a in lands],
                   jax.ShapeDtypeStruct((8, 128), F32)),
        in_specs=[HBM_SPEC] * (2 * n) + [ANY],
        out_specs=(SEM_SPEC, SEM_SPEC, *[HBM_SPEC] * (2 * n), pl.BlockSpec(memory_space=pltpu.VMEM)),
        input_output_aliases={i: 2 + i for i in range(2 * n)},
        compiler_params=pltpu.CompilerParams(has_side_effects=EFFECT),
    )(*[pltpu.with_memory_space_constraint(a, pltpu.HBM) for a in arrs],
      *[pltpu.with_memory_space_constraint(a, pltpu.HBM) for a in lands], after)
    return dict(send=res[0], recv=res[1], srcs=res[2:2 + n], lands=res[2 + n:2 + 2 * n], token=res[-1], scatter=scatter,
                flips=flips)


def exchange_wait(state, after, name):
    n = len(state["srcs"])

    def body(*refs):
        ins, lz = refs[:n], refs[n:2 * n]
        send_sems, recv_sems = refs[2 * n], refs[2 * n + 1]
        for cp in _exchange_copies(ins, lz, send_sems, recv_sems, state["scatter"], state["flips"], True):
            cp.wait_send()
            cp.wait_recv()

    both = list(state["srcs"]) + list(state["lands"])
    res = pl.pallas_call(
        body, name=name, out_shape=tuple(pltpu.HBM(a.shape, a.dtype) for a in both),
        in_specs=[HBM_SPEC] * (2 * n) + [SEM_SPEC, SEM_SPEC, ANY], out_specs=tuple([HBM_SPEC] * (2 * n)),
        input_output_aliases={i: i for i in range(2 * n)},
        compiler_params=pltpu.CompilerParams(has_side_effects=EFFECT),
    )(*both, state["send"], state["recv"], after)
    return res[:n], res[n:]


def adamw(w, m, v, parts, name):
    r, cdim = w.shape
    n_parts = parts.shape[0]
    rb = 256 if r % 256 == 0 else (SHARD_COLS // 4 if r == SHARD_COLS else r)

    def body(w_ref, m_ref, v_ref, p_ref, g_ref, d_ref, mo_ref, vo_ref):
        g = p_ref[0].astype(F32)
        for i in range(1, n_parts):
            g = g + p_ref[i].astype(F32)
        m_new = ADAM_B1 * m_ref[...] + (1.0 - ADAM_B1) * g
        v_new = ADAM_B2 * v_ref[...] + (1.0 - ADAM_B2) * (g * g)
        m_hat = m_new / (1.0 - ADAM_B1 ** ADAM_STEP)
        v_hat = v_new / (1.0 - ADAM_B2 ** ADAM_STEP)
        g_ref[...] = g
        d_ref[...] = -ADAM_LR * (m_hat / (jnp.sqrt(v_hat) + ADAM_EPS) + ADAM_WD * w_ref[...])
        mo_ref[...] = m_new
        vo_ref[...] = v_new

    blk = pl.BlockSpec((rb, cdim), lambda i: (i, 0))
    return pl.pallas_call(
        body, name=name, grid=(r // rb,),
        in_specs=[blk, blk, blk, pl.BlockSpec((n_parts, rb, cdim), lambda i: (0, i, 0))],
        out_specs=[blk] * 4, out_shape=[jax.ShapeDtypeStruct((r, cdim), F32)] * 4,
        compiler_params=_cparams(("arbitrary",)),
    )(w, m, v, parts)


def adamw_layers(w, m, v, lands, srcs, me, lo, prev, name):
    nl_all, r, cdim = w.shape
    nl = len(lands)
    rb = 256 if r % 256 == 0 else (SHARD_COLS // 4 if r == SHARD_COLS else r)
    nblk = r // rb
    n_prev = 0 if prev is None else 4

    def body(me_ref, w_ref, m_ref, v_ref, *rest):
        land_refs, src_refs = rest[:nl], rest[nl:2 * nl]
        g_ref, d_ref, mo_ref, vo_ref = rest[2 * nl + n_prev:]
        for k in range(nl):
            @pl.when(pl.program_id(0) == k)
            def _(k=k):
                own = src_refs[k][...].astype(F32)
                g = jnp.where(me_ref[0] == 0, own, land_refs[k][0].astype(F32))
                for i in range(1, N_DEV):
                    g = g + jnp.where(me_ref[0] == i, own, land_refs[k][i].astype(F32))
                m_new = ADAM_B1 * m_ref[...] + (1.0 - ADAM_B1) * g
                v_new = ADAM_B2 * v_ref[...] + (1.0 - ADAM_B2) * (g * g)
                m_hat = m_new / (1.0 - ADAM_B1 ** ADAM_STEP)
                v_hat = v_new / (1.0 - ADAM_B2 ** ADAM_STEP)
                g_ref[...] = g
                d_ref[...] = -ADAM_LR * (m_hat / (jnp.sqrt(v_hat) + ADAM_EPS) + ADAM_WD * w_ref[...])
                mo_ref[...] = m_new
                vo_ref[...] = v_new

    blk = pl.BlockSpec((None, rb, cdim), lambda l, i, me_ref: (lo + l, i, 0))

    def rows(l, i, k):
        return jnp.where(l < k, 0, jnp.where(l == k, i, nblk - 1))

    land_specs = [pl.BlockSpec((N_DEV, rb, cdim), lambda l, i, me_ref, k=k: (0, rows(l, i, k), 0)) for k in range(nl)]
    src_specs = [pl.BlockSpec((None, rb, cdim), lambda l, i, me_ref, k=k: (me_ref[0], rows(l, i, k), 0)) for k in range(nl)]
    return pl.pallas_call(
        body, name=name,
        grid_spec=pltpu.PrefetchScalarGridSpec(
            num_scalar_prefetch=1, grid=(nl, nblk),
            in_specs=[blk, blk, blk] + land_specs + src_specs + [ANY] * n_prev, out_specs=[blk] * 4),
        out_shape=[jax.ShapeDtypeStruct((nl_all, r, cdim), F32)] * 4,
        input_output_aliases={4 + 2 * nl + j: j for j in range(n_prev)},
        compiler_params=_cparams(("arbitrary", "arbitrary")),
    )(me, w, m, v, *lands, *srcs, *([] if prev is None else prev))


def small_sum(parts, fold, name):
    rows = parts.shape[1]

    def dot3(xv, sel):
        out = jnp.zeros((xv.shape[0], sel.shape[1]), F32)
        for _ in range(3):
            hi = xv.astype(BF16)
            out = out + _dot(hi, sel)
            xv = xv - hi.astype(F32)
        return out

    def body(p_ref, fold_ref, o_ref):
        tot = p_ref[0]
        for i in range(1, N_DEV):
            tot = tot + p_ref[i]
        o_ref[...] = tot
        for l in range(rows // SM_ROWS):
            blk = tot[l * SM_ROWS:l * SM_ROWS + 8, 0:D]
            folded = dot3(blk, fold_ref[...])
            o_ref[l * SM_ROWS + 9:l * SM_ROWS + 10, 0:128] = folded[SM_QG:SM_QG + 1, :]
            o_ref[l * SM_ROWS + 10:l * SM_ROWS + 11, 0:128] = folded[SM_KG:SM_KG + 1, :]

    return pl.pallas_call(
        body, name=name, out_shape=jax.ShapeDtypeStruct((rows, 2 * D), F32),
        compiler_params=_cparams(None),
    )(parts, fold)


def kernel(x, norm_g, w_in, conv_w, q_norm_g, k_norm_g, sinks, w_conv_out, w_attn_out, gate_b, w_out, loss_target, m_norm_g, m_w_in, m_conv_w, m_q_norm_g, m_k_norm_g, m_sinks, m_w_conv_out, m_w_attn_out, m_gate_b, m_w_out, v_norm_g, v_w_in, v_conv_w, v_q_norm_g, v_k_norm_g, v_sinks, v_w_conv_out, v_w_attn_out, v_gate_b, v_w_out):
    c = _selectors()
    me = 4 * lax.axis_index("x") + 2 * lax.axis_index("y") + lax.axis_index("c")

    w_in_t, m_w_in_t, v_w_in_t = (jnp.swapaxes(a, 1, 2) for a in (w_in, m_w_in, v_w_in))

    def shards(l):
        return [w_in_t[l].astype(BF16), w_conv_out[l].astype(BF16), w_attn_out[l].astype(BF16), w_out[l].astype(BF16)]

    def gather_start(arrs, after, tag):
        return exchange_start(arrs, False, ALL_PEERS, after, f"gather_start_{tag}")

    h = x[0]
    saved, lws = [], []
    gather = None
    for l in range(DEPTH):
        if l == 0:
            lands = all_gather(shards(0) + [conv_w], h, "gather_0")
            conv_full = jnp.transpose(lands[4], (1, 2, 0, 3)).reshape(DEPTH, 3, D)
            after = lands[1]
        else:
            mine, lands = exchange_wait(gather, h, f"gather_wait_{l}")
            lands = [lax.dynamic_update_index_in_dim(land, src, me, 0) for land, src in zip(lands, mine)]
            after = mine[0]
        if l + 1 < DEPTH:
            gather = gather_start(shards(l + 1), after, l + 1)
        lws.append(layer_operands(l, norm_g, conv_full, q_norm_g, k_norm_g, sinks, gate_b,
                                  {l: lands[0].reshape(IN_COLS, D)}, {l: lands[1].reshape(D, D)},
                                  {l: lands[2].reshape(D, D)}, {l: lands[3].reshape(D, D)}))
        u, hb = inproj_fwd(h, lws[l]["ng"] + gather["token"][0:1, 0:1], lws[l]["w_in"], f"inproj_fwd_{l}")
        x_in = h
        h, ya, yb = mixer_fwd(x_in, u, lws[l]["cw"], lws[l]["qg"], lws[l]["kg"], lws[l]["sinks"], lws[l]["gb"],
                              lws[l]["wco"], lws[l]["wao"], lws[l]["wout"], c, f"mixer_fwd_{l}")
        saved.append((x_in, u, hb, ya, yb))
    dh, loss_part = loss_head(h, loss_target[0], "loss_head")

    grads, scatters = [None] * DEPTH, [[] for _ in range(DEPTH)]
    for l in reversed(range(DEPTH)):
        def send_off(g, done, l=l):
            rest = [g[k].reshape(N_DEV, SHARD_ROWS, D) for k in ("wco", "wao", "wout")] if done else []
            first = [g["w_in"].reshape(N_DEV, SHARD_COLS, D)] if done == (l > 0) else []
            if not first + rest:
                return None
            tag = f"{l}" if l > 0 else ("0_rest" if done else "0_in")
            scatters[l].append(exchange_start(first + rest, True, ALL_PEERS, g["small"], f"scatter_start_{tag}"))
            return scatters[l][-1]["token"]
        dh, grads[l] = layer_bwd(dh, saved[l], lws[l], c, l, send_off)

    me1 = me.astype(jnp.int32).reshape(1)
    mine, lands = {}, {}
    for l in (3, 2, 1):
        mine[l], lands[l] = exchange_wait(scatters[l][0], dh, f"scatter_wait_{l}")
    weights = [(w_in_t, m_w_in_t, v_w_in_t, "w_in"), (w_conv_out, m_w_conv_out, v_w_conv_out, "w_conv_out"),
               (w_attn_out, m_w_attn_out, v_w_attn_out, "w_attn_out"), (w_out, m_w_out, v_w_out, "w_out")]
    upd = [adamw_layers(w, m, v, [lands[l][i] for l in (1, 2, 3)], [mine[l][i] for l in (1, 2, 3)], me1, 1, None,
                        f"adamw_{n}_upper") for i, (w, m, v, n) in enumerate(weights)]
    m_in, l_in = exchange_wait(scatters[0][0], upd[0][0], "scatter_wait_0_in")
    m_rest, l_rest = exchange_wait(scatters[0][1], upd[3][0], "scatter_wait_0_rest")
    mine[0], lands[0] = list(m_in) + list(m_rest), list(l_in) + list(l_rest)
    upd = [adamw_layers(w, m, v, [lands[0][i]], [mine[0][i]], me1, 0, upd[i], f"adamw_{n}_0")
           for i, (w, m, v, n) in enumerate(weights)]
    u_in = [jnp.swapaxes(o, 1, 2) for o in upd[0]]
    u_co, u_ao, u_out = upd[1], upd[2], upd[3]

    blocks = []
    for l in range(DEPTH):
        blk = grads[l]["small"]
        blk = blk.at[SM_NORM, 0:D].set(grads[l]["dng"][0])
        if l == 0:
            blk = blk.at[SM_LOSS, 0:128].set(loss_part[0])
        blocks.append(blk)
    gathered = all_gather([jnp.concatenate(blocks, axis=0)], u_out[0], "gather_small")[0]
    tot = small_sum(gathered, c["fold"], "small_sum")
    tot = tot.reshape(DEPTH, SM_ROWS, 2 * D)
    loss = tot[0, SM_LOSS, 0]

    def update_small(w, m, v, g, name):
        return adamw(w, m, v, g[None], name)

    u_ng = update_small(norm_g, m_norm_g, v_norm_g, tot[:, SM_NORM, 0:D], "adamw_norm_g")
    u_qg = update_small(q_norm_g, m_q_norm_g, v_q_norm_g, tot[:, 9, 0:HEAD], "adamw_q_norm_g")
    u_kg = update_small(k_norm_g, m_k_norm_g, v_k_norm_g, tot[:, 10, 0:HEAD], "adamw_k_norm_g")
    u_sk = update_small(sinks, m_sinks, v_sinks, tot[:, SM_SINK, 0:16], "adamw_sinks")
    u_gb = update_small(gate_b, m_gate_b, v_gate_b, tot[:, SM_GATE, :], "adamw_gate_b")
    g_conv = lax.dynamic_slice_in_dim(tot[:, SM_CONV:SM_CONV + 3, 0:D], me * SHARD_ROWS, SHARD_ROWS, axis=2)
    u_cw = [o.reshape(DEPTH, 3, SHARD_ROWS) for o in update_small(
        conv_w.reshape(DEPTH * 3, SHARD_ROWS), m_conv_w.reshape(DEPTH * 3, SHARD_ROWS),
        v_conv_w.reshape(DEPTH * 3, SHARD_ROWS), g_conv.reshape(DEPTH * 3, SHARD_ROWS), "adamw_conv_w")]

    order = [u_ng, u_in, u_cw, u_qg, u_kg, u_sk, u_co, u_ao, u_gb, u_out]
    return (loss, dh[None], *[u[0] for u in order], *[u[1] for u in order], *[u[2] for u in order], *[u[3] for u in order])
```

```python
import functools

import jax
import jax.numpy as jnp
from jax import lax
from jax.experimental import pallas as pl
from jax.experimental.pallas import tpu as pltpu

F32 = jnp.float32
BF16 = jnp.bfloat16

N_DEV = 8
DEPTH = 4
D = 1024
N_KV = 4
GROUP = 4
HEAD = 64
BLK = 128
KVW = N_KV * HEAD
IN_COLS = 8704
SHARD_COLS = IN_COLS // N_DEV
SHARD_ROWS = D // N_DEV
C_VC, C_BC, C_CC, C_ZC, C_Q, C_K, C_V, C_ZA, C_GA, C_GB = 0, 1024, 2048, 3072, 4096, 5120, 5376, 5632, 6656, 7680
EPS = 1e-6
NEG_INF = -1e30
SCALE = HEAD ** -0.5

ADAM_LR = 0.001
ADAM_B1 = 0.9
ADAM_B2 = 0.999
ADAM_EPS = 1e-08
ADAM_WD = 0.01
ADAM_STEP = 10

VMEM_LIMIT = 60 * 1024 * 1024
SM_ROWS = 16
SM_GATE, SM_CONV, SM_QG, SM_KG, SM_SINK, SM_NORM, SM_LOSS = 0, 1, 4, 5, 6, 7, 8


def _cparams(sem):
    return pltpu.CompilerParams(dimension_semantics=sem, vmem_limit_bytes=VMEM_LIMIT)


def _dot(a, b):
    return jnp.dot(a, b, preferred_element_type=F32)


def _dot_nt(a, b):
    return lax.dot_general(a, b, (((1,), (1,)), ((), ())), preferred_element_type=F32)


def _dot_tn(a, b):
    return lax.dot_general(a, b, (((0,), (0,)), ((), ())), preferred_element_type=F32)


def _dot2(x, sel):
    hi = x.astype(BF16)
    lo = (x - hi.astype(F32)).astype(BF16)
    return _dot(hi, sel) + _dot(lo, sel)


def _sigmoid(z):
    return 0.5 * jnp.tanh(0.5 * z) + 0.5


def _head_mean(t, sel, exp):
    return _dot2(_dot(t.astype(BF16), sel) * (1.0 / HEAD), exp)


def _shift_down(a, k, before):
    r = pltpu.roll(a, k, 0)
    row = lax.broadcasted_iota(jnp.int32, (8, 1), 0)
    head = jnp.where(row < k, pltpu.roll(before, k, 0), r[0:8, :])
    return jnp.concatenate([head, r[8:, :]], axis=0)


def _shift_up(a, k, after):
    n = a.shape[0]
    r = pltpu.roll(a, n - k, 0)
    row = lax.broadcasted_iota(jnp.int32, (8, 1), 0)
    tail = jnp.where(row >= 8 - k, pltpu.roll(after, 8 - k, 0), r[n - 8:, :])
    return jnp.concatenate([r[:n - 8, :], tail], axis=0)


def _bf(ref, c0, width):
    return ref[:, c0:c0 + width].astype(F32)


def _selectors():
    c = jnp.arange(D)
    sel_q = (c[:, None] // HEAD == jnp.arange(128)[None, :]).astype(BF16)
    ck = jnp.arange(KVW)
    sel_k = (ck[:, None] // HEAD == jnp.arange(128)[None, :]).astype(BF16)
    src = jnp.arange(KVW)[:, None]
    dst = jnp.arange(KVW)[None, :]
    rep = jnp.stack([((src // HEAD == h) & (src % HEAD == dst % HEAD)).astype(BF16) for h in range(N_KV)])
    fold = (c[:, None] % HEAD == jnp.arange(128)[None, :]).astype(BF16)
    return dict(sel_q=sel_q, exp_q=sel_q.T, sel_k=sel_k, exp_k=sel_k.T, rep=rep, rep_t=jnp.swapaxes(rep, 1, 2), fold=fold)


def inproj_fwd(x, ng, w, name):
    t = x.shape[0]
    tm = min(1024, t)
    cb = 2176
    def body(x_ref, ng_ref, w_ref, u_ref, h_ref, h_scr):
        @pl.when(pl.program_id(1) == 0)
        def _():
            xf = x_ref[...]
            r = lax.rsqrt(jnp.mean(xf * xf, axis=-1, keepdims=True) + EPS)
            hb = (xf * r * ng_ref[...]).astype(BF16)
            h_scr[...] = hb
            h_ref[...] = hb
        u_ref[...] = _dot_nt(h_scr[...], w_ref[...]).astype(BF16)

    return pl.pallas_call(
        body, name=name, grid=(t // tm, IN_COLS // cb),
        in_specs=[pl.BlockSpec((tm, D), lambda i, j: (i, 0)), pl.BlockSpec((1, D), lambda i, j: (0, 0)),
                  pl.BlockSpec((cb, D), lambda i, j: (j, 0))],
        out_specs=[pl.BlockSpec((tm, cb), lambda i, j: (i, j)), pl.BlockSpec((tm, D), lambda i, j: (i, 0))],
        out_shape=[jax.ShapeDtypeStruct((t, IN_COLS), BF16), jax.ShapeDtypeStruct((t, D), BF16)],
        scratch_shapes=[pltpu.VMEM((tm, D), BF16)],
        compiler_params=_cparams(("arbitrary", "arbitrary")),
    )(x, ng, w)


def _conv_fwd(u_ref, uvc_prev, ucc_prev, cw_ref, is_first, tm):
    p = _bf(u_ref, C_CC, D) * _bf(u_ref, C_VC, D)
    pprev = ucc_prev[...].astype(F32) * uvc_prev[...].astype(F32)
    pprev = jnp.where(is_first, 0.0, pprev)
    p1 = _shift_down(p, 1, pprev[8:16, :])
    p2 = _shift_down(p, 2, pprev[8:16, :])
    cw = cw_ref[...]
    conv = cw[0:1, :] * p2 + cw[1:2, :] * p1 + cw[2:3, :] * p
    return p, p1, p2, conv


def _attn_inputs(u_ref, ukv_prev, qg_ref, kg_ref, c, tm):
    q = _bf(u_ref, C_Q, D)
    rq = lax.rsqrt(_head_mean(q * q, c["sel_q"][...], c["exp_q"][...]) + EPS)
    qhat = q * rq
    qn = (qhat * qg_ref[...]).astype(BF16)
    kband = jnp.concatenate([ukv_prev[:, 0:KVW].astype(F32), _bf(u_ref, C_K, KVW)], axis=0)
    rk = lax.rsqrt(_head_mean(kband * kband, c["sel_k"][...], c["exp_k"][...]) + EPS)
    khat = kband * rk
    knb = (khat * kg_ref[...]).astype(BF16)
    vband = jnp.concatenate([ukv_prev[:, KVW:2 * KVW], u_ref[:, C_V:C_V + KVW]], axis=0)
    kt = [_dot(knb, c["rep"][h]).astype(BF16) for h in range(N_KV)]
    vt = [_dot(vband, c["rep"][h]).astype(BF16) for h in range(N_KV)]
    return qhat, rq, qn, khat, rk, kt, vt


def _attn_masks(is_first):
    rows = GROUP * BLK
    r = lax.broadcasted_iota(jnp.int32, (rows, 2 * BLK), 0)
    kk = lax.broadcasted_iota(jnp.int32, (rows, 2 * BLK), 1)
    qq = r % BLK
    valid = (kk > qq) & (kk <= qq + BLK)
    valid_first = valid & ((kk >= BLK) | jnp.logical_not(is_first))
    lane_grp = lax.broadcasted_iota(jnp.int32, (BLK, KVW), 1) // HEAD
    row_grp = lax.broadcasted_iota(jnp.int32, (rows, 1), 0) // BLK
    return valid, valid_first, lane_grp, row_grp


def _sink_col(sinks_ref, h, row_grp):
    col = jnp.full(row_grp.shape, sinks_ref[0, GROUP * h], F32)
    for gi in range(1, GROUP):
        col = jnp.where(row_grp == gi, sinks_ref[0, GROUP * h + gi], col)
    return col


def _stack_groups(a256, lane_grp):
    zero = jnp.zeros_like(a256)
    return jnp.concatenate([jnp.where(lane_grp == gi, a256, zero) for gi in range(GROUP)], axis=0)


def _unstack_groups(a4, lane_grp):
    out = jnp.where(lane_grp == 0, a4[0:BLK], 0.0)
    for gi in range(1, GROUP):
        out = out + jnp.where(lane_grp == gi, a4[gi * BLK:(gi + 1) * BLK], 0.0)
    return out


def _band_sum(parts):
    pieces = [parts[0][0:BLK, :]]
    for b in range(1, len(parts)):
        pieces.append(parts[b - 1][BLK:, :] + parts[b][0:BLK, :])
    pieces.append(parts[-1][BLK:, :])
    return jnp.concatenate(pieces, axis=0)


def _softmax_block(qs, kt_b, valid, sink):
    s = _dot_nt(qs, kt_b)
    s = jnp.where(valid, s, NEG_INF)
    m = jnp.maximum(jnp.max(s, axis=-1, keepdims=True), sink)
    e = jnp.exp(s - m)
    es = jnp.exp(sink - m)
    inv = 1.0 / (jnp.sum(e, axis=-1, keepdims=True) + es)
    return e * inv, es * inv


def _mixer_specs(t, tm, n_tiles, tile_of):
    nb = tm // BLK
    u_spec = pl.BlockSpec((tm, IN_COLS), lambda g: (tile_of(g), 0))
    ukv_prev = pl.BlockSpec((BLK, 2 * KVW), lambda g: (jnp.maximum(tile_of(g) * nb - 1, 0), C_K // (2 * KVW)))
    uvc_prev = pl.BlockSpec((16, D), lambda g: (jnp.maximum(tile_of(g) * (tm // 16) - 1, 0), C_VC // D))
    ucc_prev = pl.BlockSpec((16, D), lambda g: (jnp.maximum(tile_of(g) * (tm // 16) - 1, 0), C_CC // D))
    return u_spec, ukv_prev, uvc_prev, ucc_prev


def _full(shape):
    n = len(shape)
    return pl.BlockSpec(shape, lambda g: (0,) * n)


def mixer_fwd(x, u, cw, qg, kg, sinks, gb, wco, wao, wout, c, name):
    t = x.shape[0]
    tm = min(256, t)
    n_tiles = t // tm
    nb = tm // BLK
    cn = sorted(c)

    def body(x_ref, u_ref, ukv_prev, uvc_prev, ucc_prev, cw_ref, qg_ref, kg_ref, sinks_ref, gb_ref, wco_ref, wao_ref,
             wout_ref, *rest):
        cref = dict(zip(cn, rest[:len(cn)]))
        xo_ref, ya_ref, yb_ref = rest[len(cn):]
        is_first = pl.program_id(0) == 0
        _, _, _, conv = _conv_fwd(u_ref, uvc_prev, ucc_prev, cw_ref, is_first, tm)
        zc = _bf(u_ref, C_ZC, D)
        yc = _bf(u_ref, C_BC, D) * conv * (zc * _sigmoid(zc))
        ya = _dot(yc.astype(BF16), wco_ref[...])

        _, _, qn, _, _, kt, vt = _attn_inputs(u_ref, ukv_prev, qg_ref, kg_ref, cref, tm)
        valid, valid_first, lane_grp, row_grp = _attn_masks(is_first)
        o_cols = []
        for h in range(N_KV):
            sink = _sink_col(sinks_ref, h, row_grp)
            o_rows = []
            for b in range(nb):
                qs = _stack_groups(qn[b * BLK:(b + 1) * BLK, h * KVW:(h + 1) * KVW], lane_grp)
                pn, _ = _softmax_block(qs, kt[h][b * BLK:(b + 2) * BLK], valid_first if b == 0 else valid, sink)
                o4 = _dot(pn.astype(BF16), vt[h][b * BLK:(b + 2) * BLK])
                o_rows.append(_unstack_groups(o4, lane_grp))
            o_cols.append(jnp.concatenate(o_rows, axis=0))
        za = _bf(u_ref, C_ZA, D)
        ob = jnp.concatenate(o_cols, axis=1) * (za * _sigmoid(za))
        yb = _dot(ob.astype(BF16), wao_ref[...])

        g_a = _sigmoid(_bf(u_ref, C_GA, D) + gb_ref[:, 0:D])
        g_b = _sigmoid(_bf(u_ref, C_GB, D) + gb_ref[:, D:2 * D])
        merged = g_a * ya + g_b * yb
        xo_ref[...] = x_ref[...] + _dot(merged.astype(BF16), wout_ref[...])
        ya_ref[...] = ya.astype(BF16)
        yb_ref[...] = yb.astype(BF16)

    u_spec, ukv_prev, uvc_prev, ucc_prev = _mixer_specs(t, tm, n_tiles, lambda g: g)
    tok = pl.BlockSpec((tm, D), lambda g: (g, 0))
    consts = [c[k] for k in cn]
    return pl.pallas_call(
        body, name=name, grid=(n_tiles,),
        in_specs=[tok, u_spec, ukv_prev, uvc_prev, ucc_prev, _full((8, D)), _full((1, D)), _full((1, KVW)),
                  pl.BlockSpec(memory_space=pltpu.SMEM), _full((1, 2 * D)), _full((D, D)), _full((D, D)), _full((D, D))]
                 + [_full(a.shape) for a in consts],
        out_specs=[tok, tok, tok],
        out_shape=[jax.ShapeDtypeStruct((t, D), F32)] + [jax.ShapeDtypeStruct((t, D), BF16)] * 2,
        compiler_params=_cparams(("arbitrary",)),
    )(x, u, u, u, u, cw, qg, kg, sinks, gb, wco, wao, wout, *consts)


def mixer_bwd(dout, u, ya, yb, cw, qg, kg, sinks, gb, wco, wao, wout, c, name):
    t = dout.shape[0]
    tm = min(256, t)
    n_tiles = t // tm
    nb = tm // BLK
    kb = tm + BLK
    cn = sorted(c)

    def body(dout_ref, u_ref, ukv_prev, uvc_prev, ucc_prev, ya_ref, yb_ref, cw_ref, qg_ref, kg_ref, sinks_ref, gb_ref,
             wco_ref, wao_ref, wout_ref, *rest):
        cref = dict(zip(cn, rest[:len(cn)]))
        (du_ref, small_ref, merged_ref, yc_ref, ob_ref, dya_ref, dyb_ref, carry_kv, carry_conv) = rest[len(cn):]
        g = pl.program_id(0)
        is_first = g == n_tiles - 1

        @pl.when(g == 0)
        def _():
            carry_kv[...] = jnp.zeros_like(carry_kv)
            carry_conv[...] = jnp.zeros_like(carry_conv)
            small_ref[...] = jnp.zeros_like(small_ref)

        dout = dout_ref[...]
        dout_b = dout.astype(BF16)
        ya_v = ya_ref[...].astype(F32)
        yb_v = yb_ref[...].astype(F32)
        g_a = _sigmoid(_bf(u_ref, C_GA, D) + gb_ref[:, 0:D])
        g_b = _sigmoid(_bf(u_ref, C_GB, D) + gb_ref[:, D:2 * D])
        merged = g_a * ya_v + g_b * yb_v
        dmerged = _dot_nt(dout_b, wout_ref[...])
        merged_ref[...] = merged.astype(BF16)
        dya = dmerged * g_a
        dyb = dmerged * g_b
        dgl_a = dmerged * ya_v * g_a * (1.0 - g_a)
        dgl_b = dmerged * yb_v * g_b * (1.0 - g_b)
        du_ref[:, C_GA:C_GA + D] = dgl_a.astype(BF16)
        du_ref[:, C_GB:C_GB + D] = dgl_b.astype(BF16)
        small_ref[SM_GATE:SM_GATE + 1, 0:D] += jnp.sum(dgl_a, axis=0, keepdims=True)
        small_ref[SM_GATE:SM_GATE + 1, D:2 * D] += jnp.sum(dgl_b, axis=0, keepdims=True)

        p, p1, p2, conv = _conv_fwd(u_ref, uvc_prev, ucc_prev, cw_ref, is_first, tm)
        zc = _bf(u_ref, C_ZC, D)
        bc = _bf(u_ref, C_BC, D)
        sg = _sigmoid(zc)
        sc = zc * sg
        yc = bc * conv * sc
        dya_b = dya.astype(BF16)
        yc_ref[...] = yc.astype(BF16)
        dya_ref[...] = dya_b
        dyc = _dot_nt(dya_b, wco_ref[...])
        du_ref[:, C_BC:C_BC + D] = (dyc * conv * sc).astype(BF16)
        du_ref[:, C_ZC:C_ZC + D] = (dyc * bc * conv * (sg * (1.0 + zc * (1.0 - sg)))).astype(BF16)
        dconv = dyc * bc * sc
        small_ref[SM_CONV + 2:SM_CONV + 3, 0:D] += jnp.sum(dconv * p, axis=0, keepdims=True)
        small_ref[SM_CONV + 1:SM_CONV + 2, 0:D] += jnp.sum(dconv * p1, axis=0, keepdims=True)
        small_ref[SM_CONV:SM_CONV + 1, 0:D] += jnp.sum(dconv * p2, axis=0, keepdims=True)
        nxt = carry_conv[...]
        d1 = _shift_up(dconv, 1, nxt)
        d2 = _shift_up(dconv, 2, nxt)
        carry_conv[...] = dconv[0:8, :]
        cw = cw_ref[...]
        dp = cw[2:3, :] * dconv + cw[1:2, :] * d1 + cw[0:1, :] * d2
        du_ref[:, C_CC:C_CC + D] = (dp * _bf(u_ref, C_VC, D)).astype(BF16)
        du_ref[:, C_VC:C_VC + D] = (dp * _bf(u_ref, C_CC, D)).astype(BF16)

        dyb_b = dyb.astype(BF16)
        dob = _dot_nt(dyb_b, wao_ref[...])
        za = _bf(u_ref, C_ZA, D)
        sga = _sigmoid(za)
        sa = za * sga
        do = dob * sa
        qhat, rq, qn, khat, rk, kt, vt = _attn_inputs(u_ref, ukv_prev, qg_ref, kg_ref, cref, tm)
        valid, valid_first, lane_grp, row_grp = _attn_masks(is_first)
        lane16 = lax.broadcasted_iota(jnp.int32, (1, 2 * D), 1)
        dsink_row = jnp.zeros((1, 2 * D), F32)
        o_cols, dq_cols, dk4, dv4 = [], [], [], []
        for h in range(N_KV):
            sink = _sink_col(sinks_ref, h, row_grp)
            dsink_col = jnp.zeros((GROUP * BLK, 1), F32)
            o_rows, dq_rows, dk_parts, dv_parts = [], [], [], []
            for b in range(nb):
                rows = slice(b * BLK, (b + 1) * BLK)
                band = slice(b * BLK, (b + 2) * BLK)
                cols = slice(h * KVW, (h + 1) * KVW)
                qs = _stack_groups(qn[rows, cols], lane_grp)
                pn, ps = _softmax_block(qs, kt[h][band], valid_first if b == 0 else valid, sink)
                pn_b = pn.astype(BF16)
                o_rows.append(_unstack_groups(_dot(pn_b, vt[h][band]), lane_grp))
                dos = _stack_groups(do[rows, cols], lane_grp).astype(BF16)
                dpn = _dot_nt(dos, vt[h][band])
                delta = jnp.sum(pn * dpn, axis=-1, keepdims=True)
                ds = (pn * (dpn - delta)).astype(BF16)
                dsink_col = dsink_col - ps * delta
                dq_rows.append(_unstack_groups(_dot(ds, kt[h][band]), lane_grp))
                dk_parts.append(_dot_tn(ds, qs))
                dv_parts.append(_dot_tn(pn_b, dos))
            o_cols.append(jnp.concatenate(o_rows, axis=0))
            dq_cols.append(jnp.concatenate(dq_rows, axis=0))
            dk4.append(_band_sum(dk_parts))
            dv4.append(_band_sum(dv_parts))
            for gi in range(GROUP):
                tot = jnp.sum(dsink_col[gi * BLK:(gi + 1) * BLK, :], axis=0, keepdims=True)
                dsink_row = dsink_row + jnp.where(lane16 == GROUP * h + gi, tot, 0.0)
        small_ref[SM_SINK:SM_SINK + 1, :] += dsink_row

        o = jnp.concatenate(o_cols, axis=1)
        ob_ref[...] = (o * sa).astype(BF16)
        dyb_ref[...] = dyb_b
        du_ref[:, C_ZA:C_ZA + D] = (dob * o * (sga * (1.0 + za * (1.0 - sga)))).astype(BF16)

        dqn = jnp.concatenate(dq_cols, axis=1)
        small_ref[SM_QG:SM_QG + 1, 0:D] += SCALE * jnp.sum(dqn * qhat, axis=0, keepdims=True)
        dqh = dqn * qg_ref[...]
        dq = rq * (dqh - qhat * _head_mean(dqh * qhat, cref["sel_q"][...], cref["exp_q"][...]))
        du_ref[:, C_Q:C_Q + D] = dq.astype(BF16)

        dkn_band = jnp.zeros((kb, KVW), F32)
        dv_band = jnp.zeros((kb, KVW), F32)
        for h in range(N_KV):
            dkn_band = dkn_band + _dot2(dk4[h], cref["rep_t"][h])
            dv_band = dv_band + _dot2(dv4[h], cref["rep_t"][h])
        carried = carry_kv[...]
        pad = jnp.zeros((tm - BLK, KVW), F32)
        if nb > 1:
            dkn = dkn_band[BLK:, :] + jnp.concatenate([pad, carried[:, 0:KVW]], axis=0)
            dv = dv_band[BLK:, :] + jnp.concatenate([pad, carried[:, KVW:2 * KVW]], axis=0)
        else:
            dkn = dkn_band[BLK:, :] + carried[:, 0:KVW]
            dv = dv_band[BLK:, :] + carried[:, KVW:2 * KVW]
        carry_kv[:, 0:KVW] = dkn_band[0:BLK, :]
        carry_kv[:, KVW:2 * KVW] = dv_band[0:BLK, :]
        khat_t = khat[BLK:, :]
        small_ref[SM_KG:SM_KG + 1, 0:KVW] += jnp.sum(dkn * khat_t, axis=0, keepdims=True)
        dkh = dkn * kg_ref[...]
        dk = rk[BLK:, :] * (dkh - khat_t * _head_mean(dkh * khat_t, cref["sel_k"][...], cref["exp_k"][...]))
        du_ref[:, C_K:C_K + KVW] = dk.astype(BF16)
        du_ref[:, C_V:C_V + KVW] = dv.astype(BF16)

    rev = lambda g: n_tiles - 1 - g
    u_spec, ukv_prev, uvc_prev, ucc_prev = _mixer_specs(t, tm, n_tiles, rev)
    tok = pl.BlockSpec((tm, D), lambda g: (rev(g), 0))
    consts = [c[k] for k in cn]
    wspec = _full((D, D))
    return pl.pallas_call(
        body, name=name, grid=(n_tiles,),
        in_specs=[tok, u_spec, ukv_prev, uvc_prev, ucc_prev, tok, tok, _full((8, D)), _full((1, D)), _full((1, KVW)),
                  pl.BlockSpec(memory_space=pltpu.SMEM), _full((1, 2 * D)), wspec, wspec, wspec]
                 + [_full(a.shape) for a in consts],
        out_specs=[pl.BlockSpec((tm, IN_COLS), lambda g: (rev(g), 0)), _full((SM_ROWS, 2 * D))] + [tok] * 5,
        out_shape=[jax.ShapeDtypeStruct((t, IN_COLS), BF16), jax.ShapeDtypeStruct((SM_ROWS, 2 * D), F32)]
                  + [jax.ShapeDtypeStruct((t, D), BF16)] * 5,
        scratch_shapes=[pltpu.VMEM((BLK, 2 * KVW), F32), pltpu.VMEM((8, D), F32)],
        compiler_params=_cparams(("arbitrary",)),
    )(dout, u, u, u, u, ya, yb, cw, qg, kg, sinks, gb, wco, wao, wout, *consts)


def matmul_tn(a, b, name, after=None):
    t, m = a.shape
    tk = min(1024, t)
    mb = 2176 if m == IN_COLS else m
    nk = t // tk

    def body(a_ref, b_ref, *rest):
        o_ref, acc = rest[-2:]
        k = pl.program_id(1)
        prod = _dot_tn(a_ref[...].astype(BF16), b_ref[...].astype(BF16))

        @pl.when(k == 0)
        def _():
            acc[...] = prod

        @pl.when(k > 0)
        def _():
            acc[...] += prod

        @pl.when(k == nk - 1)
        def _():
            o_ref[...] = acc[...].astype(BF16)

    return pl.pallas_call(
        body, name=name, grid=(m // mb, nk),
        in_specs=[pl.BlockSpec((tk, mb), lambda j, k: (k, j)), pl.BlockSpec((tk, D), lambda j, k: (k, 0))]
                 + ([] if after is None else [ANY]),
        out_specs=pl.BlockSpec((mb, D), lambda j, k: (j, 0)),
        out_shape=jax.ShapeDtypeStruct((m, D), BF16),
        scratch_shapes=[pltpu.VMEM((mb, D), F32)],
        compiler_params=_cparams(("arbitrary", "arbitrary")),
    )(a, b, *([] if after is None else [after]))


def inproj_bwd_x(du, w, x, ng, dout, name):
    t = x.shape[0]
    tm = min(1024, t)
    kc = 2176
    nk = IN_COLS // kc

    def body(du_ref, w_ref, x_ref, ng_ref, dout_ref, dx_ref, dng_ref, acc):
        i = pl.program_id(0)
        k = pl.program_id(1)
        prod = _dot(du_ref[...], w_ref[...])

        @pl.when(k == 0)
        def _():
            acc[...] = prod

        @pl.when(k > 0)
        def _():
            acc[...] += prod

        @pl.when((i == 0) & (k == 0))
        def _():
            dng_ref[...] = jnp.zeros_like(dng_ref)

        @pl.when(k == nk - 1)
        def _():
            dh = acc[...]
            xf = x_ref[...]
            r = lax.rsqrt(jnp.mean(xf * xf, axis=-1, keepdims=True) + EPS)
            xhat = xf * r
            dng_ref[0:1, :] += jnp.sum(dh * xhat, axis=0, keepdims=True)
            dxh = dh * ng_ref[...]
            dx_ref[...] = dout_ref[...] + r * (dxh - xhat * jnp.mean(dxh * xhat, axis=-1, keepdims=True))

    tok = pl.BlockSpec((tm, D), lambda i, k: (i, 0))
    return pl.pallas_call(
        body, name=name, grid=(t // tm, nk),
        in_specs=[pl.BlockSpec((tm, kc), lambda i, k: (i, k)), pl.BlockSpec((kc, D), lambda i, k: (k, 0)), tok,
                  pl.BlockSpec((1, D), lambda i, k: (0, 0)), tok],
        out_specs=[tok, pl.BlockSpec((8, D), lambda i, k: (0, 0))],
        out_shape=[jax.ShapeDtypeStruct((t, D), F32), jax.ShapeDtypeStruct((8, D), F32)],
        scratch_shapes=[pltpu.VMEM((tm, D), F32)],
        compiler_params=_cparams(("arbitrary", "arbitrary")),
    )(du, w, x, ng, dout)


def loss_head(y, target, name):
    t = y.shape[0]
    tm = min(1024, t)

    def body(y_ref, t_ref, dy_ref, loss_ref):
        @pl.when(pl.program_id(0) == 0)
        def _():
            loss_ref[...] = jnp.zeros_like(loss_ref)
        err = y_ref[...] - t_ref[...]
        dy_ref[...] = err * (1.0 / D)
        part = jnp.sum(jnp.sum(err * err, axis=-1, keepdims=True) * (1.0 / D), axis=0, keepdims=True)
        loss_ref[...] += 0.5 * part

    tok = pl.BlockSpec((tm, D), lambda i: (i, 0))
    return pl.pallas_call(
        body, name=name, grid=(t // tm,), in_specs=[tok, tok],
        out_specs=[tok, pl.BlockSpec((8, 128), lambda i: (0, 0))],
        out_shape=[jax.ShapeDtypeStruct((t, D), F32), jax.ShapeDtypeStruct((8, 128), F32)],
        compiler_params=_cparams(("arbitrary",)),
    )(y, target)


def layer_operands(l, norm_g, conv_w_full, q_norm_g, k_norm_g, sinks, gate_b, w_in_b, wco_b, wao_b, wout_b):
    return dict(
        ng=norm_g[l][None, :], cw=jnp.pad(conv_w_full[l], ((0, 5), (0, 0))),
        qg=jnp.tile(q_norm_g[l] * SCALE, D // HEAD)[None, :], kg=jnp.tile(k_norm_g[l], N_KV)[None, :],
        sinks=sinks[l][None, :], gb=gate_b[l][None, :],
        w_in=w_in_b[l], wco=wco_b[l], wao=wao_b[l], wout=wout_b[l])


def layer_bwd(dout, saved, lw, c, l, send_off):
    x, u, h, ya, yb = saved
    du, small, merged, yc, ob, dya, dyb = mixer_bwd(dout, u, ya, yb, lw["cw"], lw["qg"], lw["kg"], lw["sinks"], lw["gb"],
                                                    lw["wco"], lw["wao"], lw["wout"], c, f"mixer_bwd_{l}")
    grads = dict(w_in=matmul_tn(du, h, f"dw_in_{l}"), small=small)
    token = send_off(grads, False)
    grads["wout"] = matmul_tn(merged, dout, f"dw_out_{l}", after=token)
    grads["wco"] = matmul_tn(yc, dya, f"dw_conv_out_{l}")
    grads["wao"] = matmul_tn(ob, dyb, f"dw_attn_out_{l}")
    token = send_off(grads, True)
    dx, grads["dng"] = inproj_bwd_x(du, lw["w_in"], x, lw["ng"] + token[0:1, 0:1], dout, f"inproj_bwd_{l}")
    return dx, grads


MESH = pl.DeviceIdType.MESH
ANY = pl.BlockSpec(memory_space=pl.ANY)


def _place():
    return lax.axis_index("x"), lax.axis_index("y"), lax.axis_index("c")


def all_gather(arrs, after, name):
    n = len(arrs)

    def body(*refs):
        ins, outs = refs[:n], refs[n + 1:2 * n + 1]
        send_sems, recv_sems, local_sems = refs[2 * n + 1:]
        x, y, c = _place()
        me, sibling = (x, y, c), (x, y, 1 - c)
        chips = [(1 - x, y), (x, 1 - y), (1 - x, 1 - y)]

        def slot(a, block):
            px, py, pc = block
            return outs[a].at[4 * px + 2 * py + pc]

        def copy(a, k, block, to, src=None):
            return pltpu.make_async_remote_copy(
                src_ref=slot(a, block) if src is None else src, dst_ref=slot(a, block),
                send_sem=send_sems.at[a, k], recv_sem=recv_sems.at[a, k], device_id=to, device_id_type=MESH)

        mine = [pltpu.make_async_copy(ins[a], slot(a, me), local_sems.at[a]) for a in range(n)]
        for cp in mine:
            cp.start()
        first = []
        for a in range(n):
            first.append(copy(a, 0, me, sibling, src=ins[a]))
            first += [copy(a, 1 + j, me, (*chip, c), src=ins[a]) for j, chip in enumerate(chips)]
        for cp in first:
            cp.start()
        passed = []
        for j, chip in enumerate(chips):
            for a in range(n):
                copy(a, 1 + j, (*chip, c), me).wait_recv()
                passed.append(copy(a, 4 + j, (*chip, c), sibling))
                passed[-1].start()
        for a in range(n):
            copy(a, 0, sibling, me).wait_recv()
            for j, chip in enumerate(chips):
                copy(a, 4 + j, (*chip, 1 - c), me).wait_recv()
        for cp in first + passed:
            cp.wait_send()
        for cp in mine:
            cp.wait()

    return pl.pallas_call(
        body, name=name, in_specs=[ANY] * (n + 1), out_specs=[ANY] * n,
        out_shape=[jax.ShapeDtypeStruct((N_DEV,) + a.shape, a.dtype) for a in arrs],
        scratch_shapes=[pltpu.SemaphoreType.DMA((n, 7)), pltpu.SemaphoreType.DMA((n, 7)), pltpu.SemaphoreType.DMA((n,))],
    )(*arrs, after)


HBM_SPEC = pl.BlockSpec(memory_space=pltpu.HBM)
SEM_SPEC = pl.BlockSpec(memory_space=pltpu.SEMAPHORE)
EFFECT = pltpu.SideEffectType.DATAFLOW_SIDE_EFFECTING


ALL_PEERS = (1, 2, 3, 4, 5, 6, 7)


def _flip(place, k):
    x, y, c = place
    return (1 - x if k & 4 else x, 1 - y if k & 2 else y, 1 - c if k & 1 else c)


def _slot(place):
    return 4 * place[0] + 2 * place[1] + place[2]


def _exchange_copies(ins, lands, send_sems, recv_sems, scatter, flips, arriving):
    me = _place()
    out = []
    for a in range(len(ins)):
        for i, k in enumerate(flips):
            peer = _flip(me, k)
            out.append(pltpu.make_async_remote_copy(
                src_ref=ins[a].at[_slot(peer)] if scatter else ins[a],
                dst_ref=lands[a].at[_slot(peer) if arriving else _slot(me)],
                send_sem=send_sems.at[a * len(flips) + i], recv_sem=recv_sems.at[a * len(flips) + i],
                device_id=peer, device_id_type=MESH))
    return out


def exchange_start(arrs, scatter, flips, after, name):
    n = len(arrs)
    lands = [lax.empty(a.shape if scatter else (N_DEV,) + a.shape, a.dtype) for a in arrs]

    def body(*refs):
        ins, lz = refs[:n], refs[n:2 * n]
        send_sems, recv_sems = refs[2 * n + 1], refs[2 * n + 2]
        token = refs[-1]
        for cp in _exchange_copies(ins, lz, send_sems, recv_sems, scatter, flips, False):
            cp.start()
        token[...] = jnp.zeros_like(token)

    sems = pltpu.SemaphoreType.DMA((n * len(flips),))
    res = pl.pallas_call(
        body, name=name,
        out_shape=(sems, sems, *[pltpu.HBM(a.shape, a.dtype) for a in arrs], *[pltpu.HBM(a.shape, a.dtype) for a in lands],
                   jax.ShapeDtypeStruct((8, 128), F32)),
        in_specs=[HBM_SPEC] * (2 * n) + [ANY],
        out_specs=(SEM_SPEC, SEM_SPEC, *[HBM_SPEC] * (2 * n), pl.BlockSpec(memory_space=pltpu.VMEM)),
        input_output_aliases={i: 2 + i for i in range(2 * n)},
        compiler_params=pltpu.CompilerParams(has_side_effects=EFFECT),
    )(*[pltpu.with_memory_space_constraint(a, pltpu.HBM) for a in arrs],
      *[pltpu.with_memory_space_constraint(a, pltpu.HBM) for a in lands], after)
    return dict(send=res[0], recv=res[1], srcs=res[2:2 + n], lands=res[2 + n:2 + 2 * n], token=res[-1], scatter=scatter,
                flips=flips)


def exchange_wait(state, after, name):
    n = len(state["srcs"])

    def body(*refs):
        ins, lz = refs[:n], refs[n:2 * n]
        send_sems, recv_sems = refs[2 * n], refs[2 * n + 1]
        for cp in _exchange_copies(ins, lz, send_sems, recv_sems, state["scatter"], state["flips"], True):
            cp.wait_send()
            cp.wait_recv()

    both = list(state["srcs"]) + list(state["lands"])
    res = pl.pallas_call(
        body, name=name, out_shape=tuple(pltpu.HBM(a.shape, a.dtype) for a in both),
        in_specs=[HBM_SPEC] * (2 * n) + [SEM_SPEC, SEM_SPEC, ANY], out_specs=tuple([HBM_SPEC] * (2 * n)),
        input_output_aliases={i: i for i in range(2 * n)},
        compiler_params=pltpu.CompilerParams(has_side_effects=EFFECT),
    )(*both, state["send"], state["recv"], after)
    return res[:n], res[n:]


def adamw(w, m, v, parts, name):
    r, cdim = w.shape
    n_parts = parts.shape[0]
    rb = 256 if r % 256 == 0 else (SHARD_COLS // 4 if r == SHARD_COLS else r)

    def body(w_ref, m_ref, v_ref, p_ref, g_ref, d_ref, mo_ref, vo_ref):
        g = p_ref[0].astype(F32)
        for i in range(1, n_parts):
            g = g + p_ref[i].astype(F32)
        m_new = ADAM_B1 * m_ref[...] + (1.0 - ADAM_B1) * g
        v_new = ADAM_B2 * v_ref[...] + (1.0 - ADAM_B2) * (g * g)
        m_hat = m_new / (1.0 - ADAM_B1 ** ADAM_STEP)
        v_hat = v_new / (1.0 - ADAM_B2 ** ADAM_STEP)
        g_ref[...] = g
        d_ref[...] = -ADAM_LR * (m_hat / (jnp.sqrt(v_hat) + ADAM_EPS) + ADAM_WD * w_ref[...])
        mo_ref[...] = m_new
        vo_ref[...] = v_new

    blk = pl.BlockSpec((rb, cdim), lambda i: (i, 0))
    return pl.pallas_call(
        body, name=name, grid=(r // rb,),
        in_specs=[blk, blk, blk, pl.BlockSpec((n_parts, rb, cdim), lambda i: (0, i, 0))],
        out_specs=[blk] * 4, out_shape=[jax.ShapeDtypeStruct((r, cdim), F32)] * 4,
        compiler_params=_cparams(("arbitrary",)),
    )(w, m, v, parts)


def adamw_layers(w, m, v, lands, srcs, me, lo, prev, name):
    nl_all, r, cdim = w.shape
    nl = len(lands)
    rb = 256 if r % 256 == 0 else (SHARD_COLS // 4 if r == SHARD_COLS else r)
    nblk = r // rb
    n_prev = 0 if prev is None else 4

    def body(me_ref, w_ref, m_ref, v_ref, *rest):
        land_refs, src_refs = rest[:nl], rest[nl:2 * nl]
        g_ref, d_ref, mo_ref, vo_ref = rest[2 * nl + n_prev:]
        for k in range(nl):
            @pl.when(pl.program_id(0) == k)
            def _(k=k):
                own = src_refs[k][...].astype(F32)
                g = jnp.where(me_ref[0] == 0, own, land_refs[k][0].astype(F32))
                for i in range(1, N_DEV):
                    g = g + jnp.where(me_ref[0] == i, own, land_refs[k][i].astype(F32))
                m_new = ADAM_B1 * m_ref[...] + (1.0 - ADAM_B1) * g
                v_new = ADAM_B2 * v_ref[...] + (1.0 - ADAM_B2) * (g * g)
                m_hat = m_new / (1.0 - ADAM_B1 ** ADAM_STEP)
                v_hat = v_new / (1.0 - ADAM_B2 ** ADAM_STEP)
                g_ref[...] = g
                d_ref[...] = -ADAM_LR * (m_hat / (jnp.sqrt(v_hat) + ADAM_EPS) + ADAM_WD * w_ref[...])
                mo_ref[...] = m_new
                vo_ref[...] = v_new

    blk = pl.BlockSpec((None, rb, cdim), lambda l, i, me_ref: (lo + l, i, 0))

    def rows(l, i, k):
        return jnp.where(l < k, 0, jnp.where(l == k, i, nblk - 1))

    land_specs = [pl.BlockSpec((N_DEV, rb, cdim), lambda l, i, me_ref, k=k: (0, rows(l, i, k), 0)) for k in range(nl)]
    src_specs = [pl.BlockSpec((None, rb, cdim), lambda l, i, me_ref, k=k: (me_ref[0], rows(l, i, k), 0)) for k in range(nl)]
    return pl.pallas_call(
        body, name=name,
        grid_spec=pltpu.PrefetchScalarGridSpec(
            num_scalar_prefetch=1, grid=(nl, nblk),
            in_specs=[blk, blk, blk] + land_specs + src_specs + [ANY] * n_prev, out_specs=[blk] * 4),
        out_shape=[jax.ShapeDtypeStruct((nl_all, r, cdim), F32)] * 4,
        input_output_aliases={4 + 2 * nl + j: j for j in range(n_prev)},
        compiler_params=_cparams(("arbitrary", "arbitrary")),
    )(me, w, m, v, *lands, *srcs, *([] if prev is None else prev))


def small_sum(parts, fold, name):
    rows = parts.shape[1]

    def dot3(xv, sel):
        out = jnp.zeros((xv.shape[0], sel.shape[1]), F32)
        for _ in range(3):
            hi = xv.astype(BF16)
            out = out + _dot(hi, sel)
            xv = xv - hi.astype(F32)
        return out

    def body(p_ref, fold_ref, o_ref):
        tot = p_ref[0]
        for i in range(1, N_DEV):
            tot = tot + p_ref[i]
        o_ref[...] = tot
        for l in range(rows // SM_ROWS):
            blk = tot[l * SM_ROWS:l * SM_ROWS + 8, 0:D]
            folded = dot3(blk, fold_ref[...])
            o_ref[l * SM_ROWS + 9:l * SM_ROWS + 10, 0:128] = folded[SM_QG:SM_QG + 1, :]
            o_ref[l * SM_ROWS + 10:l * SM_ROWS + 11, 0:128] = folded[SM_KG:SM_KG + 1, :]

    return pl.pallas_call(
        body, name=name, out_shape=jax.ShapeDtypeStruct((rows, 2 * D), F32),
        compiler_params=_cparams(None),
    )(parts, fold)


def kernel(x, norm_g, w_in, conv_w, q_norm_g, k_norm_g, sinks, w_conv_out, w_attn_out, gate_b, w_out, loss_target, m_norm_g, m_w_in, m_conv_w, m_q_norm_g, m_k_norm_g, m_sinks, m_w_conv_out, m_w_attn_out, m_gate_b, m_w_out, v_norm_g, v_w_in, v_conv_w, v_q_norm_g, v_k_norm_g, v_sinks, v_w_conv_out, v_w_attn_out, v_gate_b, v_w_out):
    c = _selectors()
    me = 4 * lax.axis_index("x") + 2 * lax.axis_index("y") + lax.axis_index("c")

    w_in_t, m_w_in_t, v_w_in_t = (jnp.swapaxes(a, 1, 2) for a in (w_in, m_w_in, v_w_in))

    def shards(l):
        return [w_in_t[l].astype(BF16), w_conv_out[l].astype(BF16), w_attn_out[l].astype(BF16), w_out[l].astype(BF16)]

    def gather_start(arrs, after, tag):
        return exchange_start(arrs, False, ALL_PEERS, after, f"gather_start_{tag}")

    h = x[0]
    saved, lws = [], []
    gather = None
    for l in range(DEPTH):
        if l == 0:
            lands = all_gather(shards(0) + [conv_w], h, "gather_0")
            conv_full = jnp.transpose(lands[4], (1, 2, 0, 3)).reshape(DEPTH, 3, D)
            after = lands[1]
        else:
            mine, lands = exchange_wait(gather, h, f"gather_wait_{l}")
            lands = [lax.dynamic_update_index_in_dim(land, src, me, 0) for land, src in zip(lands, mine)]
            after = mine[0]
        if l + 1 < DEPTH:
            gather = gather_start(shards(l + 1), after, l + 1)
        lws.append(layer_operands(l, norm_g, conv_full, q_norm_g, k_norm_g, sinks, gate_b,
                                  {l: lands[0].reshape(IN_COLS, D)}, {l: lands[1].reshape(D, D)},
                                  {l: lands[2].reshape(D, D)}, {l: lands[3].reshape(D, D)}))
        u, hb = inproj_fwd(h, lws[l]["ng"] + gather["token"][0:1, 0:1], lws[l]["w_in"], f"inproj_fwd_{l}")
        x_in = h
        h, ya, yb = mixer_fwd(x_in, u, lws[l]["cw"], lws[l]["qg"], lws[l]["kg"], lws[l]["sinks"], lws[l]["gb"],
                              lws[l]["wco"], lws[l]["wao"], lws[l]["wout"], c, f"mixer_fwd_{l}")
        saved.append((x_in, u, hb, ya, yb))
    dh, loss_part = loss_head(h, loss_target[0], "loss_head")

    grads, scatters = [None] * DEPTH, [[] for _ in range(DEPTH)]
    for l in reversed(range(DEPTH)):
        def send_off(g, done, l=l):
            rest = [g[k].reshape(N_DEV, SHARD_ROWS, D) for k in ("wco", "wao", "wout")] if done else []
            first = [g["w_in"].reshape(N_DEV, SHARD_COLS, D)] if done == (l > 0) else []
            if not first + rest:
                return None
            tag = f"{l}" if l > 0 else ("0_rest" if done else "0_in")
            scatters[l].append(exchange_start(first + rest, True, ALL_PEERS, g["small"], f"scatter_start_{tag}"))
            return scatters[l][-1]["token"]
        dh, grads[l] = layer_bwd(dh, saved[l], lws[l], c, l, send_off)

    me1 = me.astype(jnp.int32).reshape(1)
    mine, lands = {}, {}
    for l in (3, 2, 1):
        mine[l], lands[l] = exchange_wait(scatters[l][0], dh, f"scatter_wait_{l}")
    weights = [(w_in_t, m_w_in_t, v_w_in_t, "w_in"), (w_conv_out, m_w_conv_out, v_w_conv_out, "w_conv_out"),
               (w_attn_out, m_w_attn_out, v_w_attn_out, "w_attn_out"), (w_out, m_w_out, v_w_out, "w_out")]
    upd = [adamw_layers(w, m, v, [lands[l][i] for l in (1, 2, 3)], [mine[l][i] for l in (1, 2, 3)], me1, 1, None,
                        f"adamw_{n}_upper") for i, (w, m, v, n) in enumerate(weights)]
    blocks = []
    for l in range(DEPTH):
        blk = grads[l]["small"]
        blk = blk.at[SM_NORM, 0:D].set(grads[l]["dng"][0])
        if l == 0:
            blk = blk.at[SM_LOSS, 0:128].set(loss_part[0])
        blocks.append(blk)
    small_x = exchange_start([jnp.concatenate(blocks, axis=0)], False, ALL_PEERS, grads[0]["dng"], "gather_small_start")

    m_in, l_in = exchange_wait(scatters[0][0], upd[0][0], "scatter_wait_0_in")
    upd[0] = adamw_layers(*weights[0][:3], [l_in[0]], [m_in[0]], me1, 0, upd[0], "adamw_w_in_0")
    m_rest, l_rest = exchange_wait(scatters[0][1], upd[0][0], "scatter_wait_0_rest")
    for i in (1, 2, 3):
        upd[i] = adamw_layers(*weights[i][:3], [l_rest[i - 1]], [m_rest[i - 1]], me1, 0, upd[i], f"adamw_{weights[i][3]}_0")
    u_in = [jnp.swapaxes(o, 1, 2) for o in upd[0]]
    u_co, u_ao, u_out = upd[1], upd[2], upd[3]

    mine_s, lands_s = exchange_wait(small_x, u_out[0], "gather_small_wait")
    gathered = lax.dynamic_update_index_in_dim(lands_s[0], mine_s[0], me, 0)
    tot = small_sum(gathered, c["fold"], "small_sum")
    tot = tot.reshape(DEPTH, SM_ROWS, 2 * D)
    loss = tot[0, SM_LOSS, 0]

    def update_small(w, m, v, g, name):
        return adamw(w, m, v, g[None], name)

    u_ng = update_small(norm_g, m_norm_g, v_norm_g, tot[:, SM_NORM, 0:D], "adamw_norm_g")
    u_qg = update_small(q_norm_g, m_q_norm_g, v_q_norm_g, tot[:, 9, 0:HEAD], "adamw_q_norm_g")
    u_kg = update_small(k_norm_g, m_k_norm_g, v_k_norm_g, tot[:, 10, 0:HEAD], "adamw_k_norm_g")
    u_sk = update_small(sinks, m_sinks, v_sinks, tot[:, SM_SINK, 0:16], "adamw_sinks")
    u_gb = update_small(gate_b, m_gate_b, v_gate_b, tot[:, SM_GATE, :], "adamw_gate_b")
    g_conv = lax.dynamic_slice_in_dim(tot[:, SM_CONV:SM_CONV + 3, 0:D], me * SHARD_ROWS, SHARD_ROWS, axis=2)
    u_cw = [o.reshape(DEPTH, 3, SHARD_ROWS) for o in update_small(
        conv_w.reshape(DEPTH * 3, SHARD_ROWS), m_conv_w.reshape(DEPTH * 3, SHARD_ROWS),
        v_conv_w.reshape(DEPTH * 3, SHARD_ROWS), g_conv.reshape(DEPTH * 3, SHARD_ROWS), "adamw_conv_w")]

    order = [u_ng, u_in, u_cw, u_qg, u_kg, u_sk, u_co, u_ao, u_gb, u_out]
    return (loss, dh[None], *[u[0] for u in order], *[u[1] for u in order], *[u[2] for u in order], *[u[3] for u in order])
```

```python
import functools

import jax
import jax.numpy as jnp
from jax import lax
from jax.experimental import pallas as pl
from jax.experimental.pallas import tpu as pltpu

F32 = jnp.float32
BF16 = jnp.bfloat16

N_DEV = 8
DEPTH = 4
D = 1024
N_KV = 4
GROUP = 4
HEAD = 64
BLK = 128
KVW = N_KV * HEAD
IN_COLS = 8704
SHARD_COLS = IN_COLS // N_DEV
SHARD_ROWS = D // N_DEV
C_VC, C_BC, C_CC, C_ZC, C_Q, C_K, C_V, C_ZA, C_GA, C_GB = 0, 1024, 2048, 3072, 4096, 5120, 5376, 5632, 6656, 7680
EPS = 1e-6
NEG_INF = -1e30
SCALE = HEAD ** -0.5

ADAM_LR = 0.001
ADAM_B1 = 0.9
ADAM_B2 = 0.999
ADAM_EPS = 1e-08
ADAM_WD = 0.01
ADAM_STEP = 10

VMEM_LIMIT = 60 * 1024 * 1024
SM_ROWS = 16
SM_GATE, SM_CONV, SM_QG, SM_KG, SM_SINK, SM_NORM, SM_LOSS, SM_GATE_B, SM_QG_FOLDED, SM_KG_FOLDED = 0, 1, 4, 5, 6, 7, 8, 9, 10, 11


def _cparams(sem):
    return pltpu.CompilerParams(dimension_semantics=sem, vmem_limit_bytes=VMEM_LIMIT)


def _dot(a, b):
    return jnp.dot(a, b, preferred_element_type=F32)


def _dot_nt(a, b):
    return lax.dot_general(a, b, (((1,), (1,)), ((), ())), preferred_element_type=F32)


def _dot_tn(a, b):
    return lax.dot_general(a, b, (((0,), (0,)), ((), ())), preferred_element_type=F32)


def _dot2(x, sel):
    hi = x.astype(BF16)
    lo = (x - hi.astype(F32)).astype(BF16)
    return _dot(hi, sel) + _dot(lo, sel)


def _sigmoid(z):
    return 0.5 * jnp.tanh(0.5 * z) + 0.5


def _head_mean(t, sel, exp):
    return _dot2(_dot(t.astype(BF16), sel) * (1.0 / HEAD), exp)


def _shift_down(a, k, before):
    r = pltpu.roll(a, k, 0)
    row = lax.broadcasted_iota(jnp.int32, (8, 1), 0)
    head = jnp.where(row < k, pltpu.roll(before, k, 0), r[0:8, :])
    return jnp.concatenate([head, r[8:, :]], axis=0)


def _shift_up(a, k, after):
    n = a.shape[0]
    r = pltpu.roll(a, n - k, 0)
    row = lax.broadcasted_iota(jnp.int32, (8, 1), 0)
    tail = jnp.where(row >= 8 - k, pltpu.roll(after, 8 - k, 0), r[n - 8:, :])
    return jnp.concatenate([r[:n - 8, :], tail], axis=0)


def _bf(ref, c0, width):
    return ref[:, c0:c0 + width].astype(F32)


def _selectors():
    c = jnp.arange(D)
    sel_q = (c[:, None] // HEAD == jnp.arange(128)[None, :]).astype(BF16)
    ck = jnp.arange(KVW)
    sel_k = (ck[:, None] // HEAD == jnp.arange(128)[None, :]).astype(BF16)
    src = jnp.arange(KVW)[:, None]
    dst = jnp.arange(KVW)[None, :]
    rep = jnp.stack([((src // HEAD == h) & (src % HEAD == dst % HEAD)).astype(BF16) for h in range(N_KV)])
    fold = (c[:, None] % HEAD == jnp.arange(128)[None, :]).astype(BF16)
    return dict(sel_q=sel_q, exp_q=sel_q.T, sel_k=sel_k, exp_k=sel_k.T, rep=rep, rep_t=jnp.swapaxes(rep, 1, 2), fold=fold)


def inproj_fwd(x, ng, w, name):
    t = x.shape[0]
    tm = min(1024, t)
    cb = 2176
    def body(x_ref, ng_ref, w_ref, u_ref, h_ref, h_scr):
        @pl.when(pl.program_id(1) == 0)
        def _():
            xf = x_ref[...]
            r = lax.rsqrt(jnp.mean(xf * xf, axis=-1, keepdims=True) + EPS)
            hb = (xf * r * ng_ref[...]).astype(BF16)
            h_scr[...] = hb
            h_ref[...] = hb
        u_ref[...] = _dot_nt(h_scr[...], w_ref[...]).astype(BF16)

    return pl.pallas_call(
        body, name=name, grid=(t // tm, IN_COLS // cb),
        in_specs=[pl.BlockSpec((tm, D), lambda i, j: (i, 0)), pl.BlockSpec((1, D), lambda i, j: (0, 0)),
                  pl.BlockSpec((cb, D), lambda i, j: (j, 0))],
        out_specs=[pl.BlockSpec((tm, cb), lambda i, j: (i, j)), pl.BlockSpec((tm, D), lambda i, j: (i, 0))],
        out_shape=[jax.ShapeDtypeStruct((t, IN_COLS), BF16), jax.ShapeDtypeStruct((t, D), BF16)],
        scratch_shapes=[pltpu.VMEM((tm, D), BF16)],
        compiler_params=_cparams(("arbitrary", "arbitrary")),
    )(x, ng, w)


def _conv_fwd(u_ref, uvc_prev, ucc_prev, cw_ref, is_first, tm):
    p = _bf(u_ref, C_CC, D) * _bf(u_ref, C_VC, D)
    pprev = ucc_prev[...].astype(F32) * uvc_prev[...].astype(F32)
    pprev = jnp.where(is_first, 0.0, pprev)
    p1 = _shift_down(p, 1, pprev[8:16, :])
    p2 = _shift_down(p, 2, pprev[8:16, :])
    cw = cw_ref[...]
    conv = cw[0:1, :] * p2 + cw[1:2, :] * p1 + cw[2:3, :] * p
    return p, p1, p2, conv


def _attn_inputs(u_ref, ukv_prev, qg_ref, kg_ref, c, tm):
    q = _bf(u_ref, C_Q, D)
    rq = lax.rsqrt(_head_mean(q * q, c["sel_q"][...], c["exp_q"][...]) + EPS)
    qhat = q * rq
    qn = (qhat * qg_ref[...]).astype(BF16)
    kband = jnp.concatenate([ukv_prev[:, 0:KVW].astype(F32), _bf(u_ref, C_K, KVW)], axis=0)
    rk = lax.rsqrt(_head_mean(kband * kband, c["sel_k"][...], c["exp_k"][...]) + EPS)
    khat = kband * rk
    knb = (khat * kg_ref[...]).astype(BF16)
    vband = jnp.concatenate([ukv_prev[:, KVW:2 * KVW], u_ref[:, C_V:C_V + KVW]], axis=0)
    kt = [_dot(knb, c["rep"][h]).astype(BF16) for h in range(N_KV)]
    vt = [_dot(vband, c["rep"][h]).astype(BF16) for h in range(N_KV)]
    return qhat, rq, qn, khat, rk, kt, vt


def _attn_masks(is_first):
    rows = GROUP * BLK
    r = lax.broadcasted_iota(jnp.int32, (rows, 2 * BLK), 0)
    kk = lax.broadcasted_iota(jnp.int32, (rows, 2 * BLK), 1)
    qq = r % BLK
    valid = (kk > qq) & (kk <= qq + BLK)
    valid_first = valid & ((kk >= BLK) | jnp.logical_not(is_first))
    lane_grp = lax.broadcasted_iota(jnp.int32, (BLK, KVW), 1) // HEAD
    row_grp = lax.broadcasted_iota(jnp.int32, (rows, 1), 0) // BLK
    return valid, valid_first, lane_grp, row_grp


def _sink_col(sinks_ref, h, row_grp):
    col = jnp.full(row_grp.shape, sinks_ref[0, GROUP * h], F32)
    for gi in range(1, GROUP):
        col = jnp.where(row_grp == gi, sinks_ref[0, GROUP * h + gi], col)
    return col


def _stack_groups(a256, lane_grp):
    zero = jnp.zeros_like(a256)
    return jnp.concatenate([jnp.where(lane_grp == gi, a256, zero) for gi in range(GROUP)], axis=0)


def _unstack_groups(a4, lane_grp):
    out = jnp.where(lane_grp == 0, a4[0:BLK], 0.0)
    for gi in range(1, GROUP):
        out = out + jnp.where(lane_grp == gi, a4[gi * BLK:(gi + 1) * BLK], 0.0)
    return out


def _band_sum(parts):
    pieces = [parts[0][0:BLK, :]]
    for b in range(1, len(parts)):
        pieces.append(parts[b - 1][BLK:, :] + parts[b][0:BLK, :])
    pieces.append(parts[-1][BLK:, :])
    return jnp.concatenate(pieces, axis=0)


def _softmax_block(qs, kt_b, valid, sink):
    s = _dot_nt(qs, kt_b)
    s = jnp.where(valid, s, NEG_INF)
    m = jnp.maximum(jnp.max(s, axis=-1, keepdims=True), sink)
    e = jnp.exp(s - m)
    es = jnp.exp(sink - m)
    inv = 1.0 / (jnp.sum(e, axis=-1, keepdims=True) + es)
    return e * inv, es * inv


def _mixer_specs(t, tm, n_tiles, tile_of):
    nb = tm // BLK
    u_spec = pl.BlockSpec((tm, IN_COLS), lambda g: (tile_of(g), 0))
    ukv_prev = pl.BlockSpec((BLK, 2 * KVW), lambda g: (jnp.maximum(tile_of(g) * nb - 1, 0), C_K // (2 * KVW)))
    uvc_prev = pl.BlockSpec((16, D), lambda g: (jnp.maximum(tile_of(g) * (tm // 16) - 1, 0), C_VC // D))
    ucc_prev = pl.BlockSpec((16, D), lambda g: (jnp.maximum(tile_of(g) * (tm // 16) - 1, 0), C_CC // D))
    return u_spec, ukv_prev, uvc_prev, ucc_prev


def _full(shape):
    n = len(shape)
    return pl.BlockSpec(shape, lambda g: (0,) * n)


def mixer_fwd(x, u, cw, qg, kg, sinks, gb, wco, wao, wout, c, name):
    t = x.shape[0]
    tm = min(256, t)
    n_tiles = t // tm
    nb = tm // BLK
    cn = sorted(c)

    def body(x_ref, u_ref, ukv_prev, uvc_prev, ucc_prev, cw_ref, qg_ref, kg_ref, sinks_ref, gb_ref, wco_ref, wao_ref,
             wout_ref, *rest):
        cref = dict(zip(cn, rest[:len(cn)]))
        xo_ref, ya_ref, yb_ref = rest[len(cn):]
        is_first = pl.program_id(0) == 0
        _, _, _, conv = _conv_fwd(u_ref, uvc_prev, ucc_prev, cw_ref, is_first, tm)
        zc = _bf(u_ref, C_ZC, D)
        yc = _bf(u_ref, C_BC, D) * conv * (zc * _sigmoid(zc))
        ya = _dot(yc.astype(BF16), wco_ref[...])

        _, _, qn, _, _, kt, vt = _attn_inputs(u_ref, ukv_prev, qg_ref, kg_ref, cref, tm)
        valid, valid_first, lane_grp, row_grp = _attn_masks(is_first)
        o_cols = []
        for h in range(N_KV):
            sink = _sink_col(sinks_ref, h, row_grp)
            o_rows = []
            for b in range(nb):
                qs = _stack_groups(qn[b * BLK:(b + 1) * BLK, h * KVW:(h + 1) * KVW], lane_grp)
                pn, _ = _softmax_block(qs, kt[h][b * BLK:(b + 2) * BLK], valid_first if b == 0 else valid, sink)
                o4 = _dot(pn.astype(BF16), vt[h][b * BLK:(b + 2) * BLK])
                o_rows.append(_unstack_groups(o4, lane_grp))
            o_cols.append(jnp.concatenate(o_rows, axis=0))
        za = _bf(u_ref, C_ZA, D)
        ob = jnp.concatenate(o_cols, axis=1) * (za * _sigmoid(za))
        yb = _dot(ob.astype(BF16), wao_ref[...])

        g_a = _sigmoid(_bf(u_ref, C_GA, D) + gb_ref[:, 0:D])
        g_b = _sigmoid(_bf(u_ref, C_GB, D) + gb_ref[:, D:2 * D])
        merged = g_a * ya + g_b * yb
        xo_ref[...] = x_ref[...] + _dot(merged.astype(BF16), wout_ref[...])
        ya_ref[...] = ya.astype(BF16)
        yb_ref[...] = yb.astype(BF16)

    u_spec, ukv_prev, uvc_prev, ucc_prev = _mixer_specs(t, tm, n_tiles, lambda g: g)
    tok = pl.BlockSpec((tm, D), lambda g: (g, 0))
    consts = [c[k] for k in cn]
    return pl.pallas_call(
        body, name=name, grid=(n_tiles,),
        in_specs=[tok, u_spec, ukv_prev, uvc_prev, ucc_prev, _full((8, D)), _full((1, D)), _full((1, KVW)),
                  pl.BlockSpec(memory_space=pltpu.SMEM), _full((1, 2 * D)), _full((D, D)), _full((D, D)), _full((D, D))]
                 + [_full(a.shape) for a in consts],
        out_specs=[tok, tok, tok],
        out_shape=[jax.ShapeDtypeStruct((t, D), F32)] + [jax.ShapeDtypeStruct((t, D), BF16)] * 2,
        compiler_params=_cparams(("arbitrary",)),
    )(x, u, u, u, u, cw, qg, kg, sinks, gb, wco, wao, wout, *consts)


def mixer_bwd(dout, u, ya, yb, cw, qg, kg, sinks, gb, wco, wao, wout, c, name):
    t = dout.shape[0]
    tm = min(256, t)
    n_tiles = t // tm
    nb = tm // BLK
    kb = tm + BLK
    cn = sorted(c)

    def body(dout_ref, u_ref, ukv_prev, uvc_prev, ucc_prev, ya_ref, yb_ref, cw_ref, qg_ref, kg_ref, sinks_ref, gb_ref,
             wco_ref, wao_ref, wout_ref, *rest):
        cref = dict(zip(cn, rest[:len(cn)]))
        (du_ref, small_ref, merged_ref, yc_ref, ob_ref, dya_ref, dyb_ref, carry_kv, carry_conv) = rest[len(cn):]
        g = pl.program_id(0)
        is_first = g == n_tiles - 1

        @pl.when(g == 0)
        def _():
            carry_kv[...] = jnp.zeros_like(carry_kv)
            carry_conv[...] = jnp.zeros_like(carry_conv)
            small_ref[...] = jnp.zeros_like(small_ref)

        dout = dout_ref[...]
        dout_b = dout.astype(BF16)
        ya_v = ya_ref[...].astype(F32)
        yb_v = yb_ref[...].astype(F32)
        g_a = _sigmoid(_bf(u_ref, C_GA, D) + gb_ref[:, 0:D])
        g_b = _sigmoid(_bf(u_ref, C_GB, D) + gb_ref[:, D:2 * D])
        merged = g_a * ya_v + g_b * yb_v
        dmerged = _dot_nt(dout_b, wout_ref[...])
        merged_ref[...] = merged.astype(BF16)
        dya = dmerged * g_a
        dyb = dmerged * g_b
        dgl_a = dmerged * ya_v * g_a * (1.0 - g_a)
        dgl_b = dmerged * yb_v * g_b * (1.0 - g_b)
        du_ref[:, C_GA:C_GA + D] = dgl_a.astype(BF16)
        du_ref[:, C_GB:C_GB + D] = dgl_b.astype(BF16)
        small_ref[SM_GATE:SM_GATE + 1, 0:D] += jnp.sum(dgl_a, axis=0, keepdims=True)
        small_ref[SM_GATE_B:SM_GATE_B + 1, 0:D] += jnp.sum(dgl_b, axis=0, keepdims=True)

        p, p1, p2, conv = _conv_fwd(u_ref, uvc_prev, ucc_prev, cw_ref, is_first, tm)
        zc = _bf(u_ref, C_ZC, D)
        bc = _bf(u_ref, C_BC, D)
        sg = _sigmoid(zc)
        sc = zc * sg
        yc = bc * conv * sc
        dya_b = dya.astype(BF16)
        yc_ref[...] = yc.astype(BF16)
        dya_ref[...] = dya_b
        dyc = _dot_nt(dya_b, wco_ref[...])
        du_ref[:, C_BC:C_BC + D] = (dyc * conv * sc).astype(BF16)
        du_ref[:, C_ZC:C_ZC + D] = (dyc * bc * conv * (sg * (1.0 + zc * (1.0 - sg)))).astype(BF16)
        dconv = dyc * bc * sc
        small_ref[SM_CONV + 2:SM_CONV + 3, 0:D] += jnp.sum(dconv * p, axis=0, keepdims=True)
        small_ref[SM_CONV + 1:SM_CONV + 2, 0:D] += jnp.sum(dconv * p1, axis=0, keepdims=True)
        small_ref[SM_CONV:SM_CONV + 1, 0:D] += jnp.sum(dconv * p2, axis=0, keepdims=True)
        nxt = carry_conv[...]
        d1 = _shift_up(dconv, 1, nxt)
        d2 = _shift_up(dconv, 2, nxt)
        carry_conv[...] = dconv[0:8, :]
        cw = cw_ref[...]
        dp = cw[2:3, :] * dconv + cw[1:2, :] * d1 + cw[0:1, :] * d2
        du_ref[:, C_CC:C_CC + D] = (dp * _bf(u_ref, C_VC, D)).astype(BF16)
        du_ref[:, C_VC:C_VC + D] = (dp * _bf(u_ref, C_CC, D)).astype(BF16)

        dyb_b = dyb.astype(BF16)
        dob = _dot_nt(dyb_b, wao_ref[...])
        za = _bf(u_ref, C_ZA, D)
        sga = _sigmoid(za)
        sa = za * sga
        do = dob * sa
        qhat, rq, qn, khat, rk, kt, vt = _attn_inputs(u_ref, ukv_prev, qg_ref, kg_ref, cref, tm)
        valid, valid_first, lane_grp, row_grp = _attn_masks(is_first)
        lane16 = lax.broadcasted_iota(jnp.int32, (1, D), 1)
        dsink_row = jnp.zeros((1, D), F32)
        o_cols, dq_cols, dk4, dv4 = [], [], [], []
        for h in range(N_KV):
            sink = _sink_col(sinks_ref, h, row_grp)
            dsink_col = jnp.zeros((GROUP * BLK, 1), F32)
            o_rows, dq_rows, dk_parts, dv_parts = [], [], [], []
            for b in range(nb):
                rows = slice(b * BLK, (b + 1) * BLK)
                band = slice(b * BLK, (b + 2) * BLK)
                cols = slice(h * KVW, (h + 1) * KVW)
                qs = _stack_groups(qn[rows, cols], lane_grp)
                pn, ps = _softmax_block(qs, kt[h][band], valid_first if b == 0 else valid, sink)
                pn_b = pn.astype(BF16)
                o_rows.append(_unstack_groups(_dot(pn_b, vt[h][band]), lane_grp))
                dos = _stack_groups(do[rows, cols], lane_grp).astype(BF16)
                dpn = _dot_nt(dos, vt[h][band])
                delta = jnp.sum(pn * dpn, axis=-1, keepdims=True)
                ds = (pn * (dpn - delta)).astype(BF16)
                dsink_col = dsink_col - ps * delta
                dq_rows.append(_unstack_groups(_dot(ds, kt[h][band]), lane_grp))
                dk_parts.append(_dot_tn(ds, qs))
                dv_parts.append(_dot_tn(pn_b, dos))
            o_cols.append(jnp.concatenate(o_rows, axis=0))
            dq_cols.append(jnp.concatenate(dq_rows, axis=0))
            dk4.append(_band_sum(dk_parts))
            dv4.append(_band_sum(dv_parts))
            for gi in range(GROUP):
                tot = jnp.sum(dsink_col[gi * BLK:(gi + 1) * BLK, :], axis=0, keepdims=True)
                dsink_row = dsink_row + jnp.where(lane16 == GROUP * h + gi, tot, 0.0)
        small_ref[SM_SINK:SM_SINK + 1, :] += dsink_row

        o = jnp.concatenate(o_cols, axis=1)
        ob_ref[...] = (o * sa).astype(BF16)
        dyb_ref[...] = dyb_b
        du_ref[:, C_ZA:C_ZA + D] = (dob * o * (sga * (1.0 + za * (1.0 - sga)))).astype(BF16)

        dqn = jnp.concatenate(dq_cols, axis=1)
        small_ref[SM_QG:SM_QG + 1, 0:D] += SCALE * jnp.sum(dqn * qhat, axis=0, keepdims=True)
        dqh = dqn * qg_ref[...]
        dq = rq * (dqh - qhat * _head_mean(dqh * qhat, cref["sel_q"][...], cref["exp_q"][...]))
        du_ref[:, C_Q:C_Q + D] = dq.astype(BF16)

        dkn_band = jnp.zeros((kb, KVW), F32)
        dv_band = jnp.zeros((kb, KVW), F32)
        for h in range(N_KV):
            dkn_band = dkn_band + _dot2(dk4[h], cref["rep_t"][h])
            dv_band = dv_band + _dot2(dv4[h], cref["rep_t"][h])
        carried = carry_kv[...]
        pad = jnp.zeros((tm - BLK, KVW), F32)
        if nb > 1:
            dkn = dkn_band[BLK:, :] + jnp.concatenate([pad, carried[:, 0:KVW]], axis=0)
            dv = dv_band[BLK:, :] + jnp.concatenate([pad, carried[:, KVW:2 * KVW]], axis=0)
        else:
            dkn = dkn_band[BLK:, :] + carried[:, 0:KVW]
            dv = dv_band[BLK:, :] + carried[:, KVW:2 * KVW]
        carry_kv[:, 0:KVW] = dkn_band[0:BLK, :]
        carry_kv[:, KVW:2 * KVW] = dv_band[0:BLK, :]
        khat_t = khat[BLK:, :]
        small_ref[SM_KG:SM_KG + 1, 0:KVW] += jnp.sum(dkn * khat_t, axis=0, keepdims=True)
        dkh = dkn * kg_ref[...]
        dk = rk[BLK:, :] * (dkh - khat_t * _head_mean(dkh * khat_t, cref["sel_k"][...], cref["exp_k"][...]))
        du_ref[:, C_K:C_K + KVW] = dk.astype(BF16)
        du_ref[:, C_V:C_V + KVW] = dv.astype(BF16)

    rev = lambda g: n_tiles - 1 - g
    u_spec, ukv_prev, uvc_prev, ucc_prev = _mixer_specs(t, tm, n_tiles, rev)
    tok = pl.BlockSpec((tm, D), lambda g: (rev(g), 0))
    consts = [c[k] for k in cn]
    wspec = _full((D, D))
    return pl.pallas_call(
        body, name=name, grid=(n_tiles,),
        in_specs=[tok, u_spec, ukv_prev, uvc_prev, ucc_prev, tok, tok, _full((8, D)), _full((1, D)), _full((1, KVW)),
                  pl.BlockSpec(memory_space=pltpu.SMEM), _full((1, 2 * D)), wspec, wspec, wspec]
                 + [_full(a.shape) for a in consts],
        out_specs=[pl.BlockSpec((tm, IN_COLS), lambda g: (rev(g), 0)), _full((SM_ROWS, D))] + [tok] * 5,
        out_shape=[jax.ShapeDtypeStruct((t, IN_COLS), BF16), jax.ShapeDtypeStruct((SM_ROWS, D), F32)]
                  + [jax.ShapeDtypeStruct((t, D), BF16)] * 5,
        scratch_shapes=[pltpu.VMEM((BLK, 2 * KVW), F32), pltpu.VMEM((8, D), F32)],
        compiler_params=_cparams(("arbitrary",)),
    )(dout, u, u, u, u, ya, yb, cw, qg, kg, sinks, gb, wco, wao, wout, *consts)


def matmul_tn(a, b, name, after=None):
    t, m = a.shape
    tk = min(1024, t)
    mb = 2176 if m == IN_COLS else m
    nk = t // tk

    def body(a_ref, b_ref, *rest):
        o_ref, acc = rest[-2:]
        k = pl.program_id(1)
        prod = _dot_tn(a_ref[...].astype(BF16), b_ref[...].astype(BF16))

        @pl.when(k == 0)
        def _():
            acc[...] = prod

        @pl.when(k > 0)
        def _():
            acc[...] += prod

        @pl.when(k == nk - 1)
        def _():
            o_ref[...] = acc[...].astype(BF16)

    return pl.pallas_call(
        body, name=name, grid=(m // mb, nk),
        in_specs=[pl.BlockSpec((tk, mb), lambda j, k: (k, j)), pl.BlockSpec((tk, D), lambda j, k: (k, 0))]
                 + ([] if after is None else [ANY]),
        out_specs=pl.BlockSpec((mb, D), lambda j, k: (j, 0)),
        out_shape=jax.ShapeDtypeStruct((m, D), BF16),
        scratch_shapes=[pltpu.VMEM((mb, D), F32)],
        compiler_params=_cparams(("arbitrary", "arbitrary")),
    )(a, b, *([] if after is None else [after]))


def inproj_bwd_x(du, w, x, ng, dout, name):
    t = x.shape[0]
    tm = min(1024, t)
    kc = 2176
    nk = IN_COLS // kc

    def body(du_ref, w_ref, x_ref, ng_ref, dout_ref, dx_ref, dng_ref, acc):
        i = pl.program_id(0)
        k = pl.program_id(1)
        prod = _dot(du_ref[...], w_ref[...])

        @pl.when(k == 0)
        def _():
            acc[...] = prod

        @pl.when(k > 0)
        def _():
            acc[...] += prod

        @pl.when((i == 0) & (k == 0))
        def _():
            dng_ref[...] = jnp.zeros_like(dng_ref)

        @pl.when(k == nk - 1)
        def _():
            dh = acc[...]
            xf = x_ref[...]
            r = lax.rsqrt(jnp.mean(xf * xf, axis=-1, keepdims=True) + EPS)
            xhat = xf * r
            dng_ref[0:1, :] += jnp.sum(dh * xhat, axis=0, keepdims=True)
            dxh = dh * ng_ref[...]
            dx_ref[...] = dout_ref[...] + r * (dxh - xhat * jnp.mean(dxh * xhat, axis=-1, keepdims=True))

    tok = pl.BlockSpec((tm, D), lambda i, k: (i, 0))
    return pl.pallas_call(
        body, name=name, grid=(t // tm, nk),
        in_specs=[pl.BlockSpec((tm, kc), lambda i, k: (i, k)), pl.BlockSpec((kc, D), lambda i, k: (k, 0)), tok,
                  pl.BlockSpec((1, D), lambda i, k: (0, 0)), tok],
        out_specs=[tok, pl.BlockSpec((8, D), lambda i, k: (0, 0))],
        out_shape=[jax.ShapeDtypeStruct((t, D), F32), jax.ShapeDtypeStruct((8, D), F32)],
        scratch_shapes=[pltpu.VMEM((tm, D), F32)],
        compiler_params=_cparams(("arbitrary", "arbitrary")),
    )(du, w, x, ng, dout)


def loss_head(y, target, name):
    t = y.shape[0]
    tm = min(1024, t)

    def body(y_ref, t_ref, dy_ref, loss_ref):
        @pl.when(pl.program_id(0) == 0)
        def _():
            loss_ref[...] = jnp.zeros_like(loss_ref)
        err = y_ref[...] - t_ref[...]
        dy_ref[...] = err * (1.0 / D)
        part = jnp.sum(jnp.sum(err * err, axis=-1, keepdims=True) * (1.0 / D), axis=0, keepdims=True)
        loss_ref[...] += 0.5 * part

    tok = pl.BlockSpec((tm, D), lambda i: (i, 0))
    return pl.pallas_call(
        body, name=name, grid=(t // tm,), in_specs=[tok, tok],
        out_specs=[tok, pl.BlockSpec((8, 128), lambda i: (0, 0))],
        out_shape=[jax.ShapeDtypeStruct((t, D), F32), jax.ShapeDtypeStruct((8, 128), F32)],
        compiler_params=_cparams(("arbitrary",)),
    )(y, target)


def layer_operands(l, norm_g, conv_w_full, q_norm_g, k_norm_g, sinks, gate_b, w_in_b, wco_b, wao_b, wout_b):
    return dict(
        ng=norm_g[l][None, :], cw=jnp.pad(conv_w_full[l], ((0, 5), (0, 0))),
        qg=jnp.tile(q_norm_g[l] * SCALE, D // HEAD)[None, :], kg=jnp.tile(k_norm_g[l], N_KV)[None, :],
        sinks=sinks[l][None, :], gb=gate_b[l][None, :],
        w_in=w_in_b[l], wco=wco_b[l], wao=wao_b[l], wout=wout_b[l])


def layer_bwd(dout, saved, lw, c, l, send_off):
    x, u, h, ya, yb = saved
    du, small, merged, yc, ob, dya, dyb = mixer_bwd(dout, u, ya, yb, lw["cw"], lw["qg"], lw["kg"], lw["sinks"], lw["gb"],
                                                    lw["wco"], lw["wao"], lw["wout"], c, f"mixer_bwd_{l}")
    grads = dict(w_in=matmul_tn(du, h, f"dw_in_{l}"), small=small)
    token = send_off(grads, False)
    grads["wout"] = matmul_tn(merged, dout, f"dw_out_{l}", after=token)
    grads["wco"] = matmul_tn(yc, dya, f"dw_conv_out_{l}")
    grads["wao"] = matmul_tn(ob, dyb, f"dw_attn_out_{l}")
    token = send_off(grads, True)
    dx, grads["dng"] = inproj_bwd_x(du, lw["w_in"], x, lw["ng"] + token[0:1, 0:1], dout, f"inproj_bwd_{l}")
    return dx, grads


MESH = pl.DeviceIdType.MESH
ANY = pl.BlockSpec(memory_space=pl.ANY)


def _place():
    return lax.axis_index("x"), lax.axis_index("y"), lax.axis_index("c")


def all_gather(arrs, after, name):
    n = len(arrs)

    def body(*refs):
        ins, outs = refs[:n], refs[n + 1:2 * n + 1]
        send_sems, recv_sems, local_sems = refs[2 * n + 1:]
        x, y, c = _place()
        me, sibling = (x, y, c), (x, y, 1 - c)
        chips = [(1 - x, y), (x, 1 - y), (1 - x, 1 - y)]

        def slot(a, block):
            px, py, pc = block
            return outs[a].at[4 * px + 2 * py + pc]

        def copy(a, k, block, to, src=None):
            return pltpu.make_async_remote_copy(
                src_ref=slot(a, block) if src is None else src, dst_ref=slot(a, block),
                send_sem=send_sems.at[a, k], recv_sem=recv_sems.at[a, k], device_id=to, device_id_type=MESH)

        mine = [pltpu.make_async_copy(ins[a], slot(a, me), local_sems.at[a]) for a in range(n)]
        for cp in mine:
            cp.start()
        first = []
        for a in range(n):
            first.append(copy(a, 0, me, sibling, src=ins[a]))
            first += [copy(a, 1 + j, me, (*chip, c), src=ins[a]) for j, chip in enumerate(chips)]
        for cp in first:
            cp.start()
        passed = []
        for j, chip in enumerate(chips):
            for a in range(n):
                copy(a, 1 + j, (*chip, c), me).wait_recv()
                passed.append(copy(a, 4 + j, (*chip, c), sibling))
                passed[-1].start()
        for a in range(n):
            copy(a, 0, sibling, me).wait_recv()
            for j, chip in enumerate(chips):
                copy(a, 4 + j, (*chip, 1 - c), me).wait_recv()
        for cp in first + passed:
            cp.wait_send()
        for cp in mine:
            cp.wait()

    return pl.pallas_call(
        body, name=name, in_specs=[ANY] * (n + 1), out_specs=[ANY] * n,
        out_shape=[jax.ShapeDtypeStruct((N_DEV,) + a.shape, a.dtype) for a in arrs],
        scratch_shapes=[pltpu.SemaphoreType.DMA((n, 7)), pltpu.SemaphoreType.DMA((n, 7)), pltpu.SemaphoreType.DMA((n,))],
    )(*arrs, after)


HBM_SPEC = pl.BlockSpec(memory_space=pltpu.HBM)
SEM_SPEC = pl.BlockSpec(memory_space=pltpu.SEMAPHORE)
EFFECT = pltpu.SideEffectType.DATAFLOW_SIDE_EFFECTING


ALL_PEERS = (1, 2, 3, 4, 5, 6, 7)


def _flip(place, k):
    x, y, c = place
    return (1 - x if k & 4 else x, 1 - y if k & 2 else y, 1 - c if k & 1 else c)


def _slot(place):
    return 4 * place[0] + 2 * place[1] + place[2]


def _exchange_copies(ins, lands, send_sems, recv_sems, scatter, flips, arriving):
    me = _place()
    out = []
    for a in range(len(ins)):
        for i, k in enumerate(flips):
            peer = _flip(me, k)
            out.append(pltpu.make_async_remote_copy(
                src_ref=ins[a].at[_slot(peer)] if scatter else ins[a],
                dst_ref=lands[a].at[_slot(peer) if arriving else _slot(me)],
                send_sem=send_sems.at[a * len(flips) + i], recv_sem=recv_sems.at[a * len(flips) + i],
                device_id=peer, device_id_type=MESH))
    return out


def exchange_start(arrs, scatter, flips, after, name):
    n = len(arrs)
    lands = [lax.empty(a.shape if scatter else (N_DEV,) + a.shape, a.dtype) for a in arrs]

    def body(*refs):
        ins, lz = refs[:n], refs[n:2 * n]
        send_sems, recv_sems = refs[2 * n + 1], refs[2 * n + 2]
        token = refs[-1]
        for cp in _exchange_copies(ins, lz, send_sems, recv_sems, scatter, flips, False):
            cp.start()
        token[...] = jnp.zeros_like(token)

    sems = pltpu.SemaphoreType.DMA((n * len(flips),))
    res = pl.pallas_call(
        body, name=name,
        out_shape=(sems, sems, *[pltpu.HBM(a.shape, a.dtype) for a in arrs], *[pltpu.HBM(a.shape, a.dtype) for a in lands],
                   jax.ShapeDtypeStruct((8, 128), F32)),
        in_specs=[HBM_SPEC] * (2 * n) + [ANY],
        out_specs=(SEM_SPEC, SEM_SPEC, *[HBM_SPEC] * (2 * n), pl.BlockSpec(memory_space=pltpu.VMEM)),
        input_output_aliases={i: 2 + i for i in range(2 * n)},
        compiler_params=pltpu.CompilerParams(has_side_effects=EFFECT),
    )(*[pltpu.with_memory_space_constraint(a, pltpu.HBM) for a in arrs],
      *[pltpu.with_memory_space_constraint(a, pltpu.HBM) for a in lands], after)
    return dict(send=res[0], recv=res[1], srcs=res[2:2 + n], lands=res[2 + n:2 + 2 * n], token=res[-1], scatter=scatter,
                flips=flips)


def exchange_wait(state, after, name):
    n = len(state["srcs"])

    def body(*refs):
        ins, lz = refs[:n], refs[n:2 * n]
        send_sems, recv_sems = refs[2 * n], refs[2 * n + 1]
        for cp in _exchange_copies(ins, lz, send_sems, recv_sems, state["scatter"], state["flips"], True):
            cp.wait_send()
            cp.wait_recv()

    both = list(state["srcs"]) + list(state["lands"])
    res = pl.pallas_call(
        body, name=name, out_shape=tuple(pltpu.HBM(a.shape, a.dtype) for a in both),
        in_specs=[HBM_SPEC] * (2 * n) + [SEM_SPEC, SEM_SPEC, ANY], out_specs=tuple([HBM_SPEC] * (2 * n)),
        input_output_aliases={i: i for i in range(2 * n)},
        compiler_params=pltpu.CompilerParams(has_side_effects=EFFECT),
    )(*both, state["send"], state["recv"], after)
    return res[:n], res[n:]


def adamw(w, m, v, parts, name):
    r, cdim = w.shape
    n_parts = parts.shape[0]
    rb = 256 if r % 256 == 0 else (SHARD_COLS // 4 if r == SHARD_COLS else r)

    def body(w_ref, m_ref, v_ref, p_ref, g_ref, d_ref, mo_ref, vo_ref):
        g = p_ref[0].astype(F32)
        for i in range(1, n_parts):
            g = g + p_ref[i].astype(F32)
        m_new = ADAM_B1 * m_ref[...] + (1.0 - ADAM_B1) * g
        v_new = ADAM_B2 * v_ref[...] + (1.0 - ADAM_B2) * (g * g)
        m_hat = m_new / (1.0 - ADAM_B1 ** ADAM_STEP)
        v_hat = v_new / (1.0 - ADAM_B2 ** ADAM_STEP)
        g_ref[...] = g
        d_ref[...] = -ADAM_LR * (m_hat / (jnp.sqrt(v_hat) + ADAM_EPS) + ADAM_WD * w_ref[...])
        mo_ref[...] = m_new
        vo_ref[...] = v_new

    blk = pl.BlockSpec((rb, cdim), lambda i: (i, 0))
    return pl.pallas_call(
        body, name=name, grid=(r // rb,),
        in_specs=[blk, blk, blk, pl.BlockSpec((n_parts, rb, cdim), lambda i: (0, i, 0))],
        out_specs=[blk] * 4, out_shape=[jax.ShapeDtypeStruct((r, cdim), F32)] * 4,
        compiler_params=_cparams(("arbitrary",)),
    )(w, m, v, parts)


def adamw_layers(w, m, v, lands, srcs, me, lo, prev, name):
    nl_all, r, cdim = w.shape
    nl = len(lands)
    rb = 256 if r % 256 == 0 else (SHARD_COLS // 4 if r == SHARD_COLS else r)
    nblk = r // rb
    n_prev = 0 if prev is None else 4

    def body(me_ref, w_ref, m_ref, v_ref, *rest):
        land_refs, src_refs = rest[:nl], rest[nl:2 * nl]
        g_ref, d_ref, mo_ref, vo_ref = rest[2 * nl + n_prev:]
        for k in range(nl):
            @pl.when(pl.program_id(0) == k)
            def _(k=k):
                own = src_refs[k][...].astype(F32)
                g = jnp.where(me_ref[0] == 0, own, land_refs[k][0].astype(F32))
                for i in range(1, N_DEV):
                    g = g + jnp.where(me_ref[0] == i, own, land_refs[k][i].astype(F32))
                m_new = ADAM_B1 * m_ref[...] + (1.0 - ADAM_B1) * g
                v_new = ADAM_B2 * v_ref[...] + (1.0 - ADAM_B2) * (g * g)
                m_hat = m_new / (1.0 - ADAM_B1 ** ADAM_STEP)
                v_hat = v_new / (1.0 - ADAM_B2 ** ADAM_STEP)
                g_ref[...] = g
                d_ref[...] = -ADAM_LR * (m_hat / (jnp.sqrt(v_hat) + ADAM_EPS) + ADAM_WD * w_ref[...])
                mo_ref[...] = m_new
                vo_ref[...] = v_new

    blk = pl.BlockSpec((None, rb, cdim), lambda l, i, me_ref: (lo + l, i, 0))

    def rows(l, i, k):
        return jnp.where(l < k, 0, jnp.where(l == k, i, nblk - 1))

    land_specs = [pl.BlockSpec((N_DEV, rb, cdim), lambda l, i, me_ref, k=k: (0, rows(l, i, k), 0)) for k in range(nl)]
    src_specs = [pl.BlockSpec((None, rb, cdim), lambda l, i, me_ref, k=k: (me_ref[0], rows(l, i, k), 0)) for k in range(nl)]
    return pl.pallas_call(
        body, name=name,
        grid_spec=pltpu.PrefetchScalarGridSpec(
            num_scalar_prefetch=1, grid=(nl, nblk),
            in_specs=[blk, blk, blk] + land_specs + src_specs + [ANY] * n_prev, out_specs=[blk] * 4),
        out_shape=[jax.ShapeDtypeStruct((nl_all, r, cdim), F32)] * 4,
        input_output_aliases={4 + 2 * nl + j: j for j in range(n_prev)},
        compiler_params=_cparams(("arbitrary", "arbitrary")),
    )(me, w, m, v, *lands, *srcs, *([] if prev is None else prev))


def small_sum(parts, fold, name):
    rows = parts.shape[1]

    def dot3(xv, sel):
        out = jnp.zeros((xv.shape[0], sel.shape[1]), F32)
        for _ in range(3):
            hi = xv.astype(BF16)
            out = out + _dot(hi, sel)
            xv = xv - hi.astype(F32)
        return out

    def body(p_ref, fold_ref, o_ref):
        tot = p_ref[0]
        for i in range(1, N_DEV):
            tot = tot + p_ref[i]
        o_ref[...] = tot
        for l in range(rows // SM_ROWS):
            blk = tot[l * SM_ROWS:l * SM_ROWS + 8, 0:D]
            folded = dot3(blk, fold_ref[...])
            o_ref[l * SM_ROWS + SM_QG_FOLDED:l * SM_ROWS + SM_QG_FOLDED + 1, 0:128] = folded[SM_QG:SM_QG + 1, :]
            o_ref[l * SM_ROWS + SM_KG_FOLDED:l * SM_ROWS + SM_KG_FOLDED + 1, 0:128] = folded[SM_KG:SM_KG + 1, :]

    return pl.pallas_call(
        body, name=name, out_shape=jax.ShapeDtypeStruct((rows, D), F32),
        compiler_params=_cparams(None),
    )(parts, fold)


def kernel(x, norm_g, w_in, conv_w, q_norm_g, k_norm_g, sinks, w_conv_out, w_attn_out, gate_b, w_out, loss_target, m_norm_g, m_w_in, m_conv_w, m_q_norm_g, m_k_norm_g, m_sinks, m_w_conv_out, m_w_attn_out, m_gate_b, m_w_out, v_norm_g, v_w_in, v_conv_w, v_q_norm_g, v_k_norm_g, v_sinks, v_w_conv_out, v_w_attn_out, v_gate_b, v_w_out):
    c = _selectors()
    me = 4 * lax.axis_index("x") + 2 * lax.axis_index("y") + lax.axis_index("c")

    w_in_t, m_w_in_t, v_w_in_t = (jnp.swapaxes(a, 1, 2) for a in (w_in, m_w_in, v_w_in))

    def shards(l):
        return [w_in_t[l].astype(BF16), w_conv_out[l].astype(BF16), w_attn_out[l].astype(BF16), w_out[l].astype(BF16)]

    def gather_start(arrs, after, tag):
        return exchange_start(arrs, False, ALL_PEERS, after, f"gather_start_{tag}")

    h = x[0]
    saved, lws = [], []
    gather = None
    for l in range(DEPTH):
        if l == 0:
            lands = all_gather(shards(0) + [conv_w], h, "gather_0")
            conv_full = jnp.transpose(lands[4], (1, 2, 0, 3)).reshape(DEPTH, 3, D)
            after = lands[1]
        else:
            mine, lands = exchange_wait(gather, h, f"gather_wait_{l}")
            lands = [lax.dynamic_update_index_in_dim(land, src, me, 0) for land, src in zip(lands, mine)]
            after = mine[0]
        if l + 1 < DEPTH:
            gather = gather_start(shards(l + 1), after, l + 1)
        lws.append(layer_operands(l, norm_g, conv_full, q_norm_g, k_norm_g, sinks, gate_b,
                                  {l: lands[0].reshape(IN_COLS, D)}, {l: lands[1].reshape(D, D)},
                                  {l: lands[2].reshape(D, D)}, {l: lands[3].reshape(D, D)}))
        u, hb = inproj_fwd(h, lws[l]["ng"] + gather["token"][0:1, 0:1], lws[l]["w_in"], f"inproj_fwd_{l}")
        x_in = h
        h, ya, yb = mixer_fwd(x_in, u, lws[l]["cw"], lws[l]["qg"], lws[l]["kg"], lws[l]["sinks"], lws[l]["gb"],
                              lws[l]["wco"], lws[l]["wao"], lws[l]["wout"], c, f"mixer_fwd_{l}")
        saved.append((x_in, u, hb, ya, yb))
    dh, loss_part = loss_head(h, loss_target[0], "loss_head")

    grads, scatters = [None] * DEPTH, [[] for _ in range(DEPTH)]
    for l in reversed(range(DEPTH)):
        def send_off(g, done, l=l):
            rest = [g[k].reshape(N_DEV, SHARD_ROWS, D) for k in ("wco", "wao", "wout")] if done else []
            first = [g["w_in"].reshape(N_DEV, SHARD_COLS, D)] if done == (l > 0) else []
            if not first + rest:
                return None
            tag = f"{l}" if l > 0 else ("0_rest" if done else "0_in")
            scatters[l].append(exchange_start(first + rest, True, ALL_PEERS, g["small"], f"scatter_start_{tag}"))
            return scatters[l][-1]["token"]
        dh, grads[l] = layer_bwd(dh, saved[l], lws[l], c, l, send_off)

    me1 = me.astype(jnp.int32).reshape(1)
    mine, lands = {}, {}
    for l in (3, 2, 1):
        mine[l], lands[l] = exchange_wait(scatters[l][0], dh, f"scatter_wait_{l}")
    weights = [(w_in_t, m_w_in_t, v_w_in_t, "w_in"), (w_conv_out, m_w_conv_out, v_w_conv_out, "w_conv_out"),
               (w_attn_out, m_w_attn_out, v_w_attn_out, "w_attn_out"), (w_out, m_w_out, v_w_out, "w_out")]
    upd = [adamw_layers(w, m, v, [lands[l][i] for l in (1, 2, 3)], [mine[l][i] for l in (1, 2, 3)], me1, 1, None,
                        f"adamw_{n}_upper") for i, (w, m, v, n) in enumerate(weights)]
    blocks = []
    for l in range(DEPTH):
        blk = grads[l]["small"]
        blk = blk.at[SM_NORM, 0:D].set(grads[l]["dng"][0])
        if l == 0:
            blk = blk.at[SM_LOSS, 0:128].set(loss_part[0])
        blocks.append(blk)
    small_x = exchange_start([jnp.concatenate(blocks, axis=0)], False, ALL_PEERS, grads[0]["dng"], "gather_small_start")

    m_in, l_in = exchange_wait(scatters[0][0], upd[0][0], "scatter_wait_0_in")
    upd[0] = adamw_layers(*weights[0][:3], [l_in[0]], [m_in[0]], me1, 0, upd[0], "adamw_w_in_0")
    m_rest, l_rest = exchange_wait(scatters[0][1], upd[0][0], "scatter_wait_0_rest")
    for i in (1, 2, 3):
        upd[i] = adamw_layers(*weights[i][:3], [l_rest[i - 1]], [m_rest[i - 1]], me1, 0, upd[i], f"adamw_{weights[i][3]}_0")
    u_in = [jnp.swapaxes(o, 1, 2) for o in upd[0]]
    u_co, u_ao, u_out = upd[1], upd[2], upd[3]

    mine_s, lands_s = exchange_wait(small_x, u_out[0], "gather_small_wait")
    gathered = lax.dynamic_update_index_in_dim(lands_s[0], mine_s[0], me, 0)
    tot = small_sum(gathered, c["fold"], "small_sum")
    tot = tot.reshape(DEPTH, SM_ROWS, D)
    loss = tot[0, SM_LOSS, 0]

    def update_small(w, m, v, g, name):
        return adamw(w, m, v, g[None], name)

    u_ng = update_small(norm_g, m_norm_g, v_norm_g, tot[:, SM_NORM, 0:D], "adamw_norm_g")
    u_qg = update_small(q_norm_g, m_q_norm_g, v_q_norm_g, tot[:, SM_QG_FOLDED, 0:HEAD], "adamw_q_norm_g")
    u_kg = update_small(k_norm_g, m_k_norm_g, v_k_norm_g, tot[:, SM_KG_FOLDED, 0:HEAD], "adamw_k_norm_g")
    u_sk = update_small(sinks, m_sinks, v_sinks, tot[:, SM_SINK, 0:16], "adamw_sinks")
    g_gate = jnp.concatenate([tot[:, SM_GATE, :], tot[:, SM_GATE_B, :]], axis=1)
    u_gb = update_small(gate_b, m_gate_b, v_gate_b, g_gate, "adamw_gate_b")
    g_conv = lax.dynamic_slice_in_dim(tot[:, SM_CONV:SM_CONV + 3, 0:D], me * SHARD_ROWS, SHARD_ROWS, axis=2)
    u_cw = [o.reshape(DEPTH, 3, SHARD_ROWS) for o in update_small(
        conv_w.reshape(DEPTH * 3, SHARD_ROWS), m_conv_w.reshape(DEPTH * 3, SHARD_ROWS),
        v_conv_w.reshape(DEPTH * 3, SHARD_ROWS), g_conv.reshape(DEPTH * 3, SHARD_ROWS), "adamw_conv_w")]

    order = [u_ng, u_in, u_cw, u_qg, u_kg, u_sk, u_co, u_ao, u_gb, u_out]
    return (loss, dh[None], *[u[0] for u in order], *[u[1] for u in order], *[u[2] for u in order], *[u[3] for u in order])
```

```python
import functools

import jax
import jax.numpy as jnp
from jax import lax
from jax.experimental import pallas as pl
from jax.experimental.pallas import tpu as pltpu

F32 = jnp.float32
BF16 = jnp.bfloat16

N_DEV = 8
DEPTH = 4
D = 1024
N_KV = 4
GROUP = 4
HEAD = 64
BLK = 128
KVW = N_KV * HEAD
IN_COLS = 8704
SHARD_COLS = IN_COLS // N_DEV
SHARD_ROWS = D // N_DEV
C_VC, C_BC, C_CC, C_ZC, C_Q, C_K, C_V, C_ZA, C_GA, C_GB = 0, 1024, 2048, 3072, 4096, 5120, 5376, 5632, 6656, 7680
EPS = 1e-6
NEG_INF = -1e30
SCALE = HEAD ** -0.5

ADAM_LR = 0.001
ADAM_B1 = 0.9
ADAM_B2 = 0.999
ADAM_EPS = 1e-08
ADAM_WD = 0.01
ADAM_STEP = 10

VMEM_LIMIT = 60 * 1024 * 1024
SM_ROWS = 16
SM_GATE, SM_CONV, SM_QG, SM_KG, SM_SINK, SM_NORM, SM_LOSS, SM_GATE_B, SM_QG_FOLDED, SM_KG_FOLDED = 0, 1, 4, 5, 6, 7, 8, 9, 10, 11


def _cparams(sem):
    return pltpu.CompilerParams(dimension_semantics=sem, vmem_limit_bytes=VMEM_LIMIT)


def _dot(a, b):
    return jnp.dot(a, b, preferred_element_type=F32)


def _dot_nt(a, b):
    return lax.dot_general(a, b, (((1,), (1,)), ((), ())), preferred_element_type=F32)


def _dot_tn(a, b):
    return lax.dot_general(a, b, (((0,), (0,)), ((), ())), preferred_element_type=F32)


def _dot2(x, sel):
    hi = x.astype(BF16)
    lo = (x - hi.astype(F32)).astype(BF16)
    return _dot(hi, sel) + _dot(lo, sel)


def _sigmoid(z):
    return 0.5 * jnp.tanh(0.5 * z) + 0.5


def _head_mean(t, sel, exp):
    return _dot2(_dot(t.astype(BF16), sel) * (1.0 / HEAD), exp)


def _shift_down(a, k, before):
    r = pltpu.roll(a, k, 0)
    row = lax.broadcasted_iota(jnp.int32, (8, 1), 0)
    head = jnp.where(row < k, pltpu.roll(before, k, 0), r[0:8, :])
    return jnp.concatenate([head, r[8:, :]], axis=0)


def _shift_up(a, k, after):
    n = a.shape[0]
    r = pltpu.roll(a, n - k, 0)
    row = lax.broadcasted_iota(jnp.int32, (8, 1), 0)
    tail = jnp.where(row >= 8 - k, pltpu.roll(after, 8 - k, 0), r[n - 8:, :])
    return jnp.concatenate([r[:n - 8, :], tail], axis=0)


def _bf(ref, c0, width):
    return ref[:, c0:c0 + width].astype(F32)


def _selectors():
    c = jnp.arange(D)
    sel_q = (c[:, None] // HEAD == jnp.arange(128)[None, :]).astype(BF16)
    ck = jnp.arange(KVW)
    sel_k = (ck[:, None] // HEAD == jnp.arange(128)[None, :]).astype(BF16)
    src = jnp.arange(KVW)[:, None]
    dst = jnp.arange(KVW)[None, :]
    rep = jnp.stack([((src // HEAD == h) & (src % HEAD == dst % HEAD)).astype(BF16) for h in range(N_KV)])
    fold = (c[:, None] % HEAD == jnp.arange(128)[None, :]).astype(BF16)
    return dict(sel_q=sel_q, exp_q=sel_q.T, sel_k=sel_k, exp_k=sel_k.T, rep=rep, rep_t=jnp.swapaxes(rep, 1, 2), fold=fold)


def inproj_fwd(x, ng, w, name):
    t = x.shape[0]
    tm = min(1024, t)
    cb = 2176
    def body(x_ref, ng_ref, w_ref, u_ref, h_ref, h_scr):
        @pl.when(pl.program_id(1) == 0)
        def _():
            xf = x_ref[...]
            r = lax.rsqrt(jnp.mean(xf * xf, axis=-1, keepdims=True) + EPS)
            hb = (xf * r * ng_ref[...]).astype(BF16)
            h_scr[...] = hb
            h_ref[...] = hb
        u_ref[...] = _dot_nt(h_scr[...], w_ref[...]).astype(BF16)

    return pl.pallas_call(
        body, name=name, grid=(t // tm, IN_COLS // cb),
        in_specs=[pl.BlockSpec((tm, D), lambda i, j: (i, 0)), pl.BlockSpec((1, D), lambda i, j: (0, 0)),
                  pl.BlockSpec((cb, D), lambda i, j: (j, 0))],
        out_specs=[pl.BlockSpec((tm, cb), lambda i, j: (i, j)), pl.BlockSpec((tm, D), lambda i, j: (i, 0))],
        out_shape=[jax.ShapeDtypeStruct((t, IN_COLS), BF16), jax.ShapeDtypeStruct((t, D), BF16)],
        scratch_shapes=[pltpu.VMEM((tm, D), BF16)],
        compiler_params=_cparams(("arbitrary", "arbitrary")),
    )(x, ng, w)


def _conv_fwd(u_ref, uvc_prev, ucc_prev, cw_ref, is_first, tm):
    p = _bf(u_ref, C_CC, D) * _bf(u_ref, C_VC, D)
    pprev = ucc_prev[...].astype(F32) * uvc_prev[...].astype(F32)
    pprev = jnp.where(is_first, 0.0, pprev)
    p1 = _shift_down(p, 1, pprev[8:16, :])
    p2 = _shift_down(p, 2, pprev[8:16, :])
    cw = cw_ref[...]
    conv = cw[0:1, :] * p2 + cw[1:2, :] * p1 + cw[2:3, :] * p
    return p, p1, p2, conv


def _attn_inputs(u_ref, ukv_prev, qg_ref, kg_ref, c, tm):
    q = _bf(u_ref, C_Q, D)
    rq = lax.rsqrt(_head_mean(q * q, c["sel_q"][...], c["exp_q"][...]) + EPS)
    qhat = q * rq
    qn = (qhat * qg_ref[...]).astype(BF16)
    kband = jnp.concatenate([ukv_prev[:, 0:KVW].astype(F32), _bf(u_ref, C_K, KVW)], axis=0)
    rk = lax.rsqrt(_head_mean(kband * kband, c["sel_k"][...], c["exp_k"][...]) + EPS)
    khat = kband * rk
    knb = (khat * kg_ref[...]).astype(BF16)
    vband = jnp.concatenate([ukv_prev[:, KVW:2 * KVW], u_ref[:, C_V:C_V + KVW]], axis=0)
    kt = [_dot(knb, c["rep"][h]).astype(BF16) for h in range(N_KV)]
    vt = [_dot(vband, c["rep"][h]).astype(BF16) for h in range(N_KV)]
    return qhat, rq, qn, khat, rk, kt, vt


def _attn_masks(is_first):
    rows = GROUP * BLK
    r = lax.broadcasted_iota(jnp.int32, (rows, 2 * BLK), 0)
    kk = lax.broadcasted_iota(jnp.int32, (rows, 2 * BLK), 1)
    qq = r % BLK
    valid = (kk > qq) & (kk <= qq + BLK)
    valid_first = valid & ((kk >= BLK) | jnp.logical_not(is_first))
    lane_grp = lax.broadcasted_iota(jnp.int32, (BLK, KVW), 1) // HEAD
    row_grp = lax.broadcasted_iota(jnp.int32, (rows, 1), 0) // BLK
    return valid, valid_first, lane_grp, row_grp


def _sink_col(sinks_ref, h, row_grp):
    col = jnp.full(row_grp.shape, sinks_ref[0, GROUP * h], F32)
    for gi in range(1, GROUP):
        col = jnp.where(row_grp == gi, sinks_ref[0, GROUP * h + gi], col)
    return col


def _stack_groups(a256, lane_grp):
    zero = jnp.zeros_like(a256)
    return jnp.concatenate([jnp.where(lane_grp == gi, a256, zero) for gi in range(GROUP)], axis=0)


def _unstack_groups(a4, lane_grp):
    out = jnp.where(lane_grp == 0, a4[0:BLK], 0.0)
    for gi in range(1, GROUP):
        out = out + jnp.where(lane_grp == gi, a4[gi * BLK:(gi + 1) * BLK], 0.0)
    return out


def _band_sum(parts):
    pieces = [parts[0][0:BLK, :]]
    for b in range(1, len(parts)):
        pieces.append(parts[b - 1][BLK:, :] + parts[b][0:BLK, :])
    pieces.append(parts[-1][BLK:, :])
    return jnp.concatenate(pieces, axis=0)


def _softmax_block(qs, kt_b, valid, sink):
    s = _dot_nt(qs, kt_b)
    s = jnp.where(valid, s, NEG_INF)
    m = jnp.maximum(jnp.max(s, axis=-1, keepdims=True), sink)
    e = jnp.exp(s - m)
    es = jnp.exp(sink - m)
    inv = 1.0 / (jnp.sum(e, axis=-1, keepdims=True) + es)
    return e * inv, es * inv


def _mixer_specs(t, tm, n_tiles, tile_of):
    nb = tm // BLK
    u_spec = pl.BlockSpec((tm, IN_COLS), lambda g: (tile_of(g), 0))
    ukv_prev = pl.BlockSpec((BLK, 2 * KVW), lambda g: (jnp.maximum(tile_of(g) * nb - 1, 0), C_K // (2 * KVW)))
    uvc_prev = pl.BlockSpec((16, D), lambda g: (jnp.maximum(tile_of(g) * (tm // 16) - 1, 0), C_VC // D))
    ucc_prev = pl.BlockSpec((16, D), lambda g: (jnp.maximum(tile_of(g) * (tm // 16) - 1, 0), C_CC // D))
    return u_spec, ukv_prev, uvc_prev, ucc_prev


def _full(shape):
    n = len(shape)
    return pl.BlockSpec(shape, lambda g: (0,) * n)


def mixer_fwd(x, u, cw, qg, kg, sinks, gb, wco, wao, wout, c, name):
    t = x.shape[0]
    tm = min(256, t)
    n_tiles = t // tm
    nb = tm // BLK
    cn = sorted(c)

    def body(x_ref, u_ref, ukv_prev, uvc_prev, ucc_prev, cw_ref, qg_ref, kg_ref, sinks_ref, gb_ref, wco_ref, wao_ref,
             wout_ref, *rest):
        cref = dict(zip(cn, rest[:len(cn)]))
        xo_ref, ya_ref, yb_ref = rest[len(cn):]
        is_first = pl.program_id(0) == 0
        _, _, _, conv = _conv_fwd(u_ref, uvc_prev, ucc_prev, cw_ref, is_first, tm)
        zc = _bf(u_ref, C_ZC, D)
        yc = _bf(u_ref, C_BC, D) * conv * (zc * _sigmoid(zc))
        ya = _dot(yc.astype(BF16), wco_ref[...])

        _, _, qn, _, _, kt, vt = _attn_inputs(u_ref, ukv_prev, qg_ref, kg_ref, cref, tm)
        valid, valid_first, lane_grp, row_grp = _attn_masks(is_first)
        o_cols = []
        for h in range(N_KV):
            sink = _sink_col(sinks_ref, h, row_grp)
            o_rows = []
            for b in range(nb):
                qs = _stack_groups(qn[b * BLK:(b + 1) * BLK, h * KVW:(h + 1) * KVW], lane_grp)
                pn, _ = _softmax_block(qs, kt[h][b * BLK:(b + 2) * BLK], valid_first if b == 0 else valid, sink)
                o4 = _dot(pn.astype(BF16), vt[h][b * BLK:(b + 2) * BLK])
                o_rows.append(_unstack_groups(o4, lane_grp))
            o_cols.append(jnp.concatenate(o_rows, axis=0))
        za = _bf(u_ref, C_ZA, D)
        ob = jnp.concatenate(o_cols, axis=1) * (za * _sigmoid(za))
        yb = _dot(ob.astype(BF16), wao_ref[...])

        g_a = _sigmoid(_bf(u_ref, C_GA, D) + gb_ref[:, 0:D])
        g_b = _sigmoid(_bf(u_ref, C_GB, D) + gb_ref[:, D:2 * D])
        merged = g_a * ya + g_b * yb
        xo_ref[...] = x_ref[...] + _dot(merged.astype(BF16), wout_ref[...])
        ya_ref[...] = ya.astype(BF16)
        yb_ref[...] = yb.astype(BF16)

    u_spec, ukv_prev, uvc_prev, ucc_prev = _mixer_specs(t, tm, n_tiles, lambda g: g)
    tok = pl.BlockSpec((tm, D), lambda g: (g, 0))
    consts = [c[k] for k in cn]
    return pl.pallas_call(
        body, name=name, grid=(n_tiles,),
        in_specs=[tok, u_spec, ukv_prev, uvc_prev, ucc_prev, _full((8, D)), _full((1, D)), _full((1, KVW)),
                  pl.BlockSpec(memory_space=pltpu.SMEM), _full((1, 2 * D)), _full((D, D)), _full((D, D)), _full((D, D))]
                 + [_full(a.shape) for a in consts],
        out_specs=[tok, tok, tok],
        out_shape=[jax.ShapeDtypeStruct((t, D), F32)] + [jax.ShapeDtypeStruct((t, D), BF16)] * 2,
        compiler_params=_cparams(("arbitrary",)),
    )(x, u, u, u, u, cw, qg, kg, sinks, gb, wco, wao, wout, *consts)


def mixer_bwd(dout, u, ya, yb, cw, qg, kg, sinks, gb, wco, wao, wout, c, name):
    t = dout.shape[0]
    tm = min(256, t)
    n_tiles = t // tm
    nb = tm // BLK
    kb = tm + BLK
    cn = sorted(c)

    def body(dout_ref, u_ref, ukv_prev, uvc_prev, ucc_prev, ya_ref, yb_ref, cw_ref, qg_ref, kg_ref, sinks_ref, gb_ref,
             wco_ref, wao_ref, wout_ref, *rest):
        cref = dict(zip(cn, rest[:len(cn)]))
        (du_ref, small_ref, merged_ref, yc_ref, ob_ref, dya_ref, dyb_ref, carry_kv, carry_conv) = rest[len(cn):]
        g = pl.program_id(0)
        is_first = g == n_tiles - 1

        @pl.when(g == 0)
        def _():
            carry_kv[...] = jnp.zeros_like(carry_kv)
            carry_conv[...] = jnp.zeros_like(carry_conv)
            small_ref[...] = jnp.zeros_like(small_ref)

        dout = dout_ref[...]
        dout_b = dout.astype(BF16)
        ya_v = ya_ref[...].astype(F32)
        yb_v = yb_ref[...].astype(F32)
        g_a = _sigmoid(_bf(u_ref, C_GA, D) + gb_ref[:, 0:D])
        g_b = _sigmoid(_bf(u_ref, C_GB, D) + gb_ref[:, D:2 * D])
        merged = g_a * ya_v + g_b * yb_v
        dmerged = _dot_nt(dout_b, wout_ref[...])
        merged_ref[...] = merged.astype(BF16)
        dya = dmerged * g_a
        dyb = dmerged * g_b
        dgl_a = dmerged * ya_v * g_a * (1.0 - g_a)
        dgl_b = dmerged * yb_v * g_b * (1.0 - g_b)
        du_ref[:, C_GA:C_GA + D] = dgl_a.astype(BF16)
        du_ref[:, C_GB:C_GB + D] = dgl_b.astype(BF16)
        small_ref[SM_GATE:SM_GATE + 1, 0:D] += jnp.sum(dgl_a, axis=0, keepdims=True)
        small_ref[SM_GATE_B:SM_GATE_B + 1, 0:D] += jnp.sum(dgl_b, axis=0, keepdims=True)

        p, p1, p2, conv = _conv_fwd(u_ref, uvc_prev, ucc_prev, cw_ref, is_first, tm)
        zc = _bf(u_ref, C_ZC, D)
        bc = _bf(u_ref, C_BC, D)
        sg = _sigmoid(zc)
        sc = zc * sg
        yc = bc * conv * sc
        dya_b = dya.astype(BF16)
        yc_ref[...] = yc.astype(BF16)
        dya_ref[...] = dya_b
        dyc = _dot_nt(dya_b, wco_ref[...])
        du_ref[:, C_BC:C_BC + D] = (dyc * conv * sc).astype(BF16)
        du_ref[:, C_ZC:C_ZC + D] = (dyc * bc * conv * (sg * (1.0 + zc * (1.0 - sg)))).astype(BF16)
        dconv = dyc * bc * sc
        small_ref[SM_CONV + 2:SM_CONV + 3, 0:D] += jnp.sum(dconv * p, axis=0, keepdims=True)
        small_ref[SM_CONV + 1:SM_CONV + 2, 0:D] += jnp.sum(dconv * p1, axis=0, keepdims=True)
        small_ref[SM_CONV:SM_CONV + 1, 0:D] += jnp.sum(dconv * p2, axis=0, keepdims=True)
        nxt = carry_conv[...]
        d1 = _shift_up(dconv, 1, nxt)
        d2 = _shift_up(dconv, 2, nxt)
        carry_conv[...] = dconv[0:8, :]
        cw = cw_ref[...]
        dp = cw[2:3, :] * dconv + cw[1:2, :] * d1 + cw[0:1, :] * d2
        du_ref[:, C_CC:C_CC + D] = (dp * _bf(u_ref, C_VC, D)).astype(BF16)
        du_ref[:, C_VC:C_VC + D] = (dp * _bf(u_ref, C_CC, D)).astype(BF16)

        dyb_b = dyb.astype(BF16)
        dob = _dot_nt(dyb_b, wao_ref[...])
        za = _bf(u_ref, C_ZA, D)
        sga = _sigmoid(za)
        sa = za * sga
        do = dob * sa
        qhat, rq, qn, khat, rk, kt, vt = _attn_inputs(u_ref, ukv_prev, qg_ref, kg_ref, cref, tm)
        valid, valid_first, lane_grp, row_grp = _attn_masks(is_first)
        lane16 = lax.broadcasted_iota(jnp.int32, (1, D), 1)
        dsink_row = jnp.zeros((1, D), F32)
        o_cols, dq_cols, dk4, dv4 = [], [], [], []
        for h in range(N_KV):
            sink = _sink_col(sinks_ref, h, row_grp)
            dsink_col = jnp.zeros((GROUP * BLK, 1), F32)
            o_rows, dq_rows, dk_parts, dv_parts = [], [], [], []
            for b in range(nb):
                rows = slice(b * BLK, (b + 1) * BLK)
                band = slice(b * BLK, (b + 2) * BLK)
                cols = slice(h * KVW, (h + 1) * KVW)
                qs = _stack_groups(qn[rows, cols], lane_grp)
                pn, ps = _softmax_block(qs, kt[h][band], valid_first if b == 0 else valid, sink)
                pn_b = pn.astype(BF16)
                o_rows.append(_unstack_groups(_dot(pn_b, vt[h][band]), lane_grp))
                dos = _stack_groups(do[rows, cols], lane_grp).astype(BF16)
                dpn = _dot_nt(dos, vt[h][band])
                delta = jnp.sum(pn * dpn, axis=-1, keepdims=True)
                ds = (pn * (dpn - delta)).astype(BF16)
                dsink_col = dsink_col - ps * delta
                dq_rows.append(_unstack_groups(_dot(ds, kt[h][band]), lane_grp))
                dk_parts.append(_dot_tn(ds, qs))
                dv_parts.append(_dot_tn(pn_b, dos))
            o_cols.append(jnp.concatenate(o_rows, axis=0))
            dq_cols.append(jnp.concatenate(dq_rows, axis=0))
            dk4.append(_band_sum(dk_parts))
            dv4.append(_band_sum(dv_parts))
            for gi in range(GROUP):
                tot = jnp.sum(dsink_col[gi * BLK:(gi + 1) * BLK, :], axis=0, keepdims=True)
                dsink_row = dsink_row + jnp.where(lane16 == GROUP * h + gi, tot, 0.0)
        small_ref[SM_SINK:SM_SINK + 1, :] += dsink_row

        o = jnp.concatenate(o_cols, axis=1)
        ob_ref[...] = (o * sa).astype(BF16)
        dyb_ref[...] = dyb_b
        du_ref[:, C_ZA:C_ZA + D] = (dob * o * (sga * (1.0 + za * (1.0 - sga)))).astype(BF16)

        dqn = jnp.concatenate(dq_cols, axis=1)
        small_ref[SM_QG:SM_QG + 1, 0:D] += SCALE * jnp.sum(dqn * qhat, axis=0, keepdims=True)
        dqh = dqn * qg_ref[...]
        dq = rq * (dqh - qhat * _head_mean(dqh * qhat, cref["sel_q"][...], cref["exp_q"][...]))
        du_ref[:, C_Q:C_Q + D] = dq.astype(BF16)

        dkn_band = jnp.zeros((kb, KVW), F32)
        dv_band = jnp.zeros((kb, KVW), F32)
        for h in range(N_KV):
            dkn_band = dkn_band + _dot2(dk4[h], cref["rep_t"][h])
            dv_band = dv_band + _dot2(dv4[h], cref["rep_t"][h])
        carried = carry_kv[...]
        pad = jnp.zeros((tm - BLK, KVW), F32)
        if nb > 1:
            dkn = dkn_band[BLK:, :] + jnp.concatenate([pad, carried[:, 0:KVW]], axis=0)
            dv = dv_band[BLK:, :] + jnp.concatenate([pad, carried[:, KVW:2 * KVW]], axis=0)
        else:
            dkn = dkn_band[BLK:, :] + carried[:, 0:KVW]
            dv = dv_band[BLK:, :] + carried[:, KVW:2 * KVW]
        carry_kv[:, 0:KVW] = dkn_band[0:BLK, :]
        carry_kv[:, KVW:2 * KVW] = dv_band[0:BLK, :]
        khat_t = khat[BLK:, :]
        small_ref[SM_KG:SM_KG + 1, 0:KVW] += jnp.sum(dkn * khat_t, axis=0, keepdims=True)
        dkh = dkn * kg_ref[...]
        dk = rk[BLK:, :] * (dkh - khat_t * _head_mean(dkh * khat_t, cref["sel_k"][...], cref["exp_k"][...]))
        du_ref[:, C_K:C_K + KVW] = dk.astype(BF16)
        du_ref[:, C_V:C_V + KVW] = dv.astype(BF16)

    rev = lambda g: n_tiles - 1 - g
    u_spec, ukv_prev, uvc_prev, ucc_prev = _mixer_specs(t, tm, n_tiles, rev)
    tok = pl.BlockSpec((tm, D), lambda g: (rev(g), 0))
    consts = [c[k] for k in cn]
    wspec = _full((D, D))
    return pl.pallas_call(
        body, name=name, grid=(n_tiles,),
        in_specs=[tok, u_spec, ukv_prev, uvc_prev, ucc_prev, tok, tok, _full((8, D)), _full((1, D)), _full((1, KVW)),
                  pl.BlockSpec(memory_space=pltpu.SMEM), _full((1, 2 * D)), wspec, wspec, wspec]
                 + [_full(a.shape) for a in consts],
        out_specs=[pl.BlockSpec((tm, IN_COLS), lambda g: (rev(g), 0)), _full((SM_ROWS, D))] + [tok] * 5,
        out_shape=[jax.ShapeDtypeStruct((t, IN_COLS), BF16), jax.ShapeDtypeStruct((SM_ROWS, D), F32)]
                  + [jax.ShapeDtypeStruct((t, D), BF16)] * 5,
        scratch_shapes=[pltpu.VMEM((BLK, 2 * KVW), F32), pltpu.VMEM((8, D), F32)],
        compiler_params=_cparams(("arbitrary",)),
    )(dout, u, u, u, u, ya, yb, cw, qg, kg, sinks, gb, wco, wao, wout, *consts)


def matmul_tn(a, b, name, after=None):
    t, m = a.shape
    tk = min(1024, t)
    mb = 2176 if m == IN_COLS else m
    nk = t // tk

    def body(a_ref, b_ref, *rest):
        o_ref, acc = rest[-2:]
        k = pl.program_id(1)
        prod = _dot_tn(a_ref[...].astype(BF16), b_ref[...].astype(BF16))

        @pl.when(k == 0)
        def _():
            acc[...] = prod

        @pl.when(k > 0)
        def _():
            acc[...] += prod

        @pl.when(k == nk - 1)
        def _():
            o_ref[...] = acc[...].astype(BF16)

    return pl.pallas_call(
        body, name=name, grid=(m // mb, nk),
        in_specs=[pl.BlockSpec((tk, mb), lambda j, k: (k, j)), pl.BlockSpec((tk, D), lambda j, k: (k, 0))]
                 + ([] if after is None else [ANY]),
        out_specs=pl.BlockSpec((mb, D), lambda j, k: (j, 0)),
        out_shape=jax.ShapeDtypeStruct((m, D), BF16),
        scratch_shapes=[pltpu.VMEM((mb, D), F32)],
        compiler_params=_cparams(("arbitrary", "arbitrary")),
    )(a, b, *([] if after is None else [after]))


def inproj_bwd_x(du, w, x, ng, dout, name):
    t = x.shape[0]
    tm = min(1024, t)
    kc = 2176
    nk = IN_COLS // kc

    def body(du_ref, w_ref, x_ref, ng_ref, dout_ref, dx_ref, dng_ref, acc):
        i = pl.program_id(0)
        k = pl.program_id(1)
        prod = _dot(du_ref[...], w_ref[...])

        @pl.when(k == 0)
        def _():
            acc[...] = prod

        @pl.when(k > 0)
        def _():
            acc[...] += prod

        @pl.when((i == 0) & (k == 0))
        def _():
            dng_ref[...] = jnp.zeros_like(dng_ref)

        @pl.when(k == nk - 1)
        def _():
            dh = acc[...]
            xf = x_ref[...]
            r = lax.rsqrt(jnp.mean(xf * xf, axis=-1, keepdims=True) + EPS)
            xhat = xf * r
            dng_ref[0:1, :] += jnp.sum(dh * xhat, axis=0, keepdims=True)
            dxh = dh * ng_ref[...]
            dx_ref[...] = dout_ref[...] + r * (dxh - xhat * jnp.mean(dxh * xhat, axis=-1, keepdims=True))

    tok = pl.BlockSpec((tm, D), lambda i, k: (i, 0))
    return pl.pallas_call(
        body, name=name, grid=(t // tm, nk),
        in_specs=[pl.BlockSpec((tm, kc), lambda i, k: (i, k)), pl.BlockSpec((kc, D), lambda i, k: (k, 0)), tok,
                  pl.BlockSpec((1, D), lambda i, k: (0, 0)), tok],
        out_specs=[tok, pl.BlockSpec((8, D), lambda i, k: (0, 0))],
        out_shape=[jax.ShapeDtypeStruct((t, D), F32), jax.ShapeDtypeStruct((8, D), F32)],
        scratch_shapes=[pltpu.VMEM((tm, D), F32)],
        compiler_params=_cparams(("arbitrary", "arbitrary")),
    )(du, w, x, ng, dout)


def loss_head(y, target, name):
    t = y.shape[0]
    tm = min(1024, t)

    def body(y_ref, t_ref, dy_ref, loss_ref):
        @pl.when(pl.program_id(0) == 0)
        def _():
            loss_ref[...] = jnp.zeros_like(loss_ref)
        err = y_ref[...] - t_ref[...]
        dy_ref[...] = err * (1.0 / D)
        part = jnp.sum(jnp.sum(err * err, axis=-1, keepdims=True) * (1.0 / D), axis=0, keepdims=True)
        loss_ref[...] += 0.5 * part

    tok = pl.BlockSpec((tm, D), lambda i: (i, 0))
    return pl.pallas_call(
        body, name=name, grid=(t // tm,), in_specs=[tok, tok],
        out_specs=[tok, pl.BlockSpec((8, 128), lambda i: (0, 0))],
        out_shape=[jax.ShapeDtypeStruct((t, D), F32), jax.ShapeDtypeStruct((8, 128), F32)],
        compiler_params=_cparams(("arbitrary",)),
    )(y, target)


def layer_operands(l, norm_g, conv_w_full, q_norm_g, k_norm_g, sinks, gate_b, w_in_b, wco_b, wao_b, wout_b):
    return dict(
        ng=norm_g[l][None, :], cw=jnp.pad(conv_w_full[l], ((0, 5), (0, 0))),
        qg=jnp.tile(q_norm_g[l] * SCALE, D // HEAD)[None, :], kg=jnp.tile(k_norm_g[l], N_KV)[None, :],
        sinks=sinks[l][None, :], gb=gate_b[l][None, :],
        w_in=w_in_b[l], wco=wco_b[l], wao=wao_b[l], wout=wout_b[l])


def layer_bwd(dout, saved, lw, c, l, send_off):
    x, u, h, ya, yb = saved
    du, small, merged, yc, ob, dya, dyb = mixer_bwd(dout, u, ya, yb, lw["cw"], lw["qg"], lw["kg"], lw["sinks"], lw["gb"],
                                                    lw["wco"], lw["wao"], lw["wout"], c, f"mixer_bwd_{l}")
    grads = dict(w_in=matmul_tn(du, h, f"dw_in_{l}"), small=small)
    token = send_off(grads, False)
    grads["wout"] = matmul_tn(merged, dout, f"dw_out_{l}", after=token)
    grads["wco"] = matmul_tn(yc, dya, f"dw_conv_out_{l}")
    grads["wao"] = matmul_tn(ob, dyb, f"dw_attn_out_{l}")
    token = send_off(grads, True)
    dx, grads["dng"] = inproj_bwd_x(du, lw["w_in"], x, lw["ng"] + token[0:1, 0:1], dout, f"inproj_bwd_{l}")
    return dx, grads


MESH = pl.DeviceIdType.MESH
ANY = pl.BlockSpec(memory_space=pl.ANY)


def _place():
    return lax.axis_index("x"), lax.axis_index("y"), lax.axis_index("c")


def all_gather(arrs, after, name):
    n = len(arrs)

    def body(*refs):
        ins, outs = refs[:n], refs[n + 1:2 * n + 1]
        send_sems, recv_sems, local_sems = refs[2 * n + 1:]
        x, y, c = _place()
        me, sibling = (x, y, c), (x, y, 1 - c)
        chips = [(1 - x, y), (x, 1 - y), (1 - x, 1 - y)]

        def slot(a, block):
            px, py, pc = block
            return outs[a].at[4 * px + 2 * py + pc]

        def copy(a, k, block, to, src=None):
            return pltpu.make_async_remote_copy(
                src_ref=slot(a, block) if src is None else src, dst_ref=slot(a, block),
                send_sem=send_sems.at[a, k], recv_sem=recv_sems.at[a, k], device_id=to, device_id_type=MESH)

        mine = [pltpu.make_async_copy(ins[a], slot(a, me), local_sems.at[a]) for a in range(n)]
        for cp in mine:
            cp.start()
        first = []
        for a in range(n):
            first.append(copy(a, 0, me, sibling, src=ins[a]))
            first += [copy(a, 1 + j, me, (*chip, c), src=ins[a]) for j, chip in enumerate(chips)]
        for cp in first:
            cp.start()
        passed = []
        for j, chip in enumerate(chips):
            for a in range(n):
                copy(a, 1 + j, (*chip, c), me).wait_recv()
                passed.append(copy(a, 4 + j, (*chip, c), sibling))
                passed[-1].start()
        for a in range(n):
            copy(a, 0, sibling, me).wait_recv()
            for j, chip in enumerate(chips):
                copy(a, 4 + j, (*chip, 1 - c), me).wait_recv()
        for cp in first + passed:
            cp.wait_send()
        for cp in mine:
            cp.wait()

    return pl.pallas_call(
        body, name=name, in_specs=[ANY] * (n + 1), out_specs=[ANY] * n,
        out_shape=[jax.ShapeDtypeStruct((N_DEV,) + a.shape, a.dtype) for a in arrs],
        scratch_shapes=[pltpu.SemaphoreType.DMA((n, 7)), pltpu.SemaphoreType.DMA((n, 7)), pltpu.SemaphoreType.DMA((n,))],
    )(*arrs, after)


HBM_SPEC = pl.BlockSpec(memory_space=pltpu.HBM)
SEM_SPEC = pl.BlockSpec(memory_space=pltpu.SEMAPHORE)
EFFECT = pltpu.SideEffectType.DATAFLOW_SIDE_EFFECTING


ALL_PEERS = (1, 2, 3, 4, 5, 6, 7)


def _flip(place, k):
    x, y, c = place
    return (1 - x if k & 4 else x, 1 - y if k & 2 else y, 1 - c if k & 1 else c)


def _slot(place):
    return 4 * place[0] + 2 * place[1] + place[2]


def _exchange_copies(ins, lands, send_sems, recv_sems, scatter, flips, arriving):
    me = _place()
    out = []
    for a in range(len(ins)):
        for i, k in enumerate(flips):
            peer = _flip(me, k)
            out.append(pltpu.make_async_remote_copy(
                src_ref=ins[a].at[_slot(peer)] if scatter else ins[a],
                dst_ref=lands[a].at[_slot(peer) if arriving else _slot(me)],
                send_sem=send_sems.at[a * len(flips) + i], recv_sem=recv_sems.at[a * len(flips) + i],
                device_id=peer, device_id_type=MESH))
    return out


def exchange_start(arrs, scatter, flips, after, name):
    n = len(arrs)
    lands = [lax.empty(a.shape if scatter else (N_DEV,) + a.shape, a.dtype) for a in arrs]

    def body(*refs):
        ins, lz = refs[:n], refs[n:2 * n]
        send_sems, recv_sems = refs[2 * n + 1], refs[2 * n + 2]
        token = refs[-1]
        for cp in _exchange_copies(ins, lz, send_sems, recv_sems, scatter, flips, False):
            cp.start()
        token[...] = jnp.zeros_like(token)

    sems = pltpu.SemaphoreType.DMA((n * len(flips),))
    res = pl.pallas_call(
        body, name=name,
        out_shape=(sems, sems, *[pltpu.HBM(a.shape, a.dtype) for a in arrs], *[pltpu.HBM(a.shape, a.dtype) for a in lands],
                   jax.ShapeDtypeStruct((8, 128), F32)),
        in_specs=[HBM_SPEC] * (2 * n) + [ANY],
        out_specs=(SEM_SPEC, SEM_SPEC, *[HBM_SPEC] * (2 * n), pl.BlockSpec(memory_space=pltpu.VMEM)),
        input_output_aliases={i: 2 + i for i in range(2 * n)},
        compiler_params=pltpu.CompilerParams(has_side_effects=EFFECT),
    )(*[pltpu.with_memory_space_constraint(a, pltpu.HBM) for a in arrs],
      *[pltpu.with_memory_space_constraint(a, pltpu.HBM) for a in lands], after)
    return dict(send=res[0], recv=res[1], srcs=res[2:2 + n], lands=res[2 + n:2 + 2 * n], token=res[-1], scatter=scatter,
                flips=flips)


def exchange_wait(state, after, name):
    n = len(state["srcs"])

    def body(*refs):
        ins, lz = refs[:n], refs[n:2 * n]
        send_sems, recv_sems = refs[2 * n], refs[2 * n + 1]
        for cp in _exchange_copies(ins, lz, send_sems, recv_sems, state["scatter"], state["flips"], True):
            cp.wait_send()
            cp.wait_recv()

    both = list(state["srcs"]) + list(state["lands"])
    res = pl.pallas_call(
        body, name=name, out_shape=tuple(pltpu.HBM(a.shape, a.dtype) for a in both),
        in_specs=[HBM_SPEC] * (2 * n) + [SEM_SPEC, SEM_SPEC, ANY], out_specs=tuple([HBM_SPEC] * (2 * n)),
        input_output_aliases={i: i for i in range(2 * n)},
        compiler_params=pltpu.CompilerParams(has_side_effects=EFFECT),
    )(*both, state["send"], state["recv"], after)
    return res[:n], res[n:]


OTHER_CHIPS = (2, 4, 6)
SAME_CORE = (1, 2, 4, 6)


def _forward_copies(lands, send_sems, recv_sems, arriving):
    me = _place()
    sibling = _flip(me, 1)
    origin = sibling if arriving else me
    return [pltpu.make_async_remote_copy(
        src_ref=lands[a].at[_slot(_flip(origin, k))], dst_ref=lands[a].at[_slot(_flip(origin, k))],
        send_sem=send_sems.at[a * len(OTHER_CHIPS) + i], recv_sem=recv_sems.at[a * len(OTHER_CHIPS) + i],
        device_id=sibling, device_id_type=MESH) for a in range(len(lands)) for i, k in enumerate(OTHER_CHIPS)]


def forward_start(lands, after, name):
    n = len(lands)

    def body(*refs):
        lz = refs[:n]
        send_sems, recv_sems = refs[n + 1], refs[n + 2]
        token = refs[-1]
        for cp in _forward_copies(lz, send_sems, recv_sems, False):
            cp.start()
        token[...] = jnp.zeros_like(token)

    sems = pltpu.SemaphoreType.DMA((n * len(OTHER_CHIPS),))
    res = pl.pallas_call(
        body, name=name,
        out_shape=(sems, sems, *[pltpu.HBM(a.shape, a.dtype) for a in lands], jax.ShapeDtypeStruct((8, 128), F32)),
        in_specs=[HBM_SPEC] * n + [ANY],
        out_specs=(SEM_SPEC, SEM_SPEC, *[HBM_SPEC] * n, pl.BlockSpec(memory_space=pltpu.VMEM)),
        input_output_aliases={i: 2 + i for i in range(n)},
        compiler_params=pltpu.CompilerParams(has_side_effects=EFFECT),
    )(*[pltpu.with_memory_space_constraint(a, pltpu.HBM) for a in lands], after)
    return dict(send=res[0], recv=res[1], lands=res[2:2 + n], token=res[-1])


def forward_wait(state, after, name):
    n = len(state["lands"])

    def body(*refs):
        lz = refs[:n]
        send_sems, recv_sems = refs[n], refs[n + 1]
        for cp in _forward_copies(lz, send_sems, recv_sems, False):
            cp.wait_send()
        for cp in _forward_copies(lz, send_sems, recv_sems, True):
            cp.wait_recv()

    return pl.pallas_call(
        body, name=name, out_shape=tuple(pltpu.HBM(a.shape, a.dtype) for a in state["lands"]),
        in_specs=[HBM_SPEC] * n + [SEM_SPEC, SEM_SPEC, ANY], out_specs=tuple([HBM_SPEC] * n),
        input_output_aliases={i: i for i in range(n)},
        compiler_params=pltpu.CompilerParams(has_side_effects=EFFECT),
    )(*state["lands"], state["send"], state["recv"], after)


def adamw(w, m, v, parts, name):
    r, cdim = w.shape
    n_parts = parts.shape[0]
    rb = 256 if r % 256 == 0 else (SHARD_COLS // 4 if r == SHARD_COLS else r)

    def body(w_ref, m_ref, v_ref, p_ref, g_ref, d_ref, mo_ref, vo_ref):
        g = p_ref[0].astype(F32)
        for i in range(1, n_parts):
            g = g + p_ref[i].astype(F32)
        m_new = ADAM_B1 * m_ref[...] + (1.0 - ADAM_B1) * g
        v_new = ADAM_B2 * v_ref[...] + (1.0 - ADAM_B2) * (g * g)
        m_hat = m_new / (1.0 - ADAM_B1 ** ADAM_STEP)
        v_hat = v_new / (1.0 - ADAM_B2 ** ADAM_STEP)
        g_ref[...] = g
        d_ref[...] = -ADAM_LR * (m_hat / (jnp.sqrt(v_hat) + ADAM_EPS) + ADAM_WD * w_ref[...])
        mo_ref[...] = m_new
        vo_ref[...] = v_new

    blk = pl.BlockSpec((rb, cdim), lambda i: (i, 0))
    return pl.pallas_call(
        body, name=name, grid=(r // rb,),
        in_specs=[blk, blk, blk, pl.BlockSpec((n_parts, rb, cdim), lambda i: (0, i, 0))],
        out_specs=[blk] * 4, out_shape=[jax.ShapeDtypeStruct((r, cdim), F32)] * 4,
        compiler_params=_cparams(("arbitrary",)),
    )(w, m, v, parts)


def adamw_layers(w, m, v, lands, srcs, me, lo, prev, name):
    nl_all, r, cdim = w.shape
    nl = len(lands)
    rb = 256 if r % 256 == 0 else (SHARD_COLS // 4 if r == SHARD_COLS else r)
    nblk = r // rb
    n_prev = 0 if prev is None else 4

    def body(me_ref, w_ref, m_ref, v_ref, *rest):
        land_refs, src_refs = rest[:nl], rest[nl:2 * nl]
        g_ref, d_ref, mo_ref, vo_ref = rest[2 * nl + n_prev:]
        for k in range(nl):
            @pl.when(pl.program_id(0) == k)
            def _(k=k):
                own = src_refs[k][...].astype(F32)
                g = jnp.where(me_ref[0] == 0, own, land_refs[k][0].astype(F32))
                for i in range(1, N_DEV):
                    g = g + jnp.where(me_ref[0] == i, own, land_refs[k][i].astype(F32))
                m_new = ADAM_B1 * m_ref[...] + (1.0 - ADAM_B1) * g
                v_new = ADAM_B2 * v_ref[...] + (1.0 - ADAM_B2) * (g * g)
                m_hat = m_new / (1.0 - ADAM_B1 ** ADAM_STEP)
                v_hat = v_new / (1.0 - ADAM_B2 ** ADAM_STEP)
                g_ref[...] = g
                d_ref[...] = -ADAM_LR * (m_hat / (jnp.sqrt(v_hat) + ADAM_EPS) + ADAM_WD * w_ref[...])
                mo_ref[...] = m_new
                vo_ref[...] = v_new

    blk = pl.BlockSpec((None, rb, cdim), lambda l, i, me_ref: (lo + l, i, 0))

    def rows(l, i, k):
        return jnp.where(l < k, 0, jnp.where(l == k, i, nblk - 1))

    land_specs = [pl.BlockSpec((N_DEV, rb, cdim), lambda l, i, me_ref, k=k: (0, rows(l, i, k), 0)) for k in range(nl)]
    src_specs = [pl.BlockSpec((None, rb, cdim), lambda l, i, me_ref, k=k: (me_ref[0], rows(l, i, k), 0)) for k in range(nl)]
    return pl.pallas_call(
        body, name=name,
        grid_spec=pltpu.PrefetchScalarGridSpec(
            num_scalar_prefetch=1, grid=(nl, nblk),
            in_specs=[blk, blk, blk] + land_specs + src_specs + [ANY] * n_prev, out_specs=[blk] * 4),
        out_shape=[jax.ShapeDtypeStruct((nl_all, r, cdim), F32)] * 4,
        input_output_aliases={4 + 2 * nl + j: j for j in range(n_prev)},
        compiler_params=_cparams(("arbitrary", "arbitrary")),
    )(me, w, m, v, *lands, *srcs, *([] if prev is None else prev))


def small_sum(parts, fold, name):
    rows = parts.shape[1]

    def dot3(xv, sel):
        out = jnp.zeros((xv.shape[0], sel.shape[1]), F32)
        for _ in range(3):
            hi = xv.astype(BF16)
            out = out + _dot(hi, sel)
            xv = xv - hi.astype(F32)
        return out

    def body(p_ref, fold_ref, o_ref):
        tot = p_ref[0]
        for i in range(1, N_DEV):
            tot = tot + p_ref[i]
        o_ref[...] = tot
        for l in range(rows // SM_ROWS):
            blk = tot[l * SM_ROWS:l * SM_ROWS + 8, 0:D]
            folded = dot3(blk, fold_ref[...])
            o_ref[l * SM_ROWS + SM_QG_FOLDED:l * SM_ROWS + SM_QG_FOLDED + 1, 0:128] = folded[SM_QG:SM_QG + 1, :]
            o_ref[l * SM_ROWS + SM_KG_FOLDED:l * SM_ROWS + SM_KG_FOLDED + 1, 0:128] = folded[SM_KG:SM_KG + 1, :]

    return pl.pallas_call(
        body, name=name, out_shape=jax.ShapeDtypeStruct((rows, D), F32),
        compiler_params=_cparams(None),
    )(parts, fold)


def kernel(x, norm_g, w_in, conv_w, q_norm_g, k_norm_g, sinks, w_conv_out, w_attn_out, gate_b, w_out, loss_target, m_norm_g, m_w_in, m_conv_w, m_q_norm_g, m_k_norm_g, m_sinks, m_w_conv_out, m_w_attn_out, m_gate_b, m_w_out, v_norm_g, v_w_in, v_conv_w, v_q_norm_g, v_k_norm_g, v_sinks, v_w_conv_out, v_w_attn_out, v_gate_b, v_w_out):
    c = _selectors()
    me = 4 * lax.axis_index("x") + 2 * lax.axis_index("y") + lax.axis_index("c")

    w_in_t, m_w_in_t, v_w_in_t = (jnp.swapaxes(a, 1, 2) for a in (w_in, m_w_in, v_w_in))

    def shards(l):
        return [w_in_t[l].astype(BF16), w_conv_out[l].astype(BF16), w_attn_out[l].astype(BF16), w_out[l].astype(BF16)]

    def ici_start(l, after):
        return exchange_start(shards(l), False, SAME_CORE, after, f"gather_start_{l}")

    h = x[0]
    saved, lws = [], []
    lands = all_gather(shards(0) + [conv_w], h, "gather_0")
    conv_full = jnp.transpose(lands[4], (1, 2, 0, 3)).reshape(DEPTH, 3, D)
    ici = ici_start(1, lands[1])
    for l in range(DEPTH):
        lws.append(layer_operands(l, norm_g, conv_full, q_norm_g, k_norm_g, sinks, gate_b,
                                  {l: lands[0].reshape(IN_COLS, D)}, {l: lands[1].reshape(D, D)},
                                  {l: lands[2].reshape(D, D)}, {l: lands[3].reshape(D, D)}))
        ng_l = lws[l]["ng"] + ici["token"][0:1, 0:1] if l == 0 else lws[l]["ng"]
        u, hb = inproj_fwd(h, ng_l, lws[l]["w_in"], f"inproj_fwd_{l}")
        gb_l = lws[l]["gb"]
        if l + 1 < DEPTH:
            mine, arrived = exchange_wait(ici, u, f"gather_wait_{l + 1}")
            chip = forward_start(arrived, mine[0], f"gather_forward_start_{l + 1}")
            started = chip["token"]
            if l + 2 < DEPTH:
                ici = ici_start(l + 2, started)
                started = ici["token"]
            gb_l = gb_l + started[0:1, 0:1]
        x_in = h
        h, ya, yb = mixer_fwd(x_in, u, lws[l]["cw"], lws[l]["qg"], lws[l]["kg"], lws[l]["sinks"], gb_l,
                              lws[l]["wco"], lws[l]["wao"], lws[l]["wout"], c, f"mixer_fwd_{l}")
        saved.append((x_in, u, hb, ya, yb))
        if l + 1 < DEPTH:
            lands = forward_wait(chip, h, f"gather_forward_wait_{l + 1}")
            lands = [lax.dynamic_update_index_in_dim(land, src, me, 0) for land, src in zip(lands, mine)]
    dh, loss_part = loss_head(h, loss_target[0], "loss_head")

    grads, scatters = [None] * DEPTH, [[] for _ in range(DEPTH)]
    for l in reversed(range(DEPTH)):
        def send_off(g, done, l=l):
            rest = [g[k].reshape(N_DEV, SHARD_ROWS, D) for k in ("wco", "wao", "wout")] if done else []
            first = [g["w_in"].reshape(N_DEV, SHARD_COLS, D)] if done == (l > 0) else []
            if not first + rest:
                return None
            tag = f"{l}" if l > 0 else ("0_rest" if done else "0_in")
            scatters[l].append(exchange_start(first + rest, True, ALL_PEERS, g["small"], f"scatter_start_{tag}"))
            return scatters[l][-1]["token"]
        dh, grads[l] = layer_bwd(dh, saved[l], lws[l], c, l, send_off)

    me1 = me.astype(jnp.int32).reshape(1)
    mine, lands = {}, {}
    for l in (3, 2, 1):
        mine[l], lands[l] = exchange_wait(scatters[l][0], dh, f"scatter_wait_{l}")
    weights = [(w_in_t, m_w_in_t, v_w_in_t, "w_in"), (w_conv_out, m_w_conv_out, v_w_conv_out, "w_conv_out"),
               (w_attn_out, m_w_attn_out, v_w_attn_out, "w_attn_out"), (w_out, m_w_out, v_w_out, "w_out")]
    upd = [adamw_layers(w, m, v, [lands[l][i] for l in (1, 2, 3)], [mine[l][i] for l in (1, 2, 3)], me1, 1, None,
                        f"adamw_{n}_upper") for i, (w, m, v, n) in enumerate(weights)]
    blocks = []
    for l in range(DEPTH):
        blk = grads[l]["small"]
        blk = blk.at[SM_NORM, 0:D].set(grads[l]["dng"][0])
        if l == 0:
            blk = blk.at[SM_LOSS, 0:128].set(loss_part[0])
        blocks.append(blk)
    small_x = exchange_start([jnp.concatenate(blocks, axis=0)], False, ALL_PEERS, grads[0]["dng"], "gather_small_start")

    m_in, l_in = exchange_wait(scatters[0][0], upd[0][0], "scatter_wait_0_in")
    upd[0] = adamw_layers(*weights[0][:3], [l_in[0]], [m_in[0]], me1, 0, upd[0], "adamw_w_in_0")
    m_rest, l_rest = exchange_wait(scatters[0][1], upd[0][0], "scatter_wait_0_rest")
    for i in (1, 2, 3):
        upd[i] = adamw_layers(*weights[i][:3], [l_rest[i - 1]], [m_rest[i - 1]], me1, 0, upd[i], f"adamw_{weights[i][3]}_0")
    u_in = [jnp.swapaxes(o, 1, 2) for o in upd[0]]
    u_co, u_ao, u_out = upd[1], upd[2], upd[3]

    mine_s, lands_s = exchange_wait(small_x, u_out[0], "gather_small_wait")
    gathered = lax.dynamic_update_index_in_dim(lands_s[0], mine_s[0], me, 0)
    tot = small_sum(gathered, c["fold"], "small_sum")
    tot = tot.reshape(DEPTH, SM_ROWS, D)
    loss = tot[0, SM_LOSS, 0]

    def update_small(w, m, v, g, name):
        return adamw(w, m, v, g[None], name)

    u_ng = update_small(norm_g, m_norm_g, v_norm_g, tot[:, SM_NORM, 0:D], "adamw_norm_g")
    u_qg = update_small(q_norm_g, m_q_norm_g, v_q_norm_g, tot[:, SM_QG_FOLDED, 0:HEAD], "adamw_q_norm_g")
    u_kg = update_small(k_norm_g, m_k_norm_g, v_k_norm_g, tot[:, SM_KG_FOLDED, 0:HEAD], "adamw_k_norm_g")
    u_sk = update_small(sinks, m_sinks, v_sinks, tot[:, SM_SINK, 0:16], "adamw_sinks")
    g_gate = jnp.concatenate([tot[:, SM_GATE, :], tot[:, SM_GATE_B, :]], axis=1)
    u_gb = update_small(gate_b, m_gate_b, v_gate_b, g_gate, "adamw_gate_b")
    g_conv = lax.dynamic_slice_in_dim(tot[:, SM_CONV:SM_CONV + 3, 0:D], me * SHARD_ROWS, SHARD_ROWS, axis=2)
    u_cw = [o.reshape(DEPTH, 3, SHARD_ROWS) for o in update_small(
        conv_w.reshape(DEPTH * 3, SHARD_ROWS), m_conv_w.reshape(DEPTH * 3, SHARD_ROWS),
        v_conv_w.reshape(DEPTH * 3, SHARD_ROWS), g_conv.reshape(DEPTH * 3, SHARD_ROWS), "adamw_conv_w")]

    order = [u_ng, u_in, u_cw, u_qg, u_kg, u_sk, u_co, u_ao, u_gb, u_out]
    return (loss, dh[None], *[u[0] for u in order], *[u[1] for u in order], *[u[2] for u in order], *[u[3] for u in order])
```

```python
import functools

import jax
import jax.numpy as jnp
from jax import lax
from jax.experimental import pallas as pl
from jax.experimental.pallas import tpu as pltpu

F32 = jnp.float32
BF16 = jnp.bfloat16

N_DEV = 8
DEPTH = 4
D = 1024
N_KV = 4
GROUP = 4
HEAD = 64
BLK = 128
KVW = N_KV * HEAD
IN_COLS = 8704
SHARD_COLS = IN_COLS // N_DEV
SHARD_ROWS = D // N_DEV
C_VC, C_BC, C_CC, C_ZC, C_Q, C_K, C_V, C_ZA, C_GA, C_GB = 0, 1024, 2048, 3072, 4096, 5120, 5376, 5632, 6656, 7680
EPS = 1e-6
NEG_INF = -1e30
SCALE = HEAD ** -0.5

ADAM_LR = 0.001
ADAM_B1 = 0.9
ADAM_B2 = 0.999
ADAM_EPS = 1e-08
ADAM_WD = 0.01
ADAM_STEP = 10

VMEM_LIMIT = 60 * 1024 * 1024
SM_ROWS = 16
SM_GATE, SM_CONV, SM_QG, SM_KG, SM_SINK, SM_NORM, SM_LOSS, SM_GATE_B, SM_QG_FOLDED, SM_KG_FOLDED = 0, 1, 4, 5, 6, 7, 8, 9, 10, 11


def _cparams(sem):
    return pltpu.CompilerParams(dimension_semantics=sem, vmem_limit_bytes=VMEM_LIMIT)


def _dot(a, b):
    return jnp.dot(a, b, preferred_element_type=F32)


def _dot_nt(a, b):
    return lax.dot_general(a, b, (((1,), (1,)), ((), ())), preferred_element_type=F32)


def _dot_tn(a, b):
    return lax.dot_general(a, b, (((0,), (0,)), ((), ())), preferred_element_type=F32)


def _dot2(x, sel):
    hi = x.astype(BF16)
    lo = (x - hi.astype(F32)).astype(BF16)
    return _dot(hi, sel) + _dot(lo, sel)


def _sigmoid(z):
    return 0.5 * jnp.tanh(0.5 * z) + 0.5


def _head_mean(t, sel, exp):
    return _dot2(_dot(t.astype(BF16), sel) * (1.0 / HEAD), exp)


def _shift_down(a, k, before):
    r = pltpu.roll(a, k, 0)
    row = lax.broadcasted_iota(jnp.int32, (8, 1), 0)
    head = jnp.where(row < k, pltpu.roll(before, k, 0), r[0:8, :])
    return jnp.concatenate([head, r[8:, :]], axis=0)


def _shift_up(a, k, after):
    n = a.shape[0]
    r = pltpu.roll(a, n - k, 0)
    row = lax.broadcasted_iota(jnp.int32, (8, 1), 0)
    tail = jnp.where(row >= 8 - k, pltpu.roll(after, 8 - k, 0), r[n - 8:, :])
    return jnp.concatenate([r[:n - 8, :], tail], axis=0)


def _bf(ref, c0, width):
    return ref[:, c0:c0 + width].astype(F32)


def _selectors():
    c = jnp.arange(D)
    sel_q = (c[:, None] // HEAD == jnp.arange(128)[None, :]).astype(BF16)
    ck = jnp.arange(KVW)
    sel_k = (ck[:, None] // HEAD == jnp.arange(128)[None, :]).astype(BF16)
    src = jnp.arange(KVW)[:, None]
    dst = jnp.arange(KVW)[None, :]
    rep = jnp.stack([((src // HEAD == h) & (src % HEAD == dst % HEAD)).astype(BF16) for h in range(N_KV)])
    fold = (c[:, None] % HEAD == jnp.arange(128)[None, :]).astype(BF16)
    qq = jnp.arange(GROUP * BLK)[:, None] % BLK
    kk = jnp.arange(2 * BLK)[None, :]
    valid = (kk > qq) & (kk <= qq + BLK)
    bias = jnp.where(valid, 0.0, NEG_INF).astype(F32)
    bias_first = jnp.where(valid & (kk >= BLK), 0.0, NEG_INF).astype(F32)
    return dict(bias=bias, bias_first=bias_first,sel_q=sel_q, exp_q=sel_q.T, sel_k=sel_k, exp_k=sel_k.T, rep=rep, rep_t=jnp.swapaxes(rep, 1, 2), fold=fold)


def inproj_fwd(x, ng, w, name):
    t = x.shape[0]
    tm = min(1024, t)
    cb = 2176
    def body(x_ref, ng_ref, w_ref, u_ref, h_ref, h_scr):
        @pl.when(pl.program_id(1) == 0)
        def _():
            xf = x_ref[...]
            r = lax.rsqrt(jnp.mean(xf * xf, axis=-1, keepdims=True) + EPS)
            hb = (xf * r * ng_ref[...]).astype(BF16)
            h_scr[...] = hb
            h_ref[...] = hb
        u_ref[...] = _dot_nt(h_scr[...], w_ref[...]).astype(BF16)

    return pl.pallas_call(
        body, name=name, grid=(t // tm, IN_COLS // cb),
        in_specs=[pl.BlockSpec((tm, D), lambda i, j: (i, 0)), pl.BlockSpec((1, D), lambda i, j: (0, 0)),
                  pl.BlockSpec((cb, D), lambda i, j: (j, 0))],
        out_specs=[pl.BlockSpec((tm, cb), lambda i, j: (i, j)), pl.BlockSpec((tm, D), lambda i, j: (i, 0))],
        out_shape=[jax.ShapeDtypeStruct((t, IN_COLS), BF16), jax.ShapeDtypeStruct((t, D), BF16)],
        scratch_shapes=[pltpu.VMEM((tm, D), BF16)],
        compiler_params=_cparams(("arbitrary", "arbitrary")),
    )(x, ng, w)


def _conv_fwd(u_ref, uvc_prev, ucc_prev, cw_ref, is_first, tm):
    p = _bf(u_ref, C_CC, D) * _bf(u_ref, C_VC, D)
    pprev = ucc_prev[...].astype(F32) * uvc_prev[...].astype(F32)
    pprev = jnp.where(is_first, 0.0, pprev)
    p1 = _shift_down(p, 1, pprev[8:16, :])
    p2 = _shift_down(p, 2, pprev[8:16, :])
    cw = cw_ref[...]
    conv = cw[0:1, :] * p2 + cw[1:2, :] * p1 + cw[2:3, :] * p
    return p, p1, p2, conv


def _attn_inputs(u_ref, ukv_prev, qg_ref, kg_ref, c, tm):
    q = _bf(u_ref, C_Q, D)
    rq = lax.rsqrt(_head_mean(q * q, c["sel_q"][...], c["exp_q"][...]) + EPS)
    qhat = q * rq
    qn = (qhat * qg_ref[...]).astype(BF16)
    kband = jnp.concatenate([ukv_prev[:, 0:KVW].astype(F32), _bf(u_ref, C_K, KVW)], axis=0)
    rk = lax.rsqrt(_head_mean(kband * kband, c["sel_k"][...], c["exp_k"][...]) + EPS)
    khat = kband * rk
    knb = (khat * kg_ref[...]).astype(BF16)
    vband = jnp.concatenate([ukv_prev[:, KVW:2 * KVW], u_ref[:, C_V:C_V + KVW]], axis=0)
    kt = [_dot(knb, c["rep"][h]).astype(BF16) for h in range(N_KV)]
    vt = [_dot(vband, c["rep"][h]).astype(BF16) for h in range(N_KV)]
    return qhat, rq, qn, khat, rk, kt, vt


def _attn_masks(is_first, c):
    rows = GROUP * BLK
    bias = c["bias"][...]
    bias_first = jnp.where(is_first, c["bias_first"][...], bias)
    lane_grp = lax.broadcasted_iota(jnp.int32, (BLK, KVW), 1) // HEAD
    row_grp = lax.broadcasted_iota(jnp.int32, (rows, 1), 0) // BLK
    return bias, bias_first, lane_grp, row_grp


def _sink_col(sinks_ref, h, row_grp):
    col = jnp.full(row_grp.shape, sinks_ref[0, GROUP * h], F32)
    for gi in range(1, GROUP):
        col = jnp.where(row_grp == gi, sinks_ref[0, GROUP * h + gi], col)
    return col


def _stack_groups(a256, lane_grp):
    zero = jnp.zeros_like(a256)
    return jnp.concatenate([jnp.where(lane_grp == gi, a256, zero) for gi in range(GROUP)], axis=0)


def _unstack_groups(a4, lane_grp):
    out = jnp.where(lane_grp == 0, a4[0:BLK], 0.0)
    for gi in range(1, GROUP):
        out = out + jnp.where(lane_grp == gi, a4[gi * BLK:(gi + 1) * BLK], 0.0)
    return out


def _band_sum(parts):
    pieces = [parts[0][0:BLK, :]]
    for b in range(1, len(parts)):
        pieces.append(parts[b - 1][BLK:, :] + parts[b][0:BLK, :])
    pieces.append(parts[-1][BLK:, :])
    return jnp.concatenate(pieces, axis=0)


def _softmax_block(qs, kt_b, bias, sink):
    s = _dot_nt(qs, kt_b) + bias
    m = jnp.maximum(jnp.max(s, axis=-1, keepdims=True), sink)
    e = jnp.exp(s - m)
    es = jnp.exp(sink - m)
    inv = 1.0 / (jnp.sum(e, axis=-1, keepdims=True) + es)
    return e * inv, es * inv


def _mixer_specs(t, tm, n_tiles, tile_of):
    nb = tm // BLK
    u_spec = pl.BlockSpec((tm, IN_COLS), lambda g: (tile_of(g), 0))
    ukv_prev = pl.BlockSpec((BLK, 2 * KVW), lambda g: (jnp.maximum(tile_of(g) * nb - 1, 0), C_K // (2 * KVW)))
    uvc_prev = pl.BlockSpec((16, D), lambda g: (jnp.maximum(tile_of(g) * (tm // 16) - 1, 0), C_VC // D))
    ucc_prev = pl.BlockSpec((16, D), lambda g: (jnp.maximum(tile_of(g) * (tm // 16) - 1, 0), C_CC // D))
    return u_spec, ukv_prev, uvc_prev, ucc_prev


def _full(shape):
    n = len(shape)
    return pl.BlockSpec(shape, lambda g: (0,) * n)


def mixer_fwd(x, u, cw, qg, kg, sinks, gb, wco, wao, wout, c, name):
    t = x.shape[0]
    tm = min(256, t)
    n_tiles = t // tm
    nb = tm // BLK
    cn = sorted(c)

    def body(x_ref, u_ref, ukv_prev, uvc_prev, ucc_prev, cw_ref, qg_ref, kg_ref, sinks_ref, gb_ref, wco_ref, wao_ref,
             wout_ref, *rest):
        cref = dict(zip(cn, rest[:len(cn)]))
        xo_ref, ya_ref, yb_ref = rest[len(cn):]
        is_first = pl.program_id(0) == 0
        _, _, _, conv = _conv_fwd(u_ref, uvc_prev, ucc_prev, cw_ref, is_first, tm)
        zc = _bf(u_ref, C_ZC, D)
        yc = _bf(u_ref, C_BC, D) * conv * (zc * _sigmoid(zc))
        ya = _dot(yc.astype(BF16), wco_ref[...])

        _, _, qn, _, _, kt, vt = _attn_inputs(u_ref, ukv_prev, qg_ref, kg_ref, cref, tm)
        valid, valid_first, lane_grp, row_grp = _attn_masks(is_first, cref)
        o_cols = []
        for h in range(N_KV):
            sink = _sink_col(sinks_ref, h, row_grp)
            o_rows = []
            for b in range(nb):
                qs = _stack_groups(qn[b * BLK:(b + 1) * BLK, h * KVW:(h + 1) * KVW], lane_grp)
                pn, _ = _softmax_block(qs, kt[h][b * BLK:(b + 2) * BLK], valid_first if b == 0 else valid, sink)
                o4 = _dot(pn.astype(BF16), vt[h][b * BLK:(b + 2) * BLK])
                o_rows.append(_unstack_groups(o4, lane_grp))
            o_cols.append(jnp.concatenate(o_rows, axis=0))
        za = _bf(u_ref, C_ZA, D)
        ob = jnp.concatenate(o_cols, axis=1) * (za * _sigmoid(za))
        yb = _dot(ob.astype(BF16), wao_ref[...])

        g_a = _sigmoid(_bf(u_ref, C_GA, D) + gb_ref[:, 0:D])
        g_b = _sigmoid(_bf(u_ref, C_GB, D) + gb_ref[:, D:2 * D])
        merged = g_a * ya + g_b * yb
        xo_ref[...] = x_ref[...] + _dot(merged.astype(BF16), wout_ref[...])
        ya_ref[...] = ya.astype(BF16)
        yb_ref[...] = yb.astype(BF16)

    u_spec, ukv_prev, uvc_prev, ucc_prev = _mixer_specs(t, tm, n_tiles, lambda g: g)
    tok = pl.BlockSpec((tm, D), lambda g: (g, 0))
    consts = [c[k] for k in cn]
    return pl.pallas_call(
        body, name=name, grid=(n_tiles,),
        in_specs=[tok, u_spec, ukv_prev, uvc_prev, ucc_prev, _full((8, D)), _full((1, D)), _full((1, KVW)),
                  pl.BlockSpec(memory_space=pltpu.SMEM), _full((1, 2 * D)), _full((D, D)), _full((D, D)), _full((D, D))]
                 + [_full(a.shape) for a in consts],
        out_specs=[tok, tok, tok],
        out_shape=[jax.ShapeDtypeStruct((t, D), F32)] + [jax.ShapeDtypeStruct((t, D), BF16)] * 2,
        compiler_params=_cparams(("arbitrary",)),
    )(x, u, u, u, u, cw, qg, kg, sinks, gb, wco, wao, wout, *consts)


def mixer_bwd(dout, u, ya, yb, cw, qg, kg, sinks, gb, wco, wao, wout, c, name):
    t = dout.shape[0]
    tm = min(256, t)
    n_tiles = t // tm
    nb = tm // BLK
    kb = tm + BLK
    cn = sorted(c)

    def body(dout_ref, u_ref, ukv_prev, uvc_prev, ucc_prev, ya_ref, yb_ref, cw_ref, qg_ref, kg_ref, sinks_ref, gb_ref,
             wco_ref, wao_ref, wout_ref, *rest):
        cref = dict(zip(cn, rest[:len(cn)]))
        (du_ref, small_ref, merged_ref, yc_ref, ob_ref, dya_ref, dyb_ref, carry_kv, carry_conv) = rest[len(cn):]
        g = pl.program_id(0)
        is_first = g == n_tiles - 1

        @pl.when(g == 0)
        def _():
            carry_kv[...] = jnp.zeros_like(carry_kv)
            carry_conv[...] = jnp.zeros_like(carry_conv)
            small_ref[...] = jnp.zeros_like(small_ref)

        dout = dout_ref[...]
        dout_b = dout.astype(BF16)
        ya_v = ya_ref[...].astype(F32)
        yb_v = yb_ref[...].astype(F32)
        g_a = _sigmoid(_bf(u_ref, C_GA, D) + gb_ref[:, 0:D])
        g_b = _sigmoid(_bf(u_ref, C_GB, D) + gb_ref[:, D:2 * D])
        merged = g_a * ya_v + g_b * yb_v
        dmerged = _dot_nt(dout_b, wout_ref[...])
        merged_ref[...] = merged.astype(BF16)
        dya = dmerged * g_a
        dyb = dmerged * g_b
        dgl_a = dmerged * ya_v * g_a * (1.0 - g_a)
        dgl_b = dmerged * yb_v * g_b * (1.0 - g_b)
        du_ref[:, C_GA:C_GA + D] = dgl_a.astype(BF16)
        du_ref[:, C_GB:C_GB + D] = dgl_b.astype(BF16)
        small_ref[SM_GATE:SM_GATE + 1, 0:D] += jnp.sum(dgl_a, axis=0, keepdims=True)
        small_ref[SM_GATE_B:SM_GATE_B + 1, 0:D] += jnp.sum(dgl_b, axis=0, keepdims=True)

        p, p1, p2, conv = _conv_fwd(u_ref, uvc_prev, ucc_prev, cw_ref, is_first, tm)
        zc = _bf(u_ref, C_ZC, D)
        bc = _bf(u_ref, C_BC, D)
        sg = _sigmoid(zc)
        sc = zc * sg
        yc = bc * conv * sc
        dya_b = dya.astype(BF16)
        yc_ref[...] = yc.astype(BF16)
        dya_ref[...] = dya_b
        dyc = _dot_nt(dya_b, wco_ref[...])
        du_ref[:, C_BC:C_BC + D] = (dyc * conv * sc).astype(BF16)
        du_ref[:, C_ZC:C_ZC + D] = (dyc * bc * conv * (sg * (1.0 + zc * (1.0 - sg)))).astype(BF16)
        dconv = dyc * bc * sc
        small_ref[SM_CONV + 2:SM_CONV + 3, 0:D] += jnp.sum(dconv * p, axis=0, keepdims=True)
        small_ref[SM_CONV + 1:SM_CONV + 2, 0:D] += jnp.sum(dconv * p1, axis=0, keepdims=True)
        small_ref[SM_CONV:SM_CONV + 1, 0:D] += jnp.sum(dconv * p2, axis=0, keepdims=True)
        nxt = carry_conv[...]
        d1 = _shift_up(dconv, 1, nxt)
        d2 = _shift_up(dconv, 2, nxt)
        carry_conv[...] = dconv[0:8, :]
        cw = cw_ref[...]
        dp = cw[2:3, :] * dconv + cw[1:2, :] * d1 + cw[0:1, :] * d2
        du_ref[:, C_CC:C_CC + D] = (dp * _bf(u_ref, C_VC, D)).astype(BF16)
        du_ref[:, C_VC:C_VC + D] = (dp * _bf(u_ref, C_CC, D)).astype(BF16)

        dyb_b = dyb.astype(BF16)
        dob = _dot_nt(dyb_b, wao_ref[...])
        za = _bf(u_ref, C_ZA, D)
        sga = _sigmoid(za)
        sa = za * sga
        do = dob * sa
        qhat, rq, qn, khat, rk, kt, vt = _attn_inputs(u_ref, ukv_prev, qg_ref, kg_ref, cref, tm)
        valid, valid_first, lane_grp, row_grp = _attn_masks(is_first, cref)
        lane16 = lax.broadcasted_iota(jnp.int32, (1, D), 1)
        dsink_row = jnp.zeros((1, D), F32)
        o_cols, dq_cols, dk4, dv4 = [], [], [], []
        for h in range(N_KV):
            sink = _sink_col(sinks_ref, h, row_grp)
            dsink_col = jnp.zeros((GROUP * BLK, 1), F32)
            o_rows, dq_rows, dk_parts, dv_parts = [], [], [], []
            for b in range(nb):
                rows = slice(b * BLK, (b + 1) * BLK)
                band = slice(b * BLK, (b + 2) * BLK)
                cols = slice(h * KVW, (h + 1) * KVW)
                qs = _stack_groups(qn[rows, cols], lane_grp)
                pn, ps = _softmax_block(qs, kt[h][band], valid_first if b == 0 else valid, sink)
                pn_b = pn.astype(BF16)
                o_rows.append(_unstack_groups(_dot(pn_b, vt[h][band]), lane_grp))
                dos = _stack_groups(do[rows, cols].astype(BF16), lane_grp)
                dpn = _dot_nt(dos, vt[h][band])
                delta = jnp.sum(pn * dpn, axis=-1, keepdims=True)
                ds = (pn * (dpn - delta)).astype(BF16)
                dsink_col = dsink_col - ps * delta
                dq_rows.append(_unstack_groups(_dot(ds, kt[h][band]), lane_grp))
                dk_parts.append(_dot_tn(ds, qs))
                dv_parts.append(_dot_tn(pn_b, dos))
            o_cols.append(jnp.concatenate(o_rows, axis=0))
            dq_cols.append(jnp.concatenate(dq_rows, axis=0))
            dk4.append(_band_sum(dk_parts))
            dv4.append(_band_sum(dv_parts))
            for gi in range(GROUP):
                tot = jnp.sum(dsink_col[gi * BLK:(gi + 1) * BLK, :], axis=0, keepdims=True)
                dsink_row = dsink_row + jnp.where(lane16 == GROUP * h + gi, tot, 0.0)
        small_ref[SM_SINK:SM_SINK + 1, :] += dsink_row

        o = jnp.concatenate(o_cols, axis=1)
        ob_ref[...] = (o * sa).astype(BF16)
        dyb_ref[...] = dyb_b
        du_ref[:, C_ZA:C_ZA + D] = (dob * o * (sga * (1.0 + za * (1.0 - sga)))).astype(BF16)

        dqn = jnp.concatenate(dq_cols, axis=1)
        small_ref[SM_QG:SM_QG + 1, 0:D] += SCALE * jnp.sum(dqn * qhat, axis=0, keepdims=True)
        dqh = dqn * qg_ref[...]
        dq = rq * (dqh - qhat * _head_mean(dqh * qhat, cref["sel_q"][...], cref["exp_q"][...]))
        du_ref[:, C_Q:C_Q + D] = dq.astype(BF16)

        dkn_band = jnp.zeros((kb, KVW), F32)
        dv_band = jnp.zeros((kb, KVW), F32)
        for h in range(N_KV):
            dkn_band = dkn_band + _dot2(dk4[h], cref["rep_t"][h])
            dv_band = dv_band + _dot2(dv4[h], cref["rep_t"][h])
        carried = carry_kv[...]
        pad = jnp.zeros((tm - BLK, KVW), F32)
        if nb > 1:
            dkn = dkn_band[BLK:, :] + jnp.concatenate([pad, carried[:, 0:KVW]], axis=0)
            dv = dv_band[BLK:, :] + jnp.concatenate([pad, carried[:, KVW:2 * KVW]], axis=0)
        else:
            dkn = dkn_band[BLK:, :] + carried[:, 0:KVW]
            dv = dv_band[BLK:, :] + carried[:, KVW:2 * KVW]
        carry_kv[:, 0:KVW] = dkn_band[0:BLK, :]
        carry_kv[:, KVW:2 * KVW] = dv_band[0:BLK, :]
        khat_t = khat[BLK:, :]
        small_ref[SM_KG:SM_KG + 1, 0:KVW] += jnp.sum(dkn * khat_t, axis=0, keepdims=True)
        dkh = dkn * kg_ref[...]
        dk = rk[BLK:, :] * (dkh - khat_t * _head_mean(dkh * khat_t, cref["sel_k"][...], cref["exp_k"][...]))
        du_ref[:, C_K:C_K + KVW] = dk.astype(BF16)
        du_ref[:, C_V:C_V + KVW] = dv.astype(BF16)

    rev = lambda g: n_tiles - 1 - g
    u_spec, ukv_prev, uvc_prev, ucc_prev = _mixer_specs(t, tm, n_tiles, rev)
    tok = pl.BlockSpec((tm, D), lambda g: (rev(g), 0))
    consts = [c[k] for k in cn]
    wspec = _full((D, D))
    return pl.pallas_call(
        body, name=name, grid=(n_tiles,),
        in_specs=[tok, u_spec, ukv_prev, uvc_prev, ucc_prev, tok, tok, _full((8, D)), _full((1, D)), _full((1, KVW)),
                  pl.BlockSpec(memory_space=pltpu.SMEM), _full((1, 2 * D)), wspec, wspec, wspec]
                 + [_full(a.shape) for a in consts],
        out_specs=[pl.BlockSpec((tm, IN_COLS), lambda g: (rev(g), 0)), _full((SM_ROWS, D))] + [tok] * 5,
        out_shape=[jax.ShapeDtypeStruct((t, IN_COLS), BF16), jax.ShapeDtypeStruct((SM_ROWS, D), F32)]
                  + [jax.ShapeDtypeStruct((t, D), BF16)] * 5,
        scratch_shapes=[pltpu.VMEM((BLK, 2 * KVW), F32), pltpu.VMEM((8, D), F32)],
        compiler_params=_cparams(("arbitrary",)),
    )(dout, u, u, u, u, ya, yb, cw, qg, kg, sinks, gb, wco, wao, wout, *consts)


def matmul_tn(a, b, name, after=None):
    t, m = a.shape
    tk = min(1024, t)
    mb = 2176 if m == IN_COLS else m
    nk = t // tk

    def body(a_ref, b_ref, *rest):
        o_ref, acc = rest[-2:]
        k = pl.program_id(1)
        prod = _dot_tn(a_ref[...].astype(BF16), b_ref[...].astype(BF16))

        @pl.when(k == 0)
        def _():
            acc[...] = prod

        @pl.when(k > 0)
        def _():
            acc[...] += prod

        @pl.when(k == nk - 1)
        def _():
            o_ref[...] = acc[...].astype(BF16)

    return pl.pallas_call(
        body, name=name, grid=(m // mb, nk),
        in_specs=[pl.BlockSpec((tk, mb), lambda j, k: (k, j)), pl.BlockSpec((tk, D), lambda j, k: (k, 0))]
                 + ([] if after is None else [ANY]),
        out_specs=pl.BlockSpec((mb, D), lambda j, k: (j, 0)),
        out_shape=jax.ShapeDtypeStruct((m, D), BF16),
        scratch_shapes=[pltpu.VMEM((mb, D), F32)],
        compiler_params=_cparams(("arbitrary", "arbitrary")),
    )(a, b, *([] if after is None else [after]))


def inproj_bwd_x(du, w, x, ng, dout, name):
    t = x.shape[0]
    tm = min(1024, t)
    kc = 2176
    nk = IN_COLS // kc

    def body(du_ref, w_ref, x_ref, ng_ref, dout_ref, dx_ref, dng_ref, acc):
        i = pl.program_id(0)
        k = pl.program_id(1)
        prod = _dot(du_ref[...], w_ref[...])

        @pl.when(k == 0)
        def _():
            acc[...] = prod

        @pl.when(k > 0)
        def _():
            acc[...] += prod

        @pl.when((i == 0) & (k == 0))
        def _():
            dng_ref[...] = jnp.zeros_like(dng_ref)

        @pl.when(k == nk - 1)
        def _():
            dh = acc[...]
            xf = x_ref[...]
            r = lax.rsqrt(jnp.mean(xf * xf, axis=-1, keepdims=True) + EPS)
            xhat = xf * r
            dng_ref[0:1, :] += jnp.sum(dh * xhat, axis=0, keepdims=True)
            dxh = dh * ng_ref[...]
            dx_ref[...] = dout_ref[...] + r * (dxh - xhat * jnp.mean(dxh * xhat, axis=-1, keepdims=True))

    tok = pl.BlockSpec((tm, D), lambda i, k: (i, 0))
    return pl.pallas_call(
        body, name=name, grid=(t // tm, nk),
        in_specs=[pl.BlockSpec((tm, kc), lambda i, k: (i, k)), pl.BlockSpec((kc, D), lambda i, k: (k, 0)), tok,
                  pl.BlockSpec((1, D), lambda i, k: (0, 0)), tok],
        out_specs=[tok, pl.BlockSpec((8, D), lambda i, k: (0, 0))],
        out_shape=[jax.ShapeDtypeStruct((t, D), F32), jax.ShapeDtypeStruct((8, D), F32)],
        scratch_shapes=[pltpu.VMEM((tm, D), F32)],
        compiler_params=_cparams(("arbitrary", "arbitrary")),
    )(du, w, x, ng, dout)


def loss_head(y, target, name):
    t = y.shape[0]
    tm = min(1024, t)

    def body(y_ref, t_ref, dy_ref, loss_ref):
        @pl.when(pl.program_id(0) == 0)
        def _():
            loss_ref[...] = jnp.zeros_like(loss_ref)
        err = y_ref[...] - t_ref[...]
        dy_ref[...] = err * (1.0 / D)
        part = jnp.sum(jnp.sum(err * err, axis=-1, keepdims=True) * (1.0 / D), axis=0, keepdims=True)
        loss_ref[...] += 0.5 * part

    tok = pl.BlockSpec((tm, D), lambda i: (i, 0))
    return pl.pallas_call(
        body, name=name, grid=(t // tm,), in_specs=[tok, tok],
        out_specs=[tok, pl.BlockSpec((8, 128), lambda i: (0, 0))],
        out_shape=[jax.ShapeDtypeStruct((t, D), F32), jax.ShapeDtypeStruct((8, 128), F32)],
        compiler_params=_cparams(("arbitrary",)),
    )(y, target)


def layer_operands(l, norm_g, conv_w_full, q_norm_g, k_norm_g, sinks, gate_b, w_in_b, wco_b, wao_b, wout_b):
    return dict(
        ng=norm_g[l][None, :], cw=jnp.pad(conv_w_full[l], ((0, 5), (0, 0))),
        qg=jnp.tile(q_norm_g[l] * SCALE, D // HEAD)[None, :], kg=jnp.tile(k_norm_g[l], N_KV)[None, :],
        sinks=sinks[l][None, :], gb=gate_b[l][None, :],
        w_in=w_in_b[l], wco=wco_b[l], wao=wao_b[l], wout=wout_b[l])


def layer_bwd(dout, saved, lw, c, l, send_off):
    x, u, h, ya, yb = saved
    du, small, merged, yc, ob, dya, dyb = mixer_bwd(dout, u, ya, yb, lw["cw"], lw["qg"], lw["kg"], lw["sinks"], lw["gb"],
                                                    lw["wco"], lw["wao"], lw["wout"], c, f"mixer_bwd_{l}")
    grads = dict(w_in=matmul_tn(du, h, f"dw_in_{l}"), small=small)
    token = send_off(grads, False)
    grads["wout"] = matmul_tn(merged, dout, f"dw_out_{l}", after=token)
    grads["wco"] = matmul_tn(yc, dya, f"dw_conv_out_{l}")
    grads["wao"] = matmul_tn(ob, dyb, f"dw_attn_out_{l}")
    token = send_off(grads, True)
    dx, grads["dng"] = inproj_bwd_x(du, lw["w_in"], x, lw["ng"] + token[0:1, 0:1], dout, f"inproj_bwd_{l}")
    return dx, grads


MESH = pl.DeviceIdType.MESH
ANY = pl.BlockSpec(memory_space=pl.ANY)


def _place():
    return lax.axis_index("x"), lax.axis_index("y"), lax.axis_index("c")


def all_gather(arrs, after, name):
    n = len(arrs)

    def body(*refs):
        ins, outs = refs[:n], refs[n + 1:2 * n + 1]
        send_sems, recv_sems, local_sems = refs[2 * n + 1:]
        x, y, c = _place()
        me, sibling = (x, y, c), (x, y, 1 - c)
        chips = [(1 - x, y), (x, 1 - y), (1 - x, 1 - y)]

        def slot(a, block):
            px, py, pc = block
            return outs[a].at[4 * px + 2 * py + pc]

        def copy(a, k, block, to, src=None):
            return pltpu.make_async_remote_copy(
                src_ref=slot(a, block) if src is None else src, dst_ref=slot(a, block),
                send_sem=send_sems.at[a, k], recv_sem=recv_sems.at[a, k], device_id=to, device_id_type=MESH)

        mine = [pltpu.make_async_copy(ins[a], slot(a, me), local_sems.at[a]) for a in range(n)]
        for cp in mine:
            cp.start()
        first = []
        for a in range(n):
            first.append(copy(a, 0, me, sibling, src=ins[a]))
            first += [copy(a, 1 + j, me, (*chip, c), src=ins[a]) for j, chip in enumerate(chips)]
        for cp in first:
            cp.start()
        passed = []
        for j, chip in enumerate(chips):
            for a in range(n):
                copy(a, 1 + j, (*chip, c), me).wait_recv()
                passed.append(copy(a, 4 + j, (*chip, c), sibling))
                passed[-1].start()
        for a in range(n):
            copy(a, 0, sibling, me).wait_recv()
            for j, chip in enumerate(chips):
                copy(a, 4 + j, (*chip, 1 - c), me).wait_recv()
        for cp in first + passed:
            cp.wait_send()
        for cp in mine:
            cp.wait()

    return pl.pallas_call(
        body, name=name, in_specs=[ANY] * (n + 1), out_specs=[ANY] * n,
        out_shape=[jax.ShapeDtypeStruct((N_DEV,) + a.shape, a.dtype) for a in arrs],
        scratch_shapes=[pltpu.SemaphoreType.DMA((n, 7)), pltpu.SemaphoreType.DMA((n, 7)), pltpu.SemaphoreType.DMA((n,))],
    )(*arrs, after)


HBM_SPEC = pl.BlockSpec(memory_space=pltpu.HBM)
SEM_SPEC = pl.BlockSpec(memory_space=pltpu.SEMAPHORE)
EFFECT = pltpu.SideEffectType.DATAFLOW_SIDE_EFFECTING


ALL_PEERS = (1, 2, 3, 4, 5, 6, 7)


def _flip(place, k):
    x, y, c = place
    return (1 - x if k & 4 else x, 1 - y if k & 2 else y, 1 - c if k & 1 else c)


def _slot(place):
    return 4 * place[0] + 2 * place[1] + place[2]


def _exchange_copies(ins, lands, send_sems, recv_sems, scatter, flips, arriving):
    me = _place()
    out = []
    for a in range(len(ins)):
        for i, k in enumerate(flips):
            peer = _flip(me, k)
            out.append(pltpu.make_async_remote_copy(
                src_ref=ins[a].at[_slot(peer)] if scatter else ins[a],
                dst_ref=lands[a].at[_slot(peer) if arriving else _slot(me)],
                send_sem=send_sems.at[a * len(flips) + i], recv_sem=recv_sems.at[a * len(flips) + i],
                device_id=peer, device_id_type=MESH))
    return out


def exchange_start(arrs, scatter, flips, after, name):
    n = len(arrs)
    lands = [lax.empty(a.shape if scatter else (N_DEV,) + a.shape, a.dtype) for a in arrs]

    def body(*refs):
        ins, lz = refs[:n], refs[n:2 * n]
        send_sems, recv_sems = refs[2 * n + 1], refs[2 * n + 2]
        token = refs[-1]
        for cp in _exchange_copies(ins, lz, send_sems, recv_sems, scatter, flips, False):
            cp.start()
        token[...] = jnp.zeros_like(token)

    sems = pltpu.SemaphoreType.DMA((n * len(flips),))
    res = pl.pallas_call(
        body, name=name,
        out_shape=(sems, sems, *[pltpu.HBM(a.shape, a.dtype) for a in arrs], *[pltpu.HBM(a.shape, a.dtype) for a in lands],
                   jax.ShapeDtypeStruct((8, 128), F32)),
        in_specs=[HBM_SPEC] * (2 * n) + [ANY],
        out_specs=(SEM_SPEC, SEM_SPEC, *[HBM_SPEC] * (2 * n), pl.BlockSpec(memory_space=pltpu.VMEM)),
        input_output_aliases={i: 2 + i for i in range(2 * n)},
        compiler_params=pltpu.CompilerParams(has_side_effects=EFFECT),
    )(*[pltpu.with_memory_space_constraint(a, pltpu.HBM) for a in arrs],
      *[pltpu.with_memory_space_constraint(a, pltpu.HBM) for a in lands], after)
    return dict(send=res[0], recv=res[1], srcs=res[2:2 + n], lands=res[2 + n:2 + 2 * n], token=res[-1], scatter=scatter,
                flips=flips)


def exchange_wait(state, after, name):
    n = len(state["srcs"])

    def body(*refs):
        ins, lz = refs[:n], refs[n:2 * n]
        send_sems, recv_sems = refs[2 * n], refs[2 * n + 1]
        for cp in _exchange_copies(ins, lz, send_sems, recv_sems, state["scatter"], state["flips"], True):
            cp.wait_send()
            cp.wait_recv()

    both = list(state["srcs"]) + list(state["lands"])
    res = pl.pallas_call(
        body, name=name, out_shape=tuple(pltpu.HBM(a.shape, a.dtype) for a in both),
        in_specs=[HBM_SPEC] * (2 * n) + [SEM_SPEC, SEM_SPEC, ANY], out_specs=tuple([HBM_SPEC] * (2 * n)),
        input_output_aliases={i: i for i in range(2 * n)},
        compiler_params=pltpu.CompilerParams(has_side_effects=EFFECT),
    )(*both, state["send"], state["recv"], after)
    return res[:n], res[n:]


OTHER_CHIPS = (2, 4, 6)
SAME_CORE = (1, 2, 4, 6)


def _forward_copies(lands, send_sems, recv_sems, arriving):
    me = _place()
    sibling = _flip(me, 1)
    origin = sibling if arriving else me
    return [pltpu.make_async_remote_copy(
        src_ref=lands[a].at[_slot(_flip(origin, k))], dst_ref=lands[a].at[_slot(_flip(origin, k))],
        send_sem=send_sems.at[a * len(OTHER_CHIPS) + i], recv_sem=recv_sems.at[a * len(OTHER_CHIPS) + i],
        device_id=sibling, device_id_type=MESH) for a in range(len(lands)) for i, k in enumerate(OTHER_CHIPS)]


def forward_start(lands, after, name):
    n = len(lands)

    def body(*refs):
        lz = refs[:n]
        send_sems, recv_sems = refs[n + 1], refs[n + 2]
        token = refs[-1]
        for cp in _forward_copies(lz, send_sems, recv_sems, False):
            cp.start()
        token[...] = jnp.zeros_like(token)

    sems = pltpu.SemaphoreType.DMA((n * len(OTHER_CHIPS),))
    res = pl.pallas_call(
        body, name=name,
        out_shape=(sems, sems, *[pltpu.HBM(a.shape, a.dtype) for a in lands], jax.ShapeDtypeStruct((8, 128), F32)),
        in_specs=[HBM_SPEC] * n + [ANY],
        out_specs=(SEM_SPEC, SEM_SPEC, *[HBM_SPEC] * n, pl.BlockSpec(memory_space=pltpu.VMEM)),
        input_output_aliases={i: 2 + i for i in range(n)},
        compiler_params=pltpu.CompilerParams(has_side_effects=EFFECT),
    )(*[pltpu.with_memory_space_constraint(a, pltpu.HBM) for a in lands], after)
    return dict(send=res[0], recv=res[1], lands=res[2:2 + n], token=res[-1])


def forward_wait(state, after, name):
    n = len(state["lands"])

    def body(*refs):
        lz = refs[:n]
        send_sems, recv_sems = refs[n], refs[n + 1]
        for cp in _forward_copies(lz, send_sems, recv_sems, False):
            cp.wait_send()
        for cp in _forward_copies(lz, send_sems, recv_sems, True):
            cp.wait_recv()

    return pl.pallas_call(
        body, name=name, out_shape=tuple(pltpu.HBM(a.shape, a.dtype) for a in state["lands"]),
        in_specs=[HBM_SPEC] * n + [SEM_SPEC, SEM_SPEC, ANY], out_specs=tuple([HBM_SPEC] * n),
        input_output_aliases={i: i for i in range(n)},
        compiler_params=pltpu.CompilerParams(has_side_effects=EFFECT),
    )(*state["lands"], state["send"], state["recv"], after)


def adamw(w, m, v, parts, name):
    r, cdim = w.shape
    n_parts = parts.shape[0]
    rb = 256 if r % 256 == 0 else (SHARD_COLS // 4 if r == SHARD_COLS else r)

    def body(w_ref, m_ref, v_ref, p_ref, g_ref, d_ref, mo_ref, vo_ref):
        g = p_ref[0].astype(F32)
        for i in range(1, n_parts):
            g = g + p_ref[i].astype(F32)
        m_new = ADAM_B1 * m_ref[...] + (1.0 - ADAM_B1) * g
        v_new = ADAM_B2 * v_ref[...] + (1.0 - ADAM_B2) * (g * g)
        m_hat = m_new / (1.0 - ADAM_B1 ** ADAM_STEP)
        v_hat = v_new / (1.0 - ADAM_B2 ** ADAM_STEP)
        g_ref[...] = g
        d_ref[...] = -ADAM_LR * (m_hat / (jnp.sqrt(v_hat) + ADAM_EPS) + ADAM_WD * w_ref[...])
        mo_ref[...] = m_new
        vo_ref[...] = v_new

    blk = pl.BlockSpec((rb, cdim), lambda i: (i, 0))
    return pl.pallas_call(
        body, name=name, grid=(r // rb,),
        in_specs=[blk, blk, blk, pl.BlockSpec((n_parts, rb, cdim), lambda i: (0, i, 0))],
        out_specs=[blk] * 4, out_shape=[jax.ShapeDtypeStruct((r, cdim), F32)] * 4,
        compiler_params=_cparams(("arbitrary",)),
    )(w, m, v, parts)


def adamw_layers(w, m, v, lands, srcs, me, lo, prev, name):
    nl_all, r, cdim = w.shape
    nl = len(lands)
    rb = 256 if r % 256 == 0 else (SHARD_COLS // 4 if r == SHARD_COLS else r)
    nblk = r // rb
    n_prev = 0 if prev is None else 4

    def body(me_ref, w_ref, m_ref, v_ref, *rest):
        land_refs, src_refs = rest[:nl], rest[nl:2 * nl]
        g_ref, d_ref, mo_ref, vo_ref = rest[2 * nl + n_prev:]
        for k in range(nl):
            @pl.when(pl.program_id(0) == k)
            def _(k=k):
                own = src_refs[k][...].astype(F32)
                g = jnp.where(me_ref[0] == 0, own, land_refs[k][0].astype(F32))
                for i in range(1, N_DEV):
                    g = g + jnp.where(me_ref[0] == i, own, land_refs[k][i].astype(F32))
                m_new = ADAM_B1 * m_ref[...] + (1.0 - ADAM_B1) * g
                v_new = ADAM_B2 * v_ref[...] + (1.0 - ADAM_B2) * (g * g)
                m_hat = m_new / (1.0 - ADAM_B1 ** ADAM_STEP)
                v_hat = v_new / (1.0 - ADAM_B2 ** ADAM_STEP)
                g_ref[...] = g
                d_ref[...] = -ADAM_LR * (m_hat / (jnp.sqrt(v_hat) + ADAM_EPS) + ADAM_WD * w_ref[...])
                mo_ref[...] = m_new
                vo_ref[...] = v_new

    blk = pl.BlockSpec((None, rb, cdim), lambda l, i, me_ref: (lo + l, i, 0))

    def rows(l, i, k):
        return jnp.where(l < k, 0, jnp.where(l == k, i, nblk - 1))

    land_specs = [pl.BlockSpec((N_DEV, rb, cdim), lambda l, i, me_ref, k=k: (0, rows(l, i, k), 0)) for k in range(nl)]
    src_specs = [pl.BlockSpec((None, rb, cdim), lambda l, i, me_ref, k=k: (me_ref[0], rows(l, i, k), 0)) for k in range(nl)]
    return pl.pallas_call(
        body, name=name,
        grid_spec=pltpu.PrefetchScalarGridSpec(
            num_scalar_prefetch=1, grid=(nl, nblk),
            in_specs=[blk, blk, blk] + land_specs + src_specs + [ANY] * n_prev, out_specs=[blk] * 4),
        out_shape=[jax.ShapeDtypeStruct((nl_all, r, cdim), F32)] * 4,
        input_output_aliases={4 + 2 * nl + j: j for j in range(n_prev)},
        compiler_params=_cparams(("arbitrary", "arbitrary")),
    )(me, w, m, v, *lands, *srcs, *([] if prev is None else prev))


def small_sum(parts, fold, name):
    rows = parts.shape[1]

    def dot3(xv, sel):
        out = jnp.zeros((xv.shape[0], sel.shape[1]), F32)
        for _ in range(3):
            hi = xv.astype(BF16)
            out = out + _dot(hi, sel)
            xv = xv - hi.astype(F32)
        return out

    def body(p_ref, fold_ref, o_ref):
        tot = p_ref[0]
        for i in range(1, N_DEV):
            tot = tot + p_ref[i]
        o_ref[...] = tot
        for l in range(rows // SM_ROWS):
            blk = tot[l * SM_ROWS:l * SM_ROWS + 8, 0:D]
            folded = dot3(blk, fold_ref[...])
            o_ref[l * SM_ROWS + SM_QG_FOLDED:l * SM_ROWS + SM_QG_FOLDED + 1, 0:128] = folded[SM_QG:SM_QG + 1, :]
            o_ref[l * SM_ROWS + SM_KG_FOLDED:l * SM_ROWS + SM_KG_FOLDED + 1, 0:128] = folded[SM_KG:SM_KG + 1, :]

    return pl.pallas_call(
        body, name=name, out_shape=jax.ShapeDtypeStruct((rows, D), F32),
        compiler_params=_cparams(None),
    )(parts, fold)


def kernel(x, norm_g, w_in, conv_w, q_norm_g, k_norm_g, sinks, w_conv_out, w_attn_out, gate_b, w_out, loss_target, m_norm_g, m_w_in, m_conv_w, m_q_norm_g, m_k_norm_g, m_sinks, m_w_conv_out, m_w_attn_out, m_gate_b, m_w_out, v_norm_g, v_w_in, v_conv_w, v_q_norm_g, v_k_norm_g, v_sinks, v_w_conv_out, v_w_attn_out, v_gate_b, v_w_out):
    c = _selectors()
    me = 4 * lax.axis_index("x") + 2 * lax.axis_index("y") + lax.axis_index("c")

    w_in_t, m_w_in_t, v_w_in_t = (jnp.swapaxes(a, 1, 2) for a in (w_in, m_w_in, v_w_in))

    def shards(l):
        return [w_in_t[l].astype(BF16), w_conv_out[l].astype(BF16), w_attn_out[l].astype(BF16), w_out[l].astype(BF16)]

    def ici_start(l, after):
        return exchange_start(shards(l), False, SAME_CORE, after, f"gather_start_{l}")

    h = x[0]
    saved, lws = [], []
    lands = all_gather(shards(0) + [conv_w], h, "gather_0")
    conv_full = jnp.transpose(lands[4], (1, 2, 0, 3)).reshape(DEPTH, 3, D)
    ici = ici_start(1, lands[1])
    for l in range(DEPTH):
        lws.append(layer_operands(l, norm_g, conv_full, q_norm_g, k_norm_g, sinks, gate_b,
                                  {l: lands[0].reshape(IN_COLS, D)}, {l: lands[1].reshape(D, D)},
                                  {l: lands[2].reshape(D, D)}, {l: lands[3].reshape(D, D)}))
        ng_l = lws[l]["ng"] + ici["token"][0:1, 0:1] if l == 0 else lws[l]["ng"]
        u, hb = inproj_fwd(h, ng_l, lws[l]["w_in"], f"inproj_fwd_{l}")
        gb_l = lws[l]["gb"]
        if l + 1 < DEPTH:
            mine, arrived = exchange_wait(ici, u, f"gather_wait_{l + 1}")
            chip = forward_start(arrived, mine[0], f"gather_forward_start_{l + 1}")
            started = chip["token"]
            if l + 2 < DEPTH:
                ici = ici_start(l + 2, started)
                started = ici["token"]
            gb_l = gb_l + started[0:1, 0:1]
        x_in = h
        h, ya, yb = mixer_fwd(x_in, u, lws[l]["cw"], lws[l]["qg"], lws[l]["kg"], lws[l]["sinks"], gb_l,
                              lws[l]["wco"], lws[l]["wao"], lws[l]["wout"], c, f"mixer_fwd_{l}")
        saved.append((x_in, u, hb, ya, yb))
        if l + 1 < DEPTH:
            lands = forward_wait(chip, h, f"gather_forward_wait_{l + 1}")
            lands = [lax.dynamic_update_index_in_dim(land, src, me, 0) for land, src in zip(lands, mine)]
    dh, loss_part = loss_head(h, loss_target[0], "loss_head")

    grads, scatters = [None] * DEPTH, [[] for _ in range(DEPTH)]
    for l in reversed(range(DEPTH)):
        def send_off(g, done, l=l):
            rest = [g[k].reshape(N_DEV, SHARD_ROWS, D) for k in ("wco", "wao", "wout")] if done else []
            first = [g["w_in"].reshape(N_DEV, SHARD_COLS, D)] if done == (l > 0) else []
            if not first + rest:
                return None
            tag = f"{l}" if l > 0 else ("0_rest" if done else "0_in")
            scatters[l].append(exchange_start(first + rest, True, ALL_PEERS, g["small"], f"scatter_start_{tag}"))
            return scatters[l][-1]["token"]
        dh, grads[l] = layer_bwd(dh, saved[l], lws[l], c, l, send_off)

    me1 = me.astype(jnp.int32).reshape(1)
    mine, lands = {}, {}
    for l in (3, 2, 1):
        mine[l], lands[l] = exchange_wait(scatters[l][0], dh, f"scatter_wait_{l}")
    weights = [(w_in_t, m_w_in_t, v_w_in_t, "w_in"), (w_conv_out, m_w_conv_out, v_w_conv_out, "w_conv_out"),
               (w_attn_out, m_w_attn_out, v_w_attn_out, "w_attn_out"), (w_out, m_w_out, v_w_out, "w_out")]
    upd = [adamw_layers(w, m, v, [lands[l][i] for l in (1, 2, 3)], [mine[l][i] for l in (1, 2, 3)], me1, 1, None,
                        f"adamw_{n}_upper") for i, (w, m, v, n) in enumerate(weights)]
    blocks = []
    for l in range(DEPTH):
        blk = grads[l]["small"]
        blk = blk.at[SM_NORM, 0:D].set(grads[l]["dng"][0])
        if l == 0:
            blk = blk.at[SM_LOSS, 0:128].set(loss_part[0])
        blocks.append(blk)
    small_x = exchange_start([jnp.concatenate(blocks, axis=0)], False, ALL_PEERS, grads[0]["dng"], "gather_small_start")

    m_in, l_in = exchange_wait(scatters[0][0], upd[0][0], "scatter_wait_0_in")
    upd[0] = adamw_layers(*weights[0][:3], [l_in[0]], [m_in[0]], me1, 0, upd[0], "adamw_w_in_0")
    m_rest, l_rest = exchange_wait(scatters[0][1], upd[0][0], "scatter_wait_0_rest")
    for i in (1, 2, 3):
        upd[i] = adamw_layers(*weights[i][:3], [l_rest[i - 1]], [m_rest[i - 1]], me1, 0, upd[i], f"adamw_{weights[i][3]}_0")
    u_in = [jnp.swapaxes(o, 1, 2) for o in upd[0]]
    u_co, u_ao, u_out = upd[1], upd[2], upd[3]

    mine_s, lands_s = exchange_wait(small_x, u_out[0], "gather_small_wait")
    gathered = lax.dynamic_update_index_in_dim(lands_s[0], mine_s[0], me, 0)
    tot = small_sum(gathered, c["fold"], "small_sum")
    tot = tot.reshape(DEPTH, SM_ROWS, D)
    loss = tot[0, SM_LOSS, 0]

    def update_small(w, m, v, g, name):
        return adamw(w, m, v, g[None], name)

    u_ng = update_small(norm_g, m_norm_g, v_norm_g, tot[:, SM_NORM, 0:D], "adamw_norm_g")
    u_qg = update_small(q_norm_g, m_q_norm_g, v_q_norm_g, tot[:, SM_QG_FOLDED, 0:HEAD], "adamw_q_norm_g")
    u_kg = update_small(k_norm_g, m_k_norm_g, v_k_norm_g, tot[:, SM_KG_FOLDED, 0:HEAD], "adamw_k_norm_g")
    u_sk = update_small(sinks, m_sinks, v_sinks, tot[:, SM_SINK, 0:16], "adamw_sinks")
    g_gate = jnp.concatenate([tot[:, SM_GATE, :], tot[:, SM_GATE_B, :]], axis=1)
    u_gb = update_small(gate_b, m_gate_b, v_gate_b, g_gate, "adamw_gate_b")
    g_conv = lax.dynamic_slice_in_dim(tot[:, SM_CONV:SM_CONV + 3, 0:D], me * SHARD_ROWS, SHARD_ROWS, axis=2)
    u_cw = [o.reshape(DEPTH, 3, SHARD_ROWS) for o in update_small(
        conv_w.reshape(DEPTH * 3, SHARD_ROWS), m_conv_w.reshape(DEPTH * 3, SHARD_ROWS),
        v_conv_w.reshape(DEPTH * 3, SHARD_ROWS), g_conv.reshape(DEPTH * 3, SHARD_ROWS), "adamw_conv_w")]

    order = [u_ng, u_in, u_cw, u_qg, u_kg, u_sk, u_co, u_ao, u_gb, u_out]
    return (loss, dh[None], *[u[0] for u in order], *[u[1] for u in order], *[u[2] for u in order], *[u[3] for u in order])
```

```python
import functools

import jax
import jax.numpy as jnp
from jax import lax
from jax.experimental import pallas as pl
from jax.experimental.pallas import tpu as pltpu

F32 = jnp.float32
BF16 = jnp.bfloat16

N_DEV = 8
DEPTH = 4
D = 1024
N_KV = 4
GROUP = 4
HEAD = 64
BLK = 128
KVW = N_KV * HEAD
IN_COLS = 8704
SHARD_COLS = IN_COLS // N_DEV
SHARD_ROWS = D // N_DEV
C_VC, C_BC, C_CC, C_ZC, C_Q, C_K, C_V, C_ZA, C_GA, C_GB = 0, 1024, 2048, 3072, 4096, 5120, 5376, 5632, 6656, 7680
EPS = 1e-6
NEG_INF = -1e30
SCALE = HEAD ** -0.5

ADAM_LR = 0.001
ADAM_B1 = 0.9
ADAM_B2 = 0.999
ADAM_EPS = 1e-08
ADAM_WD = 0.01
ADAM_STEP = 10

VMEM_LIMIT = 60 * 1024 * 1024
SM_ROWS = 16
SM_GATE, SM_CONV, SM_QG, SM_KG, SM_SINK, SM_NORM, SM_LOSS, SM_GATE_B, SM_QG_FOLDED, SM_KG_FOLDED = 0, 1, 4, 5, 6, 7, 8, 9, 10, 11


def _cparams(sem):
    return pltpu.CompilerParams(dimension_semantics=sem, vmem_limit_bytes=VMEM_LIMIT)


def _dot(a, b):
    return jnp.dot(a, b, preferred_element_type=F32)


def _dot_nt(a, b):
    return lax.dot_general(a, b, (((1,), (1,)), ((), ())), preferred_element_type=F32)


def _dot_tn(a, b):
    return lax.dot_general(a, b, (((0,), (0,)), ((), ())), preferred_element_type=F32)


def _dot2(x, sel):
    hi = x.astype(BF16)
    lo = (x - hi.astype(F32)).astype(BF16)
    return _dot(hi, sel) + _dot(lo, sel)


def _sigmoid(z):
    return 0.5 * jnp.tanh(0.5 * z) + 0.5


def _head_mean(t, sel, exp):
    return _dot2(_dot(t.astype(BF16), sel) * (1.0 / HEAD), exp)


def _shift_down(a, k, before):
    r = pltpu.roll(a, k, 0)
    row = lax.broadcasted_iota(jnp.int32, (8, 1), 0)
    head = jnp.where(row < k, pltpu.roll(before, k, 0), r[0:8, :])
    return jnp.concatenate([head, r[8:, :]], axis=0)


def _shift_up(a, k, after):
    n = a.shape[0]
    r = pltpu.roll(a, n - k, 0)
    row = lax.broadcasted_iota(jnp.int32, (8, 1), 0)
    tail = jnp.where(row >= 8 - k, pltpu.roll(after, 8 - k, 0), r[n - 8:, :])
    return jnp.concatenate([r[:n - 8, :], tail], axis=0)


def _bf(ref, c0, width):
    return ref[:, c0:c0 + width].astype(F32)


def _selectors():
    c = jnp.arange(D)
    sel_q = (c[:, None] // HEAD == jnp.arange(128)[None, :]).astype(BF16)
    ck = jnp.arange(KVW)
    sel_k = (ck[:, None] // HEAD == jnp.arange(128)[None, :]).astype(BF16)
    src = jnp.arange(KVW)[:, None]
    dst = jnp.arange(KVW)[None, :]
    rep = jnp.stack([((src // HEAD == h) & (src % HEAD == dst % HEAD)).astype(BF16) for h in range(N_KV)])
    fold = (c[:, None] % HEAD == jnp.arange(128)[None, :]).astype(BF16)
    qq = jnp.arange(GROUP * BLK)[:, None] % BLK
    kk = jnp.arange(2 * BLK)[None, :]
    valid = (kk > qq) & (kk <= qq + BLK)
    bias = jnp.where(valid, 0.0, NEG_INF).astype(F32).T
    bias_first = jnp.where(valid & (kk >= BLK), 0.0, NEG_INF).astype(F32).T
    return dict(bias=bias, bias_first=bias_first,sel_q=sel_q, exp_q=sel_q.T, sel_k=sel_k, exp_k=sel_k.T, rep=rep, rep_t=jnp.swapaxes(rep, 1, 2), fold=fold)


def inproj_fwd(x, ng, w, name):
    t = x.shape[0]
    tm = min(1024, t)
    cb = 2176
    def body(x_ref, ng_ref, w_ref, u_ref, h_ref, h_scr):
        @pl.when(pl.program_id(1) == 0)
        def _():
            xf = x_ref[...]
            r = lax.rsqrt(jnp.mean(xf * xf, axis=-1, keepdims=True) + EPS)
            hb = (xf * r * ng_ref[...]).astype(BF16)
            h_scr[...] = hb
            h_ref[...] = hb
        u_ref[...] = _dot_nt(h_scr[...], w_ref[...]).astype(BF16)

    return pl.pallas_call(
        body, name=name, grid=(t // tm, IN_COLS // cb),
        in_specs=[pl.BlockSpec((tm, D), lambda i, j: (i, 0)), pl.BlockSpec((1, D), lambda i, j: (0, 0)),
                  pl.BlockSpec((cb, D), lambda i, j: (j, 0))],
        out_specs=[pl.BlockSpec((tm, cb), lambda i, j: (i, j)), pl.BlockSpec((tm, D), lambda i, j: (i, 0))],
        out_shape=[jax.ShapeDtypeStruct((t, IN_COLS), BF16), jax.ShapeDtypeStruct((t, D), BF16)],
        scratch_shapes=[pltpu.VMEM((tm, D), BF16)],
        compiler_params=_cparams(("arbitrary", "arbitrary")),
    )(x, ng, w)


def _conv_fwd(u_ref, uvc_prev, ucc_prev, cw_ref, is_first, tm):
    p = _bf(u_ref, C_CC, D) * _bf(u_ref, C_VC, D)
    pprev = ucc_prev[...].astype(F32) * uvc_prev[...].astype(F32)
    pprev = jnp.where(is_first, 0.0, pprev)
    p1 = _shift_down(p, 1, pprev[8:16, :])
    p2 = _shift_down(p, 2, pprev[8:16, :])
    cw = cw_ref[...]
    conv = cw[0:1, :] * p2 + cw[1:2, :] * p1 + cw[2:3, :] * p
    return p, p1, p2, conv


def _attn_inputs(u_ref, ukv_prev, qg_ref, kg_ref, c, tm):
    q = _bf(u_ref, C_Q, D)
    rq = lax.rsqrt(_head_mean(q * q, c["sel_q"][...], c["exp_q"][...]) + EPS)
    qhat = q * rq
    qn = (qhat * qg_ref[...]).astype(BF16)
    kband = jnp.concatenate([ukv_prev[:, 0:KVW].astype(F32), _bf(u_ref, C_K, KVW)], axis=0)
    rk = lax.rsqrt(_head_mean(kband * kband, c["sel_k"][...], c["exp_k"][...]) + EPS)
    khat = kband * rk
    knb = (khat * kg_ref[...]).astype(BF16)
    vband = jnp.concatenate([ukv_prev[:, KVW:2 * KVW], u_ref[:, C_V:C_V + KVW]], axis=0)
    kt = [_dot(knb, c["rep"][h]).astype(BF16) for h in range(N_KV)]
    vt = [_dot(vband, c["rep"][h]).astype(BF16) for h in range(N_KV)]
    return qhat, rq, qn, khat, rk, kt, vt


def _attn_masks(is_first, c):
    bias = c["bias"][...]
    bias_first = jnp.where(is_first, c["bias_first"][...], bias)
    lane_grp = lax.broadcasted_iota(jnp.int32, (BLK, KVW), 1) // HEAD
    query_grp = lax.broadcasted_iota(jnp.int32, (1, GROUP * BLK), 1) // BLK
    return bias, bias_first, lane_grp, query_grp


def _sink_row(sinks_ref, h, query_grp):
    row = jnp.full(query_grp.shape, sinks_ref[0, GROUP * h], F32)
    for gi in range(1, GROUP):
        row = jnp.where(query_grp == gi, sinks_ref[0, GROUP * h + gi], row)
    return row


def _stack_groups(a256, lane_grp):
    zero = jnp.zeros_like(a256)
    return jnp.concatenate([jnp.where(lane_grp == gi, a256, zero) for gi in range(GROUP)], axis=0)


def _unstack_groups(a4, lane_grp):
    out = jnp.where(lane_grp == 0, a4[0:BLK], 0.0)
    for gi in range(1, GROUP):
        out = out + jnp.where(lane_grp == gi, a4[gi * BLK:(gi + 1) * BLK], 0.0)
    return out


def _band_sum(parts):
    pieces = [parts[0][0:BLK, :]]
    for b in range(1, len(parts)):
        pieces.append(parts[b - 1][BLK:, :] + parts[b][0:BLK, :])
    pieces.append(parts[-1][BLK:, :])
    return jnp.concatenate(pieces, axis=0)


def _softmax_block(qs, kt_b, bias, sink):
    s = _dot_nt(kt_b, qs) + bias
    m = jnp.maximum(jnp.max(s, axis=0, keepdims=True), sink)
    e = jnp.exp(s - m)
    es = jnp.exp(sink - m)
    inv = 1.0 / (jnp.sum(e, axis=0, keepdims=True) + es)
    return e * inv, es * inv


def _mixer_specs(t, tm, n_tiles, tile_of):
    nb = tm // BLK
    u_spec = pl.BlockSpec((tm, IN_COLS), lambda g: (tile_of(g), 0))
    ukv_prev = pl.BlockSpec((BLK, 2 * KVW), lambda g: (jnp.maximum(tile_of(g) * nb - 1, 0), C_K // (2 * KVW)))
    uvc_prev = pl.BlockSpec((16, D), lambda g: (jnp.maximum(tile_of(g) * (tm // 16) - 1, 0), C_VC // D))
    ucc_prev = pl.BlockSpec((16, D), lambda g: (jnp.maximum(tile_of(g) * (tm // 16) - 1, 0), C_CC // D))
    return u_spec, ukv_prev, uvc_prev, ucc_prev


def _full(shape):
    n = len(shape)
    return pl.BlockSpec(shape, lambda g: (0,) * n)


def mixer_fwd(x, u, cw, qg, kg, sinks, gb, wco, wao, wout, c, name):
    t = x.shape[0]
    tm = min(256, t)
    n_tiles = t // tm
    nb = tm // BLK
    cn = sorted(c)

    def body(x_ref, u_ref, ukv_prev, uvc_prev, ucc_prev, cw_ref, qg_ref, kg_ref, sinks_ref, gb_ref, wco_ref, wao_ref,
             wout_ref, *rest):
        cref = dict(zip(cn, rest[:len(cn)]))
        xo_ref, ya_ref, yb_ref = rest[len(cn):]
        is_first = pl.program_id(0) == 0
        _, _, _, conv = _conv_fwd(u_ref, uvc_prev, ucc_prev, cw_ref, is_first, tm)
        zc = _bf(u_ref, C_ZC, D)
        yc = _bf(u_ref, C_BC, D) * conv * (zc * _sigmoid(zc))
        ya = _dot(yc.astype(BF16), wco_ref[...])

        _, _, qn, _, _, kt, vt = _attn_inputs(u_ref, ukv_prev, qg_ref, kg_ref, cref, tm)
        bias, bias_first, lane_grp, query_grp = _attn_masks(is_first, cref)
        o_cols = []
        for h in range(N_KV):
            sink = _sink_row(sinks_ref, h, query_grp)
            o_rows = []
            for b in range(nb):
                qs = _stack_groups(qn[b * BLK:(b + 1) * BLK, h * KVW:(h + 1) * KVW], lane_grp)
                pn_t, _ = _softmax_block(qs, kt[h][b * BLK:(b + 2) * BLK], bias_first if b == 0 else bias, sink)
                o4 = _dot_tn(pn_t.astype(BF16), vt[h][b * BLK:(b + 2) * BLK])
                o_rows.append(_unstack_groups(o4, lane_grp))
            o_cols.append(jnp.concatenate(o_rows, axis=0))
        za = _bf(u_ref, C_ZA, D)
        ob = jnp.concatenate(o_cols, axis=1) * (za * _sigmoid(za))
        yb = _dot(ob.astype(BF16), wao_ref[...])

        g_a = _sigmoid(_bf(u_ref, C_GA, D) + gb_ref[:, 0:D])
        g_b = _sigmoid(_bf(u_ref, C_GB, D) + gb_ref[:, D:2 * D])
        merged = g_a * ya + g_b * yb
        xo_ref[...] = x_ref[...] + _dot(merged.astype(BF16), wout_ref[...])
        ya_ref[...] = ya.astype(BF16)
        yb_ref[...] = yb.astype(BF16)

    u_spec, ukv_prev, uvc_prev, ucc_prev = _mixer_specs(t, tm, n_tiles, lambda g: g)
    tok = pl.BlockSpec((tm, D), lambda g: (g, 0))
    consts = [c[k] for k in cn]
    return pl.pallas_call(
        body, name=name, grid=(n_tiles,),
        in_specs=[tok, u_spec, ukv_prev, uvc_prev, ucc_prev, _full((8, D)), _full((1, D)), _full((1, KVW)),
                  pl.BlockSpec(memory_space=pltpu.SMEM), _full((1, 2 * D)), _full((D, D)), _full((D, D)), _full((D, D))]
                 + [_full(a.shape) for a in consts],
        out_specs=[tok, tok, tok],
        out_shape=[jax.ShapeDtypeStruct((t, D), F32)] + [jax.ShapeDtypeStruct((t, D), BF16)] * 2,
        compiler_params=_cparams(("arbitrary",)),
    )(x, u, u, u, u, cw, qg, kg, sinks, gb, wco, wao, wout, *consts)


def mixer_bwd(dout, u, ya, yb, cw, qg, kg, sinks, gb, wco, wao, wout, c, name):
    t = dout.shape[0]
    tm = min(256, t)
    n_tiles = t // tm
    nb = tm // BLK
    kb = tm + BLK
    cn = sorted(c)

    def body(dout_ref, u_ref, ukv_prev, uvc_prev, ucc_prev, ya_ref, yb_ref, cw_ref, qg_ref, kg_ref, sinks_ref, gb_ref,
             wco_ref, wao_ref, wout_ref, *rest):
        cref = dict(zip(cn, rest[:len(cn)]))
        (du_ref, small_ref, merged_ref, yc_ref, ob_ref, dya_ref, dyb_ref, carry_kv, carry_conv) = rest[len(cn):]
        g = pl.program_id(0)
        is_first = g == n_tiles - 1

        @pl.when(g == 0)
        def _():
            carry_kv[...] = jnp.zeros_like(carry_kv)
            carry_conv[...] = jnp.zeros_like(carry_conv)
            small_ref[...] = jnp.zeros_like(small_ref)

        dout = dout_ref[...]
        dout_b = dout.astype(BF16)
        ya_v = ya_ref[...].astype(F32)
        yb_v = yb_ref[...].astype(F32)
        g_a = _sigmoid(_bf(u_ref, C_GA, D) + gb_ref[:, 0:D])
        g_b = _sigmoid(_bf(u_ref, C_GB, D) + gb_ref[:, D:2 * D])
        merged = g_a * ya_v + g_b * yb_v
        dmerged = _dot_nt(dout_b, wout_ref[...])
        merged_ref[...] = merged.astype(BF16)
        dya = dmerged * g_a
        dyb = dmerged * g_b
        dgl_a = dmerged * ya_v * g_a * (1.0 - g_a)
        dgl_b = dmerged * yb_v * g_b * (1.0 - g_b)
        du_ref[:, C_GA:C_GA + D] = dgl_a.astype(BF16)
        du_ref[:, C_GB:C_GB + D] = dgl_b.astype(BF16)
        small_ref[SM_GATE:SM_GATE + 1, 0:D] += jnp.sum(dgl_a, axis=0, keepdims=True)
        small_ref[SM_GATE_B:SM_GATE_B + 1, 0:D] += jnp.sum(dgl_b, axis=0, keepdims=True)

        p, p1, p2, conv = _conv_fwd(u_ref, uvc_prev, ucc_prev, cw_ref, is_first, tm)
        zc = _bf(u_ref, C_ZC, D)
        bc = _bf(u_ref, C_BC, D)
        sg = _sigmoid(zc)
        sc = zc * sg
        yc = bc * conv * sc
        dya_b = dya.astype(BF16)
        yc_ref[...] = yc.astype(BF16)
        dya_ref[...] = dya_b
        dyc = _dot_nt(dya_b, wco_ref[...])
        du_ref[:, C_BC:C_BC + D] = (dyc * conv * sc).astype(BF16)
        du_ref[:, C_ZC:C_ZC + D] = (dyc * bc * conv * (sg * (1.0 + zc * (1.0 - sg)))).astype(BF16)
        dconv = dyc * bc * sc
        small_ref[SM_CONV + 2:SM_CONV + 3, 0:D] += jnp.sum(dconv * p, axis=0, keepdims=True)
        small_ref[SM_CONV + 1:SM_CONV + 2, 0:D] += jnp.sum(dconv * p1, axis=0, keepdims=True)
        small_ref[SM_CONV:SM_CONV + 1, 0:D] += jnp.sum(dconv * p2, axis=0, keepdims=True)
        nxt = carry_conv[...]
        d1 = _shift_up(dconv, 1, nxt)
        d2 = _shift_up(dconv, 2, nxt)
        carry_conv[...] = dconv[0:8, :]
        cw = cw_ref[...]
        dp = cw[2:3, :] * dconv + cw[1:2, :] * d1 + cw[0:1, :] * d2
        du_ref[:, C_CC:C_CC + D] = (dp * _bf(u_ref, C_VC, D)).astype(BF16)
        du_ref[:, C_VC:C_VC + D] = (dp * _bf(u_ref, C_CC, D)).astype(BF16)

        dyb_b = dyb.astype(BF16)
        dob = _dot_nt(dyb_b, wao_ref[...])
        za = _bf(u_ref, C_ZA, D)
        sga = _sigmoid(za)
        sa = za * sga
        do = dob * sa
        qhat, rq, qn, khat, rk, kt, vt = _attn_inputs(u_ref, ukv_prev, qg_ref, kg_ref, cref, tm)
        bias, bias_first, lane_grp, query_grp = _attn_masks(is_first, cref)
        lane16 = lax.broadcasted_iota(jnp.int32, (1, D), 1)
        dsink_row = jnp.zeros((1, D), F32)
        o_cols, dq_cols, dk4, dv4 = [], [], [], []
        for h in range(N_KV):
            sink = _sink_row(sinks_ref, h, query_grp)
            dsink = jnp.zeros((1, GROUP * BLK), F32)
            o_rows, dq_rows, dk_parts, dv_parts = [], [], [], []
            for b in range(nb):
                rows = slice(b * BLK, (b + 1) * BLK)
                band = slice(b * BLK, (b + 2) * BLK)
                cols = slice(h * KVW, (h + 1) * KVW)
                qs = _stack_groups(qn[rows, cols], lane_grp)
                pn_t, ps = _softmax_block(qs, kt[h][band], bias_first if b == 0 else bias, sink)
                pn_b = pn_t.astype(BF16)
                o_rows.append(_unstack_groups(_dot_tn(pn_b, vt[h][band]), lane_grp))
                dos = _stack_groups(do[rows, cols].astype(BF16), lane_grp)
                dpn_t = _dot_nt(vt[h][band], dos)
                delta = jnp.sum(pn_t * dpn_t, axis=0, keepdims=True)
                ds_t = (pn_t * (dpn_t - delta)).astype(BF16)
                dsink = dsink - ps * delta
                dq_rows.append(_unstack_groups(_dot_tn(ds_t, kt[h][band]), lane_grp))
                dk_parts.append(_dot(ds_t, qs))
                dv_parts.append(_dot(pn_b, dos))
            o_cols.append(jnp.concatenate(o_rows, axis=0))
            dq_cols.append(jnp.concatenate(dq_rows, axis=0))
            dk4.append(_band_sum(dk_parts))
            dv4.append(_band_sum(dv_parts))
            for gi in range(GROUP):
                tot = jnp.sum(dsink[:, gi * BLK:(gi + 1) * BLK], axis=1, keepdims=True)
                dsink_row = dsink_row + jnp.where(lane16 == GROUP * h + gi, tot, 0.0)
        small_ref[SM_SINK:SM_SINK + 1, :] += dsink_row

        o = jnp.concatenate(o_cols, axis=1)
        ob_ref[...] = (o * sa).astype(BF16)
        dyb_ref[...] = dyb_b
        du_ref[:, C_ZA:C_ZA + D] = (dob * o * (sga * (1.0 + za * (1.0 - sga)))).astype(BF16)

        dqn = jnp.concatenate(dq_cols, axis=1)
        small_ref[SM_QG:SM_QG + 1, 0:D] += SCALE * jnp.sum(dqn * qhat, axis=0, keepdims=True)
        dqh = dqn * qg_ref[...]
        dq = rq * (dqh - qhat * _head_mean(dqh * qhat, cref["sel_q"][...], cref["exp_q"][...]))
        du_ref[:, C_Q:C_Q + D] = dq.astype(BF16)

        dkn_band = jnp.zeros((kb, KVW), F32)
        dv_band = jnp.zeros((kb, KVW), F32)
        for h in range(N_KV):
            dkn_band = dkn_band + _dot2(dk4[h], cref["rep_t"][h])
            dv_band = dv_band + _dot2(dv4[h], cref["rep_t"][h])
        carried = carry_kv[...]
        pad = jnp.zeros((tm - BLK, KVW), F32)
        if nb > 1:
            dkn = dkn_band[BLK:, :] + jnp.concatenate([pad, carried[:, 0:KVW]], axis=0)
            dv = dv_band[BLK:, :] + jnp.concatenate([pad, carried[:, KVW:2 * KVW]], axis=0)
        else:
            dkn = dkn_band[BLK:, :] + carried[:, 0:KVW]
            dv = dv_band[BLK:, :] + carried[:, KVW:2 * KVW]
        carry_kv[:, 0:KVW] = dkn_band[0:BLK, :]
        carry_kv[:, KVW:2 * KVW] = dv_band[0:BLK, :]
        khat_t = khat[BLK:, :]
        small_ref[SM_KG:SM_KG + 1, 0:KVW] += jnp.sum(dkn * khat_t, axis=0, keepdims=True)
        dkh = dkn * kg_ref[...]
        dk = rk[BLK:, :] * (dkh - khat_t * _head_mean(dkh * khat_t, cref["sel_k"][...], cref["exp_k"][...]))
        du_ref[:, C_K:C_K + KVW] = dk.astype(BF16)
        du_ref[:, C_V:C_V + KVW] = dv.astype(BF16)

    rev = lambda g: n_tiles - 1 - g
    u_spec, ukv_prev, uvc_prev, ucc_prev = _mixer_specs(t, tm, n_tiles, rev)
    tok = pl.BlockSpec((tm, D), lambda g: (rev(g), 0))
    consts = [c[k] for k in cn]
    wspec = _full((D, D))
    return pl.pallas_call(
        body, name=name, grid=(n_tiles,),
        in_specs=[tok, u_spec, ukv_prev, uvc_prev, ucc_prev, tok, tok, _full((8, D)), _full((1, D)), _full((1, KVW)),
                  pl.BlockSpec(memory_space=pltpu.SMEM), _full((1, 2 * D)), wspec, wspec, wspec]
                 + [_full(a.shape) for a in consts],
        out_specs=[pl.BlockSpec((tm, IN_COLS), lambda g: (rev(g), 0)), _full((SM_ROWS, D))] + [tok] * 5,
        out_shape=[jax.ShapeDtypeStruct((t, IN_COLS), BF16), jax.ShapeDtypeStruct((SM_ROWS, D), F32)]
                  + [jax.ShapeDtypeStruct((t, D), BF16)] * 5,
        scratch_shapes=[pltpu.VMEM((BLK, 2 * KVW), F32), pltpu.VMEM((8, D), F32)],
        compiler_params=_cparams(("arbitrary",)),
    )(dout, u, u, u, u, ya, yb, cw, qg, kg, sinks, gb, wco, wao, wout, *consts)


def matmul_tn(a, b, name, after=None):
    t, m = a.shape
    tk = min(1024, t)
    mb = 2176 if m == IN_COLS else m
    nk = t // tk

    def body(a_ref, b_ref, *rest):
        o_ref, acc = rest[-2:]
        k = pl.program_id(1)
        prod = _dot_tn(a_ref[...].astype(BF16), b_ref[...].astype(BF16))

        @pl.when(k == 0)
        def _():
            acc[...] = prod

        @pl.when(k > 0)
        def _():
            acc[...] += prod

        @pl.when(k == nk - 1)
        def _():
            o_ref[...] = acc[...].astype(BF16)

    return pl.pallas_call(
        body, name=name, grid=(m // mb, nk),
        in_specs=[pl.BlockSpec((tk, mb), lambda j, k: (k, j)), pl.BlockSpec((tk, D), lambda j, k: (k, 0))]
                 + ([] if after is None else [ANY]),
        out_specs=pl.BlockSpec((mb, D), lambda j, k: (j, 0)),
        out_shape=jax.ShapeDtypeStruct((m, D), BF16),
        scratch_shapes=[pltpu.VMEM((mb, D), F32)],
        compiler_params=_cparams(("arbitrary", "arbitrary")),
    )(a, b, *([] if after is None else [after]))


def inproj_bwd_x(du, w, x, ng, dout, name):
    t = x.shape[0]
    tm = min(1024, t)
    kc = 2176
    nk = IN_COLS // kc

    def body(du_ref, w_ref, x_ref, ng_ref, dout_ref, dx_ref, dng_ref, acc):
        i = pl.program_id(0)
        k = pl.program_id(1)
        prod = _dot(du_ref[...], w_ref[...])

        @pl.when(k == 0)
        def _():
            acc[...] = prod

        @pl.when(k > 0)
        def _():
            acc[...] += prod

        @pl.when((i == 0) & (k == 0))
        def _():
            dng_ref[...] = jnp.zeros_like(dng_ref)

        @pl.when(k == nk - 1)
        def _():
            dh = acc[...]
            xf = x_ref[...]
            r = lax.rsqrt(jnp.mean(xf * xf, axis=-1, keepdims=True) + EPS)
            xhat = xf * r
            dng_ref[0:1, :] += jnp.sum(dh * xhat, axis=0, keepdims=True)
            dxh = dh * ng_ref[...]
            dx_ref[...] = dout_ref[...] + r * (dxh - xhat * jnp.mean(dxh * xhat, axis=-1, keepdims=True))

    tok = pl.BlockSpec((tm, D), lambda i, k: (i, 0))
    return pl.pallas_call(
        body, name=name, grid=(t // tm, nk),
        in_specs=[pl.BlockSpec((tm, kc), lambda i, k: (i, k)), pl.BlockSpec((kc, D), lambda i, k: (k, 0)), tok,
                  pl.BlockSpec((1, D), lambda i, k: (0, 0)), tok],
        out_specs=[tok, pl.BlockSpec((8, D), lambda i, k: (0, 0))],
        out_shape=[jax.ShapeDtypeStruct((t, D), F32), jax.ShapeDtypeStruct((8, D), F32)],
        scratch_shapes=[pltpu.VMEM((tm, D), F32)],
        compiler_params=_cparams(("arbitrary", "arbitrary")),
    )(du, w, x, ng, dout)


def loss_head(y, target, name):
    t = y.shape[0]
    tm = min(1024, t)

    def body(y_ref, t_ref, dy_ref, loss_ref):
        @pl.when(pl.program_id(0) == 0)
        def _():
            loss_ref[...] = jnp.zeros_like(loss_ref)
        err = y_ref[...] - t_ref[...]
        dy_ref[...] = err * (1.0 / D)
        part = jnp.sum(jnp.sum(err * err, axis=-1, keepdims=True) * (1.0 / D), axis=0, keepdims=True)
        loss_ref[...] += 0.5 * part

    tok = pl.BlockSpec((tm, D), lambda i: (i, 0))
    return pl.pallas_call(
        body, name=name, grid=(t // tm,), in_specs=[tok, tok],
        out_specs=[tok, pl.BlockSpec((8, 128), lambda i: (0, 0))],
        out_shape=[jax.ShapeDtypeStruct((t, D), F32), jax.ShapeDtypeStruct((8, 128), F32)],
        compiler_params=_cparams(("arbitrary",)),
    )(y, target)


def layer_operands(l, norm_g, conv_w_full, q_norm_g, k_norm_g, sinks, gate_b, w_in_b, wco_b, wao_b, wout_b):
    return dict(
        ng=norm_g[l][None, :], cw=jnp.pad(conv_w_full[l], ((0, 5), (0, 0))),
        qg=jnp.tile(q_norm_g[l] * SCALE, D // HEAD)[None, :], kg=jnp.tile(k_norm_g[l], N_KV)[None, :],
        sinks=sinks[l][None, :], gb=gate_b[l][None, :],
        w_in=w_in_b[l], wco=wco_b[l], wao=wao_b[l], wout=wout_b[l])


def layer_bwd(dout, saved, lw, c, l, send_off):
    x, u, h, ya, yb = saved
    du, small, merged, yc, ob, dya, dyb = mixer_bwd(dout, u, ya, yb, lw["cw"], lw["qg"], lw["kg"], lw["sinks"], lw["gb"],
                                                    lw["wco"], lw["wao"], lw["wout"], c, f"mixer_bwd_{l}")
    grads = dict(w_in=matmul_tn(du, h, f"dw_in_{l}"), small=small)
    token = send_off(grads, False)
    grads["wout"] = matmul_tn(merged, dout, f"dw_out_{l}", after=token)
    grads["wco"] = matmul_tn(yc, dya, f"dw_conv_out_{l}")
    grads["wao"] = matmul_tn(ob, dyb, f"dw_attn_out_{l}")
    token = send_off(grads, True)
    dx, grads["dng"] = inproj_bwd_x(du, lw["w_in"], x, lw["ng"] + token[0:1, 0:1], dout, f"inproj_bwd_{l}")
    return dx, grads


MESH = pl.DeviceIdType.MESH
ANY = pl.BlockSpec(memory_space=pl.ANY)


def _place():
    return lax.axis_index("x"), lax.axis_index("y"), lax.axis_index("c")


def all_gather(arrs, after, name):
    n = len(arrs)

    def body(*refs):
        ins, outs = refs[:n], refs[n + 1:2 * n + 1]
        send_sems, recv_sems, local_sems = refs[2 * n + 1:]
        x, y, c = _place()
        me, sibling = (x, y, c), (x, y, 1 - c)
        chips = [(1 - x, y), (x, 1 - y), (1 - x, 1 - y)]

        def slot(a, block):
            px, py, pc = block
            return outs[a].at[4 * px + 2 * py + pc]

        def copy(a, k, block, to, src=None):
            return pltpu.make_async_remote_copy(
                src_ref=slot(a, block) if src is None else src, dst_ref=slot(a, block),
                send_sem=send_sems.at[a, k], recv_sem=recv_sems.at[a, k], device_id=to, device_id_type=MESH)

        mine = [pltpu.make_async_copy(ins[a], slot(a, me), local_sems.at[a]) for a in range(n)]
        for cp in mine:
            cp.start()
        first = []
        for a in range(n):
            first.append(copy(a, 0, me, sibling, src=ins[a]))
            first += [copy(a, 1 + j, me, (*chip, c), src=ins[a]) for j, chip in enumerate(chips)]
        for cp in first:
            cp.start()
        passed = []
        for j, chip in enumerate(chips):
            for a in range(n):
                copy(a, 1 + j, (*chip, c), me).wait_recv()
                passed.append(copy(a, 4 + j, (*chip, c), sibling))
                passed[-1].start()
        for a in range(n):
            copy(a, 0, sibling, me).wait_recv()
            for j, chip in enumerate(chips):
                copy(a, 4 + j, (*chip, 1 - c), me).wait_recv()
        for cp in first + passed:
            cp.wait_send()
        for cp in mine:
            cp.wait()

    return pl.pallas_call(
        body, name=name, in_specs=[ANY] * (n + 1), out_specs=[ANY] * n,
        out_shape=[jax.ShapeDtypeStruct((N_DEV,) + a.shape, a.dtype) for a in arrs],
        scratch_shapes=[pltpu.SemaphoreType.DMA((n, 7)), pltpu.SemaphoreType.DMA((n, 7)), pltpu.SemaphoreType.DMA((n,))],
    )(*arrs, after)


HBM_SPEC = pl.BlockSpec(memory_space=pltpu.HBM)
SEM_SPEC = pl.BlockSpec(memory_space=pltpu.SEMAPHORE)
EFFECT = pltpu.SideEffectType.DATAFLOW_SIDE_EFFECTING


ALL_PEERS = (1, 2, 3, 4, 5, 6, 7)


def _flip(place, k):
    x, y, c = place
    return (1 - x if k & 4 else x, 1 - y if k & 2 else y, 1 - c if k & 1 else c)


def _slot(place):
    return 4 * place[0] + 2 * place[1] + place[2]


def _exchange_copies(ins, lands, send_sems, recv_sems, scatter, flips, arriving):
    me = _place()
    out = []
    for a in range(len(ins)):
        for i, k in enumerate(flips):
            peer = _flip(me, k)
            out.append(pltpu.make_async_remote_copy(
                src_ref=ins[a].at[_slot(peer)] if scatter else ins[a],
                dst_ref=lands[a].at[_slot(peer) if arriving else _slot(me)],
                send_sem=send_sems.at[a * len(flips) + i], recv_sem=recv_sems.at[a * len(flips) + i],
                device_id=peer, device_id_type=MESH))
    return out


def exchange_start(arrs, scatter, flips, after, name):
    n = len(arrs)
    lands = [lax.empty(a.shape if scatter else (N_DEV,) + a.shape, a.dtype) for a in arrs]

    def body(*refs):
        ins, lz = refs[:n], refs[n:2 * n]
        send_sems, recv_sems = refs[2 * n + 1], refs[2 * n + 2]
        token = refs[-1]
        for cp in _exchange_copies(ins, lz, send_sems, recv_sems, scatter, flips, False):
            cp.start()
        token[...] = jnp.zeros_like(token)

    sems = pltpu.SemaphoreType.DMA((n * len(flips),))
    res = pl.pallas_call(
        body, name=name,
        out_shape=(sems, sems, *[pltpu.HBM(a.shape, a.dtype) for a in arrs], *[pltpu.HBM(a.shape, a.dtype) for a in lands],
                   jax.ShapeDtypeStruct((8, 128), F32)),
        in_specs=[HBM_SPEC] * (2 * n) + [ANY],
        out_specs=(SEM_SPEC, SEM_SPEC, *[HBM_SPEC] * (2 * n), pl.BlockSpec(memory_space=pltpu.VMEM)),
        input_output_aliases={i: 2 + i for i in range(2 * n)},
        compiler_params=pltpu.CompilerParams(has_side_effects=EFFECT),
    )(*[pltpu.with_memory_space_constraint(a, pltpu.HBM) for a in arrs],
      *[pltpu.with_memory_space_constraint(a, pltpu.HBM) for a in lands], after)
    return dict(send=res[0], recv=res[1], srcs=res[2:2 + n], lands=res[2 + n:2 + 2 * n], token=res[-1], scatter=scatter,
                flips=flips)


def exchange_wait(state, after, name):
    n = len(state["srcs"])

    def body(*refs):
        ins, lz = refs[:n], refs[n:2 * n]
        send_sems, recv_sems = refs[2 * n], refs[2 * n + 1]
        for cp in _exchange_copies(ins, lz, send_sems, recv_sems, state["scatter"], state["flips"], True):
            cp.wait_send()
            cp.wait_recv()

    both = list(state["srcs"]) + list(state["lands"])
    res = pl.pallas_call(
        body, name=name, out_shape=tuple(pltpu.HBM(a.shape, a.dtype) for a in both),
        in_specs=[HBM_SPEC] * (2 * n) + [SEM_SPEC, SEM_SPEC, ANY], out_specs=tuple([HBM_SPEC] * (2 * n)),
        input_output_aliases={i: i for i in range(2 * n)},
        compiler_params=pltpu.CompilerParams(has_side_effects=EFFECT),
    )(*both, state["send"], state["recv"], after)
    return res[:n], res[n:]


OTHER_CHIPS = (2, 4, 6)
SAME_CORE = (1, 2, 4, 6)


def _forward_copies(lands, send_sems, recv_sems, arriving):
    me = _place()
    sibling = _flip(me, 1)
    origin = sibling if arriving else me
    return [pltpu.make_async_remote_copy(
        src_ref=lands[a].at[_slot(_flip(origin, k))], dst_ref=lands[a].at[_slot(_flip(origin, k))],
        send_sem=send_sems.at[a * len(OTHER_CHIPS) + i], recv_sem=recv_sems.at[a * len(OTHER_CHIPS) + i],
        device_id=sibling, device_id_type=MESH) for a in range(len(lands)) for i, k in enumerate(OTHER_CHIPS)]


def forward_start(lands, after, name):
    n = len(lands)

    def body(*refs):
        lz = refs[:n]
        send_sems, recv_sems = refs[n + 1], refs[n + 2]
        token = refs[-1]
        for cp in _forward_copies(lz, send_sems, recv_sems, False):
            cp.start()
        token[...] = jnp.zeros_like(token)

    sems = pltpu.SemaphoreType.DMA((n * len(OTHER_CHIPS),))
    res = pl.pallas_call(
        body, name=name,
        out_shape=(sems, sems, *[pltpu.HBM(a.shape, a.dtype) for a in lands], jax.ShapeDtypeStruct((8, 128), F32)),
        in_specs=[HBM_SPEC] * n + [ANY],
        out_specs=(SEM_SPEC, SEM_SPEC, *[HBM_SPEC] * n, pl.BlockSpec(memory_space=pltpu.VMEM)),
        input_output_aliases={i: 2 + i for i in range(n)},
        compiler_params=pltpu.CompilerParams(has_side_effects=EFFECT),
    )(*[pltpu.with_memory_space_constraint(a, pltpu.HBM) for a in lands], after)
    return dict(send=res[0], recv=res[1], lands=res[2:2 + n], token=res[-1])


def forward_wait(state, after, name):
    n = len(state["lands"])

    def body(*refs):
        lz = refs[:n]
        send_sems, recv_sems = refs[n], refs[n + 1]
        for cp in _forward_copies(lz, send_sems, recv_sems, False):
            cp.wait_send()
        for cp in _forward_copies(lz, send_sems, recv_sems, True):
            cp.wait_recv()

    return pl.pallas_call(
        body, name=name, out_shape=tuple(pltpu.HBM(a.shape, a.dtype) for a in state["lands"]),
        in_specs=[HBM_SPEC] * n + [SEM_SPEC, SEM_SPEC, ANY], out_specs=tuple([HBM_SPEC] * n),
        input_output_aliases={i: i for i in range(n)},
        compiler_params=pltpu.CompilerParams(has_side_effects=EFFECT),
    )(*state["lands"], state["send"], state["recv"], after)


def adamw(w, m, v, parts, name):
    r, cdim = w.shape
    n_parts = parts.shape[0]
    rb = 256 if r % 256 == 0 else (SHARD_COLS // 4 if r == SHARD_COLS else r)

    def body(w_ref, m_ref, v_ref, p_ref, g_ref, d_ref, mo_ref, vo_ref):
        g = p_ref[0].astype(F32)
        for i in range(1, n_parts):
            g = g + p_ref[i].astype(F32)
        m_new = ADAM_B1 * m_ref[...] + (1.0 - ADAM_B1) * g
        v_new = ADAM_B2 * v_ref[...] + (1.0 - ADAM_B2) * (g * g)
        m_hat = m_new / (1.0 - ADAM_B1 ** ADAM_STEP)
        v_hat = v_new / (1.0 - ADAM_B2 ** ADAM_STEP)
        g_ref[...] = g
        d_ref[...] = -ADAM_LR * (m_hat / (jnp.sqrt(v_hat) + ADAM_EPS) + ADAM_WD * w_ref[...])
        mo_ref[...] = m_new
        vo_ref[...] = v_new

    blk = pl.BlockSpec((rb, cdim), lambda i: (i, 0))
    return pl.pallas_call(
        body, name=name, grid=(r // rb,),
        in_specs=[blk, blk, blk, pl.BlockSpec((n_parts, rb, cdim), lambda i: (0, i, 0))],
        out_specs=[blk] * 4, out_shape=[jax.ShapeDtypeStruct((r, cdim), F32)] * 4,
        compiler_params=_cparams(("arbitrary",)),
    )(w, m, v, parts)


def adamw_layers(w, m, v, lands, srcs, me, lo, prev, name):
    nl_all, r, cdim = w.shape
    nl = len(lands)
    rb = 256 if r % 256 == 0 else (SHARD_COLS // 4 if r == SHARD_COLS else r)
    nblk = r // rb
    n_prev = 0 if prev is None else 4

    def body(me_ref, w_ref, m_ref, v_ref, *rest):
        land_refs, src_refs = rest[:nl], rest[nl:2 * nl]
        g_ref, d_ref, mo_ref, vo_ref = rest[2 * nl + n_prev:]
        for k in range(nl):
            @pl.when(pl.program_id(0) == k)
            def _(k=k):
                own = src_refs[k][...].astype(F32)
                g = jnp.where(me_ref[0] == 0, own, land_refs[k][0].astype(F32))
                for i in range(1, N_DEV):
                    g = g + jnp.where(me_ref[0] == i, own, land_refs[k][i].astype(F32))
                m_new = ADAM_B1 * m_ref[...] + (1.0 - ADAM_B1) * g
                v_new = ADAM_B2 * v_ref[...] + (1.0 - ADAM_B2) * (g * g)
                m_hat = m_new / (1.0 - ADAM_B1 ** ADAM_STEP)
                v_hat = v_new / (1.0 - ADAM_B2 ** ADAM_STEP)
                g_ref[...] = g
                d_ref[...] = -ADAM_LR * (m_hat / (jnp.sqrt(v_hat) + ADAM_EPS) + ADAM_WD * w_ref[...])
                mo_ref[...] = m_new
                vo_ref[...] = v_new

    blk = pl.BlockSpec((None, rb, cdim), lambda l, i, me_ref: (lo + l, i, 0))

    def rows(l, i, k):
        return jnp.where(l < k, 0, jnp.where(l == k, i, nblk - 1))

    land_specs = [pl.BlockSpec((N_DEV, rb, cdim), lambda l, i, me_ref, k=k: (0, rows(l, i, k), 0)) for k in range(nl)]
    src_specs = [pl.BlockSpec((None, rb, cdim), lambda l, i, me_ref, k=k: (me_ref[0], rows(l, i, k), 0)) for k in range(nl)]
    return pl.pallas_call(
        body, name=name,
        grid_spec=pltpu.PrefetchScalarGridSpec(
            num_scalar_prefetch=1, grid=(nl, nblk),
            in_specs=[blk, blk, blk] + land_specs + src_specs + [ANY] * n_prev, out_specs=[blk] * 4),
        out_shape=[jax.ShapeDtypeStruct((nl_all, r, cdim), F32)] * 4,
        input_output_aliases={4 + 2 * nl + j: j for j in range(n_prev)},
        compiler_params=_cparams(("arbitrary", "arbitrary")),
    )(me, w, m, v, *lands, *srcs, *([] if prev is None else prev))


def small_sum(parts, fold, name):
    rows = parts.shape[1]

    def dot3(xv, sel):
        out = jnp.zeros((xv.shape[0], sel.shape[1]), F32)
        for _ in range(3):
            hi = xv.astype(BF16)
            out = out + _dot(hi, sel)
            xv = xv - hi.astype(F32)
        return out

    def body(p_ref, fold_ref, o_ref):
        tot = p_ref[0]
        for i in range(1, N_DEV):
            tot = tot + p_ref[i]
        o_ref[...] = tot
        for l in range(rows // SM_ROWS):
            blk = tot[l * SM_ROWS:l * SM_ROWS + 8, 0:D]
            folded = dot3(blk, fold_ref[...])
            o_ref[l * SM_ROWS + SM_QG_FOLDED:l * SM_ROWS + SM_QG_FOLDED + 1, 0:128] = folded[SM_QG:SM_QG + 1, :]
            o_ref[l * SM_ROWS + SM_KG_FOLDED:l * SM_ROWS + SM_KG_FOLDED + 1, 0:128] = folded[SM_KG:SM_KG + 1, :]

    return pl.pallas_call(
        body, name=name, out_shape=jax.ShapeDtypeStruct((rows, D), F32),
        compiler_params=_cparams(None),
    )(parts, fold)


def kernel(x, norm_g, w_in, conv_w, q_norm_g, k_norm_g, sinks, w_conv_out, w_attn_out, gate_b, w_out, loss_target, m_norm_g, m_w_in, m_conv_w, m_q_norm_g, m_k_norm_g, m_sinks, m_w_conv_out, m_w_attn_out, m_gate_b, m_w_out, v_norm_g, v_w_in, v_conv_w, v_q_norm_g, v_k_norm_g, v_sinks, v_w_conv_out, v_w_attn_out, v_gate_b, v_w_out):
    c = _selectors()
    me = 4 * lax.axis_index("x") + 2 * lax.axis_index("y") + lax.axis_index("c")

    w_in_t, m_w_in_t, v_w_in_t = (jnp.swapaxes(a, 1, 2) for a in (w_in, m_w_in, v_w_in))

    def shards(l):
        return [w_in_t[l].astype(BF16), w_conv_out[l].astype(BF16), w_attn_out[l].astype(BF16), w_out[l].astype(BF16)]

    def ici_start(l, after):
        return exchange_start(shards(l), False, SAME_CORE, after, f"gather_start_{l}")

    h = x[0]
    saved, lws = [], []
    lands = all_gather(shards(0) + [conv_w], h, "gather_0")
    conv_full = jnp.transpose(lands[4], (1, 2, 0, 3)).reshape(DEPTH, 3, D)
    ici = ici_start(1, lands[1])
    for l in range(DEPTH):
        lws.append(layer_operands(l, norm_g, conv_full, q_norm_g, k_norm_g, sinks, gate_b,
                                  {l: lands[0].reshape(IN_COLS, D)}, {l: lands[1].reshape(D, D)},
                                  {l: lands[2].reshape(D, D)}, {l: lands[3].reshape(D, D)}))
        ng_l = lws[l]["ng"] + ici["token"][0:1, 0:1] if l == 0 else lws[l]["ng"]
        u, hb = inproj_fwd(h, ng_l, lws[l]["w_in"], f"inproj_fwd_{l}")
        gb_l = lws[l]["gb"]
        if l + 1 < DEPTH:
            mine, arrived = exchange_wait(ici, u, f"gather_wait_{l + 1}")
            chip = forward_start(arrived, mine[0], f"gather_forward_start_{l + 1}")
            started = chip["token"]
            if l + 2 < DEPTH:
                ici = ici_start(l + 2, started)
                started = ici["token"]
            gb_l = gb_l + started[0:1, 0:1]
        x_in = h
        h, ya, yb = mixer_fwd(x_in, u, lws[l]["cw"], lws[l]["qg"], lws[l]["kg"], lws[l]["sinks"], gb_l,
                              lws[l]["wco"], lws[l]["wao"], lws[l]["wout"], c, f"mixer_fwd_{l}")
        saved.append((x_in, u, hb, ya, yb))
        if l + 1 < DEPTH:
            lands = forward_wait(chip, h, f"gather_forward_wait_{l + 1}")
            lands = [lax.dynamic_update_index_in_dim(land, src, me, 0) for land, src in zip(lands, mine)]
    dh, loss_part = loss_head(h, loss_target[0], "loss_head")

    grads, scatters = [None] * DEPTH, [[] for _ in range(DEPTH)]
    for l in reversed(range(DEPTH)):
        def send_off(g, done, l=l):
            rest = [g[k].reshape(N_DEV, SHARD_ROWS, D) for k in ("wco", "wao", "wout")] if done else []
            first = [g["w_in"].reshape(N_DEV, SHARD_COLS, D)] if done == (l > 0) else []
            if not first + rest:
                return None
            tag = f"{l}" if l > 0 else ("0_rest" if done else "0_in")
            scatters[l].append(exchange_start(first + rest, True, ALL_PEERS, g["small"], f"scatter_start_{tag}"))
            return scatters[l][-1]["token"]
        dh, grads[l] = layer_bwd(dh, saved[l], lws[l], c, l, send_off)

    me1 = me.astype(jnp.int32).reshape(1)
    mine, lands = {}, {}
    for l in (3, 2, 1):
        mine[l], lands[l] = exchange_wait(scatters[l][0], dh, f"scatter_wait_{l}")
    weights = [(w_in_t, m_w_in_t, v_w_in_t, "w_in"), (w_conv_out, m_w_conv_out, v_w_conv_out, "w_conv_out"),
               (w_attn_out, m_w_attn_out, v_w_attn_out, "w_attn_out"), (w_out, m_w_out, v_w_out, "w_out")]
    upd = [adamw_layers(w, m, v, [lands[l][i] for l in (1, 2, 3)], [mine[l][i] for l in (1, 2, 3)], me1, 1, None,
                        f"adamw_{n}_upper") for i, (w, m, v, n) in enumerate(weights)]
    blocks = []
    for l in range(DEPTH):
        blk = grads[l]["small"]
        blk = blk.at[SM_NORM, 0:D].set(grads[l]["dng"][0])
        if l == 0:
            blk = blk.at[SM_LOSS, 0:128].set(loss_part[0])
        blocks.append(blk)
    small_x = exchange_start([jnp.concatenate(blocks, axis=0)], False, ALL_PEERS, grads[0]["dng"], "gather_small_start")

    m_in, l_in = exchange_wait(scatters[0][0], upd[0][0], "scatter_wait_0_in")
    upd[0] = adamw_layers(*weights[0][:3], [l_in[0]], [m_in[0]], me1, 0, upd[0], "adamw_w_in_0")
    m_rest, l_rest = exchange_wait(scatters[0][1], upd[0][0], "scatter_wait_0_rest")
    for i in (1, 2, 3):
        upd[i] = adamw_layers(*weights[i][:3], [l_rest[i - 1]], [m_rest[i - 1]], me1, 0, upd[i], f"adamw_{weights[i][3]}_0")
    u_in = [jnp.swapaxes(o, 1, 2) for o in upd[0]]
    u_co, u_ao, u_out = upd[1], upd[2], upd[3]

    mine_s, lands_s = exchange_wait(small_x, u_out[0], "gather_small_wait")
    gathered = lax.dynamic_update_index_in_dim(lands_s[0], mine_s[0], me, 0)
    tot = small_sum(gathered, c["fold"], "small_sum")
    tot = tot.reshape(DEPTH, SM_ROWS, D)
    loss = tot[0, SM_LOSS, 0]

    def update_small(w, m, v, g, name):
        return adamw(w, m, v, g[None], name)

    u_ng = update_small(norm_g, m_norm_g, v_norm_g, tot[:, SM_NORM, 0:D], "adamw_norm_g")
    u_qg = update_small(q_norm_g, m_q_norm_g, v_q_norm_g, tot[:, SM_QG_FOLDED, 0:HEAD], "adamw_q_norm_g")
    u_kg = update_small(k_norm_g, m_k_norm_g, v_k_norm_g, tot[:, SM_KG_FOLDED, 0:HEAD], "adamw_k_norm_g")
    u_sk = update_small(sinks, m_sinks, v_sinks, tot[:, SM_SINK, 0:16], "adamw_sinks")
    g_gate = jnp.concatenate([tot[:, SM_GATE, :], tot[:, SM_GATE_B, :]], axis=1)
    u_gb = update_small(gate_b, m_gate_b, v_gate_b, g_gate, "adamw_gate_b")
    g_conv = lax.dynamic_slice_in_dim(tot[:, SM_CONV:SM_CONV + 3, 0:D], me * SHARD_ROWS, SHARD_ROWS, axis=2)
    u_cw = [o.reshape(DEPTH, 3, SHARD_ROWS) for o in update_small(
        conv_w.reshape(DEPTH * 3, SHARD_ROWS), m_conv_w.reshape(DEPTH * 3, SHARD_ROWS),
        v_conv_w.reshape(DEPTH * 3, SHARD_ROWS), g_conv.reshape(DEPTH * 3, SHARD_ROWS), "adamw_conv_w")]

    order = [u_ng, u_in, u_cw, u_qg, u_kg, u_sk, u_co, u_ao, u_gb, u_out]
    return (loss, dh[None], *[u[0] for u in order], *[u[1] for u in order], *[u[2] for u in order], *[u[3] for u in order])
```

```python
import functools

import jax
import jax.numpy as jnp
from jax import lax
from jax.experimental import pallas as pl
from jax.experimental.pallas import tpu as pltpu

F32 = jnp.float32
BF16 = jnp.bfloat16

N_DEV = 8
DEPTH = 4
D = 1024
N_KV = 4
GROUP = 4
HEAD = 64
BLK = 128
KVW = N_KV * HEAD
IN_COLS = 8704
SHARD_COLS = IN_COLS // N_DEV
SHARD_ROWS = D // N_DEV
C_VC, C_BC, C_CC, C_ZC, C_Q, C_K, C_V, C_ZA, C_GA, C_GB = 0, 1024, 2048, 3072, 4096, 5120, 5376, 5632, 6656, 7680
EPS = 1e-6
NEG_INF = -1e30
SCALE = HEAD ** -0.5

ADAM_LR = 0.001
ADAM_B1 = 0.9
ADAM_B2 = 0.999
ADAM_EPS = 1e-08
ADAM_WD = 0.01
ADAM_STEP = 10

VMEM_LIMIT = 60 * 1024 * 1024
SM_ROWS = 16
SM_GATE, SM_CONV, SM_QG, SM_KG, SM_SINK, SM_NORM, SM_LOSS, SM_GATE_B, SM_QG_FOLDED, SM_KG_FOLDED = 0, 1, 4, 5, 6, 7, 8, 9, 10, 11


def _cparams(sem):
    return pltpu.CompilerParams(dimension_semantics=sem, vmem_limit_bytes=VMEM_LIMIT)


def _dot(a, b):
    return jnp.dot(a, b, preferred_element_type=F32)


def _dot_nt(a, b):
    return lax.dot_general(a, b, (((1,), (1,)), ((), ())), preferred_element_type=F32)


def _dot_tn(a, b):
    return lax.dot_general(a, b, (((0,), (0,)), ((), ())), preferred_element_type=F32)


def _dot2(x, sel):
    hi = x.astype(BF16)
    lo = (x - hi.astype(F32)).astype(BF16)
    return _dot(hi, sel) + _dot(lo, sel)


def _sigmoid(z):
    return 0.5 * jnp.tanh(0.5 * z) + 0.5


def _head_mean(t, sel, exp):
    return _dot2(_dot(t.astype(BF16), sel) * (1.0 / HEAD), exp)


def _shift_down(a, k, before):
    r = pltpu.roll(a, k, 0)
    row = lax.broadcasted_iota(jnp.int32, (8, 1), 0)
    head = jnp.where(row < k, pltpu.roll(before, k, 0), r[0:8, :])
    return jnp.concatenate([head, r[8:, :]], axis=0)


def _shift_up(a, k, after):
    n = a.shape[0]
    r = pltpu.roll(a, n - k, 0)
    row = lax.broadcasted_iota(jnp.int32, (8, 1), 0)
    tail = jnp.where(row >= 8 - k, pltpu.roll(after, 8 - k, 0), r[n - 8:, :])
    return jnp.concatenate([r[:n - 8, :], tail], axis=0)


def _bf(ref, c0, width):
    return ref[:, c0:c0 + width].astype(F32)


def _selectors():
    c = jnp.arange(D)
    sel_q = (c[:, None] // HEAD == jnp.arange(128)[None, :]).astype(BF16)
    ck = jnp.arange(KVW)
    sel_k = (ck[:, None] // HEAD == jnp.arange(128)[None, :]).astype(BF16)
    src = jnp.arange(KVW)[:, None]
    dst = jnp.arange(KVW)[None, :]
    rep = jnp.stack([((src // HEAD == h) & (src % HEAD == dst % HEAD)).astype(BF16) for h in range(N_KV)])
    fold = (c[:, None] % HEAD == jnp.arange(128)[None, :]).astype(BF16)
    qq = jnp.arange(GROUP * BLK)[:, None] % BLK
    kk = jnp.arange(2 * BLK)[None, :]
    valid = (kk > qq) & (kk <= qq + BLK)
    bias = jnp.where(valid, 0.0, NEG_INF).astype(F32).T
    bias_first = jnp.where(valid & (kk >= BLK), 0.0, NEG_INF).astype(F32).T
    return dict(bias=bias, bias_first=bias_first,sel_q=sel_q, exp_q=sel_q.T, sel_k=sel_k, exp_k=sel_k.T, rep=rep, rep_t=jnp.swapaxes(rep, 1, 2), fold=fold)


def inproj_fwd(x, ng, w, name):
    t = x.shape[0]
    tm = min(1024, t)
    cb = 2176
    def body(x_ref, ng_ref, w_ref, u_ref, h_ref, h_scr):
        @pl.when(pl.program_id(1) == 0)
        def _():
            xf = x_ref[...]
            r = lax.rsqrt(jnp.mean(xf * xf, axis=-1, keepdims=True) + EPS)
            hb = (xf * r * ng_ref[...]).astype(BF16)
            h_scr[...] = hb
            h_ref[...] = hb
        u_ref[...] = _dot_nt(h_scr[...], w_ref[...]).astype(BF16)

    return pl.pallas_call(
        body, name=name, grid=(t // tm, IN_COLS // cb),
        in_specs=[pl.BlockSpec((tm, D), lambda i, j: (i, 0)), pl.BlockSpec((1, D), lambda i, j: (0, 0)),
                  pl.BlockSpec((cb, D), lambda i, j: (j, 0))],
        out_specs=[pl.BlockSpec((tm, cb), lambda i, j: (i, j)), pl.BlockSpec((tm, D), lambda i, j: (i, 0))],
        out_shape=[jax.ShapeDtypeStruct((t, IN_COLS), BF16), jax.ShapeDtypeStruct((t, D), BF16)],
        scratch_shapes=[pltpu.VMEM((tm, D), BF16)],
        compiler_params=_cparams(("arbitrary", "arbitrary")),
    )(x, ng, w)


def _conv_fwd(u_ref, uvc_prev, ucc_prev, cw_ref, is_first, tm):
    p = _bf(u_ref, C_CC, D) * _bf(u_ref, C_VC, D)
    pprev = ucc_prev[...].astype(F32) * uvc_prev[...].astype(F32)
    pprev = jnp.where(is_first, 0.0, pprev)
    p1 = _shift_down(p, 1, pprev[8:16, :])
    p2 = _shift_down(p, 2, pprev[8:16, :])
    cw = cw_ref[...]
    conv = cw[0:1, :] * p2 + cw[1:2, :] * p1 + cw[2:3, :] * p
    return p, p1, p2, conv


def _attn_inputs(u_ref, ukv_prev, qg_ref, kg_ref, c, tm):
    q = _bf(u_ref, C_Q, D)
    rq = lax.rsqrt(_head_mean(q * q, c["sel_q"][...], c["exp_q"][...]) + EPS)
    qhat = q * rq
    qn = (qhat * qg_ref[...]).astype(BF16)
    kband = jnp.concatenate([ukv_prev[:, 0:KVW].astype(F32), _bf(u_ref, C_K, KVW)], axis=0)
    rk = lax.rsqrt(_head_mean(kband * kband, c["sel_k"][...], c["exp_k"][...]) + EPS)
    khat = kband * rk
    knb = (khat * kg_ref[...]).astype(BF16)
    vband = jnp.concatenate([ukv_prev[:, KVW:2 * KVW], u_ref[:, C_V:C_V + KVW]], axis=0)
    kt = [_dot(knb, c["rep"][h]).astype(BF16) for h in range(N_KV)]
    vt = [_dot(vband, c["rep"][h]).astype(BF16) for h in range(N_KV)]
    return qhat, rq, qn, khat, rk, kt, vt


def _attn_masks(is_first, c):
    bias = c["bias"][...]
    bias_first = jnp.where(is_first, c["bias_first"][...], bias)
    lane_grp = lax.broadcasted_iota(jnp.int32, (BLK, KVW), 1) // HEAD
    query_grp = lax.broadcasted_iota(jnp.int32, (1, GROUP * BLK), 1) // BLK
    return bias, bias_first, lane_grp, query_grp


def _sink_row(sinks_ref, h, query_grp):
    row = jnp.full(query_grp.shape, sinks_ref[0, GROUP * h], F32)
    for gi in range(1, GROUP):
        row = jnp.where(query_grp == gi, sinks_ref[0, GROUP * h + gi], row)
    return row


def _stack_groups(a256, lane_grp):
    zero = jnp.zeros_like(a256)
    return jnp.concatenate([jnp.where(lane_grp == gi, a256, zero) for gi in range(GROUP)], axis=0)


def _unstack_groups(a4, lane_grp):
    out = jnp.where(lane_grp == 0, a4[0:BLK], 0.0)
    for gi in range(1, GROUP):
        out = out + jnp.where(lane_grp == gi, a4[gi * BLK:(gi + 1) * BLK], 0.0)
    return out


def _band_sum(parts):
    pieces = [parts[0][0:BLK, :]]
    for b in range(1, len(parts)):
        pieces.append(parts[b - 1][BLK:, :] + parts[b][0:BLK, :])
    pieces.append(parts[-1][BLK:, :])
    return jnp.concatenate(pieces, axis=0)


def _softmax_block(qs, kt_b, bias, sink):
    s = _dot_nt(kt_b, qs) + bias
    m = jnp.maximum(jnp.max(s, axis=0, keepdims=True), sink)
    e = jnp.exp(s - m)
    es = jnp.exp(sink - m)
    inv = 1.0 / (jnp.sum(e, axis=0, keepdims=True) + es)
    return e * inv, es * inv


def _mixer_specs(t, tm, n_tiles, tile_of):
    nb = tm // BLK
    u_spec = pl.BlockSpec((tm, IN_COLS), lambda g: (tile_of(g), 0))
    ukv_prev = pl.BlockSpec((BLK, 2 * KVW), lambda g: (jnp.maximum(tile_of(g) * nb - 1, 0), C_K // (2 * KVW)))
    uvc_prev = pl.BlockSpec((16, D), lambda g: (jnp.maximum(tile_of(g) * (tm // 16) - 1, 0), C_VC // D))
    ucc_prev = pl.BlockSpec((16, D), lambda g: (jnp.maximum(tile_of(g) * (tm // 16) - 1, 0), C_CC // D))
    return u_spec, ukv_prev, uvc_prev, ucc_prev


def _full(shape):
    n = len(shape)
    return pl.BlockSpec(shape, lambda g: (0,) * n)


def mixer_fwd(x, u, cw, qg, kg, sinks, gb, wco, wao, wout, c, name):
    t = x.shape[0]
    tm = min(256, t)
    n_tiles = t // tm
    nb = tm // BLK
    cn = sorted(c)

    def body(x_ref, u_ref, ukv_prev, uvc_prev, ucc_prev, cw_ref, qg_ref, kg_ref, sinks_ref, gb_ref, wco_ref, wao_ref,
             wout_ref, *rest):
        cref = dict(zip(cn, rest[:len(cn)]))
        xo_ref, ya_ref, yb_ref = rest[len(cn):]
        is_first = pl.program_id(0) == 0
        _, _, _, conv = _conv_fwd(u_ref, uvc_prev, ucc_prev, cw_ref, is_first, tm)
        zc = _bf(u_ref, C_ZC, D)
        yc = _bf(u_ref, C_BC, D) * conv * (zc * _sigmoid(zc))
        ya = _dot(yc.astype(BF16), wco_ref[...])

        _, _, qn, _, _, kt, vt = _attn_inputs(u_ref, ukv_prev, qg_ref, kg_ref, cref, tm)
        bias, bias_first, lane_grp, query_grp = _attn_masks(is_first, cref)
        o_cols = []
        for h in range(N_KV):
            sink = _sink_row(sinks_ref, h, query_grp)
            o_rows = []
            for b in range(nb):
                qs = _stack_groups(qn[b * BLK:(b + 1) * BLK, h * KVW:(h + 1) * KVW], lane_grp)
                pn_t, _ = _softmax_block(qs, kt[h][b * BLK:(b + 2) * BLK], bias_first if b == 0 else bias, sink)
                o4 = _dot_tn(pn_t.astype(BF16), vt[h][b * BLK:(b + 2) * BLK])
                o_rows.append(_unstack_groups(o4, lane_grp))
            o_cols.append(jnp.concatenate(o_rows, axis=0))
        za = _bf(u_ref, C_ZA, D)
        ob = jnp.concatenate(o_cols, axis=1) * (za * _sigmoid(za))
        yb = _dot(ob.astype(BF16), wao_ref[...])

        g_a = _sigmoid(_bf(u_ref, C_GA, D) + gb_ref[:, 0:D])
        g_b = _sigmoid(_bf(u_ref, C_GB, D) + gb_ref[:, D:2 * D])
        merged = g_a * ya + g_b * yb
        xo_ref[...] = x_ref[...] + _dot(merged.astype(BF16), wout_ref[...])
        ya_ref[...] = ya.astype(BF16)
        yb_ref[...] = yb.astype(BF16)

    u_spec, ukv_prev, uvc_prev, ucc_prev = _mixer_specs(t, tm, n_tiles, lambda g: g)
    tok = pl.BlockSpec((tm, D), lambda g: (g, 0))
    consts = [c[k] for k in cn]
    return pl.pallas_call(
        body, name=name, grid=(n_tiles,),
        in_specs=[tok, u_spec, ukv_prev, uvc_prev, ucc_prev, _full((8, D)), _full((1, D)), _full((1, KVW)),
                  pl.BlockSpec(memory_space=pltpu.SMEM), _full((1, 2 * D)), _full((D, D)), _full((D, D)), _full((D, D))]
                 + [_full(a.shape) for a in consts],
        out_specs=[tok, tok, tok],
        out_shape=[jax.ShapeDtypeStruct((t, D), F32)] + [jax.ShapeDtypeStruct((t, D), BF16)] * 2,
        compiler_params=_cparams(("arbitrary",)),
    )(x, u, u, u, u, cw, qg, kg, sinks, gb, wco, wao, wout, *consts)


def mixer_bwd(dout, u, ya, yb, cw, qg, kg, sinks, gb, wco, wao, wout, c, name):
    t = dout.shape[0]
    tm = min(256, t)
    n_tiles = t // tm
    nb = tm // BLK
    kb = tm + BLK
    cn = sorted(c)

    def body(dout_ref, u_ref, ukv_prev, uvc_prev, ucc_prev, ya_ref, yb_ref, cw_ref, qg_ref, kg_ref, sinks_ref, gb_ref,
             wco_ref, wao_ref, wout_ref, *rest):
        cref = dict(zip(cn, rest[:len(cn)]))
        (du_ref, small_ref, merged_ref, yc_ref, ob_ref, dya_ref, dyb_ref, carry_kv, carry_conv) = rest[len(cn):]
        g = pl.program_id(0)
        is_first = g == n_tiles - 1

        @pl.when(g == 0)
        def _():
            carry_kv[...] = jnp.zeros_like(carry_kv)
            carry_conv[...] = jnp.zeros_like(carry_conv)
            small_ref[...] = jnp.zeros_like(small_ref)

        dout = dout_ref[...]
        dout_b = dout.astype(BF16)
        ya_v = ya_ref[...].astype(F32)
        yb_v = yb_ref[...].astype(F32)
        g_a = _sigmoid(_bf(u_ref, C_GA, D) + gb_ref[:, 0:D])
        g_b = _sigmoid(_bf(u_ref, C_GB, D) + gb_ref[:, D:2 * D])
        merged = g_a * ya_v + g_b * yb_v
        dmerged = _dot_nt(dout_b, wout_ref[...])
        merged_ref[...] = merged.astype(BF16)
        dya = dmerged * g_a
        dyb = dmerged * g_b
        dgl_a = dya * ya_v * (1.0 - g_a)
        dgl_b = dyb * yb_v * (1.0 - g_b)
        du_ref[:, C_GA:C_GA + D] = dgl_a.astype(BF16)
        du_ref[:, C_GB:C_GB + D] = dgl_b.astype(BF16)
        small_ref[SM_GATE:SM_GATE + 1, 0:D] += jnp.sum(dgl_a, axis=0, keepdims=True)
        small_ref[SM_GATE_B:SM_GATE_B + 1, 0:D] += jnp.sum(dgl_b, axis=0, keepdims=True)

        p, p1, p2, conv = _conv_fwd(u_ref, uvc_prev, ucc_prev, cw_ref, is_first, tm)
        zc = _bf(u_ref, C_ZC, D)
        bc = _bf(u_ref, C_BC, D)
        sg = _sigmoid(zc)
        sc = zc * sg
        bconv = bc * conv
        dya_b = dya.astype(BF16)
        yc_ref[...] = (bconv * sc).astype(BF16)
        dya_ref[...] = dya_b
        dyc = _dot_nt(dya_b, wco_ref[...])
        dyc_s = dyc * sc
        du_ref[:, C_BC:C_BC + D] = (dyc_s * conv).astype(BF16)
        du_ref[:, C_ZC:C_ZC + D] = (dyc * bconv * (sg + sc * (1.0 - sg))).astype(BF16)
        dconv = dyc_s * bc
        small_ref[SM_CONV + 2:SM_CONV + 3, 0:D] += jnp.sum(dconv * p, axis=0, keepdims=True)
        small_ref[SM_CONV + 1:SM_CONV + 2, 0:D] += jnp.sum(dconv * p1, axis=0, keepdims=True)
        small_ref[SM_CONV:SM_CONV + 1, 0:D] += jnp.sum(dconv * p2, axis=0, keepdims=True)
        nxt = carry_conv[...]
        d1 = _shift_up(dconv, 1, nxt)
        d2 = _shift_up(dconv, 2, nxt)
        carry_conv[...] = dconv[0:8, :]
        cw = cw_ref[...]
        dp = cw[2:3, :] * dconv + cw[1:2, :] * d1 + cw[0:1, :] * d2
        du_ref[:, C_CC:C_CC + D] = (dp * _bf(u_ref, C_VC, D)).astype(BF16)
        du_ref[:, C_VC:C_VC + D] = (dp * _bf(u_ref, C_CC, D)).astype(BF16)

        dyb_b = dyb.astype(BF16)
        dob = _dot_nt(dyb_b, wao_ref[...])
        za = _bf(u_ref, C_ZA, D)
        sga = _sigmoid(za)
        sa = za * sga
        do = dob * sa
        qhat, rq, qn, khat, rk, kt, vt = _attn_inputs(u_ref, ukv_prev, qg_ref, kg_ref, cref, tm)
        bias, bias_first, lane_grp, query_grp = _attn_masks(is_first, cref)
        lane16 = lax.broadcasted_iota(jnp.int32, (1, D), 1)
        dsink_row = jnp.zeros((1, D), F32)
        o_cols, dq_cols, dk4, dv4 = [], [], [], []
        for h in range(N_KV):
            sink = _sink_row(sinks_ref, h, query_grp)
            dsink = jnp.zeros((1, GROUP * BLK), F32)
            o_rows, dq_rows, dk_parts, dv_parts = [], [], [], []
            for b in range(nb):
                rows = slice(b * BLK, (b + 1) * BLK)
                band = slice(b * BLK, (b + 2) * BLK)
                cols = slice(h * KVW, (h + 1) * KVW)
                qs = _stack_groups(qn[rows, cols], lane_grp)
                pn_t, ps = _softmax_block(qs, kt[h][band], bias_first if b == 0 else bias, sink)
                pn_b = pn_t.astype(BF16)
                o_rows.append(_unstack_groups(_dot_tn(pn_b, vt[h][band]), lane_grp))
                dos = _stack_groups(do[rows, cols].astype(BF16), lane_grp)
                dpn_t = _dot_nt(vt[h][band], dos)
                delta = jnp.sum(pn_t * dpn_t, axis=0, keepdims=True)
                ds_t = (pn_t * (dpn_t - delta)).astype(BF16)
                dsink = dsink - ps * delta
                dq_rows.append(_unstack_groups(_dot_tn(ds_t, kt[h][band]), lane_grp))
                dk_parts.append(_dot(ds_t, qs))
                dv_parts.append(_dot(pn_b, dos))
            o_cols.append(jnp.concatenate(o_rows, axis=0))
            dq_cols.append(jnp.concatenate(dq_rows, axis=0))
            dk4.append(_band_sum(dk_parts))
            dv4.append(_band_sum(dv_parts))
            for gi in range(GROUP):
                tot = jnp.sum(dsink[:, gi * BLK:(gi + 1) * BLK], axis=1, keepdims=True)
                dsink_row = dsink_row + jnp.where(lane16 == GROUP * h + gi, tot, 0.0)
        small_ref[SM_SINK:SM_SINK + 1, :] += dsink_row

        o = jnp.concatenate(o_cols, axis=1)
        ob_ref[...] = (o * sa).astype(BF16)
        dyb_ref[...] = dyb_b
        du_ref[:, C_ZA:C_ZA + D] = (dob * o * (sga + sa * (1.0 - sga))).astype(BF16)

        dqn = jnp.concatenate(dq_cols, axis=1)
        small_ref[SM_QG:SM_QG + 1, 0:D] += SCALE * jnp.sum(dqn * qhat, axis=0, keepdims=True)
        dqh = dqn * qg_ref[...]
        dq = rq * (dqh - qhat * _head_mean(dqh * qhat, cref["sel_q"][...], cref["exp_q"][...]))
        du_ref[:, C_Q:C_Q + D] = dq.astype(BF16)

        dkn_band = jnp.zeros((kb, KVW), F32)
        dv_band = jnp.zeros((kb, KVW), F32)
        for h in range(N_KV):
            dkn_band = dkn_band + _dot(dk4[h].astype(BF16), cref["rep_t"][h])
            dv_band = dv_band + _dot(dv4[h].astype(BF16), cref["rep_t"][h])
        carried = carry_kv[...]
        pad = jnp.zeros((tm - BLK, KVW), F32)
        if nb > 1:
            dkn = dkn_band[BLK:, :] + jnp.concatenate([pad, carried[:, 0:KVW]], axis=0)
            dv = dv_band[BLK:, :] + jnp.concatenate([pad, carried[:, KVW:2 * KVW]], axis=0)
        else:
            dkn = dkn_band[BLK:, :] + carried[:, 0:KVW]
            dv = dv_band[BLK:, :] + carried[:, KVW:2 * KVW]
        carry_kv[:, 0:KVW] = dkn_band[0:BLK, :]
        carry_kv[:, KVW:2 * KVW] = dv_band[0:BLK, :]
        khat_t = khat[BLK:, :]
        small_ref[SM_KG:SM_KG + 1, 0:KVW] += jnp.sum(dkn * khat_t, axis=0, keepdims=True)
        dkh = dkn * kg_ref[...]
        dk = rk[BLK:, :] * (dkh - khat_t * _head_mean(dkh * khat_t, cref["sel_k"][...], cref["exp_k"][...]))
        du_ref[:, C_K:C_K + KVW] = dk.astype(BF16)
        du_ref[:, C_V:C_V + KVW] = dv.astype(BF16)

    rev = lambda g: n_tiles - 1 - g
    u_spec, ukv_prev, uvc_prev, ucc_prev = _mixer_specs(t, tm, n_tiles, rev)
    tok = pl.BlockSpec((tm, D), lambda g: (rev(g), 0))
    consts = [c[k] for k in cn]
    wspec = _full((D, D))
    return pl.pallas_call(
        body, name=name, grid=(n_tiles,),
        in_specs=[tok, u_spec, ukv_prev, uvc_prev, ucc_prev, tok, tok, _full((8, D)), _full((1, D)), _full((1, KVW)),
                  pl.BlockSpec(memory_space=pltpu.SMEM), _full((1, 2 * D)), wspec, wspec, wspec]
                 + [_full(a.shape) for a in consts],
        out_specs=[pl.BlockSpec((tm, IN_COLS), lambda g: (rev(g), 0)), _full((SM_ROWS, D))] + [tok] * 5,
        out_shape=[jax.ShapeDtypeStruct((t, IN_COLS), BF16), jax.ShapeDtypeStruct((SM_ROWS, D), F32)]
                  + [jax.ShapeDtypeStruct((t, D), BF16)] * 5,
        scratch_shapes=[pltpu.VMEM((BLK, 2 * KVW), F32), pltpu.VMEM((8, D), F32)],
        compiler_params=_cparams(("arbitrary",)),
    )(dout, u, u, u, u, ya, yb, cw, qg, kg, sinks, gb, wco, wao, wout, *consts)


def matmul_tn(a, b, name, after=None):
    t, m = a.shape
    tk = min(1024, t)
    mb = 2176 if m == IN_COLS else m
    nk = t // tk

    def body(a_ref, b_ref, *rest):
        o_ref, acc = rest[-2:]
        k = pl.program_id(1)
        prod = _dot_tn(a_ref[...].astype(BF16), b_ref[...].astype(BF16))

        @pl.when(k == 0)
        def _():
            acc[...] = prod

        @pl.when(k > 0)
        def _():
            acc[...] += prod

        @pl.when(k == nk - 1)
        def _():
            o_ref[...] = acc[...].astype(BF16)

    return pl.pallas_call(
        body, name=name, grid=(m // mb, nk),
        in_specs=[pl.BlockSpec((tk, mb), lambda j, k: (k, j)), pl.BlockSpec((tk, D), lambda j, k: (k, 0))]
                 + ([] if after is None else [ANY]),
        out_specs=pl.BlockSpec((mb, D), lambda j, k: (j, 0)),
        out_shape=jax.ShapeDtypeStruct((m, D), BF16),
        scratch_shapes=[pltpu.VMEM((mb, D), F32)],
        compiler_params=_cparams(("arbitrary", "arbitrary")),
    )(a, b, *([] if after is None else [after]))


def inproj_bwd_x(du, w, x, ng, dout, name):
    t = x.shape[0]
    tm = min(1024, t)
    kc = 2176
    nk = IN_COLS // kc

    def body(du_ref, w_ref, x_ref, ng_ref, dout_ref, dx_ref, dng_ref, acc):
        i = pl.program_id(0)
        k = pl.program_id(1)
        prod = _dot(du_ref[...], w_ref[...])

        @pl.when(k == 0)
        def _():
            acc[...] = prod

        @pl.when(k > 0)
        def _():
            acc[...] += prod

        @pl.when((i == 0) & (k == 0))
        def _():
            dng_ref[...] = jnp.zeros_like(dng_ref)

        @pl.when(k == nk - 1)
        def _():
            dh = acc[...]
            xf = x_ref[...]
            r = lax.rsqrt(jnp.mean(xf * xf, axis=-1, keepdims=True) + EPS)
            xhat = xf * r
            dng_ref[0:1, :] += jnp.sum(dh * xhat, axis=0, keepdims=True)
            dxh = dh * ng_ref[...]
            dx_ref[...] = dout_ref[...] + r * (dxh - xhat * jnp.mean(dxh * xhat, axis=-1, keepdims=True))

    tok = pl.BlockSpec((tm, D), lambda i, k: (i, 0))
    return pl.pallas_call(
        body, name=name, grid=(t // tm, nk),
        in_specs=[pl.BlockSpec((tm, kc), lambda i, k: (i, k)), pl.BlockSpec((kc, D), lambda i, k: (k, 0)), tok,
                  pl.BlockSpec((1, D), lambda i, k: (0, 0)), tok],
        out_specs=[tok, pl.BlockSpec((8, D), lambda i, k: (0, 0))],
        out_shape=[jax.ShapeDtypeStruct((t, D), F32), jax.ShapeDtypeStruct((8, D), F32)],
        scratch_shapes=[pltpu.VMEM((tm, D), F32)],
        compiler_params=_cparams(("arbitrary", "arbitrary")),
    )(du, w, x, ng, dout)


def loss_head(y, target, name):
    t = y.shape[0]
    tm = min(1024, t)

    def body(y_ref, t_ref, dy_ref, loss_ref):
        @pl.when(pl.program_id(0) == 0)
        def _():
            loss_ref[...] = jnp.zeros_like(loss_ref)
        err = y_ref[...] - t_ref[...]
        dy_ref[...] = err * (1.0 / D)
        part = jnp.sum(jnp.sum(err * err, axis=-1, keepdims=True) * (1.0 / D), axis=0, keepdims=True)
        loss_ref[...] += 0.5 * part

    tok = pl.BlockSpec((tm, D), lambda i: (i, 0))
    return pl.pallas_call(
        body, name=name, grid=(t // tm,), in_specs=[tok, tok],
        out_specs=[tok, pl.BlockSpec((8, 128), lambda i: (0, 0))],
        out_shape=[jax.ShapeDtypeStruct((t, D), F32), jax.ShapeDtypeStruct((8, 128), F32)],
        compiler_params=_cparams(("arbitrary",)),
    )(y, target)


def layer_operands(l, norm_g, conv_w_full, q_norm_g, k_norm_g, sinks, gate_b, w_in_b, wco_b, wao_b, wout_b):
    return dict(
        ng=norm_g[l][None, :], cw=jnp.pad(conv_w_full[l], ((0, 5), (0, 0))),
        qg=jnp.tile(q_norm_g[l] * SCALE, D // HEAD)[None, :], kg=jnp.tile(k_norm_g[l], N_KV)[None, :],
        sinks=sinks[l][None, :], gb=gate_b[l][None, :],
        w_in=w_in_b[l], wco=wco_b[l], wao=wao_b[l], wout=wout_b[l])


def layer_bwd(dout, saved, lw, c, l, send_off):
    x, u, h, ya, yb = saved
    du, small, merged, yc, ob, dya, dyb = mixer_bwd(dout, u, ya, yb, lw["cw"], lw["qg"], lw["kg"], lw["sinks"], lw["gb"],
                                                    lw["wco"], lw["wao"], lw["wout"], c, f"mixer_bwd_{l}")
    grads = dict(w_in=matmul_tn(du, h, f"dw_in_{l}"), small=small)
    token = send_off(grads, False)
    grads["wout"] = matmul_tn(merged, dout, f"dw_out_{l}", after=token)
    grads["wco"] = matmul_tn(yc, dya, f"dw_conv_out_{l}")
    grads["wao"] = matmul_tn(ob, dyb, f"dw_attn_out_{l}")
    token = send_off(grads, True)
    dx, grads["dng"] = inproj_bwd_x(du, lw["w_in"], x, lw["ng"] + token[0:1, 0:1], dout, f"inproj_bwd_{l}")
    return dx, grads


MESH = pl.DeviceIdType.MESH
ANY = pl.BlockSpec(memory_space=pl.ANY)


def _place():
    return lax.axis_index("x"), lax.axis_index("y"), lax.axis_index("c")


def all_gather(arrs, after, name):
    n = len(arrs)

    def body(*refs):
        ins, outs = refs[:n], refs[n + 1:2 * n + 1]
        send_sems, recv_sems, local_sems = refs[2 * n + 1:]
        x, y, c = _place()
        me, sibling = (x, y, c), (x, y, 1 - c)
        chips = [(1 - x, y), (x, 1 - y), (1 - x, 1 - y)]

        def slot(a, block):
            px, py, pc = block
            return outs[a].at[4 * px + 2 * py + pc]

        def copy(a, k, block, to, src=None):
            return pltpu.make_async_remote_copy(
                src_ref=slot(a, block) if src is None else src, dst_ref=slot(a, block),
                send_sem=send_sems.at[a, k], recv_sem=recv_sems.at[a, k], device_id=to, device_id_type=MESH)

        mine = [pltpu.make_async_copy(ins[a], slot(a, me), local_sems.at[a]) for a in range(n)]
        for cp in mine:
            cp.start()
        first = []
        for a in range(n):
            first.append(copy(a, 0, me, sibling, src=ins[a]))
            first += [copy(a, 1 + j, me, (*chip, c), src=ins[a]) for j, chip in enumerate(chips)]
        for cp in first:
            cp.start()
        passed = []
        for j, chip in enumerate(chips):
            for a in range(n):
                copy(a, 1 + j, (*chip, c), me).wait_recv()
                passed.append(copy(a, 4 + j, (*chip, c), sibling))
                passed[-1].start()
        for a in range(n):
            copy(a, 0, sibling, me).wait_recv()
            for j, chip in enumerate(chips):
                copy(a, 4 + j, (*chip, 1 - c), me).wait_recv()
        for cp in first + passed:
            cp.wait_send()
        for cp in mine:
            cp.wait()

    return pl.pallas_call(
        body, name=name, in_specs=[ANY] * (n + 1), out_specs=[ANY] * n,
        out_shape=[jax.ShapeDtypeStruct((N_DEV,) + a.shape, a.dtype) for a in arrs],
        scratch_shapes=[pltpu.SemaphoreType.DMA((n, 7)), pltpu.SemaphoreType.DMA((n, 7)), pltpu.SemaphoreType.DMA((n,))],
    )(*arrs, after)


HBM_SPEC = pl.BlockSpec(memory_space=pltpu.HBM)
SEM_SPEC = pl.BlockSpec(memory_space=pltpu.SEMAPHORE)
EFFECT = pltpu.SideEffectType.DATAFLOW_SIDE_EFFECTING


ALL_PEERS = (1, 2, 3, 4, 5, 6, 7)


def _flip(place, k):
    x, y, c = place
    return (1 - x if k & 4 else x, 1 - y if k & 2 else y, 1 - c if k & 1 else c)


def _slot(place):
    return 4 * place[0] + 2 * place[1] + place[2]


def _exchange_copies(ins, lands, send_sems, recv_sems, scatter, flips, arriving):
    me = _place()
    out = []
    for a in range(len(ins)):
        for i, k in enumerate(flips):
            peer = _flip(me, k)
            out.append(pltpu.make_async_remote_copy(
                src_ref=ins[a].at[_slot(peer)] if scatter else ins[a],
                dst_ref=lands[a].at[_slot(peer) if arriving else _slot(me)],
                send_sem=send_sems.at[a * len(flips) + i], recv_sem=recv_sems.at[a * len(flips) + i],
                device_id=peer, device_id_type=MESH))
    return out


def exchange_start(arrs, scatter, flips, after, name):
    n = len(arrs)
    lands = [lax.empty(a.shape if scatter else (N_DEV,) + a.shape, a.dtype) for a in arrs]

    def body(*refs):
        ins, lz = refs[:n], refs[n:2 * n]
        send_sems, recv_sems = refs[2 * n + 1], refs[2 * n + 2]
        token = refs[-1]
        for cp in _exchange_copies(ins, lz, send_sems, recv_sems, scatter, flips, False):
            cp.start()
        token[...] = jnp.zeros_like(token)

    sems = pltpu.SemaphoreType.DMA((n * len(flips),))
    res = pl.pallas_call(
        body, name=name,
        out_shape=(sems, sems, *[pltpu.HBM(a.shape, a.dtype) for a in arrs], *[pltpu.HBM(a.shape, a.dtype) for a in lands],
                   jax.ShapeDtypeStruct((8, 128), F32)),
        in_specs=[HBM_SPEC] * (2 * n) + [ANY],
        out_specs=(SEM_SPEC, SEM_SPEC, *[HBM_SPEC] * (2 * n), pl.BlockSpec(memory_space=pltpu.VMEM)),
        input_output_aliases={i: 2 + i for i in range(2 * n)},
        compiler_params=pltpu.CompilerParams(has_side_effects=EFFECT),
    )(*[pltpu.with_memory_space_constraint(a, pltpu.HBM) for a in arrs],
      *[pltpu.with_memory_space_constraint(a, pltpu.HBM) for a in lands], after)
    return dict(send=res[0], recv=res[1], srcs=res[2:2 + n], lands=res[2 + n:2 + 2 * n], token=res[-1], scatter=scatter,
                flips=flips)


def exchange_wait(state, after, name):
    n = len(state["srcs"])

    def body(*refs):
        ins, lz = refs[:n], refs[n:2 * n]
        send_sems, recv_sems = refs[2 * n], refs[2 * n + 1]
        for cp in _exchange_copies(ins, lz, send_sems, recv_sems, state["scatter"], state["flips"], True):
            cp.wait_send()
            cp.wait_recv()

    both = list(state["srcs"]) + list(state["lands"])
    res = pl.pallas_call(
        body, name=name, out_shape=tuple(pltpu.HBM(a.shape, a.dtype) for a in both),
        in_specs=[HBM_SPEC] * (2 * n) + [SEM_SPEC, SEM_SPEC, ANY], out_specs=tuple([HBM_SPEC] * (2 * n)),
        input_output_aliases={i: i for i in range(2 * n)},
        compiler_params=pltpu.CompilerParams(has_side_effects=EFFECT),
    )(*both, state["send"], state["recv"], after)
    return res[:n], res[n:]


OTHER_CHIPS = (2, 4, 6)
SAME_CORE = (1, 2, 4, 6)


def _forward_copies(lands, send_sems, recv_sems, arriving):
    me = _place()
    sibling = _flip(me, 1)
    origin = sibling if arriving else me
    return [pltpu.make_async_remote_copy(
        src_ref=lands[a].at[_slot(_flip(origin, k))], dst_ref=lands[a].at[_slot(_flip(origin, k))],
        send_sem=send_sems.at[a * len(OTHER_CHIPS) + i], recv_sem=recv_sems.at[a * len(OTHER_CHIPS) + i],
        device_id=sibling, device_id_type=MESH) for a in range(len(lands)) for i, k in enumerate(OTHER_CHIPS)]


def forward_start(lands, after, name):
    n = len(lands)

    def body(*refs):
        lz = refs[:n]
        send_sems, recv_sems = refs[n + 1], refs[n + 2]
        token = refs[-1]
        for cp in _forward_copies(lz, send_sems, recv_sems, False):
            cp.start()
        token[...] = jnp.zeros_like(token)

    sems = pltpu.SemaphoreType.DMA((n * len(OTHER_CHIPS),))
    res = pl.pallas_call(
        body, name=name,
        out_shape=(sems, sems, *[pltpu.HBM(a.shape, a.dtype) for a in lands], jax.ShapeDtypeStruct((8, 128), F32)),
        in_specs=[HBM_SPEC] * n + [ANY],
        out_specs=(SEM_SPEC, SEM_SPEC, *[HBM_SPEC] * n, pl.BlockSpec(memory_space=pltpu.VMEM)),
        input_output_aliases={i: 2 + i for i in range(n)},
        compiler_params=pltpu.CompilerParams(has_side_effects=EFFECT),
    )(*[pltpu.with_memory_space_constraint(a, pltpu.HBM) for a in lands], after)
    return dict(send=res[0], recv=res[1], lands=res[2:2 + n], token=res[-1])


def forward_wait(state, after, name):
    n = len(state["lands"])

    def body(*refs):
        lz = refs[:n]
        send_sems, recv_sems = refs[n], refs[n + 1]
        for cp in _forward_copies(lz, send_sems, recv_sems, False):
            cp.wait_send()
        for cp in _forward_copies(lz, send_sems, recv_sems, True):
            cp.wait_recv()

    return pl.pallas_call(
        body, name=name, out_shape=tuple(pltpu.HBM(a.shape, a.dtype) for a in state["lands"]),
        in_specs=[HBM_SPEC] * n + [SEM_SPEC, SEM_SPEC, ANY], out_specs=tuple([HBM_SPEC] * n),
        input_output_aliases={i: i for i in range(n)},
        compiler_params=pltpu.CompilerParams(has_side_effects=EFFECT),
    )(*state["lands"], state["send"], state["recv"], after)


def adamw(w, m, v, parts, name):
    r, cdim = w.shape
    n_parts = parts.shape[0]
    rb = 256 if r % 256 == 0 else (SHARD_COLS // 4 if r == SHARD_COLS else r)

    def body(w_ref, m_ref, v_ref, p_ref, g_ref, d_ref, mo_ref, vo_ref):
        g = p_ref[0].astype(F32)
        for i in range(1, n_parts):
            g = g + p_ref[i].astype(F32)
        m_new = ADAM_B1 * m_ref[...] + (1.0 - ADAM_B1) * g
        v_new = ADAM_B2 * v_ref[...] + (1.0 - ADAM_B2) * (g * g)
        m_hat = m_new / (1.0 - ADAM_B1 ** ADAM_STEP)
        v_hat = v_new / (1.0 - ADAM_B2 ** ADAM_STEP)
        g_ref[...] = g
        d_ref[...] = -ADAM_LR * (m_hat / (jnp.sqrt(v_hat) + ADAM_EPS) + ADAM_WD * w_ref[...])
        mo_ref[...] = m_new
        vo_ref[...] = v_new

    blk = pl.BlockSpec((rb, cdim), lambda i: (i, 0))
    return pl.pallas_call(
        body, name=name, grid=(r // rb,),
        in_specs=[blk, blk, blk, pl.BlockSpec((n_parts, rb, cdim), lambda i: (0, i, 0))],
        out_specs=[blk] * 4, out_shape=[jax.ShapeDtypeStruct((r, cdim), F32)] * 4,
        compiler_params=_cparams(("arbitrary",)),
    )(w, m, v, parts)


def adamw_layers(w, m, v, lands, srcs, me, lo, prev, name):
    nl_all, r, cdim = w.shape
    nl = len(lands)
    rb = 256 if r % 256 == 0 else (SHARD_COLS // 4 if r == SHARD_COLS else r)
    nblk = r // rb
    n_prev = 0 if prev is None else 4

    def body(me_ref, w_ref, m_ref, v_ref, *rest):
        land_refs, src_refs = rest[:nl], rest[nl:2 * nl]
        g_ref, d_ref, mo_ref, vo_ref = rest[2 * nl + n_prev:]
        for k in range(nl):
            @pl.when(pl.program_id(0) == k)
            def _(k=k):
                own = src_refs[k][...].astype(F32)
                g = jnp.where(me_ref[0] == 0, own, land_refs[k][0].astype(F32))
                for i in range(1, N_DEV):
                    g = g + jnp.where(me_ref[0] == i, own, land_refs[k][i].astype(F32))
                m_new = ADAM_B1 * m_ref[...] + (1.0 - ADAM_B1) * g
                v_new = ADAM_B2 * v_ref[...] + (1.0 - ADAM_B2) * (g * g)
                m_hat = m_new / (1.0 - ADAM_B1 ** ADAM_STEP)
                v_hat = v_new / (1.0 - ADAM_B2 ** ADAM_STEP)
                g_ref[...] = g
                d_ref[...] = -ADAM_LR * (m_hat / (jnp.sqrt(v_hat) + ADAM_EPS) + ADAM_WD * w_ref[...])
                mo_ref[...] = m_new
                vo_ref[...] = v_new

    blk = pl.BlockSpec((None, rb, cdim), lambda l, i, me_ref: (lo + l, i, 0))

    def rows(l, i, k):
        return jnp.where(l < k, 0, jnp.where(l == k, i, nblk - 1))

    land_specs = [pl.BlockSpec((N_DEV, rb, cdim), lambda l, i, me_ref, k=k: (0, rows(l, i, k), 0)) for k in range(nl)]
    src_specs = [pl.BlockSpec((None, rb, cdim), lambda l, i, me_ref, k=k: (me_ref[0], rows(l, i, k), 0)) for k in range(nl)]
    return pl.pallas_call(
        body, name=name,
        grid_spec=pltpu.PrefetchScalarGridSpec(
            num_scalar_prefetch=1, grid=(nl, nblk),
            in_specs=[blk, blk, blk] + land_specs + src_specs + [ANY] * n_prev, out_specs=[blk] * 4),
        out_shape=[jax.ShapeDtypeStruct((nl_all, r, cdim), F32)] * 4,
        input_output_aliases={4 + 2 * nl + j: j for j in range(n_prev)},
        compiler_params=_cparams(("arbitrary", "arbitrary")),
    )(me, w, m, v, *lands, *srcs, *([] if prev is None else prev))


def small_sum(parts, fold, name):
    rows = parts.shape[1]

    def dot3(xv, sel):
        out = jnp.zeros((xv.shape[0], sel.shape[1]), F32)
        for _ in range(3):
            hi = xv.astype(BF16)
            out = out + _dot(hi, sel)
            xv = xv - hi.astype(F32)
        return out

    def body(p_ref, fold_ref, o_ref):
        tot = p_ref[0]
        for i in range(1, N_DEV):
            tot = tot + p_ref[i]
        o_ref[...] = tot
        for l in range(rows // SM_ROWS):
            blk = tot[l * SM_ROWS:l * SM_ROWS + 8, 0:D]
            folded = dot3(blk, fold_ref[...])
            o_ref[l * SM_ROWS + SM_QG_FOLDED:l * SM_ROWS + SM_QG_FOLDED + 1, 0:128] = folded[SM_QG:SM_QG + 1, :]
            o_ref[l * SM_ROWS + SM_KG_FOLDED:l * SM_ROWS + SM_KG_FOLDED + 1, 0:128] = folded[SM_KG:SM_KG + 1, :]

    return pl.pallas_call(
        body, name=name, out_shape=jax.ShapeDtypeStruct((rows, D), F32),
        compiler_params=_cparams(None),
    )(parts, fold)


def kernel(x, norm_g, w_in, conv_w, q_norm_g, k_norm_g, sinks, w_conv_out, w_attn_out, gate_b, w_out, loss_target, m_norm_g, m_w_in, m_conv_w, m_q_norm_g, m_k_norm_g, m_sinks, m_w_conv_out, m_w_attn_out, m_gate_b, m_w_out, v_norm_g, v_w_in, v_conv_w, v_q_norm_g, v_k_norm_g, v_sinks, v_w_conv_out, v_w_attn_out, v_gate_b, v_w_out):
    c = _selectors()
    me = 4 * lax.axis_index("x") + 2 * lax.axis_index("y") + lax.axis_index("c")

    w_in_t, m_w_in_t, v_w_in_t = (jnp.swapaxes(a, 1, 2) for a in (w_in, m_w_in, v_w_in))

    def shards(l):
        return [w_in_t[l].astype(BF16), w_conv_out[l].astype(BF16), w_attn_out[l].astype(BF16), w_out[l].astype(BF16)]

    def ici_start(l, after):
        return exchange_start(shards(l), False, SAME_CORE, after, f"gather_start_{l}")

    h = x[0]
    saved, lws = [], []
    lands = all_gather(shards(0) + [conv_w], h, "gather_0")
    conv_full = jnp.transpose(lands[4], (1, 2, 0, 3)).reshape(DEPTH, 3, D)
    ici = ici_start(1, lands[1])
    for l in range(DEPTH):
        lws.append(layer_operands(l, norm_g, conv_full, q_norm_g, k_norm_g, sinks, gate_b,
                                  {l: lands[0].reshape(IN_COLS, D)}, {l: lands[1].reshape(D, D)},
                                  {l: lands[2].reshape(D, D)}, {l: lands[3].reshape(D, D)}))
        ng_l = lws[l]["ng"] + ici["token"][0:1, 0:1] if l == 0 else lws[l]["ng"]
        u, hb = inproj_fwd(h, ng_l, lws[l]["w_in"], f"inproj_fwd_{l}")
        gb_l = lws[l]["gb"]
        if l + 1 < DEPTH:
            mine, arrived = exchange_wait(ici, u, f"gather_wait_{l + 1}")
            chip = forward_start(arrived, mine[0], f"gather_forward_start_{l + 1}")
            started = chip["token"]
            if l + 2 < DEPTH:
                ici = ici_start(l + 2, started)
                started = ici["token"]
            gb_l = gb_l + started[0:1, 0:1]
        x_in = h
        h, ya, yb = mixer_fwd(x_in, u, lws[l]["cw"], lws[l]["qg"], lws[l]["kg"], lws[l]["sinks"], gb_l,
                              lws[l]["wco"], lws[l]["wao"], lws[l]["wout"], c, f"mixer_fwd_{l}")
        saved.append((x_in, u, hb, ya, yb))
        if l + 1 < DEPTH:
            lands = forward_wait(chip, h, f"gather_forward_wait_{l + 1}")
            lands = [lax.dynamic_update_index_in_dim(land, src, me, 0) for land, src in zip(lands, mine)]
    dh, loss_part = loss_head(h, loss_target[0], "loss_head")

    grads, scatters = [None] * DEPTH, [[] for _ in range(DEPTH)]
    for l in reversed(range(DEPTH)):
        def send_off(g, done, l=l):
            rest = [g[k].reshape(N_DEV, SHARD_ROWS, D) for k in ("wco", "wao", "wout")] if done else []
            first = [g["w_in"].reshape(N_DEV, SHARD_COLS, D)] if done == (l > 0) else []
            if not first + rest:
                return None
            tag = f"{l}" if l > 0 else ("0_rest" if done else "0_in")
            scatters[l].append(exchange_start(first + rest, True, ALL_PEERS, g["small"], f"scatter_start_{tag}"))
            return scatters[l][-1]["token"]
        dh, grads[l] = layer_bwd(dh, saved[l], lws[l], c, l, send_off)

    me1 = me.astype(jnp.int32).reshape(1)
    mine, lands = {}, {}
    for l in (3, 2, 1):
        mine[l], lands[l] = exchange_wait(scatters[l][0], dh, f"scatter_wait_{l}")
    weights = [(w_in_t, m_w_in_t, v_w_in_t, "w_in"), (w_conv_out, m_w_conv_out, v_w_conv_out, "w_conv_out"),
               (w_attn_out, m_w_attn_out, v_w_attn_out, "w_attn_out"), (w_out, m_w_out, v_w_out, "w_out")]
    upd = [adamw_layers(w, m, v, [lands[l][i] for l in (1, 2, 3)], [mine[l][i] for l in (1, 2, 3)], me1, 1, None,
                        f"adamw_{n}_upper") for i, (w, m, v, n) in enumerate(weights)]
    blocks = []
    for l in range(DEPTH):
        blk = grads[l]["small"]
        blk = blk.at[SM_NORM, 0:D].set(grads[l]["dng"][0])
        if l == 0:
            blk = blk.at[SM_LOSS, 0:128].set(loss_part[0])
        blocks.append(blk)
    small_x = exchange_start([jnp.concatenate(blocks, axis=0)], False, ALL_PEERS, grads[0]["dng"], "gather_small_start")

    m_in, l_in = exchange_wait(scatters[0][0], upd[0][0], "scatter_wait_0_in")
    upd[0] = adamw_layers(*weights[0][:3], [l_in[0]], [m_in[0]], me1, 0, upd[0], "adamw_w_in_0")
    m_rest, l_rest = exchange_wait(scatters[0][1], upd[0][0], "scatter_wait_0_rest")
    for i in (1, 2, 3):
        upd[i] = adamw_layers(*weights[i][:3], [l_rest[i - 1]], [m_rest[i - 1]], me1, 0, upd[i], f"adamw_{weights[i][3]}_0")
    u_in = [jnp.swapaxes(o, 1, 2) for o in upd[0]]
    u_co, u_ao, u_out = upd[1], upd[2], upd[3]

    mine_s, lands_s = exchange_wait(small_x, u_out[0], "gather_small_wait")
    gathered = lax.dynamic_update_index_in_dim(lands_s[0], mine_s[0], me, 0)
    tot = small_sum(gathered, c["fold"], "small_sum")
    tot = tot.reshape(DEPTH, SM_ROWS, D)
    loss = tot[0, SM_LOSS, 0]

    def update_small(w, m, v, g, name):
        return adamw(w, m, v, g[None], name)

    u_ng = update_small(norm_g, m_norm_g, v_norm_g, tot[:, SM_NORM, 0:D], "adamw_norm_g")
    u_qg = update_small(q_norm_g, m_q_norm_g, v_q_norm_g, tot[:, SM_QG_FOLDED, 0:HEAD], "adamw_q_norm_g")
    u_kg = update_small(k_norm_g, m_k_norm_g, v_k_norm_g, tot[:, SM_KG_FOLDED, 0:HEAD], "adamw_k_norm_g")
    u_sk = update_small(sinks, m_sinks, v_sinks, tot[:, SM_SINK, 0:16], "adamw_sinks")
    g_gate = jnp.concatenate([tot[:, SM_GATE, :], tot[:, SM_GATE_B, :]], axis=1)
    u_gb = update_small(gate_b, m_gate_b, v_gate_b, g_gate, "adamw_gate_b")
    g_conv = lax.dynamic_slice_in_dim(tot[:, SM_CONV:SM_CONV + 3, 0:D], me * SHARD_ROWS, SHARD_ROWS, axis=2)
    u_cw = [o.reshape(DEPTH, 3, SHARD_ROWS) for o in update_small(
        conv_w.reshape(DEPTH * 3, SHARD_ROWS), m_conv_w.reshape(DEPTH * 3, SHARD_ROWS),
        v_conv_w.reshape(DEPTH * 3, SHARD_ROWS), g_conv.reshape(DEPTH * 3, SHARD_ROWS), "adamw_conv_w")]

    order = [u_ng, u_in, u_cw, u_qg, u_kg, u_sk, u_co, u_ao, u_gb, u_out]
    return (loss, dh[None], *[u[0] for u in order], *[u[1] for u in order], *[u[2] for u in order], *[u[3] for u in order])
```

```python
import functools

import jax
import jax.numpy as jnp
from jax import lax
from jax.experimental import pallas as pl
from jax.experimental.pallas import tpu as pltpu

F32 = jnp.float32
BF16 = jnp.bfloat16

N_DEV = 8
DEPTH = 4
D = 1024
N_KV = 4
GROUP = 4
HEAD = 64
BLK = 128
KVW = N_KV * HEAD
IN_COLS = 8704
SHARD_COLS = IN_COLS // N_DEV
SHARD_ROWS = D // N_DEV
C_VC, C_BC, C_CC, C_ZC, C_Q, C_K, C_V, C_ZA, C_GA, C_GB = 0, 1024, 2048, 3072, 4096, 5120, 5376, 5632, 6656, 7680
EPS = 1e-6
NEG_INF = -1e30
SCALE = HEAD ** -0.5

ADAM_LR = 0.001
ADAM_B1 = 0.9
ADAM_B2 = 0.999
ADAM_EPS = 1e-08
ADAM_WD = 0.01
ADAM_STEP = 10

VMEM_LIMIT = 60 * 1024 * 1024
SM_ROWS = 16
SM_GATE, SM_CONV, SM_QG, SM_KG, SM_SINK, SM_NORM, SM_LOSS, SM_GATE_B, SM_QG_FOLDED, SM_KG_FOLDED = 0, 1, 4, 5, 6, 7, 8, 9, 10, 11


def _cparams(sem):
    return pltpu.CompilerParams(dimension_semantics=sem, vmem_limit_bytes=VMEM_LIMIT)


def _dot(a, b):
    return jnp.dot(a, b, preferred_element_type=F32)


def _dot_nt(a, b):
    return lax.dot_general(a, b, (((1,), (1,)), ((), ())), preferred_element_type=F32)


def _dot_tn(a, b):
    return lax.dot_general(a, b, (((0,), (0,)), ((), ())), preferred_element_type=F32)


def _dot2(x, sel):
    hi = x.astype(BF16)
    lo = (x - hi.astype(F32)).astype(BF16)
    return _dot(hi, sel) + _dot(lo, sel)


def _sigmoid(z):
    return 0.5 * jnp.tanh(0.5 * z) + 0.5


def _head_mean(t, sel, exp):
    return _dot2(_dot(t.astype(BF16), sel) * (1.0 / HEAD), exp)


def _shift_down(a, k, before):
    r = pltpu.roll(a, k, 0)
    row = lax.broadcasted_iota(jnp.int32, (8, 1), 0)
    head = jnp.where(row < k, pltpu.roll(before, k, 0), r[0:8, :])
    return jnp.concatenate([head, r[8:, :]], axis=0)


def _shift_up(a, k, after):
    n = a.shape[0]
    r = pltpu.roll(a, n - k, 0)
    row = lax.broadcasted_iota(jnp.int32, (8, 1), 0)
    tail = jnp.where(row >= 8 - k, pltpu.roll(after, 8 - k, 0), r[n - 8:, :])
    return jnp.concatenate([r[:n - 8, :], tail], axis=0)


def _bf(ref, c0, width):
    return ref[:, c0:c0 + width].astype(F32)


def _selectors():
    c = jnp.arange(D)
    sel_q = (c[:, None] // HEAD == jnp.arange(128)[None, :]).astype(BF16)
    ck = jnp.arange(KVW)
    sel_k = (ck[:, None] // HEAD == jnp.arange(128)[None, :]).astype(BF16)
    fold = (c[:, None] % HEAD == jnp.arange(128)[None, :]).astype(BF16)
    qq = jnp.arange(GROUP * BLK)[:, None] % BLK
    kk = jnp.arange(2 * BLK)[None, :]
    valid = (kk > qq) & (kk <= qq + BLK)
    bias = jnp.where(valid, 0.0, NEG_INF).astype(F32).T
    bias_first = jnp.where(valid & (kk >= BLK), 0.0, NEG_INF).astype(F32).T
    return dict(bias=bias, bias_first=bias_first,sel_q=sel_q, exp_q=sel_q.T, sel_k=sel_k, exp_k=sel_k.T, fold=fold)


def inproj_fwd(x, ng, w, name):
    t = x.shape[0]
    tm = min(1024, t)
    cb = 2176
    def body(x_ref, ng_ref, w_ref, u_ref, h_ref, h_scr):
        @pl.when(pl.program_id(1) == 0)
        def _():
            xf = x_ref[...]
            r = lax.rsqrt(jnp.mean(xf * xf, axis=-1, keepdims=True) + EPS)
            hb = (xf * r * ng_ref[...]).astype(BF16)
            h_scr[...] = hb
            h_ref[...] = hb
        u_ref[...] = _dot_nt(h_scr[...], w_ref[...]).astype(BF16)

    return pl.pallas_call(
        body, name=name, grid=(t // tm, IN_COLS // cb),
        in_specs=[pl.BlockSpec((tm, D), lambda i, j: (i, 0)), pl.BlockSpec((1, D), lambda i, j: (0, 0)),
                  pl.BlockSpec((cb, D), lambda i, j: (j, 0))],
        out_specs=[pl.BlockSpec((tm, cb), lambda i, j: (i, j)), pl.BlockSpec((tm, D), lambda i, j: (i, 0))],
        out_shape=[jax.ShapeDtypeStruct((t, IN_COLS), BF16), jax.ShapeDtypeStruct((t, D), BF16)],
        scratch_shapes=[pltpu.VMEM((tm, D), BF16)],
        compiler_params=_cparams(("arbitrary", "arbitrary")),
    )(x, ng, w)


def _conv_fwd(u_ref, uvc_prev, ucc_prev, cw_ref, is_first, tm):
    p = _bf(u_ref, C_CC, D) * _bf(u_ref, C_VC, D)
    pprev = ucc_prev[...].astype(F32) * uvc_prev[...].astype(F32)
    pprev = jnp.where(is_first, 0.0, pprev)
    p1 = _shift_down(p, 1, pprev[8:16, :])
    p2 = _shift_down(p, 2, pprev[8:16, :])
    cw = cw_ref[...]
    conv = cw[0:1, :] * p2 + cw[1:2, :] * p1 + cw[2:3, :] * p
    return p, p1, p2, conv


def _attn_inputs(u_ref, ukv_prev, qg_ref, kg_ref, c, tm):
    q = _bf(u_ref, C_Q, D)
    rq = lax.rsqrt(_head_mean(q * q, c["sel_q"][...], c["exp_q"][...]) + EPS)
    qhat = q * rq
    qn = (qhat * qg_ref[...]).astype(BF16)
    kband = jnp.concatenate([ukv_prev[:, 0:KVW].astype(F32), _bf(u_ref, C_K, KVW)], axis=0)
    rk = lax.rsqrt(_head_mean(kband * kband, c["sel_k"][...], c["exp_k"][...]) + EPS)
    khat = kband * rk
    knb = (khat * kg_ref[...]).astype(BF16)
    vband = jnp.concatenate([ukv_prev[:, KVW:2 * KVW], u_ref[:, C_V:C_V + KVW]], axis=0)
    return qhat, rq, qn, khat, rk, knb, vband


def _attn_masks(is_first, c):
    bias = c["bias"][...]
    bias_first = jnp.where(is_first, c["bias_first"][...], bias)
    lane_grp = lax.broadcasted_iota(jnp.int32, (BLK, KVW), 1) // HEAD
    query_grp = lax.broadcasted_iota(jnp.int32, (1, GROUP * BLK), 1) // BLK
    return bias, bias_first, lane_grp, query_grp


def _sink_row(sinks_ref, h, query_grp):
    row = jnp.full(query_grp.shape, sinks_ref[0, GROUP * h], F32)
    for gi in range(1, GROUP):
        row = jnp.where(query_grp == gi, sinks_ref[0, GROUP * h + gi], row)
    return row


def _lane_shift(a, groups):
    shift = (HEAD * groups) % KVW
    return a if shift == 0 else pltpu.roll(a, shift, 1)


def _stack_groups(a256, lane_grp, h):
    zero = jnp.zeros_like(a256)
    return jnp.concatenate([jnp.where(lane_grp == h, _lane_shift(a256, h - gi), zero) for gi in range(GROUP)], axis=0)


def _unstack_groups(a4, lane_grp, h):
    out = jnp.where(lane_grp == 0, _lane_shift(a4[0:BLK], 0 - h), 0.0)
    for gi in range(1, GROUP):
        out = out + jnp.where(lane_grp == gi, _lane_shift(a4[gi * BLK:(gi + 1) * BLK], gi - h), 0.0)
    return out


def _band_sum(parts):
    pieces = [parts[0][0:BLK, :]]
    for b in range(1, len(parts)):
        pieces.append(parts[b - 1][BLK:, :] + parts[b][0:BLK, :])
    pieces.append(parts[-1][BLK:, :])
    return jnp.concatenate(pieces, axis=0)


def _softmax_block(qs, kt_b, bias, sink):
    s = _dot_nt(kt_b, qs) + bias
    m = jnp.maximum(jnp.max(s, axis=0, keepdims=True), sink)
    e = jnp.exp(s - m)
    es = jnp.exp(sink - m)
    inv = 1.0 / (jnp.sum(e, axis=0, keepdims=True) + es)
    return e * inv, es * inv


def _mixer_specs(t, tm, n_tiles, tile_of):
    nb = tm // BLK
    u_spec = pl.BlockSpec((tm, IN_COLS), lambda g: (tile_of(g), 0))
    ukv_prev = pl.BlockSpec((BLK, 2 * KVW), lambda g: (jnp.maximum(tile_of(g) * nb - 1, 0), C_K // (2 * KVW)))
    uvc_prev = pl.BlockSpec((16, D), lambda g: (jnp.maximum(tile_of(g) * (tm // 16) - 1, 0), C_VC // D))
    ucc_prev = pl.BlockSpec((16, D), lambda g: (jnp.maximum(tile_of(g) * (tm // 16) - 1, 0), C_CC // D))
    return u_spec, ukv_prev, uvc_prev, ucc_prev


def _full(shape):
    n = len(shape)
    return pl.BlockSpec(shape, lambda g: (0,) * n)


def mixer_fwd(x, u, cw, qg, kg, sinks, gb, wco, wao, wout, c, name):
    t = x.shape[0]
    tm = min(256, t)
    n_tiles = t // tm
    nb = tm // BLK
    cn = sorted(c)

    def body(x_ref, u_ref, ukv_prev, uvc_prev, ucc_prev, cw_ref, qg_ref, kg_ref, sinks_ref, gb_ref, wco_ref, wao_ref,
             wout_ref, *rest):
        cref = dict(zip(cn, rest[:len(cn)]))
        xo_ref, ya_ref, yb_ref = rest[len(cn):]
        is_first = pl.program_id(0) == 0
        _, _, _, conv = _conv_fwd(u_ref, uvc_prev, ucc_prev, cw_ref, is_first, tm)
        zc = _bf(u_ref, C_ZC, D)
        yc = _bf(u_ref, C_BC, D) * conv * (zc * _sigmoid(zc))
        ya = _dot(yc.astype(BF16), wco_ref[...])

        _, _, qn, _, _, knb, vband = _attn_inputs(u_ref, ukv_prev, qg_ref, kg_ref, cref, tm)
        bias, bias_first, lane_grp, query_grp = _attn_masks(is_first, cref)
        o_cols = []
        for h in range(N_KV):
            sink = _sink_row(sinks_ref, h, query_grp)
            o_rows = []
            for b in range(nb):
                qs = _stack_groups(qn[b * BLK:(b + 1) * BLK, h * KVW:(h + 1) * KVW], lane_grp, h)
                pn_t, _ = _softmax_block(qs, knb[b * BLK:(b + 2) * BLK], bias_first if b == 0 else bias, sink)
                o4 = _dot_tn(pn_t.astype(BF16), vband[b * BLK:(b + 2) * BLK])
                o_rows.append(_unstack_groups(o4, lane_grp, h))
            o_cols.append(jnp.concatenate(o_rows, axis=0))
        za = _bf(u_ref, C_ZA, D)
        ob = jnp.concatenate(o_cols, axis=1) * (za * _sigmoid(za))
        yb = _dot(ob.astype(BF16), wao_ref[...])

        g_a = _sigmoid(_bf(u_ref, C_GA, D) + gb_ref[:, 0:D])
        g_b = _sigmoid(_bf(u_ref, C_GB, D) + gb_ref[:, D:2 * D])
        merged = g_a * ya + g_b * yb
        xo_ref[...] = x_ref[...] + _dot(merged.astype(BF16), wout_ref[...])
        ya_ref[...] = ya.astype(BF16)
        yb_ref[...] = yb.astype(BF16)

    u_spec, ukv_prev, uvc_prev, ucc_prev = _mixer_specs(t, tm, n_tiles, lambda g: g)
    tok = pl.BlockSpec((tm, D), lambda g: (g, 0))
    consts = [c[k] for k in cn]
    return pl.pallas_call(
        body, name=name, grid=(n_tiles,),
        in_specs=[tok, u_spec, ukv_prev, uvc_prev, ucc_prev, _full((8, D)), _full((1, D)), _full((1, KVW)),
                  pl.BlockSpec(memory_space=pltpu.SMEM), _full((1, 2 * D)), _full((D, D)), _full((D, D)), _full((D, D))]
                 + [_full(a.shape) for a in consts],
        out_specs=[tok, tok, tok],
        out_shape=[jax.ShapeDtypeStruct((t, D), F32)] + [jax.ShapeDtypeStruct((t, D), BF16)] * 2,
        compiler_params=_cparams(("arbitrary",)),
    )(x, u, u, u, u, cw, qg, kg, sinks, gb, wco, wao, wout, *consts)


def mixer_bwd(dout, u, ya, yb, cw, qg, kg, sinks, gb, wco, wao, wout, c, name):
    t = dout.shape[0]
    tm = min(256, t)
    n_tiles = t // tm
    nb = tm // BLK
    kb = tm + BLK
    cn = sorted(c)

    def body(dout_ref, u_ref, ukv_prev, uvc_prev, ucc_prev, ya_ref, yb_ref, cw_ref, qg_ref, kg_ref, sinks_ref, gb_ref,
             wco_ref, wao_ref, wout_ref, *rest):
        cref = dict(zip(cn, rest[:len(cn)]))
        (du_ref, small_ref, merged_ref, yc_ref, ob_ref, dya_ref, dyb_ref, carry_kv, carry_conv) = rest[len(cn):]
        g = pl.program_id(0)
        is_first = g == n_tiles - 1

        @pl.when(g == 0)
        def _():
            carry_kv[...] = jnp.zeros_like(carry_kv)
            carry_conv[...] = jnp.zeros_like(carry_conv)
            small_ref[...] = jnp.zeros_like(small_ref)

        dout = dout_ref[...]
        dout_b = dout.astype(BF16)
        ya_v = ya_ref[...].astype(F32)
        yb_v = yb_ref[...].astype(F32)
        g_a = _sigmoid(_bf(u_ref, C_GA, D) + gb_ref[:, 0:D])
        g_b = _sigmoid(_bf(u_ref, C_GB, D) + gb_ref[:, D:2 * D])
        merged = g_a * ya_v + g_b * yb_v
        dmerged = _dot_nt(dout_b, wout_ref[...])
        merged_ref[...] = merged.astype(BF16)
        dya = dmerged * g_a
        dyb = dmerged * g_b
        dgl_a = dya * ya_v * (1.0 - g_a)
        dgl_b = dyb * yb_v * (1.0 - g_b)
        du_ref[:, C_GA:C_GA + D] = dgl_a.astype(BF16)
        du_ref[:, C_GB:C_GB + D] = dgl_b.astype(BF16)
        small_ref[SM_GATE:SM_GATE + 1, 0:D] += jnp.sum(dgl_a, axis=0, keepdims=True)
        small_ref[SM_GATE_B:SM_GATE_B + 1, 0:D] += jnp.sum(dgl_b, axis=0, keepdims=True)

        p, p1, p2, conv = _conv_fwd(u_ref, uvc_prev, ucc_prev, cw_ref, is_first, tm)
        zc = _bf(u_ref, C_ZC, D)
        bc = _bf(u_ref, C_BC, D)
        sg = _sigmoid(zc)
        sc = zc * sg
        bconv = bc * conv
        dya_b = dya.astype(BF16)
        yc_ref[...] = (bconv * sc).astype(BF16)
        dya_ref[...] = dya_b
        dyc = _dot_nt(dya_b, wco_ref[...])
        dyc_s = dyc * sc
        du_ref[:, C_BC:C_BC + D] = (dyc_s * conv).astype(BF16)
        du_ref[:, C_ZC:C_ZC + D] = (dyc * bconv * (sg + sc * (1.0 - sg))).astype(BF16)
        dconv = dyc_s * bc
        small_ref[SM_CONV + 2:SM_CONV + 3, 0:D] += jnp.sum(dconv * p, axis=0, keepdims=True)
        small_ref[SM_CONV + 1:SM_CONV + 2, 0:D] += jnp.sum(dconv * p1, axis=0, keepdims=True)
        small_ref[SM_CONV:SM_CONV + 1, 0:D] += jnp.sum(dconv * p2, axis=0, keepdims=True)
        nxt = carry_conv[...]
        d1 = _shift_up(dconv, 1, nxt)
        d2 = _shift_up(dconv, 2, nxt)
        carry_conv[...] = dconv[0:8, :]
        cw = cw_ref[...]
        dp = cw[2:3, :] * dconv + cw[1:2, :] * d1 + cw[0:1, :] * d2
        du_ref[:, C_CC:C_CC + D] = (dp * _bf(u_ref, C_VC, D)).astype(BF16)
        du_ref[:, C_VC:C_VC + D] = (dp * _bf(u_ref, C_CC, D)).astype(BF16)

        dyb_b = dyb.astype(BF16)
        dob = _dot_nt(dyb_b, wao_ref[...])
        za = _bf(u_ref, C_ZA, D)
        sga = _sigmoid(za)
        sa = za * sga
        do = dob * sa
        qhat, rq, qn, khat, rk, knb, vband = _attn_inputs(u_ref, ukv_prev, qg_ref, kg_ref, cref, tm)
        bias, bias_first, lane_grp, query_grp = _attn_masks(is_first, cref)
        lane16 = lax.broadcasted_iota(jnp.int32, (1, D), 1)
        dsink_row = jnp.zeros((1, D), F32)
        o_cols, dq_cols, dk4, dv4 = [], [], [], []
        for h in range(N_KV):
            sink = _sink_row(sinks_ref, h, query_grp)
            dsink = jnp.zeros((1, GROUP * BLK), F32)
            o_rows, dq_rows, dk_parts, dv_parts = [], [], [], []
            for b in range(nb):
                rows = slice(b * BLK, (b + 1) * BLK)
                band = slice(b * BLK, (b + 2) * BLK)
                cols = slice(h * KVW, (h + 1) * KVW)
                qs = _stack_groups(qn[rows, cols], lane_grp, h)
                pn_t, ps = _softmax_block(qs, knb[band], bias_first if b == 0 else bias, sink)
                pn_b = pn_t.astype(BF16)
                o_rows.append(_unstack_groups(_dot_tn(pn_b, vband[band]), lane_grp, h))
                dos = _stack_groups(do[rows, cols].astype(BF16), lane_grp, h)
                dpn_t = _dot_nt(vband[band], dos)
                delta = jnp.sum(pn_t * dpn_t, axis=0, keepdims=True)
                ds_t = (pn_t * (dpn_t - delta)).astype(BF16)
                dsink = dsink - ps * delta
                dq_rows.append(_unstack_groups(_dot_tn(ds_t, knb[band]), lane_grp, h))
                dk_parts.append(_dot(ds_t, qs))
                dv_parts.append(_dot(pn_b, dos))
            o_cols.append(jnp.concatenate(o_rows, axis=0))
            dq_cols.append(jnp.concatenate(dq_rows, axis=0))
            dk4.append(_band_sum(dk_parts))
            dv4.append(_band_sum(dv_parts))
            for gi in range(GROUP):
                tot = jnp.sum(dsink[:, gi * BLK:(gi + 1) * BLK], axis=1, keepdims=True)
                dsink_row = dsink_row + jnp.where(lane16 == GROUP * h + gi, tot, 0.0)
        small_ref[SM_SINK:SM_SINK + 1, :] += dsink_row

        o = jnp.concatenate(o_cols, axis=1)
        ob_ref[...] = (o * sa).astype(BF16)
        dyb_ref[...] = dyb_b
        du_ref[:, C_ZA:C_ZA + D] = (dob * o * (sga + sa * (1.0 - sga))).astype(BF16)

        dqn = jnp.concatenate(dq_cols, axis=1)
        small_ref[SM_QG:SM_QG + 1, 0:D] += SCALE * jnp.sum(dqn * qhat, axis=0, keepdims=True)
        dqh = dqn * qg_ref[...]
        dq = rq * (dqh - qhat * _head_mean(dqh * qhat, cref["sel_q"][...], cref["exp_q"][...]))
        du_ref[:, C_Q:C_Q + D] = dq.astype(BF16)

        dkn_band = dk4[0] + dk4[1] + dk4[2] + dk4[3]
        dv_band = dv4[0] + dv4[1] + dv4[2] + dv4[3]
        carried = carry_kv[...]
        pad = jnp.zeros((tm - BLK, KVW), F32)
        if nb > 1:
            dkn = dkn_band[BLK:, :] + jnp.concatenate([pad, carried[:, 0:KVW]], axis=0)
            dv = dv_band[BLK:, :] + jnp.concatenate([pad, carried[:, KVW:2 * KVW]], axis=0)
        else:
            dkn = dkn_band[BLK:, :] + carried[:, 0:KVW]
            dv = dv_band[BLK:, :] + carried[:, KVW:2 * KVW]
        carry_kv[:, 0:KVW] = dkn_band[0:BLK, :]
        carry_kv[:, KVW:2 * KVW] = dv_band[0:BLK, :]
        khat_t = khat[BLK:, :]
        small_ref[SM_KG:SM_KG + 1, 0:KVW] += jnp.sum(dkn * khat_t, axis=0, keepdims=True)
        dkh = dkn * kg_ref[...]
        dk = rk[BLK:, :] * (dkh - khat_t * _head_mean(dkh * khat_t, cref["sel_k"][...], cref["exp_k"][...]))
        du_ref[:, C_K:C_K + KVW] = dk.astype(BF16)
        du_ref[:, C_V:C_V + KVW] = dv.astype(BF16)

    rev = lambda g: n_tiles - 1 - g
    u_spec, ukv_prev, uvc_prev, ucc_prev = _mixer_specs(t, tm, n_tiles, rev)
    tok = pl.BlockSpec((tm, D), lambda g: (rev(g), 0))
    consts = [c[k] for k in cn]
    wspec = _full((D, D))
    return pl.pallas_call(
        body, name=name, grid=(n_tiles,),
        in_specs=[tok, u_spec, ukv_prev, uvc_prev, ucc_prev, tok, tok, _full((8, D)), _full((1, D)), _full((1, KVW)),
                  pl.BlockSpec(memory_space=pltpu.SMEM), _full((1, 2 * D)), wspec, wspec, wspec]
                 + [_full(a.shape) for a in consts],
        out_specs=[pl.BlockSpec((tm, IN_COLS), lambda g: (rev(g), 0)), _full((SM_ROWS, D))] + [tok] * 5,
        out_shape=[jax.ShapeDtypeStruct((t, IN_COLS), BF16), jax.ShapeDtypeStruct((SM_ROWS, D), F32)]
                  + [jax.ShapeDtypeStruct((t, D), BF16)] * 5,
        scratch_shapes=[pltpu.VMEM((BLK, 2 * KVW), F32), pltpu.VMEM((8, D), F32)],
        compiler_params=_cparams(("arbitrary",)),
    )(dout, u, u, u, u, ya, yb, cw, qg, kg, sinks, gb, wco, wao, wout, *consts)


def matmul_tn(a, b, name, after=None):
    t, m = a.shape
    tk = min(1024, t)
    mb = 2176 if m == IN_COLS else m
    nk = t // tk

    def body(a_ref, b_ref, *rest):
        o_ref, acc = rest[-2:]
        k = pl.program_id(1)
        prod = _dot_tn(a_ref[...].astype(BF16), b_ref[...].astype(BF16))

        @pl.when(k == 0)
        def _():
            acc[...] = prod

        @pl.when(k > 0)
        def _():
            acc[...] += prod

        @pl.when(k == nk - 1)
        def _():
            o_ref[...] = acc[...].astype(BF16)

    return pl.pallas_call(
        body, name=name, grid=(m // mb, nk),
        in_specs=[pl.BlockSpec((tk, mb), lambda j, k: (k, j)), pl.BlockSpec((tk, D), lambda j, k: (k, 0))]
                 + ([] if after is None else [ANY]),
        out_specs=pl.BlockSpec((mb, D), lambda j, k: (j, 0)),
        out_shape=jax.ShapeDtypeStruct((m, D), BF16),
        scratch_shapes=[pltpu.VMEM((mb, D), F32)],
        compiler_params=_cparams(("arbitrary", "arbitrary")),
    )(a, b, *([] if after is None else [after]))


def inproj_bwd_x(du, w, x, ng, dout, name):
    t = x.shape[0]
    tm = min(1024, t)
    kc = 2176
    nk = IN_COLS // kc

    def body(du_ref, w_ref, x_ref, ng_ref, dout_ref, dx_ref, dng_ref, acc):
        i = pl.program_id(0)
        k = pl.program_id(1)
        prod = _dot(du_ref[...], w_ref[...])

        @pl.when(k == 0)
        def _():
            acc[...] = prod

        @pl.when(k > 0)
        def _():
            acc[...] += prod

        @pl.when((i == 0) & (k == 0))
        def _():
            dng_ref[...] = jnp.zeros_like(dng_ref)

        @pl.when(k == nk - 1)
        def _():
            dh = acc[...]
            xf = x_ref[...]
            r = lax.rsqrt(jnp.mean(xf * xf, axis=-1, keepdims=True) + EPS)
            xhat = xf * r
            dng_ref[0:1, :] += jnp.sum(dh * xhat, axis=0, keepdims=True)
            dxh = dh * ng_ref[...]
            dx_ref[...] = dout_ref[...] + r * (dxh - xhat * jnp.mean(dxh * xhat, axis=-1, keepdims=True))

    tok = pl.BlockSpec((tm, D), lambda i, k: (i, 0))
    return pl.pallas_call(
        body, name=name, grid=(t // tm, nk),
        in_specs=[pl.BlockSpec((tm, kc), lambda i, k: (i, k)), pl.BlockSpec((kc, D), lambda i, k: (k, 0)), tok,
                  pl.BlockSpec((1, D), lambda i, k: (0, 0)), tok],
        out_specs=[tok, pl.BlockSpec((8, D), lambda i, k: (0, 0))],
        out_shape=[jax.ShapeDtypeStruct((t, D), F32), jax.ShapeDtypeStruct((8, D), F32)],
        scratch_shapes=[pltpu.VMEM((tm, D), F32)],
        compiler_params=_cparams(("arbitrary", "arbitrary")),
    )(du, w, x, ng, dout)


def loss_head(y, target, name):
    t = y.shape[0]
    tm = min(1024, t)

    def body(y_ref, t_ref, dy_ref, loss_ref):
        @pl.when(pl.program_id(0) == 0)
        def _():
            loss_ref[...] = jnp.zeros_like(loss_ref)
        err = y_ref[...] - t_ref[...]
        dy_ref[...] = err * (1.0 / D)
        part = jnp.sum(jnp.sum(err * err, axis=-1, keepdims=True) * (1.0 / D), axis=0, keepdims=True)
        loss_ref[...] += 0.5 * part

    tok = pl.BlockSpec((tm, D), lambda i: (i, 0))
    return pl.pallas_call(
        body, name=name, grid=(t // tm,), in_specs=[tok, tok],
        out_specs=[tok, pl.BlockSpec((8, 128), lambda i: (0, 0))],
        out_shape=[jax.ShapeDtypeStruct((t, D), F32), jax.ShapeDtypeStruct((8, 128), F32)],
        compiler_params=_cparams(("arbitrary",)),
    )(y, target)


def layer_operands(l, norm_g, conv_w_full, q_norm_g, k_norm_g, sinks, gate_b, w_in_b, wco_b, wao_b, wout_b):
    return dict(
        ng=norm_g[l][None, :], cw=jnp.pad(conv_w_full[l], ((0, 5), (0, 0))),
        qg=jnp.tile(q_norm_g[l] * SCALE, D // HEAD)[None, :], kg=jnp.tile(k_norm_g[l], N_KV)[None, :],
        sinks=sinks[l][None, :], gb=gate_b[l][None, :],
        w_in=w_in_b[l], wco=wco_b[l], wao=wao_b[l], wout=wout_b[l])


def layer_bwd(dout, saved, lw, c, l, send_off):
    x, u, h, ya, yb = saved
    du, small, merged, yc, ob, dya, dyb = mixer_bwd(dout, u, ya, yb, lw["cw"], lw["qg"], lw["kg"], lw["sinks"], lw["gb"],
                                                    lw["wco"], lw["wao"], lw["wout"], c, f"mixer_bwd_{l}")
    grads = dict(w_in=matmul_tn(du, h, f"dw_in_{l}"), small=small)
    token = send_off(grads, False)
    grads["wout"] = matmul_tn(merged, dout, f"dw_out_{l}", after=token)
    grads["wco"] = matmul_tn(yc, dya, f"dw_conv_out_{l}")
    grads["wao"] = matmul_tn(ob, dyb, f"dw_attn_out_{l}")
    token = send_off(grads, True)
    dx, grads["dng"] = inproj_bwd_x(du, lw["w_in"], x, lw["ng"] + token[0:1, 0:1], dout, f"inproj_bwd_{l}")
    return dx, grads


MESH = pl.DeviceIdType.MESH
ANY = pl.BlockSpec(memory_space=pl.ANY)


def _place():
    return lax.axis_index("x"), lax.axis_index("y"), lax.axis_index("c")


def all_gather(arrs, after, name):
    n = len(arrs)

    def body(*refs):
        ins, outs = refs[:n], refs[n + 1:2 * n + 1]
        send_sems, recv_sems, local_sems = refs[2 * n + 1:]
        x, y, c = _place()
        me, sibling = (x, y, c), (x, y, 1 - c)
        chips = [(1 - x, y), (x, 1 - y), (1 - x, 1 - y)]

        def slot(a, block):
            px, py, pc = block
            return outs[a].at[4 * px + 2 * py + pc]

        def copy(a, k, block, to, src=None):
            return pltpu.make_async_remote_copy(
                src_ref=slot(a, block) if src is None else src, dst_ref=slot(a, block),
                send_sem=send_sems.at[a, k], recv_sem=recv_sems.at[a, k], device_id=to, device_id_type=MESH)

        mine = [pltpu.make_async_copy(ins[a], slot(a, me), local_sems.at[a]) for a in range(n)]
        for cp in mine:
            cp.start()
        first = []
        for a in range(n):
            first.append(copy(a, 0, me, sibling, src=ins[a]))
            first += [copy(a, 1 + j, me, (*chip, c), src=ins[a]) for j, chip in enumerate(chips)]
        for cp in first:
            cp.start()
        passed = []
        for j, chip in enumerate(chips):
            for a in range(n):
                copy(a, 1 + j, (*chip, c), me).wait_recv()
                passed.append(copy(a, 4 + j, (*chip, c), sibling))
                passed[-1].start()
        for a in range(n):
            copy(a, 0, sibling, me).wait_recv()
            for j, chip in enumerate(chips):
                copy(a, 4 + j, (*chip, 1 - c), me).wait_recv()
        for cp in first + passed:
            cp.wait_send()
        for cp in mine:
            cp.wait()

    return pl.pallas_call(
        body, name=name, in_specs=[ANY] * (n + 1), out_specs=[ANY] * n,
        out_shape=[jax.ShapeDtypeStruct((N_DEV,) + a.shape, a.dtype) for a in arrs],
        scratch_shapes=[pltpu.SemaphoreType.DMA((n, 7)), pltpu.SemaphoreType.DMA((n, 7)), pltpu.SemaphoreType.DMA((n,))],
    )(*arrs, after)


HBM_SPEC = pl.BlockSpec(memory_space=pltpu.HBM)
SEM_SPEC = pl.BlockSpec(memory_space=pltpu.SEMAPHORE)
EFFECT = pltpu.SideEffectType.DATAFLOW_SIDE_EFFECTING


ALL_PEERS = (1, 2, 3, 4, 5, 6, 7)


def _flip(place, k):
    x, y, c = place
    return (1 - x if k & 4 else x, 1 - y if k & 2 else y, 1 - c if k & 1 else c)


def _slot(place):
    return 4 * place[0] + 2 * place[1] + place[2]


def _exchange_copies(ins, lands, send_sems, recv_sems, scatter, flips, arriving):
    me = _place()
    out = []
    for a in range(len(ins)):
        for i, k in enumerate(flips):
            peer = _flip(me, k)
            out.append(pltpu.make_async_remote_copy(
                src_ref=ins[a].at[_slot(peer)] if scatter else ins[a],
                dst_ref=lands[a].at[_slot(peer) if arriving else _slot(me)],
                send_sem=send_sems.at[a * len(flips) + i], recv_sem=recv_sems.at[a * len(flips) + i],
                device_id=peer, device_id_type=MESH))
    return out


def exchange_start(arrs, scatter, flips, after, name):
    n = len(arrs)
    lands = [lax.empty(a.shape if scatter else (N_DEV,) + a.shape, a.dtype) for a in arrs]

    def body(*refs):
        ins, lz = refs[:n], refs[n:2 * n]
        send_sems, recv_sems = refs[2 * n + 1], refs[2 * n + 2]
        token = refs[-1]
        for cp in _exchange_copies(ins, lz, send_sems, recv_sems, scatter, flips, False):
            cp.start()
        token[...] = jnp.zeros_like(token)

    sems = pltpu.SemaphoreType.DMA((n * len(flips),))
    res = pl.pallas_call(
        body, name=name,
        out_shape=(sems, sems, *[pltpu.HBM(a.shape, a.dtype) for a in arrs], *[pltpu.HBM(a.shape, a.dtype) for a in lands],
                   jax.ShapeDtypeStruct((8, 128), F32)),
        in_specs=[HBM_SPEC] * (2 * n) + [ANY],
        out_specs=(SEM_SPEC, SEM_SPEC, *[HBM_SPEC] * (2 * n), pl.BlockSpec(memory_space=pltpu.VMEM)),
        input_output_aliases={i: 2 + i for i in range(2 * n)},
        compiler_params=pltpu.CompilerParams(has_side_effects=EFFECT),
    )(*[pltpu.with_memory_space_constraint(a, pltpu.HBM) for a in arrs],
      *[pltpu.with_memory_space_constraint(a, pltpu.HBM) for a in lands], after)
    return dict(send=res[0], recv=res[1], srcs=res[2:2 + n], lands=res[2 + n:2 + 2 * n], token=res[-1], scatter=scatter,
                flips=flips)


def exchange_wait(state, after, name):
    n = len(state["srcs"])

    def body(*refs):
        ins, lz = refs[:n], refs[n:2 * n]
        send_sems, recv_sems = refs[2 * n], refs[2 * n + 1]
        for cp in _exchange_copies(ins, lz, send_sems, recv_sems, state["scatter"], state["flips"], True):
            cp.wait_send()
            cp.wait_recv()

    both = list(state["srcs"]) + list(state["lands"])
    res = pl.pallas_call(
        body, name=name, out_shape=tuple(pltpu.HBM(a.shape, a.dtype) for a in both),
        in_specs=[HBM_SPEC] * (2 * n) + [SEM_SPEC, SEM_SPEC, ANY], out_specs=tuple([HBM_SPEC] * (2 * n)),
        input_output_aliases={i: i for i in range(2 * n)},
        compiler_params=pltpu.CompilerParams(has_side_effects=EFFECT),
    )(*both, state["send"], state["recv"], after)
    return res[:n], res[n:]


OTHER_CHIPS = (2, 4, 6)
SAME_CORE = (1, 2, 4, 6)


def _forward_copies(lands, send_sems, recv_sems, arriving):
    me = _place()
    sibling = _flip(me, 1)
    origin = sibling if arriving else me
    return [pltpu.make_async_remote_copy(
        src_ref=lands[a].at[_slot(_flip(origin, k))], dst_ref=lands[a].at[_slot(_flip(origin, k))],
        send_sem=send_sems.at[a * len(OTHER_CHIPS) + i], recv_sem=recv_sems.at[a * len(OTHER_CHIPS) + i],
        device_id=sibling, device_id_type=MESH) for a in range(len(lands)) for i, k in enumerate(OTHER_CHIPS)]


def forward_start(lands, after, name):
    n = len(lands)

    def body(*refs):
        lz = refs[:n]
        send_sems, recv_sems = refs[n + 1], refs[n + 2]
        token = refs[-1]
        for cp in _forward_copies(lz, send_sems, recv_sems, False):
            cp.start()
        token[...] = jnp.zeros_like(token)

    sems = pltpu.SemaphoreType.DMA((n * len(OTHER_CHIPS),))
    res = pl.pallas_call(
        body, name=name,
        out_shape=(sems, sems, *[pltpu.HBM(a.shape, a.dtype) for a in lands], jax.ShapeDtypeStruct((8, 128), F32)),
        in_specs=[HBM_SPEC] * n + [ANY],
        out_specs=(SEM_SPEC, SEM_SPEC, *[HBM_SPEC] * n, pl.BlockSpec(memory_space=pltpu.VMEM)),
        input_output_aliases={i: 2 + i for i in range(n)},
        compiler_params=pltpu.CompilerParams(has_side_effects=EFFECT),
    )(*[pltpu.with_memory_space_constraint(a, pltpu.HBM) for a in lands], after)
    return dict(send=res[0], recv=res[1], lands=res[2:2 + n], token=res[-1])


def forward_wait(state, after, name):
    n = len(state["lands"])

    def body(*refs):
        lz = refs[:n]
        send_sems, recv_sems = refs[n], refs[n + 1]
        for cp in _forward_copies(lz, send_sems, recv_sems, False):
            cp.wait_send()
        for cp in _forward_copies(lz, send_sems, recv_sems, True):
            cp.wait_recv()

    return pl.pallas_call(
        body, name=name, out_shape=tuple(pltpu.HBM(a.shape, a.dtype) for a in state["lands"]),
        in_specs=[HBM_SPEC] * n + [SEM_SPEC, SEM_SPEC, ANY], out_specs=tuple([HBM_SPEC] * n),
        input_output_aliases={i: i for i in range(n)},
        compiler_params=pltpu.CompilerParams(has_side_effects=EFFECT),
    )(*state["lands"], state["send"], state["recv"], after)


def adamw(w, m, v, parts, name):
    r, cdim = w.shape
    n_parts = parts.shape[0]
    rb = 256 if r % 256 == 0 else (SHARD_COLS // 4 if r == SHARD_COLS else r)

    def body(w_ref, m_ref, v_ref, p_ref, g_ref, d_ref, mo_ref, vo_ref):
        g = p_ref[0].astype(F32)
        for i in range(1, n_parts):
            g = g + p_ref[i].astype(F32)
        m_new = ADAM_B1 * m_ref[...] + (1.0 - ADAM_B1) * g
        v_new = ADAM_B2 * v_ref[...] + (1.0 - ADAM_B2) * (g * g)
        m_hat = m_new / (1.0 - ADAM_B1 ** ADAM_STEP)
        v_hat = v_new / (1.0 - ADAM_B2 ** ADAM_STEP)
        g_ref[...] = g
        d_ref[...] = -ADAM_LR * (m_hat / (jnp.sqrt(v_hat) + ADAM_EPS) + ADAM_WD * w_ref[...])
        mo_ref[...] = m_new
        vo_ref[...] = v_new

    blk = pl.BlockSpec((rb, cdim), lambda i: (i, 0))
    return pl.pallas_call(
        body, name=name, grid=(r // rb,),
        in_specs=[blk, blk, blk, pl.BlockSpec((n_parts, rb, cdim), lambda i: (0, i, 0))],
        out_specs=[blk] * 4, out_shape=[jax.ShapeDtypeStruct((r, cdim), F32)] * 4,
        compiler_params=_cparams(("arbitrary",)),
    )(w, m, v, parts)


def adamw_layers(w, m, v, lands, srcs, me, lo, prev, name):
    nl_all, r, cdim = w.shape
    nl = len(lands)
    rb = 256 if r % 256 == 0 else (SHARD_COLS // 4 if r == SHARD_COLS else r)
    nblk = r // rb
    n_prev = 0 if prev is None else 4

    def body(me_ref, w_ref, m_ref, v_ref, *rest):
        land_refs, src_refs = rest[:nl], rest[nl:2 * nl]
        g_ref, d_ref, mo_ref, vo_ref = rest[2 * nl + n_prev:]
        for k in range(nl):
            @pl.when(pl.program_id(0) == k)
            def _(k=k):
                own = src_refs[k][...].astype(F32)
                g = jnp.where(me_ref[0] == 0, own, land_refs[k][0].astype(F32))
                for i in range(1, N_DEV):
                    g = g + jnp.where(me_ref[0] == i, own, land_refs[k][i].astype(F32))
                m_new = ADAM_B1 * m_ref[...] + (1.0 - ADAM_B1) * g
                v_new = ADAM_B2 * v_ref[...] + (1.0 - ADAM_B2) * (g * g)
                m_hat = m_new / (1.0 - ADAM_B1 ** ADAM_STEP)
                v_hat = v_new / (1.0 - ADAM_B2 ** ADAM_STEP)
                g_ref[...] = g
                d_ref[...] = -ADAM_LR * (m_hat / (jnp.sqrt(v_hat) + ADAM_EPS) + ADAM_WD * w_ref[...])
                mo_ref[...] = m_new
                vo_ref[...] = v_new

    blk = pl.BlockSpec((None, rb, cdim), lambda l, i, me_ref: (lo + l, i, 0))

    def rows(l, i, k):
        return jnp.where(l < k, 0, jnp.where(l == k, i, nblk - 1))

    land_specs = [pl.BlockSpec((N_DEV, rb, cdim), lambda l, i, me_ref, k=k: (0, rows(l, i, k), 0)) for k in range(nl)]
    src_specs = [pl.BlockSpec((None, rb, cdim), lambda l, i, me_ref, k=k: (me_ref[0], rows(l, i, k), 0)) for k in range(nl)]
    return pl.pallas_call(
        body, name=name,
        grid_spec=pltpu.PrefetchScalarGridSpec(
            num_scalar_prefetch=1, grid=(nl, nblk),
            in_specs=[blk, blk, blk] + land_specs + src_specs + [ANY] * n_prev, out_specs=[blk] * 4),
        out_shape=[jax.ShapeDtypeStruct((nl_all, r, cdim), F32)] * 4,
        input_output_aliases={4 + 2 * nl + j: j for j in range(n_prev)},
        compiler_params=_cparams(("arbitrary", "arbitrary")),
    )(me, w, m, v, *lands, *srcs, *([] if prev is None else prev))


def small_sum(parts, fold, name):
    rows = parts.shape[1]

    def dot3(xv, sel):
        out = jnp.zeros((xv.shape[0], sel.shape[1]), F32)
        for _ in range(3):
            hi = xv.astype(BF16)
            out = out + _dot(hi, sel)
            xv = xv - hi.astype(F32)
        return out

    def body(p_ref, fold_ref, o_ref):
        tot = p_ref[0]
        for i in range(1, N_DEV):
            tot = tot + p_ref[i]
        o_ref[...] = tot
        for l in range(rows // SM_ROWS):
            blk = tot[l * SM_ROWS:l * SM_ROWS + 8, 0:D]
            folded = dot3(blk, fold_ref[...])
            o_ref[l * SM_ROWS + SM_QG_FOLDED:l * SM_ROWS + SM_QG_FOLDED + 1, 0:128] = folded[SM_QG:SM_QG + 1, :]
            o_ref[l * SM_ROWS + SM_KG_FOLDED:l * SM_ROWS + SM_KG_FOLDED + 1, 0:128] = folded[SM_KG:SM_KG + 1, :]

    return pl.pallas_call(
        body, name=name, out_shape=jax.ShapeDtypeStruct((rows, D), F32),
        compiler_params=_cparams(None),
    )(parts, fold)


def kernel(x, norm_g, w_in, conv_w, q_norm_g, k_norm_g, sinks, w_conv_out, w_attn_out, gate_b, w_out, loss_target, m_norm_g, m_w_in, m_conv_w, m_q_norm_g, m_k_norm_g, m_sinks, m_w_conv_out, m_w_attn_out, m_gate_b, m_w_out, v_norm_g, v_w_in, v_conv_w, v_q_norm_g, v_k_norm_g, v_sinks, v_w_conv_out, v_w_attn_out, v_gate_b, v_w_out):
    c = _selectors()
    me = 4 * lax.axis_index("x") + 2 * lax.axis_index("y") + lax.axis_index("c")

    w_in_t, m_w_in_t, v_w_in_t = (jnp.swapaxes(a, 1, 2) for a in (w_in, m_w_in, v_w_in))

    def shards(l):
        return [w_in_t[l].astype(BF16), w_conv_out[l].astype(BF16), w_attn_out[l].astype(BF16), w_out[l].astype(BF16)]

    def ici_start(l, after):
        return exchange_start(shards(l), False, SAME_CORE, after, f"gather_start_{l}")

    h = x[0]
    saved, lws = [], []
    lands = all_gather(shards(0) + [conv_w], h, "gather_0")
    conv_full = jnp.transpose(lands[4], (1, 2, 0, 3)).reshape(DEPTH, 3, D)
    ici = ici_start(1, lands[1])
    for l in range(DEPTH):
        lws.append(layer_operands(l, norm_g, conv_full, q_norm_g, k_norm_g, sinks, gate_b,
                                  {l: lands[0].reshape(IN_COLS, D)}, {l: lands[1].reshape(D, D)},
                                  {l: lands[2].reshape(D, D)}, {l: lands[3].reshape(D, D)}))
        ng_l = lws[l]["ng"] + ici["token"][0:1, 0:1] if l == 0 else lws[l]["ng"]
        u, hb = inproj_fwd(h, ng_l, lws[l]["w_in"], f"inproj_fwd_{l}")
        gb_l = lws[l]["gb"]
        if l + 1 < DEPTH:
            mine, arrived = exchange_wait(ici, u, f"gather_wait_{l + 1}")
            chip = forward_start(arrived, mine[0], f"gather_forward_start_{l + 1}")
            started = chip["token"]
            if l + 2 < DEPTH:
                ici = ici_start(l + 2, started)
                started = ici["token"]
            gb_l = gb_l + started[0:1, 0:1]
        x_in = h
        h, ya, yb = mixer_fwd(x_in, u, lws[l]["cw"], lws[l]["qg"], lws[l]["kg"], lws[l]["sinks"], gb_l,
                              lws[l]["wco"], lws[l]["wao"], lws[l]["wout"], c, f"mixer_fwd_{l}")
        saved.append((x_in, u, hb, ya, yb))
        if l + 1 < DEPTH:
            lands = forward_wait(chip, h, f"gather_forward_wait_{l + 1}")
            lands = [lax.dynamic_update_index_in_dim(land, src, me, 0) for land, src in zip(lands, mine)]
    dh, loss_part = loss_head(h, loss_target[0], "loss_head")

    grads, scatters = [None] * DEPTH, [[] for _ in range(DEPTH)]
    for l in reversed(range(DEPTH)):
        def send_off(g, done, l=l):
            rest = [g[k].reshape(N_DEV, SHARD_ROWS, D) for k in ("wco", "wao", "wout")] if done else []
            first = [g["w_in"].reshape(N_DEV, SHARD_COLS, D)] if done == (l > 0) else []
            if not first + rest:
                return None
            tag = f"{l}" if l > 0 else ("0_rest" if done else "0_in")
            scatters[l].append(exchange_start(first + rest, True, ALL_PEERS, g["small"], f"scatter_start_{tag}"))
            return scatters[l][-1]["token"]
        dh, grads[l] = layer_bwd(dh, saved[l], lws[l], c, l, send_off)

    me1 = me.astype(jnp.int32).reshape(1)
    mine, lands = {}, {}
    for l in (3, 2, 1):
        mine[l], lands[l] = exchange_wait(scatters[l][0], dh, f"scatter_wait_{l}")
    weights = [(w_in_t, m_w_in_t, v_w_in_t, "w_in"), (w_conv_out, m_w_conv_out, v_w_conv_out, "w_conv_out"),
               (w_attn_out, m_w_attn_out, v_w_attn_out, "w_attn_out"), (w_out, m_w_out, v_w_out, "w_out")]
    upd = [adamw_layers(w, m, v, [lands[l][i] for l in (1, 2, 3)], [mine[l][i] for l in (1, 2, 3)], me1, 1, None,
                        f"adamw_{n}_upper") for i, (w, m, v, n) in enumerate(weights)]
    blocks = []
    for l in range(DEPTH):
        blk = grads[l]["small"]
        blk = blk.at[SM_NORM, 0:D].set(grads[l]["dng"][0])
        if l == 0:
            blk = blk.at[SM_LOSS, 0:128].set(loss_part[0])
        blocks.append(blk)
    small_x = exchange_start([jnp.concatenate(blocks, axis=0)], False, ALL_PEERS, grads[0]["dng"], "gather_small_start")

    m_in, l_in = exchange_wait(scatters[0][0], upd[0][0], "scatter_wait_0_in")
    upd[0] = adamw_layers(*weights[0][:3], [l_in[0]], [m_in[0]], me1, 0, upd[0], "adamw_w_in_0")
    m_rest, l_rest = exchange_wait(scatters[0][1], upd[0][0], "scatter_wait_0_rest")
    for i in (1, 2, 3):
        upd[i] = adamw_layers(*weights[i][:3], [l_rest[i - 1]], [m_rest[i - 1]], me1, 0, upd[i], f"adamw_{weights[i][3]}_0")
    u_in = [jnp.swapaxes(o, 1, 2) for o in upd[0]]
    u_co, u_ao, u_out = upd[1], upd[2], upd[3]

    mine_s, lands_s = exchange_wait(small_x, u_out[0], "gather_small_wait")
    gathered = lax.dynamic_update_index_in_dim(lands_s[0], mine_s[0], me, 0)
    tot = small_sum(gathered, c["fold"], "small_sum")
    tot = tot.reshape(DEPTH, SM_ROWS, D)
    loss = tot[0, SM_LOSS, 0]

    def update_small(w, m, v, g, name):
        return adamw(w, m, v, g[None], name)

    u_ng = update_small(norm_g, m_norm_g, v_norm_g, tot[:, SM_NORM, 0:D], "adamw_norm_g")
    u_qg = update_small(q_norm_g, m_q_norm_g, v_q_norm_g, tot[:, SM_QG_FOLDED, 0:HEAD], "adamw_q_norm_g")
    u_kg = update_small(k_norm_g, m_k_norm_g, v_k_norm_g, tot[:, SM_KG_FOLDED, 0:HEAD], "adamw_k_norm_g")
    u_sk = update_small(sinks, m_sinks, v_sinks, tot[:, SM_SINK, 0:16], "adamw_sinks")
    g_gate = jnp.concatenate([tot[:, SM_GATE, :], tot[:, SM_GATE_B, :]], axis=1)
    u_gb = update_small(gate_b, m_gate_b, v_gate_b, g_gate, "adamw_gate_b")
    g_conv = lax.dynamic_slice_in_dim(tot[:, SM_CONV:SM_CONV + 3, 0:D], me * SHARD_ROWS, SHARD_ROWS, axis=2)
    u_cw = [o.reshape(DEPTH, 3, SHARD_ROWS) for o in update_small(
        conv_w.reshape(DEPTH * 3, SHARD_ROWS), m_conv_w.reshape(DEPTH * 3, SHARD_ROWS),
        v_conv_w.reshape(DEPTH * 3, SHARD_ROWS), g_conv.reshape(DEPTH * 3, SHARD_ROWS), "adamw_conv_w")]

    order = [u_ng, u_in, u_cw, u_qg, u_kg, u_sk, u_co, u_ao, u_gb, u_out]
    return (loss, dh[None], *[u[0] for u in order], *[u[1] for u in order], *[u[2] for u in order], *[u[3] for u in order])
```

```python
import functools

import jax
import jax.numpy as jnp
from jax import lax
from jax.experimental import pallas as pl
from jax.experimental.pallas import tpu as pltpu

F32 = jnp.float32
BF16 = jnp.bfloat16

N_DEV = 8
DEPTH = 4
D = 1024
N_KV = 4
GROUP = 4
HEAD = 64
BLK = 128
KVW = N_KV * HEAD
IN_COLS = 8704
SHARD_COLS = IN_COLS // N_DEV
SHARD_ROWS = D // N_DEV
C_VC, C_BC, C_CC, C_ZC, C_Q, C_K, C_V, C_ZA, C_GA, C_GB = 0, 1024, 2048, 3072, 4096, 5120, 5376, 5632, 6656, 7680
EPS = 1e-6
NEG_INF = -1e30
SCALE = HEAD ** -0.5
LOG2E = 1.4426950408889634
LN2 = 0.6931471805599453

ADAM_LR = 0.001
ADAM_B1 = 0.9
ADAM_B2 = 0.999
ADAM_EPS = 1e-08
ADAM_WD = 0.01
ADAM_STEP = 10

VMEM_LIMIT = 60 * 1024 * 1024
SM_ROWS = 16
SM_GATE, SM_CONV, SM_QG, SM_KG, SM_SINK, SM_NORM, SM_LOSS, SM_GATE_B, SM_QG_FOLDED, SM_KG_FOLDED = 0, 1, 4, 5, 6, 7, 8, 9, 10, 11


def _cparams(sem):
    return pltpu.CompilerParams(dimension_semantics=sem, vmem_limit_bytes=VMEM_LIMIT)


def _dot(a, b):
    return jnp.dot(a, b, preferred_element_type=F32)


def _dot_nt(a, b):
    return lax.dot_general(a, b, (((1,), (1,)), ((), ())), preferred_element_type=F32)


def _dot_tn(a, b):
    return lax.dot_general(a, b, (((0,), (0,)), ((), ())), preferred_element_type=F32)


def _dot2(x, sel):
    hi = x.astype(BF16)
    lo = (x - hi.astype(F32)).astype(BF16)
    return _dot(hi, sel) + _dot(lo, sel)


def _sigmoid(z):
    return 0.5 * jnp.tanh(0.5 * z) + 0.5


def _head_mean(t, sel, exp):
    return _dot2(_dot(t.astype(BF16), sel) * (1.0 / HEAD), exp)


def _shift_down(a, k, before):
    r = pltpu.roll(a, k, 0)
    row = lax.broadcasted_iota(jnp.int32, (8, 1), 0)
    head = jnp.where(row < k, pltpu.roll(before, k, 0), r[0:8, :])
    return jnp.concatenate([head, r[8:, :]], axis=0)


def _shift_up(a, k, after):
    n = a.shape[0]
    r = pltpu.roll(a, n - k, 0)
    row = lax.broadcasted_iota(jnp.int32, (8, 1), 0)
    tail = jnp.where(row >= 8 - k, pltpu.roll(after, 8 - k, 0), r[n - 8:, :])
    return jnp.concatenate([r[:n - 8, :], tail], axis=0)


def _bf(ref, c0, width):
    return ref[:, c0:c0 + width].astype(F32)


def _selectors():
    c = jnp.arange(D)
    sel_q = (c[:, None] // HEAD == jnp.arange(128)[None, :]).astype(BF16)
    ck = jnp.arange(KVW)
    sel_k = (ck[:, None] // HEAD == jnp.arange(128)[None, :]).astype(BF16)
    fold = (c[:, None] % HEAD == jnp.arange(128)[None, :]).astype(BF16)
    qq = jnp.arange(GROUP * BLK)[:, None] % BLK
    kk = jnp.arange(2 * BLK)[None, :]
    valid = (kk > qq) & (kk <= qq + BLK)
    bias = jnp.where(valid, 0.0, NEG_INF).astype(F32).T
    bias_first = jnp.where(valid & (kk >= BLK), 0.0, NEG_INF).astype(F32).T
    return dict(bias=bias, bias_first=bias_first,sel_q=sel_q, exp_q=sel_q.T, sel_k=sel_k, exp_k=sel_k.T, fold=fold)


def inproj_fwd(x, ng, w, name):
    t = x.shape[0]
    tm = min(1024, t)
    cb = 2176
    def body(x_ref, ng_ref, w_ref, u_ref, h_ref, h_scr):
        @pl.when(pl.program_id(1) == 0)
        def _():
            xf = x_ref[...]
            r = lax.rsqrt(jnp.mean(xf * xf, axis=-1, keepdims=True) + EPS)
            hb = (xf * r * ng_ref[...]).astype(BF16)
            h_scr[...] = hb
            h_ref[...] = hb
        u_ref[...] = _dot_nt(h_scr[...], w_ref[...]).astype(BF16)

    return pl.pallas_call(
        body, name=name, grid=(t // tm, IN_COLS // cb),
        in_specs=[pl.BlockSpec((tm, D), lambda i, j: (i, 0)), pl.BlockSpec((1, D), lambda i, j: (0, 0)),
                  pl.BlockSpec((cb, D), lambda i, j: (j, 0))],
        out_specs=[pl.BlockSpec((tm, cb), lambda i, j: (i, j)), pl.BlockSpec((tm, D), lambda i, j: (i, 0))],
        out_shape=[jax.ShapeDtypeStruct((t, IN_COLS), BF16), jax.ShapeDtypeStruct((t, D), BF16)],
        scratch_shapes=[pltpu.VMEM((tm, D), BF16)],
        compiler_params=_cparams(("arbitrary", "arbitrary")),
    )(x, ng, w)


def _conv_fwd(u_ref, uvc_prev, ucc_prev, cw_ref, is_first, tm):
    p = _bf(u_ref, C_CC, D) * _bf(u_ref, C_VC, D)
    pprev = ucc_prev[...].astype(F32) * uvc_prev[...].astype(F32)
    pprev = jnp.where(is_first, 0.0, pprev)
    p1 = _shift_down(p, 1, pprev[8:16, :])
    p2 = _shift_down(p, 2, pprev[8:16, :])
    cw = cw_ref[...]
    conv = cw[0:1, :] * p2 + cw[1:2, :] * p1 + cw[2:3, :] * p
    return p, p1, p2, conv


def _attn_inputs(u_ref, ukv_prev, qg_ref, kg_ref, c, tm):
    q = _bf(u_ref, C_Q, D)
    rq = lax.rsqrt(_head_mean(q * q, c["sel_q"][...], c["exp_q"][...]) + EPS)
    qhat = q * rq
    qn = (qhat * qg_ref[...]).astype(BF16)
    kband = jnp.concatenate([ukv_prev[:, 0:KVW].astype(F32), _bf(u_ref, C_K, KVW)], axis=0)
    rk = lax.rsqrt(_head_mean(kband * kband, c["sel_k"][...], c["exp_k"][...]) + EPS)
    khat = kband * rk
    knb = (khat * kg_ref[...]).astype(BF16)
    vband = jnp.concatenate([ukv_prev[:, KVW:2 * KVW], u_ref[:, C_V:C_V + KVW]], axis=0)
    return qhat, rq, qn, khat, rk, knb, vband


def _attn_masks(is_first, c):
    bias = c["bias"][...]
    bias_first = jnp.where(is_first, c["bias_first"][...], bias)
    lane_grp = lax.broadcasted_iota(jnp.int32, (BLK, KVW), 1) // HEAD
    query_grp = lax.broadcasted_iota(jnp.int32, (1, GROUP * BLK), 1) // BLK
    return bias, bias_first, lane_grp, query_grp


def _sink_row(sinks_ref, h, query_grp):
    row = jnp.full(query_grp.shape, sinks_ref[0, GROUP * h] * LOG2E, F32)
    for gi in range(1, GROUP):
        row = jnp.where(query_grp == gi, sinks_ref[0, GROUP * h + gi] * LOG2E, row)
    return row


def _lane_shift(a, groups):
    shift = (HEAD * groups) % KVW
    return a if shift == 0 else pltpu.roll(a, shift, 1)


def _stack_groups(a256, lane_grp, h):
    zero = jnp.zeros_like(a256)
    return jnp.concatenate([jnp.where(lane_grp == h, _lane_shift(a256, h - gi), zero) for gi in range(GROUP)], axis=0)


def _unstack_groups(a4, lane_grp, h):
    out = _lane_shift(a4[(GROUP - 1) * BLK:, :], GROUP - 1 - h)
    for gi in reversed(range(GROUP - 1)):
        out = jnp.where(lane_grp == gi, _lane_shift(a4[gi * BLK:(gi + 1) * BLK], gi - h), out)
    return out


def _band_sum(parts):
    pieces = [parts[0][0:BLK, :]]
    for b in range(1, len(parts)):
        pieces.append(parts[b - 1][BLK:, :] + parts[b][0:BLK, :])
    pieces.append(parts[-1][BLK:, :])
    return jnp.concatenate(pieces, axis=0)


def _softmax_block(qs, kt_b, bias, sink):
    s = _dot_nt(kt_b, qs) + bias
    m = jnp.maximum(jnp.max(s, axis=0, keepdims=True), sink)
    e = jnp.exp2(s - m)
    es = jnp.exp2(sink - m)
    inv = 1.0 / (jnp.sum(e, axis=0, keepdims=True) + es)
    return e * inv, es * inv


def _mixer_specs(t, tm, n_tiles, tile_of):
    nb = tm // BLK
    u_spec = pl.BlockSpec((tm, IN_COLS), lambda g: (tile_of(g), 0))
    ukv_prev = pl.BlockSpec((BLK, 2 * KVW), lambda g: (jnp.maximum(tile_of(g) * nb - 1, 0), C_K // (2 * KVW)))
    uvc_prev = pl.BlockSpec((16, D), lambda g: (jnp.maximum(tile_of(g) * (tm // 16) - 1, 0), C_VC // D))
    ucc_prev = pl.BlockSpec((16, D), lambda g: (jnp.maximum(tile_of(g) * (tm // 16) - 1, 0), C_CC // D))
    return u_spec, ukv_prev, uvc_prev, ucc_prev


def _full(shape):
    n = len(shape)
    return pl.BlockSpec(shape, lambda g: (0,) * n)


def mixer_fwd(x, u, cw, qg, kg, sinks, gb, wco, wao, wout, c, name):
    t = x.shape[0]
    tm = min(256, t)
    n_tiles = t // tm
    nb = tm // BLK
    cn = sorted(c)

    def body(x_ref, u_ref, ukv_prev, uvc_prev, ucc_prev, cw_ref, qg_ref, kg_ref, sinks_ref, gb_ref, wco_ref, wao_ref,
             wout_ref, *rest):
        cref = dict(zip(cn, rest[:len(cn)]))
        xo_ref, ya_ref, yb_ref = rest[len(cn):]
        is_first = pl.program_id(0) == 0
        _, _, _, conv = _conv_fwd(u_ref, uvc_prev, ucc_prev, cw_ref, is_first, tm)
        zc = _bf(u_ref, C_ZC, D)
        yc = _bf(u_ref, C_BC, D) * conv * (zc * _sigmoid(zc))
        ya = _dot(yc.astype(BF16), wco_ref[...])

        _, _, qn, _, _, knb, vband = _attn_inputs(u_ref, ukv_prev, qg_ref, kg_ref, cref, tm)
        bias, bias_first, lane_grp, query_grp = _attn_masks(is_first, cref)
        o_cols = []
        for h in range(N_KV):
            sink = _sink_row(sinks_ref, h, query_grp)
            o_rows = []
            for b in range(nb):
                qs = _stack_groups(qn[b * BLK:(b + 1) * BLK, h * KVW:(h + 1) * KVW], lane_grp, h)
                pn_t, _ = _softmax_block(qs, knb[b * BLK:(b + 2) * BLK], bias_first if b == 0 else bias, sink)
                o4 = _dot_tn(pn_t.astype(BF16), vband[b * BLK:(b + 2) * BLK])
                o_rows.append(_unstack_groups(o4, lane_grp, h))
            o_cols.append(jnp.concatenate(o_rows, axis=0))
        za = _bf(u_ref, C_ZA, D)
        ob = jnp.concatenate(o_cols, axis=1) * (za * _sigmoid(za))
        yb = _dot(ob.astype(BF16), wao_ref[...])

        g_a = _sigmoid(_bf(u_ref, C_GA, D) + gb_ref[:, 0:D])
        g_b = _sigmoid(_bf(u_ref, C_GB, D) + gb_ref[:, D:2 * D])
        merged = g_a * ya + g_b * yb
        xo_ref[...] = x_ref[...] + _dot(merged.astype(BF16), wout_ref[...])
        ya_ref[...] = ya.astype(BF16)
        yb_ref[...] = yb.astype(BF16)

    u_spec, ukv_prev, uvc_prev, ucc_prev = _mixer_specs(t, tm, n_tiles, lambda g: g)
    tok = pl.BlockSpec((tm, D), lambda g: (g, 0))
    consts = [c[k] for k in cn]
    return pl.pallas_call(
        body, name=name, grid=(n_tiles,),
        in_specs=[tok, u_spec, ukv_prev, uvc_prev, ucc_prev, _full((8, D)), _full((1, D)), _full((1, KVW)),
                  pl.BlockSpec(memory_space=pltpu.SMEM), _full((1, 2 * D)), _full((D, D)), _full((D, D)), _full((D, D))]
                 + [_full(a.shape) for a in consts],
        out_specs=[tok, tok, tok],
        out_shape=[jax.ShapeDtypeStruct((t, D), F32)] + [jax.ShapeDtypeStruct((t, D), BF16)] * 2,
        compiler_params=_cparams(("arbitrary",)),
    )(x, u, u, u, u, cw, qg, kg, sinks, gb, wco, wao, wout, *consts)


def mixer_bwd(dout, u, ya, yb, cw, qg, kg, sinks, gb, wco, wao, wout, c, name):
    t = dout.shape[0]
    tm = min(256, t)
    n_tiles = t // tm
    nb = tm // BLK
    kb = tm + BLK
    cn = sorted(c)

    def body(dout_ref, u_ref, ukv_prev, uvc_prev, ucc_prev, ya_ref, yb_ref, cw_ref, qg_ref, kg_ref, sinks_ref, gb_ref,
             wco_ref, wao_ref, wout_ref, *rest):
        cref = dict(zip(cn, rest[:len(cn)]))
        (du_ref, small_ref, merged_ref, yc_ref, ob_ref, dya_ref, dyb_ref, carry_kv, carry_conv) = rest[len(cn):]
        g = pl.program_id(0)
        is_first = g == n_tiles - 1

        @pl.when(g == 0)
        def _():
            carry_kv[...] = jnp.zeros_like(carry_kv)
            carry_conv[...] = jnp.zeros_like(carry_conv)
            small_ref[...] = jnp.zeros_like(small_ref)

        dout = dout_ref[...]
        dout_b = dout.astype(BF16)
        ya_v = ya_ref[...].astype(F32)
        yb_v = yb_ref[...].astype(F32)
        g_a = _sigmoid(_bf(u_ref, C_GA, D) + gb_ref[:, 0:D])
        g_b = _sigmoid(_bf(u_ref, C_GB, D) + gb_ref[:, D:2 * D])
        merged = g_a * ya_v + g_b * yb_v
        dmerged = _dot_nt(dout_b, wout_ref[...])
        merged_ref[...] = merged.astype(BF16)
        dya = dmerged * g_a
        dyb = dmerged * g_b
        dgl_a = dya * ya_v * (1.0 - g_a)
        dgl_b = dyb * yb_v * (1.0 - g_b)
        du_ref[:, C_GA:C_GA + D] = dgl_a.astype(BF16)
        du_ref[:, C_GB:C_GB + D] = dgl_b.astype(BF16)
        small_ref[SM_GATE:SM_GATE + 1, 0:D] += jnp.sum(dgl_a, axis=0, keepdims=True)
        small_ref[SM_GATE_B:SM_GATE_B + 1, 0:D] += jnp.sum(dgl_b, axis=0, keepdims=True)

        p, p1, p2, conv = _conv_fwd(u_ref, uvc_prev, ucc_prev, cw_ref, is_first, tm)
        zc = _bf(u_ref, C_ZC, D)
        bc = _bf(u_ref, C_BC, D)
        sg = _sigmoid(zc)
        sc = zc * sg
        bconv = bc * conv
        dya_b = dya.astype(BF16)
        yc_ref[...] = (bconv * sc).astype(BF16)
        dya_ref[...] = dya_b
        dyc = _dot_nt(dya_b, wco_ref[...])
        dyc_s = dyc * sc
        du_ref[:, C_BC:C_BC + D] = (dyc_s * conv).astype(BF16)
        du_ref[:, C_ZC:C_ZC + D] = (dyc * bconv * (sg + sc * (1.0 - sg))).astype(BF16)
        dconv = dyc_s * bc
        small_ref[SM_CONV + 2:SM_CONV + 3, 0:D] += jnp.sum(dconv * p, axis=0, keepdims=True)
        small_ref[SM_CONV + 1:SM_CONV + 2, 0:D] += jnp.sum(dconv * p1, axis=0, keepdims=True)
        small_ref[SM_CONV:SM_CONV + 1, 0:D] += jnp.sum(dconv * p2, axis=0, keepdims=True)
        nxt = carry_conv[...]
        d1 = _shift_up(dconv, 1, nxt)
        d2 = _shift_up(dconv, 2, nxt)
        carry_conv[...] = dconv[0:8, :]
        cw = cw_ref[...]
        dp = cw[2:3, :] * dconv + cw[1:2, :] * d1 + cw[0:1, :] * d2
        du_ref[:, C_CC:C_CC + D] = (dp * _bf(u_ref, C_VC, D)).astype(BF16)
        du_ref[:, C_VC:C_VC + D] = (dp * _bf(u_ref, C_CC, D)).astype(BF16)

        dyb_b = dyb.astype(BF16)
        dob = _dot_nt(dyb_b, wao_ref[...])
        za = _bf(u_ref, C_ZA, D)
        sga = _sigmoid(za)
        sa = za * sga
        do = dob * sa
        qhat, rq, qn, khat, rk, knb, vband = _attn_inputs(u_ref, ukv_prev, qg_ref, kg_ref, cref, tm)
        bias, bias_first, lane_grp, query_grp = _attn_masks(is_first, cref)
        lane16 = lax.broadcasted_iota(jnp.int32, (1, D), 1)
        dsink_row = jnp.zeros((1, D), F32)
        o_cols, dq_cols, dk4, dv4 = [], [], [], []
        for h in range(N_KV):
            sink = _sink_row(sinks_ref, h, query_grp)
            dsink = jnp.zeros((1, GROUP * BLK), F32)
            o_rows, dq_rows, dk_parts, dv_parts = [], [], [], []
            for b in range(nb):
                rows = slice(b * BLK, (b + 1) * BLK)
                band = slice(b * BLK, (b + 2) * BLK)
                cols = slice(h * KVW, (h + 1) * KVW)
                qs = _stack_groups(qn[rows, cols], lane_grp, h)
                pn_t, ps = _softmax_block(qs, knb[band], bias_first if b == 0 else bias, sink)
                pn_b = pn_t.astype(BF16)
                o_rows.append(_unstack_groups(_dot_tn(pn_b, vband[band]), lane_grp, h))
                dos = _stack_groups(do[rows, cols].astype(BF16), lane_grp, h)
                dpn_t = _dot_nt(vband[band], dos)
                delta = jnp.sum(pn_t * dpn_t, axis=0, keepdims=True)
                ds_t = (pn_t * (dpn_t - delta)).astype(BF16)
                dsink = dsink - ps * delta
                dq_rows.append(_unstack_groups(_dot_tn(ds_t, knb[band]), lane_grp, h))
                dk_parts.append(_dot(ds_t, qs))
                dv_parts.append(_dot(pn_b, dos))
            o_cols.append(jnp.concatenate(o_rows, axis=0))
            dq_cols.append(jnp.concatenate(dq_rows, axis=0))
            dk4.append(_band_sum(dk_parts))
            dv4.append(_band_sum(dv_parts))
            for gi in range(GROUP):
                tot = jnp.sum(dsink[:, gi * BLK:(gi + 1) * BLK], axis=1, keepdims=True)
                dsink_row = dsink_row + jnp.where(lane16 == GROUP * h + gi, tot, 0.0)
        small_ref[SM_SINK:SM_SINK + 1, :] += dsink_row

        o = jnp.concatenate(o_cols, axis=1)
        ob_ref[...] = (o * sa).astype(BF16)
        dyb_ref[...] = dyb_b
        du_ref[:, C_ZA:C_ZA + D] = (dob * o * (sga + sa * (1.0 - sga))).astype(BF16)

        dqn = jnp.concatenate(dq_cols, axis=1)
        small_ref[SM_QG:SM_QG + 1, 0:D] += SCALE * jnp.sum(dqn * qhat, axis=0, keepdims=True)
        dqh = dqn * (qg_ref[...] * LN2)
        dq = rq * (dqh - qhat * _head_mean(dqh * qhat, cref["sel_q"][...], cref["exp_q"][...]))
        du_ref[:, C_Q:C_Q + D] = dq.astype(BF16)

        dkn_band = (dk4[0] + dk4[1] + dk4[2] + dk4[3]) * LN2
        dv_band = dv4[0] + dv4[1] + dv4[2] + dv4[3]
        carried = carry_kv[...]
        pad = jnp.zeros((tm - BLK, KVW), F32)
        if nb > 1:
            dkn = dkn_band[BLK:, :] + jnp.concatenate([pad, carried[:, 0:KVW]], axis=0)
            dv = dv_band[BLK:, :] + jnp.concatenate([pad, carried[:, KVW:2 * KVW]], axis=0)
        else:
            dkn = dkn_band[BLK:, :] + carried[:, 0:KVW]
            dv = dv_band[BLK:, :] + carried[:, KVW:2 * KVW]
        carry_kv[:, 0:KVW] = dkn_band[0:BLK, :]
        carry_kv[:, KVW:2 * KVW] = dv_band[0:BLK, :]
        khat_t = khat[BLK:, :]
        small_ref[SM_KG:SM_KG + 1, 0:KVW] += jnp.sum(dkn * khat_t, axis=0, keepdims=True)
        dkh = dkn * kg_ref[...]
        dk = rk[BLK:, :] * (dkh - khat_t * _head_mean(dkh * khat_t, cref["sel_k"][...], cref["exp_k"][...]))
        du_ref[:, C_K:C_K + KVW] = dk.astype(BF16)
        du_ref[:, C_V:C_V + KVW] = dv.astype(BF16)

    rev = lambda g: n_tiles - 1 - g
    u_spec, ukv_prev, uvc_prev, ucc_prev = _mixer_specs(t, tm, n_tiles, rev)
    tok = pl.BlockSpec((tm, D), lambda g: (rev(g), 0))
    consts = [c[k] for k in cn]
    wspec = _full((D, D))
    return pl.pallas_call(
        body, name=name, grid=(n_tiles,),
        in_specs=[tok, u_spec, ukv_prev, uvc_prev, ucc_prev, tok, tok, _full((8, D)), _full((1, D)), _full((1, KVW)),
                  pl.BlockSpec(memory_space=pltpu.SMEM), _full((1, 2 * D)), wspec, wspec, wspec]
                 + [_full(a.shape) for a in consts],
        out_specs=[pl.BlockSpec((tm, IN_COLS), lambda g: (rev(g), 0)), _full((SM_ROWS, D))] + [tok] * 5,
        out_shape=[jax.ShapeDtypeStruct((t, IN_COLS), BF16), jax.ShapeDtypeStruct((SM_ROWS, D), F32)]
                  + [jax.ShapeDtypeStruct((t, D), BF16)] * 5,
        scratch_shapes=[pltpu.VMEM((BLK, 2 * KVW), F32), pltpu.VMEM((8, D), F32)],
        compiler_params=_cparams(("arbitrary",)),
    )(dout, u, u, u, u, ya, yb, cw, qg, kg, sinks, gb, wco, wao, wout, *consts)


def matmul_tn(a, b, name, after=None):
    t, m = a.shape
    tk = min(1024, t)
    mb = 2176 if m == IN_COLS else m
    nk = t // tk

    def body(a_ref, b_ref, *rest):
        o_ref, acc = rest[-2:]
        k = pl.program_id(1)
        prod = _dot_tn(a_ref[...].astype(BF16), b_ref[...].astype(BF16))

        @pl.when(k == 0)
        def _():
            acc[...] = prod

        @pl.when(k > 0)
        def _():
            acc[...] += prod

        @pl.when(k == nk - 1)
        def _():
            o_ref[...] = acc[...].astype(BF16)

    return pl.pallas_call(
        body, name=name, grid=(m // mb, nk),
        in_specs=[pl.BlockSpec((tk, mb), lambda j, k: (k, j)), pl.BlockSpec((tk, D), lambda j, k: (k, 0))]
                 + ([] if after is None else [ANY]),
        out_specs=pl.BlockSpec((mb, D), lambda j, k: (j, 0)),
        out_shape=jax.ShapeDtypeStruct((m, D), BF16),
        scratch_shapes=[pltpu.VMEM((mb, D), F32)],
        compiler_params=_cparams(("arbitrary", "arbitrary")),
    )(a, b, *([] if after is None else [after]))


def inproj_bwd_x(du, w, x, ng, dout, name):
    t = x.shape[0]
    tm = min(1024, t)
    kc = 2176
    nk = IN_COLS // kc

    def body(du_ref, w_ref, x_ref, ng_ref, dout_ref, dx_ref, dng_ref, acc):
        i = pl.program_id(0)
        k = pl.program_id(1)
        prod = _dot(du_ref[...], w_ref[...])

        @pl.when(k == 0)
        def _():
            acc[...] = prod

        @pl.when(k > 0)
        def _():
            acc[...] += prod

        @pl.when((i == 0) & (k == 0))
        def _():
            dng_ref[...] = jnp.zeros_like(dng_ref)

        @pl.when(k == nk - 1)
        def _():
            dh = acc[...]
            xf = x_ref[...]
            r = lax.rsqrt(jnp.mean(xf * xf, axis=-1, keepdims=True) + EPS)
            xhat = xf * r
            dng_ref[0:1, :] += jnp.sum(dh * xhat, axis=0, keepdims=True)
            dxh = dh * ng_ref[...]
            dx_ref[...] = dout_ref[...] + r * (dxh - xhat * jnp.mean(dxh * xhat, axis=-1, keepdims=True))

    tok = pl.BlockSpec((tm, D), lambda i, k: (i, 0))
    return pl.pallas_call(
        body, name=name, grid=(t // tm, nk),
        in_specs=[pl.BlockSpec((tm, kc), lambda i, k: (i, k)), pl.BlockSpec((kc, D), lambda i, k: (k, 0)), tok,
                  pl.BlockSpec((1, D), lambda i, k: (0, 0)), tok],
        out_specs=[tok, pl.BlockSpec((8, D), lambda i, k: (0, 0))],
        out_shape=[jax.ShapeDtypeStruct((t, D), F32), jax.ShapeDtypeStruct((8, D), F32)],
        scratch_shapes=[pltpu.VMEM((tm, D), F32)],
        compiler_params=_cparams(("arbitrary", "arbitrary")),
    )(du, w, x, ng, dout)


def loss_head(y, target, name):
    t = y.shape[0]
    tm = min(1024, t)

    def body(y_ref, t_ref, dy_ref, loss_ref):
        @pl.when(pl.program_id(0) == 0)
        def _():
            loss_ref[...] = jnp.zeros_like(loss_ref)
        err = y_ref[...] - t_ref[...]
        dy_ref[...] = err * (1.0 / D)
        part = jnp.sum(jnp.sum(err * err, axis=-1, keepdims=True) * (1.0 / D), axis=0, keepdims=True)
        loss_ref[...] += 0.5 * part

    tok = pl.BlockSpec((tm, D), lambda i: (i, 0))
    return pl.pallas_call(
        body, name=name, grid=(t // tm,), in_specs=[tok, tok],
        out_specs=[tok, pl.BlockSpec((8, 128), lambda i: (0, 0))],
        out_shape=[jax.ShapeDtypeStruct((t, D), F32), jax.ShapeDtypeStruct((8, 128), F32)],
        compiler_params=_cparams(("arbitrary",)),
    )(y, target)


def layer_operands(l, norm_g, conv_w_full, q_norm_g, k_norm_g, sinks, gate_b, w_in_b, wco_b, wao_b, wout_b):
    return dict(
        ng=norm_g[l][None, :], cw=jnp.pad(conv_w_full[l], ((0, 5), (0, 0))),
        qg=jnp.tile(q_norm_g[l] * (SCALE * LOG2E), D // HEAD)[None, :], kg=jnp.tile(k_norm_g[l], N_KV)[None, :],
        sinks=sinks[l][None, :], gb=gate_b[l][None, :],
        w_in=w_in_b[l], wco=wco_b[l], wao=wao_b[l], wout=wout_b[l])


def layer_bwd(dout, saved, lw, c, l, send_off):
    x, u, h, ya, yb = saved
    du, small, merged, yc, ob, dya, dyb = mixer_bwd(dout, u, ya, yb, lw["cw"], lw["qg"], lw["kg"], lw["sinks"], lw["gb"],
                                                    lw["wco"], lw["wao"], lw["wout"], c, f"mixer_bwd_{l}")
    grads = dict(w_in=matmul_tn(du, h, f"dw_in_{l}"), small=small)
    token = send_off(grads, False)
    grads["wout"] = matmul_tn(merged, dout, f"dw_out_{l}", after=token)
    grads["wco"] = matmul_tn(yc, dya, f"dw_conv_out_{l}")
    grads["wao"] = matmul_tn(ob, dyb, f"dw_attn_out_{l}")
    token = send_off(grads, True)
    dx, grads["dng"] = inproj_bwd_x(du, lw["w_in"], x, lw["ng"] + token[0:1, 0:1], dout, f"inproj_bwd_{l}")
    return dx, grads


MESH = pl.DeviceIdType.MESH
ANY = pl.BlockSpec(memory_space=pl.ANY)


def _place():
    return lax.axis_index("x"), lax.axis_index("y"), lax.axis_index("c")


def all_gather(arrs, after, name):
    n = len(arrs)

    def body(*refs):
        ins, outs = refs[:n], refs[n + 1:2 * n + 1]
        send_sems, recv_sems, local_sems = refs[2 * n + 1:]
        x, y, c = _place()
        me, sibling = (x, y, c), (x, y, 1 - c)
        chips = [(1 - x, y), (x, 1 - y), (1 - x, 1 - y)]

        def slot(a, block):
            px, py, pc = block
            return outs[a].at[4 * px + 2 * py + pc]

        def copy(a, k, block, to, src=None):
            return pltpu.make_async_remote_copy(
                src_ref=slot(a, block) if src is None else src, dst_ref=slot(a, block),
                send_sem=send_sems.at[a, k], recv_sem=recv_sems.at[a, k], device_id=to, device_id_type=MESH)

        mine = [pltpu.make_async_copy(ins[a], slot(a, me), local_sems.at[a]) for a in range(n)]
        for cp in mine:
            cp.start()
        first = []
        for a in range(n):
            first.append(copy(a, 0, me, sibling, src=ins[a]))
            first += [copy(a, 1 + j, me, (*chip, c), src=ins[a]) for j, chip in enumerate(chips)]
        for cp in first:
            cp.start()
        passed = []
        for j, chip in enumerate(chips):
            for a in range(n):
                copy(a, 1 + j, (*chip, c), me).wait_recv()
                passed.append(copy(a, 4 + j, (*chip, c), sibling))
                passed[-1].start()
        for a in range(n):
            copy(a, 0, sibling, me).wait_recv()
            for j, chip in enumerate(chips):
                copy(a, 4 + j, (*chip, 1 - c), me).wait_recv()
        for cp in first + passed:
            cp.wait_send()
        for cp in mine:
            cp.wait()

    return pl.pallas_call(
        body, name=name, in_specs=[ANY] * (n + 1), out_specs=[ANY] * n,
        out_shape=[jax.ShapeDtypeStruct((N_DEV,) + a.shape, a.dtype) for a in arrs],
        scratch_shapes=[pltpu.SemaphoreType.DMA((n, 7)), pltpu.SemaphoreType.DMA((n, 7)), pltpu.SemaphoreType.DMA((n,))],
    )(*arrs, after)


HBM_SPEC = pl.BlockSpec(memory_space=pltpu.HBM)
SEM_SPEC = pl.BlockSpec(memory_space=pltpu.SEMAPHORE)
EFFECT = pltpu.SideEffectType.DATAFLOW_SIDE_EFFECTING


ALL_PEERS = (1, 2, 3, 4, 5, 6, 7)


def _flip(place, k):
    x, y, c = place
    return (1 - x if k & 4 else x, 1 - y if k & 2 else y, 1 - c if k & 1 else c)


def _slot(place):
    return 4 * place[0] + 2 * place[1] + place[2]


def _exchange_copies(ins, lands, send_sems, recv_sems, scatter, flips, arriving):
    me = _place()
    out = []
    for a in range(len(ins)):
        for i, k in enumerate(flips):
            peer = _flip(me, k)
            out.append(pltpu.make_async_remote_copy(
                src_ref=ins[a].at[_slot(peer)] if scatter else ins[a],
                dst_ref=lands[a].at[_slot(peer) if arriving else _slot(me)],
                send_sem=send_sems.at[a * len(flips) + i], recv_sem=recv_sems.at[a * len(flips) + i],
                device_id=peer, device_id_type=MESH))
    return out


def exchange_start(arrs, scatter, flips, after, name):
    n = len(arrs)
    lands = [lax.empty(a.shape if scatter else (N_DEV,) + a.shape, a.dtype) for a in arrs]

    def body(*refs):
        ins, lz = refs[:n], refs[n:2 * n]
        send_sems, recv_sems = refs[2 * n + 1], refs[2 * n + 2]
        token = refs[-1]
        for cp in _exchange_copies(ins, lz, send_sems, recv_sems, scatter, flips, False):
            cp.start()
        token[...] = jnp.zeros_like(token)

    sems = pltpu.SemaphoreType.DMA((n * len(flips),))
    res = pl.pallas_call(
        body, name=name,
        out_shape=(sems, sems, *[pltpu.HBM(a.shape, a.dtype) for a in arrs], *[pltpu.HBM(a.shape, a.dtype) for a in lands],
                   jax.ShapeDtypeStruct((8, 128), F32)),
        in_specs=[HBM_SPEC] * (2 * n) + [ANY],
        out_specs=(SEM_SPEC, SEM_SPEC, *[HBM_SPEC] * (2 * n), pl.BlockSpec(memory_space=pltpu.VMEM)),
        input_output_aliases={i: 2 + i for i in range(2 * n)},
        compiler_params=pltpu.CompilerParams(has_side_effects=EFFECT),
    )(*[pltpu.with_memory_space_constraint(a, pltpu.HBM) for a in arrs],
      *[pltpu.with_memory_space_constraint(a, pltpu.HBM) for a in lands], after)
    return dict(send=res[0], recv=res[1], srcs=res[2:2 + n], lands=res[2 + n:2 + 2 * n], token=res[-1], scatter=scatter,
                flips=flips)


def exchange_wait(state, after, name):
    n = len(state["srcs"])

    def body(*refs):
        ins, lz = refs[:n], refs[n:2 * n]
        send_sems, recv_sems = refs[2 * n], refs[2 * n + 1]
        for cp in _exchange_copies(ins, lz, send_sems, recv_sems, state["scatter"], state["flips"], True):
            cp.wait_send()
            cp.wait_recv()

    both = list(state["srcs"]) + list(state["lands"])
    res = pl.pallas_call(
        body, name=name, out_shape=tuple(pltpu.HBM(a.shape, a.dtype) for a in both),
        in_specs=[HBM_SPEC] * (2 * n) + [SEM_SPEC, SEM_SPEC, ANY], out_specs=tuple([HBM_SPEC] * (2 * n)),
        input_output_aliases={i: i for i in range(2 * n)},
        compiler_params=pltpu.CompilerParams(has_side_effects=EFFECT),
    )(*both, state["send"], state["recv"], after)
    return res[:n], res[n:]


OTHER_CHIPS = (2, 4, 6)
SAME_CORE = (1, 2, 4, 6)


def _forward_copies(lands, send_sems, recv_sems, arriving):
    me = _place()
    sibling = _flip(me, 1)
    origin = sibling if arriving else me
    return [pltpu.make_async_remote_copy(
        src_ref=lands[a].at[_slot(_flip(origin, k))], dst_ref=lands[a].at[_slot(_flip(origin, k))],
        send_sem=send_sems.at[a * len(OTHER_CHIPS) + i], recv_sem=recv_sems.at[a * len(OTHER_CHIPS) + i],
        device_id=sibling, device_id_type=MESH) for a in range(len(lands)) for i, k in enumerate(OTHER_CHIPS)]


def forward_start(lands, after, name):
    n = len(lands)

    def body(*refs):
        lz = refs[:n]
        send_sems, recv_sems = refs[n + 1], refs[n + 2]
        token = refs[-1]
        for cp in _forward_copies(lz, send_sems, recv_sems, False):
            cp.start()
        token[...] = jnp.zeros_like(token)

    sems = pltpu.SemaphoreType.DMA((n * len(OTHER_CHIPS),))
    res = pl.pallas_call(
        body, name=name,
        out_shape=(sems, sems, *[pltpu.HBM(a.shape, a.dtype) for a in lands], jax.ShapeDtypeStruct((8, 128), F32)),
        in_specs=[HBM_SPEC] * n + [ANY],
        out_specs=(SEM_SPEC, SEM_SPEC, *[HBM_SPEC] * n, pl.BlockSpec(memory_space=pltpu.VMEM)),
        input_output_aliases={i: 2 + i for i in range(n)},
        compiler_params=pltpu.CompilerParams(has_side_effects=EFFECT),
    )(*[pltpu.with_memory_space_constraint(a, pltpu.HBM) for a in lands], after)
    return dict(send=res[0], recv=res[1], lands=res[2:2 + n], token=res[-1])


def forward_wait(state, after, name):
    n = len(state["lands"])

    def body(*refs):
        lz = refs[:n]
        send_sems, recv_sems = refs[n], refs[n + 1]
        for cp in _forward_copies(lz, send_sems, recv_sems, False):
            cp.wait_send()
        for cp in _forward_copies(lz, send_sems, recv_sems, True):
            cp.wait_recv()

    return pl.pallas_call(
        body, name=name, out_shape=tuple(pltpu.HBM(a.shape, a.dtype) for a in state["lands"]),
        in_specs=[HBM_SPEC] * n + [SEM_SPEC, SEM_SPEC, ANY], out_specs=tuple([HBM_SPEC] * n),
        input_output_aliases={i: i for i in range(n)},
        compiler_params=pltpu.CompilerParams(has_side_effects=EFFECT),
    )(*state["lands"], state["send"], state["recv"], after)


def adamw(w, m, v, parts, name):
    r, cdim = w.shape
    n_parts = parts.shape[0]
    rb = 256 if r % 256 == 0 else (SHARD_COLS // 4 if r == SHARD_COLS else r)

    def body(w_ref, m_ref, v_ref, p_ref, g_ref, d_ref, mo_ref, vo_ref):
        g = p_ref[0].astype(F32)
        for i in range(1, n_parts):
            g = g + p_ref[i].astype(F32)
        m_new = ADAM_B1 * m_ref[...] + (1.0 - ADAM_B1) * g
        v_new = ADAM_B2 * v_ref[...] + (1.0 - ADAM_B2) * (g * g)
        m_hat = m_new / (1.0 - ADAM_B1 ** ADAM_STEP)
        v_hat = v_new / (1.0 - ADAM_B2 ** ADAM_STEP)
        g_ref[...] = g
        d_ref[...] = -ADAM_LR * (m_hat / (jnp.sqrt(v_hat) + ADAM_EPS) + ADAM_WD * w_ref[...])
        mo_ref[...] = m_new
        vo_ref[...] = v_new

    blk = pl.BlockSpec((rb, cdim), lambda i: (i, 0))
    return pl.pallas_call(
        body, name=name, grid=(r // rb,),
        in_specs=[blk, blk, blk, pl.BlockSpec((n_parts, rb, cdim), lambda i: (0, i, 0))],
        out_specs=[blk] * 4, out_shape=[jax.ShapeDtypeStruct((r, cdim), F32)] * 4,
        compiler_params=_cparams(("arbitrary",)),
    )(w, m, v, parts)


def adamw_layers(w, m, v, lands, srcs, me, lo, prev, name):
    nl_all, r, cdim = w.shape
    nl = len(lands)
    rb = 256 if r % 256 == 0 else (SHARD_COLS // 4 if r == SHARD_COLS else r)
    nblk = r // rb
    n_prev = 0 if prev is None else 4

    def body(me_ref, w_ref, m_ref, v_ref, *rest):
        land_refs, src_refs = rest[:nl], rest[nl:2 * nl]
        g_ref, d_ref, mo_ref, vo_ref = rest[2 * nl + n_prev:]
        for k in range(nl):
            @pl.when(pl.program_id(0) == k)
            def _(k=k):
                own = src_refs[k][...].astype(F32)
                g = jnp.where(me_ref[0] == 0, own, land_refs[k][0].astype(F32))
                for i in range(1, N_DEV):
                    g = g + jnp.where(me_ref[0] == i, own, land_refs[k][i].astype(F32))
                m_new = ADAM_B1 * m_ref[...] + (1.0 - ADAM_B1) * g
                v_new = ADAM_B2 * v_ref[...] + (1.0 - ADAM_B2) * (g * g)
                m_hat = m_new / (1.0 - ADAM_B1 ** ADAM_STEP)
                v_hat = v_new / (1.0 - ADAM_B2 ** ADAM_STEP)
                g_ref[...] = g
                d_ref[...] = -ADAM_LR * (m_hat / (jnp.sqrt(v_hat) + ADAM_EPS) + ADAM_WD * w_ref[...])
                mo_ref[...] = m_new
                vo_ref[...] = v_new

    blk = pl.BlockSpec((None, rb, cdim), lambda l, i, me_ref: (lo + l, i, 0))

    def rows(l, i, k):
        return jnp.where(l < k, 0, jnp.where(l == k, i, nblk - 1))

    land_specs = [pl.BlockSpec((N_DEV, rb, cdim), lambda l, i, me_ref, k=k: (0, rows(l, i, k), 0)) for k in range(nl)]
    src_specs = [pl.BlockSpec((None, rb, cdim), lambda l, i, me_ref, k=k: (me_ref[0], rows(l, i, k), 0)) for k in range(nl)]
    return pl.pallas_call(
        body, name=name,
        grid_spec=pltpu.PrefetchScalarGridSpec(
            num_scalar_prefetch=1, grid=(nl, nblk),
            in_specs=[blk, blk, blk] + land_specs + src_specs + [ANY] * n_prev, out_specs=[blk] * 4),
        out_shape=[jax.ShapeDtypeStruct((nl_all, r, cdim), F32)] * 4,
        input_output_aliases={4 + 2 * nl + j: j for j in range(n_prev)},
        compiler_params=_cparams(("arbitrary", "arbitrary")),
    )(me, w, m, v, *lands, *srcs, *([] if prev is None else prev))


def small_sum(parts, fold, name):
    rows = parts.shape[1]

    def dot3(xv, sel):
        out = jnp.zeros((xv.shape[0], sel.shape[1]), F32)
        for _ in range(3):
            hi = xv.astype(BF16)
            out = out + _dot(hi, sel)
            xv = xv - hi.astype(F32)
        return out

    def body(p_ref, fold_ref, o_ref):
        tot = p_ref[0]
        for i in range(1, N_DEV):
            tot = tot + p_ref[i]
        o_ref[...] = tot
        for l in range(rows // SM_ROWS):
            blk = tot[l * SM_ROWS:l * SM_ROWS + 8, 0:D]
            folded = dot3(blk, fold_ref[...])
            o_ref[l * SM_ROWS + SM_QG_FOLDED:l * SM_ROWS + SM_QG_FOLDED + 1, 0:128] = folded[SM_QG:SM_QG + 1, :]
            o_ref[l * SM_ROWS + SM_KG_FOLDED:l * SM_ROWS + SM_KG_FOLDED + 1, 0:128] = folded[SM_KG:SM_KG + 1, :]

    return pl.pallas_call(
        body, name=name, out_shape=jax.ShapeDtypeStruct((rows, D), F32),
        compiler_params=_cparams(None),
    )(parts, fold)


def kernel(x, norm_g, w_in, conv_w, q_norm_g, k_norm_g, sinks, w_conv_out, w_attn_out, gate_b, w_out, loss_target, m_norm_g, m_w_in, m_conv_w, m_q_norm_g, m_k_norm_g, m_sinks, m_w_conv_out, m_w_attn_out, m_gate_b, m_w_out, v_norm_g, v_w_in, v_conv_w, v_q_norm_g, v_k_norm_g, v_sinks, v_w_conv_out, v_w_attn_out, v_gate_b, v_w_out):
    c = _selectors()
    me = 4 * lax.axis_index("x") + 2 * lax.axis_index("y") + lax.axis_index("c")

    w_in_t, m_w_in_t, v_w_in_t = (jnp.swapaxes(a, 1, 2) for a in (w_in, m_w_in, v_w_in))

    def shards(l):
        return [w_in_t[l].astype(BF16), w_conv_out[l].astype(BF16), w_attn_out[l].astype(BF16), w_out[l].astype(BF16)]

    def ici_start(l, after):
        return exchange_start(shards(l), False, SAME_CORE, after, f"gather_start_{l}")

    h = x[0]
    saved, lws = [], []
    lands = all_gather(shards(0) + [conv_w], h, "gather_0")
    conv_full = jnp.transpose(lands[4], (1, 2, 0, 3)).reshape(DEPTH, 3, D)
    ici = ici_start(1, lands[1])
    for l in range(DEPTH):
        lws.append(layer_operands(l, norm_g, conv_full, q_norm_g, k_norm_g, sinks, gate_b,
                                  {l: lands[0].reshape(IN_COLS, D)}, {l: lands[1].reshape(D, D)},
                                  {l: lands[2].reshape(D, D)}, {l: lands[3].reshape(D, D)}))
        ng_l = lws[l]["ng"] + ici["token"][0:1, 0:1] if l == 0 else lws[l]["ng"]
        u, hb = inproj_fwd(h, ng_l, lws[l]["w_in"], f"inproj_fwd_{l}")
        gb_l = lws[l]["gb"]
        if l + 1 < DEPTH:
            mine, arrived = exchange_wait(ici, u, f"gather_wait_{l + 1}")
            chip = forward_start(arrived, mine[0], f"gather_forward_start_{l + 1}")
            started = chip["token"]
            if l + 2 < DEPTH:
                ici = ici_start(l + 2, started)
                started = ici["token"]
            gb_l = gb_l + started[0:1, 0:1]
        x_in = h
        h, ya, yb = mixer_fwd(x_in, u, lws[l]["cw"], lws[l]["qg"], lws[l]["kg"], lws[l]["sinks"], gb_l,
                              lws[l]["wco"], lws[l]["wao"], lws[l]["wout"], c, f"mixer_fwd_{l}")
        saved.append((x_in, u, hb, ya, yb))
        if l + 1 < DEPTH:
            lands = forward_wait(chip, h, f"gather_forward_wait_{l + 1}")
            lands = [lax.dynamic_update_index_in_dim(land, src, me, 0) for land, src in zip(lands, mine)]
    dh, loss_part = loss_head(h, loss_target[0], "loss_head")

    grads, scatters = [None] * DEPTH, [[] for _ in range(DEPTH)]
    for l in reversed(range(DEPTH)):
        def send_off(g, done, l=l):
            rest = [g[k].reshape(N_DEV, SHARD_ROWS, D) for k in ("wco", "wao", "wout")] if done else []
            first = [g["w_in"].reshape(N_DEV, SHARD_COLS, D)] if done == (l > 0) else []
            if not first + rest:
                return None
            tag = f"{l}" if l > 0 else ("0_rest" if done else "0_in")
            scatters[l].append(exchange_start(first + rest, True, ALL_PEERS, g["small"], f"scatter_start_{tag}"))
            return scatters[l][-1]["token"]
        dh, grads[l] = layer_bwd(dh, saved[l], lws[l], c, l, send_off)

    me1 = me.astype(jnp.int32).reshape(1)
    mine, lands = {}, {}
    for l in (3, 2, 1):
        mine[l], lands[l] = exchange_wait(scatters[l][0], dh, f"scatter_wait_{l}")
    weights = [(w_in_t, m_w_in_t, v_w_in_t, "w_in"), (w_conv_out, m_w_conv_out, v_w_conv_out, "w_conv_out"),
               (w_attn_out, m_w_attn_out, v_w_attn_out, "w_attn_out"), (w_out, m_w_out, v_w_out, "w_out")]
    upd = [adamw_layers(w, m, v, [lands[l][i] for l in (1, 2, 3)], [mine[l][i] for l in (1, 2, 3)], me1, 1, None,
                        f"adamw_{n}_upper") for i, (w, m, v, n) in enumerate(weights)]
    blocks = []
    for l in range(DEPTH):
        blk = grads[l]["small"]
        blk = blk.at[SM_NORM, 0:D].set(grads[l]["dng"][0])
        if l == 0:
            blk = blk.at[SM_LOSS, 0:128].set(loss_part[0])
        blocks.append(blk)
    small_x = exchange_start([jnp.concatenate(blocks, axis=0)], False, ALL_PEERS, grads[0]["dng"], "gather_small_start")

    m_in, l_in = exchange_wait(scatters[0][0], upd[0][0], "scatter_wait_0_in")
    upd[0] = adamw_layers(*weights[0][:3], [l_in[0]], [m_in[0]], me1, 0, upd[0], "adamw_w_in_0")
    m_rest, l_rest = exchange_wait(scatters[0][1], upd[0][0], "scatter_wait_0_rest")
    for i in (1, 2, 3):
        upd[i] = adamw_layers(*weights[i][:3], [l_rest[i - 1]], [m_rest[i - 1]], me1, 0, upd[i], f"adamw_{weights[i][3]}_0")
    u_in = [jnp.swapaxes(o, 1, 2) for o in upd[0]]
    u_co, u_ao, u_out = upd[1], upd[2], upd[3]

    mine_s, lands_s = exchange_wait(small_x, u_out[0], "gather_small_wait")
    gathered = lax.dynamic_update_index_in_dim(lands_s[0], mine_s[0], me, 0)
    tot = small_sum(gathered, c["fold"], "small_sum")
    tot = tot.reshape(DEPTH, SM_ROWS, D)
    loss = tot[0, SM_LOSS, 0]

    def update_small(w, m, v, g, name):
        return adamw(w, m, v, g[None], name)

    u_ng = update_small(norm_g, m_norm_g, v_norm_g, tot[:, SM_NORM, 0:D], "adamw_norm_g")
    u_qg = update_small(q_norm_g, m_q_norm_g, v_q_norm_g, tot[:, SM_QG_FOLDED, 0:HEAD], "adamw_q_norm_g")
    u_kg = update_small(k_norm_g, m_k_norm_g, v_k_norm_g, tot[:, SM_KG_FOLDED, 0:HEAD], "adamw_k_norm_g")
    u_sk = update_small(sinks, m_sinks, v_sinks, tot[:, SM_SINK, 0:16], "adamw_sinks")
    g_gate = jnp.concatenate([tot[:, SM_GATE, :], tot[:, SM_GATE_B, :]], axis=1)
    u_gb = update_small(gate_b, m_gate_b, v_gate_b, g_gate, "adamw_gate_b")
    g_conv = lax.dynamic_slice_in_dim(tot[:, SM_CONV:SM_CONV + 3, 0:D], me * SHARD_ROWS, SHARD_ROWS, axis=2)
    u_cw = [o.reshape(DEPTH, 3, SHARD_ROWS) for o in update_small(
        conv_w.reshape(DEPTH * 3, SHARD_ROWS), m_conv_w.reshape(DEPTH * 3, SHARD_ROWS),
        v_conv_w.reshape(DEPTH * 3, SHARD_ROWS), g_conv.reshape(DEPTH * 3, SHARD_ROWS), "adamw_conv_w")]

    order = [u_ng, u_in, u_cw, u_qg, u_kg, u_sk, u_co, u_ao, u_gb, u_out]
    return (loss, dh[None], *[u[0] for u in order], *[u[1] for u in order], *[u[2] for u in order], *[u[3] for u in order])
```

```python
import functools

import jax
import jax.numpy as jnp
from jax import lax
from jax.experimental import pallas as pl
from jax.experimental.pallas import tpu as pltpu

F32 = jnp.float32
BF16 = jnp.bfloat16

N_DEV = 8
DEPTH = 4
D = 1024
N_KV = 4
GROUP = 4
HEAD = 64
BLK = 128
KVW = N_KV * HEAD
IN_COLS = 8704
SHARD_COLS = IN_COLS // N_DEV
SHARD_ROWS = D // N_DEV
C_VC, C_BC, C_CC, C_ZC, C_Q, C_K, C_V, C_ZA, C_GA, C_GB = 0, 1024, 2048, 3072, 4096, 5120, 5376, 5632, 6656, 7680
EPS = 1e-6
NEG_INF = -1e30
SCALE = HEAD ** -0.5
LOG2E = 1.4426950408889634
LN2 = 0.6931471805599453

ADAM_LR = 0.001
ADAM_B1 = 0.9
ADAM_B2 = 0.999
ADAM_EPS = 1e-08
ADAM_WD = 0.01
ADAM_STEP = 10

VMEM_LIMIT = 60 * 1024 * 1024
SM_ROWS = 16
SM_GATE, SM_CONV, SM_QG, SM_KG, SM_SINK, SM_NORM, SM_LOSS, SM_GATE_B, SM_QG_FOLDED, SM_KG_FOLDED = 0, 1, 4, 5, 6, 7, 8, 9, 10, 11


def _cparams(sem):
    return pltpu.CompilerParams(dimension_semantics=sem, vmem_limit_bytes=VMEM_LIMIT)


def _dot(a, b):
    return jnp.dot(a, b, preferred_element_type=F32)


def _dot_nt(a, b):
    return lax.dot_general(a, b, (((1,), (1,)), ((), ())), preferred_element_type=F32)


def _dot_tn(a, b):
    return lax.dot_general(a, b, (((0,), (0,)), ((), ())), preferred_element_type=F32)


def _dot2(x, sel):
    hi = x.astype(BF16)
    lo = (x - hi.astype(F32)).astype(BF16)
    return _dot(hi, sel) + _dot(lo, sel)


def _sigmoid(z):
    return 0.5 * jnp.tanh(0.5 * z) + 0.5


def _head_mean(t, sel, exp):
    return _dot2(_dot(t.astype(BF16), sel) * (1.0 / HEAD), exp)


def _shift_down(a, k, before):
    r = pltpu.roll(a, k, 0)
    row = lax.broadcasted_iota(jnp.int32, (8, 1), 0)
    head = jnp.where(row < k, pltpu.roll(before, k, 0), r[0:8, :])
    return jnp.concatenate([head, r[8:, :]], axis=0)


def _shift_up(a, k, after):
    n = a.shape[0]
    r = pltpu.roll(a, n - k, 0)
    row = lax.broadcasted_iota(jnp.int32, (8, 1), 0)
    tail = jnp.where(row >= 8 - k, pltpu.roll(after, 8 - k, 0), r[n - 8:, :])
    return jnp.concatenate([r[:n - 8, :], tail], axis=0)


def _bf(ref, c0, width):
    return ref[:, c0:c0 + width].astype(F32)


def _selectors():
    c = jnp.arange(D)
    sel_q = (c[:, None] // HEAD == jnp.arange(128)[None, :]).astype(BF16)
    ck = jnp.arange(KVW)
    sel_k = (ck[:, None] // HEAD == jnp.arange(128)[None, :]).astype(BF16)
    fold = (c[:, None] % HEAD == jnp.arange(128)[None, :]).astype(BF16)
    qq = jnp.arange(GROUP * BLK)[:, None] % BLK
    kk = jnp.arange(2 * BLK)[None, :]
    valid = (kk > qq) & (kk <= qq + BLK)
    bias = jnp.where(valid, 0.0, NEG_INF).astype(F32).T
    bias_first = jnp.where(valid & (kk >= BLK), 0.0, NEG_INF).astype(F32).T
    return dict(bias=bias, bias_first=bias_first,sel_q=sel_q, exp_q=sel_q.T, sel_k=sel_k, exp_k=sel_k.T, fold=fold)


def inproj_fwd(x, ng, w, name):
    t = x.shape[0]
    tm = min(1024, t)
    cb = 2176
    def body(x_ref, ng_ref, w_ref, u_ref, h_ref, h_scr):
        @pl.when(pl.program_id(1) == 0)
        def _():
            xf = x_ref[...]
            r = lax.rsqrt(jnp.mean(xf * xf, axis=-1, keepdims=True) + EPS)
            hb = (xf * r * ng_ref[...]).astype(BF16)
            h_scr[...] = hb
            h_ref[...] = hb
        u_ref[...] = _dot_nt(h_scr[...], w_ref[...]).astype(BF16)

    return pl.pallas_call(
        body, name=name, grid=(t // tm, IN_COLS // cb),
        in_specs=[pl.BlockSpec((tm, D), lambda i, j: (i, 0)), pl.BlockSpec((1, D), lambda i, j: (0, 0)),
                  pl.BlockSpec((cb, D), lambda i, j: (j, 0))],
        out_specs=[pl.BlockSpec((tm, cb), lambda i, j: (i, j)), pl.BlockSpec((tm, D), lambda i, j: (i, 0))],
        out_shape=[jax.ShapeDtypeStruct((t, IN_COLS), BF16), jax.ShapeDtypeStruct((t, D), BF16)],
        scratch_shapes=[pltpu.VMEM((tm, D), BF16)],
        compiler_params=_cparams(("arbitrary", "arbitrary")),
    )(x, ng, w)


def _conv_fwd(u_ref, uvc_prev, ucc_prev, cw_ref, is_first, tm):
    p = _bf(u_ref, C_CC, D) * _bf(u_ref, C_VC, D)
    pprev = ucc_prev[...].astype(F32) * uvc_prev[...].astype(F32)
    pprev = jnp.where(is_first, 0.0, pprev)
    p1 = _shift_down(p, 1, pprev[8:16, :])
    p2 = _shift_down(p, 2, pprev[8:16, :])
    cw = cw_ref[...]
    conv = cw[0:1, :] * p2 + cw[1:2, :] * p1 + cw[2:3, :] * p
    return p, p1, p2, conv


def _attn_inputs(u_ref, ukv_prev, qg_ref, kg_ref, c, tm):
    q = _bf(u_ref, C_Q, D)
    rq = lax.rsqrt(_head_mean(q * q, c["sel_q"][...], c["exp_q"][...]) + EPS)
    qhat = q * rq
    qn = (qhat * qg_ref[...]).astype(BF16)
    kband = jnp.concatenate([ukv_prev[:, 0:KVW].astype(F32), _bf(u_ref, C_K, KVW)], axis=0)
    rk = lax.rsqrt(_head_mean(kband * kband, c["sel_k"][...], c["exp_k"][...]) + EPS)
    khat = kband * rk
    knb = (khat * kg_ref[...]).astype(BF16)
    vband = jnp.concatenate([ukv_prev[:, KVW:2 * KVW], u_ref[:, C_V:C_V + KVW]], axis=0)
    return qhat, rq, qn, khat, rk, knb, vband


def _attn_masks(is_first, c):
    bias = c["bias"][...]
    bias_first = jnp.where(is_first, c["bias_first"][...], bias)
    lane_grp = lax.broadcasted_iota(jnp.int32, (BLK, KVW), 1) // HEAD
    query_grp = lax.broadcasted_iota(jnp.int32, (1, GROUP * BLK), 1) // BLK
    return bias, bias_first, lane_grp, query_grp


def _sink_row(sinks_ref, h, query_grp):
    row = jnp.full(query_grp.shape, sinks_ref[0, GROUP * h] * LOG2E, F32)
    for gi in range(1, GROUP):
        row = jnp.where(query_grp == gi, sinks_ref[0, GROUP * h + gi] * LOG2E, row)
    return row


def _lane_shift(a, groups):
    shift = (HEAD * groups) % KVW
    return a if shift == 0 else pltpu.roll(a, shift, 1)


def _stack_groups(a256, lane_grp, h):
    zero = jnp.zeros_like(a256)
    return jnp.concatenate([jnp.where(lane_grp == h, _lane_shift(a256, h - gi), zero) for gi in range(GROUP)], axis=0)


def _unstack_groups(a4, lane_grp, h):
    out = _lane_shift(a4[(GROUP - 1) * BLK:, :], GROUP - 1 - h)
    for gi in reversed(range(GROUP - 1)):
        out = jnp.where(lane_grp == gi, _lane_shift(a4[gi * BLK:(gi + 1) * BLK], gi - h), out)
    return out


def _band_sum(parts):
    pieces = [parts[0][0:BLK, :]]
    for b in range(1, len(parts)):
        pieces.append(parts[b - 1][BLK:, :] + parts[b][0:BLK, :])
    pieces.append(parts[-1][BLK:, :])
    return jnp.concatenate(pieces, axis=0)


def _softmax_block(qs, kt_b, bias, sink):
    s = _dot_nt(kt_b, qs) + bias
    m = jnp.maximum(jnp.max(s, axis=0, keepdims=True), sink)
    e = jnp.exp2(s - m)
    es = jnp.exp2(sink - m)
    inv = 1.0 / (jnp.sum(e, axis=0, keepdims=True) + es)
    return e * inv, es * inv


def _mixer_specs(t, tm, n_tiles, tile_of):
    nb = tm // BLK
    u_spec = pl.BlockSpec((tm, IN_COLS), lambda g: (tile_of(g), 0))
    ukv_prev = pl.BlockSpec((BLK, 2 * KVW), lambda g: (jnp.maximum(tile_of(g) * nb - 1, 0), C_K // (2 * KVW)))
    uvc_prev = pl.BlockSpec((16, D), lambda g: (jnp.maximum(tile_of(g) * (tm // 16) - 1, 0), C_VC // D))
    ucc_prev = pl.BlockSpec((16, D), lambda g: (jnp.maximum(tile_of(g) * (tm // 16) - 1, 0), C_CC // D))
    return u_spec, ukv_prev, uvc_prev, ucc_prev


def _full(shape):
    n = len(shape)
    return pl.BlockSpec(shape, lambda g: (0,) * n)


def mixer_fwd(x, u, cw, qg, kg, sinks, gb, wco, wao, wout, c, name):
    t = x.shape[0]
    tm = min(512, t)
    n_tiles = t // tm
    nb = tm // BLK
    cn = sorted(c)

    def body(x_ref, u_ref, ukv_prev, uvc_prev, ucc_prev, cw_ref, qg_ref, kg_ref, sinks_ref, gb_ref, wco_ref, wao_ref,
             wout_ref, *rest):
        cref = dict(zip(cn, rest[:len(cn)]))
        xo_ref, ya_ref, yb_ref = rest[len(cn):]
        is_first = pl.program_id(0) == 0
        _, _, _, conv = _conv_fwd(u_ref, uvc_prev, ucc_prev, cw_ref, is_first, tm)
        zc = _bf(u_ref, C_ZC, D)
        yc = _bf(u_ref, C_BC, D) * conv * (zc * _sigmoid(zc))
        ya = _dot(yc.astype(BF16), wco_ref[...])

        _, _, qn, _, _, knb, vband = _attn_inputs(u_ref, ukv_prev, qg_ref, kg_ref, cref, tm)
        bias, bias_first, lane_grp, query_grp = _attn_masks(is_first, cref)
        o_cols = []
        for h in range(N_KV):
            sink = _sink_row(sinks_ref, h, query_grp)
            o_rows = []
            for b in range(nb):
                qs = _stack_groups(qn[b * BLK:(b + 1) * BLK, h * KVW:(h + 1) * KVW], lane_grp, h)
                pn_t, _ = _softmax_block(qs, knb[b * BLK:(b + 2) * BLK], bias_first if b == 0 else bias, sink)
                o4 = _dot_tn(pn_t.astype(BF16), vband[b * BLK:(b + 2) * BLK])
                o_rows.append(_unstack_groups(o4, lane_grp, h))
            o_cols.append(jnp.concatenate(o_rows, axis=0))
        za = _bf(u_ref, C_ZA, D)
        ob = jnp.concatenate(o_cols, axis=1) * (za * _sigmoid(za))
        yb = _dot(ob.astype(BF16), wao_ref[...])

        g_a = _sigmoid(_bf(u_ref, C_GA, D) + gb_ref[:, 0:D])
        g_b = _sigmoid(_bf(u_ref, C_GB, D) + gb_ref[:, D:2 * D])
        merged = g_a * ya + g_b * yb
        xo_ref[...] = x_ref[...] + _dot(merged.astype(BF16), wout_ref[...])
        ya_ref[...] = ya.astype(BF16)
        yb_ref[...] = yb.astype(BF16)

    u_spec, ukv_prev, uvc_prev, ucc_prev = _mixer_specs(t, tm, n_tiles, lambda g: g)
    tok = pl.BlockSpec((tm, D), lambda g: (g, 0))
    consts = [c[k] for k in cn]
    return pl.pallas_call(
        body, name=name, grid=(n_tiles,),
        in_specs=[tok, u_spec, ukv_prev, uvc_prev, ucc_prev, _full((8, D)), _full((1, D)), _full((1, KVW)),
                  pl.BlockSpec(memory_space=pltpu.SMEM), _full((1, 2 * D)), _full((D, D)), _full((D, D)), _full((D, D))]
                 + [_full(a.shape) for a in consts],
        out_specs=[tok, tok, tok],
        out_shape=[jax.ShapeDtypeStruct((t, D), F32)] + [jax.ShapeDtypeStruct((t, D), BF16)] * 2,
        compiler_params=_cparams(("arbitrary",)),
    )(x, u, u, u, u, cw, qg, kg, sinks, gb, wco, wao, wout, *consts)


def mixer_bwd(dout, u, ya, yb, cw, qg, kg, sinks, gb, wco, wao, wout, c, name):
    t = dout.shape[0]
    tm = min(256, t)
    n_tiles = t // tm
    nb = tm // BLK
    kb = tm + BLK
    cn = sorted(c)

    def body(dout_ref, u_ref, ukv_prev, uvc_prev, ucc_prev, ya_ref, yb_ref, cw_ref, qg_ref, kg_ref, sinks_ref, gb_ref,
             wco_ref, wao_ref, wout_ref, *rest):
        cref = dict(zip(cn, rest[:len(cn)]))
        (du_ref, small_ref, merged_ref, yc_ref, ob_ref, dya_ref, dyb_ref, carry_kv, carry_conv) = rest[len(cn):]
        g = pl.program_id(0)
        is_first = g == n_tiles - 1

        @pl.when(g == 0)
        def _():
            carry_kv[...] = jnp.zeros_like(carry_kv)
            carry_conv[...] = jnp.zeros_like(carry_conv)
            small_ref[...] = jnp.zeros_like(small_ref)

        dout = dout_ref[...]
        dout_b = dout.astype(BF16)
        ya_v = ya_ref[...].astype(F32)
        yb_v = yb_ref[...].astype(F32)
        g_a = _sigmoid(_bf(u_ref, C_GA, D) + gb_ref[:, 0:D])
        g_b = _sigmoid(_bf(u_ref, C_GB, D) + gb_ref[:, D:2 * D])
        merged = g_a * ya_v + g_b * yb_v
        dmerged = _dot_nt(dout_b, wout_ref[...])
        merged_ref[...] = merged.astype(BF16)
        dya = dmerged * g_a
        dyb = dmerged * g_b
        dgl_a = dya * ya_v * (1.0 - g_a)
        dgl_b = dyb * yb_v * (1.0 - g_b)
        du_ref[:, C_GA:C_GA + D] = dgl_a.astype(BF16)
        du_ref[:, C_GB:C_GB + D] = dgl_b.astype(BF16)
        small_ref[SM_GATE:SM_GATE + 1, 0:D] += jnp.sum(dgl_a, axis=0, keepdims=True)
        small_ref[SM_GATE_B:SM_GATE_B + 1, 0:D] += jnp.sum(dgl_b, axis=0, keepdims=True)

        p, p1, p2, conv = _conv_fwd(u_ref, uvc_prev, ucc_prev, cw_ref, is_first, tm)
        zc = _bf(u_ref, C_ZC, D)
        bc = _bf(u_ref, C_BC, D)
        sg = _sigmoid(zc)
        sc = zc * sg
        bconv = bc * conv
        dya_b = dya.astype(BF16)
        yc_ref[...] = (bconv * sc).astype(BF16)
        dya_ref[...] = dya_b
        dyc = _dot_nt(dya_b, wco_ref[...])
        dyc_s = dyc * sc
        du_ref[:, C_BC:C_BC + D] = (dyc_s * conv).astype(BF16)
        du_ref[:, C_ZC:C_ZC + D] = (dyc * bconv * (sg + sc * (1.0 - sg))).astype(BF16)
        dconv = dyc_s * bc
        small_ref[SM_CONV + 2:SM_CONV + 3, 0:D] += jnp.sum(dconv * p, axis=0, keepdims=True)
        small_ref[SM_CONV + 1:SM_CONV + 2, 0:D] += jnp.sum(dconv * p1, axis=0, keepdims=True)
        small_ref[SM_CONV:SM_CONV + 1, 0:D] += jnp.sum(dconv * p2, axis=0, keepdims=True)
        nxt = carry_conv[...]
        d1 = _shift_up(dconv, 1, nxt)
        d2 = _shift_up(dconv, 2, nxt)
        carry_conv[...] = dconv[0:8, :]
        cw = cw_ref[...]
        dp = cw[2:3, :] * dconv + cw[1:2, :] * d1 + cw[0:1, :] * d2
        du_ref[:, C_CC:C_CC + D] = (dp * _bf(u_ref, C_VC, D)).astype(BF16)
        du_ref[:, C_VC:C_VC + D] = (dp * _bf(u_ref, C_CC, D)).astype(BF16)

        dyb_b = dyb.astype(BF16)
        dob = _dot_nt(dyb_b, wao_ref[...])
        za = _bf(u_ref, C_ZA, D)
        sga = _sigmoid(za)
        sa = za * sga
        do = dob * sa
        qhat, rq, qn, khat, rk, knb, vband = _attn_inputs(u_ref, ukv_prev, qg_ref, kg_ref, cref, tm)
        bias, bias_first, lane_grp, query_grp = _attn_masks(is_first, cref)
        lane16 = lax.broadcasted_iota(jnp.int32, (1, D), 1)
        dsink_row = jnp.zeros((1, D), F32)
        o_cols, dq_cols, dk4, dv4 = [], [], [], []
        for h in range(N_KV):
            sink = _sink_row(sinks_ref, h, query_grp)
            dsink = jnp.zeros((1, GROUP * BLK), F32)
            o_rows, dq_rows, dk_parts, dv_parts = [], [], [], []
            for b in range(nb):
                rows = slice(b * BLK, (b + 1) * BLK)
                band = slice(b * BLK, (b + 2) * BLK)
                cols = slice(h * KVW, (h + 1) * KVW)
                qs = _stack_groups(qn[rows, cols], lane_grp, h)
                pn_t, ps = _softmax_block(qs, knb[band], bias_first if b == 0 else bias, sink)
                pn_b = pn_t.astype(BF16)
                o_rows.append(_unstack_groups(_dot_tn(pn_b, vband[band]), lane_grp, h))
                dos = _stack_groups(do[rows, cols].astype(BF16), lane_grp, h)
                dpn_t = _dot_nt(vband[band], dos)
                delta = jnp.sum(pn_t * dpn_t, axis=0, keepdims=True)
                ds_t = (pn_t * (dpn_t - delta)).astype(BF16)
                dsink = dsink - ps * delta
                dq_rows.append(_unstack_groups(_dot_tn(ds_t, knb[band]), lane_grp, h))
                dk_parts.append(_dot(ds_t, qs))
                dv_parts.append(_dot(pn_b, dos))
            o_cols.append(jnp.concatenate(o_rows, axis=0))
            dq_cols.append(jnp.concatenate(dq_rows, axis=0))
            dk4.append(_band_sum(dk_parts))
            dv4.append(_band_sum(dv_parts))
            for gi in range(GROUP):
                tot = jnp.sum(dsink[:, gi * BLK:(gi + 1) * BLK], axis=1, keepdims=True)
                dsink_row = dsink_row + jnp.where(lane16 == GROUP * h + gi, tot, 0.0)
        small_ref[SM_SINK:SM_SINK + 1, :] += dsink_row

        o = jnp.concatenate(o_cols, axis=1)
        ob_ref[...] = (o * sa).astype(BF16)
        dyb_ref[...] = dyb_b
        du_ref[:, C_ZA:C_ZA + D] = (dob * o * (sga + sa * (1.0 - sga))).astype(BF16)

        dqn = jnp.concatenate(dq_cols, axis=1)
        small_ref[SM_QG:SM_QG + 1, 0:D] += SCALE * jnp.sum(dqn * qhat, axis=0, keepdims=True)
        dqh = dqn * (qg_ref[...] * LN2)
        dq = rq * (dqh - qhat * _head_mean(dqh * qhat, cref["sel_q"][...], cref["exp_q"][...]))
        du_ref[:, C_Q:C_Q + D] = dq.astype(BF16)

        dkn_band = (dk4[0] + dk4[1] + dk4[2] + dk4[3]) * LN2
        dv_band = dv4[0] + dv4[1] + dv4[2] + dv4[3]
        carried = carry_kv[...]
        pad = jnp.zeros((tm - BLK, KVW), F32)
        if nb > 1:
            dkn = dkn_band[BLK:, :] + jnp.concatenate([pad, carried[:, 0:KVW]], axis=0)
            dv = dv_band[BLK:, :] + jnp.concatenate([pad, carried[:, KVW:2 * KVW]], axis=0)
        else:
            dkn = dkn_band[BLK:, :] + carried[:, 0:KVW]
            dv = dv_band[BLK:, :] + carried[:, KVW:2 * KVW]
        carry_kv[:, 0:KVW] = dkn_band[0:BLK, :]
        carry_kv[:, KVW:2 * KVW] = dv_band[0:BLK, :]
        khat_t = khat[BLK:, :]
        small_ref[SM_KG:SM_KG + 1, 0:KVW] += jnp.sum(dkn * khat_t, axis=0, keepdims=True)
        dkh = dkn * kg_ref[...]
        dk = rk[BLK:, :] * (dkh - khat_t * _head_mean(dkh * khat_t, cref["sel_k"][...], cref["exp_k"][...]))
        du_ref[:, C_K:C_K + KVW] = dk.astype(BF16)
        du_ref[:, C_V:C_V + KVW] = dv.astype(BF16)

    rev = lambda g: n_tiles - 1 - g
    u_spec, ukv_prev, uvc_prev, ucc_prev = _mixer_specs(t, tm, n_tiles, rev)
    tok = pl.BlockSpec((tm, D), lambda g: (rev(g), 0))
    consts = [c[k] for k in cn]
    wspec = _full((D, D))
    return pl.pallas_call(
        body, name=name, grid=(n_tiles,),
        in_specs=[tok, u_spec, ukv_prev, uvc_prev, ucc_prev, tok, tok, _full((8, D)), _full((1, D)), _full((1, KVW)),
                  pl.BlockSpec(memory_space=pltpu.SMEM), _full((1, 2 * D)), wspec, wspec, wspec]
                 + [_full(a.shape) for a in consts],
        out_specs=[pl.BlockSpec((tm, IN_COLS), lambda g: (rev(g), 0)), _full((SM_ROWS, D))] + [tok] * 5,
        out_shape=[jax.ShapeDtypeStruct((t, IN_COLS), BF16), jax.ShapeDtypeStruct((SM_ROWS, D), F32)]
                  + [jax.ShapeDtypeStruct((t, D), BF16)] * 5,
        scratch_shapes=[pltpu.VMEM((BLK, 2 * KVW), F32), pltpu.VMEM((8, D), F32)],
        compiler_params=_cparams(("arbitrary",)),
    )(dout, u, u, u, u, ya, yb, cw, qg, kg, sinks, gb, wco, wao, wout, *consts)


def matmul_tn(a, b, name, after=None):
    t, m = a.shape
    tk = min(1024, t)
    mb = 2176 if m == IN_COLS else m
    nk = t // tk

    def body(a_ref, b_ref, *rest):
        o_ref, acc = rest[-2:]
        k = pl.program_id(1)
        prod = _dot_tn(a_ref[...].astype(BF16), b_ref[...].astype(BF16))

        @pl.when(k == 0)
        def _():
            acc[...] = prod

        @pl.when(k > 0)
        def _():
            acc[...] += prod

        @pl.when(k == nk - 1)
        def _():
            o_ref[...] = acc[...].astype(BF16)

    return pl.pallas_call(
        body, name=name, grid=(m // mb, nk),
        in_specs=[pl.BlockSpec((tk, mb), lambda j, k: (k, j)), pl.BlockSpec((tk, D), lambda j, k: (k, 0))]
                 + ([] if after is None else [ANY]),
        out_specs=pl.BlockSpec((mb, D), lambda j, k: (j, 0)),
        out_shape=jax.ShapeDtypeStruct((m, D), BF16),
        scratch_shapes=[pltpu.VMEM((mb, D), F32)],
        compiler_params=_cparams(("arbitrary", "arbitrary")),
    )(a, b, *([] if after is None else [after]))


def inproj_bwd_x(du, w, x, ng, dout, name):
    t = x.shape[0]
    tm = min(1024, t)
    kc = 2176
    nk = IN_COLS // kc

    def body(du_ref, w_ref, x_ref, ng_ref, dout_ref, dx_ref, dng_ref, acc):
        i = pl.program_id(0)
        k = pl.program_id(1)
        prod = _dot(du_ref[...], w_ref[...])

        @pl.when(k == 0)
        def _():
            acc[...] = prod

        @pl.when(k > 0)
        def _():
            acc[...] += prod

        @pl.when((i == 0) & (k == 0))
        def _():
            dng_ref[...] = jnp.zeros_like(dng_ref)

        @pl.when(k == nk - 1)
        def _():
            dh = acc[...]
            xf = x_ref[...]
            r = lax.rsqrt(jnp.mean(xf * xf, axis=-1, keepdims=True) + EPS)
            xhat = xf * r
            dng_ref[0:1, :] += jnp.sum(dh * xhat, axis=0, keepdims=True)
            dxh = dh * ng_ref[...]
            dx_ref[...] = dout_ref[...] + r * (dxh - xhat * jnp.mean(dxh * xhat, axis=-1, keepdims=True))

    tok = pl.BlockSpec((tm, D), lambda i, k: (i, 0))
    return pl.pallas_call(
        body, name=name, grid=(t // tm, nk),
        in_specs=[pl.BlockSpec((tm, kc), lambda i, k: (i, k)), pl.BlockSpec((kc, D), lambda i, k: (k, 0)), tok,
                  pl.BlockSpec((1, D), lambda i, k: (0, 0)), tok],
        out_specs=[tok, pl.BlockSpec((8, D), lambda i, k: (0, 0))],
        out_shape=[jax.ShapeDtypeStruct((t, D), F32), jax.ShapeDtypeStruct((8, D), F32)],
        scratch_shapes=[pltpu.VMEM((tm, D), F32)],
        compiler_params=_cparams(("arbitrary", "arbitrary")),
    )(du, w, x, ng, dout)


def loss_head(y, target, name):
    t = y.shape[0]
    tm = min(1024, t)

    def body(y_ref, t_ref, dy_ref, loss_ref):
        @pl.when(pl.program_id(0) == 0)
        def _():
            loss_ref[...] = jnp.zeros_like(loss_ref)
        err = y_ref[...] - t_ref[...]
        dy_ref[...] = err * (1.0 / D)
        part = jnp.sum(jnp.sum(err * err, axis=-1, keepdims=True) * (1.0 / D), axis=0, keepdims=True)
        loss_ref[...] += 0.5 * part

    tok = pl.BlockSpec((tm, D), lambda i: (i, 0))
    return pl.pallas_call(
        body, name=name, grid=(t // tm,), in_specs=[tok, tok],
        out_specs=[tok, pl.BlockSpec((8, 128), lambda i: (0, 0))],
        out_shape=[jax.ShapeDtypeStruct((t, D), F32), jax.ShapeDtypeStruct((8, 128), F32)],
        compiler_params=_cparams(("arbitrary",)),
    )(y, target)


def layer_operands(l, norm_g, conv_w_full, q_norm_g, k_norm_g, sinks, gate_b, w_in_b, wco_b, wao_b, wout_b):
    return dict(
        ng=norm_g[l][None, :], cw=jnp.pad(conv_w_full[l], ((0, 5), (0, 0))),
        qg=jnp.tile(q_norm_g[l] * (SCALE * LOG2E), D // HEAD)[None, :], kg=jnp.tile(k_norm_g[l], N_KV)[None, :],
        sinks=sinks[l][None, :], gb=gate_b[l][None, :],
        w_in=w_in_b[l], wco=wco_b[l], wao=wao_b[l], wout=wout_b[l])


def layer_bwd(dout, saved, lw, c, l, send_off):
    x, u, h, ya, yb = saved
    du, small, merged, yc, ob, dya, dyb = mixer_bwd(dout, u, ya, yb, lw["cw"], lw["qg"], lw["kg"], lw["sinks"], lw["gb"],
                                                    lw["wco"], lw["wao"], lw["wout"], c, f"mixer_bwd_{l}")
    grads = dict(w_in=matmul_tn(du, h, f"dw_in_{l}"), small=small)
    token = send_off(grads, False)
    grads["wout"] = matmul_tn(merged, dout, f"dw_out_{l}", after=token)
    grads["wco"] = matmul_tn(yc, dya, f"dw_conv_out_{l}")
    grads["wao"] = matmul_tn(ob, dyb, f"dw_attn_out_{l}")
    token = send_off(grads, True)
    dx, grads["dng"] = inproj_bwd_x(du, lw["w_in"], x, lw["ng"] + token[0:1, 0:1], dout, f"inproj_bwd_{l}")
    return dx, grads


MESH = pl.DeviceIdType.MESH
ANY = pl.BlockSpec(memory_space=pl.ANY)


def _place():
    return lax.axis_index("x"), lax.axis_index("y"), lax.axis_index("c")


def all_gather(arrs, after, name):
    n = len(arrs)

    def body(*refs):
        ins, outs = refs[:n], refs[n + 1:2 * n + 1]
        send_sems, recv_sems, local_sems = refs[2 * n + 1:]
        x, y, c = _place()
        me, sibling = (x, y, c), (x, y, 1 - c)
        chips = [(1 - x, y), (x, 1 - y), (1 - x, 1 - y)]

        def slot(a, block):
            px, py, pc = block
            return outs[a].at[4 * px + 2 * py + pc]

        def copy(a, k, block, to, src=None):
            return pltpu.make_async_remote_copy(
                src_ref=slot(a, block) if src is None else src, dst_ref=slot(a, block),
                send_sem=send_sems.at[a, k], recv_sem=recv_sems.at[a, k], device_id=to, device_id_type=MESH)

        mine = [pltpu.make_async_copy(ins[a], slot(a, me), local_sems.at[a]) for a in range(n)]
        for cp in mine:
            cp.start()
        first = []
        for a in range(n):
            first.append(copy(a, 0, me, sibling, src=ins[a]))
            first += [copy(a, 1 + j, me, (*chip, c), src=ins[a]) for j, chip in enumerate(chips)]
        for cp in first:
            cp.start()
        passed = []
        for j, chip in enumerate(chips):
            for a in range(n):
                copy(a, 1 + j, (*chip, c), me).wait_recv()
                passed.append(copy(a, 4 + j, (*chip, c), sibling))
                passed[-1].start()
        for a in range(n):
            copy(a, 0, sibling, me).wait_recv()
            for j, chip in enumerate(chips):
                copy(a, 4 + j, (*chip, 1 - c), me).wait_recv()
        for cp in first + passed:
            cp.wait_send()
        for cp in mine:
            cp.wait()

    return pl.pallas_call(
        body, name=name, in_specs=[ANY] * (n + 1), out_specs=[ANY] * n,
        out_shape=[jax.ShapeDtypeStruct((N_DEV,) + a.shape, a.dtype) for a in arrs],
        scratch_shapes=[pltpu.SemaphoreType.DMA((n, 7)), pltpu.SemaphoreType.DMA((n, 7)), pltpu.SemaphoreType.DMA((n,))],
    )(*arrs, after)


HBM_SPEC = pl.BlockSpec(memory_space=pltpu.HBM)
SEM_SPEC = pl.BlockSpec(memory_space=pltpu.SEMAPHORE)
EFFECT = pltpu.SideEffectType.DATAFLOW_SIDE_EFFECTING


ALL_PEERS = (1, 2, 3, 4, 5, 6, 7)


def _flip(place, k):
    x, y, c = place
    return (1 - x if k & 4 else x, 1 - y if k & 2 else y, 1 - c if k & 1 else c)


def _slot(place):
    return 4 * place[0] + 2 * place[1] + place[2]


def _exchange_copies(ins, lands, send_sems, recv_sems, scatter, flips, arriving):
    me = _place()
    out = []
    for a in range(len(ins)):
        for i, k in enumerate(flips):
            peer = _flip(me, k)
            out.append(pltpu.make_async_remote_copy(
                src_ref=ins[a].at[_slot(peer)] if scatter else ins[a],
                dst_ref=lands[a].at[_slot(peer) if arriving else _slot(me)],
                send_sem=send_sems.at[a * len(flips) + i], recv_sem=recv_sems.at[a * len(flips) + i],
                device_id=peer, device_id_type=MESH))
    return out


def exchange_start(arrs, scatter, flips, after, name):
    n = len(arrs)
    lands = [lax.empty(a.shape if scatter else (N_DEV,) + a.shape, a.dtype) for a in arrs]

    def body(*refs):
        ins, lz = refs[:n], refs[n:2 * n]
        send_sems, recv_sems = refs[2 * n + 1], refs[2 * n + 2]
        token = refs[-1]
        for cp in _exchange_copies(ins, lz, send_sems, recv_sems, scatter, flips, False):
            cp.start()
        token[...] = jnp.zeros_like(token)

    sems = pltpu.SemaphoreType.DMA((n * len(flips),))
    res = pl.pallas_call(
        body, name=name,
        out_shape=(sems, sems, *[pltpu.HBM(a.shape, a.dtype) for a in arrs], *[pltpu.HBM(a.shape, a.dtype) for a in lands],
                   jax.ShapeDtypeStruct((8, 128), F32)),
        in_specs=[HBM_SPEC] * (2 * n) + [ANY],
        out_specs=(SEM_SPEC, SEM_SPEC, *[HBM_SPEC] * (2 * n), pl.BlockSpec(memory_space=pltpu.VMEM)),
        input_output_aliases={i: 2 + i for i in range(2 * n)},
        compiler_params=pltpu.CompilerParams(has_side_effects=EFFECT),
    )(*[pltpu.with_memory_space_constraint(a, pltpu.HBM) for a in arrs],
      *[pltpu.with_memory_space_constraint(a, pltpu.HBM) for a in lands], after)
    return dict(send=res[0], recv=res[1], srcs=res[2:2 + n], lands=res[2 + n:2 + 2 * n], token=res[-1], scatter=scatter,
                flips=flips)


def exchange_wait(state, after, name):
    n = len(state["srcs"])

    def body(*refs):
        ins, lz = refs[:n], refs[n:2 * n]
        send_sems, recv_sems = refs[2 * n], refs[2 * n + 1]
        for cp in _exchange_copies(ins, lz, send_sems, recv_sems, state["scatter"], state["flips"], True):
            cp.wait_send()
            cp.wait_recv()

    both = list(state["srcs"]) + list(state["lands"])
    res = pl.pallas_call(
        body, name=name, out_shape=tuple(pltpu.HBM(a.shape, a.dtype) for a in both),
        in_specs=[HBM_SPEC] * (2 * n) + [SEM_SPEC, SEM_SPEC, ANY], out_specs=tuple([HBM_SPEC] * (2 * n)),
        input_output_aliases={i: i for i in range(2 * n)},
        compiler_params=pltpu.CompilerParams(has_side_effects=EFFECT),
    )(*both, state["send"], state["recv"], after)
    return res[:n], res[n:]


OTHER_CHIPS = (2, 4, 6)
SAME_CORE = (1, 2, 4, 6)


def _forward_copies(lands, send_sems, recv_sems, arriving):
    me = _place()
    sibling = _flip(me, 1)
    origin = sibling if arriving else me
    return [pltpu.make_async_remote_copy(
        src_ref=lands[a].at[_slot(_flip(origin, k))], dst_ref=lands[a].at[_slot(_flip(origin, k))],
        send_sem=send_sems.at[a * len(OTHER_CHIPS) + i], recv_sem=recv_sems.at[a * len(OTHER_CHIPS) + i],
        device_id=sibling, device_id_type=MESH) for a in range(len(lands)) for i, k in enumerate(OTHER_CHIPS)]


def forward_start(lands, after, name):
    n = len(lands)

    def body(*refs):
        lz = refs[:n]
        send_sems, recv_sems = refs[n + 1], refs[n + 2]
        token = refs[-1]
        for cp in _forward_copies(lz, send_sems, recv_sems, False):
            cp.start()
        token[...] = jnp.zeros_like(token)

    sems = pltpu.SemaphoreType.DMA((n * len(OTHER_CHIPS),))
    res = pl.pallas_call(
        body, name=name,
        out_shape=(sems, sems, *[pltpu.HBM(a.shape, a.dtype) for a in lands], jax.ShapeDtypeStruct((8, 128), F32)),
        in_specs=[HBM_SPEC] * n + [ANY],
        out_specs=(SEM_SPEC, SEM_SPEC, *[HBM_SPEC] * n, pl.BlockSpec(memory_space=pltpu.VMEM)),
        input_output_aliases={i: 2 + i for i in range(n)},
        compiler_params=pltpu.CompilerParams(has_side_effects=EFFECT),
    )(*[pltpu.with_memory_space_constraint(a, pltpu.HBM) for a in lands], after)
    return dict(send=res[0], recv=res[1], lands=res[2:2 + n], token=res[-1])


def forward_wait(state, after, name):
    n = len(state["lands"])

    def body(*refs):
        lz = refs[:n]
        send_sems, recv_sems = refs[n], refs[n + 1]
        for cp in _forward_copies(lz, send_sems, recv_sems, False):
            cp.wait_send()
        for cp in _forward_copies(lz, send_sems, recv_sems, True):
            cp.wait_recv()

    return pl.pallas_call(
        body, name=name, out_shape=tuple(pltpu.HBM(a.shape, a.dtype) for a in state["lands"]),
        in_specs=[HBM_SPEC] * n + [SEM_SPEC, SEM_SPEC, ANY], out_specs=tuple([HBM_SPEC] * n),
        input_output_aliases={i: i for i in range(n)},
        compiler_params=pltpu.CompilerParams(has_side_effects=EFFECT),
    )(*state["lands"], state["send"], state["recv"], after)


def adamw(w, m, v, parts, name):
    r, cdim = w.shape
    n_parts = parts.shape[0]
    rb = 256 if r % 256 == 0 else (SHARD_COLS // 4 if r == SHARD_COLS else r)

    def body(w_ref, m_ref, v_ref, p_ref, g_ref, d_ref, mo_ref, vo_ref):
        g = p_ref[0].astype(F32)
        for i in range(1, n_parts):
            g = g + p_ref[i].astype(F32)
        m_new = ADAM_B1 * m_ref[...] + (1.0 - ADAM_B1) * g
        v_new = ADAM_B2 * v_ref[...] + (1.0 - ADAM_B2) * (g * g)
        m_hat = m_new / (1.0 - ADAM_B1 ** ADAM_STEP)
        v_hat = v_new / (1.0 - ADAM_B2 ** ADAM_STEP)
        g_ref[...] = g
        d_ref[...] = -ADAM_LR * (m_hat / (jnp.sqrt(v_hat) + ADAM_EPS) + ADAM_WD * w_ref[...])
        mo_ref[...] = m_new
        vo_ref[...] = v_new

    blk = pl.BlockSpec((rb, cdim), lambda i: (i, 0))
    return pl.pallas_call(
        body, name=name, grid=(r // rb,),
        in_specs=[blk, blk, blk, pl.BlockSpec((n_parts, rb, cdim), lambda i: (0, i, 0))],
        out_specs=[blk] * 4, out_shape=[jax.ShapeDtypeStruct((r, cdim), F32)] * 4,
        compiler_params=_cparams(("arbitrary",)),
    )(w, m, v, parts)


def adamw_layers(w, m, v, lands, srcs, me, lo, prev, name):
    nl_all, r, cdim = w.shape
    nl = len(lands)
    rb = 256 if r % 256 == 0 else (SHARD_COLS // 4 if r == SHARD_COLS else r)
    nblk = r // rb
    n_prev = 0 if prev is None else 4

    def body(me_ref, w_ref, m_ref, v_ref, *rest):
        land_refs, src_refs = rest[:nl], rest[nl:2 * nl]
        g_ref, d_ref, mo_ref, vo_ref = rest[2 * nl + n_prev:]
        for k in range(nl):
            @pl.when(pl.program_id(0) == k)
            def _(k=k):
                own = src_refs[k][...].astype(F32)
                g = jnp.where(me_ref[0] == 0, own, land_refs[k][0].astype(F32))
                for i in range(1, N_DEV):
                    g = g + jnp.where(me_ref[0] == i, own, land_refs[k][i].astype(F32))
                m_new = ADAM_B1 * m_ref[...] + (1.0 - ADAM_B1) * g
                v_new = ADAM_B2 * v_ref[...] + (1.0 - ADAM_B2) * (g * g)
                m_hat = m_new / (1.0 - ADAM_B1 ** ADAM_STEP)
                v_hat = v_new / (1.0 - ADAM_B2 ** ADAM_STEP)
                g_ref[...] = g
                d_ref[...] = -ADAM_LR * (m_hat / (jnp.sqrt(v_hat) + ADAM_EPS) + ADAM_WD * w_ref[...])
                mo_ref[...] = m_new
                vo_ref[...] = v_new

    blk = pl.BlockSpec((None, rb, cdim), lambda l, i, me_ref: (lo + l, i, 0))

    def rows(l, i, k):
        return jnp.where(l < k, 0, jnp.where(l == k, i, nblk - 1))

    land_specs = [pl.BlockSpec((N_DEV, rb, cdim), lambda l, i, me_ref, k=k: (0, rows(l, i, k), 0)) for k in range(nl)]
    src_specs = [pl.BlockSpec((None, rb, cdim), lambda l, i, me_ref, k=k: (me_ref[0], rows(l, i, k), 0)) for k in range(nl)]
    return pl.pallas_call(
        body, name=name,
        grid_spec=pltpu.PrefetchScalarGridSpec(
            num_scalar_prefetch=1, grid=(nl, nblk),
            in_specs=[blk, blk, blk] + land_specs + src_specs + [ANY] * n_prev, out_specs=[blk] * 4),
        out_shape=[jax.ShapeDtypeStruct((nl_all, r, cdim), F32)] * 4,
        input_output_aliases={4 + 2 * nl + j: j for j in range(n_prev)},
        compiler_params=_cparams(("arbitrary", "arbitrary")),
    )(me, w, m, v, *lands, *srcs, *([] if prev is None else prev))


def small_sum(parts, fold, name):
    rows = parts.shape[1]

    def dot3(xv, sel):
        out = jnp.zeros((xv.shape[0], sel.shape[1]), F32)
        for _ in range(3):
            hi = xv.astype(BF16)
            out = out + _dot(hi, sel)
            xv = xv - hi.astype(F32)
        return out

    def body(p_ref, fold_ref, o_ref):
        tot = p_ref[0]
        for i in range(1, N_DEV):
            tot = tot + p_ref[i]
        o_ref[...] = tot
        for l in range(rows // SM_ROWS):
            blk = tot[l * SM_ROWS:l * SM_ROWS + 8, 0:D]
            folded = dot3(blk, fold_ref[...])
            o_ref[l * SM_ROWS + SM_QG_FOLDED:l * SM_ROWS + SM_QG_FOLDED + 1, 0:128] = folded[SM_QG:SM_QG + 1, :]
            o_ref[l * SM_ROWS + SM_KG_FOLDED:l * SM_ROWS + SM_KG_FOLDED + 1, 0:128] = folded[SM_KG:SM_KG + 1, :]

    return pl.pallas_call(
        body, name=name, out_shape=jax.ShapeDtypeStruct((rows, D), F32),
        compiler_params=_cparams(None),
    )(parts, fold)


def kernel(x, norm_g, w_in, conv_w, q_norm_g, k_norm_g, sinks, w_conv_out, w_attn_out, gate_b, w_out, loss_target, m_norm_g, m_w_in, m_conv_w, m_q_norm_g, m_k_norm_g, m_sinks, m_w_conv_out, m_w_attn_out, m_gate_b, m_w_out, v_norm_g, v_w_in, v_conv_w, v_q_norm_g, v_k_norm_g, v_sinks, v_w_conv_out, v_w_attn_out, v_gate_b, v_w_out):
    c = _selectors()
    me = 4 * lax.axis_index("x") + 2 * lax.axis_index("y") + lax.axis_index("c")

    w_in_t, m_w_in_t, v_w_in_t = (jnp.swapaxes(a, 1, 2) for a in (w_in, m_w_in, v_w_in))

    def shards(l):
        return [w_in_t[l].astype(BF16), w_conv_out[l].astype(BF16), w_attn_out[l].astype(BF16), w_out[l].astype(BF16)]

    def ici_start(l, after):
        return exchange_start(shards(l), False, SAME_CORE, after, f"gather_start_{l}")

    h = x[0]
    saved, lws = [], []
    lands = all_gather(shards(0) + [conv_w], h, "gather_0")
    conv_full = jnp.transpose(lands[4], (1, 2, 0, 3)).reshape(DEPTH, 3, D)
    ici = ici_start(1, lands[1])
    for l in range(DEPTH):
        lws.append(layer_operands(l, norm_g, conv_full, q_norm_g, k_norm_g, sinks, gate_b,
                                  {l: lands[0].reshape(IN_COLS, D)}, {l: lands[1].reshape(D, D)},
                                  {l: lands[2].reshape(D, D)}, {l: lands[3].reshape(D, D)}))
        ng_l = lws[l]["ng"] + ici["token"][0:1, 0:1] if l == 0 else lws[l]["ng"]
        u, hb = inproj_fwd(h, ng_l, lws[l]["w_in"], f"inproj_fwd_{l}")
        gb_l = lws[l]["gb"]
        if l + 1 < DEPTH:
            mine, arrived = exchange_wait(ici, u, f"gather_wait_{l + 1}")
            chip = forward_start(arrived, mine[0], f"gather_forward_start_{l + 1}")
            started = chip["token"]
            if l + 2 < DEPTH:
                ici = ici_start(l + 2, started)
                started = ici["token"]
            gb_l = gb_l + started[0:1, 0:1]
        x_in = h
        h, ya, yb = mixer_fwd(x_in, u, lws[l]["cw"], lws[l]["qg"], lws[l]["kg"], lws[l]["sinks"], gb_l,
                              lws[l]["wco"], lws[l]["wao"], lws[l]["wout"], c, f"mixer_fwd_{l}")
        saved.append((x_in, u, hb, ya, yb))
        if l + 1 < DEPTH:
            lands = forward_wait(chip, h, f"gather_forward_wait_{l + 1}")
            lands = [lax.dynamic_update_index_in_dim(land, src, me, 0) for land, src in zip(lands, mine)]
    dh, loss_part = loss_head(h, loss_target[0], "loss_head")

    grads, scatters = [None] * DEPTH, [[] for _ in range(DEPTH)]
    for l in reversed(range(DEPTH)):
        def send_off(g, done, l=l):
            rest = [g[k].reshape(N_DEV, SHARD_ROWS, D) for k in ("wco", "wao", "wout")] if done else []
            first = [g["w_in"].reshape(N_DEV, SHARD_COLS, D)] if done == (l > 0) else []
            if not first + rest:
                return None
            tag = f"{l}" if l > 0 else ("0_rest" if done else "0_in")
            scatters[l].append(exchange_start(first + rest, True, ALL_PEERS, g["small"], f"scatter_start_{tag}"))
            return scatters[l][-1]["token"]
        dh, grads[l] = layer_bwd(dh, saved[l], lws[l], c, l, send_off)

    me1 = me.astype(jnp.int32).reshape(1)
    mine, lands = {}, {}
    for l in (3, 2, 1):
        mine[l], lands[l] = exchange_wait(scatters[l][0], dh, f"scatter_wait_{l}")
    weights = [(w_in_t, m_w_in_t, v_w_in_t, "w_in"), (w_conv_out, m_w_conv_out, v_w_conv_out, "w_conv_out"),
               (w_attn_out, m_w_attn_out, v_w_attn_out, "w_attn_out"), (w_out, m_w_out, v_w_out, "w_out")]
    upd = [adamw_layers(w, m, v, [lands[l][i] for l in (1, 2, 3)], [mine[l][i] for l in (1, 2, 3)], me1, 1, None,
                        f"adamw_{n}_upper") for i, (w, m, v, n) in enumerate(weights)]
    blocks = []
    for l in range(DEPTH):
        blk = grads[l]["small"]
        blk = blk.at[SM_NORM, 0:D].set(grads[l]["dng"][0])
        if l == 0:
            blk = blk.at[SM_LOSS, 0:128].set(loss_part[0])
        blocks.append(blk)
    small_x = exchange_start([jnp.concatenate(blocks, axis=0)], False, ALL_PEERS, grads[0]["dng"], "gather_small_start")

    m_in, l_in = exchange_wait(scatters[0][0], upd[0][0], "scatter_wait_0_in")
    upd[0] = adamw_layers(*weights[0][:3], [l_in[0]], [m_in[0]], me1, 0, upd[0], "adamw_w_in_0")
    m_rest, l_rest = exchange_wait(scatters[0][1], upd[0][0], "scatter_wait_0_rest")
    for i in (1, 2, 3):
        upd[i] = adamw_layers(*weights[i][:3], [l_rest[i - 1]], [m_rest[i - 1]], me1, 0, upd[i], f"adamw_{weights[i][3]}_0")
    u_in = [jnp.swapaxes(o, 1, 2) for o in upd[0]]
    u_co, u_ao, u_out = upd[1], upd[2], upd[3]

    mine_s, lands_s = exchange_wait(small_x, u_out[0], "gather_small_wait")
    gathered = lax.dynamic_update_index_in_dim(lands_s[0], mine_s[0], me, 0)
    tot = small_sum(gathered, c["fold"], "small_sum")
    tot = tot.reshape(DEPTH, SM_ROWS, D)
    loss = tot[0, SM_LOSS, 0]

    def update_small(w, m, v, g, name):
        return adamw(w, m, v, g[None], name)

    u_ng = update_small(norm_g, m_norm_g, v_norm_g, tot[:, SM_NORM, 0:D], "adamw_norm_g")
    u_qg = update_small(q_norm_g, m_q_norm_g, v_q_norm_g, tot[:, SM_QG_FOLDED, 0:HEAD], "adamw_q_norm_g")
    u_kg = update_small(k_norm_g, m_k_norm_g, v_k_norm_g, tot[:, SM_KG_FOLDED, 0:HEAD], "adamw_k_norm_g")
    u_sk = update_small(sinks, m_sinks, v_sinks, tot[:, SM_SINK, 0:16], "adamw_sinks")
    g_gate = jnp.concatenate([tot[:, SM_GATE, :], tot[:, SM_GATE_B, :]], axis=1)
    u_gb = update_small(gate_b, m_gate_b, v_gate_b, g_gate, "adamw_gate_b")
    g_conv = lax.dynamic_slice_in_dim(tot[:, SM_CONV:SM_CONV + 3, 0:D], me * SHARD_ROWS, SHARD_ROWS, axis=2)
    u_cw = [o.reshape(DEPTH, 3, SHARD_ROWS) for o in update_small(
        conv_w.reshape(DEPTH * 3, SHARD_ROWS), m_conv_w.reshape(DEPTH * 3, SHARD_ROWS),
        v_conv_w.reshape(DEPTH * 3, SHARD_ROWS), g_conv.reshape(DEPTH * 3, SHARD_ROWS), "adamw_conv_w")]

    order = [u_ng, u_in, u_cw, u_qg, u_kg, u_sk, u_co, u_ao, u_gb, u_out]
    return (loss, dh[None], *[u[0] for u in order], *[u[1] for u in order], *[u[2] for u in order], *[u[3] for u in order])
```

```python
import functools

import jax
import jax.numpy as jnp
from jax import lax
from jax.experimental import pallas as pl
from jax.experimental.pallas import tpu as pltpu

F32 = jnp.float32
BF16 = jnp.bfloat16

N_DEV = 8
DEPTH = 4
D = 1024
N_KV = 4
GROUP = 4
HEAD = 64
BLK = 128
KVW = N_KV * HEAD
IN_COLS = 8704
SHARD_COLS = IN_COLS // N_DEV
SHARD_ROWS = D // N_DEV
C_VC, C_BC, C_CC, C_ZC, C_Q, C_K, C_V, C_ZA, C_GA, C_GB = 0, 1024, 2048, 3072, 4096, 5120, 5376, 5632, 6656, 7680
EPS = 1e-6
NEG_INF = -1e30
SCALE = HEAD ** -0.5
LOG2E = 1.4426950408889634
LN2 = 0.6931471805599453

ADAM_LR = 0.001
ADAM_B1 = 0.9
ADAM_B2 = 0.999
ADAM_EPS = 1e-08
ADAM_WD = 0.01
ADAM_STEP = 10

VMEM_LIMIT = 60 * 1024 * 1024
SM_ROWS = 16
SM_GATE, SM_CONV, SM_QG, SM_KG, SM_SINK, SM_NORM, SM_LOSS, SM_GATE_B, SM_QG_FOLDED, SM_KG_FOLDED = 0, 1, 4, 5, 6, 7, 8, 9, 10, 11


def _cparams(sem):
    return pltpu.CompilerParams(dimension_semantics=sem, vmem_limit_bytes=VMEM_LIMIT)


def _dot(a, b):
    return jnp.dot(a, b, preferred_element_type=F32)


def _dot_nt(a, b):
    return lax.dot_general(a, b, (((1,), (1,)), ((), ())), preferred_element_type=F32)


def _dot_tn(a, b):
    return lax.dot_general(a, b, (((0,), (0,)), ((), ())), preferred_element_type=F32)


def _dot2(x, sel):
    hi = x.astype(BF16)
    lo = (x - hi.astype(F32)).astype(BF16)
    return _dot(hi, sel) + _dot(lo, sel)


def _sigmoid(z):
    return 0.5 * jnp.tanh(0.5 * z) + 0.5


def _head_mean(t, sel, exp):
    return _dot2(_dot(t.astype(BF16), sel) * (1.0 / HEAD), exp)


def _shift_down(a, k, before):
    r = pltpu.roll(a, k, 0)
    row = lax.broadcasted_iota(jnp.int32, (8, 1), 0)
    head = jnp.where(row < k, pltpu.roll(before, k, 0), r[0:8, :])
    return jnp.concatenate([head, r[8:, :]], axis=0)


def _shift_up(a, k, after):
    n = a.shape[0]
    r = pltpu.roll(a, n - k, 0)
    row = lax.broadcasted_iota(jnp.int32, (8, 1), 0)
    tail = jnp.where(row >= 8 - k, pltpu.roll(after, 8 - k, 0), r[n - 8:, :])
    return jnp.concatenate([r[:n - 8, :], tail], axis=0)


def _bf(ref, c0, width):
    return ref[:, c0:c0 + width].astype(F32)


def _selectors():
    c = jnp.arange(D)
    sel_q = (c[:, None] // HEAD == jnp.arange(128)[None, :]).astype(BF16)
    ck = jnp.arange(KVW)
    sel_k = (ck[:, None] // HEAD == jnp.arange(128)[None, :]).astype(BF16)
    fold = (c[:, None] % HEAD == jnp.arange(128)[None, :]).astype(BF16)
    qq = jnp.arange(GROUP * BLK)[:, None] % BLK
    kk = jnp.arange(2 * BLK)[None, :]
    valid = (kk > qq) & (kk <= qq + BLK)
    bias = jnp.where(valid, 0.0, NEG_INF).astype(F32).T
    bias_first = jnp.where(valid & (kk >= BLK), 0.0, NEG_INF).astype(F32).T
    return dict(bias=bias, bias_first=bias_first,sel_q=sel_q, exp_q=sel_q.T, sel_k=sel_k, exp_k=sel_k.T, fold=fold)


def inproj_fwd(x, ng, w, name):
    t = x.shape[0]
    tm = min(1024, t)
    cb = 2176
    def body(x_ref, ng_ref, w_ref, u_ref, h_ref, h_scr):
        @pl.when(pl.program_id(1) == 0)
        def _():
            xf = x_ref[...]
            r = lax.rsqrt(jnp.mean(xf * xf, axis=-1, keepdims=True) + EPS)
            hb = (xf * r * ng_ref[...]).astype(BF16)
            h_scr[...] = hb
            h_ref[...] = hb
        u_ref[...] = _dot_nt(h_scr[...], w_ref[...]).astype(BF16)

    return pl.pallas_call(
        body, name=name, grid=(t // tm, IN_COLS // cb),
        in_specs=[pl.BlockSpec((tm, D), lambda i, j: (i, 0)), pl.BlockSpec((1, D), lambda i, j: (0, 0)),
                  pl.BlockSpec((cb, D), lambda i, j: (j, 0))],
        out_specs=[pl.BlockSpec((tm, cb), lambda i, j: (i, j)), pl.BlockSpec((tm, D), lambda i, j: (i, 0))],
        out_shape=[jax.ShapeDtypeStruct((t, IN_COLS), BF16), jax.ShapeDtypeStruct((t, D), BF16)],
        scratch_shapes=[pltpu.VMEM((tm, D), BF16)],
        compiler_params=_cparams(("arbitrary", "arbitrary")),
    )(x, ng, w)


def _conv_fwd(u_ref, uvc_prev, ucc_prev, cw_ref, is_first, tm):
    p = _bf(u_ref, C_CC, D) * _bf(u_ref, C_VC, D)
    pprev = ucc_prev[...].astype(F32) * uvc_prev[...].astype(F32)
    pprev = jnp.where(is_first, 0.0, pprev)
    p1 = _shift_down(p, 1, pprev[8:16, :])
    p2 = _shift_down(p, 2, pprev[8:16, :])
    cw = cw_ref[...]
    conv = cw[0:1, :] * p2 + cw[1:2, :] * p1 + cw[2:3, :] * p
    return p, p1, p2, conv


def _attn_inputs(u_ref, ukv_prev, qg_ref, kg_ref, c, tm):
    q = _bf(u_ref, C_Q, D)
    rq = lax.rsqrt(_head_mean(q * q, c["sel_q"][...], c["exp_q"][...]) + EPS)
    qhat = q * rq
    qn = (qhat * qg_ref[...]).astype(BF16)
    kband = jnp.concatenate([ukv_prev[:, 0:KVW].astype(F32), _bf(u_ref, C_K, KVW)], axis=0)
    rk = lax.rsqrt(_head_mean(kband * kband, c["sel_k"][...], c["exp_k"][...]) + EPS)
    khat = kband * rk
    knb = (khat * kg_ref[...]).astype(BF16)
    vband = jnp.concatenate([ukv_prev[:, KVW:2 * KVW], u_ref[:, C_V:C_V + KVW]], axis=0)
    return qhat, rq, qn, khat, rk, knb, vband


def _attn_masks(is_first, c):
    bias = c["bias"][...]
    bias_first = jnp.where(is_first, c["bias_first"][...], bias)
    lane_grp = lax.broadcasted_iota(jnp.int32, (BLK, KVW), 1) // HEAD
    query_grp = lax.broadcasted_iota(jnp.int32, (1, GROUP * BLK), 1) // BLK
    return bias, bias_first, lane_grp, query_grp


def _sink_row(sinks_ref, h, query_grp):
    row = jnp.full(query_grp.shape, sinks_ref[0, GROUP * h] * LOG2E, F32)
    for gi in range(1, GROUP):
        row = jnp.where(query_grp == gi, sinks_ref[0, GROUP * h + gi] * LOG2E, row)
    return row


def _lane_shift(a, groups):
    shift = (HEAD * groups) % KVW
    return a if shift == 0 else pltpu.roll(a, shift, 1)


def _stack_groups(a256, lane_grp, h):
    zero = jnp.zeros_like(a256)
    return jnp.concatenate([jnp.where(lane_grp == h, _lane_shift(a256, h - gi), zero) for gi in range(GROUP)], axis=0)


def _unstack_groups(a4, lane_grp, h):
    out = _lane_shift(a4[(GROUP - 1) * BLK:, :], GROUP - 1 - h)
    for gi in reversed(range(GROUP - 1)):
        out = jnp.where(lane_grp == gi, _lane_shift(a4[gi * BLK:(gi + 1) * BLK], gi - h), out)
    return out


def _band_sum(parts):
    pieces = [parts[0][0:BLK, :]]
    for b in range(1, len(parts)):
        pieces.append(parts[b - 1][BLK:, :] + parts[b][0:BLK, :])
    pieces.append(parts[-1][BLK:, :])
    return jnp.concatenate(pieces, axis=0)


def _softmax_block(qs, kt_b, bias, sink):
    s = _dot_nt(kt_b, qs) + bias
    m = jnp.maximum(jnp.max(s, axis=0, keepdims=True), sink)
    e = jnp.exp2(s - m)
    es = jnp.exp2(sink - m)
    inv = 1.0 / (jnp.sum(e, axis=0, keepdims=True) + es)
    return e * inv, es * inv


def _mixer_specs(t, tm, n_tiles, tile_of):
    nb = tm // BLK
    u_spec = pl.BlockSpec((tm, IN_COLS), lambda g: (tile_of(g), 0))
    ukv_prev = pl.BlockSpec((BLK, 2 * KVW), lambda g: (jnp.maximum(tile_of(g) * nb - 1, 0), C_K // (2 * KVW)))
    uvc_prev = pl.BlockSpec((16, D), lambda g: (jnp.maximum(tile_of(g) * (tm // 16) - 1, 0), C_VC // D))
    ucc_prev = pl.BlockSpec((16, D), lambda g: (jnp.maximum(tile_of(g) * (tm // 16) - 1, 0), C_CC // D))
    return u_spec, ukv_prev, uvc_prev, ucc_prev


def _full(shape):
    n = len(shape)
    return pl.BlockSpec(shape, lambda g: (0,) * n)


def mixer_fwd(x, u, cw, qg, kg, sinks, gb, wco, wao, wout, c, name):
    t = x.shape[0]
    tm = min(512, t)
    n_tiles = t // tm
    nb = tm // BLK
    cn = sorted(c)

    def body(x_ref, u_ref, ukv_prev, uvc_prev, ucc_prev, cw_ref, qg_ref, kg_ref, sinks_ref, gb_ref, wco_ref, wao_ref,
             wout_ref, *rest):
        cref = dict(zip(cn, rest[:len(cn)]))
        xo_ref, ya_ref, yb_ref = rest[len(cn):]
        is_first = pl.program_id(0) == 0
        _, _, _, conv = _conv_fwd(u_ref, uvc_prev, ucc_prev, cw_ref, is_first, tm)
        zc = _bf(u_ref, C_ZC, D)
        yc = _bf(u_ref, C_BC, D) * conv * (zc * _sigmoid(zc))
        ya = _dot(yc.astype(BF16), wco_ref[...])

        _, _, qn, _, _, knb, vband = _attn_inputs(u_ref, ukv_prev, qg_ref, kg_ref, cref, tm)
        bias, bias_first, lane_grp, query_grp = _attn_masks(is_first, cref)
        o_cols = []
        for h in range(N_KV):
            sink = _sink_row(sinks_ref, h, query_grp)
            o_rows = []
            for b in range(nb):
                qs = _stack_groups(qn[b * BLK:(b + 1) * BLK, h * KVW:(h + 1) * KVW], lane_grp, h)
                pn_t, _ = _softmax_block(qs, knb[b * BLK:(b + 2) * BLK], bias_first if b == 0 else bias, sink)
                o4 = _dot_tn(pn_t.astype(BF16), vband[b * BLK:(b + 2) * BLK])
                o_rows.append(_unstack_groups(o4, lane_grp, h))
            o_cols.append(jnp.concatenate(o_rows, axis=0))
        za = _bf(u_ref, C_ZA, D)
        ob = jnp.concatenate(o_cols, axis=1) * (za * _sigmoid(za))
        yb = _dot(ob.astype(BF16), wao_ref[...])

        g_a = _sigmoid(_bf(u_ref, C_GA, D) + gb_ref[:, 0:D])
        g_b = _sigmoid(_bf(u_ref, C_GB, D) + gb_ref[:, D:2 * D])
        merged = g_a * ya + g_b * yb
        xo_ref[...] = x_ref[...] + _dot(merged.astype(BF16), wout_ref[...])
        ya_ref[...] = ya.astype(BF16)
        yb_ref[...] = yb.astype(BF16)

    u_spec, ukv_prev, uvc_prev, ucc_prev = _mixer_specs(t, tm, n_tiles, lambda g: g)
    tok = pl.BlockSpec((tm, D), lambda g: (g, 0))
    consts = [c[k] for k in cn]
    return pl.pallas_call(
        body, name=name, grid=(n_tiles,),
        in_specs=[tok, u_spec, ukv_prev, uvc_prev, ucc_prev, _full((8, D)), _full((1, D)), _full((1, KVW)),
                  pl.BlockSpec(memory_space=pltpu.SMEM), _full((1, 2 * D)), _full((D, D)), _full((D, D)), _full((D, D))]
                 + [_full(a.shape) for a in consts],
        out_specs=[tok, tok, tok],
        out_shape=[jax.ShapeDtypeStruct((t, D), F32)] + [jax.ShapeDtypeStruct((t, D), BF16)] * 2,
        compiler_params=_cparams(("arbitrary",)),
    )(x, u, u, u, u, cw, qg, kg, sinks, gb, wco, wao, wout, *consts)


def mixer_bwd(dout, u, ya, yb, cw, qg, kg, sinks, gb, wco, wao, wout, c, name):
    t = dout.shape[0]
    tm = min(256, t)
    n_tiles = t // tm
    nb = tm // BLK
    kb = tm + BLK
    cn = sorted(c)

    def body(dout_ref, u_ref, ukv_prev, uvc_prev, ucc_prev, ya_ref, yb_ref, cw_ref, qg_ref, kg_ref, sinks_ref, gb_ref,
             wco_ref, wao_ref, wout_ref, *rest):
        cref = dict(zip(cn, rest[:len(cn)]))
        (du_ref, small_ref, merged_ref, yc_ref, ob_ref, dya_ref, dyb_ref, carry_kv, carry_conv) = rest[len(cn):]
        g = pl.program_id(0)
        is_first = g == n_tiles - 1

        @pl.when(g == 0)
        def _():
            carry_kv[...] = jnp.zeros_like(carry_kv)
            carry_conv[...] = jnp.zeros_like(carry_conv)
            small_ref[...] = jnp.zeros_like(small_ref)

        dout = dout_ref[...]
        dout_b = dout.astype(BF16)
        ya_v = ya_ref[...].astype(F32)
        yb_v = yb_ref[...].astype(F32)
        g_a = _sigmoid(_bf(u_ref, C_GA, D) + gb_ref[:, 0:D])
        g_b = _sigmoid(_bf(u_ref, C_GB, D) + gb_ref[:, D:2 * D])
        merged = g_a * ya_v + g_b * yb_v
        dmerged = _dot_nt(dout_b, wout_ref[...])
        merged_ref[...] = merged.astype(BF16)
        dya = dmerged * g_a
        dyb = dmerged * g_b
        dgl_a = dya * ya_v * (1.0 - g_a)
        dgl_b = dyb * yb_v * (1.0 - g_b)
        du_ref[:, C_GA:C_GA + D] = dgl_a.astype(BF16)
        du_ref[:, C_GB:C_GB + D] = dgl_b.astype(BF16)
        small_ref[SM_GATE:SM_GATE + 1, 0:D] += jnp.sum(dgl_a, axis=0, keepdims=True)
        small_ref[SM_GATE_B:SM_GATE_B + 1, 0:D] += jnp.sum(dgl_b, axis=0, keepdims=True)

        p, p1, p2, conv = _conv_fwd(u_ref, uvc_prev, ucc_prev, cw_ref, is_first, tm)
        zc = _bf(u_ref, C_ZC, D)
        bc = _bf(u_ref, C_BC, D)
        sg = _sigmoid(zc)
        sc = zc * sg
        bconv = bc * conv
        dya_b = dya.astype(BF16)
        yc_ref[...] = (bconv * sc).astype(BF16)
        dya_ref[...] = dya_b
        dyc = _dot_nt(dya_b, wco_ref[...])
        dyc_s = dyc * sc
        du_ref[:, C_BC:C_BC + D] = (dyc_s * conv).astype(BF16)
        du_ref[:, C_ZC:C_ZC + D] = (dyc * bconv * (sg + sc * (1.0 - sg))).astype(BF16)
        dconv = dyc_s * bc
        small_ref[SM_CONV + 2:SM_CONV + 3, 0:D] += jnp.sum(dconv * p, axis=0, keepdims=True)
        small_ref[SM_CONV + 1:SM_CONV + 2, 0:D] += jnp.sum(dconv * p1, axis=0, keepdims=True)
        small_ref[SM_CONV:SM_CONV + 1, 0:D] += jnp.sum(dconv * p2, axis=0, keepdims=True)
        nxt = carry_conv[...]
        d1 = _shift_up(dconv, 1, nxt)
        d2 = _shift_up(dconv, 2, nxt)
        carry_conv[...] = dconv[0:8, :]
        cw = cw_ref[...]
        dp = cw[2:3, :] * dconv + cw[1:2, :] * d1 + cw[0:1, :] * d2
        du_ref[:, C_CC:C_CC + D] = (dp * _bf(u_ref, C_VC, D)).astype(BF16)
        du_ref[:, C_VC:C_VC + D] = (dp * _bf(u_ref, C_CC, D)).astype(BF16)

        dyb_b = dyb.astype(BF16)
        dob = _dot_nt(dyb_b, wao_ref[...])
        za = _bf(u_ref, C_ZA, D)
        sga = _sigmoid(za)
        sa = za * sga
        do = dob * sa
        qhat, rq, qn, khat, rk, knb, vband = _attn_inputs(u_ref, ukv_prev, qg_ref, kg_ref, cref, tm)
        bias, bias_first, lane_grp, query_grp = _attn_masks(is_first, cref)
        lane16 = lax.broadcasted_iota(jnp.int32, (1, D), 1)
        dsink_row = jnp.zeros((1, D), F32)
        o_cols, dq_cols, dk4, dv4 = [], [], [], []
        for h in range(N_KV):
            sink = _sink_row(sinks_ref, h, query_grp)
            dsink = jnp.zeros((1, GROUP * BLK), F32)
            o_rows, dq_rows, dk_parts, dv_parts = [], [], [], []
            for b in range(nb):
                rows = slice(b * BLK, (b + 1) * BLK)
                band = slice(b * BLK, (b + 2) * BLK)
                cols = slice(h * KVW, (h + 1) * KVW)
                qs = _stack_groups(qn[rows, cols], lane_grp, h)
                pn_t, ps = _softmax_block(qs, knb[band], bias_first if b == 0 else bias, sink)
                pn_b = pn_t.astype(BF16)
                o_rows.append(_unstack_groups(_dot_tn(pn_b, vband[band]), lane_grp, h))
                dos = _stack_groups(do[rows, cols].astype(BF16), lane_grp, h)
                dpn_t = _dot_nt(vband[band], dos)
                delta = jnp.sum(pn_t * dpn_t, axis=0, keepdims=True)
                ds_t = (pn_t * (dpn_t - delta)).astype(BF16)
                dsink = dsink - ps * delta
                dq_rows.append(_unstack_groups(_dot_tn(ds_t, knb[band]), lane_grp, h))
                dk_parts.append(_dot(ds_t, qs))
                dv_parts.append(_dot(pn_b, dos))
            o_cols.append(jnp.concatenate(o_rows, axis=0))
            dq_cols.append(jnp.concatenate(dq_rows, axis=0))
            dk4.append(_band_sum(dk_parts))
            dv4.append(_band_sum(dv_parts))
            for gi in range(GROUP):
                tot = jnp.sum(dsink[:, gi * BLK:(gi + 1) * BLK], axis=1, keepdims=True)
                dsink_row = dsink_row + jnp.where(lane16 == GROUP * h + gi, tot, 0.0)
        small_ref[SM_SINK:SM_SINK + 1, :] += dsink_row

        o = jnp.concatenate(o_cols, axis=1)
        ob_ref[...] = (o * sa).astype(BF16)
        dyb_ref[...] = dyb_b
        du_ref[:, C_ZA:C_ZA + D] = (dob * o * (sga + sa * (1.0 - sga))).astype(BF16)

        dqn = jnp.concatenate(dq_cols, axis=1)
        small_ref[SM_QG:SM_QG + 1, 0:D] += SCALE * jnp.sum(dqn * qhat, axis=0, keepdims=True)
        dqh = dqn * (qg_ref[...] * LN2)
        dq = rq * (dqh - qhat * _head_mean(dqh * qhat, cref["sel_q"][...], cref["exp_q"][...]))
        du_ref[:, C_Q:C_Q + D] = dq.astype(BF16)

        dkn_band = (dk4[0] + dk4[1] + dk4[2] + dk4[3]) * LN2
        dv_band = dv4[0] + dv4[1] + dv4[2] + dv4[3]
        carried = carry_kv[...]
        pad = jnp.zeros((tm - BLK, KVW), F32)
        if nb > 1:
            dkn = dkn_band[BLK:, :] + jnp.concatenate([pad, carried[:, 0:KVW]], axis=0)
            dv = dv_band[BLK:, :] + jnp.concatenate([pad, carried[:, KVW:2 * KVW]], axis=0)
        else:
            dkn = dkn_band[BLK:, :] + carried[:, 0:KVW]
            dv = dv_band[BLK:, :] + carried[:, KVW:2 * KVW]
        carry_kv[:, 0:KVW] = dkn_band[0:BLK, :]
        carry_kv[:, KVW:2 * KVW] = dv_band[0:BLK, :]
        khat_t = khat[BLK:, :]
        small_ref[SM_KG:SM_KG + 1, 0:KVW] += jnp.sum(dkn * khat_t, axis=0, keepdims=True)
        dkh = dkn * kg_ref[...]
        dk = rk[BLK:, :] * (dkh - khat_t * _head_mean(dkh * khat_t, cref["sel_k"][...], cref["exp_k"][...]))
        du_ref[:, C_K:C_K + KVW] = dk.astype(BF16)
        du_ref[:, C_V:C_V + KVW] = dv.astype(BF16)

    rev = lambda g: n_tiles - 1 - g
    u_spec, ukv_prev, uvc_prev, ucc_prev = _mixer_specs(t, tm, n_tiles, rev)
    tok = pl.BlockSpec((tm, D), lambda g: (rev(g), 0))
    consts = [c[k] for k in cn]
    wspec = _full((D, D))
    return pl.pallas_call(
        body, name=name, grid=(n_tiles,),
        in_specs=[tok, u_spec, ukv_prev, uvc_prev, ucc_prev, tok, tok, _full((8, D)), _full((1, D)), _full((1, KVW)),
                  pl.BlockSpec(memory_space=pltpu.SMEM), _full((1, 2 * D)), wspec, wspec, wspec]
                 + [_full(a.shape) for a in consts],
        out_specs=[pl.BlockSpec((tm, IN_COLS), lambda g: (rev(g), 0)), _full((SM_ROWS, D))] + [tok] * 5,
        out_shape=[jax.ShapeDtypeStruct((t, IN_COLS), BF16), jax.ShapeDtypeStruct((SM_ROWS, D), F32)]
                  + [jax.ShapeDtypeStruct((t, D), BF16)] * 5,
        scratch_shapes=[pltpu.VMEM((BLK, 2 * KVW), F32), pltpu.VMEM((8, D), F32)],
        compiler_params=_cparams(("arbitrary",)),
    )(dout, u, u, u, u, ya, yb, cw, qg, kg, sinks, gb, wco, wao, wout, *consts)


def matmul_tn(a, b, name, after=None):
    t, m = a.shape
    tk = min(2048, t)
    mb = 2176 if m == IN_COLS else m
    nk = t // tk

    def body(a_ref, b_ref, *rest):
        o_ref, acc = rest[-2:]
        k = pl.program_id(1)
        prod = _dot_tn(a_ref[...].astype(BF16), b_ref[...].astype(BF16))

        @pl.when(k == 0)
        def _():
            acc[...] = prod

        @pl.when(k > 0)
        def _():
            acc[...] += prod

        @pl.when(k == nk - 1)
        def _():
            o_ref[...] = acc[...].astype(BF16)

    return pl.pallas_call(
        body, name=name, grid=(m // mb, nk),
        in_specs=[pl.BlockSpec((tk, mb), lambda j, k: (k, j)), pl.BlockSpec((tk, D), lambda j, k: (k, 0))]
                 + ([] if after is None else [ANY]),
        out_specs=pl.BlockSpec((mb, D), lambda j, k: (j, 0)),
        out_shape=jax.ShapeDtypeStruct((m, D), BF16),
        scratch_shapes=[pltpu.VMEM((mb, D), F32)],
        compiler_params=_cparams(("arbitrary", "arbitrary")),
    )(a, b, *([] if after is None else [after]))


def inproj_bwd_x(du, w, x, ng, dout, name):
    t = x.shape[0]
    tm = min(1024, t)
    kc = 2176
    nk = IN_COLS // kc

    def body(du_ref, w_ref, x_ref, ng_ref, dout_ref, dx_ref, dng_ref, acc):
        i = pl.program_id(0)
        k = pl.program_id(1)
        prod = _dot(du_ref[...], w_ref[...])

        @pl.when(k == 0)
        def _():
            acc[...] = prod

        @pl.when(k > 0)
        def _():
            acc[...] += prod

        @pl.when((i == 0) & (k == 0))
        def _():
            dng_ref[...] = jnp.zeros_like(dng_ref)

        @pl.when(k == nk - 1)
        def _():
            dh = acc[...]
            xf = x_ref[...]
            r = lax.rsqrt(jnp.mean(xf * xf, axis=-1, keepdims=True) + EPS)
            xhat = xf * r
            dng_ref[0:1, :] += jnp.sum(dh * xhat, axis=0, keepdims=True)
            dxh = dh * ng_ref[...]
            dx_ref[...] = dout_ref[...] + r * (dxh - xhat * jnp.mean(dxh * xhat, axis=-1, keepdims=True))

    tok = pl.BlockSpec((tm, D), lambda i, k: (i, 0))
    return pl.pallas_call(
        body, name=name, grid=(t // tm, nk),
        in_specs=[pl.BlockSpec((tm, kc), lambda i, k: (i, k)), pl.BlockSpec((kc, D), lambda i, k: (k, 0)), tok,
                  pl.BlockSpec((1, D), lambda i, k: (0, 0)), tok],
        out_specs=[tok, pl.BlockSpec((8, D), lambda i, k: (0, 0))],
        out_shape=[jax.ShapeDtypeStruct((t, D), F32), jax.ShapeDtypeStruct((8, D), F32)],
        scratch_shapes=[pltpu.VMEM((tm, D), F32)],
        compiler_params=_cparams(("arbitrary", "arbitrary")),
    )(du, w, x, ng, dout)


def loss_head(y, target, name):
    t = y.shape[0]
    tm = min(1024, t)

    def body(y_ref, t_ref, dy_ref, loss_ref):
        @pl.when(pl.program_id(0) == 0)
        def _():
            loss_ref[...] = jnp.zeros_like(loss_ref)
        err = y_ref[...] - t_ref[...]
        dy_ref[...] = err * (1.0 / D)
        part = jnp.sum(jnp.sum(err * err, axis=-1, keepdims=True) * (1.0 / D), axis=0, keepdims=True)
        loss_ref[...] += 0.5 * part

    tok = pl.BlockSpec((tm, D), lambda i: (i, 0))
    return pl.pallas_call(
        body, name=name, grid=(t // tm,), in_specs=[tok, tok],
        out_specs=[tok, pl.BlockSpec((8, 128), lambda i: (0, 0))],
        out_shape=[jax.ShapeDtypeStruct((t, D), F32), jax.ShapeDtypeStruct((8, 128), F32)],
        compiler_params=_cparams(("arbitrary",)),
    )(y, target)


def layer_operands(l, norm_g, conv_w_full, q_norm_g, k_norm_g, sinks, gate_b, w_in_b, wco_b, wao_b, wout_b):
    return dict(
        ng=norm_g[l][None, :], cw=jnp.pad(conv_w_full[l], ((0, 5), (0, 0))),
        qg=jnp.tile(q_norm_g[l] * (SCALE * LOG2E), D // HEAD)[None, :], kg=jnp.tile(k_norm_g[l], N_KV)[None, :],
        sinks=sinks[l][None, :], gb=gate_b[l][None, :],
        w_in=w_in_b[l], wco=wco_b[l], wao=wao_b[l], wout=wout_b[l])


def layer_bwd(dout, saved, lw, c, l, send_off):
    x, u, h, ya, yb = saved
    du, small, merged, yc, ob, dya, dyb = mixer_bwd(dout, u, ya, yb, lw["cw"], lw["qg"], lw["kg"], lw["sinks"], lw["gb"],
                                                    lw["wco"], lw["wao"], lw["wout"], c, f"mixer_bwd_{l}")
    grads = dict(w_in=matmul_tn(du, h, f"dw_in_{l}"), small=small)
    token = send_off(grads, False)
    grads["wout"] = matmul_tn(merged, dout, f"dw_out_{l}", after=token)
    grads["wco"] = matmul_tn(yc, dya, f"dw_conv_out_{l}")
    grads["wao"] = matmul_tn(ob, dyb, f"dw_attn_out_{l}")
    token = send_off(grads, True)
    dx, grads["dng"] = inproj_bwd_x(du, lw["w_in"], x, lw["ng"] + token[0:1, 0:1], dout, f"inproj_bwd_{l}")
    return dx, grads


MESH = pl.DeviceIdType.MESH
ANY = pl.BlockSpec(memory_space=pl.ANY)


def _place():
    return lax.axis_index("x"), lax.axis_index("y"), lax.axis_index("c")


def all_gather(arrs, after, name):
    n = len(arrs)

    def body(*refs):
        ins, outs = refs[:n], refs[n + 1:2 * n + 1]
        send_sems, recv_sems, local_sems = refs[2 * n + 1:]
        x, y, c = _place()
        me, sibling = (x, y, c), (x, y, 1 - c)
        chips = [(1 - x, y), (x, 1 - y), (1 - x, 1 - y)]

        def slot(a, block):
            px, py, pc = block
            return outs[a].at[4 * px + 2 * py + pc]

        def copy(a, k, block, to, src=None):
            return pltpu.make_async_remote_copy(
                src_ref=slot(a, block) if src is None else src, dst_ref=slot(a, block),
                send_sem=send_sems.at[a, k], recv_sem=recv_sems.at[a, k], device_id=to, device_id_type=MESH)

        mine = [pltpu.make_async_copy(ins[a], slot(a, me), local_sems.at[a]) for a in range(n)]
        for cp in mine:
            cp.start()
        first = []
        for a in range(n):
            first.append(copy(a, 0, me, sibling, src=ins[a]))
            first += [copy(a, 1 + j, me, (*chip, c), src=ins[a]) for j, chip in enumerate(chips)]
        for cp in first:
            cp.start()
        passed = []
        for j, chip in enumerate(chips):
            for a in range(n):
                copy(a, 1 + j, (*chip, c), me).wait_recv()
                passed.append(copy(a, 4 + j, (*chip, c), sibling))
                passed[-1].start()
        for a in range(n):
            copy(a, 0, sibling, me).wait_recv()
            for j, chip in enumerate(chips):
                copy(a, 4 + j, (*chip, 1 - c), me).wait_recv()
        for cp in first + passed:
            cp.wait_send()
        for cp in mine:
            cp.wait()

    return pl.pallas_call(
        body, name=name, in_specs=[ANY] * (n + 1), out_specs=[ANY] * n,
        out_shape=[jax.ShapeDtypeStruct((N_DEV,) + a.shape, a.dtype) for a in arrs],
        scratch_shapes=[pltpu.SemaphoreType.DMA((n, 7)), pltpu.SemaphoreType.DMA((n, 7)), pltpu.SemaphoreType.DMA((n,))],
    )(*arrs, after)


HBM_SPEC = pl.BlockSpec(memory_space=pltpu.HBM)
SEM_SPEC = pl.BlockSpec(memory_space=pltpu.SEMAPHORE)
EFFECT = pltpu.SideEffectType.DATAFLOW_SIDE_EFFECTING


ALL_PEERS = (1, 2, 3, 4, 5, 6, 7)


def _flip(place, k):
    x, y, c = place
    return (1 - x if k & 4 else x, 1 - y if k & 2 else y, 1 - c if k & 1 else c)


def _slot(place):
    return 4 * place[0] + 2 * place[1] + place[2]


def _exchange_copies(ins, lands, send_sems, recv_sems, scatter, flips, arriving):
    me = _place()
    out = []
    for a in range(len(ins)):
        for i, k in enumerate(flips):
            peer = _flip(me, k)
            out.append(pltpu.make_async_remote_copy(
                src_ref=ins[a].at[_slot(peer)] if scatter else ins[a],
                dst_ref=lands[a].at[_slot(peer) if arriving else _slot(me)],
                send_sem=send_sems.at[a * len(flips) + i], recv_sem=recv_sems.at[a * len(flips) + i],
                device_id=peer, device_id_type=MESH))
    return out


def exchange_start(arrs, scatter, flips, after, name):
    n = len(arrs)
    lands = [lax.empty(a.shape if scatter else (N_DEV,) + a.shape, a.dtype) for a in arrs]

    def body(*refs):
        ins, lz = refs[:n], refs[n:2 * n]
        send_sems, recv_sems = refs[2 * n + 1], refs[2 * n + 2]
        token = refs[-1]
        for cp in _exchange_copies(ins, lz, send_sems, recv_sems, scatter, flips, False):
            cp.start()
        token[...] = jnp.zeros_like(token)

    sems = pltpu.SemaphoreType.DMA((n * len(flips),))
    res = pl.pallas_call(
        body, name=name,
        out_shape=(sems, sems, *[pltpu.HBM(a.shape, a.dtype) for a in arrs], *[pltpu.HBM(a.shape, a.dtype) for a in lands],
                   jax.ShapeDtypeStruct((8, 128), F32)),
        in_specs=[HBM_SPEC] * (2 * n) + [ANY],
        out_specs=(SEM_SPEC, SEM_SPEC, *[HBM_SPEC] * (2 * n), pl.BlockSpec(memory_space=pltpu.VMEM)),
        input_output_aliases={i: 2 + i for i in range(2 * n)},
        compiler_params=pltpu.CompilerParams(has_side_effects=EFFECT),
    )(*[pltpu.with_memory_space_constraint(a, pltpu.HBM) for a in arrs],
      *[pltpu.with_memory_space_constraint(a, pltpu.HBM) for a in lands], after)
    return dict(send=res[0], recv=res[1], srcs=res[2:2 + n], lands=res[2 + n:2 + 2 * n], token=res[-1], scatter=scatter,
                flips=flips)


def exchange_wait(state, after, name):
    n = len(state["srcs"])

    def body(*refs):
        ins, lz = refs[:n], refs[n:2 * n]
        send_sems, recv_sems = refs[2 * n], refs[2 * n + 1]
        for cp in _exchange_copies(ins, lz, send_sems, recv_sems, state["scatter"], state["flips"], True):
            cp.wait_send()
            cp.wait_recv()

    both = list(state["srcs"]) + list(state["lands"])
    res = pl.pallas_call(
        body, name=name, out_shape=tuple(pltpu.HBM(a.shape, a.dtype) for a in both),
        in_specs=[HBM_SPEC] * (2 * n) + [SEM_SPEC, SEM_SPEC, ANY], out_specs=tuple([HBM_SPEC] * (2 * n)),
        input_output_aliases={i: i for i in range(2 * n)},
        compiler_params=pltpu.CompilerParams(has_side_effects=EFFECT),
    )(*both, state["send"], state["recv"], after)
    return res[:n], res[n:]


OTHER_CHIPS = (2, 4, 6)
SAME_CORE = (1, 2, 4, 6)


def _forward_copies(lands, send_sems, recv_sems, arriving):
    me = _place()
    sibling = _flip(me, 1)
    origin = sibling if arriving else me
    return [pltpu.make_async_remote_copy(
        src_ref=lands[a].at[_slot(_flip(origin, k))], dst_ref=lands[a].at[_slot(_flip(origin, k))],
        send_sem=send_sems.at[a * len(OTHER_CHIPS) + i], recv_sem=recv_sems.at[a * len(OTHER_CHIPS) + i],
        device_id=sibling, device_id_type=MESH) for a in range(len(lands)) for i, k in enumerate(OTHER_CHIPS)]


def forward_start(lands, after, name):
    n = len(lands)

    def body(*refs):
        lz = refs[:n]
        send_sems, recv_sems = refs[n + 1], refs[n + 2]
        token = refs[-1]
        for cp in _forward_copies(lz, send_sems, recv_sems, False):
            cp.start()
        token[...] = jnp.zeros_like(token)

    sems = pltpu.SemaphoreType.DMA((n * len(OTHER_CHIPS),))
    res = pl.pallas_call(
        body, name=name,
        out_shape=(sems, sems, *[pltpu.HBM(a.shape, a.dtype) for a in lands], jax.ShapeDtypeStruct((8, 128), F32)),
        in_specs=[HBM_SPEC] * n + [ANY],
        out_specs=(SEM_SPEC, SEM_SPEC, *[HBM_SPEC] * n, pl.BlockSpec(memory_space=pltpu.VMEM)),
        input_output_aliases={i: 2 + i for i in range(n)},
        compiler_params=pltpu.CompilerParams(has_side_effects=EFFECT),
    )(*[pltpu.with_memory_space_constraint(a, pltpu.HBM) for a in lands], after)
    return dict(send=res[0], recv=res[1], lands=res[2:2 + n], token=res[-1])


def forward_wait(state, after, name):
    n = len(state["lands"])

    def body(*refs):
        lz = refs[:n]
        send_sems, recv_sems = refs[n], refs[n + 1]
        for cp in _forward_copies(lz, send_sems, recv_sems, False):
            cp.wait_send()
        for cp in _forward_copies(lz, send_sems, recv_sems, True):
            cp.wait_recv()

    return pl.pallas_call(
        body, name=name, out_shape=tuple(pltpu.HBM(a.shape, a.dtype) for a in state["lands"]),
        in_specs=[HBM_SPEC] * n + [SEM_SPEC, SEM_SPEC, ANY], out_specs=tuple([HBM_SPEC] * n),
        input_output_aliases={i: i for i in range(n)},
        compiler_params=pltpu.CompilerParams(has_side_effects=EFFECT),
    )(*state["lands"], state["send"], state["recv"], after)


def adamw(w, m, v, parts, name):
    r, cdim = w.shape
    n_parts = parts.shape[0]
    rb = 256 if r % 256 == 0 else (SHARD_COLS // 4 if r == SHARD_COLS else r)

    def body(w_ref, m_ref, v_ref, p_ref, g_ref, d_ref, mo_ref, vo_ref):
        g = p_ref[0].astype(F32)
        for i in range(1, n_parts):
            g = g + p_ref[i].astype(F32)
        m_new = ADAM_B1 * m_ref[...] + (1.0 - ADAM_B1) * g
        v_new = ADAM_B2 * v_ref[...] + (1.0 - ADAM_B2) * (g * g)
        m_hat = m_new / (1.0 - ADAM_B1 ** ADAM_STEP)
        v_hat = v_new / (1.0 - ADAM_B2 ** ADAM_STEP)
        g_ref[...] = g
        d_ref[...] = -ADAM_LR * (m_hat / (jnp.sqrt(v_hat) + ADAM_EPS) + ADAM_WD * w_ref[...])
        mo_ref[...] = m_new
        vo_ref[...] = v_new

    blk = pl.BlockSpec((rb, cdim), lambda i: (i, 0))
    return pl.pallas_call(
        body, name=name, grid=(r // rb,),
        in_specs=[blk, blk, blk, pl.BlockSpec((n_parts, rb, cdim), lambda i: (0, i, 0))],
        out_specs=[blk] * 4, out_shape=[jax.ShapeDtypeStruct((r, cdim), F32)] * 4,
        compiler_params=_cparams(("arbitrary",)),
    )(w, m, v, parts)


def adamw_layers(w, m, v, lands, srcs, me, lo, prev, name):
    nl_all, r, cdim = w.shape
    nl = len(lands)
    rb = 256 if r % 256 == 0 else (SHARD_COLS // 4 if r == SHARD_COLS else r)
    nblk = r // rb
    n_prev = 0 if prev is None else 4

    def body(me_ref, w_ref, m_ref, v_ref, *rest):
        land_refs, src_refs = rest[:nl], rest[nl:2 * nl]
        g_ref, d_ref, mo_ref, vo_ref = rest[2 * nl + n_prev:]
        for k in range(nl):
            @pl.when(pl.program_id(0) == k)
            def _(k=k):
                own = src_refs[k][...].astype(F32)
                g = jnp.where(me_ref[0] == 0, own, land_refs[k][0].astype(F32))
                for i in range(1, N_DEV):
                    g = g + jnp.where(me_ref[0] == i, own, land_refs[k][i].astype(F32))
                m_new = ADAM_B1 * m_ref[...] + (1.0 - ADAM_B1) * g
                v_new = ADAM_B2 * v_ref[...] + (1.0 - ADAM_B2) * (g * g)
                m_hat = m_new / (1.0 - ADAM_B1 ** ADAM_STEP)
                v_hat = v_new / (1.0 - ADAM_B2 ** ADAM_STEP)
                g_ref[...] = g
                d_ref[...] = -ADAM_LR * (m_hat / (jnp.sqrt(v_hat) + ADAM_EPS) + ADAM_WD * w_ref[...])
                mo_ref[...] = m_new
                vo_ref[...] = v_new

    blk = pl.BlockSpec((None, rb, cdim), lambda l, i, me_ref: (lo + l, i, 0))

    def rows(l, i, k):
        return jnp.where(l < k, 0, jnp.where(l == k, i, nblk - 1))

    land_specs = [pl.BlockSpec((N_DEV, rb, cdim), lambda l, i, me_ref, k=k: (0, rows(l, i, k), 0)) for k in range(nl)]
    src_specs = [pl.BlockSpec((None, rb, cdim), lambda l, i, me_ref, k=k: (me_ref[0], rows(l, i, k), 0)) for k in range(nl)]
    return pl.pallas_call(
        body, name=name,
        grid_spec=pltpu.PrefetchScalarGridSpec(
            num_scalar_prefetch=1, grid=(nl, nblk),
            in_specs=[blk, blk, blk] + land_specs + src_specs + [ANY] * n_prev, out_specs=[blk] * 4),
        out_shape=[jax.ShapeDtypeStruct((nl_all, r, cdim), F32)] * 4,
        input_output_aliases={4 + 2 * nl + j: j for j in range(n_prev)},
        compiler_params=_cparams(("arbitrary", "arbitrary")),
    )(me, w, m, v, *lands, *srcs, *([] if prev is None else prev))


def small_sum(parts, fold, name):
    rows = parts.shape[1]

    def dot3(xv, sel):
        out = jnp.zeros((xv.shape[0], sel.shape[1]), F32)
        for _ in range(3):
            hi = xv.astype(BF16)
            out = out + _dot(hi, sel)
            xv = xv - hi.astype(F32)
        return out

    def body(p_ref, fold_ref, o_ref):
        tot = p_ref[0]
        for i in range(1, N_DEV):
            tot = tot + p_ref[i]
        o_ref[...] = tot
        for l in range(rows // SM_ROWS):
            blk = tot[l * SM_ROWS:l * SM_ROWS + 8, 0:D]
            folded = dot3(blk, fold_ref[...])
            o_ref[l * SM_ROWS + SM_QG_FOLDED:l * SM_ROWS + SM_QG_FOLDED + 1, 0:128] = folded[SM_QG:SM_QG + 1, :]
            o_ref[l * SM_ROWS + SM_KG_FOLDED:l * SM_ROWS + SM_KG_FOLDED + 1, 0:128] = folded[SM_KG:SM_KG + 1, :]

    return pl.pallas_call(
        body, name=name, out_shape=jax.ShapeDtypeStruct((rows, D), F32),
        compiler_params=_cparams(None),
    )(parts, fold)


def kernel(x, norm_g, w_in, conv_w, q_norm_g, k_norm_g, sinks, w_conv_out, w_attn_out, gate_b, w_out, loss_target, m_norm_g, m_w_in, m_conv_w, m_q_norm_g, m_k_norm_g, m_sinks, m_w_conv_out, m_w_attn_out, m_gate_b, m_w_out, v_norm_g, v_w_in, v_conv_w, v_q_norm_g, v_k_norm_g, v_sinks, v_w_conv_out, v_w_attn_out, v_gate_b, v_w_out):
    c = _selectors()
    me = 4 * lax.axis_index("x") + 2 * lax.axis_index("y") + lax.axis_index("c")

    w_in_t, m_w_in_t, v_w_in_t = (jnp.swapaxes(a, 1, 2) for a in (w_in, m_w_in, v_w_in))

    def shards(l):
        return [w_in_t[l].astype(BF16), w_conv_out[l].astype(BF16), w_attn_out[l].astype(BF16), w_out[l].astype(BF16)]

    def ici_start(l, after):
        return exchange_start(shards(l), False, SAME_CORE, after, f"gather_start_{l}")

    h = x[0]
    saved, lws = [], []
    lands = all_gather(shards(0) + [conv_w], h, "gather_0")
    conv_full = jnp.transpose(lands[4], (1, 2, 0, 3)).reshape(DEPTH, 3, D)
    ici = ici_start(1, lands[1])
    for l in range(DEPTH):
        lws.append(layer_operands(l, norm_g, conv_full, q_norm_g, k_norm_g, sinks, gate_b,
                                  {l: lands[0].reshape(IN_COLS, D)}, {l: lands[1].reshape(D, D)},
                                  {l: lands[2].reshape(D, D)}, {l: lands[3].reshape(D, D)}))
        ng_l = lws[l]["ng"] + ici["token"][0:1, 0:1] if l == 0 else lws[l]["ng"]
        u, hb = inproj_fwd(h, ng_l, lws[l]["w_in"], f"inproj_fwd_{l}")
        gb_l = lws[l]["gb"]
        if l + 1 < DEPTH:
            mine, arrived = exchange_wait(ici, u, f"gather_wait_{l + 1}")
            chip = forward_start(arrived, mine[0], f"gather_forward_start_{l + 1}")
            started = chip["token"]
            if l + 2 < DEPTH:
                ici = ici_start(l + 2, started)
                started = ici["token"]
            gb_l = gb_l + started[0:1, 0:1]
        x_in = h
        h, ya, yb = mixer_fwd(x_in, u, lws[l]["cw"], lws[l]["qg"], lws[l]["kg"], lws[l]["sinks"], gb_l,
                              lws[l]["wco"], lws[l]["wao"], lws[l]["wout"], c, f"mixer_fwd_{l}")
        saved.append((x_in, u, hb, ya, yb))
        if l + 1 < DEPTH:
            lands = forward_wait(chip, h, f"gather_forward_wait_{l + 1}")
            lands = [lax.dynamic_update_index_in_dim(land, src, me, 0) for land, src in zip(lands, mine)]
    dh, loss_part = loss_head(h, loss_target[0], "loss_head")

    grads, scatters = [None] * DEPTH, [[] for _ in range(DEPTH)]
    for l in reversed(range(DEPTH)):
        def send_off(g, done, l=l):
            rest = [g[k].reshape(N_DEV, SHARD_ROWS, D) for k in ("wco", "wao", "wout")] if done else []
            first = [g["w_in"].reshape(N_DEV, SHARD_COLS, D)] if done == (l > 0) else []
            if not first + rest:
                return None
            tag = f"{l}" if l > 0 else ("0_rest" if done else "0_in")
            scatters[l].append(exchange_start(first + rest, True, ALL_PEERS, g["small"], f"scatter_start_{tag}"))
            return scatters[l][-1]["token"]
        dh, grads[l] = layer_bwd(dh, saved[l], lws[l], c, l, send_off)

    me1 = me.astype(jnp.int32).reshape(1)
    mine, lands = {}, {}
    for l in (3, 2, 1):
        mine[l], lands[l] = exchange_wait(scatters[l][0], dh, f"scatter_wait_{l}")
    weights = [(w_in_t, m_w_in_t, v_w_in_t, "w_in"), (w_conv_out, m_w_conv_out, v_w_conv_out, "w_conv_out"),
               (w_attn_out, m_w_attn_out, v_w_attn_out, "w_attn_out"), (w_out, m_w_out, v_w_out, "w_out")]
    upd = [adamw_layers(w, m, v, [lands[l][i] for l in (1, 2, 3)], [mine[l][i] for l in (1, 2, 3)], me1, 1, None,
                        f"adamw_{n}_upper") for i, (w, m, v, n) in enumerate(weights)]
    blocks = []
    for l in range(DEPTH):
        blk = grads[l]["small"]
        blk = blk.at[SM_NORM, 0:D].set(grads[l]["dng"][0])
        if l == 0:
            blk = blk.at[SM_LOSS, 0:128].set(loss_part[0])
        blocks.append(blk)
    small_x = exchange_start([jnp.concatenate(blocks, axis=0)], False, ALL_PEERS, grads[0]["dng"], "gather_small_start")

    m_in, l_in = exchange_wait(scatters[0][0], upd[0][0], "scatter_wait_0_in")
    upd[0] = adamw_layers(*weights[0][:3], [l_in[0]], [m_in[0]], me1, 0, upd[0], "adamw_w_in_0")
    m_rest, l_rest = exchange_wait(scatters[0][1], upd[0][0], "scatter_wait_0_rest")
    for i in (1, 2, 3):
        upd[i] = adamw_layers(*weights[i][:3], [l_rest[i - 1]], [m_rest[i - 1]], me1, 0, upd[i], f"adamw_{weights[i][3]}_0")
    u_in = [jnp.swapaxes(o, 1, 2) for o in upd[0]]
    u_co, u_ao, u_out = upd[1], upd[2], upd[3]

    mine_s, lands_s = exchange_wait(small_x, u_out[0], "gather_small_wait")
    gathered = lax.dynamic_update_index_in_dim(lands_s[0], mine_s[0], me, 0)
    tot = small_sum(gathered, c["fold"], "small_sum")
    tot = tot.reshape(DEPTH, SM_ROWS, D)
    loss = tot[0, SM_LOSS, 0]

    def update_small(w, m, v, g, name):
        return adamw(w, m, v, g[None], name)

    u_ng = update_small(norm_g, m_norm_g, v_norm_g, tot[:, SM_NORM, 0:D], "adamw_norm_g")
    u_qg = update_small(q_norm_g, m_q_norm_g, v_q_norm_g, tot[:, SM_QG_FOLDED, 0:HEAD], "adamw_q_norm_g")
    u_kg = update_small(k_norm_g, m_k_norm_g, v_k_norm_g, tot[:, SM_KG_FOLDED, 0:HEAD], "adamw_k_norm_g")
    u_sk = update_small(sinks, m_sinks, v_sinks, tot[:, SM_SINK, 0:16], "adamw_sinks")
    g_gate = jnp.concatenate([tot[:, SM_GATE, :], tot[:, SM_GATE_B, :]], axis=1)
    u_gb = update_small(gate_b, m_gate_b, v_gate_b, g_gate, "adamw_gate_b")
    g_conv = lax.dynamic_slice_in_dim(tot[:, SM_CONV:SM_CONV + 3, 0:D], me * SHARD_ROWS, SHARD_ROWS, axis=2)
    u_cw = [o.reshape(DEPTH, 3, SHARD_ROWS) for o in update_small(
        conv_w.reshape(DEPTH * 3, SHARD_ROWS), m_conv_w.reshape(DEPTH * 3, SHARD_ROWS),
        v_conv_w.reshape(DEPTH * 3, SHARD_ROWS), g_conv.reshape(DEPTH * 3, SHARD_ROWS), "adamw_conv_w")]

    order = [u_ng, u_in, u_cw, u_qg, u_kg, u_sk, u_co, u_ao, u_gb, u_out]
    return (loss, dh[None], *[u[0] for u in order], *[u[1] for u in order], *[u[2] for u in order], *[u[3] for u in order])
```

```python
import functools

import jax
import jax.numpy as jnp
from jax import lax
from jax.experimental import pallas as pl
from jax.experimental.pallas import tpu as pltpu

F32 = jnp.float32
BF16 = jnp.bfloat16

N_DEV = 8
DEPTH = 4
D = 1024
N_KV = 4
GROUP = 4
HEAD = 64
BLK = 128
KVW = N_KV * HEAD
IN_COLS = 8704
SHARD_COLS = IN_COLS // N_DEV
SHARD_ROWS = D // N_DEV
C_VC, C_BC, C_CC, C_ZC, C_Q, C_K, C_V, C_ZA, C_GA, C_GB = 0, 1024, 2048, 3072, 4096, 5120, 5376, 5632, 6656, 7680
EPS = 1e-6
NEG_INF = -1e30
SCALE = HEAD ** -0.5
LOG2E = 1.4426950408889634
LN2 = 0.6931471805599453

ADAM_LR = 0.001
ADAM_B1 = 0.9
ADAM_B2 = 0.999
ADAM_EPS = 1e-08
ADAM_WD = 0.01
ADAM_STEP = 10

VMEM_LIMIT = 60 * 1024 * 1024
SM_ROWS = 16
SM_GATE, SM_CONV, SM_QG, SM_KG, SM_SINK, SM_NORM, SM_LOSS, SM_GATE_B, SM_QG_FOLDED, SM_KG_FOLDED = 0, 1, 4, 5, 6, 7, 8, 9, 10, 11


def _cparams(sem):
    return pltpu.CompilerParams(dimension_semantics=sem, vmem_limit_bytes=VMEM_LIMIT)


def _dot(a, b):
    return jnp.dot(a, b, preferred_element_type=F32)


def _dot_nt(a, b):
    return lax.dot_general(a, b, (((1,), (1,)), ((), ())), preferred_element_type=F32)


def _dot_tn(a, b):
    return lax.dot_general(a, b, (((0,), (0,)), ((), ())), preferred_element_type=F32)


def _dot2(x, sel):
    hi = x.astype(BF16)
    lo = (x - hi.astype(F32)).astype(BF16)
    return _dot(hi, sel) + _dot(lo, sel)


def _sigmoid(z):
    return 0.5 * jnp.tanh(0.5 * z) + 0.5


def _head_mean(t, sel, exp):
    return _dot2(_dot(t.astype(BF16), sel) * (1.0 / HEAD), exp)


def _shift_down(a, k, before):
    r = pltpu.roll(a, k, 0)
    row = lax.broadcasted_iota(jnp.int32, (8, 1), 0)
    head = jnp.where(row < k, pltpu.roll(before, k, 0), r[0:8, :])
    return jnp.concatenate([head, r[8:, :]], axis=0)


def _shift_up(a, k, after):
    n = a.shape[0]
    r = pltpu.roll(a, n - k, 0)
    row = lax.broadcasted_iota(jnp.int32, (8, 1), 0)
    tail = jnp.where(row >= 8 - k, pltpu.roll(after, 8 - k, 0), r[n - 8:, :])
    return jnp.concatenate([r[:n - 8, :], tail], axis=0)


def _bf(ref, c0, width):
    return ref[:, c0:c0 + width].astype(F32)


def _selectors():
    c = jnp.arange(D)
    sel_q = (c[:, None] // HEAD == jnp.arange(128)[None, :]).astype(BF16)
    ck = jnp.arange(KVW)
    sel_k = (ck[:, None] // HEAD == jnp.arange(128)[None, :]).astype(BF16)
    fold = (c[:, None] % HEAD == jnp.arange(128)[None, :]).astype(BF16)
    qq = jnp.arange(GROUP * BLK)[:, None] % BLK
    kk = jnp.arange(2 * BLK)[None, :]
    valid = (kk > qq) & (kk <= qq + BLK)
    bias = jnp.where(valid, 0.0, NEG_INF).astype(F32).T
    bias_first = jnp.where(valid & (kk >= BLK), 0.0, NEG_INF).astype(F32).T
    return dict(bias=bias, bias_first=bias_first,sel_q=sel_q, exp_q=sel_q.T, sel_k=sel_k, exp_k=sel_k.T, fold=fold)


def inproj_fwd(x, ng, w, name):
    t = x.shape[0]
    tm = min(1024, t)
    cb = 2176
    def body(x_ref, ng_ref, w_ref, u_ref, h_ref, h_scr):
        @pl.when(pl.program_id(1) == 0)
        def _():
            xf = x_ref[...]
            r = lax.rsqrt(jnp.mean(xf * xf, axis=-1, keepdims=True) + EPS)
            hb = (xf * r * ng_ref[...]).astype(BF16)
            h_scr[...] = hb
            h_ref[...] = hb
        u_ref[...] = _dot_nt(h_scr[...], w_ref[...]).astype(BF16)

    return pl.pallas_call(
        body, name=name, grid=(t // tm, IN_COLS // cb),
        in_specs=[pl.BlockSpec((tm, D), lambda i, j: (i, 0)), pl.BlockSpec((1, D), lambda i, j: (0, 0)),
                  pl.BlockSpec((cb, D), lambda i, j: (j, 0))],
        out_specs=[pl.BlockSpec((tm, cb), lambda i, j: (i, j)), pl.BlockSpec((tm, D), lambda i, j: (i, 0))],
        out_shape=[jax.ShapeDtypeStruct((t, IN_COLS), BF16), jax.ShapeDtypeStruct((t, D), BF16)],
        scratch_shapes=[pltpu.VMEM((tm, D), BF16)],
        compiler_params=_cparams(("arbitrary", "arbitrary")),
    )(x, ng, w)


def _conv_fwd(u_ref, uvc_prev, ucc_prev, cw_ref, is_first, tm):
    p = _bf(u_ref, C_CC, D) * _bf(u_ref, C_VC, D)
    pprev = ucc_prev[...].astype(F32) * uvc_prev[...].astype(F32)
    pprev = jnp.where(is_first, 0.0, pprev)
    p1 = _shift_down(p, 1, pprev[8:16, :])
    p2 = _shift_down(p, 2, pprev[8:16, :])
    cw = cw_ref[...]
    conv = cw[0:1, :] * p2 + cw[1:2, :] * p1 + cw[2:3, :] * p
    return p, p1, p2, conv


def _attn_inputs(u_ref, ukv_prev, qg_ref, kg_ref, c, tm):
    q = _bf(u_ref, C_Q, D)
    rq = lax.rsqrt(_head_mean(q * q, c["sel_q"][...], c["exp_q"][...]) + EPS)
    qhat = q * rq
    qn = (qhat * qg_ref[...]).astype(BF16)
    kband = jnp.concatenate([ukv_prev[:, 0:KVW].astype(F32), _bf(u_ref, C_K, KVW)], axis=0)
    rk = lax.rsqrt(_head_mean(kband * kband, c["sel_k"][...], c["exp_k"][...]) + EPS)
    khat = kband * rk
    knb = (khat * kg_ref[...]).astype(BF16)
    vband = jnp.concatenate([ukv_prev[:, KVW:2 * KVW], u_ref[:, C_V:C_V + KVW]], axis=0)
    return qhat, rq, qn, khat, rk, knb, vband


def _attn_masks(is_first, c):
    bias = c["bias"][...]
    bias_first = jnp.where(is_first, c["bias_first"][...], bias)
    lane_grp = lax.broadcasted_iota(jnp.int32, (BLK, KVW), 1) // HEAD
    query_grp = lax.broadcasted_iota(jnp.int32, (1, GROUP * BLK), 1) // BLK
    return bias, bias_first, lane_grp, query_grp


def _sink_row(sinks_ref, h, query_grp):
    row = jnp.full(query_grp.shape, sinks_ref[0, GROUP * h] * LOG2E, F32)
    for gi in range(1, GROUP):
        row = jnp.where(query_grp == gi, sinks_ref[0, GROUP * h + gi] * LOG2E, row)
    return row


def _lane_shift(a, groups):
    shift = (HEAD * groups) % KVW
    return a if shift == 0 else pltpu.roll(a, shift, 1)


def _stack_groups(a256, lane_grp, h):
    zero = jnp.zeros_like(a256)
    return jnp.concatenate([jnp.where(lane_grp == h, _lane_shift(a256, h - gi), zero) for gi in range(GROUP)], axis=0)


def _unstack_groups(a4, lane_grp, h):
    out = _lane_shift(a4[(GROUP - 1) * BLK:, :], GROUP - 1 - h)
    for gi in reversed(range(GROUP - 1)):
        out = jnp.where(lane_grp == gi, _lane_shift(a4[gi * BLK:(gi + 1) * BLK], gi - h), out)
    return out


def _band_sum(parts):
    pieces = [parts[0][0:BLK, :]]
    for b in range(1, len(parts)):
        pieces.append(parts[b - 1][BLK:, :] + parts[b][0:BLK, :])
    pieces.append(parts[-1][BLK:, :])
    return jnp.concatenate(pieces, axis=0)


def _softmax_block(qs, kt_b, bias, sink):
    s = _dot_nt(kt_b, qs) + bias
    m = jnp.maximum(jnp.max(s, axis=0, keepdims=True), sink)
    e = jnp.exp2(s - m)
    es = jnp.exp2(sink - m)
    inv = 1.0 / (jnp.sum(e, axis=0, keepdims=True) + es)
    return e * inv, es * inv


def _mixer_specs(t, tm, n_tiles, tile_of):
    nb = tm // BLK
    u_spec = pl.BlockSpec((tm, IN_COLS), lambda g: (tile_of(g), 0))
    ukv_prev = pl.BlockSpec((BLK, 2 * KVW), lambda g: (jnp.maximum(tile_of(g) * nb - 1, 0), C_K // (2 * KVW)))
    uvc_prev = pl.BlockSpec((16, D), lambda g: (jnp.maximum(tile_of(g) * (tm // 16) - 1, 0), C_VC // D))
    ucc_prev = pl.BlockSpec((16, D), lambda g: (jnp.maximum(tile_of(g) * (tm // 16) - 1, 0), C_CC // D))
    return u_spec, ukv_prev, uvc_prev, ucc_prev


def _full(shape):
    n = len(shape)
    return pl.BlockSpec(shape, lambda g: (0,) * n)


def mixer_fwd(x, u, cw, qg, kg, sinks, gb, wco, wao, wout, c, name):
    t = x.shape[0]
    tm = min(512, t)
    n_tiles = t // tm
    nb = tm // BLK
    cn = sorted(c)

    def body(x_ref, u_ref, ukv_prev, uvc_prev, ucc_prev, cw_ref, qg_ref, kg_ref, sinks_ref, gb_ref, wco_ref, wao_ref,
             wout_ref, *rest):
        cref = dict(zip(cn, rest[:len(cn)]))
        xo_ref, ya_ref, yb_ref = rest[len(cn):]
        is_first = pl.program_id(0) == 0
        _, _, _, conv = _conv_fwd(u_ref, uvc_prev, ucc_prev, cw_ref, is_first, tm)
        zc = _bf(u_ref, C_ZC, D)
        yc = _bf(u_ref, C_BC, D) * conv * (zc * _sigmoid(zc))
        ya = _dot(yc.astype(BF16), wco_ref[...])

        _, _, qn, _, _, knb, vband = _attn_inputs(u_ref, ukv_prev, qg_ref, kg_ref, cref, tm)
        bias, bias_first, lane_grp, query_grp = _attn_masks(is_first, cref)
        o_cols = []
        for h in range(N_KV):
            sink = _sink_row(sinks_ref, h, query_grp)
            o_rows = []
            for b in range(nb):
                qs = _stack_groups(qn[b * BLK:(b + 1) * BLK, h * KVW:(h + 1) * KVW], lane_grp, h)
                pn_t, _ = _softmax_block(qs, knb[b * BLK:(b + 2) * BLK], bias_first if b == 0 else bias, sink)
                o4 = _dot_tn(pn_t.astype(BF16), vband[b * BLK:(b + 2) * BLK])
                o_rows.append(_unstack_groups(o4, lane_grp, h))
            o_cols.append(jnp.concatenate(o_rows, axis=0))
        za = _bf(u_ref, C_ZA, D)
        ob = jnp.concatenate(o_cols, axis=1) * (za * _sigmoid(za))
        yb = _dot(ob.astype(BF16), wao_ref[...])

        g_a = _sigmoid(_bf(u_ref, C_GA, D) + gb_ref[:, 0:D])
        g_b = _sigmoid(_bf(u_ref, C_GB, D) + gb_ref[:, D:2 * D])
        merged = g_a * ya + g_b * yb
        xo_ref[...] = x_ref[...] + _dot(merged.astype(BF16), wout_ref[...])
        ya_ref[...] = ya.astype(BF16)
        yb_ref[...] = yb.astype(BF16)

    u_spec, ukv_prev, uvc_prev, ucc_prev = _mixer_specs(t, tm, n_tiles, lambda g: g)
    tok = pl.BlockSpec((tm, D), lambda g: (g, 0))
    consts = [c[k] for k in cn]
    return pl.pallas_call(
        body, name=name, grid=(n_tiles,),
        in_specs=[tok, u_spec, ukv_prev, uvc_prev, ucc_prev, _full((8, D)), _full((1, D)), _full((1, KVW)),
                  pl.BlockSpec(memory_space=pltpu.SMEM), _full((1, 2 * D)), _full((D, D)), _full((D, D)), _full((D, D))]
                 + [_full(a.shape) for a in consts],
        out_specs=[tok, tok, tok],
        out_shape=[jax.ShapeDtypeStruct((t, D), F32)] + [jax.ShapeDtypeStruct((t, D), BF16)] * 2,
        compiler_params=_cparams(("arbitrary",)),
    )(x, u, u, u, u, cw, qg, kg, sinks, gb, wco, wao, wout, *consts)


def mixer_bwd(dout, u, ya, yb, cw, qg, kg, sinks, gb, wco, wao, wout, c, name):
    t = dout.shape[0]
    tm = min(256, t)
    n_tiles = t // tm
    nb = tm // BLK
    kb = tm + BLK
    cn = sorted(c)

    def body(dout_ref, u_ref, ukv_prev, uvc_prev, ucc_prev, ya_ref, yb_ref, cw_ref, qg_ref, kg_ref, sinks_ref, gb_ref,
             wco_ref, wao_ref, wout_ref, *rest):
        cref = dict(zip(cn, rest[:len(cn)]))
        (du_ref, small_ref, merged_ref, yc_ref, ob_ref, dya_ref, dyb_ref, carry_kv, carry_conv) = rest[len(cn):]
        g = pl.program_id(0)
        is_first = g == n_tiles - 1

        @pl.when(g == 0)
        def _():
            carry_kv[...] = jnp.zeros_like(carry_kv)
            carry_conv[...] = jnp.zeros_like(carry_conv)
            small_ref[...] = jnp.zeros_like(small_ref)

        dout = dout_ref[...]
        dout_b = dout.astype(BF16)
        ya_v = ya_ref[...].astype(F32)
        yb_v = yb_ref[...].astype(F32)
        g_a = _sigmoid(_bf(u_ref, C_GA, D) + gb_ref[:, 0:D])
        g_b = _sigmoid(_bf(u_ref, C_GB, D) + gb_ref[:, D:2 * D])
        merged = g_a * ya_v + g_b * yb_v
        dmerged = _dot_nt(dout_b, wout_ref[...])
        merged_ref[...] = merged.astype(BF16)
        dya = dmerged * g_a
        dyb = dmerged * g_b
        dgl_a = dya * ya_v * (1.0 - g_a)
        dgl_b = dyb * yb_v * (1.0 - g_b)
        du_ref[:, C_GA:C_GA + D] = dgl_a.astype(BF16)
        du_ref[:, C_GB:C_GB + D] = dgl_b.astype(BF16)
        small_ref[SM_GATE:SM_GATE + 1, 0:D] += jnp.sum(dgl_a, axis=0, keepdims=True)
        small_ref[SM_GATE_B:SM_GATE_B + 1, 0:D] += jnp.sum(dgl_b, axis=0, keepdims=True)

        p, p1, p2, conv = _conv_fwd(u_ref, uvc_prev, ucc_prev, cw_ref, is_first, tm)
        zc = _bf(u_ref, C_ZC, D)
        bc = _bf(u_ref, C_BC, D)
        sg = _sigmoid(zc)
        sc = zc * sg
        bconv = bc * conv
        dya_b = dya.astype(BF16)
        yc_ref[...] = (bconv * sc).astype(BF16)
        dya_ref[...] = dya_b
        dyc = _dot_nt(dya_b, wco_ref[...])
        dyc_s = dyc * sc
        du_ref[:, C_BC:C_BC + D] = (dyc_s * conv).astype(BF16)
        du_ref[:, C_ZC:C_ZC + D] = (dyc * bconv * (sg + sc * (1.0 - sg))).astype(BF16)
        dconv = dyc_s * bc
        small_ref[SM_CONV + 2:SM_CONV + 3, 0:D] += jnp.sum(dconv * p, axis=0, keepdims=True)
        small_ref[SM_CONV + 1:SM_CONV + 2, 0:D] += jnp.sum(dconv * p1, axis=0, keepdims=True)
        small_ref[SM_CONV:SM_CONV + 1, 0:D] += jnp.sum(dconv * p2, axis=0, keepdims=True)
        nxt = carry_conv[...]
        d1 = _shift_up(dconv, 1, nxt)
        d2 = _shift_up(dconv, 2, nxt)
        carry_conv[...] = dconv[0:8, :]
        cw = cw_ref[...]
        dp = cw[2:3, :] * dconv + cw[1:2, :] * d1 + cw[0:1, :] * d2
        du_ref[:, C_CC:C_CC + D] = (dp * _bf(u_ref, C_VC, D)).astype(BF16)
        du_ref[:, C_VC:C_VC + D] = (dp * _bf(u_ref, C_CC, D)).astype(BF16)

        dyb_b = dyb.astype(BF16)
        dob = _dot_nt(dyb_b, wao_ref[...])
        za = _bf(u_ref, C_ZA, D)
        sga = _sigmoid(za)
        sa = za * sga
        do = dob * sa
        qhat, rq, qn, khat, rk, knb, vband = _attn_inputs(u_ref, ukv_prev, qg_ref, kg_ref, cref, tm)
        bias, bias_first, lane_grp, query_grp = _attn_masks(is_first, cref)
        lane16 = lax.broadcasted_iota(jnp.int32, (1, D), 1)
        dsink_row = jnp.zeros((1, D), F32)
        o_cols, dq_cols, dk4, dv4 = [], [], [], []
        for h in range(N_KV):
            sink = _sink_row(sinks_ref, h, query_grp)
            dsink = jnp.zeros((1, GROUP * BLK), F32)
            o_rows, dq_rows, dk_parts, dv_parts = [], [], [], []
            for b in range(nb):
                rows = slice(b * BLK, (b + 1) * BLK)
                band = slice(b * BLK, (b + 2) * BLK)
                cols = slice(h * KVW, (h + 1) * KVW)
                qs = _stack_groups(qn[rows, cols], lane_grp, h)
                pn_t, ps = _softmax_block(qs, knb[band], bias_first if b == 0 else bias, sink)
                pn_b = pn_t.astype(BF16)
                o_rows.append(_unstack_groups(_dot_tn(pn_b, vband[band]), lane_grp, h))
                dos = _stack_groups(do[rows, cols].astype(BF16), lane_grp, h)
                dpn_t = _dot_nt(vband[band], dos)
                delta = jnp.sum(pn_t * dpn_t, axis=0, keepdims=True)
                ds_t = (pn_t * (dpn_t - delta)).astype(BF16)
                dsink = dsink - ps * delta
                dq_rows.append(_unstack_groups(_dot_tn(ds_t, knb[band]), lane_grp, h))
                dk_parts.append(_dot(ds_t, qs))
                dv_parts.append(_dot(pn_b, dos))
            o_cols.append(jnp.concatenate(o_rows, axis=0))
            dq_cols.append(jnp.concatenate(dq_rows, axis=0))
            dk4.append(_band_sum(dk_parts))
            dv4.append(_band_sum(dv_parts))
            for gi in range(GROUP):
                tot = jnp.sum(dsink[:, gi * BLK:(gi + 1) * BLK], axis=1, keepdims=True)
                dsink_row = dsink_row + jnp.where(lane16 == GROUP * h + gi, tot, 0.0)
        small_ref[SM_SINK:SM_SINK + 1, :] += dsink_row

        o = jnp.concatenate(o_cols, axis=1)
        ob_ref[...] = (o * sa).astype(BF16)
        dyb_ref[...] = dyb_b
        du_ref[:, C_ZA:C_ZA + D] = (dob * o * (sga + sa * (1.0 - sga))).astype(BF16)

        dqn = jnp.concatenate(dq_cols, axis=1)
        small_ref[SM_QG:SM_QG + 1, 0:D] += SCALE * jnp.sum(dqn * qhat, axis=0, keepdims=True)
        dqh = dqn * (qg_ref[...] * LN2)
        dq = rq * (dqh - qhat * _head_mean(dqh * qhat, cref["sel_q"][...], cref["exp_q"][...]))
        du_ref[:, C_Q:C_Q + D] = dq.astype(BF16)

        dkn_band = (dk4[0] + dk4[1] + dk4[2] + dk4[3]) * LN2
        dv_band = dv4[0] + dv4[1] + dv4[2] + dv4[3]
        carried = carry_kv[...]
        pad = jnp.zeros((tm - BLK, KVW), F32)
        if nb > 1:
            dkn = dkn_band[BLK:, :] + jnp.concatenate([pad, carried[:, 0:KVW]], axis=0)
            dv = dv_band[BLK:, :] + jnp.concatenate([pad, carried[:, KVW:2 * KVW]], axis=0)
        else:
            dkn = dkn_band[BLK:, :] + carried[:, 0:KVW]
            dv = dv_band[BLK:, :] + carried[:, KVW:2 * KVW]
        carry_kv[:, 0:KVW] = dkn_band[0:BLK, :]
        carry_kv[:, KVW:2 * KVW] = dv_band[0:BLK, :]
        khat_t = khat[BLK:, :]
        small_ref[SM_KG:SM_KG + 1, 0:KVW] += jnp.sum(dkn * khat_t, axis=0, keepdims=True)
        dkh = dkn * kg_ref[...]
        dk = rk[BLK:, :] * (dkh - khat_t * _head_mean(dkh * khat_t, cref["sel_k"][...], cref["exp_k"][...]))
        du_ref[:, C_K:C_K + KVW] = dk.astype(BF16)
        du_ref[:, C_V:C_V + KVW] = dv.astype(BF16)

    rev = lambda g: n_tiles - 1 - g
    u_spec, ukv_prev, uvc_prev, ucc_prev = _mixer_specs(t, tm, n_tiles, rev)
    tok = pl.BlockSpec((tm, D), lambda g: (rev(g), 0))
    consts = [c[k] for k in cn]
    wspec = _full((D, D))
    return pl.pallas_call(
        body, name=name, grid=(n_tiles,),
        in_specs=[tok, u_spec, ukv_prev, uvc_prev, ucc_prev, tok, tok, _full((8, D)), _full((1, D)), _full((1, KVW)),
                  pl.BlockSpec(memory_space=pltpu.SMEM), _full((1, 2 * D)), wspec, wspec, wspec]
                 + [_full(a.shape) for a in consts],
        out_specs=[pl.BlockSpec((tm, IN_COLS), lambda g: (rev(g), 0)), _full((SM_ROWS, D))] + [tok] * 5,
        out_shape=[jax.ShapeDtypeStruct((t, IN_COLS), BF16), jax.ShapeDtypeStruct((SM_ROWS, D), F32)]
                  + [jax.ShapeDtypeStruct((t, D), BF16)] * 5,
        scratch_shapes=[pltpu.VMEM((BLK, 2 * KVW), F32), pltpu.VMEM((8, D), F32)],
        compiler_params=_cparams(("arbitrary",)),
    )(dout, u, u, u, u, ya, yb, cw, qg, kg, sinks, gb, wco, wao, wout, *consts)


def matmul_tn(a, b, name, after=None):
    t, m = a.shape
    tk = min(2048 if m == IN_COLS else 1024, t)
    mb = 2176 if m == IN_COLS else m
    nk = t // tk

    def body(a_ref, b_ref, *rest):
        o_ref, acc = rest[-2:]
        k = pl.program_id(1)
        prod = _dot_tn(a_ref[...].astype(BF16), b_ref[...].astype(BF16))

        @pl.when(k == 0)
        def _():
            acc[...] = prod

        @pl.when(k > 0)
        def _():
            acc[...] += prod

        @pl.when(k == nk - 1)
        def _():
            o_ref[...] = acc[...].astype(BF16)

    return pl.pallas_call(
        body, name=name, grid=(m // mb, nk),
        in_specs=[pl.BlockSpec((tk, mb), lambda j, k: (k, j)), pl.BlockSpec((tk, D), lambda j, k: (k, 0))]
                 + ([] if after is None else [ANY]),
        out_specs=pl.BlockSpec((mb, D), lambda j, k: (j, 0)),
        out_shape=jax.ShapeDtypeStruct((m, D), BF16),
        scratch_shapes=[pltpu.VMEM((mb, D), F32)],
        compiler_params=_cparams(("arbitrary", "arbitrary")),
    )(a, b, *([] if after is None else [after]))


def inproj_bwd_x(du, w, x, ng, dout, name):
    t = x.shape[0]
    tm = min(1024, t)
    kc = 2176
    nk = IN_COLS // kc

    def body(du_ref, w_ref, x_ref, ng_ref, dout_ref, dx_ref, dng_ref, acc):
        i = pl.program_id(0)
        k = pl.program_id(1)
        prod = _dot(du_ref[...], w_ref[...])

        @pl.when(k == 0)
        def _():
            acc[...] = prod

        @pl.when(k > 0)
        def _():
            acc[...] += prod

        @pl.when((i == 0) & (k == 0))
        def _():
            dng_ref[...] = jnp.zeros_like(dng_ref)

        @pl.when(k == nk - 1)
        def _():
            dh = acc[...]
            xf = x_ref[...]
            r = lax.rsqrt(jnp.mean(xf * xf, axis=-1, keepdims=True) + EPS)
            xhat = xf * r
            dng_ref[0:1, :] += jnp.sum(dh * xhat, axis=0, keepdims=True)
            dxh = dh * ng_ref[...]
            dx_ref[...] = dout_ref[...] + r * (dxh - xhat * jnp.mean(dxh * xhat, axis=-1, keepdims=True))

    tok = pl.BlockSpec((tm, D), lambda i, k: (i, 0))
    return pl.pallas_call(
        body, name=name, grid=(t // tm, nk),
        in_specs=[pl.BlockSpec((tm, kc), lambda i, k: (i, k)), pl.BlockSpec((kc, D), lambda i, k: (k, 0)), tok,
                  pl.BlockSpec((1, D), lambda i, k: (0, 0)), tok],
        out_specs=[tok, pl.BlockSpec((8, D), lambda i, k: (0, 0))],
        out_shape=[jax.ShapeDtypeStruct((t, D), F32), jax.ShapeDtypeStruct((8, D), F32)],
        scratch_shapes=[pltpu.VMEM((tm, D), F32)],
        compiler_params=_cparams(("arbitrary", "arbitrary")),
    )(du, w, x, ng, dout)


def loss_head(y, target, name):
    t = y.shape[0]
    tm = min(1024, t)

    def body(y_ref, t_ref, dy_ref, loss_ref):
        @pl.when(pl.program_id(0) == 0)
        def _():
            loss_ref[...] = jnp.zeros_like(loss_ref)
        err = y_ref[...] - t_ref[...]
        dy_ref[...] = err * (1.0 / D)
        part = jnp.sum(jnp.sum(err * err, axis=-1, keepdims=True) * (1.0 / D), axis=0, keepdims=True)
        loss_ref[...] += 0.5 * part

    tok = pl.BlockSpec((tm, D), lambda i: (i, 0))
    return pl.pallas_call(
        body, name=name, grid=(t // tm,), in_specs=[tok, tok],
        out_specs=[tok, pl.BlockSpec((8, 128), lambda i: (0, 0))],
        out_shape=[jax.ShapeDtypeStruct((t, D), F32), jax.ShapeDtypeStruct((8, 128), F32)],
        compiler_params=_cparams(("arbitrary",)),
    )(y, target)


def layer_operands(l, norm_g, conv_w_full, q_norm_g, k_norm_g, sinks, gate_b, w_in_b, wco_b, wao_b, wout_b):
    return dict(
        ng=norm_g[l][None, :], cw=jnp.pad(conv_w_full[l], ((0, 5), (0, 0))),
        qg=jnp.tile(q_norm_g[l] * (SCALE * LOG2E), D // HEAD)[None, :], kg=jnp.tile(k_norm_g[l], N_KV)[None, :],
        sinks=sinks[l][None, :], gb=gate_b[l][None, :],
        w_in=w_in_b[l], wco=wco_b[l], wao=wao_b[l], wout=wout_b[l])


def layer_bwd(dout, saved, lw, c, l, send_off):
    x, u, h, ya, yb = saved
    du, small, merged, yc, ob, dya, dyb = mixer_bwd(dout, u, ya, yb, lw["cw"], lw["qg"], lw["kg"], lw["sinks"], lw["gb"],
                                                    lw["wco"], lw["wao"], lw["wout"], c, f"mixer_bwd_{l}")
    grads = dict(w_in=matmul_tn(du, h, f"dw_in_{l}"), small=small)
    token = send_off(grads, False)
    grads["wout"] = matmul_tn(merged, dout, f"dw_out_{l}", after=token)
    grads["wco"] = matmul_tn(yc, dya, f"dw_conv_out_{l}")
    grads["wao"] = matmul_tn(ob, dyb, f"dw_attn_out_{l}")
    token = send_off(grads, True)
    dx, grads["dng"] = inproj_bwd_x(du, lw["w_in"], x, lw["ng"] + token[0:1, 0:1], dout, f"inproj_bwd_{l}")
    return dx, grads


MESH = pl.DeviceIdType.MESH
ANY = pl.BlockSpec(memory_space=pl.ANY)


def _place():
    return lax.axis_index("x"), lax.axis_index("y"), lax.axis_index("c")


def all_gather(arrs, after, name):
    n = len(arrs)

    def body(*refs):
        ins, outs = refs[:n], refs[n + 1:2 * n + 1]
        send_sems, recv_sems, local_sems = refs[2 * n + 1:]
        x, y, c = _place()
        me, sibling = (x, y, c), (x, y, 1 - c)
        chips = [(1 - x, y), (x, 1 - y), (1 - x, 1 - y)]

        def slot(a, block):
            px, py, pc = block
            return outs[a].at[4 * px + 2 * py + pc]

        def copy(a, k, block, to, src=None):
            return pltpu.make_async_remote_copy(
                src_ref=slot(a, block) if src is None else src, dst_ref=slot(a, block),
                send_sem=send_sems.at[a, k], recv_sem=recv_sems.at[a, k], device_id=to, device_id_type=MESH)

        mine = [pltpu.make_async_copy(ins[a], slot(a, me), local_sems.at[a]) for a in range(n)]
        for cp in mine:
            cp.start()
        first = []
        for a in range(n):
            first.append(copy(a, 0, me, sibling, src=ins[a]))
            first += [copy(a, 1 + j, me, (*chip, c), src=ins[a]) for j, chip in enumerate(chips)]
        for cp in first:
            cp.start()
        passed = []
        for j, chip in enumerate(chips):
            for a in range(n):
                copy(a, 1 + j, (*chip, c), me).wait_recv()
                passed.append(copy(a, 4 + j, (*chip, c), sibling))
                passed[-1].start()
        for a in range(n):
            copy(a, 0, sibling, me).wait_recv()
            for j, chip in enumerate(chips):
                copy(a, 4 + j, (*chip, 1 - c), me).wait_recv()
        for cp in first + passed:
            cp.wait_send()
        for cp in mine:
            cp.wait()

    return pl.pallas_call(
        body, name=name, in_specs=[ANY] * (n + 1), out_specs=[ANY] * n,
        out_shape=[jax.ShapeDtypeStruct((N_DEV,) + a.shape, a.dtype) for a in arrs],
        scratch_shapes=[pltpu.SemaphoreType.DMA((n, 7)), pltpu.SemaphoreType.DMA((n, 7)), pltpu.SemaphoreType.DMA((n,))],
    )(*arrs, after)


HBM_SPEC = pl.BlockSpec(memory_space=pltpu.HBM)
SEM_SPEC = pl.BlockSpec(memory_space=pltpu.SEMAPHORE)
EFFECT = pltpu.SideEffectType.DATAFLOW_SIDE_EFFECTING


ALL_PEERS = (1, 2, 3, 4, 5, 6, 7)


def _flip(place, k):
    x, y, c = place
    return (1 - x if k & 4 else x, 1 - y if k & 2 else y, 1 - c if k & 1 else c)


def _slot(place):
    return 4 * place[0] + 2 * place[1] + place[2]


def _exchange_copies(ins, lands, send_sems, recv_sems, scatter, flips, arriving):
    me = _place()
    out = []
    for a in range(len(ins)):
        for i, k in enumerate(flips):
            peer = _flip(me, k)
            out.append(pltpu.make_async_remote_copy(
                src_ref=ins[a].at[_slot(peer)] if scatter else ins[a],
                dst_ref=lands[a].at[_slot(peer) if arriving else _slot(me)],
                send_sem=send_sems.at[a * len(flips) + i], recv_sem=recv_sems.at[a * len(flips) + i],
                device_id=peer, device_id_type=MESH))
    return out


def exchange_start(arrs, scatter, flips, after, name):
    n = len(arrs)
    lands = [lax.empty(a.shape if scatter else (N_DEV,) + a.shape, a.dtype) for a in arrs]

    def body(*refs):
        ins, lz = refs[:n], refs[n:2 * n]
        send_sems, recv_sems = refs[2 * n + 1], refs[2 * n + 2]
        token = refs[-1]
        for cp in _exchange_copies(ins, lz, send_sems, recv_sems, scatter, flips, False):
            cp.start()
        token[...] = jnp.zeros_like(token)

    sems = pltpu.SemaphoreType.DMA((n * len(flips),))
    res = pl.pallas_call(
        body, name=name,
        out_shape=(sems, sems, *[pltpu.HBM(a.shape, a.dtype) for a in arrs], *[pltpu.HBM(a.shape, a.dtype) for a in lands],
                   jax.ShapeDtypeStruct((8, 128), F32)),
        in_specs=[HBM_SPEC] * (2 * n) + [ANY],
        out_specs=(SEM_SPEC, SEM_SPEC, *[HBM_SPEC] * (2 * n), pl.BlockSpec(memory_space=pltpu.VMEM)),
        input_output_aliases={i: 2 + i for i in range(2 * n)},
        compiler_params=pltpu.CompilerParams(has_side_effects=EFFECT),
    )(*[pltpu.with_memory_space_constraint(a, pltpu.HBM) for a in arrs],
      *[pltpu.with_memory_space_constraint(a, pltpu.HBM) for a in lands], after)
    return dict(send=res[0], recv=res[1], srcs=res[2:2 + n], lands=res[2 + n:2 + 2 * n], token=res[-1], scatter=scatter,
                flips=flips)


def exchange_wait(state, after, name):
    n = len(state["srcs"])

    def body(*refs):
        ins, lz = refs[:n], refs[n:2 * n]
        send_sems, recv_sems = refs[2 * n], refs[2 * n + 1]
        for cp in _exchange_copies(ins, lz, send_sems, recv_sems, state["scatter"], state["flips"], True):
            cp.wait_send()
            cp.wait_recv()

    both = list(state["srcs"]) + list(state["lands"])
    res = pl.pallas_call(
        body, name=name, out_shape=tuple(pltpu.HBM(a.shape, a.dtype) for a in both),
        in_specs=[HBM_SPEC] * (2 * n) + [SEM_SPEC, SEM_SPEC, ANY], out_specs=tuple([HBM_SPEC] * (2 * n)),
        input_output_aliases={i: i for i in range(2 * n)},
        compiler_params=pltpu.CompilerParams(has_side_effects=EFFECT),
    )(*both, state["send"], state["recv"], after)
    return res[:n], res[n:]


OTHER_CHIPS = (2, 4, 6)
SAME_CORE = (1, 2, 4, 6)


def _forward_copies(lands, send_sems, recv_sems, arriving):
    me = _place()
    sibling = _flip(me, 1)
    origin = sibling if arriving else me
    return [pltpu.make_async_remote_copy(
        src_ref=lands[a].at[_slot(_flip(origin, k))], dst_ref=lands[a].at[_slot(_flip(origin, k))],
        send_sem=send_sems.at[a * len(OTHER_CHIPS) + i], recv_sem=recv_sems.at[a * len(OTHER_CHIPS) + i],
        device_id=sibling, device_id_type=MESH) for a in range(len(lands)) for i, k in enumerate(OTHER_CHIPS)]


def forward_start(lands, after, name):
    n = len(lands)

    def body(*refs):
        lz = refs[:n]
        send_sems, recv_sems = refs[n + 1], refs[n + 2]
        token = refs[-1]
        for cp in _forward_copies(lz, send_sems, recv_sems, False):
            cp.start()
        token[...] = jnp.zeros_like(token)

    sems = pltpu.SemaphoreType.DMA((n * len(OTHER_CHIPS),))
    res = pl.pallas_call(
        body, name=name,
        out_shape=(sems, sems, *[pltpu.HBM(a.shape, a.dtype) for a in lands], jax.ShapeDtypeStruct((8, 128), F32)),
        in_specs=[HBM_SPEC] * n + [ANY],
        out_specs=(SEM_SPEC, SEM_SPEC, *[HBM_SPEC] * n, pl.BlockSpec(memory_space=pltpu.VMEM)),
        input_output_aliases={i: 2 + i for i in range(n)},
        compiler_params=pltpu.CompilerParams(has_side_effects=EFFECT),
    )(*[pltpu.with_memory_space_constraint(a, pltpu.HBM) for a in lands], after)
    return dict(send=res[0], recv=res[1], lands=res[2:2 + n], token=res[-1])


def forward_wait(state, after, name):
    n = len(state["lands"])

    def body(*refs):
        lz = refs[:n]
        send_sems, recv_sems = refs[n], refs[n + 1]
        for cp in _forward_copies(lz, send_sems, recv_sems, False):
            cp.wait_send()
        for cp in _forward_copies(lz, send_sems, recv_sems, True):
            cp.wait_recv()

    return pl.pallas_call(
        body, name=name, out_shape=tuple(pltpu.HBM(a.shape, a.dtype) for a in state["lands"]),
        in_specs=[HBM_SPEC] * n + [SEM_SPEC, SEM_SPEC, ANY], out_specs=tuple([HBM_SPEC] * n),
        input_output_aliases={i: i for i in range(n)},
        compiler_params=pltpu.CompilerParams(has_side_effects=EFFECT),
    )(*state["lands"], state["send"], state["recv"], after)


def adamw(w, m, v, parts, name):
    r, cdim = w.shape
    n_parts = parts.shape[0]
    rb = 256 if r % 256 == 0 else (SHARD_COLS // 4 if r == SHARD_COLS else r)

    def body(w_ref, m_ref, v_ref, p_ref, g_ref, d_ref, mo_ref, vo_ref):
        g = p_ref[0].astype(F32)
        for i in range(1, n_parts):
            g = g + p_ref[i].astype(F32)
        m_new = ADAM_B1 * m_ref[...] + (1.0 - ADAM_B1) * g
        v_new = ADAM_B2 * v_ref[...] + (1.0 - ADAM_B2) * (g * g)
        m_hat = m_new / (1.0 - ADAM_B1 ** ADAM_STEP)
        v_hat = v_new / (1.0 - ADAM_B2 ** ADAM_STEP)
        g_ref[...] = g
        d_ref[...] = -ADAM_LR * (m_hat / (jnp.sqrt(v_hat) + ADAM_EPS) + ADAM_WD * w_ref[...])
        mo_ref[...] = m_new
        vo_ref[...] = v_new

    blk = pl.BlockSpec((rb, cdim), lambda i: (i, 0))
    return pl.pallas_call(
        body, name=name, grid=(r // rb,),
        in_specs=[blk, blk, blk, pl.BlockSpec((n_parts, rb, cdim), lambda i: (0, i, 0))],
        out_specs=[blk] * 4, out_shape=[jax.ShapeDtypeStruct((r, cdim), F32)] * 4,
        compiler_params=_cparams(("arbitrary",)),
    )(w, m, v, parts)


def adamw_layers(w, m, v, lands, srcs, me, lo, prev, name):
    nl_all, r, cdim = w.shape
    nl = len(lands)
    rb = 256 if r % 256 == 0 else (SHARD_COLS // 4 if r == SHARD_COLS else r)
    nblk = r // rb
    n_prev = 0 if prev is None else 4

    def body(me_ref, w_ref, m_ref, v_ref, *rest):
        land_refs, src_refs = rest[:nl], rest[nl:2 * nl]
        g_ref, d_ref, mo_ref, vo_ref = rest[2 * nl + n_prev:]
        for k in range(nl):
            @pl.when(pl.program_id(0) == k)
            def _(k=k):
                own = src_refs[k][...].astype(F32)
                g = jnp.where(me_ref[0] == 0, own, land_refs[k][0].astype(F32))
                for i in range(1, N_DEV):
                    g = g + jnp.where(me_ref[0] == i, own, land_refs[k][i].astype(F32))
                m_new = ADAM_B1 * m_ref[...] + (1.0 - ADAM_B1) * g
                v_new = ADAM_B2 * v_ref[...] + (1.0 - ADAM_B2) * (g * g)
                m_hat = m_new / (1.0 - ADAM_B1 ** ADAM_STEP)
                v_hat = v_new / (1.0 - ADAM_B2 ** ADAM_STEP)
                g_ref[...] = g
                d_ref[...] = -ADAM_LR * (m_hat / (jnp.sqrt(v_hat) + ADAM_EPS) + ADAM_WD * w_ref[...])
                mo_ref[...] = m_new
                vo_ref[...] = v_new

    blk = pl.BlockSpec((None, rb, cdim), lambda l, i, me_ref: (lo + l, i, 0))

    def rows(l, i, k):
        return jnp.where(l < k, 0, jnp.where(l == k, i, nblk - 1))

    land_specs = [pl.BlockSpec((N_DEV, rb, cdim), lambda l, i, me_ref, k=k: (0, rows(l, i, k), 0)) for k in range(nl)]
    src_specs = [pl.BlockSpec((None, rb, cdim), lambda l, i, me_ref, k=k: (me_ref[0], rows(l, i, k), 0)) for k in range(nl)]
    return pl.pallas_call(
        body, name=name,
        grid_spec=pltpu.PrefetchScalarGridSpec(
            num_scalar_prefetch=1, grid=(nl, nblk),
            in_specs=[blk, blk, blk] + land_specs + src_specs + [ANY] * n_prev, out_specs=[blk] * 4),
        out_shape=[jax.ShapeDtypeStruct((nl_all, r, cdim), F32)] * 4,
        input_output_aliases={4 + 2 * nl + j: j for j in range(n_prev)},
        compiler_params=_cparams(("arbitrary", "arbitrary")),
    )(me, w, m, v, *lands, *srcs, *([] if prev is None else prev))


def small_sum(parts, fold, name):
    rows = parts.shape[1]

    def dot3(xv, sel):
        out = jnp.zeros((xv.shape[0], sel.shape[1]), F32)
        for _ in range(3):
            hi = xv.astype(BF16)
            out = out + _dot(hi, sel)
            xv = xv - hi.astype(F32)
        return out

    def body(p_ref, fold_ref, o_ref):
        tot = p_ref[0]
        for i in range(1, N_DEV):
            tot = tot + p_ref[i]
        o_ref[...] = tot
        for l in range(rows // SM_ROWS):
            blk = tot[l * SM_ROWS:l * SM_ROWS + 8, 0:D]
            folded = dot3(blk, fold_ref[...])
            o_ref[l * SM_ROWS + SM_QG_FOLDED:l * SM_ROWS + SM_QG_FOLDED + 1, 0:128] = folded[SM_QG:SM_QG + 1, :]
            o_ref[l * SM_ROWS + SM_KG_FOLDED:l * SM_ROWS + SM_KG_FOLDED + 1, 0:128] = folded[SM_KG:SM_KG + 1, :]

    return pl.pallas_call(
        body, name=name, out_shape=jax.ShapeDtypeStruct((rows, D), F32),
        compiler_params=_cparams(None),
    )(parts, fold)


def kernel(x, norm_g, w_in, conv_w, q_norm_g, k_norm_g, sinks, w_conv_out, w_attn_out, gate_b, w_out, loss_target, m_norm_g, m_w_in, m_conv_w, m_q_norm_g, m_k_norm_g, m_sinks, m_w_conv_out, m_w_attn_out, m_gate_b, m_w_out, v_norm_g, v_w_in, v_conv_w, v_q_norm_g, v_k_norm_g, v_sinks, v_w_conv_out, v_w_attn_out, v_gate_b, v_w_out):
    c = _selectors()
    me = 4 * lax.axis_index("x") + 2 * lax.axis_index("y") + lax.axis_index("c")

    w_in_t, m_w_in_t, v_w_in_t = (jnp.swapaxes(a, 1, 2) for a in (w_in, m_w_in, v_w_in))

    def shards(l):
        return [w_in_t[l].astype(BF16), w_conv_out[l].astype(BF16), w_attn_out[l].astype(BF16), w_out[l].astype(BF16)]

    def ici_start(l, after):
        return exchange_start(shards(l), False, SAME_CORE, after, f"gather_start_{l}")

    h = x[0]
    saved, lws = [], []
    lands = all_gather(shards(0) + [conv_w], h, "gather_0")
    conv_full = jnp.transpose(lands[4], (1, 2, 0, 3)).reshape(DEPTH, 3, D)
    ici = ici_start(1, lands[1])
    for l in range(DEPTH):
        lws.append(layer_operands(l, norm_g, conv_full, q_norm_g, k_norm_g, sinks, gate_b,
                                  {l: lands[0].reshape(IN_COLS, D)}, {l: lands[1].reshape(D, D)},
                                  {l: lands[2].reshape(D, D)}, {l: lands[3].reshape(D, D)}))
        ng_l = lws[l]["ng"] + ici["token"][0:1, 0:1] if l == 0 else lws[l]["ng"]
        u, hb = inproj_fwd(h, ng_l, lws[l]["w_in"], f"inproj_fwd_{l}")
        gb_l = lws[l]["gb"]
        if l + 1 < DEPTH:
            mine, arrived = exchange_wait(ici, u, f"gather_wait_{l + 1}")
            chip = forward_start(arrived, mine[0], f"gather_forward_start_{l + 1}")
            started = chip["token"]
            if l + 2 < DEPTH:
                ici = ici_start(l + 2, started)
                started = ici["token"]
            gb_l = gb_l + started[0:1, 0:1]
        x_in = h
        h, ya, yb = mixer_fwd(x_in, u, lws[l]["cw"], lws[l]["qg"], lws[l]["kg"], lws[l]["sinks"], gb_l,
                              lws[l]["wco"], lws[l]["wao"], lws[l]["wout"], c, f"mixer_fwd_{l}")
        saved.append((x_in, u, hb, ya, yb))
        if l + 1 < DEPTH:
            lands = forward_wait(chip, h, f"gather_forward_wait_{l + 1}")
            lands = [lax.dynamic_update_index_in_dim(land, src, me, 0) for land, src in zip(lands, mine)]
    dh, loss_part = loss_head(h, loss_target[0], "loss_head")

    grads, scatters = [None] * DEPTH, [[] for _ in range(DEPTH)]
    for l in reversed(range(DEPTH)):
        def send_off(g, done, l=l):
            rest = [g[k].reshape(N_DEV, SHARD_ROWS, D) for k in ("wco", "wao", "wout")] if done else []
            first = [g["w_in"].reshape(N_DEV, SHARD_COLS, D)] if done == (l > 0) else []
            if not first + rest:
                return None
            tag = f"{l}" if l > 0 else ("0_rest" if done else "0_in")
            scatters[l].append(exchange_start(first + rest, True, ALL_PEERS, g["small"], f"scatter_start_{tag}"))
            return scatters[l][-1]["token"]
        dh, grads[l] = layer_bwd(dh, saved[l], lws[l], c, l, send_off)

    me1 = me.astype(jnp.int32).reshape(1)
    mine, lands = {}, {}
    for l in (3, 2, 1):
        mine[l], lands[l] = exchange_wait(scatters[l][0], dh, f"scatter_wait_{l}")
    weights = [(w_in_t, m_w_in_t, v_w_in_t, "w_in"), (w_conv_out, m_w_conv_out, v_w_conv_out, "w_conv_out"),
               (w_attn_out, m_w_attn_out, v_w_attn_out, "w_attn_out"), (w_out, m_w_out, v_w_out, "w_out")]
    upd = [adamw_layers(w, m, v, [lands[l][i] for l in (1, 2, 3)], [mine[l][i] for l in (1, 2, 3)], me1, 1, None,
                        f"adamw_{n}_upper") for i, (w, m, v, n) in enumerate(weights)]
    blocks = []
    for l in range(DEPTH):
        blk = grads[l]["small"]
        blk = blk.at[SM_NORM, 0:D].set(grads[l]["dng"][0])
        if l == 0:
            blk = blk.at[SM_LOSS, 0:128].set(loss_part[0])
        blocks.append(blk)
    small_x = exchange_start([jnp.concatenate(blocks, axis=0)], False, ALL_PEERS, grads[0]["dng"], "gather_small_start")

    m_in, l_in = exchange_wait(scatters[0][0], upd[0][0], "scatter_wait_0_in")
    upd[0] = adamw_layers(*weights[0][:3], [l_in[0]], [m_in[0]], me1, 0, upd[0], "adamw_w_in_0")
    m_rest, l_rest = exchange_wait(scatters[0][1], upd[0][0], "scatter_wait_0_rest")
    for i in (1, 2, 3):
        upd[i] = adamw_layers(*weights[i][:3], [l_rest[i - 1]], [m_rest[i - 1]], me1, 0, upd[i], f"adamw_{weights[i][3]}_0")
    u_in = [jnp.swapaxes(o, 1, 2) for o in upd[0]]
    u_co, u_ao, u_out = upd[1], upd[2], upd[3]

    mine_s, lands_s = exchange_wait(small_x, u_out[0], "gather_small_wait")
    gathered = lax.dynamic_update_index_in_dim(lands_s[0], mine_s[0], me, 0)
    tot = small_sum(gathered, c["fold"], "small_sum")
    tot = tot.reshape(DEPTH, SM_ROWS, D)
    loss = tot[0, SM_LOSS, 0]

    def update_small(w, m, v, g, name):
        return adamw(w, m, v, g[None], name)

    u_ng = update_small(norm_g, m_norm_g, v_norm_g, tot[:, SM_NORM, 0:D], "adamw_norm_g")
    u_qg = update_small(q_norm_g, m_q_norm_g, v_q_norm_g, tot[:, SM_QG_FOLDED, 0:HEAD], "adamw_q_norm_g")
    u_kg = update_small(k_norm_g, m_k_norm_g, v_k_norm_g, tot[:, SM_KG_FOLDED, 0:HEAD], "adamw_k_norm_g")
    u_sk = update_small(sinks, m_sinks, v_sinks, tot[:, SM_SINK, 0:16], "adamw_sinks")
    g_gate = jnp.concatenate([tot[:, SM_GATE, :], tot[:, SM_GATE_B, :]], axis=1)
    u_gb = update_small(gate_b, m_gate_b, v_gate_b, g_gate, "adamw_gate_b")
    g_conv = lax.dynamic_slice_in_dim(tot[:, SM_CONV:SM_CONV + 3, 0:D], me * SHARD_ROWS, SHARD_ROWS, axis=2)
    u_cw = [o.reshape(DEPTH, 3, SHARD_ROWS) for o in update_small(
        conv_w.reshape(DEPTH * 3, SHARD_ROWS), m_conv_w.reshape(DEPTH * 3, SHARD_ROWS),
        v_conv_w.reshape(DEPTH * 3, SHARD_ROWS), g_conv.reshape(DEPTH * 3, SHARD_ROWS), "adamw_conv_w")]

    order = [u_ng, u_in, u_cw, u_qg, u_kg, u_sk, u_co, u_ao, u_gb, u_out]
    return (loss, dh[None], *[u[0] for u in order], *[u[1] for u in order], *[u[2] for u in order], *[u[3] for u in order])
```

```python
import functools

import jax
import jax.numpy as jnp
from jax import lax
from jax.experimental import pallas as pl
from jax.experimental.pallas import tpu as pltpu

F32 = jnp.float32
BF16 = jnp.bfloat16

N_DEV = 8
DEPTH = 4
D = 1024
N_KV = 4
GROUP = 4
HEAD = 64
BLK = 128
KVW = N_KV * HEAD
IN_COLS = 8704
SHARD_COLS = IN_COLS // N_DEV
SHARD_ROWS = D // N_DEV
C_VC, C_BC, C_CC, C_ZC, C_Q, C_K, C_V, C_ZA, C_GA, C_GB = 0, 1024, 2048, 3072, 4096, 5120, 5376, 5632, 6656, 7680
EPS = 1e-6
NEG_INF = -1e30
SCALE = HEAD ** -0.5
LOG2E = 1.4426950408889634
LN2 = 0.6931471805599453

ADAM_LR = 0.001
ADAM_B1 = 0.9
ADAM_B2 = 0.999
ADAM_EPS = 1e-08
ADAM_WD = 0.01
ADAM_STEP = 10

VMEM_LIMIT = 60 * 1024 * 1024
SM_ROWS = 16
SM_GATE, SM_CONV, SM_QG, SM_KG, SM_SINK, SM_NORM, SM_LOSS, SM_GATE_B, SM_QG_FOLDED, SM_KG_FOLDED = 0, 1, 4, 5, 6, 7, 8, 9, 10, 11


def _cparams(sem):
    return pltpu.CompilerParams(dimension_semantics=sem, vmem_limit_bytes=VMEM_LIMIT)


def _dot(a, b):
    return jnp.dot(a, b, preferred_element_type=F32)


def _dot_nt(a, b):
    return lax.dot_general(a, b, (((1,), (1,)), ((), ())), preferred_element_type=F32)


def _dot_tn(a, b):
    return lax.dot_general(a, b, (((0,), (0,)), ((), ())), preferred_element_type=F32)


def _dot2(x, sel):
    hi = x.astype(BF16)
    lo = (x - hi.astype(F32)).astype(BF16)
    return _dot(hi, sel) + _dot(lo, sel)


def _sigmoid(z):
    return 0.5 * jnp.tanh(0.5 * z) + 0.5


def _head_mean(t, sel, exp):
    return _dot2(_dot(t.astype(BF16), sel) * (1.0 / HEAD), exp)


def _shift_down(a, k, before):
    r = pltpu.roll(a, k, 0)
    row = lax.broadcasted_iota(jnp.int32, (8, 1), 0)
    head = jnp.where(row < k, pltpu.roll(before, k, 0), r[0:8, :])
    return jnp.concatenate([head, r[8:, :]], axis=0)


def _shift_up(a, k, after):
    n = a.shape[0]
    r = pltpu.roll(a, n - k, 0)
    row = lax.broadcasted_iota(jnp.int32, (8, 1), 0)
    tail = jnp.where(row >= 8 - k, pltpu.roll(after, 8 - k, 0), r[n - 8:, :])
    return jnp.concatenate([r[:n - 8, :], tail], axis=0)


def _bf(ref, c0, width):
    return ref[:, c0:c0 + width].astype(F32)


def _selectors():
    c = jnp.arange(D)
    sel_q = (c[:, None] // HEAD == jnp.arange(128)[None, :]).astype(BF16)
    ck = jnp.arange(KVW)
    sel_k = (ck[:, None] // HEAD == jnp.arange(128)[None, :]).astype(BF16)
    fold = (c[:, None] % HEAD == jnp.arange(128)[None, :]).astype(BF16)
    qq = jnp.arange(GROUP * BLK)[:, None] % BLK
    kk = jnp.arange(2 * BLK)[None, :]
    valid = (kk > qq) & (kk <= qq + BLK)
    bias = jnp.where(valid, 0.0, NEG_INF).astype(F32).T
    bias_first = jnp.where(valid & (kk >= BLK), 0.0, NEG_INF).astype(F32).T
    return dict(bias=bias, bias_first=bias_first,sel_q=sel_q, exp_q=sel_q.T, sel_k=sel_k, exp_k=sel_k.T, fold=fold)


def inproj_fwd(x, ng, w, name):
    t = x.shape[0]
    tm = min(1024, t)
    cb = 4352
    def body(x_ref, ng_ref, w_ref, u_ref, h_ref, h_scr):
        @pl.when(pl.program_id(1) == 0)
        def _():
            xf = x_ref[...]
            r = lax.rsqrt(jnp.mean(xf * xf, axis=-1, keepdims=True) + EPS)
            hb = (xf * r * ng_ref[...]).astype(BF16)
            h_scr[...] = hb
            h_ref[...] = hb
        u_ref[...] = _dot_nt(h_scr[...], w_ref[...]).astype(BF16)

    return pl.pallas_call(
        body, name=name, grid=(t // tm, IN_COLS // cb),
        in_specs=[pl.BlockSpec((tm, D), lambda i, j: (i, 0)), pl.BlockSpec((1, D), lambda i, j: (0, 0)),
                  pl.BlockSpec((cb, D), lambda i, j: (j, 0))],
        out_specs=[pl.BlockSpec((tm, cb), lambda i, j: (i, j)), pl.BlockSpec((tm, D), lambda i, j: (i, 0))],
        out_shape=[jax.ShapeDtypeStruct((t, IN_COLS), BF16), jax.ShapeDtypeStruct((t, D), BF16)],
        scratch_shapes=[pltpu.VMEM((tm, D), BF16)],
        compiler_params=_cparams(("arbitrary", "arbitrary")),
    )(x, ng, w)


def _conv_fwd(u_ref, uvc_prev, ucc_prev, cw_ref, is_first, tm):
    p = _bf(u_ref, C_CC, D) * _bf(u_ref, C_VC, D)
    pprev = ucc_prev[...].astype(F32) * uvc_prev[...].astype(F32)
    pprev = jnp.where(is_first, 0.0, pprev)
    p1 = _shift_down(p, 1, pprev[8:16, :])
    p2 = _shift_down(p, 2, pprev[8:16, :])
    cw = cw_ref[...]
    conv = cw[0:1, :] * p2 + cw[1:2, :] * p1 + cw[2:3, :] * p
    return p, p1, p2, conv


def _attn_inputs(u_ref, ukv_prev, qg_ref, kg_ref, c, tm):
    q = _bf(u_ref, C_Q, D)
    rq = lax.rsqrt(_head_mean(q * q, c["sel_q"][...], c["exp_q"][...]) + EPS)
    qhat = q * rq
    qn = (qhat * qg_ref[...]).astype(BF16)
    kband = jnp.concatenate([ukv_prev[:, 0:KVW].astype(F32), _bf(u_ref, C_K, KVW)], axis=0)
    rk = lax.rsqrt(_head_mean(kband * kband, c["sel_k"][...], c["exp_k"][...]) + EPS)
    khat = kband * rk
    knb = (khat * kg_ref[...]).astype(BF16)
    vband = jnp.concatenate([ukv_prev[:, KVW:2 * KVW], u_ref[:, C_V:C_V + KVW]], axis=0)
    return qhat, rq, qn, khat, rk, knb, vband


def _attn_masks(is_first, c):
    bias = c["bias"][...]
    bias_first = jnp.where(is_first, c["bias_first"][...], bias)
    lane_grp = lax.broadcasted_iota(jnp.int32, (BLK, KVW), 1) // HEAD
    query_grp = lax.broadcasted_iota(jnp.int32, (1, GROUP * BLK), 1) // BLK
    return bias, bias_first, lane_grp, query_grp


def _sink_row(sinks_ref, h, query_grp):
    row = jnp.full(query_grp.shape, sinks_ref[0, GROUP * h] * LOG2E, F32)
    for gi in range(1, GROUP):
        row = jnp.where(query_grp == gi, sinks_ref[0, GROUP * h + gi] * LOG2E, row)
    return row


def _lane_shift(a, groups):
    shift = (HEAD * groups) % KVW
    return a if shift == 0 else pltpu.roll(a, shift, 1)


def _stack_groups(a256, lane_grp, h):
    zero = jnp.zeros_like(a256)
    return jnp.concatenate([jnp.where(lane_grp == h, _lane_shift(a256, h - gi), zero) for gi in range(GROUP)], axis=0)


def _unstack_groups(a4, lane_grp, h):
    out = _lane_shift(a4[(GROUP - 1) * BLK:, :], GROUP - 1 - h)
    for gi in reversed(range(GROUP - 1)):
        out = jnp.where(lane_grp == gi, _lane_shift(a4[gi * BLK:(gi + 1) * BLK], gi - h), out)
    return out


def _band_sum(parts):
    pieces = [parts[0][0:BLK, :]]
    for b in range(1, len(parts)):
        pieces.append(parts[b - 1][BLK:, :] + parts[b][0:BLK, :])
    pieces.append(parts[-1][BLK:, :])
    return jnp.concatenate(pieces, axis=0)


def _softmax_block(qs, kt_b, bias, sink):
    s = _dot_nt(kt_b, qs) + bias
    m = jnp.maximum(jnp.max(s, axis=0, keepdims=True), sink)
    e = jnp.exp2(s - m)
    es = jnp.exp2(sink - m)
    inv = 1.0 / (jnp.sum(e, axis=0, keepdims=True) + es)
    return e * inv, es * inv


def _mixer_specs(t, tm, n_tiles, tile_of):
    nb = tm // BLK
    u_spec = pl.BlockSpec((tm, IN_COLS), lambda g: (tile_of(g), 0))
    ukv_prev = pl.BlockSpec((BLK, 2 * KVW), lambda g: (jnp.maximum(tile_of(g) * nb - 1, 0), C_K // (2 * KVW)))
    uvc_prev = pl.BlockSpec((16, D), lambda g: (jnp.maximum(tile_of(g) * (tm // 16) - 1, 0), C_VC // D))
    ucc_prev = pl.BlockSpec((16, D), lambda g: (jnp.maximum(tile_of(g) * (tm // 16) - 1, 0), C_CC // D))
    return u_spec, ukv_prev, uvc_prev, ucc_prev


def _full(shape):
    n = len(shape)
    return pl.BlockSpec(shape, lambda g: (0,) * n)


def mixer_fwd(x, u, cw, qg, kg, sinks, gb, wco, wao, wout, c, name):
    t = x.shape[0]
    tm = min(512, t)
    n_tiles = t // tm
    nb = tm // BLK
    cn = sorted(c)

    def body(x_ref, u_ref, ukv_prev, uvc_prev, ucc_prev, cw_ref, qg_ref, kg_ref, sinks_ref, gb_ref, wco_ref, wao_ref,
             wout_ref, *rest):
        cref = dict(zip(cn, rest[:len(cn)]))
        xo_ref, ya_ref, yb_ref = rest[len(cn):]
        is_first = pl.program_id(0) == 0
        _, _, _, conv = _conv_fwd(u_ref, uvc_prev, ucc_prev, cw_ref, is_first, tm)
        zc = _bf(u_ref, C_ZC, D)
        yc = _bf(u_ref, C_BC, D) * conv * (zc * _sigmoid(zc))
        ya = _dot(yc.astype(BF16), wco_ref[...])

        _, _, qn, _, _, knb, vband = _attn_inputs(u_ref, ukv_prev, qg_ref, kg_ref, cref, tm)
        bias, bias_first, lane_grp, query_grp = _attn_masks(is_first, cref)
        o_cols = []
        for h in range(N_KV):
            sink = _sink_row(sinks_ref, h, query_grp)
            o_rows = []
            for b in range(nb):
                qs = _stack_groups(qn[b * BLK:(b + 1) * BLK, h * KVW:(h + 1) * KVW], lane_grp, h)
                pn_t, _ = _softmax_block(qs, knb[b * BLK:(b + 2) * BLK], bias_first if b == 0 else bias, sink)
                o4 = _dot_tn(pn_t.astype(BF16), vband[b * BLK:(b + 2) * BLK])
                o_rows.append(_unstack_groups(o4, lane_grp, h))
            o_cols.append(jnp.concatenate(o_rows, axis=0))
        za = _bf(u_ref, C_ZA, D)
        ob = jnp.concatenate(o_cols, axis=1) * (za * _sigmoid(za))
        yb = _dot(ob.astype(BF16), wao_ref[...])

        g_a = _sigmoid(_bf(u_ref, C_GA, D) + gb_ref[:, 0:D])
        g_b = _sigmoid(_bf(u_ref, C_GB, D) + gb_ref[:, D:2 * D])
        merged = g_a * ya + g_b * yb
        xo_ref[...] = x_ref[...] + _dot(merged.astype(BF16), wout_ref[...])
        ya_ref[...] = ya.astype(BF16)
        yb_ref[...] = yb.astype(BF16)

    u_spec, ukv_prev, uvc_prev, ucc_prev = _mixer_specs(t, tm, n_tiles, lambda g: g)
    tok = pl.BlockSpec((tm, D), lambda g: (g, 0))
    consts = [c[k] for k in cn]
    return pl.pallas_call(
        body, name=name, grid=(n_tiles,),
        in_specs=[tok, u_spec, ukv_prev, uvc_prev, ucc_prev, _full((8, D)), _full((1, D)), _full((1, KVW)),
                  pl.BlockSpec(memory_space=pltpu.SMEM), _full((1, 2 * D)), _full((D, D)), _full((D, D)), _full((D, D))]
                 + [_full(a.shape) for a in consts],
        out_specs=[tok, tok, tok],
        out_shape=[jax.ShapeDtypeStruct((t, D), F32)] + [jax.ShapeDtypeStruct((t, D), BF16)] * 2,
        compiler_params=_cparams(("arbitrary",)),
    )(x, u, u, u, u, cw, qg, kg, sinks, gb, wco, wao, wout, *consts)


def mixer_bwd(dout, u, ya, yb, cw, qg, kg, sinks, gb, wco, wao, wout, c, name):
    t = dout.shape[0]
    tm = min(256, t)
    n_tiles = t // tm
    nb = tm // BLK
    kb = tm + BLK
    cn = sorted(c)

    def body(dout_ref, u_ref, ukv_prev, uvc_prev, ucc_prev, ya_ref, yb_ref, cw_ref, qg_ref, kg_ref, sinks_ref, gb_ref,
             wco_ref, wao_ref, wout_ref, *rest):
        cref = dict(zip(cn, rest[:len(cn)]))
        (du_ref, small_ref, merged_ref, yc_ref, ob_ref, dya_ref, dyb_ref, carry_kv, carry_conv) = rest[len(cn):]
        g = pl.program_id(0)
        is_first = g == n_tiles - 1

        @pl.when(g == 0)
        def _():
            carry_kv[...] = jnp.zeros_like(carry_kv)
            carry_conv[...] = jnp.zeros_like(carry_conv)
            small_ref[...] = jnp.zeros_like(small_ref)

        dout = dout_ref[...]
        dout_b = dout.astype(BF16)
        ya_v = ya_ref[...].astype(F32)
        yb_v = yb_ref[...].astype(F32)
        g_a = _sigmoid(_bf(u_ref, C_GA, D) + gb_ref[:, 0:D])
        g_b = _sigmoid(_bf(u_ref, C_GB, D) + gb_ref[:, D:2 * D])
        merged = g_a * ya_v + g_b * yb_v
        dmerged = _dot_nt(dout_b, wout_ref[...])
        merged_ref[...] = merged.astype(BF16)
        dya = dmerged * g_a
        dyb = dmerged * g_b
        dgl_a = dya * ya_v * (1.0 - g_a)
        dgl_b = dyb * yb_v * (1.0 - g_b)
        du_ref[:, C_GA:C_GA + D] = dgl_a.astype(BF16)
        du_ref[:, C_GB:C_GB + D] = dgl_b.astype(BF16)
        small_ref[SM_GATE:SM_GATE + 1, 0:D] += jnp.sum(dgl_a, axis=0, keepdims=True)
        small_ref[SM_GATE_B:SM_GATE_B + 1, 0:D] += jnp.sum(dgl_b, axis=0, keepdims=True)

        p, p1, p2, conv = _conv_fwd(u_ref, uvc_prev, ucc_prev, cw_ref, is_first, tm)
        zc = _bf(u_ref, C_ZC, D)
        bc = _bf(u_ref, C_BC, D)
        sg = _sigmoid(zc)
        sc = zc * sg
        bconv = bc * conv
        dya_b = dya.astype(BF16)
        yc_ref[...] = (bconv * sc).astype(BF16)
        dya_ref[...] = dya_b
        dyc = _dot_nt(dya_b, wco_ref[...])
        dyc_s = dyc * sc
        du_ref[:, C_BC:C_BC + D] = (dyc_s * conv).astype(BF16)
        du_ref[:, C_ZC:C_ZC + D] = (dyc * bconv * (sg + sc * (1.0 - sg))).astype(BF16)
        dconv = dyc_s * bc
        small_ref[SM_CONV + 2:SM_CONV + 3, 0:D] += jnp.sum(dconv * p, axis=0, keepdims=True)
        small_ref[SM_CONV + 1:SM_CONV + 2, 0:D] += jnp.sum(dconv * p1, axis=0, keepdims=True)
        small_ref[SM_CONV:SM_CONV + 1, 0:D] += jnp.sum(dconv * p2, axis=0, keepdims=True)
        nxt = carry_conv[...]
        d1 = _shift_up(dconv, 1, nxt)
        d2 = _shift_up(dconv, 2, nxt)
        carry_conv[...] = dconv[0:8, :]
        cw = cw_ref[...]
        dp = cw[2:3, :] * dconv + cw[1:2, :] * d1 + cw[0:1, :] * d2
        du_ref[:, C_CC:C_CC + D] = (dp * _bf(u_ref, C_VC, D)).astype(BF16)
        du_ref[:, C_VC:C_VC + D] = (dp * _bf(u_ref, C_CC, D)).astype(BF16)

        dyb_b = dyb.astype(BF16)
        dob = _dot_nt(dyb_b, wao_ref[...])
        za = _bf(u_ref, C_ZA, D)
        sga = _sigmoid(za)
        sa = za * sga
        do = dob * sa
        qhat, rq, qn, khat, rk, knb, vband = _attn_inputs(u_ref, ukv_prev, qg_ref, kg_ref, cref, tm)
        bias, bias_first, lane_grp, query_grp = _attn_masks(is_first, cref)
        lane16 = lax.broadcasted_iota(jnp.int32, (1, D), 1)
        dsink_row = jnp.zeros((1, D), F32)
        o_cols, dq_cols, dk4, dv4 = [], [], [], []
        for h in range(N_KV):
            sink = _sink_row(sinks_ref, h, query_grp)
            dsink = jnp.zeros((1, GROUP * BLK), F32)
            o_rows, dq_rows, dk_parts, dv_parts = [], [], [], []
            for b in range(nb):
                rows = slice(b * BLK, (b + 1) * BLK)
                band = slice(b * BLK, (b + 2) * BLK)
                cols = slice(h * KVW, (h + 1) * KVW)
                qs = _stack_groups(qn[rows, cols], lane_grp, h)
                pn_t, ps = _softmax_block(qs, knb[band], bias_first if b == 0 else bias, sink)
                pn_b = pn_t.astype(BF16)
                o_rows.append(_unstack_groups(_dot_tn(pn_b, vband[band]), lane_grp, h))
                dos = _stack_groups(do[rows, cols].astype(BF16), lane_grp, h)
                dpn_t = _dot_nt(vband[band], dos)
                delta = jnp.sum(pn_t * dpn_t, axis=0, keepdims=True)
                ds_t = (pn_t * (dpn_t - delta)).astype(BF16)
                dsink = dsink - ps * delta
                dq_rows.append(_unstack_groups(_dot_tn(ds_t, knb[band]), lane_grp, h))
                dk_parts.append(_dot(ds_t, qs))
                dv_parts.append(_dot(pn_b, dos))
            o_cols.append(jnp.concatenate(o_rows, axis=0))
            dq_cols.append(jnp.concatenate(dq_rows, axis=0))
            dk4.append(_band_sum(dk_parts))
            dv4.append(_band_sum(dv_parts))
            for gi in range(GROUP):
                tot = jnp.sum(dsink[:, gi * BLK:(gi + 1) * BLK], axis=1, keepdims=True)
                dsink_row = dsink_row + jnp.where(lane16 == GROUP * h + gi, tot, 0.0)
        small_ref[SM_SINK:SM_SINK + 1, :] += dsink_row

        o = jnp.concatenate(o_cols, axis=1)
        ob_ref[...] = (o * sa).astype(BF16)
        dyb_ref[...] = dyb_b
        du_ref[:, C_ZA:C_ZA + D] = (dob * o * (sga + sa * (1.0 - sga))).astype(BF16)

        dqn = jnp.concatenate(dq_cols, axis=1)
        small_ref[SM_QG:SM_QG + 1, 0:D] += SCALE * jnp.sum(dqn * qhat, axis=0, keepdims=True)
        dqh = dqn * (qg_ref[...] * LN2)
        dq = rq * (dqh - qhat * _head_mean(dqh * qhat, cref["sel_q"][...], cref["exp_q"][...]))
        du_ref[:, C_Q:C_Q + D] = dq.astype(BF16)

        dkn_band = (dk4[0] + dk4[1] + dk4[2] + dk4[3]) * LN2
        dv_band = dv4[0] + dv4[1] + dv4[2] + dv4[3]
        carried = carry_kv[...]
        pad = jnp.zeros((tm - BLK, KVW), F32)
        if nb > 1:
            dkn = dkn_band[BLK:, :] + jnp.concatenate([pad, carried[:, 0:KVW]], axis=0)
            dv = dv_band[BLK:, :] + jnp.concatenate([pad, carried[:, KVW:2 * KVW]], axis=0)
        else:
            dkn = dkn_band[BLK:, :] + carried[:, 0:KVW]
            dv = dv_band[BLK:, :] + carried[:, KVW:2 * KVW]
        carry_kv[:, 0:KVW] = dkn_band[0:BLK, :]
        carry_kv[:, KVW:2 * KVW] = dv_band[0:BLK, :]
        khat_t = khat[BLK:, :]
        small_ref[SM_KG:SM_KG + 1, 0:KVW] += jnp.sum(dkn * khat_t, axis=0, keepdims=True)
        dkh = dkn * kg_ref[...]
        dk = rk[BLK:, :] * (dkh - khat_t * _head_mean(dkh * khat_t, cref["sel_k"][...], cref["exp_k"][...]))
        du_ref[:, C_K:C_K + KVW] = dk.astype(BF16)
        du_ref[:, C_V:C_V + KVW] = dv.astype(BF16)

    rev = lambda g: n_tiles - 1 - g
    u_spec, ukv_prev, uvc_prev, ucc_prev = _mixer_specs(t, tm, n_tiles, rev)
    tok = pl.BlockSpec((tm, D), lambda g: (rev(g), 0))
    consts = [c[k] for k in cn]
    wspec = _full((D, D))
    return pl.pallas_call(
        body, name=name, grid=(n_tiles,),
        in_specs=[tok, u_spec, ukv_prev, uvc_prev, ucc_prev, tok, tok, _full((8, D)), _full((1, D)), _full((1, KVW)),
                  pl.BlockSpec(memory_space=pltpu.SMEM), _full((1, 2 * D)), wspec, wspec, wspec]
                 + [_full(a.shape) for a in consts],
        out_specs=[pl.BlockSpec((tm, IN_COLS), lambda g: (rev(g), 0)), _full((SM_ROWS, D))] + [tok] * 5,
        out_shape=[jax.ShapeDtypeStruct((t, IN_COLS), BF16), jax.ShapeDtypeStruct((SM_ROWS, D), F32)]
                  + [jax.ShapeDtypeStruct((t, D), BF16)] * 5,
        scratch_shapes=[pltpu.VMEM((BLK, 2 * KVW), F32), pltpu.VMEM((8, D), F32)],
        compiler_params=_cparams(("arbitrary",)),
    )(dout, u, u, u, u, ya, yb, cw, qg, kg, sinks, gb, wco, wao, wout, *consts)


def matmul_tn(a, b, name, after=None):
    t, m = a.shape
    tk = min(2048 if m == IN_COLS else 1024, t)
    mb = 2176 if m == IN_COLS else m
    nk = t // tk

    def body(a_ref, b_ref, *rest):
        o_ref, acc = rest[-2:]
        k = pl.program_id(1)
        prod = _dot_tn(a_ref[...].astype(BF16), b_ref[...].astype(BF16))

        @pl.when(k == 0)
        def _():
            acc[...] = prod

        @pl.when(k > 0)
        def _():
            acc[...] += prod

        @pl.when(k == nk - 1)
        def _():
            o_ref[...] = acc[...].astype(BF16)

    return pl.pallas_call(
        body, name=name, grid=(m // mb, nk),
        in_specs=[pl.BlockSpec((tk, mb), lambda j, k: (k, j)), pl.BlockSpec((tk, D), lambda j, k: (k, 0))]
                 + ([] if after is None else [ANY]),
        out_specs=pl.BlockSpec((mb, D), lambda j, k: (j, 0)),
        out_shape=jax.ShapeDtypeStruct((m, D), BF16),
        scratch_shapes=[pltpu.VMEM((mb, D), F32)],
        compiler_params=_cparams(("arbitrary", "arbitrary")),
    )(a, b, *([] if after is None else [after]))


def inproj_bwd_x(du, w, x, ng, dout, name):
    t = x.shape[0]
    tm = min(1024, t)
    kc = 2176
    nk = IN_COLS // kc

    def body(du_ref, w_ref, x_ref, ng_ref, dout_ref, dx_ref, dng_ref, acc):
        i = pl.program_id(0)
        k = pl.program_id(1)
        prod = _dot(du_ref[...], w_ref[...])

        @pl.when(k == 0)
        def _():
            acc[...] = prod

        @pl.when(k > 0)
        def _():
            acc[...] += prod

        @pl.when((i == 0) & (k == 0))
        def _():
            dng_ref[...] = jnp.zeros_like(dng_ref)

        @pl.when(k == nk - 1)
        def _():
            dh = acc[...]
            xf = x_ref[...]
            r = lax.rsqrt(jnp.mean(xf * xf, axis=-1, keepdims=True) + EPS)
            xhat = xf * r
            dng_ref[0:1, :] += jnp.sum(dh * xhat, axis=0, keepdims=True)
            dxh = dh * ng_ref[...]
            dx_ref[...] = dout_ref[...] + r * (dxh - xhat * jnp.mean(dxh * xhat, axis=-1, keepdims=True))

    tok = pl.BlockSpec((tm, D), lambda i, k: (i, 0))
    return pl.pallas_call(
        body, name=name, grid=(t // tm, nk),
        in_specs=[pl.BlockSpec((tm, kc), lambda i, k: (i, k)), pl.BlockSpec((kc, D), lambda i, k: (k, 0)), tok,
                  pl.BlockSpec((1, D), lambda i, k: (0, 0)), tok],
        out_specs=[tok, pl.BlockSpec((8, D), lambda i, k: (0, 0))],
        out_shape=[jax.ShapeDtypeStruct((t, D), F32), jax.ShapeDtypeStruct((8, D), F32)],
        scratch_shapes=[pltpu.VMEM((tm, D), F32)],
        compiler_params=_cparams(("arbitrary", "arbitrary")),
    )(du, w, x, ng, dout)


def loss_head(y, target, name):
    t = y.shape[0]
    tm = min(1024, t)

    def body(y_ref, t_ref, dy_ref, loss_ref):
        @pl.when(pl.program_id(0) == 0)
        def _():
            loss_ref[...] = jnp.zeros_like(loss_ref)
        err = y_ref[...] - t_ref[...]
        dy_ref[...] = err * (1.0 / D)
        part = jnp.sum(jnp.sum(err * err, axis=-1, keepdims=True) * (1.0 / D), axis=0, keepdims=True)
        loss_ref[...] += 0.5 * part

    tok = pl.BlockSpec((tm, D), lambda i: (i, 0))
    return pl.pallas_call(
        body, name=name, grid=(t // tm,), in_specs=[tok, tok],
        out_specs=[tok, pl.BlockSpec((8, 128), lambda i: (0, 0))],
        out_shape=[jax.ShapeDtypeStruct((t, D), F32), jax.ShapeDtypeStruct((8, 128), F32)],
        compiler_params=_cparams(("arbitrary",)),
    )(y, target)


def layer_operands(l, norm_g, conv_w_full, q_norm_g, k_norm_g, sinks, gate_b, w_in_b, wco_b, wao_b, wout_b):
    return dict(
        ng=norm_g[l][None, :], cw=jnp.pad(conv_w_full[l], ((0, 5), (0, 0))),
        qg=jnp.tile(q_norm_g[l] * (SCALE * LOG2E), D // HEAD)[None, :], kg=jnp.tile(k_norm_g[l], N_KV)[None, :],
        sinks=sinks[l][None, :], gb=gate_b[l][None, :],
        w_in=w_in_b[l], wco=wco_b[l], wao=wao_b[l], wout=wout_b[l])


def layer_bwd(dout, saved, lw, c, l, send_off):
    x, u, h, ya, yb = saved
    du, small, merged, yc, ob, dya, dyb = mixer_bwd(dout, u, ya, yb, lw["cw"], lw["qg"], lw["kg"], lw["sinks"], lw["gb"],
                                                    lw["wco"], lw["wao"], lw["wout"], c, f"mixer_bwd_{l}")
    grads = dict(w_in=matmul_tn(du, h, f"dw_in_{l}"), small=small)
    token = send_off(grads, False)
    grads["wout"] = matmul_tn(merged, dout, f"dw_out_{l}", after=token)
    grads["wco"] = matmul_tn(yc, dya, f"dw_conv_out_{l}")
    grads["wao"] = matmul_tn(ob, dyb, f"dw_attn_out_{l}")
    token = send_off(grads, True)
    dx, grads["dng"] = inproj_bwd_x(du, lw["w_in"], x, lw["ng"] + token[0:1, 0:1], dout, f"inproj_bwd_{l}")
    return dx, grads


MESH = pl.DeviceIdType.MESH
ANY = pl.BlockSpec(memory_space=pl.ANY)


def _place():
    return lax.axis_index("x"), lax.axis_index("y"), lax.axis_index("c")


def all_gather(arrs, after, name):
    n = len(arrs)

    def body(*refs):
        ins, outs = refs[:n], refs[n + 1:2 * n + 1]
        send_sems, recv_sems, local_sems = refs[2 * n + 1:]
        x, y, c = _place()
        me, sibling = (x, y, c), (x, y, 1 - c)
        chips = [(1 - x, y), (x, 1 - y), (1 - x, 1 - y)]

        def slot(a, block):
            px, py, pc = block
            return outs[a].at[4 * px + 2 * py + pc]

        def copy(a, k, block, to, src=None):
            return pltpu.make_async_remote_copy(
                src_ref=slot(a, block) if src is None else src, dst_ref=slot(a, block),
                send_sem=send_sems.at[a, k], recv_sem=recv_sems.at[a, k], device_id=to, device_id_type=MESH)

        mine = [pltpu.make_async_copy(ins[a], slot(a, me), local_sems.at[a]) for a in range(n)]
        for cp in mine:
            cp.start()
        first = []
        for a in range(n):
            first.append(copy(a, 0, me, sibling, src=ins[a]))
            first += [copy(a, 1 + j, me, (*chip, c), src=ins[a]) for j, chip in enumerate(chips)]
        for cp in first:
            cp.start()
        passed = []
        for j, chip in enumerate(chips):
            for a in range(n):
                copy(a, 1 + j, (*chip, c), me).wait_recv()
                passed.append(copy(a, 4 + j, (*chip, c), sibling))
                passed[-1].start()
        for a in range(n):
            copy(a, 0, sibling, me).wait_recv()
            for j, chip in enumerate(chips):
                copy(a, 4 + j, (*chip, 1 - c), me).wait_recv()
        for cp in first + passed:
            cp.wait_send()
        for cp in mine:
            cp.wait()

    return pl.pallas_call(
        body, name=name, in_specs=[ANY] * (n + 1), out_specs=[ANY] * n,
        out_shape=[jax.ShapeDtypeStruct((N_DEV,) + a.shape, a.dtype) for a in arrs],
        scratch_shapes=[pltpu.SemaphoreType.DMA((n, 7)), pltpu.SemaphoreType.DMA((n, 7)), pltpu.SemaphoreType.DMA((n,))],
    )(*arrs, after)


HBM_SPEC = pl.BlockSpec(memory_space=pltpu.HBM)
SEM_SPEC = pl.BlockSpec(memory_space=pltpu.SEMAPHORE)
EFFECT = pltpu.SideEffectType.DATAFLOW_SIDE_EFFECTING


ALL_PEERS = (1, 2, 3, 4, 5, 6, 7)


def _flip(place, k):
    x, y, c = place
    return (1 - x if k & 4 else x, 1 - y if k & 2 else y, 1 - c if k & 1 else c)


def _slot(place):
    return 4 * place[0] + 2 * place[1] + place[2]


def _exchange_copies(ins, lands, send_sems, recv_sems, scatter, flips, arriving):
    me = _place()
    out = []
    for a in range(len(ins)):
        for i, k in enumerate(flips):
            peer = _flip(me, k)
            out.append(pltpu.make_async_remote_copy(
                src_ref=ins[a].at[_slot(peer)] if scatter else ins[a],
                dst_ref=lands[a].at[_slot(peer) if arriving else _slot(me)],
                send_sem=send_sems.at[a * len(flips) + i], recv_sem=recv_sems.at[a * len(flips) + i],
                device_id=peer, device_id_type=MESH))
    return out


def exchange_start(arrs, scatter, flips, after, name):
    n = len(arrs)
    lands = [lax.empty(a.shape if scatter else (N_DEV,) + a.shape, a.dtype) for a in arrs]

    def body(*refs):
        ins, lz = refs[:n], refs[n:2 * n]
        send_sems, recv_sems = refs[2 * n + 1], refs[2 * n + 2]
        token = refs[-1]
        for cp in _exchange_copies(ins, lz, send_sems, recv_sems, scatter, flips, False):
            cp.start()
        token[...] = jnp.zeros_like(token)

    sems = pltpu.SemaphoreType.DMA((n * len(flips),))
    res = pl.pallas_call(
        body, name=name,
        out_shape=(sems, sems, *[pltpu.HBM(a.shape, a.dtype) for a in arrs], *[pltpu.HBM(a.shape, a.dtype) for a in lands],
                   jax.ShapeDtypeStruct((8, 128), F32)),
        in_specs=[HBM_SPEC] * (2 * n) + [ANY],
        out_specs=(SEM_SPEC, SEM_SPEC, *[HBM_SPEC] * (2 * n), pl.BlockSpec(memory_space=pltpu.VMEM)),
        input_output_aliases={i: 2 + i for i in range(2 * n)},
        compiler_params=pltpu.CompilerParams(has_side_effects=EFFECT),
    )(*[pltpu.with_memory_space_constraint(a, pltpu.HBM) for a in arrs],
      *[pltpu.with_memory_space_constraint(a, pltpu.HBM) for a in lands], after)
    return dict(send=res[0], recv=res[1], srcs=res[2:2 + n], lands=res[2 + n:2 + 2 * n], token=res[-1], scatter=scatter,
                flips=flips)


def exchange_wait(state, after, name):
    n = len(state["srcs"])

    def body(*refs):
        ins, lz = refs[:n], refs[n:2 * n]
        send_sems, recv_sems = refs[2 * n], refs[2 * n + 1]
        for cp in _exchange_copies(ins, lz, send_sems, recv_sems, state["scatter"], state["flips"], True):
            cp.wait_send()
            cp.wait_recv()

    both = list(state["srcs"]) + list(state["lands"])
    res = pl.pallas_call(
        body, name=name, out_shape=tuple(pltpu.HBM(a.shape, a.dtype) for a in both),
        in_specs=[HBM_SPEC] * (2 * n) + [SEM_SPEC, SEM_SPEC, ANY], out_specs=tuple([HBM_SPEC] * (2 * n)),
        input_output_aliases={i: i for i in range(2 * n)},
        compiler_params=pltpu.CompilerParams(has_side_effects=EFFECT),
    )(*both, state["send"], state["recv"], after)
    return res[:n], res[n:]


OTHER_CHIPS = (2, 4, 6)
SAME_CORE = (1, 2, 4, 6)


def _forward_copies(lands, send_sems, recv_sems, arriving):
    me = _place()
    sibling = _flip(me, 1)
    origin = sibling if arriving else me
    return [pltpu.make_async_remote_copy(
        src_ref=lands[a].at[_slot(_flip(origin, k))], dst_ref=lands[a].at[_slot(_flip(origin, k))],
        send_sem=send_sems.at[a * len(OTHER_CHIPS) + i], recv_sem=recv_sems.at[a * len(OTHER_CHIPS) + i],
        device_id=sibling, device_id_type=MESH) for a in range(len(lands)) for i, k in enumerate(OTHER_CHIPS)]


def forward_start(lands, after, name):
    n = len(lands)

    def body(*refs):
        lz = refs[:n]
        send_sems, recv_sems = refs[n + 1], refs[n + 2]
        token = refs[-1]
        for cp in _forward_copies(lz, send_sems, recv_sems, False):
            cp.start()
        token[...] = jnp.zeros_like(token)

    sems = pltpu.SemaphoreType.DMA((n * len(OTHER_CHIPS),))
    res = pl.pallas_call(
        body, name=name,
        out_shape=(sems, sems, *[pltpu.HBM(a.shape, a.dtype) for a in lands], jax.ShapeDtypeStruct((8, 128), F32)),
        in_specs=[HBM_SPEC] * n + [ANY],
        out_specs=(SEM_SPEC, SEM_SPEC, *[HBM_SPEC] * n, pl.BlockSpec(memory_space=pltpu.VMEM)),
        input_output_aliases={i: 2 + i for i in range(n)},
        compiler_params=pltpu.CompilerParams(has_side_effects=EFFECT),
    )(*[pltpu.with_memory_space_constraint(a, pltpu.HBM) for a in lands], after)
    return dict(send=res[0], recv=res[1], lands=res[2:2 + n], token=res[-1])


def forward_wait(state, after, name):
    n = len(state["lands"])

    def body(*refs):
        lz = refs[:n]
        send_sems, recv_sems = refs[n], refs[n + 1]
        for cp in _forward_copies(lz, send_sems, recv_sems, False):
            cp.wait_send()
        for cp in _forward_copies(lz, send_sems, recv_sems, True):
            cp.wait_recv()

    return pl.pallas_call(
        body, name=name, out_shape=tuple(pltpu.HBM(a.shape, a.dtype) for a in state["lands"]),
        in_specs=[HBM_SPEC] * n + [SEM_SPEC, SEM_SPEC, ANY], out_specs=tuple([HBM_SPEC] * n),
        input_output_aliases={i: i for i in range(n)},
        compiler_params=pltpu.CompilerParams(has_side_effects=EFFECT),
    )(*state["lands"], state["send"], state["recv"], after)


def adamw(w, m, v, parts, name):
    r, cdim = w.shape
    n_parts = parts.shape[0]
    rb = 256 if r % 256 == 0 else (SHARD_COLS // 4 if r == SHARD_COLS else r)

    def body(w_ref, m_ref, v_ref, p_ref, g_ref, d_ref, mo_ref, vo_ref):
        g = p_ref[0].astype(F32)
        for i in range(1, n_parts):
            g = g + p_ref[i].astype(F32)
        m_new = ADAM_B1 * m_ref[...] + (1.0 - ADAM_B1) * g
        v_new = ADAM_B2 * v_ref[...] + (1.0 - ADAM_B2) * (g * g)
        m_hat = m_new / (1.0 - ADAM_B1 ** ADAM_STEP)
        v_hat = v_new / (1.0 - ADAM_B2 ** ADAM_STEP)
        g_ref[...] = g
        d_ref[...] = -ADAM_LR * (m_hat / (jnp.sqrt(v_hat) + ADAM_EPS) + ADAM_WD * w_ref[...])
        mo_ref[...] = m_new
        vo_ref[...] = v_new

    blk = pl.BlockSpec((rb, cdim), lambda i: (i, 0))
    return pl.pallas_call(
        body, name=name, grid=(r // rb,),
        in_specs=[blk, blk, blk, pl.BlockSpec((n_parts, rb, cdim), lambda i: (0, i, 0))],
        out_specs=[blk] * 4, out_shape=[jax.ShapeDtypeStruct((r, cdim), F32)] * 4,
        compiler_params=_cparams(("arbitrary",)),
    )(w, m, v, parts)


def adamw_layers(w, m, v, lands, srcs, me, lo, prev, name):
    nl_all, r, cdim = w.shape
    nl = len(lands)
    rb = 256 if r % 256 == 0 else (SHARD_COLS // 4 if r == SHARD_COLS else r)
    nblk = r // rb
    n_prev = 0 if prev is None else 4

    def body(me_ref, w_ref, m_ref, v_ref, *rest):
        land_refs, src_refs = rest[:nl], rest[nl:2 * nl]
        g_ref, d_ref, mo_ref, vo_ref = rest[2 * nl + n_prev:]
        for k in range(nl):
            @pl.when(pl.program_id(0) == k)
            def _(k=k):
                own = src_refs[k][...].astype(F32)
                g = jnp.where(me_ref[0] == 0, own, land_refs[k][0].astype(F32))
                for i in range(1, N_DEV):
                    g = g + jnp.where(me_ref[0] == i, own, land_refs[k][i].astype(F32))
                m_new = ADAM_B1 * m_ref[...] + (1.0 - ADAM_B1) * g
                v_new = ADAM_B2 * v_ref[...] + (1.0 - ADAM_B2) * (g * g)
                m_hat = m_new / (1.0 - ADAM_B1 ** ADAM_STEP)
                v_hat = v_new / (1.0 - ADAM_B2 ** ADAM_STEP)
                g_ref[...] = g
                d_ref[...] = -ADAM_LR * (m_hat / (jnp.sqrt(v_hat) + ADAM_EPS) + ADAM_WD * w_ref[...])
                mo_ref[...] = m_new
                vo_ref[...] = v_new

    blk = pl.BlockSpec((None, rb, cdim), lambda l, i, me_ref: (lo + l, i, 0))

    def rows(l, i, k):
        return jnp.where(l < k, 0, jnp.where(l == k, i, nblk - 1))

    land_specs = [pl.BlockSpec((N_DEV, rb, cdim), lambda l, i, me_ref, k=k: (0, rows(l, i, k), 0)) for k in range(nl)]
    src_specs = [pl.BlockSpec((None, rb, cdim), lambda l, i, me_ref, k=k: (me_ref[0], rows(l, i, k), 0)) for k in range(nl)]
    return pl.pallas_call(
        body, name=name,
        grid_spec=pltpu.PrefetchScalarGridSpec(
            num_scalar_prefetch=1, grid=(nl, nblk),
            in_specs=[blk, blk, blk] + land_specs + src_specs + [ANY] * n_prev, out_specs=[blk] * 4),
        out_shape=[jax.ShapeDtypeStruct((nl_all, r, cdim), F32)] * 4,
        input_output_aliases={4 + 2 * nl + j: j for j in range(n_prev)},
        compiler_params=_cparams(("arbitrary", "arbitrary")),
    )(me, w, m, v, *lands, *srcs, *([] if prev is None else prev))


def small_sum(parts, fold, name):
    rows = parts.shape[1]

    def dot3(xv, sel):
        out = jnp.zeros((xv.shape[0], sel.shape[1]), F32)
        for _ in range(3):
            hi = xv.astype(BF16)
            out = out + _dot(hi, sel)
            xv = xv - hi.astype(F32)
        return out

    def body(p_ref, fold_ref, o_ref):
        tot = p_ref[0]
        for i in range(1, N_DEV):
            tot = tot + p_ref[i]
        o_ref[...] = tot
        for l in range(rows // SM_ROWS):
            blk = tot[l * SM_ROWS:l * SM_ROWS + 8, 0:D]
            folded = dot3(blk, fold_ref[...])
            o_ref[l * SM_ROWS + SM_QG_FOLDED:l * SM_ROWS + SM_QG_FOLDED + 1, 0:128] = folded[SM_QG:SM_QG + 1, :]
            o_ref[l * SM_ROWS + SM_KG_FOLDED:l * SM_ROWS + SM_KG_FOLDED + 1, 0:128] = folded[SM_KG:SM_KG + 1, :]

    return pl.pallas_call(
        body, name=name, out_shape=jax.ShapeDtypeStruct((rows, D), F32),
        compiler_params=_cparams(None),
    )(parts, fold)


def kernel(x, norm_g, w_in, conv_w, q_norm_g, k_norm_g, sinks, w_conv_out, w_attn_out, gate_b, w_out, loss_target, m_norm_g, m_w_in, m_conv_w, m_q_norm_g, m_k_norm_g, m_sinks, m_w_conv_out, m_w_attn_out, m_gate_b, m_w_out, v_norm_g, v_w_in, v_conv_w, v_q_norm_g, v_k_norm_g, v_sinks, v_w_conv_out, v_w_attn_out, v_gate_b, v_w_out):
    c = _selectors()
    me = 4 * lax.axis_index("x") + 2 * lax.axis_index("y") + lax.axis_index("c")

    w_in_t, m_w_in_t, v_w_in_t = (jnp.swapaxes(a, 1, 2) for a in (w_in, m_w_in, v_w_in))

    def shards(l):
        return [w_in_t[l].astype(BF16), w_conv_out[l].astype(BF16), w_attn_out[l].astype(BF16), w_out[l].astype(BF16)]

    def ici_start(l, after):
        return exchange_start(shards(l), False, SAME_CORE, after, f"gather_start_{l}")

    h = x[0]
    saved, lws = [], []
    lands = all_gather(shards(0) + [conv_w], h, "gather_0")
    conv_full = jnp.transpose(lands[4], (1, 2, 0, 3)).reshape(DEPTH, 3, D)
    ici = ici_start(1, lands[1])
    for l in range(DEPTH):
        lws.append(layer_operands(l, norm_g, conv_full, q_norm_g, k_norm_g, sinks, gate_b,
                                  {l: lands[0].reshape(IN_COLS, D)}, {l: lands[1].reshape(D, D)},
                                  {l: lands[2].reshape(D, D)}, {l: lands[3].reshape(D, D)}))
        ng_l = lws[l]["ng"] + ici["token"][0:1, 0:1] if l == 0 else lws[l]["ng"]
        u, hb = inproj_fwd(h, ng_l, lws[l]["w_in"], f"inproj_fwd_{l}")
        gb_l = lws[l]["gb"]
        if l + 1 < DEPTH:
            mine, arrived = exchange_wait(ici, u, f"gather_wait_{l + 1}")
            chip = forward_start(arrived, mine[0], f"gather_forward_start_{l + 1}")
            started = chip["token"]
            if l + 2 < DEPTH:
                ici = ici_start(l + 2, started)
                started = ici["token"]
            gb_l = gb_l + started[0:1, 0:1]
        x_in = h
        h, ya, yb = mixer_fwd(x_in, u, lws[l]["cw"], lws[l]["qg"], lws[l]["kg"], lws[l]["sinks"], gb_l,
                              lws[l]["wco"], lws[l]["wao"], lws[l]["wout"], c, f"mixer_fwd_{l}")
        saved.append((x_in, u, hb, ya, yb))
        if l + 1 < DEPTH:
            lands = forward_wait(chip, h, f"gather_forward_wait_{l + 1}")
            lands = [lax.dynamic_update_index_in_dim(land, src, me, 0) for land, src in zip(lands, mine)]
    dh, loss_part = loss_head(h, loss_target[0], "loss_head")

    grads, scatters = [None] * DEPTH, [[] for _ in range(DEPTH)]
    for l in reversed(range(DEPTH)):
        def send_off(g, done, l=l):
            rest = [g[k].reshape(N_DEV, SHARD_ROWS, D) for k in ("wco", "wao", "wout")] if done else []
            first = [g["w_in"].reshape(N_DEV, SHARD_COLS, D)] if done == (l > 0) else []
            if not first + rest:
                return None
            tag = f"{l}" if l > 0 else ("0_rest" if done else "0_in")
            scatters[l].append(exchange_start(first + rest, True, ALL_PEERS, g["small"], f"scatter_start_{tag}"))
            return scatters[l][-1]["token"]
        dh, grads[l] = layer_bwd(dh, saved[l], lws[l], c, l, send_off)

    me1 = me.astype(jnp.int32).reshape(1)
    mine, lands = {}, {}
    for l in (3, 2, 1):
        mine[l], lands[l] = exchange_wait(scatters[l][0], dh, f"scatter_wait_{l}")
    weights = [(w_in_t, m_w_in_t, v_w_in_t, "w_in"), (w_conv_out, m_w_conv_out, v_w_conv_out, "w_conv_out"),
               (w_attn_out, m_w_attn_out, v_w_attn_out, "w_attn_out"), (w_out, m_w_out, v_w_out, "w_out")]
    upd = [adamw_layers(w, m, v, [lands[l][i] for l in (1, 2, 3)], [mine[l][i] for l in (1, 2, 3)], me1, 1, None,
                        f"adamw_{n}_upper") for i, (w, m, v, n) in enumerate(weights)]
    blocks = []
    for l in range(DEPTH):
        blk = grads[l]["small"]
        blk = blk.at[SM_NORM, 0:D].set(grads[l]["dng"][0])
        if l == 0:
            blk = blk.at[SM_LOSS, 0:128].set(loss_part[0])
        blocks.append(blk)
    small_x = exchange_start([jnp.concatenate(blocks, axis=0)], False, ALL_PEERS, grads[0]["dng"], "gather_small_start")

    m_in, l_in = exchange_wait(scatters[0][0], upd[0][0], "scatter_wait_0_in")
    upd[0] = adamw_layers(*weights[0][:3], [l_in[0]], [m_in[0]], me1, 0, upd[0], "adamw_w_in_0")
    m_rest, l_rest = exchange_wait(scatters[0][1], upd[0][0], "scatter_wait_0_rest")
    for i in (1, 2, 3):
        upd[i] = adamw_layers(*weights[i][:3], [l_rest[i - 1]], [m_rest[i - 1]], me1, 0, upd[i], f"adamw_{weights[i][3]}_0")
    u_in = [jnp.swapaxes(o, 1, 2) for o in upd[0]]
    u_co, u_ao, u_out = upd[1], upd[2], upd[3]

    mine_s, lands_s = exchange_wait(small_x, u_out[0], "gather_small_wait")
    gathered = lax.dynamic_update_index_in_dim(lands_s[0], mine_s[0], me, 0)
    tot = small_sum(gathered, c["fold"], "small_sum")
    tot = tot.reshape(DEPTH, SM_ROWS, D)
    loss = tot[0, SM_LOSS, 0]

    def update_small(w, m, v, g, name):
        return adamw(w, m, v, g[None], name)

    u_ng = update_small(norm_g, m_norm_g, v_norm_g, tot[:, SM_NORM, 0:D], "adamw_norm_g")
    u_qg = update_small(q_norm_g, m_q_norm_g, v_q_norm_g, tot[:, SM_QG_FOLDED, 0:HEAD], "adamw_q_norm_g")
    u_kg = update_small(k_norm_g, m_k_norm_g, v_k_norm_g, tot[:, SM_KG_FOLDED, 0:HEAD], "adamw_k_norm_g")
    u_sk = update_small(sinks, m_sinks, v_sinks, tot[:, SM_SINK, 0:16], "adamw_sinks")
    g_gate = jnp.concatenate([tot[:, SM_GATE, :], tot[:, SM_GATE_B, :]], axis=1)
    u_gb = update_small(gate_b, m_gate_b, v_gate_b, g_gate, "adamw_gate_b")
    g_conv = lax.dynamic_slice_in_dim(tot[:, SM_CONV:SM_CONV + 3, 0:D], me * SHARD_ROWS, SHARD_ROWS, axis=2)
    u_cw = [o.reshape(DEPTH, 3, SHARD_ROWS) for o in update_small(
        conv_w.reshape(DEPTH * 3, SHARD_ROWS), m_conv_w.reshape(DEPTH * 3, SHARD_ROWS),
        v_conv_w.reshape(DEPTH * 3, SHARD_ROWS), g_conv.reshape(DEPTH * 3, SHARD_ROWS), "adamw_conv_w")]

    order = [u_ng, u_in, u_cw, u_qg, u_kg, u_sk, u_co, u_ao, u_gb, u_out]
    return (loss, dh[None], *[u[0] for u in order], *[u[1] for u in order], *[u[2] for u in order], *[u[3] for u in order])
```
